```python
import math
import functools
import jax
import jax.numpy as jnp
from jax import lax
import numpy as np

D_MODEL = 1024
BATCH = 2
SEQ = 8192
DEPTH = 2

GRID_W = 64
CTX_LEN = 256
CHUNK = 64
NORM_EPS = 1e-6

GLA_HEADS = 4
GLA_DK = 64
GLA_DV = 128
GLA_KEY_W = GLA_HEADS * GLA_DK
GLA_VAL_W = GLA_HEADS * GLA_DV
GLA_LOW_RANK = 16
GLA_TAU = 16.0
HG_HEADS = 4
HG_EXPAND = 128
HG_DV = 128
HG_KEY_W = HG_HEADS * HG_EXPAND
HG_VAL_W = HG_HEADS * HG_DV
HY_CH = 512
HY_ORDER = 2
HY_SHORT = 3
HY_BANDS = 16
HY_POS_DIM = 1 + 2 * HY_BANDS
HY_FILT_HID = 64
HY_MIN_DECAY = math.log(1e-2) / 1.5
HY_MAX_DECAY = math.log(1e-2) / 0.3
MB_HEADS = 8
MB_HEAD_DIM = 64
MB_INNER = MB_HEADS * MB_HEAD_DIM
MB_GROUPS = 2
MB_STATE = 128
MB_CONV = 3
MB_CONV_CH = MB_INNER + 2 * MB_GROUPS * MB_STATE
AB_SPLITS = (GLA_KEY_W, GLA_KEY_W, GLA_VAL_W, GLA_VAL_W, GLA_LOW_RANK, GLA_LOW_RANK,
             HG_KEY_W, HG_KEY_W, HG_KEY_W, HG_VAL_W, HG_VAL_W)
AB_COLS = sum(AB_SPLITS)
AB_WIDTH = GLA_VAL_W + HG_VAL_W
CD_SPLITS = (3 * HY_CH, MB_INNER, MB_CONV_CH, MB_HEADS, MB_HEADS)
CD_COLS = sum(CD_SPLITS)
CD_WIDTH = HY_CH + MB_INNER
N_EXPERTS = 16
N_GROUPS = 4
EXPERTS_PER_GROUP = N_EXPERTS // N_GROUPS
TOP_K = 2
D_EXPERT = 1024
MOE_BLOCK = 256

kernel_name = "hybrid_gla_hgrn_hyena_ssd_moe_dit"


def rms_norm(x, g):
    xf = x.astype(jnp.float32)
    y = xf * lax.rsqrt(jnp.mean(xf * xf, axis=-1, keepdims=True) + NORM_EPS)
    return (y * g.astype(jnp.float32)).astype(x.dtype)


def modulate(h, g, shift, scale):
    return rms_norm(h, g) * (1.0 + scale) + shift


def adaln(cond, w, b):
    m = jnp.dot(jax.nn.silu(cond), w) + b
    return jnp.split(m[..., None, :], 6, axis=-1)


def split_cols(a, sizes):
    return jnp.split(a, np.cumsum(sizes)[:-1].tolist(), axis=-1)


def dir_view(a, n_ctx, reverse):
    if not reverse:
        return a
    return jnp.concatenate([jnp.flip(a[:, :n_ctx], 1), jnp.flip(a[:, n_ctx:], 1)], axis=1)


def centred_dwconv(u, w, b):
    kw = w.shape[-1]
    r = kw // 2
    n = u.shape[1]
    up = jnp.pad(u, ((0, 0), (r, r), (0, 0)))
    return sum(up[:, j:j + n] * w[:, j] for j in range(kw)) + b


def to_chunks(a):
    b, t = a.shape[:2]
    return jnp.moveaxis(a.reshape(b, t // CHUNK, CHUNK, *a.shape[2:]), 1, 0)


def from_chunks(a):
    n, b, c = a.shape[:3]
    return jnp.moveaxis(a, 0, 1).reshape(b, n * c, *a.shape[3:])


def gla_scan(q, k, v, log_a):
    b, _, h, dk = q.shape
    dv = v.shape[-1]
    mask = jnp.tril(jnp.ones((CHUNK, CHUNK), bool))[None, :, :, None, None]

    def step(s, blk):
        qc, kc, vc, ac = blk
        cum = jnp.cumsum(ac, axis=1)
        o = jnp.einsum('bthk,bhkv->bthv', qc * jnp.exp(cum), s)
        decay = jnp.exp(jnp.where(mask, cum[:, :, None] - cum[:, None], -jnp.inf))
        att = jnp.einsum('bthk,bshk,btshk->bths', qc, kc, decay)
        o = o + jnp.einsum('bths,bshv->bthv', att, vc)
        last = cum[:, -1]
        s = jnp.exp(last)[..., None] * s + jnp.einsum(
            'bshk,bshv->bhkv', kc * jnp.exp(last[:, None] - cum), vc)
        return s, o

    s0 = jnp.zeros((b, h, dk, dv), jnp.float32)
    _, o = lax.scan(step, s0, (to_chunks(q), to_chunks(k), to_chunks(v), to_chunks(log_a)))
    return from_chunks(o)


def ssd_scan(x, la, bm, cm):
    b, t, h, p = x.shape
    g, n = bm.shape[2:]
    r = h // g
    x = x.reshape(b, t, g, r, p)
    la = la.reshape(b, t, g, r)
    mask = jnp.tril(jnp.ones((CHUNK, CHUNK), bool))[None, :, :, None, None]

    def step(s, blk):
        xc, ac, bc, cc = blk
        cum = jnp.cumsum(ac, axis=1)
        decay = jnp.exp(jnp.where(mask, cum[:, :, None] - cum[:, None], -jnp.inf))
        y = jnp.einsum('btgn,bsgn,btsgr,bsgrp->btgrp', cc, bc, decay, xc)
        y = y + jnp.einsum('btgn,bgrnp->btgrp', cc, s) * jnp.exp(cum)[..., None]
        last = cum[:, -1]
        s = jnp.exp(last)[..., None, None] * s + jnp.einsum(
            'bsgn,bsgrp->bgrnp', bc, xc * jnp.exp(last[:, None] - cum)[..., None])
        return s, y

    s0 = jnp.zeros((b, g, r, n, p), jnp.float32)
    _, y = lax.scan(step, s0, (to_chunks(x), to_chunks(la), to_chunks(bm), to_chunks(cm)))
    return from_chunks(y).reshape(b, t, h, p)


def hyena_filters(n, w1, b1, w2, b2, w3, freq):
    t = jnp.linspace(0.0, 1.0, n, dtype=jnp.float32)[:, None]
    bands = jnp.linspace(1e-4, HY_BANDS - 1, HY_BANDS, dtype=jnp.float32)
    ang = (2.0 * math.pi / n) * jnp.arange(n, dtype=jnp.float32)[:, None] * bands
    z = jnp.concatenate([t, jnp.cos(ang), -jnp.sin(ang)], axis=-1)
    hid = jnp.sin(freq * (z @ w1 + b1))
    hid = jnp.sin(freq * (hid @ w2 + b2))
    filt = (hid @ w3).reshape(n, HY_ORDER, 2, HY_CH)
    rates = jnp.abs(jnp.linspace(HY_MIN_DECAY, HY_MAX_DECAY, HY_CH, dtype=jnp.float32))
    return filt * jnp.exp(-t * rates)[:, None, None, :]


def bidir_long_conv(u, h_fwd, h_bwd, bias):
    n = u.shape[1]
    k = jnp.concatenate([h_fwd[:1] + h_bwd[:1], h_fwd[1:], jnp.zeros_like(h_fwd[:1]),
                         jnp.flip(h_bwd[1:], 0)], axis=0)
    uf = jnp.fft.rfft(u.astype(jnp.float32), n=2 * n, axis=1)
    kf = jnp.fft.rfft(k.astype(jnp.float32), axis=0)
    y = jnp.fft.irfft(uf * kf[None], n=2 * n, axis=1)[:, :n]
    return y + u.astype(jnp.float32) * bias


def mixer_gla_hgrn(u, n_ctx, w_in, gate_w, gate_b, gla_norm_g, lb, hg_norm_g):
    bsz, t, _ = u.shape
    heads = lambda a, nh: a.reshape(bsz, t, nh, -1)
    gq, gk, gv, gg, glr_f, glr_b, hq, hf_f, hf_b, hi, hg = split_cols(u @ w_in, AB_SPLITS)
    gq = heads(gq, GLA_HEADS) * GLA_DK ** -0.5
    gk, gv = heads(gk, GLA_HEADS), heads(gv, GLA_HEADS)
    hq, hi = heads(jax.nn.silu(hq), HG_HEADS), heads(hi, HG_HEADS)
    o_gla, o_hg = 0.0, 0.0
    for d, (lr, fz) in enumerate(((glr_f, hf_f), (glr_b, hf_b))):
        view = functools.partial(dir_view, n_ctx=n_ctx, reverse=(d == 1))
        log_a = jax.nn.log_sigmoid((lr @ gate_w[d] + gate_b[d]).astype(jnp.float32)) / GLA_TAU
        f = lb[d] + (1.0 - lb[d]) * jax.nn.sigmoid(fz.astype(jnp.float32))
        o_gla = o_gla + view(gla_scan(view(gq), view(gk), view(gv), view(heads(log_a, GLA_HEADS))))
        o_hg = o_hg + view(gla_scan(view(hq), view(heads(1.0 - f, HG_HEADS)), view(hi),
                                    view(heads(jnp.log(f), HG_HEADS))))
    feat_gla = rms_norm(o_gla, gla_norm_g) * jax.nn.silu(heads(gg, GLA_HEADS))
    feat_hg = rms_norm(o_hg, hg_norm_g) * jax.nn.silu(heads(hg, HG_HEADS))
    feat = jnp.concatenate([feat_gla.reshape(bsz, t, -1), feat_hg.reshape(bsz, t, -1)],
                           axis=-1).astype(u.dtype)
    return feat[:, :n_ctx], feat[:, n_ctx:]


def mixer_hyena_ssd(u, n_ctx, need_ctx, w_in, short_w, short_b, fw1, fb1, fw2, fb2, fw3, freq,
                    conv_bias, conv_w, conv_b, dt_bias, a_log, d_skip, norm_g):
    bsz, t, _ = u.shape
    hy_in, z, xbc, dt_f, dt_b = split_cols(u @ w_in, CD_SPLITS)

    def hyena(seg):
        n = seg.shape[1]
        v, x1, x2 = jnp.split(centred_dwconv(seg, short_w, short_b), 3, axis=-1)
        filt = hyena_filters(n, fw1, fb1, fw2, fb2, fw3, freq)
        zz = x1 * bidir_long_conv(v, filt[:, 0, 0], filt[:, 0, 1], conv_bias[0])
        return x2 * bidir_long_conv(zz, filt[:, 1, 0], filt[:, 1, 1], conv_bias[1])

    hy_lat = hyena(hy_in[:, n_ctx:])
    xbc = jax.nn.silu(jnp.concatenate([centred_dwconv(xbc[:, :n_ctx], conv_w, conv_b),
                                       centred_dwconv(xbc[:, n_ctx:], conv_w, conv_b)], axis=1))
    xs, bm, cm = split_cols(xbc, (MB_INNER, MB_GROUPS * MB_STATE, MB_GROUPS * MB_STATE))
    xs = xs.reshape(bsz, t, MB_HEADS, MB_HEAD_DIM)
    bm = bm.reshape(bsz, t, MB_GROUPS, MB_STATE)
    cm = cm.reshape(bsz, t, MB_GROUPS, MB_STATE)
    y = d_skip[:, None] * xs
    for d, dt_raw in enumerate((dt_f, dt_b)):
        view = functools.partial(dir_view, n_ctx=n_ctx, reverse=(d == 1))
        dt = jax.nn.softplus(dt_raw.astype(jnp.float32) + dt_bias[d])
        la = -dt * jnp.exp(a_log[d].astype(jnp.float32))
        y = y + view(ssd_scan(view(xs * dt[..., None]), view(la), view(bm), view(cm)))
    y = (y.reshape(bsz, t, MB_INNER) * jax.nn.silu(z)).reshape(bsz, t, MB_GROUPS, -1)
    y = rms_norm(y, norm_g.reshape(MB_GROUPS, -1)).reshape(bsz, t, MB_INNER)
    f_lat = jnp.concatenate([hy_lat, y[:, n_ctx:]], axis=-1).astype(u.dtype)
    if not need_ctx:
        return None, f_lat
    f_ctx = jnp.concatenate([hyena(hy_in[:, :n_ctx]), y[:, :n_ctx]], axis=-1).astype(u.dtype)
    return f_ctx, f_lat


def moe_ffn(h, router_w, router_b, w_gate, w_up, w_down):
    n, d = h.shape
    scores = jax.nn.sigmoid(jnp.dot(h.astype(jnp.float32), router_w.astype(jnp.float32)))
    sel = scores + router_b
    group_score = lax.top_k(sel.reshape(n, N_GROUPS, EXPERTS_PER_GROUP), TOP_K)[0].sum(-1)
    best = jnp.argmax(group_score, axis=-1)
    in_group = (jnp.arange(N_EXPERTS) // EXPERTS_PER_GROUP)[None] == best[:, None]
    _, idx = lax.top_k(jnp.where(in_group, sel, -jnp.inf), TOP_K)
    w = jnp.take_along_axis(scores, idx, axis=1)
    w = w / jnp.sum(w, axis=-1, keepdims=True)
    e_flat = idx.reshape(-1)
    tok = jnp.repeat(jnp.arange(n, dtype=jnp.int32), TOP_K)
    order = jnp.argsort(e_flat)
    e_s, tok_s, w_s = e_flat[order], tok[order], w.reshape(-1)[order]
    counts = jnp.zeros(N_EXPERTS, jnp.int32).at[e_flat].add(1)
    start = jnp.cumsum(counts) - counts
    padded = (counts + MOE_BLOCK - 1) // MOE_BLOCK * MOE_BLOCK
    pend = jnp.cumsum(padded)
    dest = (pend - padded)[e_s] + jnp.arange(n * TOP_K, dtype=jnp.int32) - start[e_s]
    n_slots = (n * TOP_K + MOE_BLOCK - 1) // MOE_BLOCK * MOE_BLOCK + N_EXPERTS * MOE_BLOCK
    n_blocks = n_slots // MOE_BLOCK
    slot_tok = jnp.full((n_slots,), n, jnp.int32).at[dest].set(tok_s)
    slot_w = jnp.zeros((n_slots,), jnp.float32).at[dest].set(w_s)
    block_e = jnp.clip(jnp.searchsorted(pend, jnp.arange(n_blocks) * MOE_BLOCK, side='right'),
                       0, N_EXPERTS - 1)
    hp = jnp.concatenate([h, jnp.zeros((1, d), h.dtype)], axis=0)
    xb = hp[slot_tok].reshape(n_blocks, MOE_BLOCK, d)

    def expert_block(args):
        xblk, e = args
        return (jax.nn.silu(xblk @ w_gate[e]) * (xblk @ w_up[e])) @ w_down[e]

    yb = lax.map(expert_block, (xb, block_e)).reshape(n_slots, d)
    out = jax.ops.segment_sum(yb * slot_w[:, None], slot_tok, num_segments=n + 1)[:n]
    return out.astype(h.dtype)


def setup_inputs(seed: int = 0) -> dict:
    key = jax.random.key(seed)
    ks = iter(jax.random.split(key, 48))
    ne, no = (DEPTH + 1) // 2, DEPTH // 2

    def nrm(shape, scale):
        return scale * jax.random.normal(next(ks), shape, jnp.float32)

    dt0 = jnp.exp(jax.random.uniform(next(ks), (no, 2, MB_HEADS), jnp.float32,
                                     math.log(1e-3), math.log(1e-1)))
    mb_dt_bias = dt0 + jnp.log(-jnp.expm1(-dt0))
    mb_a_log = jnp.log(jax.random.uniform(next(ks), (no, 2, MB_HEADS), jnp.float32, 1.0, 16.0))
    return {
        "x": nrm((BATCH, SEQ, D_MODEL), 1.0),
        "c": nrm((BATCH, D_MODEL), 1.0),
        "ctx": nrm((BATCH, CTX_LEN, D_MODEL), 1.0),
        "c_ctx": nrm((D_MODEL,), 1.0),
        "ada_w": nrm((DEPTH, D_MODEL, 6 * D_MODEL), 0.5 * D_MODEL ** -0.5),
        "ada_b": nrm((DEPTH, 6 * D_MODEL), 0.02),
        "norm_mix_g": 1.0 + nrm((DEPTH, D_MODEL), 0.02),
        "norm_ffn_g": 1.0 + nrm((DEPTH, D_MODEL), 0.02),
        "norm_out_g": 1.0 + nrm((D_MODEL,), 0.02),
        "ab_w_in": nrm((ne, D_MODEL, AB_COLS), D_MODEL ** -0.5),
        "ab_w_out": nrm((ne, AB_WIDTH, D_MODEL), AB_WIDTH ** -0.5),
        "gla_gate_w": nrm((ne, 2, GLA_LOW_RANK, GLA_KEY_W), GLA_LOW_RANK ** -0.5),
        "gla_gate_b": nrm((ne, 2, GLA_KEY_W), 0.1),
        "gla_norm_g": 1.0 + nrm((ne, GLA_DV), 0.02),
        "hg_lb": nrm((2, ne + 1, HG_KEY_W), 0.1),
        "hg_norm_g": 1.0 + nrm((ne, HG_DV), 0.02),
        "cd_w_in": nrm((no, D_MODEL, CD_COLS), D_MODEL ** -0.5),
        "cd_w_out": nrm((no, CD_WIDTH, D_MODEL), CD_WIDTH ** -0.5),
        "hy_short_w": nrm((no, 3 * HY_CH, HY_SHORT), HY_SHORT ** -0.5),
        "hy_short_b": nrm((no, 3 * HY_CH), 0.02),
        "hy_w1": nrm((no, HY_POS_DIM, HY_FILT_HID), HY_POS_DIM ** -0.5),
        "hy_b1": nrm((no, HY_FILT_HID), 0.1),
        "hy_w2": nrm((no, HY_FILT_HID, HY_FILT_HID), HY_FILT_HID ** -0.5),
        "hy_b2": nrm((no, HY_FILT_HID), 0.1),
        "hy_w3": nrm((no, HY_FILT_HID, HY_ORDER * 2 * HY_CH), 0.03 * HY_FILT_HID ** -0.5),
        "hy_freq": 1.0 + nrm((no, HY_FILT_HID), 0.02),
        "hy_bias": nrm((no, HY_ORDER, HY_CH), 0.5),
        "mb_conv_w": nrm((no, MB_CONV_CH, MB_CONV), MB_CONV ** -0.5),
        "mb_conv_b": nrm((no, MB_CONV_CH), 0.02),
        "mb_dt_bias": mb_dt_bias,
        "mb_a_log": mb_a_log,
        "mb_d": 1.0 + nrm((no, MB_HEADS), 0.02),
        "mb_norm_g": 1.0 + nrm((no, MB_INNER), 0.02),
        "router_w": nrm((D_MODEL, N_EXPERTS), D_MODEL ** -0.5),
        "router_b": nrm((N_EXPERTS,), 0.01),
        "moe_w_gate": nrm((DEPTH, N_EXPERTS, D_MODEL, D_EXPERT), D_MODEL ** -0.5),
        "moe_w_up": nrm((DEPTH, N_EXPERTS, D_MODEL, D_EXPERT), D_MODEL ** -0.5),
        "moe_w_down": nrm((DEPTH, N_EXPERTS, D_EXPERT, D_MODEL), D_EXPERT ** -0.5),
    }


def reference(x, c, ctx, c_ctx, ada_w, ada_b, norm_mix_g, norm_ffn_g, norm_out_g,
              ab_w_in, ab_w_out, gla_gate_w, gla_gate_b, gla_norm_g, hg_lb, hg_norm_g,
              cd_w_in, cd_w_out, hy_short_w, hy_short_b, hy_w1, hy_b1, hy_w2, hy_b2, hy_w3,
              hy_freq, hy_bias, mb_conv_w, mb_conv_b, mb_dt_bias, mb_a_log, mb_d, mb_norm_g,
              router_w, router_b, moe_w_gate, moe_w_up, moe_w_down):
    bsz, n_lat, d = x.shape
    n_ctx = ctx.shape[1]
    lb_all = jnp.cumsum(jax.nn.softmax(hg_lb.astype(jnp.float32), axis=1), axis=1)
    h_ctx, h_lat = ctx, x
    for layer in range(DEPTH):
        need_ctx = layer < DEPTH - 1
        sh_m, sc_m, gt_m, sh_f, sc_f, gt_f = adaln(c, ada_w[layer], ada_b[layer])
        csh_m, csc_m, cgt_m, csh_f, csc_f, cgt_f = adaln(c_ctx, ada_w[layer], ada_b[layer])
        u = jnp.concatenate([modulate(h_ctx, norm_mix_g[layer], csh_m, csc_m),
                             modulate(h_lat, norm_mix_g[layer], sh_m, sc_m)], axis=1)
        j = layer // 2
        if layer % 2 == 0:
            f_ctx, f_lat = mixer_gla_hgrn(u, n_ctx, ab_w_in[j], gla_gate_w[j], gla_gate_b[j],
                                          gla_norm_g[j], lb_all[:, j], hg_norm_g[j])
            w_out = ab_w_out[j]
        else:
            f_ctx, f_lat = mixer_hyena_ssd(u, n_ctx, need_ctx, cd_w_in[j], hy_short_w[j], hy_short_b[j],
                                           hy_w1[j], hy_b1[j], hy_w2[j], hy_b2[j], hy_w3[j], hy_freq[j],
                                           hy_bias[j], mb_conv_w[j], mb_conv_b[j], mb_dt_bias[j],
                                           mb_a_log[j], mb_d[j], mb_norm_g[j])
            w_out = cd_w_out[j]
        h_lat = h_lat + gt_m * (f_lat @ w_out)
        if need_ctx:
            h_ctx = h_ctx + cgt_m * (f_ctx @ w_out)
            v = jnp.concatenate([modulate(h_ctx, norm_ffn_g[layer], csh_f, csc_f),
                                 modulate(h_lat, norm_ffn_g[layer], sh_f, sc_f)], axis=1)
            y = moe_ffn(v.reshape(-1, d), router_w, router_b, moe_w_gate[layer], moe_w_up[layer],
                        moe_w_down[layer]).reshape(bsz, n_ctx + n_lat, d)
            h_ctx = h_ctx + cgt_f * y[:, :n_ctx]
            h_lat = h_lat + gt_f * y[:, n_ctx:]
        else:
            v = modulate(h_lat, norm_ffn_g[layer], sh_f, sc_f)
            y = moe_ffn(v.reshape(-1, d), router_w, router_b, moe_w_gate[layer], moe_w_up[layer],
                        moe_w_down[layer]).reshape(bsz, n_lat, d)
            h_lat = h_lat + gt_f * y
    return rms_norm(h_lat, norm_out_g)
```

```python
import functools
import math

import numpy as np
import jax
import jax.numpy as jnp
from jax import lax
from jax.experimental import pallas as pl
from jax.experimental.pallas import tpu as pltpu

NORM_EPS = 1e-6
GLA_HEADS, GLA_DK, GLA_DV, GLA_LOW_RANK, GLA_TAU = 4, 64, 128, 16, 16.0
GLA_KEY_W, GLA_VAL_W = GLA_HEADS * GLA_DK, GLA_HEADS * GLA_DV
HG_HEADS, HG_EXPAND, HG_DV = 4, 128, 128
HG_KEY_W, HG_VAL_W = HG_HEADS * HG_EXPAND, HG_HEADS * HG_DV
HY_CH, HY_ORDER, HY_SHORT, HY_BANDS, HY_FILT_HID = 512, 2, 3, 16, 64
HY_MIN_DECAY = math.log(1e-2) / 1.5
HY_MAX_DECAY = math.log(1e-2) / 0.3
MB_HEADS, MB_HEAD_DIM, MB_GROUPS, MB_STATE = 8, 64, 2, 128
MB_INNER = MB_HEADS * MB_HEAD_DIM
MB_BC_W = MB_GROUPS * MB_STATE
N_EXPERTS, N_GROUPS, TOP_K, MOE_BLOCK = 16, 4, 2, 256
EXPERTS_PER_GROUP = N_EXPERTS // N_GROUPS

LANES = 128
SCAN_CHUNK = 128
ROW_TILE = 256
VMEM_LIMIT = 56 * 1024 * 1024

BF16 = jnp.bfloat16
F32 = jnp.float32


def _params(*sem):
    return pltpu.CompilerParams(dimension_semantics=sem, vmem_limit_bytes=VMEM_LIMIT)


def _split3(x):
    hi = x.astype(BF16)
    r1 = x - hi.astype(F32)
    mid = r1.astype(BF16)
    lo = (r1 - mid.astype(F32)).astype(BF16)
    return hi, mid, lo


def _dot(a, b):
    return jnp.dot(a, b, preferred_element_type=F32)


def _dot_nt(a, b):
    return lax.dot_general(a, b, (((1,), (1,)), ((), ())), preferred_element_type=F32)


def _dot_tn(a, b):
    return lax.dot_general(a, b, (((0,), (0,)), ((), ())), preferred_element_type=F32)


def _sel_dot(m01, x):
    hi, mid, lo = _split3(x)
    return _dot(m01, hi) + (_dot(m01, mid) + _dot(m01, lo))


def _dot_sel(x, m01):
    hi, mid, lo = _split3(x)
    return _dot(hi, m01) + (_dot(mid, m01) + _dot(lo, m01))


def _dot_f32(a, b):
    ah = a.astype(BF16)
    al = (a - ah.astype(F32)).astype(BF16)
    bh = b.astype(BF16)
    bl = (b - bh.astype(F32)).astype(BF16)
    return _dot(ah, bh) + (_dot(ah, bl) + _dot(al, bh))


def _silu(x):
    return x * (1.0 / (1.0 + jnp.exp(-x)))


def _sigmoid(x):
    return 1.0 / (1.0 + jnp.exp(-x))


def _softplus(x):
    return jnp.maximum(x, 0.0) + jnp.log(1.0 + jnp.exp(-jnp.abs(x)))


def _rms(x, g):
    return x * lax.rsqrt(jnp.mean(x * x, axis=-1, keepdims=True) + NORM_EPS) * g


def _adaln_kernel(c_ref, w_ref, b_ref, o_ref):
    o_ref[...] = _dot_f32(_silu(c_ref[...]), w_ref[...]) + b_ref[...]


def _adaln(cond, w, b):
    n_l, d, n6 = w.shape
    tn = 1536
    return pl.pallas_call(
        _adaln_kernel,
        grid=(n_l, n6 // tn),
        in_specs=[pl.BlockSpec((8, d), lambda l, j: (0, 0)),
                  pl.BlockSpec((None, d, tn), lambda l, j: (l, 0, j)),
                  pl.BlockSpec((None, 1, tn), lambda l, j: (l, 0, j))],
        out_specs=pl.BlockSpec((None, 8, tn), lambda l, j: (l, 0, j)),
        out_shape=jax.ShapeDtypeStruct((n_l, 8, n6), F32),
        compiler_params=_params("parallel", "parallel"),
        name="adaln",
    )(cond, w, b.reshape(n_l, 1, n6))


def _norm_proj_kernel(h_ref, g_ref, sh_ref, sc_ref, w_ref, o_ref):
    u = _rms(h_ref[...], g_ref[...]) * (1.0 + sc_ref[...]) + sh_ref[...]
    o_ref[...] = _dot(u.astype(BF16), w_ref[...])


def _norm_proj(h, g, shift, scale, w, n_ctx):
    bsz, t, d = h.shape
    n = w.shape[1]
    tm = ROW_TILE
    seg = lambda b, i: (b, (i * tm >= n_ctx).astype(jnp.int32), 0, 0)
    return pl.pallas_call(
        _norm_proj_kernel,
        grid=(bsz, t // tm),
        in_specs=[pl.BlockSpec((None, tm, d), lambda b, i: (b, i, 0)),
                  pl.BlockSpec((1, d), lambda b, i: (0, 0)),
                  pl.BlockSpec((None, None, 1, d), seg),
                  pl.BlockSpec((None, None, 1, d), seg),
                  pl.BlockSpec((d, n), lambda b, i: (0, 0))],
        out_specs=pl.BlockSpec((None, tm, n), lambda b, i: (b, i, 0)),
        out_shape=jax.ShapeDtypeStruct((bsz, t, n), F32),
        compiler_params=_params("parallel", "parallel"),
        name="norm_proj",
    )(h, g.reshape(1, d), shift, scale, w)


def _scan_constants(c, reverse):
    t = np.arange(c)[:, None]
    u = np.arange(c)[None, :]
    sels = [u <= t, u > t]
    masks = []
    m = c // 2
    while m >= 1:
        blk = t // (2 * m)
        upper_t = (t % (2 * m)) >= m
        r = blk * (2 * m) + m - 1
        s_blk = u // (2 * m)
        upper_s = (u % (2 * m)) >= m
        sels += [upper_t & (u > r) & (u <= t), (~upper_t) & (u > t) & (u <= r)]
        masks.append((blk == s_blk) & upper_t & (~upper_s))
        m //= 2
    masks.append(t == u)
    sel = np.stack(sels).astype(np.float32)
    msk = np.stack(masks).astype(np.float32)
    if reverse:
        sel = sel[:, ::-1, ::-1]
        msk = msk[:, ::-1, ::-1]
    return np.ascontiguousarray(sel.reshape(-1, c)), np.ascontiguousarray(msk)


def _chunk_order(i, n_ctx_chunks, n_chunks, reverse):
    if not reverse:
        return i
    return jnp.where(i < n_ctx_chunks, n_ctx_chunks - 1 - i, n_chunks - 1 - (i - n_ctx_chunks))


def _decay_chunk(q, k, v, la, sel_ref, mask_ref, st_ref, heads, dk, dv):
    c = q.shape[0]
    n_lvl = mask_ref.shape[0] - 1
    cs = _sel_dot(sel_ref[...], la)
    e_q = jnp.exp(cs[0:c])
    e_k = jnp.exp(cs[c:2 * c])
    e_tot = jnp.exp(jnp.sum(la, axis=0, keepdims=True))
    outs = []
    for h in range(heads):
        ks = slice(h * dk, (h + 1) * dk)
        qh, kh, vh = q[:, ks], k[:, ks], v[:, h * dv:(h + 1) * dv].astype(BF16)
        att = mask_ref[n_lvl] * _dot_nt(qh.astype(BF16), kh.astype(BF16))
        for l in range(n_lvl):
            eq = jnp.exp(cs[(2 + 2 * l) * c:(3 + 2 * l) * c, ks])
            ek = jnp.exp(cs[(3 + 2 * l) * c:(4 + 2 * l) * c, ks])
            att = att + mask_ref[l] * _dot_nt((qh * eq).astype(BF16), (kh * ek).astype(BF16))
        st = st_ref[h]
        o = _dot(att.astype(BF16), vh)
        o = o + _dot_nt((qh * e_q[:, ks]).astype(BF16), st.astype(BF16))
        outs.append(o)
        kd = (kh * e_k[:, ks]).astype(BF16)
        st_ref[h] = st * e_tot[:, ks] + _dot_tn(vh, kd)
    return jnp.concatenate(outs, axis=-1)


def _log_sigmoid(x):
    return jnp.minimum(x, 0.0) - jnp.log(1.0 + jnp.exp(-jnp.abs(x)))


def _gla_kernel(q_ref, k_ref, v_ref, lr_ref, gw_ref, gb_ref, sel_ref, mask_ref, o_ref, st_ref):
    @pl.when(pl.program_id(1) == 0)
    def _():
        st_ref[...] = jnp.zeros_like(st_ref)

    z = _dot_f32(lr_ref[...], gw_ref[...]) + gb_ref[...]
    la = _log_sigmoid(z) * (1.0 / GLA_TAU)
    q = q_ref[...] * (GLA_DK ** -0.5)
    o_ref[...] = _decay_chunk(q, k_ref[...], v_ref[...], la, sel_ref, mask_ref, st_ref,
                              GLA_HEADS, GLA_DK, GLA_DV)


def _hgrn_kernel(q_ref, f_ref, v_ref, lb_ref, sel_ref, mask_ref, o_ref, st_ref):
    @pl.when(pl.program_id(1) == 0)
    def _():
        st_ref[...] = jnp.zeros_like(st_ref)

    lb = lb_ref[...]
    f = lb + (1.0 - lb) * _sigmoid(f_ref[...])
    o_ref[...] = _decay_chunk(_silu(q_ref[...]), 1.0 - f, v_ref[...], jnp.log(f), sel_ref, mask_ref,
                              st_ref, HG_HEADS, HG_EXPAND, HG_DV)


def _scan_specs(c, n_ctx, t, reverse):
    n_chunks = t // c
    n_ctx_chunks = n_ctx // c
    order = functools.partial(_chunk_order, n_ctx_chunks=n_ctx_chunks, n_chunks=n_chunks, reverse=reverse)

    def col(width, idx):
        return pl.BlockSpec((None, c, width), lambda b, i: (b, order(i), idx))

    sel, msk = _scan_constants(c, reverse)
    const = lambda a: pl.BlockSpec(a.shape, lambda b, i: (0,) * a.ndim)
    return n_chunks, col, const, jnp.asarray(sel, BF16), jnp.asarray(msk, F32)


AB_Q, AB_K, AB_V, AB_G = 0, 256, 512, 1024
AB_HQ, AB_HF, AB_HI, AB_HG, AB_LR = 1536, 2048, 3072, 3584, 4096
AB_PAD_COLS = 4224


def _gla_scan(proj, gate_w_pad, gate_b, n_ctx, reverse):
    bsz, t, _ = proj.shape
    c = SCAN_CHUNK
    n_chunks, col, const, sel, msk = _scan_specs(c, n_ctx, t, reverse)
    return pl.pallas_call(
        _gla_kernel,
        grid=(bsz, n_chunks),
        in_specs=[col(GLA_KEY_W, AB_Q // GLA_KEY_W), col(GLA_KEY_W, AB_K // GLA_KEY_W),
                  col(GLA_VAL_W, AB_V // GLA_VAL_W), col(LANES, AB_LR // LANES),
                  const(gate_w_pad), const(gate_b), const(sel), const(msk)],
        out_specs=col(GLA_VAL_W, 0),
        out_shape=jax.ShapeDtypeStruct((bsz, t, GLA_VAL_W), F32),
        scratch_shapes=[pltpu.VMEM((GLA_HEADS, GLA_DV, GLA_DK), F32)],
        compiler_params=_params("parallel", "arbitrary"),
        name="gla_scan_rev" if reverse else "gla_scan_fwd",
    )(proj, proj, proj, proj, gate_w_pad, gate_b, sel, msk)


def _hgrn_scan(proj, lb, n_ctx, reverse):
    bsz, t, _ = proj.shape
    c = SCAN_CHUNK
    n_chunks, col, const, sel, msk = _scan_specs(c, n_ctx, t, reverse)
    f_col = AB_HF // HG_KEY_W + (1 if reverse else 0)
    return pl.pallas_call(
        _hgrn_kernel,
        grid=(bsz, n_chunks),
        in_specs=[col(HG_KEY_W, AB_HQ // HG_KEY_W), col(HG_KEY_W, f_col), col(HG_VAL_W, AB_HI // HG_VAL_W),
                  const(lb), const(sel), const(msk)],
        out_specs=col(HG_VAL_W, 0),
        out_shape=jax.ShapeDtypeStruct((bsz, t, HG_VAL_W), F32),
        scratch_shapes=[pltpu.VMEM((HG_HEADS, HG_DV, HG_EXPAND), F32)],
        compiler_params=_params("parallel", "arbitrary"),
        name="hgrn_scan_rev" if reverse else "hgrn_scan_fwd",
    )(proj, proj, proj, lb, sel, msk)


def _mix_out_ab_kernel(h_ref, gf_ref, gb_ref, hf_ref, hb_ref, gg_ref, hg_ref, gn_ref, hn_ref,
                       w_ref, gt_ref, o_ref):
    feats = []
    for o, gate, g in ((gf_ref[...] + gb_ref[...], gg_ref[...], gn_ref[...]),
                       (hf_ref[...] + hb_ref[...], hg_ref[...], hn_ref[...])):
        for hd in range(o.shape[-1] // LANES):
            s = slice(hd * LANES, (hd + 1) * LANES)
            feats.append(_rms(o[:, s], g) * _silu(gate[:, s]))
    feat = jnp.concatenate(feats, axis=-1).astype(BF16)
    o_ref[...] = h_ref[...] + gt_ref[...] * _dot(feat, w_ref[...])


def _mix_out_ab(h, o_gla, o_hg, proj, gla_norm_g, hg_norm_g, w_out, gate, n_ctx):
    bsz, t, d = h.shape
    tm = ROW_TILE
    seg = lambda b, i: (b, (i * tm >= n_ctx).astype(jnp.int32), 0, 0)
    row = lambda width, idx: pl.BlockSpec((None, tm, width), lambda b, i: (b, i, idx))
    vec = pl.BlockSpec((1, LANES), lambda b, i: (0, 0))
    return pl.pallas_call(
        _mix_out_ab_kernel,
        grid=(bsz, t // tm),
        in_specs=[row(d, 0), row(GLA_VAL_W, 0), row(GLA_VAL_W, 0), row(HG_VAL_W, 0), row(HG_VAL_W, 0),
                  row(GLA_VAL_W, AB_G // GLA_VAL_W), row(HG_VAL_W, AB_HG // HG_VAL_W), vec, vec,
                  pl.BlockSpec(w_out.shape, lambda b, i: (0, 0)),
                  pl.BlockSpec((None, None, 1, d), seg)],
        out_specs=row(d, 0),
        out_shape=jax.ShapeDtypeStruct((bsz, t, d), F32),
        compiler_params=_params("parallel", "parallel"),
        name="mix_out_ab",
    )(h, o_gla[0], o_gla[1], o_hg[0], o_hg[1], proj, proj, gla_norm_g.reshape(1, -1),
      hg_norm_g.reshape(1, -1), w_out, gate)


def _dwconv_kernel(x_ref, prev_ref, next_ref, w_ref, b_ref, o_ref, *, seg_tiles, act):
    i = pl.program_id(1)
    tm = x_ref.shape[0]
    x = x_ref[...]
    first = functools.reduce(jnp.logical_or, [i == s for s in seg_tiles[:-1]])
    last = functools.reduce(jnp.logical_or, [i == s - 1 for s in seg_tiles[1:]])
    prev_row = jnp.where(first, 0.0, prev_ref[7:8, :])
    next_row = jnp.where(last, 0.0, next_ref[0:1, :])
    rows = lax.broadcasted_iota(jnp.int32, x.shape, 0)
    x_prev = jnp.where(rows == 0, prev_row, pltpu.roll(x, 1, axis=0))
    x_next = jnp.where(rows == tm - 1, next_row, pltpu.roll(x, tm - 1, axis=0))
    y = x_prev * w_ref[0:1, :] + x * w_ref[1:2, :] + x_next * w_ref[2:3, :] + b_ref[...]
    o_ref[...] = _silu(y) if act else y


def _dwconv(proj, col0, width, w, b, seg_bounds, act):
    bsz, t, _ = proj.shape
    tm = ROW_TILE
    n_tiles = t // tm
    seg_tiles = tuple(s // tm for s in seg_bounds)
    cb = col0 // width
    r8 = tm // 8
    kern = functools.partial(_dwconv_kernel, seg_tiles=seg_tiles, act=act)
    return pl.pallas_call(
        kern,
        grid=(bsz, n_tiles),
        in_specs=[pl.BlockSpec((None, tm, width), lambda bb, i: (bb, i, cb)),
                  pl.BlockSpec((None, 8, width), lambda bb, i: (bb, jnp.maximum(i * r8 - 1, 0), cb)),
                  pl.BlockSpec((None, 8, width),
                               lambda bb, i: (bb, jnp.minimum((i + 1) * r8, n_tiles * r8 - 1), cb)),
                  pl.BlockSpec((3, width), lambda bb, i: (0, 0)),
                  pl.BlockSpec((1, width), lambda bb, i: (0, 0))],
        out_specs=pl.BlockSpec((None, tm, width), lambda bb, i: (bb, i, 0)),
        out_shape=jax.ShapeDtypeStruct((bsz, t, width), F32),
        compiler_params=_params("parallel", "parallel"),
        name="dwconv",
    )(proj, proj, proj, w.T, b.reshape(1, -1))


def _ssd_kernel(xbc_ref, dt_ref, bias_ref, alog_ref, hexp_ref, mq_ref, mk_ref, mask_ref, o_ref, st_ref):
    @pl.when(pl.program_id(1) == 0)
    def _():
        st_ref[...] = jnp.zeros_like(st_ref)

    c = xbc_ref.shape[0]
    hpg = MB_HEADS // MB_GROUPS
    gw = hpg * MB_HEAD_DIM
    dt = _softplus(dt_ref[...] + bias_ref[...])
    la = -dt * jnp.exp(alog_ref[...])
    cq = _sel_dot(mq_ref[...], la)
    ck = _sel_dot(mk_ref[...], la)
    cq_t = _dot_nt_sel(la, mq_ref[...])
    hexp = hexp_ref[...]
    dt_x = _dot_sel(dt, hexp)
    eq_x = jnp.exp(_dot_sel(cq, hexp))
    ek_x = jnp.exp(_dot_sel(ck, hexp))
    etot_x = jnp.exp(_dot_sel(jnp.sum(la, axis=0, keepdims=True), hexp))
    xs = xbc_ref[:, 0:MB_INNER] * dt_x
    mask = mask_ref[...]
    outs = []
    for g in range(MB_GROUPS):
        bm = xbc_ref[:, MB_INNER + g * MB_STATE:MB_INNER + (g + 1) * MB_STATE].astype(BF16)
        cm = xbc_ref[:, MB_INNER + MB_BC_W + g * MB_STATE:MB_INNER + MB_BC_W + (g + 1) * MB_STATE].astype(BF16)
        cb = _dot_nt(cm, bm)
        st = st_ref[g]
        gs = slice(g * gw, (g + 1) * gw)
        y_inter = _dot(cm, st.astype(BF16)) * eq_x[:, gs]
        for r in range(hpg):
            hd = g * hpg + r
            diff = cq[:, hd:hd + 1] - cq_t[hd:hd + 1, :]
            w = cb * jnp.exp(jnp.where(mask > 0.0, diff, -jnp.inf))
            ps = slice(hd * MB_HEAD_DIM, (hd + 1) * MB_HEAD_DIM)
            outs.append(_dot(w.astype(BF16), xs[:, ps].astype(BF16))
                        + y_inter[:, r * MB_HEAD_DIM:(r + 1) * MB_HEAD_DIM])
        st_ref[g] = st * etot_x[:, gs] + _dot_tn(bm, (xs[:, gs] * ek_x[:, gs]).astype(BF16))
    o_ref[...] = jnp.concatenate(outs, axis=-1)


def _dot_nt_sel(x, m01):
    hi, mid, lo = _split3(x)
    f = lambda p: lax.dot_general(p, m01, (((0,), (1,)), ((), ())), preferred_element_type=F32)
    return f(hi) + (f(mid) + f(lo))


def _ssd_scan(xbc, proj, dt_col, dt_bias, a_log, n_ctx, reverse):
    bsz, t, _ = xbc.shape
    c = SCAN_CHUNK
    n_chunks, col, const, sel, msk = _scan_specs(c, n_ctx, t, reverse)
    mq, mk = sel[0:c], sel[c:2 * c]
    tri = np.tril(np.ones((c, c), np.float32))
    mask = jnp.asarray(tri[::-1, ::-1].copy() if reverse else tri)
    pad = lambda v: jnp.zeros((1, LANES), F32).at[0, :MB_HEADS].set(v)
    hexp = np.zeros((LANES, MB_INNER), np.float32)
    for hd in range(MB_HEADS):
        hexp[hd, hd * MB_HEAD_DIM:(hd + 1) * MB_HEAD_DIM] = 1.0
    hexp = jnp.asarray(hexp, BF16)
    bias = pad(dt_bias)
    alog = pad(a_log.astype(F32))
    return pl.pallas_call(
        _ssd_kernel,
        grid=(bsz, n_chunks),
        in_specs=[col(xbc.shape[-1], 0), col(LANES, dt_col), const(bias), const(alog), const(hexp),
                  const(mq), const(mk), const(mask)],
        out_specs=col(MB_INNER, 0),
        out_shape=jax.ShapeDtypeStruct((bsz, t, MB_INNER), F32),
        scratch_shapes=[pltpu.VMEM((MB_GROUPS, MB_STATE, MB_INNER // MB_GROUPS), F32)],
        compiler_params=_params("parallel", "arbitrary"),
        name="ssd_scan_rev" if reverse else "ssd_scan_fwd",
    )(xbc, proj, bias, alog, hexp, mq, mk, mask)


def _mix_out_cd_kernel(h_ref, hy_ref, yf_ref, yb_ref, xs_ref, z_ref, dsk_ref, ng_ref, w_ref, gt_ref, o_ref):
    y = (yf_ref[...] + yb_ref[...] + dsk_ref[...] * xs_ref[...]) * _silu(z_ref[...])
    gw = MB_INNER // MB_GROUPS
    ys = [_rms(y[:, g * gw:(g + 1) * gw], ng_ref[:, g * gw:(g + 1) * gw]) for g in range(MB_GROUPS)]
    feat = jnp.concatenate([hy_ref[...]] + ys, axis=-1).astype(BF16)
    o_ref[...] = h_ref[...] + gt_ref[...] * _dot(feat, w_ref[...])


def _mix_out_cd(h, hy, y_ssd, xbc, proj, z_col, d_skip_x, norm_g, w_out, gate, n_ctx):
    bsz, t, d = h.shape
    tm = ROW_TILE
    n_lat = t - n_ctx
    off = n_ctx // tm
    row = lambda width, idx: pl.BlockSpec((None, tm, width), lambda b, i: (b, i + off, idx))
    vec = pl.BlockSpec((1, MB_INNER), lambda b, i: (0, 0))
    return pl.pallas_call(
        _mix_out_cd_kernel,
        grid=(bsz, n_lat // tm),
        in_specs=[row(d, 0), pl.BlockSpec((None, tm, HY_CH), lambda b, i: (b, i, 0)),
                  row(MB_INNER, 0), row(MB_INNER, 0), row(MB_INNER, 0), row(MB_INNER, z_col), vec, vec,
                  pl.BlockSpec(w_out.shape, lambda b, i: (0, 0)),
                  pl.BlockSpec((None, None, 1, d), lambda b, i: (b, 1, 0, 0))],
        out_specs=pl.BlockSpec((None, tm, d), lambda b, i: (b, i, 0)),
        out_shape=jax.ShapeDtypeStruct((bsz, n_lat, d), F32),
        compiler_params=_params("parallel", "parallel"),
        name="mix_out_cd",
    )(h, hy, y_ssd[0], y_ssd[1], xbc, proj, d_skip_x, norm_g.reshape(1, -1), w_out, gate)


def _ffn_pre_kernel(h_ref, g_ref, sh_ref, sc_ref, rw_ref, v_ref, s_ref):
    v = _rms(h_ref[...], g_ref[...]) * (1.0 + sc_ref[...]) + sh_ref[...]
    v_ref[...] = v.astype(BF16)
    s_ref[...] = _sigmoid(_dot_f32(v, rw_ref[...]))


def _ffn_pre(h, g, shift, scale, router_w_pad, n_ctx):
    bsz, t, d = h.shape
    tm = ROW_TILE
    seg = lambda b, i: (b, (i * tm >= n_ctx).astype(jnp.int32), 0, 0)
    return pl.pallas_call(
        _ffn_pre_kernel,
        grid=(bsz, t // tm),
        in_specs=[pl.BlockSpec((None, tm, d), lambda b, i: (b, i, 0)),
                  pl.BlockSpec((1, d), lambda b, i: (0, 0)),
                  pl.BlockSpec((None, None, 1, d), seg),
                  pl.BlockSpec((None, None, 1, d), seg),
                  pl.BlockSpec((d, LANES), lambda b, i: (0, 0))],
        out_specs=[pl.BlockSpec((None, tm, d), lambda b, i: (b, i, 0)),
                   pl.BlockSpec((None, tm, LANES), lambda b, i: (b, i, 0))],
        out_shape=[jax.ShapeDtypeStruct((bsz, t, d), BF16),
                   jax.ShapeDtypeStruct((bsz, t, LANES), F32)],
        compiler_params=_params("parallel", "parallel"),
        name="ffn_pre",
    )(h, g.reshape(1, d), shift, scale, router_w_pad)


def _experts_kernel(be_ref, nb_ref, x_ref, wg_ref, wu_ref, wd_ref, o_ref, wg_s, wu_s, wd_s):
    i = pl.program_id(0)
    prev = be_ref[jnp.maximum(i - 1, 0)]
    changed = jnp.logical_or(i == 0, be_ref[i] != prev)

    @pl.when(changed)
    def _():
        wg_s[...] = wg_ref[...].astype(BF16)
        wu_s[...] = wu_ref[...].astype(BF16)
        wd_s[...] = wd_ref[...].astype(BF16)

    @pl.when(i < nb_ref[0])
    def _():
        x = x_ref[...]
        hid = _silu(_dot(x, wg_s[...])) * _dot(x, wu_s[...])
        o_ref[...] = _dot(hid.astype(BF16), wd_s[...])

    @pl.when(i >= nb_ref[0])
    def _():
        o_ref[...] = jnp.zeros_like(o_ref)


def _experts(xb, block_e, n_used, w_gate, w_up, w_down):
    n_slots, d = xb.shape
    n_blocks = n_slots // MOE_BLOCK
    de = w_gate.shape[-1]
    wspec = lambda shape: pl.BlockSpec((None,) + shape, lambda i, be, nb: (be[i], 0, 0))
    return pl.pallas_call(
        _experts_kernel,
        grid_spec=pltpu.PrefetchScalarGridSpec(
            num_scalar_prefetch=2,
            grid=(n_blocks,),
            in_specs=[pl.BlockSpec((MOE_BLOCK, d), lambda i, be, nb: (i, 0)),
                      wspec((d, de)), wspec((d, de)), wspec((de, d))],
            out_specs=pl.BlockSpec((MOE_BLOCK, d), lambda i, be, nb: (i, 0)),
            scratch_shapes=[pltpu.VMEM((d, de), BF16), pltpu.VMEM((d, de), BF16), pltpu.VMEM((de, d), BF16)]),
        out_shape=jax.ShapeDtypeStruct((n_slots, d), F32),
        compiler_params=_params("arbitrary"),
        name="moe_experts",
    )(block_e, n_used, xb, w_gate, w_up, w_down)


def _ffn_post_kernel(h_ref, y0_ref, y1_ref, w_ref, gt_ref, g_ref, o_ref, *, final):
    w = w_ref[...]
    y = w[:, 0:1] * y0_ref[...] + w[:, 1:2] * y1_ref[...]
    out = h_ref[...] + gt_ref[...] * y
    o_ref[...] = _rms(out, g_ref[...]) if final else out


def _ffn_post(h, y0, y1, w, gate, n_ctx, final_g=None):
    bsz, t, d = h.shape
    tm = ROW_TILE
    seg = lambda b, i: (b, (i * tm >= n_ctx).astype(jnp.int32), 0, 0)
    row = lambda width: pl.BlockSpec((None, tm, width), lambda b, i: (b, i, 0))
    final = final_g is not None
    g = final_g if final else jnp.ones((d,), F32)
    return pl.pallas_call(
        functools.partial(_ffn_post_kernel, final=final),
        grid=(bsz, t // tm),
        in_specs=[row(d), row(d), row(d), row(LANES), pl.BlockSpec((None, None, 1, d), seg),
                  pl.BlockSpec((1, d), lambda b, i: (0, 0))],
        out_specs=row(d),
        out_shape=jax.ShapeDtypeStruct((bsz, t, d), F32),
        compiler_params=_params("parallel", "parallel"),
        name="ffn_post",
    )(h, y0, y1, w, gate, g.reshape(1, d))


def _route(scores, router_b):
    n = scores.shape[0]
    sel = scores + router_b
    group_score = lax.top_k(sel.reshape(n, N_GROUPS, EXPERTS_PER_GROUP), TOP_K)[0].sum(-1)
    best = jnp.argmax(group_score, axis=-1)
    in_group = (jnp.arange(N_EXPERTS) // EXPERTS_PER_GROUP)[None] == best[:, None]
    _, idx = lax.top_k(jnp.where(in_group, sel, -jnp.inf), TOP_K)
    w = jnp.take_along_axis(scores, idx, axis=1)
    w = w / jnp.sum(w, axis=-1, keepdims=True)
    e_flat = idx.reshape(-1).astype(jnp.int32)
    onehot = (e_flat[:, None] == jnp.arange(N_EXPERTS, dtype=jnp.int32)[None]).astype(jnp.int32)
    csum = jnp.cumsum(onehot, axis=0)
    rank = jnp.take_along_axis(csum - onehot, e_flat[:, None], axis=1)[:, 0]
    counts = csum[-1]
    padded = (counts + MOE_BLOCK - 1) // MOE_BLOCK * MOE_BLOCK
    pend = jnp.cumsum(padded)
    dest = (pend - padded)[e_flat] + rank
    n_slots = (n * TOP_K + MOE_BLOCK - 1) // MOE_BLOCK * MOE_BLOCK + N_EXPERTS * MOE_BLOCK
    n_blocks = n_slots // MOE_BLOCK
    tok = jnp.repeat(jnp.arange(n, dtype=jnp.int32), TOP_K)
    slot_tok = jnp.full((n_slots,), n, jnp.int32).at[dest].set(tok)
    block_e = jnp.clip(jnp.searchsorted(pend, jnp.arange(n_blocks, dtype=jnp.int32) * MOE_BLOCK, side='right'),
                       0, N_EXPERTS - 1).astype(jnp.int32)
    n_used = (pend[-1] // MOE_BLOCK).astype(jnp.int32).reshape(1)
    return w, dest.reshape(n, TOP_K), slot_tok, block_e, n_used


def _moe(h, g, shift, scale, gate, router_w_pad, router_b, w_gate, w_up, w_down, n_ctx, final_g=None):
    bsz, t, d = h.shape
    n = bsz * t
    v, scores = _ffn_pre(h, g, shift, scale, router_w_pad, n_ctx)
    w, dest, slot_tok, block_e, n_used = _route(scores.reshape(n, LANES)[:, :N_EXPERTS], router_b)
    vp = jnp.concatenate([v.reshape(n, d), jnp.zeros((1, d), BF16)], axis=0)
    yb = _experts(vp[slot_tok], block_e, n_used, w_gate, w_up, w_down)
    y0 = yb[dest[:, 0]].reshape(bsz, t, d)
    y1 = yb[dest[:, 1]].reshape(bsz, t, d)
    wpad = jnp.zeros((n, LANES), F32).at[:, :TOP_K].set(w).reshape(bsz, t, LANES)
    return _ffn_post(h, y0, y1, wpad, gate, n_ctx, final_g)


def _hyena_filters(n, w1, b1, w2, b2, w3, freq):
    t = jnp.linspace(0.0, 1.0, n, dtype=F32)[:, None]
    bands = jnp.linspace(1e-4, HY_BANDS - 1, HY_BANDS, dtype=F32)
    ang = (2.0 * math.pi / n) * jnp.arange(n, dtype=F32)[:, None] * bands
    z = jnp.concatenate([t, jnp.cos(ang), -jnp.sin(ang)], axis=-1)
    hid = jnp.sin(freq * (z @ w1 + b1))
    hid = jnp.sin(freq * (hid @ w2 + b2))
    filt = (hid @ w3).reshape(n, HY_ORDER, 2, HY_CH)
    rates = jnp.abs(jnp.linspace(HY_MIN_DECAY, HY_MAX_DECAY, HY_CH, dtype=F32))
    return filt * jnp.exp(-t * rates)[:, None, None, :]


def _bidir_long_conv(u, h_fwd, h_bwd, bias):
    n = u.shape[1]
    k = jnp.concatenate([h_fwd[:1] + h_bwd[:1], h_fwd[1:], jnp.zeros_like(h_fwd[:1]),
                         jnp.flip(h_bwd[1:], 0)], axis=0)
    uf = jnp.fft.rfft(u.astype(F32), n=2 * n, axis=1)
    kf = jnp.fft.rfft(k.astype(F32), axis=0)
    y = jnp.fft.irfft(uf * kf[None], n=2 * n, axis=1)[:, :n]
    return y + u.astype(F32) * bias


def _hyena(seg, fw1, fb1, fw2, fb2, fw3, freq, conv_bias):
    n = seg.shape[1]
    v, x1, x2 = jnp.split(seg, 3, axis=-1)
    filt = _hyena_filters(n, fw1, fb1, fw2, fb2, fw3, freq)
    zz = x1 * _bidir_long_conv(v, filt[:, 0, 0], filt[:, 0, 1], conv_bias[0])
    return x2 * _bidir_long_conv(zz, filt[:, 1, 0], filt[:, 1, 1], conv_bias[1])


CD_HY, CD_Z, CD_XBC, CD_DT = 0, 1536, 2048, 3072
CD_PAD_COLS = CD_DT + 2 * LANES


def _reorder_ab(w):
    gq, gk, gv, gg, lr_f, lr_b, hq, hf_f, hf_b, hi, hg = jnp.split(
        w, np.cumsum([256, 256, 512, 512, 16, 16, 512, 512, 512, 512, 512])[:-1].tolist(), axis=-1)
    pad = jnp.zeros((w.shape[0], AB_PAD_COLS - AB_LR - 2 * GLA_LOW_RANK), w.dtype)
    return jnp.concatenate([gq, gk, gv, gg, hq, hf_f, hf_b, hi, hg, lr_f, lr_b, pad], axis=-1)


def _reorder_cd(w):
    hy, z, xbc, dt_f, dt_b = jnp.split(w, np.cumsum([1536, 512, 1024, 8, 8])[:-1].tolist(), axis=-1)
    pad = jnp.zeros((w.shape[0], LANES - MB_HEADS), w.dtype)
    return jnp.concatenate([hy, z, xbc, dt_f, pad, dt_b, pad], axis=-1)


def kernel(x, c, ctx, c_ctx, ada_w, ada_b, norm_mix_g, norm_ffn_g, norm_out_g, ab_w_in, ab_w_out, gla_gate_w, gla_gate_b, gla_norm_g, hg_lb, hg_norm_g, cd_w_in, cd_w_out, hy_short_w, hy_short_b, hy_w1, hy_b1, hy_w2, hy_b2, hy_w3, hy_freq, hy_bias, mb_conv_w, mb_conv_b, mb_dt_bias, mb_a_log, mb_d, mb_norm_g, router_w, router_b, moe_w_gate, moe_w_up, moe_w_down):
    bsz, n_lat, d = x.shape
    n_ctx = ctx.shape[1]
    t = n_ctx + n_lat
    assert ada_w.shape[0] == 2 and ab_w_in.shape[0] == 1 and cd_w_in.shape[0] == 1

    cond = jnp.zeros((8, d), F32).at[:bsz].set(c).at[bsz].set(c_ctx)
    m = _adaln(cond, ada_w, ada_b)

    def mods(layer):
        lat = m[layer, :bsz].reshape(bsz, 6, d)
        cx = jnp.broadcast_to(m[layer, bsz].reshape(1, 6, d), (bsz, 6, d))
        both = jnp.stack([cx, lat], axis=1)
        return [both[:, :, j][:, :, None, :] for j in range(6)]

    lb_all = jnp.cumsum(jax.nn.softmax(hg_lb.astype(F32), axis=1), axis=1)
    router_w_pad = jnp.zeros((d, LANES), F32).at[:, :N_EXPERTS].set(router_w)
    h = jnp.concatenate([ctx, x], axis=1)

    sh_m, sc_m, gt_m, sh_f, sc_f, gt_f = mods(0)
    proj = _norm_proj(h, norm_mix_g[0], sh_m, sc_m, _reorder_ab(ab_w_in[0]).astype(BF16), n_ctx)
    o_gla, o_hg = [], []
    for dd in range(2):
        gwp = jnp.zeros((LANES, GLA_KEY_W), F32).at[
            GLA_LOW_RANK * dd:GLA_LOW_RANK * (dd + 1)].set(gla_gate_w[0, dd])
        o_gla.append(_gla_scan(proj, gwp, gla_gate_b[0, dd].reshape(1, -1), n_ctx, dd == 1))
        o_hg.append(_hgrn_scan(proj, lb_all[dd, 0].reshape(1, -1), n_ctx, dd == 1))
    h = _mix_out_ab(h, o_gla, o_hg, proj, gla_norm_g[0], hg_norm_g[0], ab_w_out[0].astype(BF16), gt_m, n_ctx)
    h = _moe(h, norm_ffn_g[0], sh_f, sc_f, gt_f, router_w_pad, router_b,
             moe_w_gate[0], moe_w_up[0], moe_w_down[0], n_ctx)

    sh_m, sc_m, gt_m, sh_f, sc_f, gt_f = mods(1)
    proj = _norm_proj(h, norm_mix_g[1], sh_m, sc_m, _reorder_cd(cd_w_in[0]).astype(BF16), n_ctx)
    hy_in = _dwconv(proj, CD_HY, 3 * HY_CH, hy_short_w[0], hy_short_b[0], (0, n_ctx, t), act=False)
    hy = _hyena(hy_in[:, n_ctx:], hy_w1[0], hy_b1[0], hy_w2[0], hy_b2[0], hy_w3[0], hy_freq[0], hy_bias[0])
    xbc = _dwconv(proj, CD_XBC, MB_INNER + 2 * MB_BC_W, mb_conv_w[0], mb_conv_b[0], (0, n_ctx, t), act=True)
    y_ssd = [_ssd_scan(xbc, proj, CD_DT // LANES + dd, mb_dt_bias[0, dd], mb_a_log[0, dd], n_ctx, dd == 1)
             for dd in range(2)]
    d_skip_x = jnp.repeat(mb_d[0], MB_HEAD_DIM).reshape(1, MB_INNER)
    h = _mix_out_cd(h, hy, y_ssd, xbc, proj, CD_Z // MB_INNER, d_skip_x, mb_norm_g[0],
                    cd_w_out[0].astype(BF16), gt_m, n_ctx)
    return _moe(h, norm_ffn_g[1], sh_f, sc_f, gt_f, router_w_pad, router_b,
                moe_w_gate[1], moe_w_up[1], moe_w_down[1], 0, final_g=norm_out_g)
```

```python
import functools
import math

import numpy as np
import jax
import jax.numpy as jnp
from jax import lax
from jax.experimental import pallas as pl
from jax.experimental.pallas import tpu as pltpu

NORM_EPS = 1e-6
GLA_HEADS, GLA_DK, GLA_DV, GLA_LOW_RANK, GLA_TAU = 4, 64, 128, 16, 16.0
GLA_KEY_W, GLA_VAL_W = GLA_HEADS * GLA_DK, GLA_HEADS * GLA_DV
HG_HEADS, HG_EXPAND, HG_DV = 4, 128, 128
HG_KEY_W, HG_VAL_W = HG_HEADS * HG_EXPAND, HG_HEADS * HG_DV
HY_CH, HY_ORDER, HY_SHORT, HY_BANDS, HY_FILT_HID = 512, 2, 3, 16, 64
HY_MIN_DECAY = math.log(1e-2) / 1.5
HY_MAX_DECAY = math.log(1e-2) / 0.3
MB_HEADS, MB_HEAD_DIM, MB_GROUPS, MB_STATE = 8, 64, 2, 128
MB_INNER = MB_HEADS * MB_HEAD_DIM
MB_BC_W = MB_GROUPS * MB_STATE
N_EXPERTS, N_GROUPS, TOP_K, MOE_BLOCK = 16, 4, 2, 256
EXPERTS_PER_GROUP = N_EXPERTS // N_GROUPS

LANES = 128
SCAN_CHUNK = 128
ROW_TILE = 256
VMEM_LIMIT = 56 * 1024 * 1024

BF16 = jnp.bfloat16
F32 = jnp.float32


def _params(*sem):
    return pltpu.CompilerParams(dimension_semantics=sem, vmem_limit_bytes=VMEM_LIMIT)


def _split3(x):
    hi = x.astype(BF16)
    r1 = x - hi.astype(F32)
    mid = r1.astype(BF16)
    lo = (r1 - mid.astype(F32)).astype(BF16)
    return hi, mid, lo


def _dot(a, b):
    return jnp.dot(a, b, preferred_element_type=F32)


def _dot_nt(a, b):
    return lax.dot_general(a, b, (((1,), (1,)), ((), ())), preferred_element_type=F32)


def _dot_tn(a, b):
    return lax.dot_general(a, b, (((0,), (0,)), ((), ())), preferred_element_type=F32)


def _sel_dot(m01, x):
    hi, mid, lo = _split3(x)
    return _dot(m01, hi) + (_dot(m01, mid) + _dot(m01, lo))


def _dot_sel(x, m01):
    hi, mid, lo = _split3(x)
    return _dot(hi, m01) + (_dot(mid, m01) + _dot(lo, m01))


def _dot_f32(a, b):
    ah = a.astype(BF16)
    al = (a - ah.astype(F32)).astype(BF16)
    bh = b.astype(BF16)
    bl = (b - bh.astype(F32)).astype(BF16)
    return _dot(ah, bh) + (_dot(ah, bl) + _dot(al, bh))


def _silu(x):
    return x * (1.0 / (1.0 + jnp.exp(-x)))


def _sigmoid(x):
    return 1.0 / (1.0 + jnp.exp(-x))


def _softplus(x):
    return jnp.maximum(x, 0.0) + jnp.log(1.0 + jnp.exp(-jnp.abs(x)))


def _rms(x, g):
    return x * lax.rsqrt(jnp.mean(x * x, axis=-1, keepdims=True) + NORM_EPS) * g


def _adaln_kernel(c_ref, w_ref, b_ref, o_ref):
    o_ref[...] = _dot_f32(_silu(c_ref[...]), w_ref[...]) + b_ref[...]


def _adaln(cond, w, b):
    n_l, d, n6 = w.shape
    tn = 1536
    return pl.pallas_call(
        _adaln_kernel,
        grid=(n_l, n6 // tn),
        in_specs=[pl.BlockSpec((8, d), lambda l, j: (0, 0)),
                  pl.BlockSpec((None, d, tn), lambda l, j: (l, 0, j)),
                  pl.BlockSpec((None, 1, tn), lambda l, j: (l, 0, j))],
        out_specs=pl.BlockSpec((None, 8, tn), lambda l, j: (l, 0, j)),
        out_shape=jax.ShapeDtypeStruct((n_l, 8, n6), F32),
        compiler_params=_params("parallel", "parallel"),
        name="adaln",
    )(cond, w, b.reshape(n_l, 1, n6))


def _norm_proj_kernel(h_ref, g_ref, sh_ref, sc_ref, w_ref, o_ref):
    u = _rms(h_ref[...], g_ref[...]) * (1.0 + sc_ref[...]) + sh_ref[...]
    o_ref[...] = _dot(u.astype(BF16), w_ref[...])


def _norm_proj(h, g, shift, scale, w, n_ctx):
    bsz, t, d = h.shape
    n = w.shape[1]
    tm = ROW_TILE
    seg = lambda b, i: (b, (i * tm >= n_ctx).astype(jnp.int32), 0, 0)
    return pl.pallas_call(
        _norm_proj_kernel,
        grid=(bsz, t // tm),
        in_specs=[pl.BlockSpec((None, tm, d), lambda b, i: (b, i, 0)),
                  pl.BlockSpec((1, d), lambda b, i: (0, 0)),
                  pl.BlockSpec((None, None, 1, d), seg),
                  pl.BlockSpec((None, None, 1, d), seg),
                  pl.BlockSpec((d, n), lambda b, i: (0, 0))],
        out_specs=pl.BlockSpec((None, tm, n), lambda b, i: (b, i, 0)),
        out_shape=jax.ShapeDtypeStruct((bsz, t, n), F32),
        compiler_params=_params("parallel", "parallel"),
        name="norm_proj",
    )(h, g.reshape(1, d), shift, scale, w)


def _scan_constants(c, reverse):
    t = np.arange(c)[:, None]
    u = np.arange(c)[None, :]
    sels = [u <= t, u > t]
    masks = []
    m = c // 2
    while m >= 1:
        blk = t // (2 * m)
        upper_t = (t % (2 * m)) >= m
        r = blk * (2 * m) + m - 1
        s_blk = u // (2 * m)
        upper_s = (u % (2 * m)) >= m
        sels += [upper_t & (u > r) & (u <= t), (~upper_t) & (u > t) & (u <= r)]
        masks.append((blk == s_blk) & upper_t & (~upper_s))
        m //= 2
    masks.append(t == u)
    sel = np.stack(sels).astype(np.float32)
    msk = np.stack(masks).astype(np.float32)
    if reverse:
        sel = sel[:, ::-1, ::-1]
        msk = msk[:, ::-1, ::-1]
    return np.ascontiguousarray(sel.reshape(-1, c)), np.ascontiguousarray(msk)


def _chunk_order(i, n_ctx_chunks, n_chunks, reverse):
    if not reverse:
        return i
    return jnp.where(i < n_ctx_chunks, n_ctx_chunks - 1 - i, n_chunks - 1 - (i - n_ctx_chunks))


def _decay_chunk(q, k, v, la, sel_ref, mask_ref, st_ref, heads, dk, dv):
    c = q.shape[0]
    n_lvl = mask_ref.shape[0] - 1
    cs = _sel_dot(sel_ref[...], la)
    e_q = jnp.exp(cs[0:c])
    e_k = jnp.exp(cs[c:2 * c])
    e_tot = jnp.exp(jnp.sum(la, axis=0, keepdims=True))
    outs = []
    for h in range(heads):
        ks = slice(h * dk, (h + 1) * dk)
        qh, kh, vh = q[:, ks], k[:, ks], v[:, h * dv:(h + 1) * dv].astype(BF16)
        att = mask_ref[n_lvl] * _dot_nt(qh.astype(BF16), kh.astype(BF16))
        for l in range(n_lvl):
            eq = jnp.exp(cs[(2 + 2 * l) * c:(3 + 2 * l) * c, ks])
            ek = jnp.exp(cs[(3 + 2 * l) * c:(4 + 2 * l) * c, ks])
            att = att + mask_ref[l] * _dot_nt((qh * eq).astype(BF16), (kh * ek).astype(BF16))
        st = st_ref[h]
        o = _dot(att.astype(BF16), vh)
        o = o + _dot_nt((qh * e_q[:, ks]).astype(BF16), st.astype(BF16))
        outs.append(o)
        kd = (kh * e_k[:, ks]).astype(BF16)
        st_ref[h] = st * e_tot[:, ks] + _dot_tn(vh, kd)
    return jnp.concatenate(outs, axis=-1)


def _log_sigmoid(x):
    return jnp.minimum(x, 0.0) - jnp.log(1.0 + jnp.exp(-jnp.abs(x)))


def _gla_kernel(q_ref, k_ref, v_ref, lr_ref, gw_ref, gb_ref, sel_ref, mask_ref, o_ref, st_ref):
    @pl.when(pl.program_id(1) == 0)
    def _():
        st_ref[...] = jnp.zeros_like(st_ref)

    z = _dot_f32(lr_ref[...], gw_ref[...]) + gb_ref[...]
    la = _log_sigmoid(z) * (1.0 / GLA_TAU)
    q = q_ref[...] * (GLA_DK ** -0.5)
    o_ref[...] = _decay_chunk(q, k_ref[...], v_ref[...], la, sel_ref, mask_ref, st_ref,
                              GLA_HEADS, GLA_DK, GLA_DV)


def _hgrn_kernel(q_ref, f_ref, v_ref, lb_ref, sel_ref, mask_ref, o_ref, st_ref):
    @pl.when(pl.program_id(1) == 0)
    def _():
        st_ref[...] = jnp.zeros_like(st_ref)

    lb = lb_ref[...]
    f = lb + (1.0 - lb) * _sigmoid(f_ref[...])
    o_ref[...] = _decay_chunk(_silu(q_ref[...]), 1.0 - f, v_ref[...], jnp.log(f), sel_ref, mask_ref,
                              st_ref, HG_HEADS, HG_EXPAND, HG_DV)


def _scan_specs(c, n_ctx, t, reverse):
    n_chunks = t // c
    n_ctx_chunks = n_ctx // c
    order = functools.partial(_chunk_order, n_ctx_chunks=n_ctx_chunks, n_chunks=n_chunks, reverse=reverse)

    def col(width, idx):
        return pl.BlockSpec((None, c, width), lambda b, i: (b, order(i), idx))

    sel, msk = _scan_constants(c, reverse)
    const = lambda a: pl.BlockSpec(a.shape, lambda b, i: (0,) * a.ndim)
    return n_chunks, col, const, jnp.asarray(sel, BF16), jnp.asarray(msk, F32)


AB_Q, AB_K, AB_V, AB_G = 0, 256, 512, 1024
AB_HQ, AB_HF, AB_HI, AB_HG, AB_LR = 1536, 2048, 3072, 3584, 4096
AB_PAD_COLS = 4224


def _gla_scan(proj, gate_w_pad, gate_b, n_ctx, reverse):
    bsz, t, _ = proj.shape
    c = SCAN_CHUNK
    n_chunks, col, const, sel, msk = _scan_specs(c, n_ctx, t, reverse)
    return pl.pallas_call(
        _gla_kernel,
        grid=(bsz, n_chunks),
        in_specs=[col(GLA_KEY_W, AB_Q // GLA_KEY_W), col(GLA_KEY_W, AB_K // GLA_KEY_W),
                  col(GLA_VAL_W, AB_V // GLA_VAL_W), col(LANES, AB_LR // LANES),
                  const(gate_w_pad), const(gate_b), const(sel), const(msk)],
        out_specs=col(GLA_VAL_W, 0),
        out_shape=jax.ShapeDtypeStruct((bsz, t, GLA_VAL_W), F32),
        scratch_shapes=[pltpu.VMEM((GLA_HEADS, GLA_DV, GLA_DK), F32)],
        compiler_params=_params("parallel", "arbitrary"),
        name="gla_scan_rev" if reverse else "gla_scan_fwd",
    )(proj, proj, proj, proj, gate_w_pad, gate_b, sel, msk)


def _hgrn_scan(proj, lb, n_ctx, reverse):
    bsz, t, _ = proj.shape
    c = SCAN_CHUNK
    n_chunks, col, const, sel, msk = _scan_specs(c, n_ctx, t, reverse)
    f_col = AB_HF // HG_KEY_W + (1 if reverse else 0)
    return pl.pallas_call(
        _hgrn_kernel,
        grid=(bsz, n_chunks),
        in_specs=[col(HG_KEY_W, AB_HQ // HG_KEY_W), col(HG_KEY_W, f_col), col(HG_VAL_W, AB_HI // HG_VAL_W),
                  const(lb), const(sel), const(msk)],
        out_specs=col(HG_VAL_W, 0),
        out_shape=jax.ShapeDtypeStruct((bsz, t, HG_VAL_W), F32),
        scratch_shapes=[pltpu.VMEM((HG_HEADS, HG_DV, HG_EXPAND), F32)],
        compiler_params=_params("parallel", "arbitrary"),
        name="hgrn_scan_rev" if reverse else "hgrn_scan_fwd",
    )(proj, proj, proj, lb, sel, msk)


def _mix_out_ab_kernel(h_ref, gf_ref, gb_ref, hf_ref, hb_ref, gg_ref, hg_ref, gn_ref, hn_ref,
                       w_ref, gt_ref, o_ref):
    feats = []
    for o, gate, g in ((gf_ref[...] + gb_ref[...], gg_ref[...], gn_ref[...]),
                       (hf_ref[...] + hb_ref[...], hg_ref[...], hn_ref[...])):
        for hd in range(o.shape[-1] // LANES):
            s = slice(hd * LANES, (hd + 1) * LANES)
            feats.append(_rms(o[:, s], g) * _silu(gate[:, s]))
    feat = jnp.concatenate(feats, axis=-1).astype(BF16)
    o_ref[...] = h_ref[...] + gt_ref[...] * _dot(feat, w_ref[...])


def _mix_out_ab(h, o_gla, o_hg, proj, gla_norm_g, hg_norm_g, w_out, gate, n_ctx):
    bsz, t, d = h.shape
    tm = ROW_TILE
    seg = lambda b, i: (b, (i * tm >= n_ctx).astype(jnp.int32), 0, 0)
    row = lambda width, idx: pl.BlockSpec((None, tm, width), lambda b, i: (b, i, idx))
    vec = pl.BlockSpec((1, LANES), lambda b, i: (0, 0))
    return pl.pallas_call(
        _mix_out_ab_kernel,
        grid=(bsz, t // tm),
        in_specs=[row(d, 0), row(GLA_VAL_W, 0), row(GLA_VAL_W, 0), row(HG_VAL_W, 0), row(HG_VAL_W, 0),
                  row(GLA_VAL_W, AB_G // GLA_VAL_W), row(HG_VAL_W, AB_HG // HG_VAL_W), vec, vec,
                  pl.BlockSpec(w_out.shape, lambda b, i: (0, 0)),
                  pl.BlockSpec((None, None, 1, d), seg)],
        out_specs=row(d, 0),
        out_shape=jax.ShapeDtypeStruct((bsz, t, d), F32),
        compiler_params=_params("parallel", "parallel"),
        name="mix_out_ab",
    )(h, o_gla[0], o_gla[1], o_hg[0], o_hg[1], proj, proj, gla_norm_g.reshape(1, -1),
      hg_norm_g.reshape(1, -1), w_out, gate)


def _dwconv_kernel(x_ref, prev_ref, next_ref, w_ref, b_ref, o_ref, *, seg_tiles, act):
    i = pl.program_id(1)
    tm = x_ref.shape[0]
    x = x_ref[...]
    first = functools.reduce(jnp.logical_or, [i == s for s in seg_tiles[:-1]])
    last = functools.reduce(jnp.logical_or, [i == s - 1 for s in seg_tiles[1:]])
    prev_row = jnp.where(first, 0.0, prev_ref[7:8, :])
    next_row = jnp.where(last, 0.0, next_ref[0:1, :])
    rows = lax.broadcasted_iota(jnp.int32, x.shape, 0)
    x_prev = jnp.where(rows == 0, prev_row, pltpu.roll(x, 1, axis=0))
    x_next = jnp.where(rows == tm - 1, next_row, pltpu.roll(x, tm - 1, axis=0))
    y = x_prev * w_ref[0:1, :] + x * w_ref[1:2, :] + x_next * w_ref[2:3, :] + b_ref[...]
    o_ref[...] = _silu(y) if act else y


def _dwconv(proj, col0, width, w, b, seg_bounds, act):
    bsz, t, _ = proj.shape
    tm = ROW_TILE
    off = seg_bounds[0] // tm
    n_tiles = t // tm - off
    seg_tiles = tuple(s // tm - off for s in seg_bounds)
    cb = col0 // width
    r8 = tm // 8
    last8 = t // 8 - 1
    kern = functools.partial(_dwconv_kernel, seg_tiles=seg_tiles, act=act)
    return pl.pallas_call(
        kern,
        grid=(bsz, n_tiles),
        in_specs=[pl.BlockSpec((None, tm, width), lambda bb, i: (bb, i + off, cb)),
                  pl.BlockSpec((None, 8, width), lambda bb, i: (bb, jnp.maximum((i + off) * r8 - 1, 0), cb)),
                  pl.BlockSpec((None, 8, width),
                               lambda bb, i: (bb, jnp.minimum((i + off + 1) * r8, last8), cb)),
                  pl.BlockSpec((3, width), lambda bb, i: (0, 0)),
                  pl.BlockSpec((1, width), lambda bb, i: (0, 0))],
        out_specs=pl.BlockSpec((None, tm, width), lambda bb, i: (bb, i, 0)),
        out_shape=jax.ShapeDtypeStruct((bsz, n_tiles * tm, width), F32),
        compiler_params=_params("parallel", "parallel"),
        name="dwconv",
    )(proj, proj, proj, w.T, b.reshape(1, -1))


def _ssd_kernel(xbc_ref, dt_ref, bias_ref, alog_ref, hexp_ref, mq_ref, mk_ref, mask_ref, o_ref, st_ref):
    @pl.when(pl.program_id(1) == 0)
    def _():
        st_ref[...] = jnp.zeros_like(st_ref)

    c = xbc_ref.shape[0]
    hpg = MB_HEADS // MB_GROUPS
    gw = hpg * MB_HEAD_DIM
    dt = _softplus(dt_ref[...] + bias_ref[...])
    la = -dt * jnp.exp(alog_ref[...])
    cq = _sel_dot(mq_ref[...], la)
    ck = _sel_dot(mk_ref[...], la)
    cq_t = _dot_nt_sel(la, mq_ref[...])
    hexp = hexp_ref[...]
    dt_x = _dot_sel(dt, hexp)
    eq_x = jnp.exp(_dot_sel(cq, hexp))
    ek_x = jnp.exp(_dot_sel(ck, hexp))
    etot_x = jnp.exp(_dot_sel(jnp.sum(la, axis=0, keepdims=True), hexp))
    xs = xbc_ref[:, 0:MB_INNER] * dt_x
    mask = mask_ref[...]
    outs = []
    for g in range(MB_GROUPS):
        bm = xbc_ref[:, MB_INNER + g * MB_STATE:MB_INNER + (g + 1) * MB_STATE].astype(BF16)
        cm = xbc_ref[:, MB_INNER + MB_BC_W + g * MB_STATE:MB_INNER + MB_BC_W + (g + 1) * MB_STATE].astype(BF16)
        cb = _dot_nt(cm, bm)
        st = st_ref[g]
        gs = slice(g * gw, (g + 1) * gw)
        y_inter = _dot(cm, st.astype(BF16)) * eq_x[:, gs]
        for r in range(hpg):
            hd = g * hpg + r
            diff = cq[:, hd:hd + 1] - cq_t[hd:hd + 1, :]
            w = cb * jnp.exp(jnp.where(mask > 0.0, diff, -jnp.inf))
            ps = slice(hd * MB_HEAD_DIM, (hd + 1) * MB_HEAD_DIM)
            outs.append(_dot(w.astype(BF16), xs[:, ps].astype(BF16))
                        + y_inter[:, r * MB_HEAD_DIM:(r + 1) * MB_HEAD_DIM])
        st_ref[g] = st * etot_x[:, gs] + _dot_tn(bm, (xs[:, gs] * ek_x[:, gs]).astype(BF16))
    o_ref[...] = jnp.concatenate(outs, axis=-1)


def _dot_nt_sel(x, m01):
    hi, mid, lo = _split3(x)
    f = lambda p: lax.dot_general(p, m01, (((0,), (1,)), ((), ())), preferred_element_type=F32)
    return f(hi) + (f(mid) + f(lo))


def _ssd_scan(xbc, proj, dt_col, dt_bias, a_log, n_ctx, reverse):
    bsz, t, _ = xbc.shape
    c = SCAN_CHUNK
    n_chunks, col, const, sel, msk = _scan_specs(c, n_ctx, t, reverse)
    mq, mk = sel[0:c], sel[c:2 * c]
    tri = np.tril(np.ones((c, c), np.float32))
    mask = jnp.asarray(tri[::-1, ::-1].copy() if reverse else tri)
    pad = lambda v: jnp.zeros((1, LANES), F32).at[0, :MB_HEADS].set(v)
    hexp = np.zeros((LANES, MB_INNER), np.float32)
    for hd in range(MB_HEADS):
        hexp[hd, hd * MB_HEAD_DIM:(hd + 1) * MB_HEAD_DIM] = 1.0
    hexp = jnp.asarray(hexp, BF16)
    bias = pad(dt_bias)
    alog = pad(a_log.astype(F32))
    return pl.pallas_call(
        _ssd_kernel,
        grid=(bsz, n_chunks),
        in_specs=[col(xbc.shape[-1], 0), col(LANES, dt_col), const(bias), const(alog), const(hexp),
                  const(mq), const(mk), const(mask)],
        out_specs=col(MB_INNER, 0),
        out_shape=jax.ShapeDtypeStruct((bsz, t, MB_INNER), F32),
        scratch_shapes=[pltpu.VMEM((MB_GROUPS, MB_STATE, MB_INNER // MB_GROUPS), F32)],
        compiler_params=_params("parallel", "arbitrary"),
        name="ssd_scan_rev" if reverse else "ssd_scan_fwd",
    )(xbc, proj, bias, alog, hexp, mq, mk, mask)


def _mix_out_cd_kernel(h_ref, hy_ref, yf_ref, yb_ref, xs_ref, z_ref, dsk_ref, ng_ref, w_ref, gt_ref, o_ref):
    y = (yf_ref[...] + yb_ref[...] + dsk_ref[...] * xs_ref[...]) * _silu(z_ref[...])
    gw = MB_INNER // MB_GROUPS
    ys = [_rms(y[:, g * gw:(g + 1) * gw], ng_ref[:, g * gw:(g + 1) * gw]) for g in range(MB_GROUPS)]
    feat = jnp.concatenate([hy_ref[...]] + ys, axis=-1).astype(BF16)
    o_ref[...] = h_ref[...] + gt_ref[...] * _dot(feat, w_ref[...])


def _mix_out_cd(h, hy, y_ssd, xbc, proj, z_col, d_skip_x, norm_g, w_out, gate, n_ctx):
    bsz, t, d = h.shape
    tm = ROW_TILE
    n_lat = t - n_ctx
    off = n_ctx // tm
    row = lambda width, idx: pl.BlockSpec((None, tm, width), lambda b, i: (b, i + off, idx))
    vec = pl.BlockSpec((1, MB_INNER), lambda b, i: (0, 0))
    return pl.pallas_call(
        _mix_out_cd_kernel,
        grid=(bsz, n_lat // tm),
        in_specs=[row(d, 0), pl.BlockSpec((None, tm, HY_CH), lambda b, i: (b, i, 0)),
                  row(MB_INNER, 0), row(MB_INNER, 0), row(MB_INNER, 0), row(MB_INNER, z_col), vec, vec,
                  pl.BlockSpec(w_out.shape, lambda b, i: (0, 0)),
                  pl.BlockSpec((None, None, 1, d), lambda b, i: (b, 1, 0, 0))],
        out_specs=pl.BlockSpec((None, tm, d), lambda b, i: (b, i, 0)),
        out_shape=jax.ShapeDtypeStruct((bsz, n_lat, d), F32),
        compiler_params=_params("parallel", "parallel"),
        name="mix_out_cd",
    )(h, hy, y_ssd[0], y_ssd[1], xbc, proj, d_skip_x, norm_g.reshape(1, -1), w_out, gate)


def _top2_of4(a, b, c, d):
    hi1, lo1, hi2, lo2 = jnp.maximum(a, b), jnp.minimum(a, b), jnp.maximum(c, d), jnp.minimum(c, d)
    return jnp.maximum(hi1, hi2) + jnp.maximum(jnp.minimum(hi1, hi2), jnp.maximum(lo1, lo2))


def _first_argmax(vals, skip=None):
    idx = None
    for j, vj in enumerate(vals):
        if idx is None and skip is None:
            idx, best = jnp.zeros(vj.shape, jnp.int32), vj
            continue
        if idx is None:
            idx, best = jnp.full(vj.shape, -1, jnp.int32), jnp.full(vj.shape, -jnp.inf, F32)
        take = vj > best
        if skip is not None:
            take = jnp.logical_and(take, skip != j)
        idx = jnp.where(take, j, idx)
        best = jnp.where(take, vj, best)
    return idx, best


def _ffn_pre_kernel(h_ref, g_ref, sh_ref, sc_ref, rw_ref, rb_ref, tri_ref,
                    v_ref, ri_ref, rwt_ref, cnt_ref, carry_ref):
    @pl.when(jnp.logical_and(pl.program_id(0) == 0, pl.program_id(1) == 0))
    def _():
        carry_ref[...] = jnp.zeros_like(carry_ref)

    v = _rms(h_ref[...], g_ref[...]) * (1.0 + sc_ref[...]) + sh_ref[...]
    v_ref[...] = v.astype(BF16)
    st = _sigmoid(_dot_f32(v, rw_ref[...])).T[0:N_EXPERTS]
    sel = st + rb_ref[...]
    row = lambda a, e: a[e:e + 1]
    epg = EXPERTS_PER_GROUP
    gscore = [_top2_of4(*[row(sel, g * epg + j) for j in range(epg)]) for g in range(N_GROUPS)]
    best, _ = _first_argmax(gscore)

    def in_best(a, j):
        out = row(a, j)
        for g in range(1, N_GROUPS):
            out = jnp.where(best == g, row(a, g * epg + j), out)
        return out

    vals = [in_best(sel, j) for j in range(epg)]
    raw = [in_best(st, j) for j in range(epg)]
    i1, _ = _first_argmax(vals)
    i2, _ = _first_argmax(vals, skip=i1)
    pick = lambda i: functools.reduce(lambda acc, j: jnp.where(i == j, raw[j], acc), range(1, epg), raw[0])
    w1, w2 = pick(i1), pick(i2)
    wsum = w1 + w2
    e1, e2 = best * epg + i1, best * epg + i2

    experts = lax.broadcasted_iota(jnp.int32, st.shape, 0)
    oh1 = (experts == e1).astype(F32)
    oh2 = (experts == e2).astype(F32)
    cnt = oh1 + oh2
    before = _dot(cnt.astype(BF16), tri_ref[...]) + carry_ref[:, 0:1]
    ri_ref[0:1, :] = e1
    ri_ref[1:2, :] = e2
    ri_ref[2:3, :] = jnp.sum(oh1 * before, axis=0, keepdims=True).astype(jnp.int32)
    ri_ref[3:4, :] = jnp.sum(oh2 * before, axis=0, keepdims=True).astype(jnp.int32)
    ri_ref[4:8, :] = jnp.zeros((4, st.shape[1]), jnp.int32)
    rwt_ref[0:1, :] = w1 / wsum
    rwt_ref[1:2, :] = w2 / wsum
    rwt_ref[2:8, :] = jnp.zeros((6, st.shape[1]), F32)
    carry_ref[...] = carry_ref[...] + jnp.sum(cnt, axis=1, keepdims=True)
    cnt_ref[...] = carry_ref[...]


def _ffn_pre(h, g, shift, scale, router_w_pad, router_b, n_ctx):
    bsz, t, d = h.shape
    tm = ROW_TILE
    seg = lambda b, i: (b, (i * tm >= n_ctx).astype(jnp.int32), 0, 0)
    tri = jnp.asarray(np.triu(np.ones((tm, tm), np.float32), 1), BF16)
    return pl.pallas_call(
        _ffn_pre_kernel,
        grid=(bsz, t // tm),
        in_specs=[pl.BlockSpec((None, tm, d), lambda b, i: (b, i, 0)),
                  pl.BlockSpec((1, d), lambda b, i: (0, 0)),
                  pl.BlockSpec((None, None, 1, d), seg),
                  pl.BlockSpec((None, None, 1, d), seg),
                  pl.BlockSpec((d, LANES), lambda b, i: (0, 0)),
                  pl.BlockSpec((N_EXPERTS, 1), lambda b, i: (0, 0)),
                  pl.BlockSpec((tm, tm), lambda b, i: (0, 0))],
        out_specs=[pl.BlockSpec((None, tm, d), lambda b, i: (b, i, 0)),
                   pl.BlockSpec((None, 8, tm), lambda b, i: (b, 0, i)),
                   pl.BlockSpec((None, 8, tm), lambda b, i: (b, 0, i)),
                   pl.BlockSpec((N_EXPERTS, LANES), lambda b, i: (0, 0))],
        out_shape=[jax.ShapeDtypeStruct((bsz, t, d), BF16),
                   jax.ShapeDtypeStruct((bsz, 8, t), jnp.int32),
                   jax.ShapeDtypeStruct((bsz, 8, t), F32),
                   jax.ShapeDtypeStruct((N_EXPERTS, LANES), F32)],
        scratch_shapes=[pltpu.VMEM((N_EXPERTS, LANES), F32)],
        compiler_params=_params("arbitrary", "arbitrary"),
        name="ffn_pre",
    )(h, g.reshape(1, d), shift, scale, router_w_pad, router_b.reshape(N_EXPERTS, 1), tri)


def _experts_kernel(be_ref, nb_ref, x_ref, wg_ref, wu_ref, wd_ref, o_ref, wg_s, wu_s, wd_s):
    i = pl.program_id(0)
    prev = be_ref[jnp.maximum(i - 1, 0)]
    changed = jnp.logical_or(i == 0, be_ref[i] != prev)

    @pl.when(changed)
    def _():
        wg_s[...] = wg_ref[...].astype(BF16)
        wu_s[...] = wu_ref[...].astype(BF16)
        wd_s[...] = wd_ref[...].astype(BF16)

    @pl.when(i < nb_ref[0])
    def _():
        x = x_ref[...]
        hid = _silu(_dot(x, wg_s[...])) * _dot(x, wu_s[...])
        o_ref[...] = _dot(hid.astype(BF16), wd_s[...])

    @pl.when(i >= nb_ref[0])
    def _():
        o_ref[...] = jnp.zeros_like(o_ref)


def _experts(xb, block_e, n_used, w_gate, w_up, w_down):
    n_slots, d = xb.shape
    n_blocks = n_slots // MOE_BLOCK
    de = w_gate.shape[-1]
    wspec = lambda shape: pl.BlockSpec((None,) + shape, lambda i, be, nb: (be[i], 0, 0))
    return pl.pallas_call(
        _experts_kernel,
        grid_spec=pltpu.PrefetchScalarGridSpec(
            num_scalar_prefetch=2,
            grid=(n_blocks,),
            in_specs=[pl.BlockSpec((MOE_BLOCK, d), lambda i, be, nb: (i, 0)),
                      wspec((d, de)), wspec((d, de)), wspec((de, d))],
            out_specs=pl.BlockSpec((MOE_BLOCK, d), lambda i, be, nb: (i, 0)),
            scratch_shapes=[pltpu.VMEM((d, de), BF16), pltpu.VMEM((d, de), BF16), pltpu.VMEM((de, d), BF16)]),
        out_shape=jax.ShapeDtypeStruct((n_slots, d), F32),
        compiler_params=_params("arbitrary"),
        name="moe_experts",
    )(block_e, n_used, xb, w_gate, w_up, w_down)


def _ffn_post_kernel(h_ref, y0_ref, y1_ref, w_ref, gt_ref, g_ref, o_ref, *, final):
    w = w_ref[...]
    y = w[:, 0:1] * y0_ref[...] + w[:, 1:2] * y1_ref[...]
    out = h_ref[...] + gt_ref[...] * y
    o_ref[...] = _rms(out, g_ref[...]) if final else out


def _ffn_post(h, y0, y1, w, gate, n_ctx, final_g=None):
    bsz, t, d = h.shape
    tm = ROW_TILE
    seg = lambda b, i: (b, (i * tm >= n_ctx).astype(jnp.int32), 0, 0)
    row = lambda width: pl.BlockSpec((None, tm, width), lambda b, i: (b, i, 0))
    final = final_g is not None
    g = final_g if final else jnp.ones((d,), F32)
    return pl.pallas_call(
        functools.partial(_ffn_post_kernel, final=final),
        grid=(bsz, t // tm),
        in_specs=[row(d), row(d), row(d), row(LANES), pl.BlockSpec((None, None, 1, d), seg),
                  pl.BlockSpec((1, d), lambda b, i: (0, 0))],
        out_specs=row(d),
        out_shape=jax.ShapeDtypeStruct((bsz, t, d), F32),
        compiler_params=_params("parallel", "parallel"),
        name="ffn_post",
    )(h, y0, y1, w, gate, g.reshape(1, d))


def _slot_layout(n, ri, counts):
    e = jnp.swapaxes(ri[:, 0:2], 1, 2).reshape(n, TOP_K)
    rank = jnp.swapaxes(ri[:, 2:4], 1, 2).reshape(n, TOP_K)
    padded = (counts + MOE_BLOCK - 1) // MOE_BLOCK * MOE_BLOCK
    pend = jnp.cumsum(padded)
    pstart = pend - padded
    experts = jnp.arange(N_EXPERTS, dtype=jnp.int32)
    dest = rank + jnp.sum(jnp.where(e[..., None] == experts, pstart, 0), axis=-1)
    n_slots = (n * TOP_K + MOE_BLOCK - 1) // MOE_BLOCK * MOE_BLOCK + N_EXPERTS * MOE_BLOCK
    n_blocks = n_slots // MOE_BLOCK
    tok = jnp.repeat(jnp.arange(n, dtype=jnp.int32), TOP_K)
    slot_tok = jnp.full((n_slots,), n, jnp.int32).at[dest.reshape(-1)].set(tok)
    blk0 = jnp.arange(n_blocks, dtype=jnp.int32)[:, None] * MOE_BLOCK
    block_e = jnp.minimum(jnp.sum((pend[None, :] <= blk0).astype(jnp.int32), axis=-1), N_EXPERTS - 1)
    n_used = (pend[-1] // MOE_BLOCK).astype(jnp.int32).reshape(1)
    return dest, slot_tok, block_e.astype(jnp.int32), n_used


def _moe(h, g, shift, scale, gate, router_w_pad, router_b, w_gate, w_up, w_down, n_ctx, final_g=None):
    bsz, t, d = h.shape
    n = bsz * t
    v, ri, rwt, counts = _ffn_pre(h, g, shift, scale, router_w_pad, router_b, n_ctx)
    dest, slot_tok, block_e, n_used = _slot_layout(n, ri, counts[:, 0].astype(jnp.int32))
    w = jnp.swapaxes(rwt[:, 0:2], 1, 2).reshape(n, TOP_K)
    vp = jnp.concatenate([v.reshape(n, d), jnp.zeros((1, d), BF16)], axis=0)
    yb = _experts(vp[slot_tok], block_e, n_used, w_gate, w_up, w_down)
    y0 = yb[dest[:, 0]].reshape(bsz, t, d)
    y1 = yb[dest[:, 1]].reshape(bsz, t, d)
    wpad = jnp.zeros((n, LANES), F32).at[:, :TOP_K].set(w).reshape(bsz, t, LANES)
    return _ffn_post(h, y0, y1, wpad, gate, n_ctx, final_g)


DFT_STEP = 8


def _dft_tables(n):
    size = 2 * n
    r = int(round(math.sqrt(size)))
    assert r * r == size and r % DFT_STEP == 0
    p1 = np.arange(r // 2)[None, None, :]
    p2 = np.arange(r)[:, None, None]
    k1 = np.arange(r)[None, :, None]
    ang = 2.0 * np.pi * (((r * p1 + p2) * k1) % size) / size
    g_re, g_im = np.cos(ang), -np.sin(ang)
    g_in = np.concatenate([g_re, g_im], axis=1)
    g_out = np.concatenate([np.swapaxes(g_re, 1, 2), np.swapaxes(g_im, 1, 2)], axis=2) / size
    a2 = 2.0 * np.pi * ((np.arange(r)[:, None] * np.arange(r)[None, :]) % r) / r
    f_re, f_im = np.cos(a2), -np.sin(a2)
    f_fwd = np.block([[f_re, -f_im], [f_im, f_re]])
    f_inv = np.block([[f_re, f_im], [-f_im, f_re]])
    cast = lambda a: jnp.asarray(a.astype(np.float32)).astype(BF16)
    return r, cast(g_in), cast(g_out), cast(f_fwd), cast(f_inv)


def _hy_filter_kernel(z_ref, w1_ref, b1_ref, w2_ref, b2_ref, w3_ref, fr_ref, rates_ref, o_ref):
    z = z_ref[...]
    hid = jnp.sin(fr_ref[...] * (_dot_f32(z, w1_ref[...]) + b1_ref[...]))
    hid = jnp.sin(fr_ref[...] * (_dot_f32(hid, w2_ref[...]) + b2_ref[...]))
    filt = _dot_f32(hid, w3_ref[...])
    decay = jnp.exp(-z[:, 0:1] * rates_ref[...])
    for q in range(o_ref.shape[0]):
        o_ref[q] = filt[:, q * HY_CH:(q + 1) * HY_CH] * decay


def _hy_filters(n, w1, b1, w2, b2, w3, freq):
    t = jnp.linspace(0.0, 1.0, n, dtype=F32)[:, None]
    bands = jnp.linspace(1e-4, HY_BANDS - 1, HY_BANDS, dtype=F32)
    ang = (2.0 * math.pi / n) * jnp.arange(n, dtype=F32)[:, None] * bands
    z = jnp.concatenate([t, jnp.cos(ang), -jnp.sin(ang)], axis=-1)
    z = jnp.pad(z, ((0, 0), (0, LANES - z.shape[1])))
    w1p = jnp.pad(w1, ((0, LANES - w1.shape[0]), (0, 0)))
    rates = jnp.abs(jnp.linspace(HY_MIN_DECAY, HY_MAX_DECAY, HY_CH, dtype=F32)).reshape(1, HY_CH)
    tm = ROW_TILE
    nq = HY_ORDER * 2
    full = lambda a: pl.BlockSpec(a.shape, lambda i: (0,) * a.ndim)
    args = (z, w1p, b1.reshape(1, -1), w2, b2.reshape(1, -1), w3, freq.reshape(1, -1), rates)
    return pl.pallas_call(
        _hy_filter_kernel,
        grid=(n // tm,),
        in_specs=[pl.BlockSpec((tm, LANES), lambda i: (i, 0))] + [full(a) for a in args[1:]],
        out_specs=pl.BlockSpec((nq, tm, HY_CH), lambda i: (0, i, 0)),
        out_shape=jax.ShapeDtypeStruct((nq, n, HY_CH), F32),
        compiler_params=_params("parallel"),
        name="hy_filters",
    )(*args)


def _dft_in_kernel(x_ref, g_ref, a_ref):
    for j in range(DFT_STEP):
        a_ref[j] = _dot(g_ref[j], x_ref[:, j, :].astype(BF16)).astype(BF16)


def _dft_in(x4, col, g_in):
    bx, rh, r, _ = x4.shape
    c = HY_CH
    return pl.pallas_call(
        _dft_in_kernel,
        grid=(bx, r // DFT_STEP),
        in_specs=[pl.BlockSpec((None, rh, DFT_STEP, c), lambda b, i: (b, 0, i, col)),
                  pl.BlockSpec((DFT_STEP, 2 * r, rh), lambda b, i: (i, 0, 0))],
        out_specs=pl.BlockSpec((None, DFT_STEP, 2 * r, c), lambda b, i: (b, i, 0, 0)),
        out_shape=jax.ShapeDtypeStruct((bx, r, 2 * r, c), BF16),
        compiler_params=_params("parallel", "parallel"),
        name="dft_in",
    )(x4, g_in)


def _stage2(a_ref, b, j, f_ref):
    a = jnp.concatenate([a_ref[b, :, 0, j, :], a_ref[b, :, 1, j, :]], axis=0)
    return _dot(f_ref[...], a)


def _dft_filt_kernel(a_ref, f_ref, k_ref):
    r = f_ref.shape[0] // 2
    for j in range(DFT_STEP):
        sf, sb = _stage2(a_ref, 0, j, f_ref), _stage2(a_ref, 1, j, f_ref)
        k_ref[j, 0:r, :] = sf[0:r] + sb[0:r]
        k_ref[j, r:2 * r, :] = sf[r:2 * r] - sb[r:2 * r]


def _dft_filt(a, f_fwd):
    nq, r, _, c = a.shape
    a5 = a.reshape(nq, r, 2, r, c)
    return pl.pallas_call(
        _dft_filt_kernel,
        grid=(nq // 2, r // DFT_STEP),
        in_specs=[pl.BlockSpec((2, r, 2, DFT_STEP, c), lambda o, i: (o, 0, 0, i, 0)),
                  pl.BlockSpec(f_fwd.shape, lambda o, i: (0, 0))],
        out_specs=pl.BlockSpec((None, DFT_STEP, 2 * r, c), lambda o, i: (o, i, 0, 0)),
        out_shape=jax.ShapeDtypeStruct((nq // 2, r, 2 * r, c), F32),
        compiler_params=_params("parallel", "parallel"),
        name="dft_filt",
    )(a5, f_fwd)


def _dft_mid_kernel(a_ref, k_ref, ff_ref, fi_ref, b_ref):
    r = ff_ref.shape[0] // 2
    for b in range(a_ref.shape[0]):
        for j in range(DFT_STEP):
            s = _stage2(a_ref, b, j, ff_ref)
            sr, si = s[0:r], s[r:2 * r]
            kr, ki = k_ref[j, 0:r, :], k_ref[j, r:2 * r, :]
            p = jnp.concatenate([sr * kr - si * ki, sr * ki + si * kr], axis=0).astype(BF16)
            b_ref[b, j] = _dot(fi_ref[...], p).astype(BF16)


def _dft_mid(a, kspec, order, f_fwd, f_inv):
    bsz, r, _, c = a.shape
    a5 = a.reshape(bsz, r, 2, r, c)
    return pl.pallas_call(
        _dft_mid_kernel,
        grid=(r // DFT_STEP,),
        in_specs=[pl.BlockSpec((bsz, r, 2, DFT_STEP, c), lambda i: (0, 0, 0, i, 0)),
                  pl.BlockSpec((None, DFT_STEP, 2 * r, c), lambda i: (order, i, 0, 0)),
                  pl.BlockSpec(f_fwd.shape, lambda i: (0, 0)),
                  pl.BlockSpec(f_inv.shape, lambda i: (0, 0))],
        out_specs=pl.BlockSpec((bsz, DFT_STEP, 2 * r, c), lambda i: (0, i, 0, 0)),
        out_shape=jax.ShapeDtypeStruct((bsz, r, 2 * r, c), BF16),
        compiler_params=_params("parallel"),
        name="dft_mid",
    )(a5, kspec, f_fwd, f_inv)


def _dft_out_kernel(b_ref, g_ref, u_ref, x_ref, bias_ref, o_ref):
    for j in range(DFT_STEP):
        rhs = jnp.concatenate([b_ref[:, 0, j, :], b_ref[:, 1, j, :]], axis=0)
        y = _dot(g_ref[j], rhs)
        o_ref[:, j, :] = x_ref[:, j, :] * (y + u_ref[:, j, :] * bias_ref[...])


def _dft_out(bm, g_out, u4, u_col, x4, x_col, bias):
    bsz, r, _, c = bm.shape
    rh = r // 2
    b5 = bm.reshape(bsz, r, 2, r, c)
    seq = lambda col: pl.BlockSpec((None, rh, DFT_STEP, c), lambda b, i: (b, 0, i, col))
    return pl.pallas_call(
        _dft_out_kernel,
        grid=(bsz, r // DFT_STEP),
        in_specs=[pl.BlockSpec((None, r, 2, DFT_STEP, c), lambda b, i: (b, 0, 0, i, 0)),
                  pl.BlockSpec((DFT_STEP, rh, 2 * r), lambda b, i: (i, 0, 0)),
                  seq(u_col), seq(x_col), pl.BlockSpec((1, c), lambda b, i: (0, 0))],
        out_specs=seq(0),
        out_shape=jax.ShapeDtypeStruct((bsz, rh, r, c), F32),
        compiler_params=_params("parallel", "parallel"),
        name="dft_out",
    )(b5, g_out, u4, x4, bias.reshape(1, c))


def _hyena(hy_in, fw1, fb1, fw2, fb2, fw3, freq, conv_bias):
    bsz, n, _ = hy_in.shape
    r, g_in, g_out, f_fwd, f_inv = _dft_tables(n)
    filt = _hy_filters(n, fw1, fb1, fw2, fb2, fw3, freq)
    kspec = _dft_filt(_dft_in(filt.reshape(-1, r // 2, r, HY_CH), 0, g_in), f_fwd)
    seq4 = hy_in.reshape(bsz, r // 2, r, 3 * HY_CH)
    zz = _dft_out(_dft_mid(_dft_in(seq4, 0, g_in), kspec, 0, f_fwd, f_inv), g_out,
                  seq4, 0, seq4, 1, conv_bias[0])
    out = _dft_out(_dft_mid(_dft_in(zz, 0, g_in), kspec, 1, f_fwd, f_inv), g_out,
                   zz, 0, seq4, 2, conv_bias[1])
    return out.reshape(bsz, n, HY_CH)


CD_HY, CD_Z, CD_XBC, CD_DT = 0, 1536, 2048, 3072
CD_PAD_COLS = CD_DT + 2 * LANES


def _reorder_ab(w):
    gq, gk, gv, gg, lr_f, lr_b, hq, hf_f, hf_b, hi, hg = jnp.split(
        w, np.cumsum([256, 256, 512, 512, 16, 16, 512, 512, 512, 512, 512])[:-1].tolist(), axis=-1)
    pad = jnp.zeros((w.shape[0], AB_PAD_COLS - AB_LR - 2 * GLA_LOW_RANK), w.dtype)
    return jnp.concatenate([gq, gk, gv, gg, hq, hf_f, hf_b, hi, hg, lr_f, lr_b, pad], axis=-1)


def _reorder_cd(w):
    hy, z, xbc, dt_f, dt_b = jnp.split(w, np.cumsum([1536, 512, 1024, 8, 8])[:-1].tolist(), axis=-1)
    pad = jnp.zeros((w.shape[0], LANES - MB_HEADS), w.dtype)
    return jnp.concatenate([hy, z, xbc, dt_f, pad, dt_b, pad], axis=-1)


def kernel(x, c, ctx, c_ctx, ada_w, ada_b, norm_mix_g, norm_ffn_g, norm_out_g, ab_w_in, ab_w_out, gla_gate_w, gla_gate_b, gla_norm_g, hg_lb, hg_norm_g, cd_w_in, cd_w_out, hy_short_w, hy_short_b, hy_w1, hy_b1, hy_w2, hy_b2, hy_w3, hy_freq, hy_bias, mb_conv_w, mb_conv_b, mb_dt_bias, mb_a_log, mb_d, mb_norm_g, router_w, router_b, moe_w_gate, moe_w_up, moe_w_down):
    bsz, n_lat, d = x.shape
    n_ctx = ctx.shape[1]
    t = n_ctx + n_lat
    assert ada_w.shape[0] == 2 and ab_w_in.shape[0] == 1 and cd_w_in.shape[0] == 1

    cond = jnp.zeros((8, d), F32).at[:bsz].set(c).at[bsz].set(c_ctx)
    m = _adaln(cond, ada_w, ada_b)

    def mods(layer):
        lat = m[layer, :bsz].reshape(bsz, 6, d)
        cx = jnp.broadcast_to(m[layer, bsz].reshape(1, 6, d), (bsz, 6, d))
        both = jnp.stack([cx, lat], axis=1)
        return [both[:, :, j][:, :, None, :] for j in range(6)]

    lb_all = jnp.cumsum(jax.nn.softmax(hg_lb.astype(F32), axis=1), axis=1)
    router_w_pad = jnp.zeros((d, LANES), F32).at[:, :N_EXPERTS].set(router_w)
    h = jnp.concatenate([ctx, x], axis=1)

    sh_m, sc_m, gt_m, sh_f, sc_f, gt_f = mods(0)
    proj = _norm_proj(h, norm_mix_g[0], sh_m, sc_m, _reorder_ab(ab_w_in[0]).astype(BF16), n_ctx)
    o_gla, o_hg = [], []
    for dd in range(2):
        gwp = jnp.zeros((LANES, GLA_KEY_W), F32).at[
            GLA_LOW_RANK * dd:GLA_LOW_RANK * (dd + 1)].set(gla_gate_w[0, dd])
        o_gla.append(_gla_scan(proj, gwp, gla_gate_b[0, dd].reshape(1, -1), n_ctx, dd == 1))
        o_hg.append(_hgrn_scan(proj, lb_all[dd, 0].reshape(1, -1), n_ctx, dd == 1))
    h = _mix_out_ab(h, o_gla, o_hg, proj, gla_norm_g[0], hg_norm_g[0], ab_w_out[0].astype(BF16), gt_m, n_ctx)
    h = _moe(h, norm_ffn_g[0], sh_f, sc_f, gt_f, router_w_pad, router_b,
             moe_w_gate[0], moe_w_up[0], moe_w_down[0], n_ctx)

    sh_m, sc_m, gt_m, sh_f, sc_f, gt_f = mods(1)
    proj = _norm_proj(h, norm_mix_g[1], sh_m, sc_m, _reorder_cd(cd_w_in[0]).astype(BF16), n_ctx)
    hy_in = _dwconv(proj, CD_HY, 3 * HY_CH, hy_short_w[0], hy_short_b[0], (n_ctx, t), act=False)
    hy = _hyena(hy_in, hy_w1[0], hy_b1[0], hy_w2[0], hy_b2[0], hy_w3[0], hy_freq[0], hy_bias[0])
    xbc = _dwconv(proj, CD_XBC, MB_INNER + 2 * MB_BC_W, mb_conv_w[0], mb_conv_b[0], (0, n_ctx, t), act=True)
    y_ssd = [_ssd_scan(xbc, proj, CD_DT // LANES + dd, mb_dt_bias[0, dd], mb_a_log[0, dd], n_ctx, dd == 1)
             for dd in range(2)]
    d_skip_x = jnp.repeat(mb_d[0], MB_HEAD_DIM).reshape(1, MB_INNER)
    h = _mix_out_cd(h, hy, y_ssd, xbc, proj, CD_Z // MB_INNER, d_skip_x, mb_norm_g[0],
                    cd_w_out[0].astype(BF16), gt_m, n_ctx)
    return _moe(h, norm_ffn_g[1], sh_f, sc_f, gt_f, router_w_pad, router_b,
                moe_w_gate[1], moe_w_up[1], moe_w_down[1], 0, final_g=norm_out_g)
```

```python
import functools
import math

import numpy as np
import jax
import jax.numpy as jnp
from jax import lax
from jax.experimental import pallas as pl
from jax.experimental.pallas import tpu as pltpu

NORM_EPS = 1e-6
GLA_HEADS, GLA_DK, GLA_DV, GLA_LOW_RANK, GLA_TAU = 4, 64, 128, 16, 16.0
GLA_KEY_W, GLA_VAL_W = GLA_HEADS * GLA_DK, GLA_HEADS * GLA_DV
HG_HEADS, HG_EXPAND, HG_DV = 4, 128, 128
HG_KEY_W, HG_VAL_W = HG_HEADS * HG_EXPAND, HG_HEADS * HG_DV
HY_CH, HY_ORDER, HY_SHORT, HY_BANDS, HY_FILT_HID = 512, 2, 3, 16, 64
HY_MIN_DECAY = math.log(1e-2) / 1.5
HY_MAX_DECAY = math.log(1e-2) / 0.3
MB_HEADS, MB_HEAD_DIM, MB_GROUPS, MB_STATE = 8, 64, 2, 128
MB_INNER = MB_HEADS * MB_HEAD_DIM
MB_BC_W = MB_GROUPS * MB_STATE
N_EXPERTS, N_GROUPS, TOP_K, MOE_BLOCK = 16, 4, 2, 256
EXPERTS_PER_GROUP = N_EXPERTS // N_GROUPS

LANES = 128
SCAN_CHUNK = 64
SCAN_BLOCK = 128
SSD_CHUNK = 128
ROW_TILE = 256
VMEM_LIMIT = 56 * 1024 * 1024

BF16 = jnp.bfloat16
F32 = jnp.float32


def _params(*sem):
    return pltpu.CompilerParams(dimension_semantics=sem, vmem_limit_bytes=VMEM_LIMIT)


def _split3(x):
    hi = x.astype(BF16)
    r1 = x - hi.astype(F32)
    mid = r1.astype(BF16)
    lo = (r1 - mid.astype(F32)).astype(BF16)
    return hi, mid, lo


def _dot(a, b):
    return jnp.dot(a, b, preferred_element_type=F32)


def _dot_nt(a, b):
    return lax.dot_general(a, b, (((1,), (1,)), ((), ())), preferred_element_type=F32)


def _dot_tn(a, b):
    return lax.dot_general(a, b, (((0,), (0,)), ((), ())), preferred_element_type=F32)


def _sel_dot(m01, x):
    hi, mid, lo = _split3(x)
    return _dot(m01, hi) + (_dot(m01, mid) + _dot(m01, lo))


def _dot_sel(x, m01):
    hi, mid, lo = _split3(x)
    return _dot(hi, m01) + (_dot(mid, m01) + _dot(lo, m01))


def _dot_f32(a, b):
    ah = a.astype(BF16)
    al = (a - ah.astype(F32)).astype(BF16)
    bh = b.astype(BF16)
    bl = (b - bh.astype(F32)).astype(BF16)
    return _dot(ah, bh) + (_dot(ah, bl) + _dot(al, bh))


def _silu(x):
    return x * (1.0 / (1.0 + jnp.exp(-x)))


def _sigmoid(x):
    return 1.0 / (1.0 + jnp.exp(-x))


def _softplus(x):
    return jnp.maximum(x, 0.0) + jnp.log(1.0 + jnp.exp(-jnp.abs(x)))


def _rms(x, g):
    return x * lax.rsqrt(jnp.mean(x * x, axis=-1, keepdims=True) + NORM_EPS) * g


def _adaln_kernel(c_ref, w_ref, b_ref, o_ref):
    o_ref[...] = _dot_f32(_silu(c_ref[...]), w_ref[...]) + b_ref[...]


def _adaln(cond, w, b):
    n_l, d, n6 = w.shape
    tn = 1536
    return pl.pallas_call(
        _adaln_kernel,
        grid=(n_l, n6 // tn),
        in_specs=[pl.BlockSpec((8, d), lambda l, j: (0, 0)),
                  pl.BlockSpec((None, d, tn), lambda l, j: (l, 0, j)),
                  pl.BlockSpec((None, 1, tn), lambda l, j: (l, 0, j))],
        out_specs=pl.BlockSpec((None, 8, tn), lambda l, j: (l, 0, j)),
        out_shape=jax.ShapeDtypeStruct((n_l, 8, n6), F32),
        compiler_params=_params("parallel", "parallel"),
        name="adaln",
    )(cond, w, b.reshape(n_l, 1, n6))


def _norm_proj_kernel(h_ref, g_ref, sh_ref, sc_ref, w_ref, o_ref):
    u = _rms(h_ref[...], g_ref[...]) * (1.0 + sc_ref[...]) + sh_ref[...]
    o_ref[...] = _dot(u.astype(BF16), w_ref[...])


def _norm_proj(h, g, shift, scale, w, n_ctx):
    bsz, t, d = h.shape
    n = w.shape[1]
    tm = ROW_TILE
    seg = lambda b, i: (b, (i * tm >= n_ctx).astype(jnp.int32), 0, 0)
    return pl.pallas_call(
        _norm_proj_kernel,
        grid=(bsz, t // tm),
        in_specs=[pl.BlockSpec((None, tm, d), lambda b, i: (b, i, 0)),
                  pl.BlockSpec((1, d), lambda b, i: (0, 0)),
                  pl.BlockSpec((None, None, 1, d), seg),
                  pl.BlockSpec((None, None, 1, d), seg),
                  pl.BlockSpec((d, n), lambda b, i: (0, 0))],
        out_specs=pl.BlockSpec((None, tm, n), lambda b, i: (b, i, 0)),
        out_shape=jax.ShapeDtypeStruct((bsz, t, n), F32),
        compiler_params=_params("parallel", "parallel"),
        name="norm_proj",
    )(h, g.reshape(1, d), shift, scale, w)


def _scan_constants(c, reverse):
    t = np.arange(c)[:, None]
    u = np.arange(c)[None, :]
    sels = [u <= t, u > t]
    masks = []
    m = c // 2
    while m >= 1:
        blk = t // (2 * m)
        upper_t = (t % (2 * m)) >= m
        r = blk * (2 * m) + m - 1
        s_blk = u // (2 * m)
        upper_s = (u % (2 * m)) >= m
        sels.append((upper_t & (u > r) & (u <= t)) | ((~upper_t) & (u > t) & (u <= r)))
        masks.append((blk == s_blk) & upper_t & (~upper_s))
        m //= 2
    masks.append(t == u)
    sel = np.stack(sels).astype(np.float32)
    msk = np.stack(masks).astype(np.float32)
    if reverse:
        sel = sel[:, ::-1, ::-1]
        msk = msk[:, ::-1, ::-1]
    return np.ascontiguousarray(sel.reshape(-1, c)), np.ascontiguousarray(msk)


def _chunk_order(i, n_ctx_chunks, n_chunks, reverse):
    if not reverse:
        return i
    return jnp.where(i < n_ctx_chunks, n_ctx_chunks - 1 - i, n_chunks - 1 - (i - n_ctx_chunks))


GROUP_KEYS = 256


def _decay_chunk(q, k, v, la, sel_ref, mask_ref, hm_ref, st_ref, heads, dk, dv):
    c = q.shape[0]
    n_lvl = mask_ref.shape[0] - 1
    hpg = GROUP_KEYS // dk
    cs = _dot(sel_ref[...], jnp.concatenate(_split3(la), axis=0))
    e_q = jnp.exp(cs[0:c])
    e_k = jnp.exp(cs[c:2 * c])
    e_tot = jnp.exp(jnp.sum(la, axis=0, keepdims=True))
    vb = v.astype(BF16)
    outs = []
    for g in range(heads // hpg):
        ks = slice(g * GROUP_KEYS, (g + 1) * GROUP_KEYS)
        vs = slice(g * hpg * dv, (g + 1) * hpg * dv)
        qg, kg = q[:, ks], k[:, ks]
        stack = lambda x: jnp.concatenate([x * hm_ref[h] for h in range(hpg)], axis=0).astype(BF16)
        att = mask_ref[n_lvl] * _dot_nt(stack(qg), kg.astype(BF16))
        for l in range(n_lvl):
            e = jnp.exp(cs[(2 + l) * c:(3 + l) * c, ks])
            att = att + mask_ref[l] * _dot_nt(stack(qg * e), (kg * e).astype(BF16))
        att = att.astype(BF16)
        st = st_ref[g]
        inter = _dot_nt(stack(qg * e_q[:, ks]), st.astype(BF16))
        upd = _dot_tn(vb[:, vs], (kg * e_k[:, ks]).astype(BF16))
        new = st * e_tot[:, ks]
        for h in range(hpg):
            rows = slice(h * c, (h + 1) * c)
            hv = slice(g * hpg * dv + h * dv, g * hpg * dv + (h + 1) * dv)
            outs.append(_dot(att[rows], vb[:, hv]) + inter[rows])
            new = new + upd[h * dv:(h + 1) * dv] * hm_ref[h]
        st_ref[g] = new
    return jnp.concatenate(outs, axis=-1)


def _log_sigmoid(x):
    return jnp.minimum(x, 0.0) - jnp.log(1.0 + jnp.exp(-jnp.abs(x)))


def _gla_kernel(*refs):
    ins, hm_ref, (o_refs, st_refs) = (refs[0:8], refs[8:16]), refs[16], (refs[17:19], refs[19:21])

    @pl.when(pl.program_id(1) == 0)
    def _():
        for st_ref in st_refs:
            st_ref[...] = jnp.zeros_like(st_ref)

    for d, ((q_ref, k_ref, v_ref, lr_ref, gw_ref, gb_ref, sel_ref, mask_ref), o_ref, st_ref) in enumerate(
            zip(ins, o_refs, st_refs)):
        z = _dot_f32(lr_ref[...], gw_ref[...]) + gb_ref[...]
        la = _log_sigmoid(z) * (1.0 / GLA_TAU)
        q = q_ref[...] * (GLA_DK ** -0.5)
        k, v = k_ref[...], v_ref[...]
        for rows in _sub_chunks(q.shape[0], d == 1):
            o_ref[rows, :] = _decay_chunk(q[rows], k[rows], v[rows], la[rows], sel_ref, mask_ref, hm_ref,
                                          st_ref, GLA_HEADS, GLA_DK, GLA_DV)


def _hgrn_kernel(*refs):
    ins, hm_ref, (o_refs, st_refs) = (refs[0:6], refs[6:12]), refs[12], (refs[13:15], refs[15:17])

    @pl.when(pl.program_id(1) == 0)
    def _():
        for st_ref in st_refs:
            st_ref[...] = jnp.zeros_like(st_ref)

    for d, ((q_ref, f_ref, v_ref, lb_ref, sel_ref, mask_ref), o_ref, st_ref) in enumerate(
            zip(ins, o_refs, st_refs)):
        lb = lb_ref[...]
        f = lb + (1.0 - lb) * _sigmoid(f_ref[...])
        q, k, v, la = _silu(q_ref[...]), 1.0 - f, v_ref[...], jnp.log(f)
        for rows in _sub_chunks(q.shape[0], d == 1):
            o_ref[rows, :] = _decay_chunk(q[rows], k[rows], v[rows], la[rows], sel_ref, mask_ref, hm_ref,
                                          st_ref, HG_HEADS, HG_EXPAND, HG_DV)


def _sub_chunks(rows, reverse):
    order = range(rows // SCAN_CHUNK)
    return [slice(j * SCAN_CHUNK, (j + 1) * SCAN_CHUNK) for j in (reversed(order) if reverse else order)]


def _scan_specs(blk, n_ctx, t, reverse, chunk=None, stacked_heads=1):
    n_blocks = t // blk
    order = functools.partial(_chunk_order, n_ctx_chunks=n_ctx // blk, n_chunks=n_blocks, reverse=reverse)

    def col(width, idx):
        return pl.BlockSpec((None, blk, width), lambda b, i: (b, order(i), idx))

    sel, msk = _scan_constants(chunk or blk, reverse)
    sel3 = np.concatenate([sel, sel, sel], axis=1)
    msk = np.tile(msk, (1, stacked_heads, 1))
    const = lambda a: pl.BlockSpec(a.shape, lambda b, i: (0,) * a.ndim)
    return n_blocks, col, const, jnp.asarray(sel3, BF16), jnp.asarray(msk, F32)


def _head_masks(dk):
    hpg = GROUP_KEYS // dk
    hm = np.zeros((hpg, 1, GROUP_KEYS), np.float32)
    for h in range(hpg):
        hm[h, 0, h * dk:(h + 1) * dk] = 1.0
    return jnp.asarray(hm)


AB_Q, AB_K, AB_V, AB_G = 0, 256, 512, 1024
AB_HQ, AB_HF, AB_HI, AB_HG, AB_LR = 1536, 2048, 3072, 3584, 4096
AB_PAD_COLS = 4224


def _gla_scan(proj, gate_w_pad, gate_b, n_ctx):
    bsz, t, _ = proj.shape
    hpg = GROUP_KEYS // GLA_DK
    in_specs, args, outs = [], [], []
    for d in range(2):
        n_blocks, col, const, sel, msk = _scan_specs(SCAN_BLOCK, n_ctx, t, d == 1, SCAN_CHUNK, hpg)
        in_specs += [col(GLA_KEY_W, AB_Q // GLA_KEY_W), col(GLA_KEY_W, AB_K // GLA_KEY_W),
                     col(GLA_VAL_W, AB_V // GLA_VAL_W), col(LANES, AB_LR // LANES),
                     const(gate_w_pad[d]), const(gate_b[d]), const(sel), const(msk)]
        args += [proj, proj, proj, proj, gate_w_pad[d], gate_b[d], sel, msk]
        outs.append(col(GLA_VAL_W, 0))
    hm = _head_masks(GLA_DK)
    return pl.pallas_call(
        _gla_kernel,
        grid=(bsz, n_blocks),
        in_specs=in_specs + [const(hm)],
        out_specs=outs,
        out_shape=[jax.ShapeDtypeStruct((bsz, t, GLA_VAL_W), F32)] * 2,
        scratch_shapes=[pltpu.VMEM((GLA_HEADS // hpg, GLA_DV, GROUP_KEYS), F32)] * 2,
        compiler_params=_params("parallel", "arbitrary"),
        name="gla_scan",
    )(*args, hm)


def _hgrn_scan(proj, lb, n_ctx):
    bsz, t, _ = proj.shape
    hpg = GROUP_KEYS // HG_EXPAND
    in_specs, args, outs = [], [], []
    for d in range(2):
        n_blocks, col, const, sel, msk = _scan_specs(SCAN_BLOCK, n_ctx, t, d == 1, SCAN_CHUNK, hpg)
        in_specs += [col(HG_KEY_W, AB_HQ // HG_KEY_W), col(HG_KEY_W, AB_HF // HG_KEY_W + d),
                     col(HG_VAL_W, AB_HI // HG_VAL_W), const(lb[d]), const(sel), const(msk)]
        args += [proj, proj, proj, lb[d], sel, msk]
        outs.append(col(HG_VAL_W, 0))
    hm = _head_masks(HG_EXPAND)
    return pl.pallas_call(
        _hgrn_kernel,
        grid=(bsz, n_blocks),
        in_specs=in_specs + [const(hm)],
        out_specs=outs,
        out_shape=[jax.ShapeDtypeStruct((bsz, t, HG_VAL_W), F32)] * 2,
        scratch_shapes=[pltpu.VMEM((HG_HEADS // hpg, HG_DV, GROUP_KEYS), F32)] * 2,
        compiler_params=_params("parallel", "arbitrary"),
        name="hgrn_scan",
    )(*args, hm)


def _mix_out_ab_kernel(h_ref, gf_ref, gb_ref, hf_ref, hb_ref, gg_ref, hg_ref, gn_ref, hn_ref,
                       w_ref, gt_ref, o_ref):
    feats = []
    for o, gate, g in ((gf_ref[...] + gb_ref[...], gg_ref[...], gn_ref[...]),
                       (hf_ref[...] + hb_ref[...], hg_ref[...], hn_ref[...])):
        for hd in range(o.shape[-1] // LANES):
            s = slice(hd * LANES, (hd + 1) * LANES)
            feats.append(_rms(o[:, s], g) * _silu(gate[:, s]))
    feat = jnp.concatenate(feats, axis=-1).astype(BF16)
    o_ref[...] = h_ref[...] + gt_ref[...] * _dot(feat, w_ref[...])


def _mix_out_ab(h, o_gla, o_hg, proj, gla_norm_g, hg_norm_g, w_out, gate, n_ctx):
    bsz, t, d = h.shape
    tm = ROW_TILE
    seg = lambda b, i: (b, (i * tm >= n_ctx).astype(jnp.int32), 0, 0)
    row = lambda width, idx: pl.BlockSpec((None, tm, width), lambda b, i: (b, i, idx))
    vec = pl.BlockSpec((1, LANES), lambda b, i: (0, 0))
    return pl.pallas_call(
        _mix_out_ab_kernel,
        grid=(bsz, t // tm),
        in_specs=[row(d, 0), row(GLA_VAL_W, 0), row(GLA_VAL_W, 0), row(HG_VAL_W, 0), row(HG_VAL_W, 0),
                  row(GLA_VAL_W, AB_G // GLA_VAL_W), row(HG_VAL_W, AB_HG // HG_VAL_W), vec, vec,
                  pl.BlockSpec(w_out.shape, lambda b, i: (0, 0)),
                  pl.BlockSpec((None, None, 1, d), seg)],
        out_specs=row(d, 0),
        out_shape=jax.ShapeDtypeStruct((bsz, t, d), F32),
        compiler_params=_params("parallel", "parallel"),
        name="mix_out_ab",
    )(h, o_gla[0], o_gla[1], o_hg[0], o_hg[1], proj, proj, gla_norm_g.reshape(1, -1),
      hg_norm_g.reshape(1, -1), w_out, gate)


def _dwconv_kernel(x_ref, prev_ref, next_ref, w_ref, b_ref, o_ref, *, seg_tiles, act):
    i = pl.program_id(1)
    tm = x_ref.shape[0]
    x = x_ref[...]
    first = functools.reduce(jnp.logical_or, [i == s for s in seg_tiles[:-1]])
    last = functools.reduce(jnp.logical_or, [i == s - 1 for s in seg_tiles[1:]])
    prev_row = jnp.where(first, 0.0, prev_ref[7:8, :])
    next_row = jnp.where(last, 0.0, next_ref[0:1, :])
    rows = lax.broadcasted_iota(jnp.int32, x.shape, 0)
    x_prev = jnp.where(rows == 0, prev_row, pltpu.roll(x, 1, axis=0))
    x_next = jnp.where(rows == tm - 1, next_row, pltpu.roll(x, tm - 1, axis=0))
    y = x_prev * w_ref[0:1, :] + x * w_ref[1:2, :] + x_next * w_ref[2:3, :] + b_ref[...]
    o_ref[...] = _silu(y) if act else y


def _dwconv(proj, col0, width, w, b, seg_bounds, act):
    bsz, t, _ = proj.shape
    tm = ROW_TILE
    off = seg_bounds[0] // tm
    n_tiles = t // tm - off
    seg_tiles = tuple(s // tm - off for s in seg_bounds)
    cb = col0 // width
    r8 = tm // 8
    last8 = t // 8 - 1
    kern = functools.partial(_dwconv_kernel, seg_tiles=seg_tiles, act=act)
    return pl.pallas_call(
        kern,
        grid=(bsz, n_tiles),
        in_specs=[pl.BlockSpec((None, tm, width), lambda bb, i: (bb, i + off, cb)),
                  pl.BlockSpec((None, 8, width), lambda bb, i: (bb, jnp.maximum((i + off) * r8 - 1, 0), cb)),
                  pl.BlockSpec((None, 8, width),
                               lambda bb, i: (bb, jnp.minimum((i + off + 1) * r8, last8), cb)),
                  pl.BlockSpec((3, width), lambda bb, i: (0, 0)),
                  pl.BlockSpec((1, width), lambda bb, i: (0, 0))],
        out_specs=pl.BlockSpec((None, tm, width), lambda bb, i: (bb, i, 0)),
        out_shape=jax.ShapeDtypeStruct((bsz, n_tiles * tm, width), F32),
        compiler_params=_params("parallel", "parallel"),
        name="dwconv",
    )(proj, proj, proj, w.T, b.reshape(1, -1))


def _ssd_kernel(xbc_ref, dt_ref, bias_ref, alog_ref, hexp_ref, mq_ref, mk_ref, mask_ref, o_ref, st_ref):
    @pl.when(pl.program_id(1) == 0)
    def _():
        st_ref[...] = jnp.zeros_like(st_ref)

    c = xbc_ref.shape[0]
    hpg = MB_HEADS // MB_GROUPS
    gw = hpg * MB_HEAD_DIM
    dt = _softplus(dt_ref[...] + bias_ref[...])
    la = -dt * jnp.exp(alog_ref[...])
    cq = _sel_dot(mq_ref[...], la)
    ck = _sel_dot(mk_ref[...], la)
    cq_t = _dot_nt_sel(la, mq_ref[...])
    hexp = hexp_ref[...]
    dt_x = _dot_sel(dt, hexp)
    eq_x = jnp.exp(_dot_sel(cq, hexp))
    ek_x = jnp.exp(_dot_sel(ck, hexp))
    etot_x = jnp.exp(_dot_sel(jnp.sum(la, axis=0, keepdims=True), hexp))
    xs = xbc_ref[:, 0:MB_INNER] * dt_x
    mask = mask_ref[...]
    outs = []
    for g in range(MB_GROUPS):
        bm = xbc_ref[:, MB_INNER + g * MB_STATE:MB_INNER + (g + 1) * MB_STATE].astype(BF16)
        cm = xbc_ref[:, MB_INNER + MB_BC_W + g * MB_STATE:MB_INNER + MB_BC_W + (g + 1) * MB_STATE].astype(BF16)
        cb = _dot_nt(cm, bm)
        st = st_ref[g]
        gs = slice(g * gw, (g + 1) * gw)
        y_inter = _dot(cm, st.astype(BF16)) * eq_x[:, gs]
        for r in range(hpg):
            hd = g * hpg + r
            diff = cq[:, hd:hd + 1] - cq_t[hd:hd + 1, :]
            w = cb * jnp.exp(jnp.where(mask > 0.0, diff, -jnp.inf))
            ps = slice(hd * MB_HEAD_DIM, (hd + 1) * MB_HEAD_DIM)
            outs.append(_dot(w.astype(BF16), xs[:, ps].astype(BF16))
                        + y_inter[:, r * MB_HEAD_DIM:(r + 1) * MB_HEAD_DIM])
        st_ref[g] = st * etot_x[:, gs] + _dot_tn(bm, (xs[:, gs] * ek_x[:, gs]).astype(BF16))
    o_ref[...] = jnp.concatenate(outs, axis=-1)


def _dot_nt_sel(x, m01):
    hi, mid, lo = _split3(x)
    f = lambda p: lax.dot_general(p, m01, (((0,), (1,)), ((), ())), preferred_element_type=F32)
    return f(hi) + (f(mid) + f(lo))


def _ssd_scan(xbc, proj, dt_col, dt_bias, a_log, n_ctx, reverse):
    bsz, t, _ = xbc.shape
    c = SSD_CHUNK
    n_chunks, col, const, sel, msk = _scan_specs(c, n_ctx, t, reverse)
    mq, mk = sel[0:c, 0:c], sel[c:2 * c, 0:c]
    tri = np.tril(np.ones((c, c), np.float32))
    mask = jnp.asarray(tri[::-1, ::-1].copy() if reverse else tri)
    pad = lambda v: jnp.zeros((1, LANES), F32).at[0, :MB_HEADS].set(v)
    hexp = np.zeros((LANES, MB_INNER), np.float32)
    for hd in range(MB_HEADS):
        hexp[hd, hd * MB_HEAD_DIM:(hd + 1) * MB_HEAD_DIM] = 1.0
    hexp = jnp.asarray(hexp, BF16)
    bias = pad(dt_bias)
    alog = pad(a_log.astype(F32))
    return pl.pallas_call(
        _ssd_kernel,
        grid=(bsz, n_chunks),
        in_specs=[col(xbc.shape[-1], 0), col(LANES, dt_col), const(bias), const(alog), const(hexp),
                  const(mq), const(mk), const(mask)],
        out_specs=col(MB_INNER, 0),
        out_shape=jax.ShapeDtypeStruct((bsz, t, MB_INNER), F32),
        scratch_shapes=[pltpu.VMEM((MB_GROUPS, MB_STATE, MB_INNER // MB_GROUPS), F32)],
        compiler_params=_params("parallel", "arbitrary"),
        name="ssd_scan_rev" if reverse else "ssd_scan_fwd",
    )(xbc, proj, bias, alog, hexp, mq, mk, mask)


def _mix_out_cd_kernel(h_ref, hy_ref, yf_ref, yb_ref, xs_ref, z_ref, dsk_ref, ng_ref, w_ref, gt_ref, o_ref):
    y = (yf_ref[...] + yb_ref[...] + dsk_ref[...] * xs_ref[...]) * _silu(z_ref[...])
    gw = MB_INNER // MB_GROUPS
    ys = [_rms(y[:, g * gw:(g + 1) * gw], ng_ref[:, g * gw:(g + 1) * gw]) for g in range(MB_GROUPS)]
    feat = jnp.concatenate([hy_ref[...]] + ys, axis=-1).astype(BF16)
    o_ref[...] = h_ref[...] + gt_ref[...] * _dot(feat, w_ref[...])


def _mix_out_cd(h, hy, y_ssd, xbc, proj, z_col, d_skip_x, norm_g, w_out, gate, n_ctx):
    bsz, t, d = h.shape
    tm = ROW_TILE
    n_lat = t - n_ctx
    off = n_ctx // tm
    row = lambda width, idx: pl.BlockSpec((None, tm, width), lambda b, i: (b, i + off, idx))
    vec = pl.BlockSpec((1, MB_INNER), lambda b, i: (0, 0))
    return pl.pallas_call(
        _mix_out_cd_kernel,
        grid=(bsz, n_lat // tm),
        in_specs=[row(d, 0), pl.BlockSpec((None, tm, HY_CH), lambda b, i: (b, i, 0)),
                  row(MB_INNER, 0), row(MB_INNER, 0), row(MB_INNER, 0), row(MB_INNER, z_col), vec, vec,
                  pl.BlockSpec(w_out.shape, lambda b, i: (0, 0)),
                  pl.BlockSpec((None, None, 1, d), lambda b, i: (b, 1, 0, 0))],
        out_specs=pl.BlockSpec((None, tm, d), lambda b, i: (b, i, 0)),
        out_shape=jax.ShapeDtypeStruct((bsz, n_lat, d), F32),
        compiler_params=_params("parallel", "parallel"),
        name="mix_out_cd",
    )(h, hy, y_ssd[0], y_ssd[1], xbc, proj, d_skip_x, norm_g.reshape(1, -1), w_out, gate)


def _top2_of4(a, b, c, d):
    hi1, lo1, hi2, lo2 = jnp.maximum(a, b), jnp.minimum(a, b), jnp.maximum(c, d), jnp.minimum(c, d)
    return jnp.maximum(hi1, hi2) + jnp.maximum(jnp.minimum(hi1, hi2), jnp.maximum(lo1, lo2))


def _first_argmax(vals, skip=None):
    idx = None
    for j, vj in enumerate(vals):
        if idx is None and skip is None:
            idx, best = jnp.zeros(vj.shape, jnp.int32), vj
            continue
        if idx is None:
            idx, best = jnp.full(vj.shape, -1, jnp.int32), jnp.full(vj.shape, -jnp.inf, F32)
        take = vj > best
        if skip is not None:
            take = jnp.logical_and(take, skip != j)
        idx = jnp.where(take, j, idx)
        best = jnp.where(take, vj, best)
    return idx, best


def _ffn_pre_kernel(h_ref, g_ref, sh_ref, sc_ref, rw_ref, rb_ref, tri_ref,
                    v_ref, ri_ref, rwt_ref, cnt_ref, carry_ref):
    @pl.when(jnp.logical_and(pl.program_id(0) == 0, pl.program_id(1) == 0))
    def _():
        carry_ref[...] = jnp.zeros_like(carry_ref)

    v = _rms(h_ref[...], g_ref[...]) * (1.0 + sc_ref[...]) + sh_ref[...]
    v_ref[...] = v.astype(BF16)
    st = _sigmoid(_dot_f32(v, rw_ref[...])).T[0:N_EXPERTS]
    sel = st + rb_ref[...]
    row = lambda a, e: a[e:e + 1]
    epg = EXPERTS_PER_GROUP
    gscore = [_top2_of4(*[row(sel, g * epg + j) for j in range(epg)]) for g in range(N_GROUPS)]
    best, _ = _first_argmax(gscore)

    def in_best(a, j):
        out = row(a, j)
        for g in range(1, N_GROUPS):
            out = jnp.where(best == g, row(a, g * epg + j), out)
        return out

    vals = [in_best(sel, j) for j in range(epg)]
    raw = [in_best(st, j) for j in range(epg)]
    i1, _ = _first_argmax(vals)
    i2, _ = _first_argmax(vals, skip=i1)
    pick = lambda i: functools.reduce(lambda acc, j: jnp.where(i == j, raw[j], acc), range(1, epg), raw[0])
    w1, w2 = pick(i1), pick(i2)
    wsum = w1 + w2
    e1, e2 = best * epg + i1, best * epg + i2

    experts = lax.broadcasted_iota(jnp.int32, st.shape, 0)
    oh1 = (experts == e1).astype(F32)
    oh2 = (experts == e2).astype(F32)
    cnt = oh1 + oh2
    before = _dot(cnt.astype(BF16), tri_ref[...]) + carry_ref[:, 0:1]
    ri_ref[0:1, :] = e1
    ri_ref[1:2, :] = e2
    ri_ref[2:3, :] = jnp.sum(oh1 * before, axis=0, keepdims=True).astype(jnp.int32)
    ri_ref[3:4, :] = jnp.sum(oh2 * before, axis=0, keepdims=True).astype(jnp.int32)
    ri_ref[4:8, :] = jnp.zeros((4, st.shape[1]), jnp.int32)
    rwt_ref[0:1, :] = w1 / wsum
    rwt_ref[1:2, :] = w2 / wsum
    rwt_ref[2:8, :] = jnp.zeros((6, st.shape[1]), F32)
    carry_ref[...] = carry_ref[...] + jnp.sum(cnt, axis=1, keepdims=True)
    cnt_ref[...] = carry_ref[...]


def _ffn_pre(h, g, shift, scale, router_w_pad, router_b, n_ctx):
    bsz, t, d = h.shape
    tm = ROW_TILE
    seg = lambda b, i: (b, (i * tm >= n_ctx).astype(jnp.int32), 0, 0)
    tri = jnp.asarray(np.triu(np.ones((tm, tm), np.float32), 1), BF16)
    return pl.pallas_call(
        _ffn_pre_kernel,
        grid=(bsz, t // tm),
        in_specs=[pl.BlockSpec((None, tm, d), lambda b, i: (b, i, 0)),
                  pl.BlockSpec((1, d), lambda b, i: (0, 0)),
                  pl.BlockSpec((None, None, 1, d), seg),
                  pl.BlockSpec((None, None, 1, d), seg),
                  pl.BlockSpec((d, LANES), lambda b, i: (0, 0)),
                  pl.BlockSpec((N_EXPERTS, 1), lambda b, i: (0, 0)),
                  pl.BlockSpec((tm, tm), lambda b, i: (0, 0))],
        out_specs=[pl.BlockSpec((None, tm, d), lambda b, i: (b, i, 0)),
                   pl.BlockSpec((None, 8, tm), lambda b, i: (b, 0, i)),
                   pl.BlockSpec((None, 8, tm), lambda b, i: (b, 0, i)),
                   pl.BlockSpec((N_EXPERTS, LANES), lambda b, i: (0, 0))],
        out_shape=[jax.ShapeDtypeStruct((bsz, t, d), BF16),
                   jax.ShapeDtypeStruct((bsz, 8, t), jnp.int32),
                   jax.ShapeDtypeStruct((bsz, 8, t), F32),
                   jax.ShapeDtypeStruct((N_EXPERTS, LANES), F32)],
        scratch_shapes=[pltpu.VMEM((N_EXPERTS, LANES), F32)],
        compiler_params=_params("arbitrary", "arbitrary"),
        name="ffn_pre",
    )(h, g.reshape(1, d), shift, scale, router_w_pad, router_b.reshape(N_EXPERTS, 1), tri)


def _experts_kernel(be_ref, nb_ref, x_ref, wg_ref, wu_ref, wd_ref, o_ref, wg_s, wu_s, wd_s):
    i = pl.program_id(0)
    prev = be_ref[jnp.maximum(i - 1, 0)]
    changed = jnp.logical_or(i == 0, be_ref[i] != prev)

    @pl.when(changed)
    def _():
        wg_s[...] = wg_ref[...].astype(BF16)
        wu_s[...] = wu_ref[...].astype(BF16)
        wd_s[...] = wd_ref[...].astype(BF16)

    @pl.when(i < nb_ref[0])
    def _():
        x = x_ref[...]
        hid = _silu(_dot(x, wg_s[...])) * _dot(x, wu_s[...])
        o_ref[...] = _dot(hid.astype(BF16), wd_s[...])

    @pl.when(i >= nb_ref[0])
    def _():
        o_ref[...] = jnp.zeros_like(o_ref)


def _experts(xb, block_e, n_used, w_gate, w_up, w_down):
    n_slots, d = xb.shape
    n_blocks = n_slots // MOE_BLOCK
    de = w_gate.shape[-1]
    wspec = lambda shape: pl.BlockSpec((None,) + shape, lambda i, be, nb: (be[i], 0, 0))
    return pl.pallas_call(
        _experts_kernel,
        grid_spec=pltpu.PrefetchScalarGridSpec(
            num_scalar_prefetch=2,
            grid=(n_blocks,),
            in_specs=[pl.BlockSpec((MOE_BLOCK, d), lambda i, be, nb: (i, 0)),
                      wspec((d, de)), wspec((d, de)), wspec((de, d))],
            out_specs=pl.BlockSpec((MOE_BLOCK, d), lambda i, be, nb: (i, 0)),
            scratch_shapes=[pltpu.VMEM((d, de), BF16), pltpu.VMEM((d, de), BF16), pltpu.VMEM((de, d), BF16)]),
        out_shape=jax.ShapeDtypeStruct((n_slots, d), F32),
        compiler_params=_params("arbitrary"),
        name="moe_experts",
    )(block_e, n_used, xb, w_gate, w_up, w_down)


def _ffn_post_kernel(h_ref, y0_ref, y1_ref, w_ref, gt_ref, g_ref, o_ref, *, final):
    w = w_ref[...]
    y = w[:, 0:1] * y0_ref[...] + w[:, 1:2] * y1_ref[...]
    out = h_ref[...] + gt_ref[...] * y
    o_ref[...] = _rms(out, g_ref[...]) if final else out


def _ffn_post(h, y0, y1, w, gate, n_ctx, final_g=None):
    bsz, t, d = h.shape
    tm = ROW_TILE
    seg = lambda b, i: (b, (i * tm >= n_ctx).astype(jnp.int32), 0, 0)
    row = lambda width: pl.BlockSpec((None, tm, width), lambda b, i: (b, i, 0))
    final = final_g is not None
    g = final_g if final else jnp.ones((d,), F32)
    return pl.pallas_call(
        functools.partial(_ffn_post_kernel, final=final),
        grid=(bsz, t // tm),
        in_specs=[row(d), row(d), row(d), row(LANES), pl.BlockSpec((None, None, 1, d), seg),
                  pl.BlockSpec((1, d), lambda b, i: (0, 0))],
        out_specs=row(d),
        out_shape=jax.ShapeDtypeStruct((bsz, t, d), F32),
        compiler_params=_params("parallel", "parallel"),
        name="ffn_post",
    )(h, y0, y1, w, gate, g.reshape(1, d))


def _slot_layout(n, ri, counts):
    e = jnp.swapaxes(ri[:, 0:2], 1, 2).reshape(n, TOP_K)
    rank = jnp.swapaxes(ri[:, 2:4], 1, 2).reshape(n, TOP_K)
    padded = (counts + MOE_BLOCK - 1) // MOE_BLOCK * MOE_BLOCK
    pend = jnp.cumsum(padded)
    pstart = pend - padded
    experts = jnp.arange(N_EXPERTS, dtype=jnp.int32)
    dest = rank + jnp.sum(jnp.where(e[..., None] == experts, pstart, 0), axis=-1)
    n_slots = (n * TOP_K + MOE_BLOCK - 1) // MOE_BLOCK * MOE_BLOCK + N_EXPERTS * MOE_BLOCK
    n_blocks = n_slots // MOE_BLOCK
    tok = jnp.repeat(jnp.arange(n, dtype=jnp.int32), TOP_K)
    slot_tok = jnp.full((n_slots,), n, jnp.int32).at[dest.reshape(-1)].set(tok)
    blk0 = jnp.arange(n_blocks, dtype=jnp.int32)[:, None] * MOE_BLOCK
    block_e = jnp.minimum(jnp.sum((pend[None, :] <= blk0).astype(jnp.int32), axis=-1), N_EXPERTS - 1)
    n_used = (pend[-1] // MOE_BLOCK).astype(jnp.int32).reshape(1)
    return dest, slot_tok, block_e.astype(jnp.int32), n_used


def _moe(h, g, shift, scale, gate, router_w_pad, router_b, w_gate, w_up, w_down, n_ctx, final_g=None):
    bsz, t, d = h.shape
    n = bsz * t
    v, ri, rwt, counts = _ffn_pre(h, g, shift, scale, router_w_pad, router_b, n_ctx)
    dest, slot_tok, block_e, n_used = _slot_layout(n, ri, counts[:, 0].astype(jnp.int32))
    w = jnp.swapaxes(rwt[:, 0:2], 1, 2).reshape(n, TOP_K)
    vp = jnp.concatenate([v.reshape(n, d), jnp.zeros((1, d), BF16)], axis=0)
    yb = _experts(vp[slot_tok], block_e, n_used, w_gate, w_up, w_down)
    y0 = yb[dest[:, 0]].reshape(bsz, t, d)
    y1 = yb[dest[:, 1]].reshape(bsz, t, d)
    wpad = jnp.zeros((n, LANES), F32).at[:, :TOP_K].set(w).reshape(bsz, t, LANES)
    return _ffn_post(h, y0, y1, wpad, gate, n_ctx, final_g)


DFT_STEP = 8


def _dft_tables(n):
    size = 2 * n
    r = int(round(math.sqrt(size)))
    assert r * r == size and r % DFT_STEP == 0
    p1 = np.arange(r // 2)[None, None, :]
    p2 = np.arange(r)[:, None, None]
    k1 = np.arange(r)[None, :, None]
    ang = 2.0 * np.pi * (((r * p1 + p2) * k1) % size) / size
    g_re, g_im = np.cos(ang), -np.sin(ang)
    g_in = np.concatenate([g_re, g_im], axis=1)
    g_out = np.concatenate([np.swapaxes(g_re, 1, 2), np.swapaxes(g_im, 1, 2)], axis=2) / size
    a2 = 2.0 * np.pi * ((np.arange(r)[:, None] * np.arange(r)[None, :]) % r) / r
    f_re, f_im = np.cos(a2), -np.sin(a2)
    f_fwd = np.block([[f_re, -f_im], [f_im, f_re]])
    f_inv = np.block([[f_re, f_im], [-f_im, f_re]])
    cast = lambda a: jnp.asarray(a.astype(np.float32)).astype(BF16)
    return r, cast(g_in), cast(g_out), cast(f_fwd), cast(f_inv)


def _hy_filter_kernel(z_ref, w1_ref, b1_ref, w2_ref, b2_ref, w3_ref, fr_ref, rates_ref, o_ref):
    z = z_ref[...]
    hid = jnp.sin(fr_ref[...] * (_dot_f32(z, w1_ref[...]) + b1_ref[...]))
    hid = jnp.sin(fr_ref[...] * (_dot_f32(hid, w2_ref[...]) + b2_ref[...]))
    filt = _dot_f32(hid, w3_ref[...])
    decay = jnp.exp(-z[:, 0:1] * rates_ref[...])
    for q in range(o_ref.shape[0]):
        o_ref[q] = filt[:, q * HY_CH:(q + 1) * HY_CH] * decay


def _hy_filters(n, w1, b1, w2, b2, w3, freq):
    t = jnp.linspace(0.0, 1.0, n, dtype=F32)[:, None]
    bands = jnp.linspace(1e-4, HY_BANDS - 1, HY_BANDS, dtype=F32)
    ang = (2.0 * math.pi / n) * jnp.arange(n, dtype=F32)[:, None] * bands
    z = jnp.concatenate([t, jnp.cos(ang), -jnp.sin(ang)], axis=-1)
    z = jnp.pad(z, ((0, 0), (0, LANES - z.shape[1])))
    w1p = jnp.pad(w1, ((0, LANES - w1.shape[0]), (0, 0)))
    rates = jnp.abs(jnp.linspace(HY_MIN_DECAY, HY_MAX_DECAY, HY_CH, dtype=F32)).reshape(1, HY_CH)
    tm = ROW_TILE
    nq = HY_ORDER * 2
    full = lambda a: pl.BlockSpec(a.shape, lambda i: (0,) * a.ndim)
    args = (z, w1p, b1.reshape(1, -1), w2, b2.reshape(1, -1), w3, freq.reshape(1, -1), rates)
    return pl.pallas_call(
        _hy_filter_kernel,
        grid=(n // tm,),
        in_specs=[pl.BlockSpec((tm, LANES), lambda i: (i, 0))] + [full(a) for a in args[1:]],
        out_specs=pl.BlockSpec((nq, tm, HY_CH), lambda i: (0, i, 0)),
        out_shape=jax.ShapeDtypeStruct((nq, n, HY_CH), F32),
        compiler_params=_params("parallel"),
        name="hy_filters",
    )(*args)


def _dft_in_kernel(x_ref, g_ref, a_ref):
    for j in range(DFT_STEP):
        a_ref[j] = _dot(g_ref[j], x_ref[:, j, :].astype(BF16)).astype(BF16)


def _dft_in(x4, col, g_in):
    bx, rh, r, _ = x4.shape
    c = HY_CH
    return pl.pallas_call(
        _dft_in_kernel,
        grid=(bx, r // DFT_STEP),
        in_specs=[pl.BlockSpec((None, rh, DFT_STEP, c), lambda b, i: (b, 0, i, col)),
                  pl.BlockSpec((DFT_STEP, 2 * r, rh), lambda b, i: (i, 0, 0))],
        out_specs=pl.BlockSpec((None, DFT_STEP, 2 * r, c), lambda b, i: (b, i, 0, 0)),
        out_shape=jax.ShapeDtypeStruct((bx, r, 2 * r, c), BF16),
        compiler_params=_params("parallel", "parallel"),
        name="dft_in",
    )(x4, g_in)


def _stage2(a_ref, b, j, f_ref):
    a = jnp.concatenate([a_ref[b, :, 0, j, :], a_ref[b, :, 1, j, :]], axis=0)
    return _dot(f_ref[...], a)


def _dft_filt_kernel(a_ref, f_ref, k_ref):
    r = f_ref.shape[0] // 2
    for j in range(DFT_STEP):
        sf, sb = _stage2(a_ref, 0, j, f_ref), _stage2(a_ref, 1, j, f_ref)
        k_ref[j, 0:r, :] = sf[0:r] + sb[0:r]
        k_ref[j, r:2 * r, :] = sf[r:2 * r] - sb[r:2 * r]


def _dft_filt(a, f_fwd):
    nq, r, _, c = a.shape
    a5 = a.reshape(nq, r, 2, r, c)
    return pl.pallas_call(
        _dft_filt_kernel,
        grid=(nq // 2, r // DFT_STEP),
        in_specs=[pl.BlockSpec((2, r, 2, DFT_STEP, c), lambda o, i: (o, 0, 0, i, 0)),
                  pl.BlockSpec(f_fwd.shape, lambda o, i: (0, 0))],
        out_specs=pl.BlockSpec((None, DFT_STEP, 2 * r, c), lambda o, i: (o, i, 0, 0)),
        out_shape=jax.ShapeDtypeStruct((nq // 2, r, 2 * r, c), F32),
        compiler_params=_params("parallel", "parallel"),
        name="dft_filt",
    )(a5, f_fwd)


def _dft_mid_kernel(a_ref, k_ref, ff_ref, fi_ref, b_ref):
    r = ff_ref.shape[0] // 2
    for b in range(a_ref.shape[0]):
        for j in range(DFT_STEP):
            s = _stage2(a_ref, b, j, ff_ref)
            sr, si = s[0:r], s[r:2 * r]
            kr, ki = k_ref[j, 0:r, :], k_ref[j, r:2 * r, :]
            p = jnp.concatenate([sr * kr - si * ki, sr * ki + si * kr], axis=0).astype(BF16)
            b_ref[b, j] = _dot(fi_ref[...], p).astype(BF16)


def _dft_mid(a, kspec, order, f_fwd, f_inv):
    bsz, r, _, c = a.shape
    a5 = a.reshape(bsz, r, 2, r, c)
    return pl.pallas_call(
        _dft_mid_kernel,
        grid=(r // DFT_STEP,),
        in_specs=[pl.BlockSpec((bsz, r, 2, DFT_STEP, c), lambda i: (0, 0, 0, i, 0)),
                  pl.BlockSpec((None, DFT_STEP, 2 * r, c), lambda i: (order, i, 0, 0)),
                  pl.BlockSpec(f_fwd.shape, lambda i: (0, 0)),
                  pl.BlockSpec(f_inv.shape, lambda i: (0, 0))],
        out_specs=pl.BlockSpec((bsz, DFT_STEP, 2 * r, c), lambda i: (0, i, 0, 0)),
        out_shape=jax.ShapeDtypeStruct((bsz, r, 2 * r, c), BF16),
        compiler_params=_params("parallel"),
        name="dft_mid",
    )(a5, kspec, f_fwd, f_inv)


def _dft_out_kernel(b_ref, g_ref, u_ref, x_ref, bias_ref, o_ref):
    for j in range(DFT_STEP):
        rhs = jnp.concatenate([b_ref[:, 0, j, :], b_ref[:, 1, j, :]], axis=0)
        y = _dot(g_ref[j], rhs)
        o_ref[:, j, :] = x_ref[:, j, :] * (y + u_ref[:, j, :] * bias_ref[...])


def _dft_out(bm, g_out, u4, u_col, x4, x_col, bias):
    bsz, r, _, c = bm.shape
    rh = r // 2
    b5 = bm.reshape(bsz, r, 2, r, c)
    seq = lambda col: pl.BlockSpec((None, rh, DFT_STEP, c), lambda b, i: (b, 0, i, col))
    return pl.pallas_call(
        _dft_out_kernel,
        grid=(bsz, r // DFT_STEP),
        in_specs=[pl.BlockSpec((None, r, 2, DFT_STEP, c), lambda b, i: (b, 0, 0, i, 0)),
                  pl.BlockSpec((DFT_STEP, rh, 2 * r), lambda b, i: (i, 0, 0)),
                  seq(u_col), seq(x_col), pl.BlockSpec((1, c), lambda b, i: (0, 0))],
        out_specs=seq(0),
        out_shape=jax.ShapeDtypeStruct((bsz, rh, r, c), F32),
        compiler_params=_params("parallel", "parallel"),
        name="dft_out",
    )(b5, g_out, u4, x4, bias.reshape(1, c))


def _hyena(hy_in, fw1, fb1, fw2, fb2, fw3, freq, conv_bias):
    bsz, n, _ = hy_in.shape
    r, g_in, g_out, f_fwd, f_inv = _dft_tables(n)
    filt = _hy_filters(n, fw1, fb1, fw2, fb2, fw3, freq)
    kspec = _dft_filt(_dft_in(filt.reshape(-1, r // 2, r, HY_CH), 0, g_in), f_fwd)
    seq4 = hy_in.reshape(bsz, r // 2, r, 3 * HY_CH)
    zz = _dft_out(_dft_mid(_dft_in(seq4, 0, g_in), kspec, 0, f_fwd, f_inv), g_out,
                  seq4, 0, seq4, 1, conv_bias[0])
    out = _dft_out(_dft_mid(_dft_in(zz, 0, g_in), kspec, 1, f_fwd, f_inv), g_out,
                   zz, 0, seq4, 2, conv_bias[1])
    return out.reshape(bsz, n, HY_CH)


CD_HY, CD_Z, CD_XBC, CD_DT = 0, 1536, 2048, 3072
CD_PAD_COLS = CD_DT + 2 * LANES


def _reorder_ab(w):
    gq, gk, gv, gg, lr_f, lr_b, hq, hf_f, hf_b, hi, hg = jnp.split(
        w, np.cumsum([256, 256, 512, 512, 16, 16, 512, 512, 512, 512, 512])[:-1].tolist(), axis=-1)
    pad = jnp.zeros((w.shape[0], AB_PAD_COLS - AB_LR - 2 * GLA_LOW_RANK), w.dtype)
    return jnp.concatenate([gq, gk, gv, gg, hq, hf_f, hf_b, hi, hg, lr_f, lr_b, pad], axis=-1)


def _reorder_cd(w):
    hy, z, xbc, dt_f, dt_b = jnp.split(w, np.cumsum([1536, 512, 1024, 8, 8])[:-1].tolist(), axis=-1)
    pad = jnp.zeros((w.shape[0], LANES - MB_HEADS), w.dtype)
    return jnp.concatenate([hy, z, xbc, dt_f, pad, dt_b, pad], axis=-1)


def kernel(x, c, ctx, c_ctx, ada_w, ada_b, norm_mix_g, norm_ffn_g, norm_out_g, ab_w_in, ab_w_out, gla_gate_w, gla_gate_b, gla_norm_g, hg_lb, hg_norm_g, cd_w_in, cd_w_out, hy_short_w, hy_short_b, hy_w1, hy_b1, hy_w2, hy_b2, hy_w3, hy_freq, hy_bias, mb_conv_w, mb_conv_b, mb_dt_bias, mb_a_log, mb_d, mb_norm_g, router_w, router_b, moe_w_gate, moe_w_up, moe_w_down):
    bsz, n_lat, d = x.shape
    n_ctx = ctx.shape[1]
    t = n_ctx + n_lat
    assert ada_w.shape[0] == 2 and ab_w_in.shape[0] == 1 and cd_w_in.shape[0] == 1

    cond = jnp.zeros((8, d), F32).at[:bsz].set(c).at[bsz].set(c_ctx)
    m = _adaln(cond, ada_w, ada_b)

    def mods(layer):
        lat = m[layer, :bsz].reshape(bsz, 6, d)
        cx = jnp.broadcast_to(m[layer, bsz].reshape(1, 6, d), (bsz, 6, d))
        both = jnp.stack([cx, lat], axis=1)
        return [both[:, :, j][:, :, None, :] for j in range(6)]

    lb_all = jnp.cumsum(jax.nn.softmax(hg_lb.astype(F32), axis=1), axis=1)
    router_w_pad = jnp.zeros((d, LANES), F32).at[:, :N_EXPERTS].set(router_w)
    h = jnp.concatenate([ctx, x], axis=1)

    sh_m, sc_m, gt_m, sh_f, sc_f, gt_f = mods(0)
    proj = _norm_proj(h, norm_mix_g[0], sh_m, sc_m, _reorder_ab(ab_w_in[0]).astype(BF16), n_ctx)
    gwp = [jnp.zeros((LANES, GLA_KEY_W), F32).at[GLA_LOW_RANK * dd:GLA_LOW_RANK * (dd + 1)].set(gla_gate_w[0, dd])
           for dd in range(2)]
    o_gla = _gla_scan(proj, gwp, [gla_gate_b[0, dd].reshape(1, -1) for dd in range(2)], n_ctx)
    o_hg = _hgrn_scan(proj, [lb_all[dd, 0].reshape(1, -1) for dd in range(2)], n_ctx)
    h = _mix_out_ab(h, o_gla, o_hg, proj, gla_norm_g[0], hg_norm_g[0], ab_w_out[0].astype(BF16), gt_m, n_ctx)
    h = _moe(h, norm_ffn_g[0], sh_f, sc_f, gt_f, router_w_pad, router_b,
             moe_w_gate[0], moe_w_up[0], moe_w_down[0], n_ctx)

    sh_m, sc_m, gt_m, sh_f, sc_f, gt_f = mods(1)
    proj = _norm_proj(h, norm_mix_g[1], sh_m, sc_m, _reorder_cd(cd_w_in[0]).astype(BF16), n_ctx)
    hy_in = _dwconv(proj, CD_HY, 3 * HY_CH, hy_short_w[0], hy_short_b[0], (n_ctx, t), act=False)
    hy = _hyena(hy_in, hy_w1[0], hy_b1[0], hy_w2[0], hy_b2[0], hy_w3[0], hy_freq[0], hy_bias[0])
    xbc = _dwconv(proj, CD_XBC, MB_INNER + 2 * MB_BC_W, mb_conv_w[0], mb_conv_b[0], (0, n_ctx, t), act=True)
    y_ssd = [_ssd_scan(xbc, proj, CD_DT // LANES + dd, mb_dt_bias[0, dd], mb_a_log[0, dd], n_ctx, dd == 1)
             for dd in range(2)]
    d_skip_x = jnp.repeat(mb_d[0], MB_HEAD_DIM).reshape(1, MB_INNER)
    h = _mix_out_cd(h, hy, y_ssd, xbc, proj, CD_Z // MB_INNER, d_skip_x, mb_norm_g[0],
                    cd_w_out[0].astype(BF16), gt_m, n_ctx)
    return _moe(h, norm_ffn_g[1], sh_f, sc_f, gt_f, router_w_pad, router_b,
                moe_w_gate[1], moe_w_up[1], moe_w_down[1], 0, final_g=norm_out_g)
```

```python
import functools
import math

import numpy as np
import jax
import jax.numpy as jnp
from jax import lax
from jax.experimental import pallas as pl
from jax.experimental.pallas import tpu as pltpu
from jax.experimental.pallas import tpu_sc as plsc

NORM_EPS = 1e-6
GLA_HEADS, GLA_DK, GLA_DV, GLA_LOW_RANK, GLA_TAU = 4, 64, 128, 16, 16.0
GLA_KEY_W, GLA_VAL_W = GLA_HEADS * GLA_DK, GLA_HEADS * GLA_DV
HG_HEADS, HG_EXPAND, HG_DV = 4, 128, 128
HG_KEY_W, HG_VAL_W = HG_HEADS * HG_EXPAND, HG_HEADS * HG_DV
HY_CH, HY_ORDER, HY_SHORT, HY_BANDS, HY_FILT_HID = 512, 2, 3, 16, 64
HY_MIN_DECAY = math.log(1e-2) / 1.5
HY_MAX_DECAY = math.log(1e-2) / 0.3
MB_HEADS, MB_HEAD_DIM, MB_GROUPS, MB_STATE = 8, 64, 2, 128
MB_INNER = MB_HEADS * MB_HEAD_DIM
MB_BC_W = MB_GROUPS * MB_STATE
N_EXPERTS, N_GROUPS, TOP_K, MOE_BLOCK = 16, 4, 2, 256
EXPERTS_PER_GROUP = N_EXPERTS // N_GROUPS

LANES = 128
SCAN_CHUNK = 64
SCAN_BLOCK = 128
SSD_CHUNK = 128
ROW_TILE = 256
VMEM_LIMIT = 56 * 1024 * 1024

BF16 = jnp.bfloat16
F32 = jnp.float32


def _params(*sem):
    return pltpu.CompilerParams(dimension_semantics=sem, vmem_limit_bytes=VMEM_LIMIT)


def _split3(x):
    hi = x.astype(BF16)
    r1 = x - hi.astype(F32)
    mid = r1.astype(BF16)
    lo = (r1 - mid.astype(F32)).astype(BF16)
    return hi, mid, lo


def _dot(a, b):
    return jnp.dot(a, b, preferred_element_type=F32)


def _dot_nt(a, b):
    return lax.dot_general(a, b, (((1,), (1,)), ((), ())), preferred_element_type=F32)


def _dot_tn(a, b):
    return lax.dot_general(a, b, (((0,), (0,)), ((), ())), preferred_element_type=F32)


def _sel_dot(m01, x):
    hi, mid, lo = _split3(x)
    return _dot(m01, hi) + (_dot(m01, mid) + _dot(m01, lo))


def _dot_sel(x, m01):
    hi, mid, lo = _split3(x)
    return _dot(hi, m01) + (_dot(mid, m01) + _dot(lo, m01))


def _dot_f32(a, b):
    ah = a.astype(BF16)
    al = (a - ah.astype(F32)).astype(BF16)
    bh = b.astype(BF16)
    bl = (b - bh.astype(F32)).astype(BF16)
    return _dot(ah, bh) + (_dot(ah, bl) + _dot(al, bh))


def _silu(x):
    return x * (1.0 / (1.0 + jnp.exp(-x)))


def _sigmoid(x):
    return 1.0 / (1.0 + jnp.exp(-x))


def _softplus(x):
    return jnp.maximum(x, 0.0) + jnp.log(1.0 + jnp.exp(-jnp.abs(x)))


def _rms(x, g):
    return x * lax.rsqrt(jnp.mean(x * x, axis=-1, keepdims=True) + NORM_EPS) * g


def _adaln_kernel(c_ref, w_ref, b_ref, o_ref):
    o_ref[...] = _dot_f32(_silu(c_ref[...]), w_ref[...]) + b_ref[...]


def _adaln(cond, w, b):
    n_l, d, n6 = w.shape
    tn = 1536
    return pl.pallas_call(
        _adaln_kernel,
        grid=(n_l, n6 // tn),
        in_specs=[pl.BlockSpec((8, d), lambda l, j: (0, 0)),
                  pl.BlockSpec((None, d, tn), lambda l, j: (l, 0, j)),
                  pl.BlockSpec((None, 1, tn), lambda l, j: (l, 0, j))],
        out_specs=pl.BlockSpec((None, 8, tn), lambda l, j: (l, 0, j)),
        out_shape=jax.ShapeDtypeStruct((n_l, 8, n6), F32),
        compiler_params=_params("parallel", "parallel"),
        name="adaln",
    )(cond, w, b.reshape(n_l, 1, n6))


def _norm_proj_kernel(h_ref, g_ref, sh_ref, sc_ref, w_ref, o_ref):
    u = _rms(h_ref[...], g_ref[...]) * (1.0 + sc_ref[...]) + sh_ref[...]
    o_ref[...] = _dot(u.astype(BF16), w_ref[...])


def _norm_proj(h, g, shift, scale, w, n_ctx):
    bsz, t, d = h.shape
    n = w.shape[1]
    tm = ROW_TILE
    seg = lambda b, i: (b, (i * tm >= n_ctx).astype(jnp.int32), 0, 0)
    return pl.pallas_call(
        _norm_proj_kernel,
        grid=(bsz, t // tm),
        in_specs=[pl.BlockSpec((None, tm, d), lambda b, i: (b, i, 0)),
                  pl.BlockSpec((1, d), lambda b, i: (0, 0)),
                  pl.BlockSpec((None, None, 1, d), seg),
                  pl.BlockSpec((None, None, 1, d), seg),
                  pl.BlockSpec((d, n), lambda b, i: (0, 0))],
        out_specs=pl.BlockSpec((None, tm, n), lambda b, i: (b, i, 0)),
        out_shape=jax.ShapeDtypeStruct((bsz, t, n), F32),
        compiler_params=_params("parallel", "parallel"),
        name="norm_proj",
    )(h, g.reshape(1, d), shift, scale, w)


def _scan_constants(c, reverse):
    t = np.arange(c)[:, None]
    u = np.arange(c)[None, :]
    sels = [u <= t, u > t]
    masks = []
    m = c // 2
    while m >= 1:
        blk = t // (2 * m)
        upper_t = (t % (2 * m)) >= m
        r = blk * (2 * m) + m - 1
        s_blk = u // (2 * m)
        upper_s = (u % (2 * m)) >= m
        sels.append((upper_t & (u > r) & (u <= t)) | ((~upper_t) & (u > t) & (u <= r)))
        masks.append((blk == s_blk) & upper_t & (~upper_s))
        m //= 2
    masks.append(t == u)
    sel = np.stack(sels).astype(np.float32)
    msk = np.stack(masks).astype(np.float32)
    if reverse:
        sel = sel[:, ::-1, ::-1]
        msk = msk[:, ::-1, ::-1]
    return np.ascontiguousarray(sel.reshape(-1, c)), np.ascontiguousarray(msk)


def _chunk_order(i, n_ctx_chunks, n_chunks, reverse):
    if not reverse:
        return i
    return jnp.where(i < n_ctx_chunks, n_ctx_chunks - 1 - i, n_chunks - 1 - (i - n_ctx_chunks))


GROUP_KEYS = 256


def _decay_chunk(q, k, v, la, sel_ref, mask_ref, hm_ref, st_ref, heads, dk, dv):
    c = q.shape[0]
    n_lvl = mask_ref.shape[0] - 1
    hpg = GROUP_KEYS // dk
    cs = _dot(sel_ref[...], jnp.concatenate(_split3(la), axis=0))
    e_q = jnp.exp(cs[0:c])
    e_k = jnp.exp(cs[c:2 * c])
    e_tot = jnp.exp(jnp.sum(la, axis=0, keepdims=True))
    vb = v.astype(BF16)
    outs = []
    for g in range(heads // hpg):
        ks = slice(g * GROUP_KEYS, (g + 1) * GROUP_KEYS)
        vs = slice(g * hpg * dv, (g + 1) * hpg * dv)
        qg, kg = q[:, ks], k[:, ks]
        stack = lambda x: jnp.concatenate([x * hm_ref[h] for h in range(hpg)], axis=0).astype(BF16)
        att = mask_ref[n_lvl] * _dot_nt(stack(qg), kg.astype(BF16))
        for l in range(n_lvl):
            e = jnp.exp(cs[(2 + l) * c:(3 + l) * c, ks])
            att = att + mask_ref[l] * _dot_nt(stack(qg * e), (kg * e).astype(BF16))
        att = att.astype(BF16)
        st = st_ref[g]
        inter = _dot_nt(stack(qg * e_q[:, ks]), st.astype(BF16))
        upd = _dot_tn(vb[:, vs], (kg * e_k[:, ks]).astype(BF16))
        new = st * e_tot[:, ks]
        for h in range(hpg):
            rows = slice(h * c, (h + 1) * c)
            hv = slice(g * hpg * dv + h * dv, g * hpg * dv + (h + 1) * dv)
            outs.append(_dot(att[rows], vb[:, hv]) + inter[rows])
            new = new + upd[h * dv:(h + 1) * dv] * hm_ref[h]
        st_ref[g] = new
    return jnp.concatenate(outs, axis=-1)


def _log_sigmoid(x):
    return jnp.minimum(x, 0.0) - jnp.log(1.0 + jnp.exp(-jnp.abs(x)))


def _gla_kernel(*refs):
    ins, hm_ref, (o_refs, st_refs) = (refs[0:8], refs[8:16]), refs[16], (refs[17:19], refs[19:21])

    @pl.when(pl.program_id(1) == 0)
    def _():
        for st_ref in st_refs:
            st_ref[...] = jnp.zeros_like(st_ref)

    for d, ((q_ref, k_ref, v_ref, lr_ref, gw_ref, gb_ref, sel_ref, mask_ref), o_ref, st_ref) in enumerate(
            zip(ins, o_refs, st_refs)):
        z = _dot_f32(lr_ref[...], gw_ref[...]) + gb_ref[...]
        la = _log_sigmoid(z) * (1.0 / GLA_TAU)
        q = q_ref[...] * (GLA_DK ** -0.5)
        k, v = k_ref[...], v_ref[...]
        for rows in _sub_chunks(q.shape[0], d == 1):
            o_ref[rows, :] = _decay_chunk(q[rows], k[rows], v[rows], la[rows], sel_ref, mask_ref, hm_ref,
                                          st_ref, GLA_HEADS, GLA_DK, GLA_DV)


def _hgrn_kernel(*refs):
    ins, hm_ref, (o_refs, st_refs) = (refs[0:6], refs[6:12]), refs[12], (refs[13:15], refs[15:17])

    @pl.when(pl.program_id(1) == 0)
    def _():
        for st_ref in st_refs:
            st_ref[...] = jnp.zeros_like(st_ref)

    for d, ((q_ref, f_ref, v_ref, lb_ref, sel_ref, mask_ref), o_ref, st_ref) in enumerate(
            zip(ins, o_refs, st_refs)):
        lb = lb_ref[...]
        f = lb + (1.0 - lb) * _sigmoid(f_ref[...])
        q, k, v, la = _silu(q_ref[...]), 1.0 - f, v_ref[...], jnp.log(f)
        for rows in _sub_chunks(q.shape[0], d == 1):
            o_ref[rows, :] = _decay_chunk(q[rows], k[rows], v[rows], la[rows], sel_ref, mask_ref, hm_ref,
                                          st_ref, HG_HEADS, HG_EXPAND, HG_DV)


def _sub_chunks(rows, reverse):
    order = range(rows // SCAN_CHUNK)
    return [slice(j * SCAN_CHUNK, (j + 1) * SCAN_CHUNK) for j in (reversed(order) if reverse else order)]


def _scan_specs(blk, n_ctx, t, reverse, chunk=None, stacked_heads=1):
    n_blocks = t // blk
    order = functools.partial(_chunk_order, n_ctx_chunks=n_ctx // blk, n_chunks=n_blocks, reverse=reverse)

    def col(width, idx):
        return pl.BlockSpec((None, blk, width), lambda b, i: (b, order(i), idx))

    sel, msk = _scan_constants(chunk or blk, reverse)
    sel3 = np.concatenate([sel, sel, sel], axis=1)
    msk = np.tile(msk, (1, stacked_heads, 1))
    const = lambda a: pl.BlockSpec(a.shape, lambda b, i: (0,) * a.ndim)
    return n_blocks, col, const, jnp.asarray(sel3, BF16), jnp.asarray(msk, F32)


def _head_masks(dk):
    hpg = GROUP_KEYS // dk
    hm = np.zeros((hpg, 1, GROUP_KEYS), np.float32)
    for h in range(hpg):
        hm[h, 0, h * dk:(h + 1) * dk] = 1.0
    return jnp.asarray(hm)


AB_Q, AB_K, AB_V, AB_G = 0, 256, 512, 1024
AB_HQ, AB_HF, AB_HI, AB_HG, AB_LR = 1536, 2048, 3072, 3584, 4096
AB_PAD_COLS = 4224


def _gla_scan(proj, gate_w_pad, gate_b, n_ctx):
    bsz, t, _ = proj.shape
    hpg = GROUP_KEYS // GLA_DK
    in_specs, args, outs = [], [], []
    for d in range(2):
        n_blocks, col, const, sel, msk = _scan_specs(SCAN_BLOCK, n_ctx, t, d == 1, SCAN_CHUNK, hpg)
        in_specs += [col(GLA_KEY_W, AB_Q // GLA_KEY_W), col(GLA_KEY_W, AB_K // GLA_KEY_W),
                     col(GLA_VAL_W, AB_V // GLA_VAL_W), col(LANES, AB_LR // LANES),
                     const(gate_w_pad[d]), const(gate_b[d]), const(sel), const(msk)]
        args += [proj, proj, proj, proj, gate_w_pad[d], gate_b[d], sel, msk]
        outs.append(col(GLA_VAL_W, 0))
    hm = _head_masks(GLA_DK)
    return pl.pallas_call(
        _gla_kernel,
        grid=(bsz, n_blocks),
        in_specs=in_specs + [const(hm)],
        out_specs=outs,
        out_shape=[jax.ShapeDtypeStruct((bsz, t, GLA_VAL_W), F32)] * 2,
        scratch_shapes=[pltpu.VMEM((GLA_HEADS // hpg, GLA_DV, GROUP_KEYS), F32)] * 2,
        compiler_params=_params("parallel", "arbitrary"),
        name="gla_scan",
    )(*args, hm)


def _hgrn_scan(proj, lb, n_ctx):
    bsz, t, _ = proj.shape
    hpg = GROUP_KEYS // HG_EXPAND
    in_specs, args, outs = [], [], []
    for d in range(2):
        n_blocks, col, const, sel, msk = _scan_specs(SCAN_BLOCK, n_ctx, t, d == 1, SCAN_CHUNK, hpg)
        in_specs += [col(HG_KEY_W, AB_HQ // HG_KEY_W), col(HG_KEY_W, AB_HF // HG_KEY_W + d),
                     col(HG_VAL_W, AB_HI // HG_VAL_W), const(lb[d]), const(sel), const(msk)]
        args += [proj, proj, proj, lb[d], sel, msk]
        outs.append(col(HG_VAL_W, 0))
    hm = _head_masks(HG_EXPAND)
    return pl.pallas_call(
        _hgrn_kernel,
        grid=(bsz, n_blocks),
        in_specs=in_specs + [const(hm)],
        out_specs=outs,
        out_shape=[jax.ShapeDtypeStruct((bsz, t, HG_VAL_W), F32)] * 2,
        scratch_shapes=[pltpu.VMEM((HG_HEADS // hpg, HG_DV, GROUP_KEYS), F32)] * 2,
        compiler_params=_params("parallel", "arbitrary"),
        name="hgrn_scan",
    )(*args, hm)


def _mix_out_ab_kernel(h_ref, gf_ref, gb_ref, hf_ref, hb_ref, gg_ref, hg_ref, gn_ref, hn_ref,
                       w_ref, gt_ref, o_ref):
    feats = []
    for o, gate, g in ((gf_ref[...] + gb_ref[...], gg_ref[...], gn_ref[...]),
                       (hf_ref[...] + hb_ref[...], hg_ref[...], hn_ref[...])):
        for hd in range(o.shape[-1] // LANES):
            s = slice(hd * LANES, (hd + 1) * LANES)
            feats.append(_rms(o[:, s], g) * _silu(gate[:, s]))
    feat = jnp.concatenate(feats, axis=-1).astype(BF16)
    o_ref[...] = h_ref[...] + gt_ref[...] * _dot(feat, w_ref[...])


def _mix_out_ab(h, o_gla, o_hg, proj, gla_norm_g, hg_norm_g, w_out, gate, n_ctx):
    bsz, t, d = h.shape
    tm = ROW_TILE
    seg = lambda b, i: (b, (i * tm >= n_ctx).astype(jnp.int32), 0, 0)
    row = lambda width, idx: pl.BlockSpec((None, tm, width), lambda b, i: (b, i, idx))
    vec = pl.BlockSpec((1, LANES), lambda b, i: (0, 0))
    return pl.pallas_call(
        _mix_out_ab_kernel,
        grid=(bsz, t // tm),
        in_specs=[row(d, 0), row(GLA_VAL_W, 0), row(GLA_VAL_W, 0), row(HG_VAL_W, 0), row(HG_VAL_W, 0),
                  row(GLA_VAL_W, AB_G // GLA_VAL_W), row(HG_VAL_W, AB_HG // HG_VAL_W), vec, vec,
                  pl.BlockSpec(w_out.shape, lambda b, i: (0, 0)),
                  pl.BlockSpec((None, None, 1, d), seg)],
        out_specs=row(d, 0),
        out_shape=jax.ShapeDtypeStruct((bsz, t, d), F32),
        compiler_params=_params("parallel", "parallel"),
        name="mix_out_ab",
    )(h, o_gla[0], o_gla[1], o_hg[0], o_hg[1], proj, proj, gla_norm_g.reshape(1, -1),
      hg_norm_g.reshape(1, -1), w_out, gate)


def _dwconv_kernel(x_ref, prev_ref, next_ref, w_ref, b_ref, o_ref, *, seg_tiles, act):
    i = pl.program_id(1)
    tm = x_ref.shape[0]
    x = x_ref[...]
    first = functools.reduce(jnp.logical_or, [i == s for s in seg_tiles[:-1]])
    last = functools.reduce(jnp.logical_or, [i == s - 1 for s in seg_tiles[1:]])
    prev_row = jnp.where(first, 0.0, prev_ref[7:8, :])
    next_row = jnp.where(last, 0.0, next_ref[0:1, :])
    rows = lax.broadcasted_iota(jnp.int32, x.shape, 0)
    x_prev = jnp.where(rows == 0, prev_row, pltpu.roll(x, 1, axis=0))
    x_next = jnp.where(rows == tm - 1, next_row, pltpu.roll(x, tm - 1, axis=0))
    y = x_prev * w_ref[0:1, :] + x * w_ref[1:2, :] + x_next * w_ref[2:3, :] + b_ref[...]
    o_ref[...] = _silu(y) if act else y


def _dwconv(proj, col0, width, w, b, seg_bounds, act):
    bsz, t, _ = proj.shape
    tm = ROW_TILE
    off = seg_bounds[0] // tm
    n_tiles = t // tm - off
    seg_tiles = tuple(s // tm - off for s in seg_bounds)
    cb = col0 // width
    r8 = tm // 8
    last8 = t // 8 - 1
    kern = functools.partial(_dwconv_kernel, seg_tiles=seg_tiles, act=act)
    return pl.pallas_call(
        kern,
        grid=(bsz, n_tiles),
        in_specs=[pl.BlockSpec((None, tm, width), lambda bb, i: (bb, i + off, cb)),
                  pl.BlockSpec((None, 8, width), lambda bb, i: (bb, jnp.maximum((i + off) * r8 - 1, 0), cb)),
                  pl.BlockSpec((None, 8, width),
                               lambda bb, i: (bb, jnp.minimum((i + off + 1) * r8, last8), cb)),
                  pl.BlockSpec((3, width), lambda bb, i: (0, 0)),
                  pl.BlockSpec((1, width), lambda bb, i: (0, 0))],
        out_specs=pl.BlockSpec((None, tm, width), lambda bb, i: (bb, i, 0)),
        out_shape=jax.ShapeDtypeStruct((bsz, n_tiles * tm, width), F32),
        compiler_params=_params("parallel", "parallel"),
        name="dwconv",
    )(proj, proj, proj, w.T, b.reshape(1, -1))


def _ssd_kernel(xbc_ref, dt_ref, bias_ref, alog_ref, hexp_ref, mq_ref, mk_ref, mask_ref, o_ref, st_ref):
    @pl.when(pl.program_id(1) == 0)
    def _():
        st_ref[...] = jnp.zeros_like(st_ref)

    c = xbc_ref.shape[0]
    hpg = MB_HEADS // MB_GROUPS
    gw = hpg * MB_HEAD_DIM
    dt = _softplus(dt_ref[...] + bias_ref[...])
    la = -dt * jnp.exp(alog_ref[...])
    cq = _sel_dot(mq_ref[...], la)
    ck = _sel_dot(mk_ref[...], la)
    cq_t = _dot_nt_sel(la, mq_ref[...])
    hexp = hexp_ref[...]
    dt_x = _dot_sel(dt, hexp)
    eq_x = jnp.exp(_dot_sel(cq, hexp))
    ek_x = jnp.exp(_dot_sel(ck, hexp))
    etot_x = jnp.exp(_dot_sel(jnp.sum(la, axis=0, keepdims=True), hexp))
    xs = xbc_ref[:, 0:MB_INNER] * dt_x
    mask = mask_ref[...]
    outs = []
    for g in range(MB_GROUPS):
        bm = xbc_ref[:, MB_INNER + g * MB_STATE:MB_INNER + (g + 1) * MB_STATE].astype(BF16)
        cm = xbc_ref[:, MB_INNER + MB_BC_W + g * MB_STATE:MB_INNER + MB_BC_W + (g + 1) * MB_STATE].astype(BF16)
        cb = _dot_nt(cm, bm)
        st = st_ref[g]
        gs = slice(g * gw, (g + 1) * gw)
        y_inter = _dot(cm, st.astype(BF16)) * eq_x[:, gs]
        for r in range(hpg):
            hd = g * hpg + r
            diff = cq[:, hd:hd + 1] - cq_t[hd:hd + 1, :]
            w = cb * jnp.exp(jnp.where(mask > 0.0, diff, -jnp.inf))
            ps = slice(hd * MB_HEAD_DIM, (hd + 1) * MB_HEAD_DIM)
            outs.append(_dot(w.astype(BF16), xs[:, ps].astype(BF16))
                        + y_inter[:, r * MB_HEAD_DIM:(r + 1) * MB_HEAD_DIM])
        st_ref[g] = st * etot_x[:, gs] + _dot_tn(bm, (xs[:, gs] * ek_x[:, gs]).astype(BF16))
    o_ref[...] = jnp.concatenate(outs, axis=-1)


def _dot_nt_sel(x, m01):
    hi, mid, lo = _split3(x)
    f = lambda p: lax.dot_general(p, m01, (((0,), (1,)), ((), ())), preferred_element_type=F32)
    return f(hi) + (f(mid) + f(lo))


def _ssd_scan(xbc, proj, dt_col, dt_bias, a_log, n_ctx, reverse):
    bsz, t, _ = xbc.shape
    c = SSD_CHUNK
    n_chunks, col, const, sel, msk = _scan_specs(c, n_ctx, t, reverse)
    mq, mk = sel[0:c, 0:c], sel[c:2 * c, 0:c]
    tri = np.tril(np.ones((c, c), np.float32))
    mask = jnp.asarray(tri[::-1, ::-1].copy() if reverse else tri)
    pad = lambda v: jnp.zeros((1, LANES), F32).at[0, :MB_HEADS].set(v)
    hexp = np.zeros((LANES, MB_INNER), np.float32)
    for hd in range(MB_HEADS):
        hexp[hd, hd * MB_HEAD_DIM:(hd + 1) * MB_HEAD_DIM] = 1.0
    hexp = jnp.asarray(hexp, BF16)
    bias = pad(dt_bias)
    alog = pad(a_log.astype(F32))
    return pl.pallas_call(
        _ssd_kernel,
        grid=(bsz, n_chunks),
        in_specs=[col(xbc.shape[-1], 0), col(LANES, dt_col), const(bias), const(alog), const(hexp),
                  const(mq), const(mk), const(mask)],
        out_specs=col(MB_INNER, 0),
        out_shape=jax.ShapeDtypeStruct((bsz, t, MB_INNER), F32),
        scratch_shapes=[pltpu.VMEM((MB_GROUPS, MB_STATE, MB_INNER // MB_GROUPS), F32)],
        compiler_params=_params("parallel", "arbitrary"),
        name="ssd_scan_rev" if reverse else "ssd_scan_fwd",
    )(xbc, proj, bias, alog, hexp, mq, mk, mask)


def _mix_out_cd_kernel(h_ref, hy_ref, yf_ref, yb_ref, xs_ref, z_ref, dsk_ref, ng_ref, w_ref, gt_ref, o_ref):
    y = (yf_ref[...] + yb_ref[...] + dsk_ref[...] * xs_ref[...]) * _silu(z_ref[...])
    gw = MB_INNER // MB_GROUPS
    ys = [_rms(y[:, g * gw:(g + 1) * gw], ng_ref[:, g * gw:(g + 1) * gw]) for g in range(MB_GROUPS)]
    feat = jnp.concatenate([hy_ref[...]] + ys, axis=-1).astype(BF16)
    o_ref[...] = h_ref[...] + gt_ref[...] * _dot(feat, w_ref[...])


def _mix_out_cd(h, hy, y_ssd, xbc, proj, z_col, d_skip_x, norm_g, w_out, gate, n_ctx):
    bsz, t, d = h.shape
    tm = ROW_TILE
    n_lat = t - n_ctx
    off = n_ctx // tm
    row = lambda width, idx: pl.BlockSpec((None, tm, width), lambda b, i: (b, i + off, idx))
    vec = pl.BlockSpec((1, MB_INNER), lambda b, i: (0, 0))
    return pl.pallas_call(
        _mix_out_cd_kernel,
        grid=(bsz, n_lat // tm),
        in_specs=[row(d, 0), pl.BlockSpec((None, tm, HY_CH), lambda b, i: (b, i, 0)),
                  row(MB_INNER, 0), row(MB_INNER, 0), row(MB_INNER, 0), row(MB_INNER, z_col), vec, vec,
                  pl.BlockSpec(w_out.shape, lambda b, i: (0, 0)),
                  pl.BlockSpec((None, None, 1, d), lambda b, i: (b, 1, 0, 0))],
        out_specs=pl.BlockSpec((None, tm, d), lambda b, i: (b, i, 0)),
        out_shape=jax.ShapeDtypeStruct((bsz, n_lat, d), F32),
        compiler_params=_params("parallel", "parallel"),
        name="mix_out_cd",
    )(h, hy, y_ssd[0], y_ssd[1], xbc, proj, d_skip_x, norm_g.reshape(1, -1), w_out, gate)


def _top2_of4(a, b, c, d):
    hi1, lo1, hi2, lo2 = jnp.maximum(a, b), jnp.minimum(a, b), jnp.maximum(c, d), jnp.minimum(c, d)
    return jnp.maximum(hi1, hi2) + jnp.maximum(jnp.minimum(hi1, hi2), jnp.maximum(lo1, lo2))


def _first_argmax(vals, skip=None):
    idx = None
    for j, vj in enumerate(vals):
        if idx is None and skip is None:
            idx, best = jnp.zeros(vj.shape, jnp.int32), vj
            continue
        if idx is None:
            idx, best = jnp.full(vj.shape, -1, jnp.int32), jnp.full(vj.shape, -jnp.inf, F32)
        take = vj > best
        if skip is not None:
            take = jnp.logical_and(take, skip != j)
        idx = jnp.where(take, j, idx)
        best = jnp.where(take, vj, best)
    return idx, best


def _ffn_pre_kernel(h_ref, g_ref, sh_ref, sc_ref, rw_ref, rb_ref, tri_ref,
                    v_ref, ri_ref, rwt_ref, cnt_ref, carry_ref):
    @pl.when(jnp.logical_and(pl.program_id(0) == 0, pl.program_id(1) == 0))
    def _():
        carry_ref[...] = jnp.zeros_like(carry_ref)

    v = _rms(h_ref[...], g_ref[...]) * (1.0 + sc_ref[...]) + sh_ref[...]
    v_ref[...] = v
    st = _sigmoid(_dot_f32(v, rw_ref[...])).T[0:N_EXPERTS]
    sel = st + rb_ref[...]
    row = lambda a, e: a[e:e + 1]
    epg = EXPERTS_PER_GROUP
    gscore = [_top2_of4(*[row(sel, g * epg + j) for j in range(epg)]) for g in range(N_GROUPS)]
    best, _ = _first_argmax(gscore)

    def in_best(a, j):
        out = row(a, j)
        for g in range(1, N_GROUPS):
            out = jnp.where(best == g, row(a, g * epg + j), out)
        return out

    vals = [in_best(sel, j) for j in range(epg)]
    raw = [in_best(st, j) for j in range(epg)]
    i1, _ = _first_argmax(vals)
    i2, _ = _first_argmax(vals, skip=i1)
    pick = lambda i: functools.reduce(lambda acc, j: jnp.where(i == j, raw[j], acc), range(1, epg), raw[0])
    w1, w2 = pick(i1), pick(i2)
    wsum = w1 + w2
    e1, e2 = best * epg + i1, best * epg + i2

    experts = lax.broadcasted_iota(jnp.int32, st.shape, 0)
    oh1 = (experts == e1).astype(F32)
    oh2 = (experts == e2).astype(F32)
    cnt = oh1 + oh2
    before = _dot(cnt.astype(BF16), tri_ref[...]) + carry_ref[:, 0:1]
    ri_ref[0:1, :] = e1
    ri_ref[1:2, :] = e2
    ri_ref[2:3, :] = jnp.sum(oh1 * before, axis=0, keepdims=True).astype(jnp.int32)
    ri_ref[3:4, :] = jnp.sum(oh2 * before, axis=0, keepdims=True).astype(jnp.int32)
    ri_ref[4:8, :] = jnp.zeros((4, st.shape[1]), jnp.int32)
    rwt_ref[0:1, :] = w1 / wsum
    rwt_ref[1:2, :] = w2 / wsum
    rwt_ref[2:8, :] = jnp.zeros((6, st.shape[1]), F32)
    carry_ref[...] = carry_ref[...] + jnp.sum(cnt, axis=1, keepdims=True)
    cnt_ref[...] = carry_ref[...]


def _ffn_pre(h, g, shift, scale, router_w_pad, router_b, n_ctx):
    bsz, t, d = h.shape
    tm = ROW_TILE
    seg = lambda b, i: (b, (i * tm >= n_ctx).astype(jnp.int32), 0, 0)
    tri = jnp.asarray(np.triu(np.ones((tm, tm), np.float32), 1), BF16)
    return pl.pallas_call(
        _ffn_pre_kernel,
        grid=(bsz, t // tm),
        in_specs=[pl.BlockSpec((None, tm, d), lambda b, i: (b, i, 0)),
                  pl.BlockSpec((1, d), lambda b, i: (0, 0)),
                  pl.BlockSpec((None, None, 1, d), seg),
                  pl.BlockSpec((None, None, 1, d), seg),
                  pl.BlockSpec((d, LANES), lambda b, i: (0, 0)),
                  pl.BlockSpec((N_EXPERTS, 1), lambda b, i: (0, 0)),
                  pl.BlockSpec((tm, tm), lambda b, i: (0, 0))],
        out_specs=[pl.BlockSpec((None, tm, d), lambda b, i: (b, i, 0)),
                   pl.BlockSpec((None, 8, tm), lambda b, i: (b, 0, i)),
                   pl.BlockSpec((None, 8, tm), lambda b, i: (b, 0, i)),
                   pl.BlockSpec((N_EXPERTS, LANES), lambda b, i: (0, 0))],
        out_shape=[jax.ShapeDtypeStruct((bsz, t, d), F32),
                   jax.ShapeDtypeStruct((bsz, 8, t), jnp.int32),
                   jax.ShapeDtypeStruct((bsz, 8, t), F32),
                   jax.ShapeDtypeStruct((N_EXPERTS, LANES), F32)],
        scratch_shapes=[pltpu.VMEM((N_EXPERTS, LANES), F32)],
        compiler_params=_params("arbitrary", "arbitrary"),
        name="ffn_pre",
    )(h, g.reshape(1, d), shift, scale, router_w_pad, router_b.reshape(N_EXPERTS, 1), tri)


def _experts_kernel(be_ref, nb_ref, x_ref, wg_ref, wu_ref, wd_ref, o_ref, wg_s, wu_s, wd_s):
    i = pl.program_id(0)
    prev = be_ref[jnp.maximum(i - 1, 0)]
    changed = jnp.logical_or(i == 0, be_ref[i] != prev)

    @pl.when(changed)
    def _():
        wg_s[...] = wg_ref[...].astype(BF16)
        wu_s[...] = wu_ref[...].astype(BF16)
        wd_s[...] = wd_ref[...].astype(BF16)

    @pl.when(i < nb_ref[0])
    def _():
        x = x_ref[...].astype(BF16)
        hid = _silu(_dot(x, wg_s[...])) * _dot(x, wu_s[...])
        o_ref[...] = _dot(hid.astype(BF16), wd_s[...])

    @pl.when(i >= nb_ref[0])
    def _():
        o_ref[...] = jnp.zeros_like(o_ref)


def _experts(xb, block_e, n_used, w_gate, w_up, w_down):
    n_slots, d = xb.shape
    n_blocks = n_slots // MOE_BLOCK
    de = w_gate.shape[-1]
    wspec = lambda shape: pl.BlockSpec((None,) + shape, lambda i, be, nb: (be[i], 0, 0))
    return pl.pallas_call(
        _experts_kernel,
        grid_spec=pltpu.PrefetchScalarGridSpec(
            num_scalar_prefetch=2,
            grid=(n_blocks,),
            in_specs=[pl.BlockSpec((MOE_BLOCK, d), lambda i, be, nb: (i, 0)),
                      wspec((d, de)), wspec((d, de)), wspec((de, d))],
            out_specs=pl.BlockSpec((MOE_BLOCK, d), lambda i, be, nb: (i, 0)),
            scratch_shapes=[pltpu.VMEM((d, de), BF16), pltpu.VMEM((d, de), BF16), pltpu.VMEM((de, d), BF16)]),
        out_shape=jax.ShapeDtypeStruct((n_slots, d), F32),
        compiler_params=_params("arbitrary"),
        name="moe_experts",
    )(block_e, n_used, xb, w_gate, w_up, w_down)


def _ffn_post_kernel(h_ref, y0_ref, y1_ref, w_ref, gt_ref, g_ref, o_ref, *, final):
    w = w_ref[...]
    y = w[:, 0:1] * y0_ref[...] + w[:, 1:2] * y1_ref[...]
    out = h_ref[...] + gt_ref[...] * y
    o_ref[...] = _rms(out, g_ref[...]) if final else out


def _ffn_post(h, y, w, gate, n_ctx, final_g=None):
    bsz, t, d = h.shape
    tm = ROW_TILE
    seg = lambda b, i: (b, (i * tm >= n_ctx).astype(jnp.int32), 0, 0)
    row = lambda width: pl.BlockSpec((None, tm, width), lambda b, i: (b, i, 0))
    choice = lambda kk: pl.BlockSpec((None, None, tm, d), lambda b, i: (kk, b, i, 0))
    final = final_g is not None
    g = final_g if final else jnp.ones((d,), F32)
    return pl.pallas_call(
        functools.partial(_ffn_post_kernel, final=final),
        grid=(bsz, t // tm),
        in_specs=[row(d), choice(0), choice(1), row(LANES), pl.BlockSpec((None, None, 1, d), seg),
                  pl.BlockSpec((1, d), lambda b, i: (0, 0))],
        out_specs=row(d),
        out_shape=jax.ShapeDtypeStruct((bsz, t, d), F32),
        compiler_params=_params("parallel", "parallel"),
        name="ffn_post",
    )(h, y, y, w, gate, g.reshape(1, d))


def _slot_layout(n, ri, counts):
    e = jnp.swapaxes(ri[:, 0:2], 1, 2).reshape(n, TOP_K)
    rank = jnp.swapaxes(ri[:, 2:4], 1, 2).reshape(n, TOP_K)
    padded = (counts + MOE_BLOCK - 1) // MOE_BLOCK * MOE_BLOCK
    pend = jnp.cumsum(padded)
    pstart = pend - padded
    experts = jnp.arange(N_EXPERTS, dtype=jnp.int32)
    dest = rank + jnp.sum(jnp.where(e[..., None] == experts, pstart, 0), axis=-1)
    n_slots = (n * TOP_K + MOE_BLOCK - 1) // MOE_BLOCK * MOE_BLOCK + N_EXPERTS * MOE_BLOCK
    n_blocks = n_slots // MOE_BLOCK
    tok = jnp.repeat(jnp.arange(n, dtype=jnp.int32), TOP_K)
    slot_tok = jnp.zeros((n_slots,), jnp.int32).at[dest.reshape(-1)].set(tok)
    blk0 = jnp.arange(n_blocks, dtype=jnp.int32)[:, None] * MOE_BLOCK
    block_e = jnp.minimum(jnp.sum((pend[None, :] <= blk0).astype(jnp.int32), axis=-1), N_EXPERTS - 1)
    n_used = (pend[-1] // MOE_BLOCK).astype(jnp.int32).reshape(1)
    return dest, slot_tok, block_e.astype(jnp.int32), n_used


SC_CORES, SC_SUBCORES = 2, 16
SC_WINDOW = 32


def _gather_rows(table, idx):
    n_rows, d = idx.shape[0], table.shape[1]
    workers = SC_CORES * SC_SUBCORES
    per_worker = n_rows // workers
    assert per_worker * workers == n_rows and per_worker % SC_WINDOW == 0
    mesh = plsc.VectorSubcoreMesh(core_axis_name="c", subcore_axis_name="s")

    @functools.partial(
        pl.kernel, mesh=mesh,
        out_type=jax.ShapeDtypeStruct((n_rows, d), table.dtype),
        scratch_types=[pltpu.VMEM((SC_WINDOW,), jnp.int32), pltpu.VMEM((SC_WINDOW, d), table.dtype),
                       pltpu.SemaphoreType.DMA],
    )
    def gather_kernel(table_hbm, idx_hbm, out_hbm, idx_v, rows_v, sem):
        base = (lax.axis_index("s") * SC_CORES + lax.axis_index("c")) * per_worker

        @pl.loop(0, per_worker // SC_WINDOW)
        def _(j):
            off = pl.multiple_of(base + j * SC_WINDOW, 8)
            pltpu.sync_copy(idx_hbm.at[pl.ds(off, SC_WINDOW)], idx_v)
            pltpu.async_copy(table_hbm.at[idx_v], rows_v, sem).wait()
            pltpu.sync_copy(rows_v, out_hbm.at[pl.ds(off, SC_WINDOW)])

    return gather_kernel(table, idx)


def _moe(h, g, shift, scale, gate, router_w_pad, router_b, w_gate, w_up, w_down, n_ctx, final_g=None):
    bsz, t, d = h.shape
    n = bsz * t
    v, ri, rwt, counts = _ffn_pre(h, g, shift, scale, router_w_pad, router_b, n_ctx)
    dest, slot_tok, block_e, n_used = _slot_layout(n, ri, counts[:, 0].astype(jnp.int32))
    w = jnp.swapaxes(rwt[:, 0:2], 1, 2).reshape(n, TOP_K)
    yb = _experts(_gather_rows(v.reshape(n, d), slot_tok), block_e, n_used, w_gate, w_up, w_down)
    y = _gather_rows(yb, jnp.swapaxes(dest, 0, 1).reshape(-1)).reshape(TOP_K, bsz, t, d)
    wpad = jnp.zeros((n, LANES), F32).at[:, :TOP_K].set(w).reshape(bsz, t, LANES)
    return _ffn_post(h, y, wpad, gate, n_ctx, final_g)


DFT_STEP = 8


def _dft_tables(n):
    size = 2 * n
    r = int(round(math.sqrt(size)))
    assert r * r == size and r % DFT_STEP == 0
    p1 = np.arange(r // 2)[None, None, :]
    p2 = np.arange(r)[:, None, None]
    k1 = np.arange(r)[None, :, None]
    ang = 2.0 * np.pi * (((r * p1 + p2) * k1) % size) / size
    g_re, g_im = np.cos(ang), -np.sin(ang)
    g_in = np.concatenate([g_re, g_im], axis=1)
    g_out = np.concatenate([np.swapaxes(g_re, 1, 2), np.swapaxes(g_im, 1, 2)], axis=2) / size
    a2 = 2.0 * np.pi * ((np.arange(r)[:, None] * np.arange(r)[None, :]) % r) / r
    f_re, f_im = np.cos(a2), -np.sin(a2)
    f_fwd = np.block([[f_re, -f_im], [f_im, f_re]])
    f_inv = np.block([[f_re, f_im], [-f_im, f_re]])
    cast = lambda a: jnp.asarray(a.astype(np.float32)).astype(BF16)
    return r, cast(g_in), cast(g_out), cast(f_fwd), cast(f_inv)


def _hy_filter_kernel(z_ref, w1_ref, b1_ref, w2_ref, b2_ref, w3_ref, fr_ref, rates_ref, o_ref):
    z = z_ref[...]
    hid = jnp.sin(fr_ref[...] * (_dot_f32(z, w1_ref[...]) + b1_ref[...]))
    hid = jnp.sin(fr_ref[...] * (_dot_f32(hid, w2_ref[...]) + b2_ref[...]))
    filt = _dot_f32(hid, w3_ref[...])
    decay = jnp.exp(-z[:, 0:1] * rates_ref[...])
    for q in range(o_ref.shape[0]):
        o_ref[q] = filt[:, q * HY_CH:(q + 1) * HY_CH] * decay


def _hy_filters(n, w1, b1, w2, b2, w3, freq):
    t = jnp.linspace(0.0, 1.0, n, dtype=F32)[:, None]
    bands = jnp.linspace(1e-4, HY_BANDS - 1, HY_BANDS, dtype=F32)
    ang = (2.0 * math.pi / n) * jnp.arange(n, dtype=F32)[:, None] * bands
    z = jnp.concatenate([t, jnp.cos(ang), -jnp.sin(ang)], axis=-1)
    z = jnp.pad(z, ((0, 0), (0, LANES - z.shape[1])))
    w1p = jnp.pad(w1, ((0, LANES - w1.shape[0]), (0, 0)))
    rates = jnp.abs(jnp.linspace(HY_MIN_DECAY, HY_MAX_DECAY, HY_CH, dtype=F32)).reshape(1, HY_CH)
    tm = ROW_TILE
    nq = HY_ORDER * 2
    full = lambda a: pl.BlockSpec(a.shape, lambda i: (0,) * a.ndim)
    args = (z, w1p, b1.reshape(1, -1), w2, b2.reshape(1, -1), w3, freq.reshape(1, -1), rates)
    return pl.pallas_call(
        _hy_filter_kernel,
        grid=(n // tm,),
        in_specs=[pl.BlockSpec((tm, LANES), lambda i: (i, 0))] + [full(a) for a in args[1:]],
        out_specs=pl.BlockSpec((nq, tm, HY_CH), lambda i: (0, i, 0)),
        out_shape=jax.ShapeDtypeStruct((nq, n, HY_CH), F32),
        compiler_params=_params("parallel"),
        name="hy_filters",
    )(*args)


def _dft_in_kernel(x_ref, g_ref, a_ref):
    for j in range(DFT_STEP):
        a_ref[j] = _dot(g_ref[j], x_ref[:, j, :].astype(BF16)).astype(BF16)


def _dft_in(x4, col, g_in):
    bx, rh, r, _ = x4.shape
    c = HY_CH
    return pl.pallas_call(
        _dft_in_kernel,
        grid=(bx, r // DFT_STEP),
        in_specs=[pl.BlockSpec((None, rh, DFT_STEP, c), lambda b, i: (b, 0, i, col)),
                  pl.BlockSpec((DFT_STEP, 2 * r, rh), lambda b, i: (i, 0, 0))],
        out_specs=pl.BlockSpec((None, DFT_STEP, 2 * r, c), lambda b, i: (b, i, 0, 0)),
        out_shape=jax.ShapeDtypeStruct((bx, r, 2 * r, c), BF16),
        compiler_params=_params("parallel", "parallel"),
        name="dft_in",
    )(x4, g_in)


def _stage2(a_ref, b, j, f_ref):
    a = jnp.concatenate([a_ref[b, :, 0, j, :], a_ref[b, :, 1, j, :]], axis=0)
    return _dot(f_ref[...], a)


def _dft_filt_kernel(a_ref, f_ref, k_ref):
    r = f_ref.shape[0] // 2
    for j in range(DFT_STEP):
        sf, sb = _stage2(a_ref, 0, j, f_ref), _stage2(a_ref, 1, j, f_ref)
        k_ref[j, 0:r, :] = sf[0:r] + sb[0:r]
        k_ref[j, r:2 * r, :] = sf[r:2 * r] - sb[r:2 * r]


def _dft_filt(a, f_fwd):
    nq, r, _, c = a.shape
    a5 = a.reshape(nq, r, 2, r, c)
    return pl.pallas_call(
        _dft_filt_kernel,
        grid=(nq // 2, r // DFT_STEP),
        in_specs=[pl.BlockSpec((2, r, 2, DFT_STEP, c), lambda o, i: (o, 0, 0, i, 0)),
                  pl.BlockSpec(f_fwd.shape, lambda o, i: (0, 0))],
        out_specs=pl.BlockSpec((None, DFT_STEP, 2 * r, c), lambda o, i: (o, i, 0, 0)),
        out_shape=jax.ShapeDtypeStruct((nq // 2, r, 2 * r, c), F32),
        compiler_params=_params("parallel", "parallel"),
        name="dft_filt",
    )(a5, f_fwd)


def _dft_mid_kernel(a_ref, k_ref, ff_ref, fi_ref, b_ref):
    r = ff_ref.shape[0] // 2
    for b in range(a_ref.shape[0]):
        for j in range(DFT_STEP):
            s = _stage2(a_ref, b, j, ff_ref)
            sr, si = s[0:r], s[r:2 * r]
            kr, ki = k_ref[j, 0:r, :], k_ref[j, r:2 * r, :]
            p = jnp.concatenate([sr * kr - si * ki, sr * ki + si * kr], axis=0).astype(BF16)
            b_ref[b, j] = _dot(fi_ref[...], p).astype(BF16)


def _dft_mid(a, kspec, order, f_fwd, f_inv):
    bsz, r, _, c = a.shape
    a5 = a.reshape(bsz, r, 2, r, c)
    return pl.pallas_call(
        _dft_mid_kernel,
        grid=(r // DFT_STEP,),
        in_specs=[pl.BlockSpec((bsz, r, 2, DFT_STEP, c), lambda i: (0, 0, 0, i, 0)),
                  pl.BlockSpec((None, DFT_STEP, 2 * r, c), lambda i: (order, i, 0, 0)),
                  pl.BlockSpec(f_fwd.shape, lambda i: (0, 0)),
                  pl.BlockSpec(f_inv.shape, lambda i: (0, 0))],
        out_specs=pl.BlockSpec((bsz, DFT_STEP, 2 * r, c), lambda i: (0, i, 0, 0)),
        out_shape=jax.ShapeDtypeStruct((bsz, r, 2 * r, c), BF16),
        compiler_params=_params("parallel"),
        name="dft_mid",
    )(a5, kspec, f_fwd, f_inv)


def _dft_out_kernel(b_ref, g_ref, u_ref, x_ref, bias_ref, o_ref):
    for j in range(DFT_STEP):
        rhs = jnp.concatenate([b_ref[:, 0, j, :], b_ref[:, 1, j, :]], axis=0)
        y = _dot(g_ref[j], rhs)
        o_ref[:, j, :] = x_ref[:, j, :] * (y + u_ref[:, j, :] * bias_ref[...])


def _dft_out(bm, g_out, u4, u_col, x4, x_col, bias):
    bsz, r, _, c = bm.shape
    rh = r // 2
    b5 = bm.reshape(bsz, r, 2, r, c)
    seq = lambda col: pl.BlockSpec((None, rh, DFT_STEP, c), lambda b, i: (b, 0, i, col))
    return pl.pallas_call(
        _dft_out_kernel,
        grid=(bsz, r // DFT_STEP),
        in_specs=[pl.BlockSpec((None, r, 2, DFT_STEP, c), lambda b, i: (b, 0, 0, i, 0)),
                  pl.BlockSpec((DFT_STEP, rh, 2 * r), lambda b, i: (i, 0, 0)),
                  seq(u_col), seq(x_col), pl.BlockSpec((1, c), lambda b, i: (0, 0))],
        out_specs=seq(0),
        out_shape=jax.ShapeDtypeStruct((bsz, rh, r, c), F32),
        compiler_params=_params("parallel", "parallel"),
        name="dft_out",
    )(b5, g_out, u4, x4, bias.reshape(1, c))


def _hyena(hy_in, fw1, fb1, fw2, fb2, fw3, freq, conv_bias):
    bsz, n, _ = hy_in.shape
    r, g_in, g_out, f_fwd, f_inv = _dft_tables(n)
    filt = _hy_filters(n, fw1, fb1, fw2, fb2, fw3, freq)
    kspec = _dft_filt(_dft_in(filt.reshape(-1, r // 2, r, HY_CH), 0, g_in), f_fwd)
    seq4 = hy_in.reshape(bsz, r // 2, r, 3 * HY_CH)
    zz = _dft_out(_dft_mid(_dft_in(seq4, 0, g_in), kspec, 0, f_fwd, f_inv), g_out,
                  seq4, 0, seq4, 1, conv_bias[0])
    out = _dft_out(_dft_mid(_dft_in(zz, 0, g_in), kspec, 1, f_fwd, f_inv), g_out,
                   zz, 0, seq4, 2, conv_bias[1])
    return out.reshape(bsz, n, HY_CH)


CD_HY, CD_Z, CD_XBC, CD_DT = 0, 1536, 2048, 3072
CD_PAD_COLS = CD_DT + 2 * LANES


def _reorder_ab(w):
    gq, gk, gv, gg, lr_f, lr_b, hq, hf_f, hf_b, hi, hg = jnp.split(
        w, np.cumsum([256, 256, 512, 512, 16, 16, 512, 512, 512, 512, 512])[:-1].tolist(), axis=-1)
    pad = jnp.zeros((w.shape[0], AB_PAD_COLS - AB_LR - 2 * GLA_LOW_RANK), w.dtype)
    return jnp.concatenate([gq, gk, gv, gg, hq, hf_f, hf_b, hi, hg, lr_f, lr_b, pad], axis=-1)


def _reorder_cd(w):
    hy, z, xbc, dt_f, dt_b = jnp.split(w, np.cumsum([1536, 512, 1024, 8, 8])[:-1].tolist(), axis=-1)
    pad = jnp.zeros((w.shape[0], LANES - MB_HEADS), w.dtype)
    return jnp.concatenate([hy, z, xbc, dt_f, pad, dt_b, pad], axis=-1)


def kernel(x, c, ctx, c_ctx, ada_w, ada_b, norm_mix_g, norm_ffn_g, norm_out_g, ab_w_in, ab_w_out, gla_gate_w, gla_gate_b, gla_norm_g, hg_lb, hg_norm_g, cd_w_in, cd_w_out, hy_short_w, hy_short_b, hy_w1, hy_b1, hy_w2, hy_b2, hy_w3, hy_freq, hy_bias, mb_conv_w, mb_conv_b, mb_dt_bias, mb_a_log, mb_d, mb_norm_g, router_w, router_b, moe_w_gate, moe_w_up, moe_w_down):
    bsz, n_lat, d = x.shape
    n_ctx = ctx.shape[1]
    t = n_ctx + n_lat
    assert ada_w.shape[0] == 2 and ab_w_in.shape[0] == 1 and cd_w_in.shape[0] == 1

    cond = jnp.zeros((8, d), F32).at[:bsz].set(c).at[bsz].set(c_ctx)
    m = _adaln(cond, ada_w, ada_b)

    def mods(layer):
        lat = m[layer, :bsz].reshape(bsz, 6, d)
        cx = jnp.broadcast_to(m[layer, bsz].reshape(1, 6, d), (bsz, 6, d))
        both = jnp.stack([cx, lat], axis=1)
        return [both[:, :, j][:, :, None, :] for j in range(6)]

    lb_all = jnp.cumsum(jax.nn.softmax(hg_lb.astype(F32), axis=1), axis=1)
    router_w_pad = jnp.zeros((d, LANES), F32).at[:, :N_EXPERTS].set(router_w)
    h = jnp.concatenate([ctx, x], axis=1)

    sh_m, sc_m, gt_m, sh_f, sc_f, gt_f = mods(0)
    proj = _norm_proj(h, norm_mix_g[0], sh_m, sc_m, _reorder_ab(ab_w_in[0]).astype(BF16), n_ctx)
    gwp = [jnp.zeros((LANES, GLA_KEY_W), F32).at[GLA_LOW_RANK * dd:GLA_LOW_RANK * (dd + 1)].set(gla_gate_w[0, dd])
           for dd in range(2)]
    o_gla = _gla_scan(proj, gwp, [gla_gate_b[0, dd].reshape(1, -1) for dd in range(2)], n_ctx)
    o_hg = _hgrn_scan(proj, [lb_all[dd, 0].reshape(1, -1) for dd in range(2)], n_ctx)
    h = _mix_out_ab(h, o_gla, o_hg, proj, gla_norm_g[0], hg_norm_g[0], ab_w_out[0].astype(BF16), gt_m, n_ctx)
    h = _moe(h, norm_ffn_g[0], sh_f, sc_f, gt_f, router_w_pad, router_b,
             moe_w_gate[0], moe_w_up[0], moe_w_down[0], n_ctx)

    sh_m, sc_m, gt_m, sh_f, sc_f, gt_f = mods(1)
    proj = _norm_proj(h, norm_mix_g[1], sh_m, sc_m, _reorder_cd(cd_w_in[0]).astype(BF16), n_ctx)
    hy_in = _dwconv(proj, CD_HY, 3 * HY_CH, hy_short_w[0], hy_short_b[0], (n_ctx, t), act=False)
    hy = _hyena(hy_in, hy_w1[0], hy_b1[0], hy_w2[0], hy_b2[0], hy_w3[0], hy_freq[0], hy_bias[0])
    xbc = _dwconv(proj, CD_XBC, MB_INNER + 2 * MB_BC_W, mb_conv_w[0], mb_conv_b[0], (0, n_ctx, t), act=True)
    y_ssd = [_ssd_scan(xbc, proj, CD_DT // LANES + dd, mb_dt_bias[0, dd], mb_a_log[0, dd], n_ctx, dd == 1)
             for dd in range(2)]
    d_skip_x = jnp.repeat(mb_d[0], MB_HEAD_DIM).reshape(1, MB_INNER)
    h = _mix_out_cd(h, hy, y_ssd, xbc, proj, CD_Z // MB_INNER, d_skip_x, mb_norm_g[0],
                    cd_w_out[0].astype(BF16), gt_m, n_ctx)
    return _moe(h, norm_ffn_g[1], sh_f, sc_f, gt_f, router_w_pad, router_b,
                moe_w_gate[1], moe_w_up[1], moe_w_down[1], 0, final_g=norm_out_g)
```

```python
import functools
import math

import numpy as np
import jax
import jax.numpy as jnp
from jax import lax
from jax.experimental import pallas as pl
from jax.experimental.pallas import tpu as pltpu
from jax.experimental.pallas import tpu_sc as plsc

NORM_EPS = 1e-6
GLA_HEADS, GLA_DK, GLA_DV, GLA_LOW_RANK, GLA_TAU = 4, 64, 128, 16, 16.0
GLA_KEY_W, GLA_VAL_W = GLA_HEADS * GLA_DK, GLA_HEADS * GLA_DV
HG_HEADS, HG_EXPAND, HG_DV = 4, 128, 128
HG_KEY_W, HG_VAL_W = HG_HEADS * HG_EXPAND, HG_HEADS * HG_DV
HY_CH, HY_ORDER, HY_SHORT, HY_BANDS, HY_FILT_HID = 512, 2, 3, 16, 64
HY_MIN_DECAY = math.log(1e-2) / 1.5
HY_MAX_DECAY = math.log(1e-2) / 0.3
MB_HEADS, MB_HEAD_DIM, MB_GROUPS, MB_STATE = 8, 64, 2, 128
MB_INNER = MB_HEADS * MB_HEAD_DIM
MB_BC_W = MB_GROUPS * MB_STATE
N_EXPERTS, N_GROUPS, TOP_K, MOE_BLOCK = 16, 4, 2, 256
EXPERTS_PER_GROUP = N_EXPERTS // N_GROUPS

LANES = 128
SCAN_CHUNK = 64
SCAN_BLOCK = 128
SSD_CHUNK = 128
ROW_TILE = 256
VMEM_LIMIT = 56 * 1024 * 1024

BF16 = jnp.bfloat16
F32 = jnp.float32


def _params(*sem):
    return pltpu.CompilerParams(dimension_semantics=sem, vmem_limit_bytes=VMEM_LIMIT)


def _split3(x):
    hi = x.astype(BF16)
    r1 = x - hi.astype(F32)
    mid = r1.astype(BF16)
    lo = (r1 - mid.astype(F32)).astype(BF16)
    return hi, mid, lo


def _dot(a, b):
    return jnp.dot(a, b, preferred_element_type=F32)


def _dot_nt(a, b):
    return lax.dot_general(a, b, (((1,), (1,)), ((), ())), preferred_element_type=F32)


def _dot_tn(a, b):
    return lax.dot_general(a, b, (((0,), (0,)), ((), ())), preferred_element_type=F32)


def _sel_dot(m01, x):
    hi, mid, lo = _split3(x)
    return _dot(m01, hi) + (_dot(m01, mid) + _dot(m01, lo))


def _dot_sel(x, m01):
    hi, mid, lo = _split3(x)
    return _dot(hi, m01) + (_dot(mid, m01) + _dot(lo, m01))


def _dot_f32(a, b):
    ah = a.astype(BF16)
    al = (a - ah.astype(F32)).astype(BF16)
    bh = b.astype(BF16)
    bl = (b - bh.astype(F32)).astype(BF16)
    return _dot(ah, bh) + (_dot(ah, bl) + _dot(al, bh))


def _silu(x):
    return x * (1.0 / (1.0 + jnp.exp(-x)))


def _sigmoid(x):
    return 1.0 / (1.0 + jnp.exp(-x))


def _softplus(x):
    return jnp.maximum(x, 0.0) + jnp.log(1.0 + jnp.exp(-jnp.abs(x)))


def _rms(x, g):
    return x * lax.rsqrt(jnp.mean(x * x, axis=-1, keepdims=True) + NORM_EPS) * g


def _adaln_kernel(c_ref, w_ref, b_ref, o_ref):
    o_ref[...] = _dot_f32(_silu(c_ref[...]), w_ref[...]) + b_ref[...]


def _adaln(cond, w, b):
    n_l, d, n6 = w.shape
    tn = 1536
    return pl.pallas_call(
        _adaln_kernel,
        grid=(n_l, n6 // tn),
        in_specs=[pl.BlockSpec((8, d), lambda l, j: (0, 0)),
                  pl.BlockSpec((None, d, tn), lambda l, j: (l, 0, j)),
                  pl.BlockSpec((None, 1, tn), lambda l, j: (l, 0, j))],
        out_specs=pl.BlockSpec((None, 8, tn), lambda l, j: (l, 0, j)),
        out_shape=jax.ShapeDtypeStruct((n_l, 8, n6), F32),
        compiler_params=_params("parallel", "parallel"),
        name="adaln",
    )(cond, w, b.reshape(n_l, 1, n6))


def _norm_proj_kernel(h_ref, g_ref, sh_ref, sc_ref, w_ref, o_ref):
    u = _rms(h_ref[...], g_ref[...]) * (1.0 + sc_ref[...]) + sh_ref[...]
    o_ref[...] = _dot(u.astype(BF16), w_ref[...])


def _norm_proj(h, g, shift, scale, w, n_ctx):
    bsz, t, d = h.shape
    n = w.shape[1]
    tm = ROW_TILE
    seg = lambda b, i: (b, (i * tm >= n_ctx).astype(jnp.int32), 0, 0)
    return pl.pallas_call(
        _norm_proj_kernel,
        grid=(bsz, t // tm),
        in_specs=[pl.BlockSpec((None, tm, d), lambda b, i: (b, i, 0)),
                  pl.BlockSpec((1, d), lambda b, i: (0, 0)),
                  pl.BlockSpec((None, None, 1, d), seg),
                  pl.BlockSpec((None, None, 1, d), seg),
                  pl.BlockSpec((d, n), lambda b, i: (0, 0))],
        out_specs=pl.BlockSpec((None, tm, n), lambda b, i: (b, i, 0)),
        out_shape=jax.ShapeDtypeStruct((bsz, t, n), F32),
        compiler_params=_params("parallel", "parallel"),
        name="norm_proj",
    )(h, g.reshape(1, d), shift, scale, w)


def _scan_constants(c, reverse):
    t = np.arange(c)[:, None]
    u = np.arange(c)[None, :]
    sels = [u <= t, u > t]
    masks = []
    m = c // 2
    while m >= 1:
        blk = t // (2 * m)
        upper_t = (t % (2 * m)) >= m
        r = blk * (2 * m) + m - 1
        s_blk = u // (2 * m)
        upper_s = (u % (2 * m)) >= m
        sels.append((upper_t & (u > r) & (u <= t)) | ((~upper_t) & (u > t) & (u <= r)))
        masks.append((blk == s_blk) & upper_t & (~upper_s))
        m //= 2
    masks.append(t == u)
    sel = np.stack(sels).astype(np.float32)
    msk = np.stack(masks).astype(np.float32)
    if reverse:
        sel = sel[:, ::-1, ::-1]
        msk = msk[:, ::-1, ::-1]
    return np.ascontiguousarray(sel.reshape(-1, c)), np.ascontiguousarray(msk)


def _chunk_order(i, n_ctx_chunks, n_chunks, reverse):
    if not reverse:
        return i
    return jnp.where(i < n_ctx_chunks, n_ctx_chunks - 1 - i, n_chunks - 1 - (i - n_ctx_chunks))


GROUP_KEYS = 256


def _decay_chunk(q, k, v, la, sel_ref, mask_ref, hm_ref, st_ref, heads, dk, dv):
    c = q.shape[0]
    n_lvl = mask_ref.shape[0] - 1
    hpg = GROUP_KEYS // dk
    cs = _dot(sel_ref[...], jnp.concatenate(_split3(la), axis=0))
    e_q = jnp.exp(cs[0:c])
    e_k = jnp.exp(cs[c:2 * c])
    e_tot = jnp.exp(jnp.sum(la, axis=0, keepdims=True))
    vb = v.astype(BF16)
    outs = []
    for g in range(heads // hpg):
        ks = slice(g * GROUP_KEYS, (g + 1) * GROUP_KEYS)
        vs = slice(g * hpg * dv, (g + 1) * hpg * dv)
        qg, kg = q[:, ks], k[:, ks]
        stack = lambda x: jnp.concatenate([x * hm_ref[h] for h in range(hpg)], axis=0).astype(BF16)
        att = mask_ref[n_lvl] * _dot_nt(stack(qg), kg.astype(BF16))
        for l in range(n_lvl):
            e = jnp.exp(cs[(2 + l) * c:(3 + l) * c, ks])
            att = att + mask_ref[l] * _dot_nt(stack(qg * e), (kg * e).astype(BF16))
        att = att.astype(BF16)
        st = st_ref[g]
        inter = _dot_nt(stack(qg * e_q[:, ks]), st.astype(BF16))
        upd = _dot_tn(vb[:, vs], (kg * e_k[:, ks]).astype(BF16))
        new = st * e_tot[:, ks]
        for h in range(hpg):
            rows = slice(h * c, (h + 1) * c)
            hv = slice(g * hpg * dv + h * dv, g * hpg * dv + (h + 1) * dv)
            outs.append(_dot(att[rows], vb[:, hv]) + inter[rows])
            new = new + upd[h * dv:(h + 1) * dv] * hm_ref[h]
        st_ref[g] = new
    return jnp.concatenate(outs, axis=-1)


def _log_sigmoid(x):
    return jnp.minimum(x, 0.0) - jnp.log(1.0 + jnp.exp(-jnp.abs(x)))


def _gla_kernel(*refs):
    ins, hm_ref, (o_refs, st_refs) = (refs[0:8], refs[8:16]), refs[16], (refs[17:19], refs[19:21])

    @pl.when(pl.program_id(1) == 0)
    def _():
        for st_ref in st_refs:
            st_ref[...] = jnp.zeros_like(st_ref)

    for d, ((q_ref, k_ref, v_ref, lr_ref, gw_ref, gb_ref, sel_ref, mask_ref), o_ref, st_ref) in enumerate(
            zip(ins, o_refs, st_refs)):
        z = _dot_f32(lr_ref[...], gw_ref[...]) + gb_ref[...]
        la = _log_sigmoid(z) * (1.0 / GLA_TAU)
        q = q_ref[...] * (GLA_DK ** -0.5)
        k, v = k_ref[...], v_ref[...]
        for rows in _sub_chunks(q.shape[0], d == 1):
            o_ref[rows, :] = _decay_chunk(q[rows], k[rows], v[rows], la[rows], sel_ref, mask_ref, hm_ref,
                                          st_ref, GLA_HEADS, GLA_DK, GLA_DV)


def _hgrn_kernel(*refs):
    ins, hm_ref, (o_refs, st_refs) = (refs[0:6], refs[6:12]), refs[12], (refs[13:15], refs[15:17])

    @pl.when(pl.program_id(1) == 0)
    def _():
        for st_ref in st_refs:
            st_ref[...] = jnp.zeros_like(st_ref)

    for d, ((q_ref, f_ref, v_ref, lb_ref, sel_ref, mask_ref), o_ref, st_ref) in enumerate(
            zip(ins, o_refs, st_refs)):
        lb = lb_ref[...]
        f = lb + (1.0 - lb) * _sigmoid(f_ref[...])
        q, k, v, la = _silu(q_ref[...]), 1.0 - f, v_ref[...], jnp.log(f)
        for rows in _sub_chunks(q.shape[0], d == 1):
            o_ref[rows, :] = _decay_chunk(q[rows], k[rows], v[rows], la[rows], sel_ref, mask_ref, hm_ref,
                                          st_ref, HG_HEADS, HG_EXPAND, HG_DV)


def _sub_chunks(rows, reverse):
    order = range(rows // SCAN_CHUNK)
    return [slice(j * SCAN_CHUNK, (j + 1) * SCAN_CHUNK) for j in (reversed(order) if reverse else order)]


def _scan_specs(blk, n_ctx, t, reverse, chunk=None, stacked_heads=1):
    n_blocks = t // blk
    order = functools.partial(_chunk_order, n_ctx_chunks=n_ctx // blk, n_chunks=n_blocks, reverse=reverse)

    def col(width, idx):
        return pl.BlockSpec((None, blk, width), lambda b, i: (b, order(i), idx))

    sel, msk = _scan_constants(chunk or blk, reverse)
    sel3 = np.concatenate([sel, sel, sel], axis=1)
    msk = np.tile(msk, (1, stacked_heads, 1))
    const = lambda a: pl.BlockSpec(a.shape, lambda b, i: (0,) * a.ndim)
    return n_blocks, col, const, jnp.asarray(sel3, BF16), jnp.asarray(msk, F32)


def _head_masks(dk):
    hpg = GROUP_KEYS // dk
    hm = np.zeros((hpg, 1, GROUP_KEYS), np.float32)
    for h in range(hpg):
        hm[h, 0, h * dk:(h + 1) * dk] = 1.0
    return jnp.asarray(hm)


AB_Q, AB_K, AB_V, AB_G = 0, 256, 512, 1024
AB_HQ, AB_HF, AB_HI, AB_HG, AB_LR = 1536, 2048, 3072, 3584, 4096
AB_PAD_COLS = 4224


def _gla_scan(proj, gate_w_pad, gate_b, n_ctx):
    bsz, t, _ = proj.shape
    hpg = GROUP_KEYS // GLA_DK
    in_specs, args, outs = [], [], []
    for d in range(2):
        n_blocks, col, const, sel, msk = _scan_specs(SCAN_BLOCK, n_ctx, t, d == 1, SCAN_CHUNK, hpg)
        in_specs += [col(GLA_KEY_W, AB_Q // GLA_KEY_W), col(GLA_KEY_W, AB_K // GLA_KEY_W),
                     col(GLA_VAL_W, AB_V // GLA_VAL_W), col(LANES, AB_LR // LANES),
                     const(gate_w_pad[d]), const(gate_b[d]), const(sel), const(msk)]
        args += [proj, proj, proj, proj, gate_w_pad[d], gate_b[d], sel, msk]
        outs.append(col(GLA_VAL_W, 0))
    hm = _head_masks(GLA_DK)
    return pl.pallas_call(
        _gla_kernel,
        grid=(bsz, n_blocks),
        in_specs=in_specs + [const(hm)],
        out_specs=outs,
        out_shape=[jax.ShapeDtypeStruct((bsz, t, GLA_VAL_W), F32)] * 2,
        scratch_shapes=[pltpu.VMEM((GLA_HEADS // hpg, GLA_DV, GROUP_KEYS), F32)] * 2,
        compiler_params=_params("parallel", "arbitrary"),
        name="gla_scan",
    )(*args, hm)


def _hgrn_scan(proj, lb, n_ctx):
    bsz, t, _ = proj.shape
    hpg = GROUP_KEYS // HG_EXPAND
    in_specs, args, outs = [], [], []
    for d in range(2):
        n_blocks, col, const, sel, msk = _scan_specs(SCAN_BLOCK, n_ctx, t, d == 1, SCAN_CHUNK, hpg)
        in_specs += [col(HG_KEY_W, AB_HQ // HG_KEY_W), col(HG_KEY_W, AB_HF // HG_KEY_W + d),
                     col(HG_VAL_W, AB_HI // HG_VAL_W), const(lb[d]), const(sel), const(msk)]
        args += [proj, proj, proj, lb[d], sel, msk]
        outs.append(col(HG_VAL_W, 0))
    hm = _head_masks(HG_EXPAND)
    return pl.pallas_call(
        _hgrn_kernel,
        grid=(bsz, n_blocks),
        in_specs=in_specs + [const(hm)],
        out_specs=outs,
        out_shape=[jax.ShapeDtypeStruct((bsz, t, HG_VAL_W), F32)] * 2,
        scratch_shapes=[pltpu.VMEM((HG_HEADS // hpg, HG_DV, GROUP_KEYS), F32)] * 2,
        compiler_params=_params("parallel", "arbitrary"),
        name="hgrn_scan",
    )(*args, hm)


def _mix_out_ab_kernel(h_ref, gf_ref, gb_ref, hf_ref, hb_ref, gg_ref, hg_ref, gn_ref, hn_ref,
                       w_ref, gt_ref, o_ref):
    feats = []
    for o, gate, g in ((gf_ref[...] + gb_ref[...], gg_ref[...], gn_ref[...]),
                       (hf_ref[...] + hb_ref[...], hg_ref[...], hn_ref[...])):
        for hd in range(o.shape[-1] // LANES):
            s = slice(hd * LANES, (hd + 1) * LANES)
            feats.append(_rms(o[:, s], g) * _silu(gate[:, s]))
    feat = jnp.concatenate(feats, axis=-1).astype(BF16)
    o_ref[...] = h_ref[...] + gt_ref[...] * _dot(feat, w_ref[...])


def _mix_out_ab(h, o_gla, o_hg, proj, gla_norm_g, hg_norm_g, w_out, gate, n_ctx):
    bsz, t, d = h.shape
    tm = ROW_TILE
    seg = lambda b, i: (b, (i * tm >= n_ctx).astype(jnp.int32), 0, 0)
    row = lambda width, idx: pl.BlockSpec((None, tm, width), lambda b, i: (b, i, idx))
    vec = pl.BlockSpec((1, LANES), lambda b, i: (0, 0))
    return pl.pallas_call(
        _mix_out_ab_kernel,
        grid=(bsz, t // tm),
        in_specs=[row(d, 0), row(GLA_VAL_W, 0), row(GLA_VAL_W, 0), row(HG_VAL_W, 0), row(HG_VAL_W, 0),
                  row(GLA_VAL_W, AB_G // GLA_VAL_W), row(HG_VAL_W, AB_HG // HG_VAL_W), vec, vec,
                  pl.BlockSpec(w_out.shape, lambda b, i: (0, 0)),
                  pl.BlockSpec((None, None, 1, d), seg)],
        out_specs=row(d, 0),
        out_shape=jax.ShapeDtypeStruct((bsz, t, d), F32),
        compiler_params=_params("parallel", "parallel"),
        name="mix_out_ab",
    )(h, o_gla[0], o_gla[1], o_hg[0], o_hg[1], proj, proj, gla_norm_g.reshape(1, -1),
      hg_norm_g.reshape(1, -1), w_out, gate)


def _dwconv_kernel(x_ref, prev_ref, next_ref, w_ref, b_ref, o_ref, *, seg_tiles, act):
    i = pl.program_id(1)
    tm = x_ref.shape[0]
    x = x_ref[...]
    first = functools.reduce(jnp.logical_or, [i == s for s in seg_tiles[:-1]])
    last = functools.reduce(jnp.logical_or, [i == s - 1 for s in seg_tiles[1:]])
    prev_row = jnp.where(first, 0.0, prev_ref[7:8, :])
    next_row = jnp.where(last, 0.0, next_ref[0:1, :])
    rows = lax.broadcasted_iota(jnp.int32, x.shape, 0)
    x_prev = jnp.where(rows == 0, prev_row, pltpu.roll(x, 1, axis=0))
    x_next = jnp.where(rows == tm - 1, next_row, pltpu.roll(x, tm - 1, axis=0))
    y = x_prev * w_ref[0:1, :] + x * w_ref[1:2, :] + x_next * w_ref[2:3, :] + b_ref[...]
    o_ref[...] = _silu(y) if act else y


def _dwconv(proj, col0, width, w, b, seg_bounds, act):
    bsz, t, _ = proj.shape
    tm = ROW_TILE
    off = seg_bounds[0] // tm
    n_tiles = t // tm - off
    seg_tiles = tuple(s // tm - off for s in seg_bounds)
    cb = col0 // width
    r8 = tm // 8
    last8 = t // 8 - 1
    kern = functools.partial(_dwconv_kernel, seg_tiles=seg_tiles, act=act)
    return pl.pallas_call(
        kern,
        grid=(bsz, n_tiles),
        in_specs=[pl.BlockSpec((None, tm, width), lambda bb, i: (bb, i + off, cb)),
                  pl.BlockSpec((None, 8, width), lambda bb, i: (bb, jnp.maximum((i + off) * r8 - 1, 0), cb)),
                  pl.BlockSpec((None, 8, width),
                               lambda bb, i: (bb, jnp.minimum((i + off + 1) * r8, last8), cb)),
                  pl.BlockSpec((3, width), lambda bb, i: (0, 0)),
                  pl.BlockSpec((1, width), lambda bb, i: (0, 0))],
        out_specs=pl.BlockSpec((None, tm, width), lambda bb, i: (bb, i, 0)),
        out_shape=jax.ShapeDtypeStruct((bsz, n_tiles * tm, width), F32),
        compiler_params=_params("parallel", "parallel"),
        name="dwconv",
    )(proj, proj, proj, w.T, b.reshape(1, -1))


def _ssd_kernel(xbc_ref, dt_ref, bias_ref, alog_ref, hexp_ref, mq_ref, mk_ref, mask_ref, o_ref, st_ref):
    @pl.when(pl.program_id(1) == 0)
    def _():
        st_ref[...] = jnp.zeros_like(st_ref)

    c = xbc_ref.shape[0]
    hpg = MB_HEADS // MB_GROUPS
    gw = hpg * MB_HEAD_DIM
    dt = _softplus(dt_ref[...] + bias_ref[...])
    la = -dt * jnp.exp(alog_ref[...])
    cq = _sel_dot(mq_ref[...], la)
    ck = _sel_dot(mk_ref[...], la)
    cq_t = _dot_nt_sel(la, mq_ref[...])
    hexp = hexp_ref[...]
    dt_x = _dot_sel(dt, hexp)
    eq_x = jnp.exp(_dot_sel(cq, hexp))
    ek_x = jnp.exp(_dot_sel(ck, hexp))
    etot_x = jnp.exp(_dot_sel(jnp.sum(la, axis=0, keepdims=True), hexp))
    xs = xbc_ref[:, 0:MB_INNER] * dt_x
    mask = mask_ref[...]
    outs = []
    for g in range(MB_GROUPS):
        bm = xbc_ref[:, MB_INNER + g * MB_STATE:MB_INNER + (g + 1) * MB_STATE].astype(BF16)
        cm = xbc_ref[:, MB_INNER + MB_BC_W + g * MB_STATE:MB_INNER + MB_BC_W + (g + 1) * MB_STATE].astype(BF16)
        cb = _dot_nt(cm, bm)
        st = st_ref[g]
        gs = slice(g * gw, (g + 1) * gw)
        y_inter = _dot(cm, st.astype(BF16)) * eq_x[:, gs]
        for r in range(hpg):
            hd = g * hpg + r
            diff = cq[:, hd:hd + 1] - cq_t[hd:hd + 1, :]
            w = cb * jnp.exp(jnp.where(mask > 0.0, diff, -jnp.inf))
            ps = slice(hd * MB_HEAD_DIM, (hd + 1) * MB_HEAD_DIM)
            outs.append(_dot(w.astype(BF16), xs[:, ps].astype(BF16))
                        + y_inter[:, r * MB_HEAD_DIM:(r + 1) * MB_HEAD_DIM])
        st_ref[g] = st * etot_x[:, gs] + _dot_tn(bm, (xs[:, gs] * ek_x[:, gs]).astype(BF16))
    o_ref[...] = jnp.concatenate(outs, axis=-1)


def _dot_nt_sel(x, m01):
    hi, mid, lo = _split3(x)
    f = lambda p: lax.dot_general(p, m01, (((0,), (1,)), ((), ())), preferred_element_type=F32)
    return f(hi) + (f(mid) + f(lo))


def _ssd_scan(xbc, proj, dt_col, dt_bias, a_log, n_ctx, reverse):
    bsz, t, _ = xbc.shape
    c = SSD_CHUNK
    n_chunks, col, const, sel, msk = _scan_specs(c, n_ctx, t, reverse)
    mq, mk = sel[0:c, 0:c], sel[c:2 * c, 0:c]
    tri = np.tril(np.ones((c, c), np.float32))
    mask = jnp.asarray(tri[::-1, ::-1].copy() if reverse else tri)
    pad = lambda v: jnp.zeros((1, LANES), F32).at[0, :MB_HEADS].set(v)
    hexp = np.zeros((LANES, MB_INNER), np.float32)
    for hd in range(MB_HEADS):
        hexp[hd, hd * MB_HEAD_DIM:(hd + 1) * MB_HEAD_DIM] = 1.0
    hexp = jnp.asarray(hexp, BF16)
    bias = pad(dt_bias)
    alog = pad(a_log.astype(F32))
    return pl.pallas_call(
        _ssd_kernel,
        grid=(bsz, n_chunks),
        in_specs=[col(xbc.shape[-1], 0), col(LANES, dt_col), const(bias), const(alog), const(hexp),
                  const(mq), const(mk), const(mask)],
        out_specs=col(MB_INNER, 0),
        out_shape=jax.ShapeDtypeStruct((bsz, t, MB_INNER), F32),
        scratch_shapes=[pltpu.VMEM((MB_GROUPS, MB_STATE, MB_INNER // MB_GROUPS), F32)],
        compiler_params=_params("parallel", "arbitrary"),
        name="ssd_scan_rev" if reverse else "ssd_scan_fwd",
    )(xbc, proj, bias, alog, hexp, mq, mk, mask)


def _mix_out_cd_kernel(h_ref, hy_ref, yf_ref, yb_ref, xs_ref, z_ref, dsk_ref, ng_ref, w_ref, gt_ref, o_ref):
    y = (yf_ref[...] + yb_ref[...] + dsk_ref[...] * xs_ref[...]) * _silu(z_ref[...])
    gw = MB_INNER // MB_GROUPS
    ys = [_rms(y[:, g * gw:(g + 1) * gw], ng_ref[:, g * gw:(g + 1) * gw]) for g in range(MB_GROUPS)]
    feat = jnp.concatenate([hy_ref[...]] + ys, axis=-1).astype(BF16)
    o_ref[...] = h_ref[...] + gt_ref[...] * _dot(feat, w_ref[...])


def _mix_out_cd(h, hy, y_ssd, xbc, proj, z_col, d_skip_x, norm_g, w_out, gate, n_ctx):
    bsz, t, d = h.shape
    tm = ROW_TILE
    n_lat = t - n_ctx
    off = n_ctx // tm
    row = lambda width, idx: pl.BlockSpec((None, tm, width), lambda b, i: (b, i + off, idx))
    vec = pl.BlockSpec((1, MB_INNER), lambda b, i: (0, 0))
    return pl.pallas_call(
        _mix_out_cd_kernel,
        grid=(bsz, n_lat // tm),
        in_specs=[row(d, 0), pl.BlockSpec((None, tm, HY_CH), lambda b, i: (b, i, 0)),
                  row(MB_INNER, 0), row(MB_INNER, 0), row(MB_INNER, 0), row(MB_INNER, z_col), vec, vec,
                  pl.BlockSpec(w_out.shape, lambda b, i: (0, 0)),
                  pl.BlockSpec((None, None, 1, d), lambda b, i: (b, 1, 0, 0))],
        out_specs=pl.BlockSpec((None, tm, d), lambda b, i: (b, i, 0)),
        out_shape=jax.ShapeDtypeStruct((bsz, n_lat, d), F32),
        compiler_params=_params("parallel", "parallel"),
        name="mix_out_cd",
    )(h, hy, y_ssd[0], y_ssd[1], xbc, proj, d_skip_x, norm_g.reshape(1, -1), w_out, gate)


def _top2_of4(a, b, c, d):
    hi1, lo1, hi2, lo2 = jnp.maximum(a, b), jnp.minimum(a, b), jnp.maximum(c, d), jnp.minimum(c, d)
    return jnp.maximum(hi1, hi2) + jnp.maximum(jnp.minimum(hi1, hi2), jnp.maximum(lo1, lo2))


def _first_argmax(vals, skip=None):
    idx = None
    for j, vj in enumerate(vals):
        if idx is None and skip is None:
            idx, best = jnp.zeros(vj.shape, jnp.int32), vj
            continue
        if idx is None:
            idx, best = jnp.full(vj.shape, -1, jnp.int32), jnp.full(vj.shape, -jnp.inf, F32)
        take = vj > best
        if skip is not None:
            take = jnp.logical_and(take, skip != j)
        idx = jnp.where(take, j, idx)
        best = jnp.where(take, vj, best)
    return idx, best


def _ffn_pre_kernel(h_ref, g_ref, sh_ref, sc_ref, rw_ref, rb_ref, tri_ref,
                    v_ref, ri_ref, rwt_ref, cnt_ref, carry_ref):
    @pl.when(jnp.logical_and(pl.program_id(0) == 0, pl.program_id(1) == 0))
    def _():
        carry_ref[...] = jnp.zeros_like(carry_ref)

    v = _rms(h_ref[...], g_ref[...]) * (1.0 + sc_ref[...]) + sh_ref[...]
    v_ref[...] = v
    st = _sigmoid(_dot_f32(v, rw_ref[...])).T[0:N_EXPERTS]
    sel = st + rb_ref[...]
    row = lambda a, e: a[e:e + 1]
    epg = EXPERTS_PER_GROUP
    gscore = [_top2_of4(*[row(sel, g * epg + j) for j in range(epg)]) for g in range(N_GROUPS)]
    best, _ = _first_argmax(gscore)

    def in_best(a, j):
        out = row(a, j)
        for g in range(1, N_GROUPS):
            out = jnp.where(best == g, row(a, g * epg + j), out)
        return out

    vals = [in_best(sel, j) for j in range(epg)]
    raw = [in_best(st, j) for j in range(epg)]
    i1, _ = _first_argmax(vals)
    i2, _ = _first_argmax(vals, skip=i1)
    pick = lambda i: functools.reduce(lambda acc, j: jnp.where(i == j, raw[j], acc), range(1, epg), raw[0])
    w1, w2 = pick(i1), pick(i2)
    wsum = w1 + w2
    e1, e2 = best * epg + i1, best * epg + i2

    experts = lax.broadcasted_iota(jnp.int32, st.shape, 0)
    oh1 = (experts == e1).astype(F32)
    oh2 = (experts == e2).astype(F32)
    cnt = oh1 + oh2
    before = _dot(cnt.astype(BF16), tri_ref[...]) + carry_ref[:, 0:1]
    ri_ref[0:1, :] = e1
    ri_ref[1:2, :] = e2
    ri_ref[2:3, :] = jnp.sum(oh1 * before, axis=0, keepdims=True).astype(jnp.int32)
    ri_ref[3:4, :] = jnp.sum(oh2 * before, axis=0, keepdims=True).astype(jnp.int32)
    ri_ref[4:8, :] = jnp.zeros((4, st.shape[1]), jnp.int32)
    rwt_ref[0:1, :] = w1 / wsum
    rwt_ref[1:2, :] = w2 / wsum
    rwt_ref[2:8, :] = jnp.zeros((6, st.shape[1]), F32)
    carry_ref[...] = carry_ref[...] + jnp.sum(cnt, axis=1, keepdims=True)
    cnt_ref[...] = carry_ref[...]


def _ffn_pre(h, g, shift, scale, router_w_pad, router_b, n_ctx):
    bsz, t, d = h.shape
    tm = ROW_TILE
    seg = lambda b, i: (b, (i * tm >= n_ctx).astype(jnp.int32), 0, 0)
    tri = jnp.asarray(np.triu(np.ones((tm, tm), np.float32), 1), BF16)
    return pl.pallas_call(
        _ffn_pre_kernel,
        grid=(bsz, t // tm),
        in_specs=[pl.BlockSpec((None, tm, d), lambda b, i: (b, i, 0)),
                  pl.BlockSpec((1, d), lambda b, i: (0, 0)),
                  pl.BlockSpec((None, None, 1, d), seg),
                  pl.BlockSpec((None, None, 1, d), seg),
                  pl.BlockSpec((d, LANES), lambda b, i: (0, 0)),
                  pl.BlockSpec((N_EXPERTS, 1), lambda b, i: (0, 0)),
                  pl.BlockSpec((tm, tm), lambda b, i: (0, 0))],
        out_specs=[pl.BlockSpec((None, tm, d), lambda b, i: (b, i, 0)),
                   pl.BlockSpec((None, 8, tm), lambda b, i: (b, 0, i)),
                   pl.BlockSpec((None, 8, tm), lambda b, i: (b, 0, i)),
                   pl.BlockSpec((N_EXPERTS, LANES), lambda b, i: (0, 0))],
        out_shape=[jax.ShapeDtypeStruct((bsz, t, d), F32),
                   jax.ShapeDtypeStruct((bsz, 8, t), jnp.int32),
                   jax.ShapeDtypeStruct((bsz, 8, t), F32),
                   jax.ShapeDtypeStruct((N_EXPERTS, LANES), F32)],
        scratch_shapes=[pltpu.VMEM((N_EXPERTS, LANES), F32)],
        compiler_params=_params("arbitrary", "arbitrary"),
        name="ffn_pre",
    )(h, g.reshape(1, d), shift, scale, router_w_pad, router_b.reshape(N_EXPERTS, 1), tri)


def _experts_kernel(be_ref, nb_ref, x_ref, wg_ref, wu_ref, wd_ref, o_ref, wg_s, wu_s, wd_s):
    i = pl.program_id(0)
    prev = be_ref[jnp.maximum(i - 1, 0)]
    changed = jnp.logical_or(i == 0, be_ref[i] != prev)

    @pl.when(changed)
    def _():
        wg_s[...] = wg_ref[...].astype(BF16)
        wu_s[...] = wu_ref[...].astype(BF16)
        wd_s[...] = wd_ref[...].astype(BF16)

    @pl.when(i < nb_ref[0])
    def _():
        x = x_ref[...].astype(BF16)
        hid = _silu(_dot(x, wg_s[...])) * _dot(x, wu_s[...])
        o_ref[...] = _dot(hid.astype(BF16), wd_s[...])

    @pl.when(i >= nb_ref[0])
    def _():
        o_ref[...] = jnp.zeros_like(o_ref)


def _experts(xb, block_e, n_used, layer, w_gate, w_up, w_down):
    n_slots, d = xb.shape
    n_blocks = n_slots // MOE_BLOCK
    de = w_gate.shape[-1]
    wspec = lambda shape: pl.BlockSpec((None, None) + shape, lambda i, be, nb: (layer, be[i], 0, 0))
    return pl.pallas_call(
        _experts_kernel,
        grid_spec=pltpu.PrefetchScalarGridSpec(
            num_scalar_prefetch=2,
            grid=(n_blocks,),
            in_specs=[pl.BlockSpec((MOE_BLOCK, d), lambda i, be, nb: (i, 0)),
                      wspec((d, de)), wspec((d, de)), wspec((de, d))],
            out_specs=pl.BlockSpec((MOE_BLOCK, d), lambda i, be, nb: (i, 0)),
            scratch_shapes=[pltpu.VMEM((d, de), BF16), pltpu.VMEM((d, de), BF16), pltpu.VMEM((de, d), BF16)]),
        out_shape=jax.ShapeDtypeStruct((n_slots, d), F32),
        compiler_params=_params("arbitrary"),
        name="moe_experts",
    )(block_e, n_used, xb, w_gate, w_up, w_down)


def _ffn_post_kernel(h_ref, y0_ref, y1_ref, w_ref, gt_ref, g_ref, o_ref, *, final):
    w = w_ref[...]
    y = w[:, 0:1] * y0_ref[...] + w[:, 1:2] * y1_ref[...]
    out = h_ref[...] + gt_ref[...] * y
    o_ref[...] = _rms(out, g_ref[...]) if final else out


def _ffn_post(h, y, w, gate, n_ctx, final_g=None):
    bsz, t, d = h.shape
    tm = ROW_TILE
    seg = lambda b, i: (b, (i * tm >= n_ctx).astype(jnp.int32), 0, 0)
    row = lambda width: pl.BlockSpec((None, tm, width), lambda b, i: (b, i, 0))
    choice = lambda kk: pl.BlockSpec((None, None, tm, d), lambda b, i: (kk, b, i, 0))
    final = final_g is not None
    g = final_g if final else jnp.ones((d,), F32)
    return pl.pallas_call(
        functools.partial(_ffn_post_kernel, final=final),
        grid=(bsz, t // tm),
        in_specs=[row(d), choice(0), choice(1), row(LANES), pl.BlockSpec((None, None, 1, d), seg),
                  pl.BlockSpec((1, d), lambda b, i: (0, 0))],
        out_specs=row(d),
        out_shape=jax.ShapeDtypeStruct((bsz, t, d), F32),
        compiler_params=_params("parallel", "parallel"),
        name="ffn_post",
    )(h, y, y, w, gate, g.reshape(1, d))


def _slot_layout(n, ri, counts):
    e = jnp.swapaxes(ri[:, 0:2], 1, 2).reshape(n, TOP_K)
    rank = jnp.swapaxes(ri[:, 2:4], 1, 2).reshape(n, TOP_K)
    padded = (counts + MOE_BLOCK - 1) // MOE_BLOCK * MOE_BLOCK
    pend = jnp.cumsum(padded)
    pstart = pend - padded
    experts = jnp.arange(N_EXPERTS, dtype=jnp.int32)
    dest = rank + jnp.sum(jnp.where(e[..., None] == experts, pstart, 0), axis=-1)
    n_slots = (n * TOP_K + MOE_BLOCK - 1) // MOE_BLOCK * MOE_BLOCK + N_EXPERTS * MOE_BLOCK
    n_blocks = n_slots // MOE_BLOCK
    tok = jnp.repeat(jnp.arange(n, dtype=jnp.int32), TOP_K)
    slot_tok = jnp.zeros((n_slots,), jnp.int32).at[dest.reshape(-1)].set(tok)
    blk0 = jnp.arange(n_blocks, dtype=jnp.int32)[:, None] * MOE_BLOCK
    block_e = jnp.minimum(jnp.sum((pend[None, :] <= blk0).astype(jnp.int32), axis=-1), N_EXPERTS - 1)
    n_used = (pend[-1] // MOE_BLOCK).astype(jnp.int32).reshape(1)
    return dest, slot_tok, block_e.astype(jnp.int32), n_used


SC_CORES, SC_SUBCORES = 2, 16
SC_WINDOW = 32


def _gather_rows(table, idx):
    n_rows, d = idx.shape[0], table.shape[1]
    workers = SC_CORES * SC_SUBCORES
    per_worker = n_rows // workers
    assert per_worker * workers == n_rows and per_worker % SC_WINDOW == 0
    mesh = plsc.VectorSubcoreMesh(core_axis_name="c", subcore_axis_name="s")

    @functools.partial(
        pl.kernel, mesh=mesh,
        out_type=jax.ShapeDtypeStruct((n_rows, d), table.dtype),
        scratch_types=[pltpu.VMEM((SC_WINDOW,), jnp.int32), pltpu.VMEM((SC_WINDOW,), jnp.int32),
                       pltpu.VMEM((SC_WINDOW, d), table.dtype), pltpu.VMEM((SC_WINDOW, d), table.dtype),
                       pltpu.SemaphoreType.DMA, pltpu.SemaphoreType.DMA],
    )
    def gather_kernel(table_hbm, idx_hbm, out_hbm, idx0, idx1, rows0, rows1, sem0, sem1):
        base = (lax.axis_index("s") * SC_CORES + lax.axis_index("c")) * per_worker
        n_win = per_worker // SC_WINDOW
        slots = ((idx0, rows0, sem0), (idx1, rows1, sem1))
        window = lambda j: pl.ds(pl.multiple_of(base + j * SC_WINDOW, 8), SC_WINDOW)

        def start(j, slot):
            idx_v, rows_v, sem = slots[slot]
            pltpu.sync_copy(idx_hbm.at[window(j)], idx_v)
            pltpu.async_copy(table_hbm.at[idx_v], rows_v, sem)

        def finish(j, slot):
            idx_v, rows_v, sem = slots[slot]
            pltpu.make_async_copy(table_hbm.at[idx_v], rows_v, sem).wait()
            pltpu.sync_copy(rows_v, out_hbm.at[window(j)])

        start(0, 0)

        @pl.loop(0, n_win, step=2)
        def _(j):
            @pl.when(j + 1 < n_win)
            def _():
                start(j + 1, 1)

            finish(j, 0)

            @pl.when(j + 2 < n_win)
            def _():
                start(j + 2, 0)

            @pl.when(j + 1 < n_win)
            def _():
                finish(j + 1, 1)

    return gather_kernel(table, idx)


def _moe(h, g, shift, scale, gate, router_w_pad, router_b, layer, w_gate, w_up, w_down, n_ctx, final_g=None):
    bsz, t, d = h.shape
    n = bsz * t
    v, ri, rwt, counts = _ffn_pre(h, g, shift, scale, router_w_pad, router_b, n_ctx)
    dest, slot_tok, block_e, n_used = _slot_layout(n, ri, counts[:, 0].astype(jnp.int32))
    w = jnp.swapaxes(rwt[:, 0:2], 1, 2).reshape(n, TOP_K)
    yb = _experts(_gather_rows(v.reshape(n, d), slot_tok), block_e, n_used, layer, w_gate, w_up, w_down)
    y = _gather_rows(yb, jnp.swapaxes(dest, 0, 1).reshape(-1)).reshape(TOP_K, bsz, t, d)
    wpad = jnp.zeros((n, LANES), F32).at[:, :TOP_K].set(w).reshape(bsz, t, LANES)
    return _ffn_post(h, y, wpad, gate, n_ctx, final_g)


DFT_STEP = 8


def _dft_tables(n):
    size = 2 * n
    r = int(round(math.sqrt(size)))
    assert r * r == size and r % DFT_STEP == 0
    p1 = np.arange(r // 2)[None, None, :]
    p2 = np.arange(r)[:, None, None]
    k1 = np.arange(r)[None, :, None]
    ang = 2.0 * np.pi * (((r * p1 + p2) * k1) % size) / size
    g_re, g_im = np.cos(ang), -np.sin(ang)
    g_in = np.concatenate([g_re, g_im], axis=1)
    g_out = np.concatenate([np.swapaxes(g_re, 1, 2), np.swapaxes(g_im, 1, 2)], axis=2) / size
    a2 = 2.0 * np.pi * ((np.arange(r)[:, None] * np.arange(r)[None, :]) % r) / r
    f_re, f_im = np.cos(a2), -np.sin(a2)
    f_fwd = np.block([[f_re, -f_im], [f_im, f_re]])
    f_inv = np.block([[f_re, f_im], [-f_im, f_re]])
    cast = lambda a: jnp.asarray(a.astype(np.float32)).astype(BF16)
    return r, cast(g_in), cast(g_out), cast(f_fwd), cast(f_inv)


def _hy_filter_kernel(z_ref, w1_ref, b1_ref, w2_ref, b2_ref, w3_ref, fr_ref, rates_ref, o_ref):
    z = z_ref[...]
    hid = jnp.sin(fr_ref[...] * (_dot_f32(z, w1_ref[...]) + b1_ref[...]))
    hid = jnp.sin(fr_ref[...] * (_dot_f32(hid, w2_ref[...]) + b2_ref[...]))
    filt = _dot_f32(hid, w3_ref[...])
    decay = jnp.exp(-z[:, 0:1] * rates_ref[...])
    for q in range(o_ref.shape[0]):
        o_ref[q] = filt[:, q * HY_CH:(q + 1) * HY_CH] * decay


def _hy_filters(n, w1, b1, w2, b2, w3, freq):
    t = jnp.linspace(0.0, 1.0, n, dtype=F32)[:, None]
    bands = jnp.linspace(1e-4, HY_BANDS - 1, HY_BANDS, dtype=F32)
    ang = (2.0 * math.pi / n) * jnp.arange(n, dtype=F32)[:, None] * bands
    z = jnp.concatenate([t, jnp.cos(ang), -jnp.sin(ang)], axis=-1)
    z = jnp.pad(z, ((0, 0), (0, LANES - z.shape[1])))
    w1p = jnp.pad(w1, ((0, LANES - w1.shape[0]), (0, 0)))
    rates = jnp.abs(jnp.linspace(HY_MIN_DECAY, HY_MAX_DECAY, HY_CH, dtype=F32)).reshape(1, HY_CH)
    tm = ROW_TILE
    nq = HY_ORDER * 2
    full = lambda a: pl.BlockSpec(a.shape, lambda i: (0,) * a.ndim)
    args = (z, w1p, b1.reshape(1, -1), w2, b2.reshape(1, -1), w3, freq.reshape(1, -1), rates)
    return pl.pallas_call(
        _hy_filter_kernel,
        grid=(n // tm,),
        in_specs=[pl.BlockSpec((tm, LANES), lambda i: (i, 0))] + [full(a) for a in args[1:]],
        out_specs=pl.BlockSpec((nq, tm, HY_CH), lambda i: (0, i, 0)),
        out_shape=jax.ShapeDtypeStruct((nq, n, HY_CH), F32),
        compiler_params=_params("parallel"),
        name="hy_filters",
    )(*args)


def _dft_in_kernel(x_ref, g_ref, a_ref):
    for j in range(DFT_STEP):
        a_ref[j] = _dot(g_ref[j], x_ref[:, j, :].astype(BF16)).astype(BF16)


def _dft_in(x4, col, g_in):
    bx, rh, r, _ = x4.shape
    c = HY_CH
    return pl.pallas_call(
        _dft_in_kernel,
        grid=(bx, r // DFT_STEP),
        in_specs=[pl.BlockSpec((None, rh, DFT_STEP, c), lambda b, i: (b, 0, i, col)),
                  pl.BlockSpec((DFT_STEP, 2 * r, rh), lambda b, i: (i, 0, 0))],
        out_specs=pl.BlockSpec((None, DFT_STEP, 2 * r, c), lambda b, i: (b, i, 0, 0)),
        out_shape=jax.ShapeDtypeStruct((bx, r, 2 * r, c), BF16),
        compiler_params=_params("parallel", "parallel"),
        name="dft_in",
    )(x4, g_in)


def _stage2(a_ref, b, j, f_ref):
    a = jnp.concatenate([a_ref[b, :, 0, j, :], a_ref[b, :, 1, j, :]], axis=0)
    return _dot(f_ref[...], a)


def _dft_filt_kernel(a_ref, f_ref, k_ref):
    r = f_ref.shape[0] // 2
    for j in range(DFT_STEP):
        sf, sb = _stage2(a_ref, 0, j, f_ref), _stage2(a_ref, 1, j, f_ref)
        k_ref[j, 0:r, :] = sf[0:r] + sb[0:r]
        k_ref[j, r:2 * r, :] = sf[r:2 * r] - sb[r:2 * r]


def _dft_filt(a, f_fwd):
    nq, r, _, c = a.shape
    a5 = a.reshape(nq, r, 2, r, c)
    return pl.pallas_call(
        _dft_filt_kernel,
        grid=(nq // 2, r // DFT_STEP),
        in_specs=[pl.BlockSpec((2, r, 2, DFT_STEP, c), lambda o, i: (o, 0, 0, i, 0)),
                  pl.BlockSpec(f_fwd.shape, lambda o, i: (0, 0))],
        out_specs=pl.BlockSpec((None, DFT_STEP, 2 * r, c), lambda o, i: (o, i, 0, 0)),
        out_shape=jax.ShapeDtypeStruct((nq // 2, r, 2 * r, c), F32),
        compiler_params=_params("parallel", "parallel"),
        name="dft_filt",
    )(a5, f_fwd)


def _dft_mid_kernel(a_ref, k_ref, ff_ref, fi_ref, b_ref):
    r = ff_ref.shape[0] // 2
    for b in range(a_ref.shape[0]):
        for j in range(DFT_STEP):
            s = _stage2(a_ref, b, j, ff_ref)
            sr, si = s[0:r], s[r:2 * r]
            kr, ki = k_ref[j, 0:r, :], k_ref[j, r:2 * r, :]
            p = jnp.concatenate([sr * kr - si * ki, sr * ki + si * kr], axis=0).astype(BF16)
            b_ref[b, j] = _dot(fi_ref[...], p).astype(BF16)


def _dft_mid(a, kspec, order, f_fwd, f_inv):
    bsz, r, _, c = a.shape
    a5 = a.reshape(bsz, r, 2, r, c)
    return pl.pallas_call(
        _dft_mid_kernel,
        grid=(r // DFT_STEP,),
        in_specs=[pl.BlockSpec((bsz, r, 2, DFT_STEP, c), lambda i: (0, 0, 0, i, 0)),
                  pl.BlockSpec((None, DFT_STEP, 2 * r, c), lambda i: (order, i, 0, 0)),
                  pl.BlockSpec(f_fwd.shape, lambda i: (0, 0)),
                  pl.BlockSpec(f_inv.shape, lambda i: (0, 0))],
        out_specs=pl.BlockSpec((bsz, DFT_STEP, 2 * r, c), lambda i: (0, i, 0, 0)),
        out_shape=jax.ShapeDtypeStruct((bsz, r, 2 * r, c), BF16),
        compiler_params=_params("parallel"),
        name="dft_mid",
    )(a5, kspec, f_fwd, f_inv)


def _dft_out_kernel(b_ref, g_ref, u_ref, x_ref, bias_ref, o_ref):
    for j in range(DFT_STEP):
        rhs = jnp.concatenate([b_ref[:, 0, j, :], b_ref[:, 1, j, :]], axis=0)
        y = _dot(g_ref[j], rhs)
        o_ref[:, j, :] = x_ref[:, j, :] * (y + u_ref[:, j, :] * bias_ref[...])


def _dft_out(bm, g_out, u4, u_col, x4, x_col, bias):
    bsz, r, _, c = bm.shape
    rh = r // 2
    b5 = bm.reshape(bsz, r, 2, r, c)
    seq = lambda col: pl.BlockSpec((None, rh, DFT_STEP, c), lambda b, i: (b, 0, i, col))
    return pl.pallas_call(
        _dft_out_kernel,
        grid=(bsz, r // DFT_STEP),
        in_specs=[pl.BlockSpec((None, r, 2, DFT_STEP, c), lambda b, i: (b, 0, 0, i, 0)),
                  pl.BlockSpec((DFT_STEP, rh, 2 * r), lambda b, i: (i, 0, 0)),
                  seq(u_col), seq(x_col), pl.BlockSpec((1, c), lambda b, i: (0, 0))],
        out_specs=seq(0),
        out_shape=jax.ShapeDtypeStruct((bsz, rh, r, c), F32),
        compiler_params=_params("parallel", "parallel"),
        name="dft_out",
    )(b5, g_out, u4, x4, bias.reshape(1, c))


def _hyena(hy_in, fw1, fb1, fw2, fb2, fw3, freq, conv_bias):
    bsz, n, _ = hy_in.shape
    r, g_in, g_out, f_fwd, f_inv = _dft_tables(n)
    filt = _hy_filters(n, fw1, fb1, fw2, fb2, fw3, freq)
    kspec = _dft_filt(_dft_in(filt.reshape(-1, r // 2, r, HY_CH), 0, g_in), f_fwd)
    seq4 = hy_in.reshape(bsz, r // 2, r, 3 * HY_CH)
    zz = _dft_out(_dft_mid(_dft_in(seq4, 0, g_in), kspec, 0, f_fwd, f_inv), g_out,
                  seq4, 0, seq4, 1, conv_bias[0])
    out = _dft_out(_dft_mid(_dft_in(zz, 0, g_in), kspec, 1, f_fwd, f_inv), g_out,
                   zz, 0, seq4, 2, conv_bias[1])
    return out.reshape(bsz, n, HY_CH)


CD_HY, CD_Z, CD_XBC, CD_DT = 0, 1536, 2048, 3072
CD_PAD_COLS = CD_DT + 2 * LANES


def _reorder_ab(w):
    gq, gk, gv, gg, lr_f, lr_b, hq, hf_f, hf_b, hi, hg = jnp.split(
        w, np.cumsum([256, 256, 512, 512, 16, 16, 512, 512, 512, 512, 512])[:-1].tolist(), axis=-1)
    pad = jnp.zeros((w.shape[0], AB_PAD_COLS - AB_LR - 2 * GLA_LOW_RANK), w.dtype)
    return jnp.concatenate([gq, gk, gv, gg, hq, hf_f, hf_b, hi, hg, lr_f, lr_b, pad], axis=-1)


def _reorder_cd(w):
    hy, z, xbc, dt_f, dt_b = jnp.split(w, np.cumsum([1536, 512, 1024, 8, 8])[:-1].tolist(), axis=-1)
    pad = jnp.zeros((w.shape[0], LANES - MB_HEADS), w.dtype)
    return jnp.concatenate([hy, z, xbc, dt_f, pad, dt_b, pad], axis=-1)


def kernel(x, c, ctx, c_ctx, ada_w, ada_b, norm_mix_g, norm_ffn_g, norm_out_g, ab_w_in, ab_w_out, gla_gate_w, gla_gate_b, gla_norm_g, hg_lb, hg_norm_g, cd_w_in, cd_w_out, hy_short_w, hy_short_b, hy_w1, hy_b1, hy_w2, hy_b2, hy_w3, hy_freq, hy_bias, mb_conv_w, mb_conv_b, mb_dt_bias, mb_a_log, mb_d, mb_norm_g, router_w, router_b, moe_w_gate, moe_w_up, moe_w_down):
    bsz, n_lat, d = x.shape
    n_ctx = ctx.shape[1]
    t = n_ctx + n_lat
    assert ada_w.shape[0] == 2 and ab_w_in.shape[0] == 1 and cd_w_in.shape[0] == 1

    cond = jnp.zeros((8, d), F32).at[:bsz].set(c).at[bsz].set(c_ctx)
    m = _adaln(cond, ada_w, ada_b)

    def mods(layer):
        lat = m[layer, :bsz].reshape(bsz, 6, d)
        cx = jnp.broadcast_to(m[layer, bsz].reshape(1, 6, d), (bsz, 6, d))
        both = jnp.stack([cx, lat], axis=1)
        return [both[:, :, j][:, :, None, :] for j in range(6)]

    lb_all = jnp.cumsum(jax.nn.softmax(hg_lb.astype(F32), axis=1), axis=1)
    router_w_pad = jnp.zeros((d, LANES), F32).at[:, :N_EXPERTS].set(router_w)
    h = jnp.concatenate([ctx, x], axis=1)

    sh_m, sc_m, gt_m, sh_f, sc_f, gt_f = mods(0)
    proj = _norm_proj(h, norm_mix_g[0], sh_m, sc_m, _reorder_ab(ab_w_in[0]).astype(BF16), n_ctx)
    gwp = [jnp.zeros((LANES, GLA_KEY_W), F32).at[GLA_LOW_RANK * dd:GLA_LOW_RANK * (dd + 1)].set(gla_gate_w[0, dd])
           for dd in range(2)]
    o_gla = _gla_scan(proj, gwp, [gla_gate_b[0, dd].reshape(1, -1) for dd in range(2)], n_ctx)
    o_hg = _hgrn_scan(proj, [lb_all[dd, 0].reshape(1, -1) for dd in range(2)], n_ctx)
    h = _mix_out_ab(h, o_gla, o_hg, proj, gla_norm_g[0], hg_norm_g[0], ab_w_out[0].astype(BF16), gt_m, n_ctx)
    h = _moe(h, norm_ffn_g[0], sh_f, sc_f, gt_f, router_w_pad, router_b,
             0, moe_w_gate, moe_w_up, moe_w_down, n_ctx)

    sh_m, sc_m, gt_m, sh_f, sc_f, gt_f = mods(1)
    proj = _norm_proj(h, norm_mix_g[1], sh_m, sc_m, _reorder_cd(cd_w_in[0]).astype(BF16), n_ctx)
    hy_in = _dwconv(proj, CD_HY, 3 * HY_CH, hy_short_w[0], hy_short_b[0], (n_ctx, t), act=False)
    hy = _hyena(hy_in, hy_w1[0], hy_b1[0], hy_w2[0], hy_b2[0], hy_w3[0], hy_freq[0], hy_bias[0])
    xbc = _dwconv(proj, CD_XBC, MB_INNER + 2 * MB_BC_W, mb_conv_w[0], mb_conv_b[0], (0, n_ctx, t), act=True)
    y_ssd = [_ssd_scan(xbc, proj, CD_DT // LANES + dd, mb_dt_bias[0, dd], mb_a_log[0, dd], n_ctx, dd == 1)
             for dd in range(2)]
    d_skip_x = jnp.repeat(mb_d[0], MB_HEAD_DIM).reshape(1, MB_INNER)
    h = _mix_out_cd(h, hy, y_ssd, xbc, proj, CD_Z // MB_INNER, d_skip_x, mb_norm_g[0],
                    cd_w_out[0].astype(BF16), gt_m, n_ctx)
    return _moe(h, norm_ffn_g[1], sh_f, sc_f, gt_f, router_w_pad, router_b,
                1, moe_w_gate, moe_w_up, moe_w_down, 0, final_g=norm_out_g)
```

```python
import functools
import math

import numpy as np
import jax
import jax.numpy as jnp
from jax import lax
from jax.experimental import pallas as pl
from jax.experimental.pallas import tpu as pltpu
from jax.experimental.pallas import tpu_sc as plsc

NORM_EPS = 1e-6
GLA_HEADS, GLA_DK, GLA_DV, GLA_LOW_RANK, GLA_TAU = 4, 64, 128, 16, 16.0
GLA_KEY_W, GLA_VAL_W = GLA_HEADS * GLA_DK, GLA_HEADS * GLA_DV
HG_HEADS, HG_EXPAND, HG_DV = 4, 128, 128
HG_KEY_W, HG_VAL_W = HG_HEADS * HG_EXPAND, HG_HEADS * HG_DV
HY_CH, HY_ORDER, HY_SHORT, HY_BANDS, HY_FILT_HID = 512, 2, 3, 16, 64
HY_MIN_DECAY = math.log(1e-2) / 1.5
HY_MAX_DECAY = math.log(1e-2) / 0.3
MB_HEADS, MB_HEAD_DIM, MB_GROUPS, MB_STATE = 8, 64, 2, 128
MB_INNER = MB_HEADS * MB_HEAD_DIM
MB_BC_W = MB_GROUPS * MB_STATE
N_EXPERTS, N_GROUPS, TOP_K, MOE_BLOCK = 16, 4, 2, 256
EXPERTS_PER_GROUP = N_EXPERTS // N_GROUPS

LANES = 128
SCAN_CHUNK = 64
SCAN_BLOCK = 128
SSD_CHUNK = 128
ROW_TILE = 256
VMEM_LIMIT = 56 * 1024 * 1024

BF16 = jnp.bfloat16
F32 = jnp.float32


def _params(*sem):
    return pltpu.CompilerParams(dimension_semantics=sem, vmem_limit_bytes=VMEM_LIMIT)


def _split3(x):
    hi = x.astype(BF16)
    r1 = x - hi.astype(F32)
    mid = r1.astype(BF16)
    lo = (r1 - mid.astype(F32)).astype(BF16)
    return hi, mid, lo


def _dot(a, b):
    return jnp.dot(a, b, preferred_element_type=F32)


def _dot_nt(a, b):
    return lax.dot_general(a, b, (((1,), (1,)), ((), ())), preferred_element_type=F32)


def _dot_tn(a, b):
    return lax.dot_general(a, b, (((0,), (0,)), ((), ())), preferred_element_type=F32)


def _sel_dot(m01, x):
    hi, mid, lo = _split3(x)
    return _dot(m01, hi) + (_dot(m01, mid) + _dot(m01, lo))


def _dot_sel(x, m01):
    hi, mid, lo = _split3(x)
    return _dot(hi, m01) + (_dot(mid, m01) + _dot(lo, m01))


def _dot_f32(a, b):
    ah = a.astype(BF16)
    al = (a - ah.astype(F32)).astype(BF16)
    bh = b.astype(BF16)
    bl = (b - bh.astype(F32)).astype(BF16)
    return _dot(ah, bh) + (_dot(ah, bl) + _dot(al, bh))


def _silu(x):
    return x * (1.0 / (1.0 + jnp.exp(-x)))


def _sigmoid(x):
    return 1.0 / (1.0 + jnp.exp(-x))


def _softplus(x):
    return jnp.maximum(x, 0.0) + jnp.log(1.0 + jnp.exp(-jnp.abs(x)))


def _rms(x, g):
    return x * lax.rsqrt(jnp.mean(x * x, axis=-1, keepdims=True) + NORM_EPS) * g


def _adaln_kernel(c_ref, w_ref, b_ref, o_ref):
    o_ref[...] = _dot_f32(_silu(c_ref[...]), w_ref[...]) + b_ref[...]


def _adaln(cond, w, b):
    n_l, d, n6 = w.shape
    tn = 1536
    return pl.pallas_call(
        _adaln_kernel,
        grid=(n_l, n6 // tn),
        in_specs=[pl.BlockSpec((8, d), lambda l, j: (0, 0)),
                  pl.BlockSpec((None, d, tn), lambda l, j: (l, 0, j)),
                  pl.BlockSpec((None, 1, tn), lambda l, j: (l, 0, j))],
        out_specs=pl.BlockSpec((None, 8, tn), lambda l, j: (l, 0, j)),
        out_shape=jax.ShapeDtypeStruct((n_l, 8, n6), F32),
        compiler_params=_params("parallel", "parallel"),
        name="adaln",
    )(cond, w, b.reshape(n_l, 1, n6))


def _norm_proj_kernel(h_ref, g_ref, sh_ref, sc_ref, w_ref, o_ref):
    u = _rms(h_ref[...], g_ref[...]) * (1.0 + sc_ref[...]) + sh_ref[...]
    o_ref[...] = _dot(u.astype(BF16), w_ref[...])


def _norm_proj(h, g, shift, scale, w, n_ctx):
    bsz, t, d = h.shape
    n = w.shape[1]
    tm = ROW_TILE
    seg = lambda b, i: (b, (i * tm >= n_ctx).astype(jnp.int32), 0, 0)
    return pl.pallas_call(
        _norm_proj_kernel,
        grid=(bsz, t // tm),
        in_specs=[pl.BlockSpec((None, tm, d), lambda b, i: (b, i, 0)),
                  pl.BlockSpec((1, d), lambda b, i: (0, 0)),
                  pl.BlockSpec((None, None, 1, d), seg),
                  pl.BlockSpec((None, None, 1, d), seg),
                  pl.BlockSpec((d, n), lambda b, i: (0, 0))],
        out_specs=pl.BlockSpec((None, tm, n), lambda b, i: (b, i, 0)),
        out_shape=jax.ShapeDtypeStruct((bsz, t, n), F32),
        compiler_params=_params("parallel", "parallel"),
        name="norm_proj",
    )(h, g.reshape(1, d), shift, scale, w)


def _scan_constants(c, reverse):
    t = np.arange(c)[:, None]
    u = np.arange(c)[None, :]
    sels = [u <= t, u > t]
    masks = []
    m = c // 2
    while m >= 1:
        blk = t // (2 * m)
        upper_t = (t % (2 * m)) >= m
        r = blk * (2 * m) + m - 1
        s_blk = u // (2 * m)
        upper_s = (u % (2 * m)) >= m
        sels.append((upper_t & (u > r) & (u <= t)) | ((~upper_t) & (u > t) & (u <= r)))
        masks.append((blk == s_blk) & upper_t & (~upper_s))
        m //= 2
    masks.append(t == u)
    sel = np.stack(sels).astype(np.float32)
    msk = np.stack(masks).astype(np.float32)
    if reverse:
        sel = sel[:, ::-1, ::-1]
        msk = msk[:, ::-1, ::-1]
    return np.ascontiguousarray(sel.reshape(-1, c)), np.ascontiguousarray(msk)


def _chunk_order(i, n_ctx_chunks, n_chunks, reverse):
    if not reverse:
        return i
    return jnp.where(i < n_ctx_chunks, n_ctx_chunks - 1 - i, n_chunks - 1 - (i - n_ctx_chunks))


GROUP_KEYS = 256


def _decay_chunk(q, k, v, la, sel_ref, mask_ref, hm_ref, st_ref, heads, dk, dv):
    c = q.shape[0]
    n_lvl = mask_ref.shape[0] - 1
    hpg = GROUP_KEYS // dk
    cs = _dot(sel_ref[...], jnp.concatenate(_split3(la), axis=0))
    e_q = jnp.exp(cs[0:c])
    e_k = jnp.exp(cs[c:2 * c])
    e_tot = jnp.exp(jnp.sum(la, axis=0, keepdims=True))
    vb = v.astype(BF16)
    outs = []
    for g in range(heads // hpg):
        ks = slice(g * GROUP_KEYS, (g + 1) * GROUP_KEYS)
        vs = slice(g * hpg * dv, (g + 1) * hpg * dv)
        qg, kg = q[:, ks], k[:, ks]
        stack = lambda x: jnp.concatenate([x * hm_ref[h] for h in range(hpg)], axis=0).astype(BF16)
        att = mask_ref[n_lvl] * _dot_nt(stack(qg), kg.astype(BF16))
        for l in range(n_lvl):
            e = jnp.exp(cs[(2 + l) * c:(3 + l) * c, ks])
            att = att + mask_ref[l] * _dot_nt(stack(qg * e), (kg * e).astype(BF16))
        att = att.astype(BF16)
        st = st_ref[g]
        inter = _dot_nt(stack(qg * e_q[:, ks]), st.astype(BF16))
        upd = _dot_tn(vb[:, vs], (kg * e_k[:, ks]).astype(BF16))
        new = st * e_tot[:, ks]
        for h in range(hpg):
            rows = slice(h * c, (h + 1) * c)
            hv = slice(g * hpg * dv + h * dv, g * hpg * dv + (h + 1) * dv)
            outs.append(_dot(att[rows], vb[:, hv]) + inter[rows])
            new = new + upd[h * dv:(h + 1) * dv] * hm_ref[h]
        st_ref[g] = new
    return jnp.concatenate(outs, axis=-1)


def _log_sigmoid(x):
    return jnp.minimum(x, 0.0) - jnp.log(1.0 + jnp.exp(-jnp.abs(x)))


def _gla_kernel(*refs):
    ins, hm_ref, (o_refs, st_refs) = (refs[0:8], refs[8:16]), refs[16], (refs[17:19], refs[19:21])

    @pl.when(pl.program_id(1) == 0)
    def _():
        for st_ref in st_refs:
            st_ref[...] = jnp.zeros_like(st_ref)

    for d, ((q_ref, k_ref, v_ref, lr_ref, gw_ref, gb_ref, sel_ref, mask_ref), o_ref, st_ref) in enumerate(
            zip(ins, o_refs, st_refs)):
        z = _dot_f32(lr_ref[...], gw_ref[...]) + gb_ref[...]
        la = _log_sigmoid(z) * (1.0 / GLA_TAU)
        q = q_ref[...] * (GLA_DK ** -0.5)
        k, v = k_ref[...], v_ref[...]
        for rows in _sub_chunks(q.shape[0], d == 1):
            o_ref[rows, :] = _decay_chunk(q[rows], k[rows], v[rows], la[rows], sel_ref, mask_ref, hm_ref,
                                          st_ref, GLA_HEADS, GLA_DK, GLA_DV)


def _hgrn_kernel(*refs):
    ins, hm_ref, (o_refs, st_refs) = (refs[0:6], refs[6:12]), refs[12], (refs[13:15], refs[15:17])

    @pl.when(pl.program_id(1) == 0)
    def _():
        for st_ref in st_refs:
            st_ref[...] = jnp.zeros_like(st_ref)

    for d, ((q_ref, f_ref, v_ref, lb_ref, sel_ref, mask_ref), o_ref, st_ref) in enumerate(
            zip(ins, o_refs, st_refs)):
        lb = lb_ref[...]
        f = lb + (1.0 - lb) * _sigmoid(f_ref[...])
        q, k, v, la = _silu(q_ref[...]), 1.0 - f, v_ref[...], jnp.log(f)
        for rows in _sub_chunks(q.shape[0], d == 1):
            o_ref[rows, :] = _decay_chunk(q[rows], k[rows], v[rows], la[rows], sel_ref, mask_ref, hm_ref,
                                          st_ref, HG_HEADS, HG_EXPAND, HG_DV)


def _sub_chunks(rows, reverse):
    order = range(rows // SCAN_CHUNK)
    return [slice(j * SCAN_CHUNK, (j + 1) * SCAN_CHUNK) for j in (reversed(order) if reverse else order)]


def _scan_specs(blk, n_ctx, t, reverse, chunk=None, stacked_heads=1):
    n_blocks = t // blk
    order = functools.partial(_chunk_order, n_ctx_chunks=n_ctx // blk, n_chunks=n_blocks, reverse=reverse)

    def col(width, idx):
        return pl.BlockSpec((None, blk, width), lambda b, i: (b, order(i), idx))

    sel, msk = _scan_constants(chunk or blk, reverse)
    sel3 = np.concatenate([sel, sel, sel], axis=1)
    msk = np.tile(msk, (1, stacked_heads, 1))
    const = lambda a: pl.BlockSpec(a.shape, lambda b, i: (0,) * a.ndim)
    return n_blocks, col, const, jnp.asarray(sel3, BF16), jnp.asarray(msk, F32)


def _head_masks(dk):
    hpg = GROUP_KEYS // dk
    hm = np.zeros((hpg, 1, GROUP_KEYS), np.float32)
    for h in range(hpg):
        hm[h, 0, h * dk:(h + 1) * dk] = 1.0
    return jnp.asarray(hm)


AB_Q, AB_K, AB_V, AB_G = 0, 256, 512, 1024
AB_HQ, AB_HF, AB_HI, AB_HG, AB_LR = 1536, 2048, 3072, 3584, 4096
AB_PAD_COLS = 4224


def _gla_scan(proj, gate_w_pad, gate_b, n_ctx):
    bsz, t, _ = proj.shape
    hpg = GROUP_KEYS // GLA_DK
    in_specs, args, outs = [], [], []
    for d in range(2):
        n_blocks, col, const, sel, msk = _scan_specs(SCAN_BLOCK, n_ctx, t, d == 1, SCAN_CHUNK, hpg)
        in_specs += [col(GLA_KEY_W, AB_Q // GLA_KEY_W), col(GLA_KEY_W, AB_K // GLA_KEY_W),
                     col(GLA_VAL_W, AB_V // GLA_VAL_W), col(LANES, AB_LR // LANES),
                     const(gate_w_pad[d]), const(gate_b[d]), const(sel), const(msk)]
        args += [proj, proj, proj, proj, gate_w_pad[d], gate_b[d], sel, msk]
        outs.append(col(GLA_VAL_W, 0))
    hm = _head_masks(GLA_DK)
    return pl.pallas_call(
        _gla_kernel,
        grid=(bsz, n_blocks),
        in_specs=in_specs + [const(hm)],
        out_specs=outs,
        out_shape=[jax.ShapeDtypeStruct((bsz, t, GLA_VAL_W), F32)] * 2,
        scratch_shapes=[pltpu.VMEM((GLA_HEADS // hpg, GLA_DV, GROUP_KEYS), F32)] * 2,
        compiler_params=_params("parallel", "arbitrary"),
        name="gla_scan",
    )(*args, hm)


def _hgrn_scan(proj, lb, n_ctx):
    bsz, t, _ = proj.shape
    hpg = GROUP_KEYS // HG_EXPAND
    in_specs, args, outs = [], [], []
    for d in range(2):
        n_blocks, col, const, sel, msk = _scan_specs(SCAN_BLOCK, n_ctx, t, d == 1, SCAN_CHUNK, hpg)
        in_specs += [col(HG_KEY_W, AB_HQ // HG_KEY_W), col(HG_KEY_W, AB_HF // HG_KEY_W + d),
                     col(HG_VAL_W, AB_HI // HG_VAL_W), const(lb[d]), const(sel), const(msk)]
        args += [proj, proj, proj, lb[d], sel, msk]
        outs.append(col(HG_VAL_W, 0))
    hm = _head_masks(HG_EXPAND)
    return pl.pallas_call(
        _hgrn_kernel,
        grid=(bsz, n_blocks),
        in_specs=in_specs + [const(hm)],
        out_specs=outs,
        out_shape=[jax.ShapeDtypeStruct((bsz, t, HG_VAL_W), F32)] * 2,
        scratch_shapes=[pltpu.VMEM((HG_HEADS // hpg, HG_DV, GROUP_KEYS), F32)] * 2,
        compiler_params=_params("parallel", "arbitrary"),
        name="hgrn_scan",
    )(*args, hm)


def _mix_out_ab_kernel(h_ref, gf_ref, gb_ref, hf_ref, hb_ref, gg_ref, hg_ref, gn_ref, hn_ref,
                       w_ref, gt_ref, o_ref):
    feats = []
    for o, gate, g in ((gf_ref[...] + gb_ref[...], gg_ref[...], gn_ref[...]),
                       (hf_ref[...] + hb_ref[...], hg_ref[...], hn_ref[...])):
        for hd in range(o.shape[-1] // LANES):
            s = slice(hd * LANES, (hd + 1) * LANES)
            feats.append(_rms(o[:, s], g) * _silu(gate[:, s]))
    feat = jnp.concatenate(feats, axis=-1).astype(BF16)
    o_ref[...] = h_ref[...] + gt_ref[...] * _dot(feat, w_ref[...])


def _mix_out_ab(h, o_gla, o_hg, proj, gla_norm_g, hg_norm_g, w_out, gate, n_ctx):
    bsz, t, d = h.shape
    tm = ROW_TILE
    seg = lambda b, i: (b, (i * tm >= n_ctx).astype(jnp.int32), 0, 0)
    row = lambda width, idx: pl.BlockSpec((None, tm, width), lambda b, i: (b, i, idx))
    vec = pl.BlockSpec((1, LANES), lambda b, i: (0, 0))
    return pl.pallas_call(
        _mix_out_ab_kernel,
        grid=(bsz, t // tm),
        in_specs=[row(d, 0), row(GLA_VAL_W, 0), row(GLA_VAL_W, 0), row(HG_VAL_W, 0), row(HG_VAL_W, 0),
                  row(GLA_VAL_W, AB_G // GLA_VAL_W), row(HG_VAL_W, AB_HG // HG_VAL_W), vec, vec,
                  pl.BlockSpec(w_out.shape, lambda b, i: (0, 0)),
                  pl.BlockSpec((None, None, 1, d), seg)],
        out_specs=row(d, 0),
        out_shape=jax.ShapeDtypeStruct((bsz, t, d), F32),
        compiler_params=_params("parallel", "parallel"),
        name="mix_out_ab",
    )(h, o_gla[0], o_gla[1], o_hg[0], o_hg[1], proj, proj, gla_norm_g.reshape(1, -1),
      hg_norm_g.reshape(1, -1), w_out, gate)


def _dwconv_kernel(x_ref, prev_ref, next_ref, w_ref, b_ref, o_ref, *, seg_tiles, act):
    i = pl.program_id(1)
    tm = x_ref.shape[0]
    x = x_ref[...]
    first = functools.reduce(jnp.logical_or, [i == s for s in seg_tiles[:-1]])
    last = functools.reduce(jnp.logical_or, [i == s - 1 for s in seg_tiles[1:]])
    prev_row = jnp.where(first, 0.0, prev_ref[7:8, :])
    next_row = jnp.where(last, 0.0, next_ref[0:1, :])
    rows = lax.broadcasted_iota(jnp.int32, x.shape, 0)
    x_prev = jnp.where(rows == 0, prev_row, pltpu.roll(x, 1, axis=0))
    x_next = jnp.where(rows == tm - 1, next_row, pltpu.roll(x, tm - 1, axis=0))
    y = x_prev * w_ref[0:1, :] + x * w_ref[1:2, :] + x_next * w_ref[2:3, :] + b_ref[...]
    o_ref[...] = _silu(y) if act else y


def _dwconv(proj, col0, width, w, b, seg_bounds, act):
    bsz, t, _ = proj.shape
    tm = ROW_TILE
    off = seg_bounds[0] // tm
    n_tiles = t // tm - off
    seg_tiles = tuple(s // tm - off for s in seg_bounds)
    cb = col0 // width
    r8 = tm // 8
    last8 = t // 8 - 1
    kern = functools.partial(_dwconv_kernel, seg_tiles=seg_tiles, act=act)
    return pl.pallas_call(
        kern,
        grid=(bsz, n_tiles),
        in_specs=[pl.BlockSpec((None, tm, width), lambda bb, i: (bb, i + off, cb)),
                  pl.BlockSpec((None, 8, width), lambda bb, i: (bb, jnp.maximum((i + off) * r8 - 1, 0), cb)),
                  pl.BlockSpec((None, 8, width),
                               lambda bb, i: (bb, jnp.minimum((i + off + 1) * r8, last8), cb)),
                  pl.BlockSpec((3, width), lambda bb, i: (0, 0)),
                  pl.BlockSpec((1, width), lambda bb, i: (0, 0))],
        out_specs=pl.BlockSpec((None, tm, width), lambda bb, i: (bb, i, 0)),
        out_shape=jax.ShapeDtypeStruct((bsz, n_tiles * tm, width), F32),
        compiler_params=_params("parallel", "parallel"),
        name="dwconv",
    )(proj, proj, proj, w.T, b.reshape(1, -1))


def _ssd_kernel(xbc_ref, dt_ref, bias_ref, alog_ref, hexp_ref, mq_ref, mk_ref, mask_ref, o_ref, st_ref):
    @pl.when(pl.program_id(1) == 0)
    def _():
        st_ref[...] = jnp.zeros_like(st_ref)

    c = xbc_ref.shape[0]
    hpg = MB_HEADS // MB_GROUPS
    gw = hpg * MB_HEAD_DIM
    dt = _softplus(dt_ref[...] + bias_ref[...])
    la = -dt * jnp.exp(alog_ref[...])
    cq = _sel_dot(mq_ref[...], la)
    ck = _sel_dot(mk_ref[...], la)
    cq_t = _dot_nt_sel(la, mq_ref[...])
    hexp = hexp_ref[...]
    dt_x = _dot_sel(dt, hexp)
    eq_x = jnp.exp(_dot_sel(cq, hexp))
    ek_x = jnp.exp(_dot_sel(ck, hexp))
    etot_x = jnp.exp(_dot_sel(jnp.sum(la, axis=0, keepdims=True), hexp))
    xs = xbc_ref[:, 0:MB_INNER] * dt_x
    mask = mask_ref[...]
    outs = []
    for g in range(MB_GROUPS):
        bm = xbc_ref[:, MB_INNER + g * MB_STATE:MB_INNER + (g + 1) * MB_STATE].astype(BF16)
        cm = xbc_ref[:, MB_INNER + MB_BC_W + g * MB_STATE:MB_INNER + MB_BC_W + (g + 1) * MB_STATE].astype(BF16)
        cb = _dot_nt(cm, bm)
        st = st_ref[g]
        gs = slice(g * gw, (g + 1) * gw)
        y_inter = _dot(cm, st.astype(BF16)) * eq_x[:, gs]
        for r in range(hpg):
            hd = g * hpg + r
            diff = cq[:, hd:hd + 1] - cq_t[hd:hd + 1, :]
            w = cb * jnp.exp(jnp.where(mask > 0.0, diff, -jnp.inf))
            ps = slice(hd * MB_HEAD_DIM, (hd + 1) * MB_HEAD_DIM)
            outs.append(_dot(w.astype(BF16), xs[:, ps].astype(BF16))
                        + y_inter[:, r * MB_HEAD_DIM:(r + 1) * MB_HEAD_DIM])
        st_ref[g] = st * etot_x[:, gs] + _dot_tn(bm, (xs[:, gs] * ek_x[:, gs]).astype(BF16))
    o_ref[...] = jnp.concatenate(outs, axis=-1)


def _dot_nt_sel(x, m01):
    hi, mid, lo = _split3(x)
    f = lambda p: lax.dot_general(p, m01, (((0,), (1,)), ((), ())), preferred_element_type=F32)
    return f(hi) + (f(mid) + f(lo))


def _ssd_scan(xbc, proj, dt_col, dt_bias, a_log, n_ctx, reverse):
    bsz, t, _ = xbc.shape
    c = SSD_CHUNK
    n_chunks, col, const, sel, msk = _scan_specs(c, n_ctx, t, reverse)
    mq, mk = sel[0:c, 0:c], sel[c:2 * c, 0:c]
    tri = np.tril(np.ones((c, c), np.float32))
    mask = jnp.asarray(tri[::-1, ::-1].copy() if reverse else tri)
    pad = lambda v: jnp.zeros((1, LANES), F32).at[0, :MB_HEADS].set(v)
    hexp = np.zeros((LANES, MB_INNER), np.float32)
    for hd in range(MB_HEADS):
        hexp[hd, hd * MB_HEAD_DIM:(hd + 1) * MB_HEAD_DIM] = 1.0
    hexp = jnp.asarray(hexp, BF16)
    bias = pad(dt_bias)
    alog = pad(a_log.astype(F32))
    return pl.pallas_call(
        _ssd_kernel,
        grid=(bsz, n_chunks),
        in_specs=[col(xbc.shape[-1], 0), col(LANES, dt_col), const(bias), const(alog), const(hexp),
                  const(mq), const(mk), const(mask)],
        out_specs=col(MB_INNER, 0),
        out_shape=jax.ShapeDtypeStruct((bsz, t, MB_INNER), F32),
        scratch_shapes=[pltpu.VMEM((MB_GROUPS, MB_STATE, MB_INNER // MB_GROUPS), F32)],
        compiler_params=_params("parallel", "arbitrary"),
        name="ssd_scan_rev" if reverse else "ssd_scan_fwd",
    )(xbc, proj, bias, alog, hexp, mq, mk, mask)


def _mix_out_cd_kernel(h_ref, hy_ref, yf_ref, yb_ref, xs_ref, z_ref, dsk_ref, ng_ref, w_ref, gt_ref, o_ref):
    y = (yf_ref[...] + yb_ref[...] + dsk_ref[...] * xs_ref[...]) * _silu(z_ref[...])
    gw = MB_INNER // MB_GROUPS
    ys = [_rms(y[:, g * gw:(g + 1) * gw], ng_ref[:, g * gw:(g + 1) * gw]) for g in range(MB_GROUPS)]
    feat = jnp.concatenate([hy_ref[...]] + ys, axis=-1).astype(BF16)
    o_ref[...] = h_ref[...] + gt_ref[...] * _dot(feat, w_ref[...])


def _mix_out_cd(h, hy, y_ssd, xbc, proj, z_col, d_skip_x, norm_g, w_out, gate, n_ctx):
    bsz, t, d = h.shape
    tm = ROW_TILE
    n_lat = t - n_ctx
    off = n_ctx // tm
    row = lambda width, idx: pl.BlockSpec((None, tm, width), lambda b, i: (b, i + off, idx))
    vec = pl.BlockSpec((1, MB_INNER), lambda b, i: (0, 0))
    return pl.pallas_call(
        _mix_out_cd_kernel,
        grid=(bsz, n_lat // tm),
        in_specs=[row(d, 0), pl.BlockSpec((None, tm, HY_CH), lambda b, i: (b, i, 0)),
                  row(MB_INNER, 0), row(MB_INNER, 0), row(MB_INNER, 0), row(MB_INNER, z_col), vec, vec,
                  pl.BlockSpec(w_out.shape, lambda b, i: (0, 0)),
                  pl.BlockSpec((None, None, 1, d), lambda b, i: (b, 1, 0, 0))],
        out_specs=pl.BlockSpec((None, tm, d), lambda b, i: (b, i, 0)),
        out_shape=jax.ShapeDtypeStruct((bsz, n_lat, d), F32),
        compiler_params=_params("parallel", "parallel"),
        name="mix_out_cd",
    )(h, hy, y_ssd[0], y_ssd[1], xbc, proj, d_skip_x, norm_g.reshape(1, -1), w_out, gate)


def _top2_of4(a, b, c, d):
    hi1, lo1, hi2, lo2 = jnp.maximum(a, b), jnp.minimum(a, b), jnp.maximum(c, d), jnp.minimum(c, d)
    return jnp.maximum(hi1, hi2) + jnp.maximum(jnp.minimum(hi1, hi2), jnp.maximum(lo1, lo2))


def _first_argmax(vals, skip=None):
    idx = None
    for j, vj in enumerate(vals):
        if idx is None and skip is None:
            idx, best = jnp.zeros(vj.shape, jnp.int32), vj
            continue
        if idx is None:
            idx, best = jnp.full(vj.shape, -1, jnp.int32), jnp.full(vj.shape, -jnp.inf, F32)
        take = vj > best
        if skip is not None:
            take = jnp.logical_and(take, skip != j)
        idx = jnp.where(take, j, idx)
        best = jnp.where(take, vj, best)
    return idx, best


def _ffn_pre_kernel(h_ref, g_ref, sh_ref, sc_ref, rw_ref, rb_ref, tri_ref,
                    v_ref, ri_ref, rwt_ref, cnt_ref, carry_ref):
    @pl.when(jnp.logical_and(pl.program_id(0) == 0, pl.program_id(1) == 0))
    def _():
        carry_ref[...] = jnp.zeros_like(carry_ref)

    v = _rms(h_ref[...], g_ref[...]) * (1.0 + sc_ref[...]) + sh_ref[...]
    bits = lax.bitcast_convert_type(v.astype(BF16).astype(F32), jnp.uint32)
    half = v.shape[1] // 2
    v_ref[...] = bits[:, :half] | (bits[:, half:] >> 16)
    st = _sigmoid(_dot_f32(v, rw_ref[...])).T[0:N_EXPERTS]
    sel = st + rb_ref[...]
    row = lambda a, e: a[e:e + 1]
    epg = EXPERTS_PER_GROUP
    gscore = [_top2_of4(*[row(sel, g * epg + j) for j in range(epg)]) for g in range(N_GROUPS)]
    best, _ = _first_argmax(gscore)

    def in_best(a, j):
        out = row(a, j)
        for g in range(1, N_GROUPS):
            out = jnp.where(best == g, row(a, g * epg + j), out)
        return out

    vals = [in_best(sel, j) for j in range(epg)]
    raw = [in_best(st, j) for j in range(epg)]
    i1, _ = _first_argmax(vals)
    i2, _ = _first_argmax(vals, skip=i1)
    pick = lambda i: functools.reduce(lambda acc, j: jnp.where(i == j, raw[j], acc), range(1, epg), raw[0])
    w1, w2 = pick(i1), pick(i2)
    wsum = w1 + w2
    e1, e2 = best * epg + i1, best * epg + i2

    experts = lax.broadcasted_iota(jnp.int32, st.shape, 0)
    oh1 = (experts == e1).astype(F32)
    oh2 = (experts == e2).astype(F32)
    cnt = oh1 + oh2
    before = _dot(cnt.astype(BF16), tri_ref[...]) + carry_ref[:, 0:1]
    ri_ref[0:1, :] = e1
    ri_ref[1:2, :] = e2
    ri_ref[2:3, :] = jnp.sum(oh1 * before, axis=0, keepdims=True).astype(jnp.int32)
    ri_ref[3:4, :] = jnp.sum(oh2 * before, axis=0, keepdims=True).astype(jnp.int32)
    ri_ref[4:8, :] = jnp.zeros((4, st.shape[1]), jnp.int32)
    rwt_ref[0:1, :] = w1 / wsum
    rwt_ref[1:2, :] = w2 / wsum
    rwt_ref[2:8, :] = jnp.zeros((6, st.shape[1]), F32)
    carry_ref[...] = carry_ref[...] + jnp.sum(cnt, axis=1, keepdims=True)
    cnt_ref[...] = carry_ref[...]


def _ffn_pre(h, g, shift, scale, router_w_pad, router_b, n_ctx):
    bsz, t, d = h.shape
    tm = ROW_TILE
    seg = lambda b, i: (b, (i * tm >= n_ctx).astype(jnp.int32), 0, 0)
    tri = jnp.asarray(np.triu(np.ones((tm, tm), np.float32), 1), BF16)
    return pl.pallas_call(
        _ffn_pre_kernel,
        grid=(bsz, t // tm),
        in_specs=[pl.BlockSpec((None, tm, d), lambda b, i: (b, i, 0)),
                  pl.BlockSpec((1, d), lambda b, i: (0, 0)),
                  pl.BlockSpec((None, None, 1, d), seg),
                  pl.BlockSpec((None, None, 1, d), seg),
                  pl.BlockSpec((d, LANES), lambda b, i: (0, 0)),
                  pl.BlockSpec((N_EXPERTS, 1), lambda b, i: (0, 0)),
                  pl.BlockSpec((tm, tm), lambda b, i: (0, 0))],
        out_specs=[pl.BlockSpec((None, tm, d // 2), lambda b, i: (b, i, 0)),
                   pl.BlockSpec((None, 8, tm), lambda b, i: (b, 0, i)),
                   pl.BlockSpec((None, 8, tm), lambda b, i: (b, 0, i)),
                   pl.BlockSpec((N_EXPERTS, LANES), lambda b, i: (0, 0))],
        out_shape=[jax.ShapeDtypeStruct((bsz, t, d // 2), jnp.uint32),
                   jax.ShapeDtypeStruct((bsz, 8, t), jnp.int32),
                   jax.ShapeDtypeStruct((bsz, 8, t), F32),
                   jax.ShapeDtypeStruct((N_EXPERTS, LANES), F32)],
        scratch_shapes=[pltpu.VMEM((N_EXPERTS, LANES), F32)],
        compiler_params=_params("arbitrary", "arbitrary"),
        name="ffn_pre",
    )(h, g.reshape(1, d), shift, scale, router_w_pad, router_b.reshape(N_EXPERTS, 1), tri)


def _experts_kernel(be_ref, nb_ref, x_ref, wg_ref, wu_ref, wd_ref, o_ref, wg_s, wu_s, wd_s):
    i = pl.program_id(0)
    prev = be_ref[jnp.maximum(i - 1, 0)]
    changed = jnp.logical_or(i == 0, be_ref[i] != prev)

    @pl.when(changed)
    def _():
        wg_s[...] = wg_ref[...].astype(BF16)
        wu_s[...] = wu_ref[...].astype(BF16)
        wd_s[...] = wd_ref[...].astype(BF16)

    @pl.when(i < nb_ref[0])
    def _():
        p = x_ref[...]
        hi = lax.bitcast_convert_type(p & jnp.uint32(0xFFFF0000), F32)
        lo = lax.bitcast_convert_type(p << 16, F32)
        x = jnp.concatenate([hi, lo], axis=1).astype(BF16)
        hid = _silu(_dot(x, wg_s[...])) * _dot(x, wu_s[...])
        o_ref[...] = _dot(hid.astype(BF16), wd_s[...])

    @pl.when(i >= nb_ref[0])
    def _():
        o_ref[...] = jnp.zeros_like(o_ref)


def _experts(xb, block_e, n_used, layer, w_gate, w_up, w_down):
    n_slots = xb.shape[0]
    n_blocks = n_slots // MOE_BLOCK
    d, de = w_gate.shape[-2:]
    wspec = lambda shape: pl.BlockSpec((None, None) + shape, lambda i, be, nb: (layer, be[i], 0, 0))
    return pl.pallas_call(
        _experts_kernel,
        grid_spec=pltpu.PrefetchScalarGridSpec(
            num_scalar_prefetch=2,
            grid=(n_blocks,),
            in_specs=[pl.BlockSpec((MOE_BLOCK, d // 2), lambda i, be, nb: (i, 0)),
                      wspec((d, de)), wspec((d, de)), wspec((de, d))],
            out_specs=pl.BlockSpec((MOE_BLOCK, d), lambda i, be, nb: (i, 0)),
            scratch_shapes=[pltpu.VMEM((d, de), BF16), pltpu.VMEM((d, de), BF16), pltpu.VMEM((de, d), BF16)]),
        out_shape=jax.ShapeDtypeStruct((n_slots, d), F32),
        compiler_params=_params("arbitrary"),
        name="moe_experts",
    )(block_e, n_used, xb, w_gate, w_up, w_down)


def _ffn_post_kernel(h_ref, y0_ref, y1_ref, w_ref, gt_ref, g_ref, o_ref, *, final):
    w = w_ref[...]
    y = w[:, 0:1] * y0_ref[...] + w[:, 1:2] * y1_ref[...]
    out = h_ref[...] + gt_ref[...] * y
    o_ref[...] = _rms(out, g_ref[...]) if final else out


def _ffn_post(h, y, w, gate, n_ctx, final_g=None):
    bsz, t, d = h.shape
    tm = ROW_TILE
    seg = lambda b, i: (b, (i * tm >= n_ctx).astype(jnp.int32), 0, 0)
    row = lambda width: pl.BlockSpec((None, tm, width), lambda b, i: (b, i, 0))
    choice = lambda kk: pl.BlockSpec((None, None, tm, d), lambda b, i: (kk, b, i, 0))
    final = final_g is not None
    g = final_g if final else jnp.ones((d,), F32)
    return pl.pallas_call(
        functools.partial(_ffn_post_kernel, final=final),
        grid=(bsz, t // tm),
        in_specs=[row(d), choice(0), choice(1), row(LANES), pl.BlockSpec((None, None, 1, d), seg),
                  pl.BlockSpec((1, d), lambda b, i: (0, 0))],
        out_specs=row(d),
        out_shape=jax.ShapeDtypeStruct((bsz, t, d), F32),
        compiler_params=_params("parallel", "parallel"),
        name="ffn_post",
    )(h, y, y, w, gate, g.reshape(1, d))


def _slot_layout(n, ri, counts):
    e = jnp.swapaxes(ri[:, 0:2], 1, 2).reshape(n, TOP_K)
    rank = jnp.swapaxes(ri[:, 2:4], 1, 2).reshape(n, TOP_K)
    padded = (counts + MOE_BLOCK - 1) // MOE_BLOCK * MOE_BLOCK
    pend = jnp.cumsum(padded)
    pstart = pend - padded
    experts = jnp.arange(N_EXPERTS, dtype=jnp.int32)
    dest = rank + jnp.sum(jnp.where(e[..., None] == experts, pstart, 0), axis=-1)
    n_slots = (n * TOP_K + MOE_BLOCK - 1) // MOE_BLOCK * MOE_BLOCK + N_EXPERTS * MOE_BLOCK
    n_blocks = n_slots // MOE_BLOCK
    tok = jnp.repeat(jnp.arange(n, dtype=jnp.int32), TOP_K)
    slot_tok = jnp.zeros((n_slots,), jnp.int32).at[dest.reshape(-1)].set(tok)
    blk0 = jnp.arange(n_blocks, dtype=jnp.int32)[:, None] * MOE_BLOCK
    block_e = jnp.minimum(jnp.sum((pend[None, :] <= blk0).astype(jnp.int32), axis=-1), N_EXPERTS - 1)
    n_used = (pend[-1] // MOE_BLOCK).astype(jnp.int32).reshape(1)
    return dest, slot_tok, block_e.astype(jnp.int32), n_used


SC_CORES, SC_SUBCORES = 2, 16
SC_WINDOW = 32


def _gather_rows(table, idx):
    n_rows, d = idx.shape[0], table.shape[1]
    workers = SC_CORES * SC_SUBCORES
    per_worker = n_rows // workers
    assert per_worker * workers == n_rows and per_worker % SC_WINDOW == 0
    mesh = plsc.VectorSubcoreMesh(core_axis_name="c", subcore_axis_name="s")

    @functools.partial(
        pl.kernel, mesh=mesh,
        out_type=jax.ShapeDtypeStruct((n_rows, d), table.dtype),
        scratch_types=[pltpu.VMEM((SC_WINDOW,), jnp.int32), pltpu.VMEM((SC_WINDOW,), jnp.int32),
                       pltpu.VMEM((SC_WINDOW, d), table.dtype), pltpu.VMEM((SC_WINDOW, d), table.dtype),
                       pltpu.SemaphoreType.DMA, pltpu.SemaphoreType.DMA],
    )
    def gather_kernel(table_hbm, idx_hbm, out_hbm, idx0, idx1, rows0, rows1, sem0, sem1):
        base = (lax.axis_index("s") * SC_CORES + lax.axis_index("c")) * per_worker
        n_win = per_worker // SC_WINDOW
        slots = ((idx0, rows0, sem0), (idx1, rows1, sem1))
        window = lambda j: pl.ds(pl.multiple_of(base + j * SC_WINDOW, 8), SC_WINDOW)

        def start(j, slot):
            idx_v, rows_v, sem = slots[slot]
            pltpu.sync_copy(idx_hbm.at[window(j)], idx_v)
            pltpu.async_copy(table_hbm.at[idx_v], rows_v, sem)

        def finish(j, slot):
            idx_v, rows_v, sem = slots[slot]
            pltpu.make_async_copy(table_hbm.at[idx_v], rows_v, sem).wait()
            pltpu.sync_copy(rows_v, out_hbm.at[window(j)])

        start(0, 0)

        @pl.loop(0, n_win, step=2)
        def _(j):
            @pl.when(j + 1 < n_win)
            def _():
                start(j + 1, 1)

            finish(j, 0)

            @pl.when(j + 2 < n_win)
            def _():
                start(j + 2, 0)

            @pl.when(j + 1 < n_win)
            def _():
                finish(j + 1, 1)

    return gather_kernel(table, idx)


def _alongside(gather, idx, side_fn, side_in):
    idx, side_in = lax.optimization_barrier((idx, side_in))
    return lax.optimization_barrier((gather(idx), side_fn(side_in)))


def _moe(h, g, shift, scale, gate, router_w_pad, router_b, layer, w_gate, w_up, w_down, n_ctx,
         final_g=None, side=None):
    bsz, t, d = h.shape
    n = bsz * t
    v, ri, rwt, counts = _ffn_pre(h, g, shift, scale, router_w_pad, router_b, n_ctx)
    dest, slot_tok, block_e, n_used = _slot_layout(n, ri, counts[:, 0].astype(jnp.int32))
    w = jnp.swapaxes(rwt[:, 0:2], 1, 2).reshape(n, TOP_K)
    dispatch = lambda idx: _gather_rows(v.reshape(n, d // 2), idx)
    if side is None:
        xb = dispatch(slot_tok)
    else:
        xb, side_a = _alongside(dispatch, slot_tok, *side[0])
    yb = _experts(xb, block_e, n_used, layer, w_gate, w_up, w_down)
    combine = lambda idx: _gather_rows(yb, idx)
    dest_flat = jnp.swapaxes(dest, 0, 1).reshape(-1)
    if side is None:
        y, side_b = combine(dest_flat), None
    else:
        y, side_b = _alongside(combine, dest_flat, side[1], side_a)
    wpad = jnp.zeros((n, LANES), F32).at[:, :TOP_K].set(w).reshape(bsz, t, LANES)
    out = _ffn_post(h, y.reshape(TOP_K, bsz, t, d), wpad, gate, n_ctx, final_g)
    return out if side is None else (out, side_b)


DFT_STEP = 8


def _dft_tables(n):
    size = 2 * n
    r = int(round(math.sqrt(size)))
    assert r * r == size and r % DFT_STEP == 0
    p1 = np.arange(r // 2)[None, None, :]
    p2 = np.arange(r)[:, None, None]
    k1 = np.arange(r)[None, :, None]
    ang = 2.0 * np.pi * (((r * p1 + p2) * k1) % size) / size
    g_re, g_im = np.cos(ang), -np.sin(ang)
    g_in = np.concatenate([g_re, g_im], axis=1)
    g_out = np.concatenate([np.swapaxes(g_re, 1, 2), np.swapaxes(g_im, 1, 2)], axis=2) / size
    a2 = 2.0 * np.pi * ((np.arange(r)[:, None] * np.arange(r)[None, :]) % r) / r
    f_re, f_im = np.cos(a2), -np.sin(a2)
    f_fwd = np.block([[f_re, -f_im], [f_im, f_re]])
    f_inv = np.block([[f_re, f_im], [-f_im, f_re]])
    cast = lambda a: jnp.asarray(a.astype(np.float32)).astype(BF16)
    return r, cast(g_in), cast(g_out), cast(f_fwd), cast(f_inv)


def _hy_filter_kernel(z_ref, w1_ref, b1_ref, w2_ref, b2_ref, w3_ref, fr_ref, rates_ref, o_ref):
    z = z_ref[...]
    hid = jnp.sin(fr_ref[...] * (_dot_f32(z, w1_ref[...]) + b1_ref[...]))
    hid = jnp.sin(fr_ref[...] * (_dot_f32(hid, w2_ref[...]) + b2_ref[...]))
    filt = _dot_f32(hid, w3_ref[...])
    decay = jnp.exp(-z[:, 0:1] * rates_ref[...])
    for q in range(o_ref.shape[0]):
        o_ref[q] = filt[:, q * HY_CH:(q + 1) * HY_CH] * decay


def _hy_filters(n, w1, b1, w2, b2, w3, freq):
    t = jnp.linspace(0.0, 1.0, n, dtype=F32)[:, None]
    bands = jnp.linspace(1e-4, HY_BANDS - 1, HY_BANDS, dtype=F32)
    ang = (2.0 * math.pi / n) * jnp.arange(n, dtype=F32)[:, None] * bands
    z = jnp.concatenate([t, jnp.cos(ang), -jnp.sin(ang)], axis=-1)
    z = jnp.pad(z, ((0, 0), (0, LANES - z.shape[1])))
    w1p = jnp.pad(w1, ((0, LANES - w1.shape[0]), (0, 0)))
    rates = jnp.abs(jnp.linspace(HY_MIN_DECAY, HY_MAX_DECAY, HY_CH, dtype=F32)).reshape(1, HY_CH)
    tm = ROW_TILE
    nq = HY_ORDER * 2
    full = lambda a: pl.BlockSpec(a.shape, lambda i: (0,) * a.ndim)
    args = (z, w1p, b1.reshape(1, -1), w2, b2.reshape(1, -1), w3, freq.reshape(1, -1), rates)
    return pl.pallas_call(
        _hy_filter_kernel,
        grid=(n // tm,),
        in_specs=[pl.BlockSpec((tm, LANES), lambda i: (i, 0))] + [full(a) for a in args[1:]],
        out_specs=pl.BlockSpec((nq, tm, HY_CH), lambda i: (0, i, 0)),
        out_shape=jax.ShapeDtypeStruct((nq, n, HY_CH), F32),
        compiler_params=_params("parallel"),
        name="hy_filters",
    )(*args)


def _dft_in_kernel(x_ref, g_ref, a_ref):
    for j in range(DFT_STEP):
        a_ref[j] = _dot(g_ref[j], x_ref[:, j, :].astype(BF16)).astype(BF16)


def _dft_in(x4, col, g_in):
    bx, rh, r, _ = x4.shape
    c = HY_CH
    return pl.pallas_call(
        _dft_in_kernel,
        grid=(bx, r // DFT_STEP),
        in_specs=[pl.BlockSpec((None, rh, DFT_STEP, c), lambda b, i: (b, 0, i, col)),
                  pl.BlockSpec((DFT_STEP, 2 * r, rh), lambda b, i: (i, 0, 0))],
        out_specs=pl.BlockSpec((None, DFT_STEP, 2 * r, c), lambda b, i: (b, i, 0, 0)),
        out_shape=jax.ShapeDtypeStruct((bx, r, 2 * r, c), BF16),
        compiler_params=_params("parallel", "parallel"),
        name="dft_in",
    )(x4, g_in)


def _stage2(a_ref, b, j, f_ref):
    a = jnp.concatenate([a_ref[b, :, 0, j, :], a_ref[b, :, 1, j, :]], axis=0)
    return _dot(f_ref[...], a)


def _dft_filt_kernel(a_ref, f_ref, k_ref):
    r = f_ref.shape[0] // 2
    for j in range(DFT_STEP):
        sf, sb = _stage2(a_ref, 0, j, f_ref), _stage2(a_ref, 1, j, f_ref)
        k_ref[j, 0:r, :] = sf[0:r] + sb[0:r]
        k_ref[j, r:2 * r, :] = sf[r:2 * r] - sb[r:2 * r]


def _dft_filt(a, f_fwd):
    nq, r, _, c = a.shape
    a5 = a.reshape(nq, r, 2, r, c)
    return pl.pallas_call(
        _dft_filt_kernel,
        grid=(nq // 2, r // DFT_STEP),
        in_specs=[pl.BlockSpec((2, r, 2, DFT_STEP, c), lambda o, i: (o, 0, 0, i, 0)),
                  pl.BlockSpec(f_fwd.shape, lambda o, i: (0, 0))],
        out_specs=pl.BlockSpec((None, DFT_STEP, 2 * r, c), lambda o, i: (o, i, 0, 0)),
        out_shape=jax.ShapeDtypeStruct((nq // 2, r, 2 * r, c), F32),
        compiler_params=_params("parallel", "parallel"),
        name="dft_filt",
    )(a5, f_fwd)


def _dft_mid_kernel(a_ref, k_ref, ff_ref, fi_ref, b_ref):
    r = ff_ref.shape[0] // 2
    for b in range(a_ref.shape[0]):
        for j in range(DFT_STEP):
            s = _stage2(a_ref, b, j, ff_ref)
            sr, si = s[0:r], s[r:2 * r]
            kr, ki = k_ref[j, 0:r, :], k_ref[j, r:2 * r, :]
            p = jnp.concatenate([sr * kr - si * ki, sr * ki + si * kr], axis=0).astype(BF16)
            b_ref[b, j] = _dot(fi_ref[...], p).astype(BF16)


def _dft_mid(a, kspec, order, f_fwd, f_inv):
    bsz, r, _, c = a.shape
    a5 = a.reshape(bsz, r, 2, r, c)
    return pl.pallas_call(
        _dft_mid_kernel,
        grid=(r // DFT_STEP,),
        in_specs=[pl.BlockSpec((bsz, r, 2, DFT_STEP, c), lambda i: (0, 0, 0, i, 0)),
                  pl.BlockSpec((None, DFT_STEP, 2 * r, c), lambda i: (order, i, 0, 0)),
                  pl.BlockSpec(f_fwd.shape, lambda i: (0, 0)),
                  pl.BlockSpec(f_inv.shape, lambda i: (0, 0))],
        out_specs=pl.BlockSpec((bsz, DFT_STEP, 2 * r, c), lambda i: (0, i, 0, 0)),
        out_shape=jax.ShapeDtypeStruct((bsz, r, 2 * r, c), BF16),
        compiler_params=_params("parallel"),
        name="dft_mid",
    )(a5, kspec, f_fwd, f_inv)


def _dft_out_kernel(b_ref, g_ref, u_ref, x_ref, bias_ref, o_ref):
    for j in range(DFT_STEP):
        rhs = jnp.concatenate([b_ref[:, 0, j, :], b_ref[:, 1, j, :]], axis=0)
        y = _dot(g_ref[j], rhs)
        o_ref[:, j, :] = x_ref[:, j, :] * (y + u_ref[:, j, :] * bias_ref[...])


def _dft_out(bm, g_out, u4, u_col, x4, x_col, bias):
    bsz, r, _, c = bm.shape
    rh = r // 2
    b5 = bm.reshape(bsz, r, 2, r, c)
    seq = lambda col: pl.BlockSpec((None, rh, DFT_STEP, c), lambda b, i: (b, 0, i, col))
    return pl.pallas_call(
        _dft_out_kernel,
        grid=(bsz, r // DFT_STEP),
        in_specs=[pl.BlockSpec((None, r, 2, DFT_STEP, c), lambda b, i: (b, 0, 0, i, 0)),
                  pl.BlockSpec((DFT_STEP, rh, 2 * r), lambda b, i: (i, 0, 0)),
                  seq(u_col), seq(x_col), pl.BlockSpec((1, c), lambda b, i: (0, 0))],
        out_specs=seq(0),
        out_shape=jax.ShapeDtypeStruct((bsz, rh, r, c), F32),
        compiler_params=_params("parallel", "parallel"),
        name="dft_out",
    )(b5, g_out, u4, x4, bias.reshape(1, c))


def _hyena_filter_stage1(n, filter_params):
    r, g_in, _, _, _ = _dft_tables(n)
    filt = _hy_filters(n, *filter_params)
    return _dft_in(filt.reshape(-1, r // 2, r, HY_CH), 0, g_in)


def _hyena_filter_spectra(n, stage1):
    return _dft_filt(stage1, _dft_tables(n)[3])


def _hyena(hy_in, kspec, conv_bias):
    bsz, n, _ = hy_in.shape
    r, g_in, g_out, f_fwd, f_inv = _dft_tables(n)
    seq4 = hy_in.reshape(bsz, r // 2, r, 3 * HY_CH)
    zz = _dft_out(_dft_mid(_dft_in(seq4, 0, g_in), kspec, 0, f_fwd, f_inv), g_out,
                  seq4, 0, seq4, 1, conv_bias[0])
    out = _dft_out(_dft_mid(_dft_in(zz, 0, g_in), kspec, 1, f_fwd, f_inv), g_out,
                   zz, 0, seq4, 2, conv_bias[1])
    return out.reshape(bsz, n, HY_CH)


CD_HY, CD_Z, CD_XBC, CD_DT = 0, 1536, 2048, 3072
CD_PAD_COLS = CD_DT + 2 * LANES


def _reorder_ab(w):
    gq, gk, gv, gg, lr_f, lr_b, hq, hf_f, hf_b, hi, hg = jnp.split(
        w, np.cumsum([256, 256, 512, 512, 16, 16, 512, 512, 512, 512, 512])[:-1].tolist(), axis=-1)
    pad = jnp.zeros((w.shape[0], AB_PAD_COLS - AB_LR - 2 * GLA_LOW_RANK), w.dtype)
    return jnp.concatenate([gq, gk, gv, gg, hq, hf_f, hf_b, hi, hg, lr_f, lr_b, pad], axis=-1)


def _reorder_cd(w):
    hy, z, xbc, dt_f, dt_b = jnp.split(w, np.cumsum([1536, 512, 1024, 8, 8])[:-1].tolist(), axis=-1)
    pad = jnp.zeros((w.shape[0], LANES - MB_HEADS), w.dtype)
    return jnp.concatenate([hy, z, xbc, dt_f, pad, dt_b, pad], axis=-1)


def kernel(x, c, ctx, c_ctx, ada_w, ada_b, norm_mix_g, norm_ffn_g, norm_out_g, ab_w_in, ab_w_out, gla_gate_w, gla_gate_b, gla_norm_g, hg_lb, hg_norm_g, cd_w_in, cd_w_out, hy_short_w, hy_short_b, hy_w1, hy_b1, hy_w2, hy_b2, hy_w3, hy_freq, hy_bias, mb_conv_w, mb_conv_b, mb_dt_bias, mb_a_log, mb_d, mb_norm_g, router_w, router_b, moe_w_gate, moe_w_up, moe_w_down):
    bsz, n_lat, d = x.shape
    n_ctx = ctx.shape[1]
    t = n_ctx + n_lat
    assert ada_w.shape[0] == 2 and ab_w_in.shape[0] == 1 and cd_w_in.shape[0] == 1

    cond = jnp.zeros((8, d), F32).at[:bsz].set(c).at[bsz].set(c_ctx)
    m = _adaln(cond, ada_w, ada_b)

    def mods(layer):
        lat = m[layer, :bsz].reshape(bsz, 6, d)
        cx = jnp.broadcast_to(m[layer, bsz].reshape(1, 6, d), (bsz, 6, d))
        both = jnp.stack([cx, lat], axis=1)
        return [both[:, :, j][:, :, None, :] for j in range(6)]

    lb_all = jnp.cumsum(jax.nn.softmax(hg_lb.astype(F32), axis=1), axis=1)
    router_w_pad = jnp.zeros((d, LANES), F32).at[:, :N_EXPERTS].set(router_w)
    h = jnp.concatenate([ctx, x], axis=1)

    sh_m, sc_m, gt_m, sh_f, sc_f, gt_f = mods(0)
    proj = _norm_proj(h, norm_mix_g[0], sh_m, sc_m, _reorder_ab(ab_w_in[0]).astype(BF16), n_ctx)
    gwp = [jnp.zeros((LANES, GLA_KEY_W), F32).at[GLA_LOW_RANK * dd:GLA_LOW_RANK * (dd + 1)].set(gla_gate_w[0, dd])
           for dd in range(2)]
    o_gla = _gla_scan(proj, gwp, [gla_gate_b[0, dd].reshape(1, -1) for dd in range(2)], n_ctx)
    o_hg = _hgrn_scan(proj, [lb_all[dd, 0].reshape(1, -1) for dd in range(2)], n_ctx)
    h = _mix_out_ab(h, o_gla, o_hg, proj, gla_norm_g[0], hg_norm_g[0], ab_w_out[0].astype(BF16), gt_m, n_ctx)
    filter_params = (hy_w1[0], hy_b1[0], hy_w2[0], hy_b2[0], hy_w3[0], hy_freq[0])
    side = ((functools.partial(_hyena_filter_stage1, n_lat), filter_params),
            functools.partial(_hyena_filter_spectra, n_lat))
    h, kspec = _moe(h, norm_ffn_g[0], sh_f, sc_f, gt_f, router_w_pad, router_b,
                    0, moe_w_gate, moe_w_up, moe_w_down, n_ctx, side=side)

    sh_m, sc_m, gt_m, sh_f, sc_f, gt_f = mods(1)
    proj = _norm_proj(h, norm_mix_g[1], sh_m, sc_m, _reorder_cd(cd_w_in[0]).astype(BF16), n_ctx)
    hy_in = _dwconv(proj, CD_HY, 3 * HY_CH, hy_short_w[0], hy_short_b[0], (n_ctx, t), act=False)
    hy = _hyena(hy_in, kspec, hy_bias[0])
    xbc = _dwconv(proj, CD_XBC, MB_INNER + 2 * MB_BC_W, mb_conv_w[0], mb_conv_b[0], (0, n_ctx, t), act=True)
    y_ssd = [_ssd_scan(xbc, proj, CD_DT // LANES + dd, mb_dt_bias[0, dd], mb_a_log[0, dd], n_ctx, dd == 1)
             for dd in range(2)]
    d_skip_x = jnp.repeat(mb_d[0], MB_HEAD_DIM).reshape(1, MB_INNER)
    h = _mix_out_cd(h, hy, y_ssd, xbc, proj, CD_Z // MB_INNER, d_skip_x, mb_norm_g[0],
                    cd_w_out[0].astype(BF16), gt_m, n_ctx)
    return _moe(h, norm_ffn_g[1], sh_f, sc_f, gt_f, router_w_pad, router_b,
                1, moe_w_gate, moe_w_up, moe_w_down, 0, final_g=norm_out_g)
```

```python
import functools
import math

import numpy as np
import jax
import jax.numpy as jnp
from jax import lax
from jax.experimental import pallas as pl
from jax.experimental.pallas import tpu as pltpu
from jax.experimental.pallas import tpu_sc as plsc

NORM_EPS = 1e-6
GLA_HEADS, GLA_DK, GLA_DV, GLA_LOW_RANK, GLA_TAU = 4, 64, 128, 16, 16.0
GLA_KEY_W, GLA_VAL_W = GLA_HEADS * GLA_DK, GLA_HEADS * GLA_DV
HG_HEADS, HG_EXPAND, HG_DV = 4, 128, 128
HG_KEY_W, HG_VAL_W = HG_HEADS * HG_EXPAND, HG_HEADS * HG_DV
HY_CH, HY_ORDER, HY_SHORT, HY_BANDS, HY_FILT_HID = 512, 2, 3, 16, 64
HY_MIN_DECAY = math.log(1e-2) / 1.5
HY_MAX_DECAY = math.log(1e-2) / 0.3
MB_HEADS, MB_HEAD_DIM, MB_GROUPS, MB_STATE = 8, 64, 2, 128
MB_INNER = MB_HEADS * MB_HEAD_DIM
MB_BC_W = MB_GROUPS * MB_STATE
N_EXPERTS, N_GROUPS, TOP_K, MOE_BLOCK = 16, 4, 2, 256
EXPERTS_PER_GROUP = N_EXPERTS // N_GROUPS

LANES = 128
SCAN_CHUNK = 64
SCAN_BLOCK = 128
SSD_CHUNK = 128
ROW_TILE = 256
VMEM_LIMIT = 56 * 1024 * 1024

BF16 = jnp.bfloat16
F32 = jnp.float32


def _params(*sem):
    return pltpu.CompilerParams(dimension_semantics=sem, vmem_limit_bytes=VMEM_LIMIT)


def _split3(x):
    hi = x.astype(BF16)
    r1 = x - hi.astype(F32)
    mid = r1.astype(BF16)
    lo = (r1 - mid.astype(F32)).astype(BF16)
    return hi, mid, lo


def _dot(a, b):
    return jnp.dot(a, b, preferred_element_type=F32)


def _dot_nt(a, b):
    return lax.dot_general(a, b, (((1,), (1,)), ((), ())), preferred_element_type=F32)


def _dot_tn(a, b):
    return lax.dot_general(a, b, (((0,), (0,)), ((), ())), preferred_element_type=F32)


def _sel_dot(m01, x):
    hi, mid, lo = _split3(x)
    return _dot(m01, hi) + (_dot(m01, mid) + _dot(m01, lo))


def _dot_sel(x, m01):
    hi, mid, lo = _split3(x)
    return _dot(hi, m01) + (_dot(mid, m01) + _dot(lo, m01))


def _dot_f32(a, b):
    ah = a.astype(BF16)
    al = (a - ah.astype(F32)).astype(BF16)
    bh = b.astype(BF16)
    bl = (b - bh.astype(F32)).astype(BF16)
    return _dot(ah, bh) + (_dot(ah, bl) + _dot(al, bh))


def _silu(x):
    return x * (1.0 / (1.0 + jnp.exp(-x)))


def _sigmoid(x):
    return 1.0 / (1.0 + jnp.exp(-x))


def _softplus(x):
    return jnp.maximum(x, 0.0) + jnp.log(1.0 + jnp.exp(-jnp.abs(x)))


def _rms(x, g):
    return x * lax.rsqrt(jnp.mean(x * x, axis=-1, keepdims=True) + NORM_EPS) * g


def _adaln_kernel(c_ref, w_ref, b_ref, o_ref):
    o_ref[...] = _dot_f32(_silu(c_ref[...]), w_ref[...]) + b_ref[...]


def _adaln(cond, w, b):
    n_l, d, n6 = w.shape
    tn = 1536
    return pl.pallas_call(
        _adaln_kernel,
        grid=(n_l, n6 // tn),
        in_specs=[pl.BlockSpec((8, d), lambda l, j: (0, 0)),
                  pl.BlockSpec((None, d, tn), lambda l, j: (l, 0, j)),
                  pl.BlockSpec((None, 1, tn), lambda l, j: (l, 0, j))],
        out_specs=pl.BlockSpec((None, 8, tn), lambda l, j: (l, 0, j)),
        out_shape=jax.ShapeDtypeStruct((n_l, 8, n6), F32),
        compiler_params=_params("parallel", "parallel"),
        name="adaln",
    )(cond, w, b.reshape(n_l, 1, n6))


def _norm_proj_kernel(h_ref, g_ref, sh_ref, sc_ref, w_ref, o_ref):
    u = _rms(h_ref[...], g_ref[...]) * (1.0 + sc_ref[...]) + sh_ref[...]
    o_ref[...] = _dot(u.astype(BF16), w_ref[...])


def _norm_proj(h, g, shift, scale, w, n_ctx):
    bsz, t, d = h.shape
    n = w.shape[1]
    tm = ROW_TILE
    seg = lambda b, i: (b, (i * tm >= n_ctx).astype(jnp.int32), 0, 0)
    return pl.pallas_call(
        _norm_proj_kernel,
        grid=(bsz, t // tm),
        in_specs=[pl.BlockSpec((None, tm, d), lambda b, i: (b, i, 0)),
                  pl.BlockSpec((1, d), lambda b, i: (0, 0)),
                  pl.BlockSpec((None, None, 1, d), seg),
                  pl.BlockSpec((None, None, 1, d), seg),
                  pl.BlockSpec((d, n), lambda b, i: (0, 0))],
        out_specs=pl.BlockSpec((None, tm, n), lambda b, i: (b, i, 0)),
        out_shape=jax.ShapeDtypeStruct((bsz, t, n), F32),
        compiler_params=_params("parallel", "parallel"),
        name="norm_proj",
    )(h, g.reshape(1, d), shift, scale, w)


def _scan_constants(c, reverse):
    t = np.arange(c)[:, None]
    u = np.arange(c)[None, :]
    sels = [u <= t, u > t]
    masks = []
    m = c // 2
    while m >= 1:
        blk = t // (2 * m)
        upper_t = (t % (2 * m)) >= m
        r = blk * (2 * m) + m - 1
        s_blk = u // (2 * m)
        upper_s = (u % (2 * m)) >= m
        sels.append((upper_t & (u > r) & (u <= t)) | ((~upper_t) & (u > t) & (u <= r)))
        masks.append((blk == s_blk) & upper_t & (~upper_s))
        m //= 2
    masks.append(t == u)
    sel = np.stack(sels).astype(np.float32)
    msk = np.stack(masks).astype(np.float32)
    if reverse:
        sel = sel[:, ::-1, ::-1]
        msk = msk[:, ::-1, ::-1]
    return np.ascontiguousarray(sel.reshape(-1, c)), np.ascontiguousarray(msk)


def _chunk_order(i, n_ctx_chunks, n_chunks, reverse):
    if not reverse:
        return i
    return jnp.where(i < n_ctx_chunks, n_ctx_chunks - 1 - i, n_chunks - 1 - (i - n_ctx_chunks))


GROUP_KEYS = 256


def _decay_chunk(q, k, v, la, sel_ref, mask_ref, hm_ref, st_ref, heads, dk, dv):
    c = q.shape[0]
    n_lvl = mask_ref.shape[0] - 1
    hpg = GROUP_KEYS // dk
    cs = _dot(sel_ref[...], jnp.concatenate(_split3(la), axis=0))
    e_q = jnp.exp(cs[0:c])
    e_k = jnp.exp(cs[c:2 * c])
    e_tot = jnp.exp(jnp.sum(la, axis=0, keepdims=True))
    vb = v.astype(BF16)
    outs = []
    for g in range(heads // hpg):
        ks = slice(g * GROUP_KEYS, (g + 1) * GROUP_KEYS)
        vs = slice(g * hpg * dv, (g + 1) * hpg * dv)
        qg, kg = q[:, ks], k[:, ks]
        stack = lambda x: jnp.concatenate([x * hm_ref[h] for h in range(hpg)], axis=0).astype(BF16)
        att = mask_ref[n_lvl] * _dot_nt(stack(qg), kg.astype(BF16))
        for l in range(n_lvl):
            e = jnp.exp(cs[(2 + l) * c:(3 + l) * c, ks])
            att = att + mask_ref[l] * _dot_nt(stack(qg * e), (kg * e).astype(BF16))
        att = att.astype(BF16)
        st = st_ref[g]
        inter = _dot_nt(stack(qg * e_q[:, ks]), st.astype(BF16))
        upd = _dot_tn(vb[:, vs], (kg * e_k[:, ks]).astype(BF16))
        new = st * e_tot[:, ks]
        for h in range(hpg):
            rows = slice(h * c, (h + 1) * c)
            hv = slice(g * hpg * dv + h * dv, g * hpg * dv + (h + 1) * dv)
            outs.append(_dot(att[rows], vb[:, hv]) + inter[rows])
            new = new + upd[h * dv:(h + 1) * dv] * hm_ref[h]
        st_ref[g] = new
    return jnp.concatenate(outs, axis=-1)


def _log_sigmoid(x):
    return jnp.minimum(x, 0.0) - jnp.log(1.0 + jnp.exp(-jnp.abs(x)))


def _gla_kernel(*refs):
    ins, hm_ref, (o_refs, st_refs) = (refs[0:8], refs[8:16]), refs[16], (refs[17:19], refs[19:21])

    @pl.when(pl.program_id(1) == 0)
    def _():
        for st_ref in st_refs:
            st_ref[...] = jnp.zeros_like(st_ref)

    for d, ((q_ref, k_ref, v_ref, lr_ref, gw_ref, gb_ref, sel_ref, mask_ref), o_ref, st_ref) in enumerate(
            zip(ins, o_refs, st_refs)):
        z = _dot_f32(lr_ref[...], gw_ref[...]) + gb_ref[...]
        la = _log_sigmoid(z) * (1.0 / GLA_TAU)
        q = q_ref[...] * (GLA_DK ** -0.5)
        k, v = k_ref[...], v_ref[...]
        for rows in _sub_chunks(q.shape[0], d == 1):
            o_ref[rows, :] = _decay_chunk(q[rows], k[rows], v[rows], la[rows], sel_ref, mask_ref, hm_ref,
                                          st_ref, GLA_HEADS, GLA_DK, GLA_DV)


def _hgrn_kernel(*refs):
    ins, hm_ref, (o_refs, st_refs) = (refs[0:6], refs[6:12]), refs[12], (refs[13:15], refs[15:17])

    @pl.when(pl.program_id(1) == 0)
    def _():
        for st_ref in st_refs:
            st_ref[...] = jnp.zeros_like(st_ref)

    for d, ((q_ref, f_ref, v_ref, lb_ref, sel_ref, mask_ref), o_ref, st_ref) in enumerate(
            zip(ins, o_refs, st_refs)):
        lb = lb_ref[...]
        f = lb + (1.0 - lb) * _sigmoid(f_ref[...])
        q, k, v, la = _silu(q_ref[...]), 1.0 - f, v_ref[...], jnp.log(f)
        for rows in _sub_chunks(q.shape[0], d == 1):
            o_ref[rows, :] = _decay_chunk(q[rows], k[rows], v[rows], la[rows], sel_ref, mask_ref, hm_ref,
                                          st_ref, HG_HEADS, HG_EXPAND, HG_DV)


def _sub_chunks(rows, reverse):
    order = range(rows // SCAN_CHUNK)
    return [slice(j * SCAN_CHUNK, (j + 1) * SCAN_CHUNK) for j in (reversed(order) if reverse else order)]


def _scan_specs(blk, n_ctx, t, reverse, chunk=None, stacked_heads=1):
    n_blocks = t // blk
    order = functools.partial(_chunk_order, n_ctx_chunks=n_ctx // blk, n_chunks=n_blocks, reverse=reverse)

    def col(width, idx):
        return pl.BlockSpec((None, blk, width), lambda b, i: (b, order(i), idx))

    sel, msk = _scan_constants(chunk or blk, reverse)
    sel3 = np.concatenate([sel, sel, sel], axis=1)
    msk = np.tile(msk, (1, stacked_heads, 1))
    const = lambda a: pl.BlockSpec(a.shape, lambda b, i: (0,) * a.ndim)
    return n_blocks, col, const, jnp.asarray(sel3, BF16), jnp.asarray(msk, F32)


def _head_masks(dk):
    hpg = GROUP_KEYS // dk
    hm = np.zeros((hpg, 1, GROUP_KEYS), np.float32)
    for h in range(hpg):
        hm[h, 0, h * dk:(h + 1) * dk] = 1.0
    return jnp.asarray(hm)


AB_Q, AB_K, AB_V, AB_G = 0, 256, 512, 1024
AB_HQ, AB_HF, AB_HI, AB_HG, AB_LR = 1536, 2048, 3072, 3584, 4096
AB_PAD_COLS = 4224


def _gla_scan(proj, gate_w_pad, gate_b, n_ctx):
    bsz, t, _ = proj.shape
    hpg = GROUP_KEYS // GLA_DK
    in_specs, args, outs = [], [], []
    for d in range(2):
        n_blocks, col, const, sel, msk = _scan_specs(SCAN_BLOCK, n_ctx, t, d == 1, SCAN_CHUNK, hpg)
        in_specs += [col(GLA_KEY_W, AB_Q // GLA_KEY_W), col(GLA_KEY_W, AB_K // GLA_KEY_W),
                     col(GLA_VAL_W, AB_V // GLA_VAL_W), col(LANES, AB_LR // LANES),
                     const(gate_w_pad[d]), const(gate_b[d]), const(sel), const(msk)]
        args += [proj, proj, proj, proj, gate_w_pad[d], gate_b[d], sel, msk]
        outs.append(col(GLA_VAL_W, 0))
    hm = _head_masks(GLA_DK)
    return pl.pallas_call(
        _gla_kernel,
        grid=(bsz, n_blocks),
        in_specs=in_specs + [const(hm)],
        out_specs=outs,
        out_shape=[jax.ShapeDtypeStruct((bsz, t, GLA_VAL_W), F32)] * 2,
        scratch_shapes=[pltpu.VMEM((GLA_HEADS // hpg, GLA_DV, GROUP_KEYS), F32)] * 2,
        compiler_params=_params("parallel", "arbitrary"),
        name="gla_scan",
    )(*args, hm)


def _hgrn_scan(proj, lb, n_ctx):
    bsz, t, _ = proj.shape
    hpg = GROUP_KEYS // HG_EXPAND
    in_specs, args, outs = [], [], []
    for d in range(2):
        n_blocks, col, const, sel, msk = _scan_specs(SCAN_BLOCK, n_ctx, t, d == 1, SCAN_CHUNK, hpg)
        in_specs += [col(HG_KEY_W, AB_HQ // HG_KEY_W), col(HG_KEY_W, AB_HF // HG_KEY_W + d),
                     col(HG_VAL_W, AB_HI // HG_VAL_W), const(lb[d]), const(sel), const(msk)]
        args += [proj, proj, proj, lb[d], sel, msk]
        outs.append(col(HG_VAL_W, 0))
    hm = _head_masks(HG_EXPAND)
    return pl.pallas_call(
        _hgrn_kernel,
        grid=(bsz, n_blocks),
        in_specs=in_specs + [const(hm)],
        out_specs=outs,
        out_shape=[jax.ShapeDtypeStruct((bsz, t, HG_VAL_W), F32)] * 2,
        scratch_shapes=[pltpu.VMEM((HG_HEADS // hpg, HG_DV, GROUP_KEYS), F32)] * 2,
        compiler_params=_params("parallel", "arbitrary"),
        name="hgrn_scan",
    )(*args, hm)


def _mix_out_ab_kernel(h_ref, gf_ref, gb_ref, hf_ref, hb_ref, gg_ref, hg_ref, gn_ref, hn_ref,
                       w_ref, gt_ref, o_ref):
    feats = []
    for o, gate, g in ((gf_ref[...] + gb_ref[...], gg_ref[...], gn_ref[...]),
                       (hf_ref[...] + hb_ref[...], hg_ref[...], hn_ref[...])):
        for hd in range(o.shape[-1] // LANES):
            s = slice(hd * LANES, (hd + 1) * LANES)
            feats.append(_rms(o[:, s], g) * _silu(gate[:, s]))
    feat = jnp.concatenate(feats, axis=-1).astype(BF16)
    o_ref[...] = h_ref[...] + gt_ref[...] * _dot(feat, w_ref[...])


def _mix_out_ab(h, o_gla, o_hg, proj, gla_norm_g, hg_norm_g, w_out, gate, n_ctx):
    bsz, t, d = h.shape
    tm = ROW_TILE
    seg = lambda b, i: (b, (i * tm >= n_ctx).astype(jnp.int32), 0, 0)
    row = lambda width, idx: pl.BlockSpec((None, tm, width), lambda b, i: (b, i, idx))
    vec = pl.BlockSpec((1, LANES), lambda b, i: (0, 0))
    return pl.pallas_call(
        _mix_out_ab_kernel,
        grid=(bsz, t // tm),
        in_specs=[row(d, 0), row(GLA_VAL_W, 0), row(GLA_VAL_W, 0), row(HG_VAL_W, 0), row(HG_VAL_W, 0),
                  row(GLA_VAL_W, AB_G // GLA_VAL_W), row(HG_VAL_W, AB_HG // HG_VAL_W), vec, vec,
                  pl.BlockSpec(w_out.shape, lambda b, i: (0, 0)),
                  pl.BlockSpec((None, None, 1, d), seg)],
        out_specs=row(d, 0),
        out_shape=jax.ShapeDtypeStruct((bsz, t, d), F32),
        compiler_params=_params("parallel", "parallel"),
        name="mix_out_ab",
    )(h, o_gla[0], o_gla[1], o_hg[0], o_hg[1], proj, proj, gla_norm_g.reshape(1, -1),
      hg_norm_g.reshape(1, -1), w_out, gate)


def _dwconv_kernel(x_ref, prev_ref, next_ref, w_ref, b_ref, o_ref, *, seg_tiles, act):
    i = pl.program_id(1)
    tm = x_ref.shape[0]
    x = x_ref[...]
    first = functools.reduce(jnp.logical_or, [i == s for s in seg_tiles[:-1]])
    last = functools.reduce(jnp.logical_or, [i == s - 1 for s in seg_tiles[1:]])
    prev_row = jnp.where(first, 0.0, prev_ref[7:8, :])
    next_row = jnp.where(last, 0.0, next_ref[0:1, :])
    rows = lax.broadcasted_iota(jnp.int32, x.shape, 0)
    x_prev = jnp.where(rows == 0, prev_row, pltpu.roll(x, 1, axis=0))
    x_next = jnp.where(rows == tm - 1, next_row, pltpu.roll(x, tm - 1, axis=0))
    y = x_prev * w_ref[0:1, :] + x * w_ref[1:2, :] + x_next * w_ref[2:3, :] + b_ref[...]
    o_ref[...] = _silu(y) if act else y


def _dwconv(proj, col0, width, w, b, seg_bounds, act):
    bsz, t, _ = proj.shape
    tm = ROW_TILE
    off = seg_bounds[0] // tm
    n_tiles = t // tm - off
    seg_tiles = tuple(s // tm - off for s in seg_bounds)
    cb = col0 // width
    r8 = tm // 8
    last8 = t // 8 - 1
    kern = functools.partial(_dwconv_kernel, seg_tiles=seg_tiles, act=act)
    return pl.pallas_call(
        kern,
        grid=(bsz, n_tiles),
        in_specs=[pl.BlockSpec((None, tm, width), lambda bb, i: (bb, i + off, cb)),
                  pl.BlockSpec((None, 8, width), lambda bb, i: (bb, jnp.maximum((i + off) * r8 - 1, 0), cb)),
                  pl.BlockSpec((None, 8, width),
                               lambda bb, i: (bb, jnp.minimum((i + off + 1) * r8, last8), cb)),
                  pl.BlockSpec((3, width), lambda bb, i: (0, 0)),
                  pl.BlockSpec((1, width), lambda bb, i: (0, 0))],
        out_specs=pl.BlockSpec((None, tm, width), lambda bb, i: (bb, i, 0)),
        out_shape=jax.ShapeDtypeStruct((bsz, n_tiles * tm, width), F32),
        compiler_params=_params("parallel", "parallel"),
        name="dwconv",
    )(proj, proj, proj, w.T, b.reshape(1, -1))


def _ssd_kernel(xbc_ref, dt_ref, bias_ref, alog_ref, hexp_ref, mq_ref, mk_ref, mask_ref, o_ref, st_ref):
    @pl.when(pl.program_id(1) == 0)
    def _():
        st_ref[...] = jnp.zeros_like(st_ref)

    c = xbc_ref.shape[0]
    hpg = MB_HEADS // MB_GROUPS
    gw = hpg * MB_HEAD_DIM
    dt = _softplus(dt_ref[...] + bias_ref[...])
    la = -dt * jnp.exp(alog_ref[...])
    cq = _sel_dot(mq_ref[...], la)
    ck = _sel_dot(mk_ref[...], la)
    cq_t = _dot_nt_sel(la, mq_ref[...])
    hexp = hexp_ref[...]
    dt_x = _dot_sel(dt, hexp)
    eq_x = jnp.exp(_dot_sel(cq, hexp))
    ek_x = jnp.exp(_dot_sel(ck, hexp))
    etot_x = jnp.exp(_dot_sel(jnp.sum(la, axis=0, keepdims=True), hexp))
    xs = xbc_ref[:, 0:MB_INNER] * dt_x
    mask = mask_ref[...]
    outs = []
    for g in range(MB_GROUPS):
        bm = xbc_ref[:, MB_INNER + g * MB_STATE:MB_INNER + (g + 1) * MB_STATE].astype(BF16)
        cm = xbc_ref[:, MB_INNER + MB_BC_W + g * MB_STATE:MB_INNER + MB_BC_W + (g + 1) * MB_STATE].astype(BF16)
        cb = _dot_nt(cm, bm)
        st = st_ref[g]
        gs = slice(g * gw, (g + 1) * gw)
        y_inter = _dot(cm, st.astype(BF16)) * eq_x[:, gs]
        for r in range(hpg):
            hd = g * hpg + r
            diff = cq[:, hd:hd + 1] - cq_t[hd:hd + 1, :]
            w = cb * jnp.exp(jnp.where(mask > 0.0, diff, -jnp.inf))
            ps = slice(hd * MB_HEAD_DIM, (hd + 1) * MB_HEAD_DIM)
            outs.append(_dot(w.astype(BF16), xs[:, ps].astype(BF16))
                        + y_inter[:, r * MB_HEAD_DIM:(r + 1) * MB_HEAD_DIM])
        st_ref[g] = st * etot_x[:, gs] + _dot_tn(bm, (xs[:, gs] * ek_x[:, gs]).astype(BF16))
    o_ref[...] = jnp.concatenate(outs, axis=-1)


def _dot_nt_sel(x, m01):
    hi, mid, lo = _split3(x)
    f = lambda p: lax.dot_general(p, m01, (((0,), (1,)), ((), ())), preferred_element_type=F32)
    return f(hi) + (f(mid) + f(lo))


def _ssd_scan(xbc, proj, dt_col, dt_bias, a_log, n_ctx, reverse):
    bsz, t, _ = xbc.shape
    c = SSD_CHUNK
    n_chunks, col, const, sel, msk = _scan_specs(c, n_ctx, t, reverse)
    mq, mk = sel[0:c, 0:c], sel[c:2 * c, 0:c]
    tri = np.tril(np.ones((c, c), np.float32))
    mask = jnp.asarray(tri[::-1, ::-1].copy() if reverse else tri)
    pad = lambda v: jnp.zeros((1, LANES), F32).at[0, :MB_HEADS].set(v)
    hexp = np.zeros((LANES, MB_INNER), np.float32)
    for hd in range(MB_HEADS):
        hexp[hd, hd * MB_HEAD_DIM:(hd + 1) * MB_HEAD_DIM] = 1.0
    hexp = jnp.asarray(hexp, BF16)
    bias = pad(dt_bias)
    alog = pad(a_log.astype(F32))
    return pl.pallas_call(
        _ssd_kernel,
        grid=(bsz, n_chunks),
        in_specs=[col(xbc.shape[-1], 0), col(LANES, dt_col), const(bias), const(alog), const(hexp),
                  const(mq), const(mk), const(mask)],
        out_specs=col(MB_INNER, 0),
        out_shape=jax.ShapeDtypeStruct((bsz, t, MB_INNER), F32),
        scratch_shapes=[pltpu.VMEM((MB_GROUPS, MB_STATE, MB_INNER // MB_GROUPS), F32)],
        compiler_params=_params("parallel", "arbitrary"),
        name="ssd_scan_rev" if reverse else "ssd_scan_fwd",
    )(xbc, proj, bias, alog, hexp, mq, mk, mask)


def _mix_out_cd_kernel(h_ref, hy_ref, yf_ref, yb_ref, xs_ref, z_ref, dsk_ref, ng_ref, w_ref, gt_ref, o_ref):
    y = (yf_ref[...] + yb_ref[...] + dsk_ref[...] * xs_ref[...]) * _silu(z_ref[...])
    gw = MB_INNER // MB_GROUPS
    ys = [_rms(y[:, g * gw:(g + 1) * gw], ng_ref[:, g * gw:(g + 1) * gw]) for g in range(MB_GROUPS)]
    feat = jnp.concatenate([hy_ref[...]] + ys, axis=-1).astype(BF16)
    o_ref[...] = h_ref[...] + gt_ref[...] * _dot(feat, w_ref[...])


def _mix_out_cd(h, hy, y_ssd, xbc, proj, z_col, d_skip_x, norm_g, w_out, gate, n_ctx):
    bsz, t, d = h.shape
    tm = ROW_TILE
    n_lat = t - n_ctx
    off = n_ctx // tm
    row = lambda width, idx: pl.BlockSpec((None, tm, width), lambda b, i: (b, i + off, idx))
    vec = pl.BlockSpec((1, MB_INNER), lambda b, i: (0, 0))
    return pl.pallas_call(
        _mix_out_cd_kernel,
        grid=(bsz, n_lat // tm),
        in_specs=[row(d, 0), pl.BlockSpec((None, tm, HY_CH), lambda b, i: (b, i, 0)),
                  row(MB_INNER, 0), row(MB_INNER, 0), row(MB_INNER, 0), row(MB_INNER, z_col), vec, vec,
                  pl.BlockSpec(w_out.shape, lambda b, i: (0, 0)),
                  pl.BlockSpec((None, None, 1, d), lambda b, i: (b, 1, 0, 0))],
        out_specs=pl.BlockSpec((None, tm, d), lambda b, i: (b, i, 0)),
        out_shape=jax.ShapeDtypeStruct((bsz, n_lat, d), F32),
        compiler_params=_params("parallel", "parallel"),
        name="mix_out_cd",
    )(h, hy, y_ssd[0], y_ssd[1], xbc, proj, d_skip_x, norm_g.reshape(1, -1), w_out, gate)


def _top2_of4(a, b, c, d):
    hi1, lo1, hi2, lo2 = jnp.maximum(a, b), jnp.minimum(a, b), jnp.maximum(c, d), jnp.minimum(c, d)
    return jnp.maximum(hi1, hi2) + jnp.maximum(jnp.minimum(hi1, hi2), jnp.maximum(lo1, lo2))


def _first_argmax(vals, skip=None):
    idx = None
    for j, vj in enumerate(vals):
        if idx is None and skip is None:
            idx, best = jnp.zeros(vj.shape, jnp.int32), vj
            continue
        if idx is None:
            idx, best = jnp.full(vj.shape, -1, jnp.int32), jnp.full(vj.shape, -jnp.inf, F32)
        take = vj > best
        if skip is not None:
            take = jnp.logical_and(take, skip != j)
        idx = jnp.where(take, j, idx)
        best = jnp.where(take, vj, best)
    return idx, best


def _ffn_pre_kernel(h_ref, g_ref, sh_ref, sc_ref, rw_ref, rb_ref, tri_ref,
                    v_ref, ri_ref, rwt_ref, cnt_ref, carry_ref):
    @pl.when(jnp.logical_and(pl.program_id(0) == 0, pl.program_id(1) == 0))
    def _():
        carry_ref[...] = jnp.zeros_like(carry_ref)

    v = _rms(h_ref[...], g_ref[...]) * (1.0 + sc_ref[...]) + sh_ref[...]
    bits = lax.bitcast_convert_type(v.astype(BF16).astype(F32), jnp.uint32)
    half = v.shape[1] // 2
    v_ref[...] = bits[:, :half] | (bits[:, half:] >> 16)
    st = _sigmoid(_dot_f32(v, rw_ref[...])).T[0:N_EXPERTS]
    sel = st + rb_ref[...]
    row = lambda a, e: a[e:e + 1]
    epg = EXPERTS_PER_GROUP
    gscore = [_top2_of4(*[row(sel, g * epg + j) for j in range(epg)]) for g in range(N_GROUPS)]
    best, _ = _first_argmax(gscore)

    def in_best(a, j):
        out = row(a, j)
        for g in range(1, N_GROUPS):
            out = jnp.where(best == g, row(a, g * epg + j), out)
        return out

    vals = [in_best(sel, j) for j in range(epg)]
    raw = [in_best(st, j) for j in range(epg)]
    i1, _ = _first_argmax(vals)
    i2, _ = _first_argmax(vals, skip=i1)
    pick = lambda i: functools.reduce(lambda acc, j: jnp.where(i == j, raw[j], acc), range(1, epg), raw[0])
    w1, w2 = pick(i1), pick(i2)
    wsum = w1 + w2
    e1, e2 = best * epg + i1, best * epg + i2

    experts = lax.broadcasted_iota(jnp.int32, st.shape, 0)
    oh1 = (experts == e1).astype(F32)
    oh2 = (experts == e2).astype(F32)
    cnt = oh1 + oh2
    before = _dot(cnt.astype(BF16), tri_ref[...]) + carry_ref[:, 0:1]
    ri_ref[0:1, :] = e1
    ri_ref[1:2, :] = e2
    ri_ref[2:3, :] = jnp.sum(oh1 * before, axis=0, keepdims=True).astype(jnp.int32)
    ri_ref[3:4, :] = jnp.sum(oh2 * before, axis=0, keepdims=True).astype(jnp.int32)
    ri_ref[4:8, :] = jnp.zeros((4, st.shape[1]), jnp.int32)
    rwt_ref[0:1, :] = w1 / wsum
    rwt_ref[1:2, :] = w2 / wsum
    rwt_ref[2:8, :] = jnp.zeros((6, st.shape[1]), F32)
    carry_ref[...] = carry_ref[...] + jnp.sum(cnt, axis=1, keepdims=True)
    cnt_ref[...] = carry_ref[...]


def _ffn_pre(h, g, shift, scale, router_w_pad, router_b, n_ctx):
    bsz, t, d = h.shape
    tm = ROW_TILE
    seg = lambda b, i: (b, (i * tm >= n_ctx).astype(jnp.int32), 0, 0)
    tri = jnp.asarray(np.triu(np.ones((tm, tm), np.float32), 1), BF16)
    return pl.pallas_call(
        _ffn_pre_kernel,
        grid=(bsz, t // tm),
        in_specs=[pl.BlockSpec((None, tm, d), lambda b, i: (b, i, 0)),
                  pl.BlockSpec((1, d), lambda b, i: (0, 0)),
                  pl.BlockSpec((None, None, 1, d), seg),
                  pl.BlockSpec((None, None, 1, d), seg),
                  pl.BlockSpec((d, LANES), lambda b, i: (0, 0)),
                  pl.BlockSpec((N_EXPERTS, 1), lambda b, i: (0, 0)),
                  pl.BlockSpec((tm, tm), lambda b, i: (0, 0))],
        out_specs=[pl.BlockSpec((None, tm, d // 2), lambda b, i: (b, i, 0)),
                   pl.BlockSpec((None, 8, tm), lambda b, i: (b, 0, i)),
                   pl.BlockSpec((None, 8, tm), lambda b, i: (b, 0, i)),
                   pl.BlockSpec((N_EXPERTS, LANES), lambda b, i: (0, 0))],
        out_shape=[jax.ShapeDtypeStruct((bsz, t, d // 2), jnp.uint32),
                   jax.ShapeDtypeStruct((bsz, 8, t), jnp.int32),
                   jax.ShapeDtypeStruct((bsz, 8, t), F32),
                   jax.ShapeDtypeStruct((N_EXPERTS, LANES), F32)],
        scratch_shapes=[pltpu.VMEM((N_EXPERTS, LANES), F32)],
        compiler_params=_params("arbitrary", "arbitrary"),
        name="ffn_pre",
    )(h, g.reshape(1, d), shift, scale, router_w_pad, router_b.reshape(N_EXPERTS, 1), tri)


def _experts_kernel(be_ref, nb_ref, x_ref, wg_ref, wu_ref, wd_ref, o_ref, wg_s, wu_s, wd_s):
    i = pl.program_id(0)
    prev = be_ref[jnp.maximum(i - 1, 0)]
    changed = jnp.logical_or(i == 0, be_ref[i] != prev)

    @pl.when(changed)
    def _():
        wg_s[...] = wg_ref[...].astype(BF16)
        wu_s[...] = wu_ref[...].astype(BF16)
        wd_s[...] = wd_ref[...].astype(BF16)

    @pl.when(i < nb_ref[0])
    def _():
        p = x_ref[...]
        hi = lax.bitcast_convert_type(p & jnp.uint32(0xFFFF0000), F32)
        lo = lax.bitcast_convert_type(p << 16, F32)
        x = jnp.concatenate([hi, lo], axis=1).astype(BF16)
        hid = _silu(_dot(x, wg_s[...])) * _dot(x, wu_s[...])
        o_ref[...] = _dot(hid.astype(BF16), wd_s[...])

    @pl.when(i >= nb_ref[0])
    def _():
        o_ref[...] = jnp.zeros_like(o_ref)


def _experts(xb, block_e, n_used, layer, w_gate, w_up, w_down):
    n_slots = xb.shape[0]
    n_blocks = n_slots // MOE_BLOCK
    d, de = w_gate.shape[-2:]
    wspec = lambda shape: pl.BlockSpec((None, None) + shape, lambda i, be, nb: (layer, be[i], 0, 0))
    return pl.pallas_call(
        _experts_kernel,
        grid_spec=pltpu.PrefetchScalarGridSpec(
            num_scalar_prefetch=2,
            grid=(n_blocks,),
            in_specs=[pl.BlockSpec((MOE_BLOCK, d // 2), lambda i, be, nb: (i, 0)),
                      wspec((d, de)), wspec((d, de)), wspec((de, d))],
            out_specs=pl.BlockSpec((MOE_BLOCK, d), lambda i, be, nb: (i, 0)),
            scratch_shapes=[pltpu.VMEM((d, de), BF16), pltpu.VMEM((d, de), BF16), pltpu.VMEM((de, d), BF16)]),
        out_shape=jax.ShapeDtypeStruct((n_slots, d), F32),
        compiler_params=_params("arbitrary"),
        name="moe_experts",
    )(block_e, n_used, xb, w_gate, w_up, w_down)


def _ffn_post_kernel(h_ref, y0_ref, y1_ref, w_ref, gt_ref, g_ref, o_ref, *, final):
    w = w_ref[...]
    y = w[:, 0:1] * y0_ref[...] + w[:, 1:2] * y1_ref[...]
    out = h_ref[...] + gt_ref[...] * y
    o_ref[...] = _rms(out, g_ref[...]) if final else out


def _ffn_post(h, y, w, gate, n_ctx, final_g=None):
    bsz, t, d = h.shape
    tm = ROW_TILE
    seg = lambda b, i: (b, (i * tm >= n_ctx).astype(jnp.int32), 0, 0)
    row = lambda width: pl.BlockSpec((None, tm, width), lambda b, i: (b, i, 0))
    choice = lambda kk: pl.BlockSpec((None, None, tm, d), lambda b, i: (kk, b, i, 0))
    final = final_g is not None
    g = final_g if final else jnp.ones((d,), F32)
    return pl.pallas_call(
        functools.partial(_ffn_post_kernel, final=final),
        grid=(bsz, t // tm),
        in_specs=[row(d), choice(0), choice(1), row(LANES), pl.BlockSpec((None, None, 1, d), seg),
                  pl.BlockSpec((1, d), lambda b, i: (0, 0))],
        out_specs=row(d),
        out_shape=jax.ShapeDtypeStruct((bsz, t, d), F32),
        compiler_params=_params("parallel", "parallel"),
        name="ffn_post",
    )(h, y, y, w, gate, g.reshape(1, d))


def _slot_layout(n, ri, counts):
    e = jnp.swapaxes(ri[:, 0:2], 1, 2).reshape(n, TOP_K)
    rank = jnp.swapaxes(ri[:, 2:4], 1, 2).reshape(n, TOP_K)
    padded = (counts + MOE_BLOCK - 1) // MOE_BLOCK * MOE_BLOCK
    pend = jnp.cumsum(padded)
    pstart = pend - padded
    experts = jnp.arange(N_EXPERTS, dtype=jnp.int32)
    dest = rank + jnp.sum(jnp.where(e[..., None] == experts, pstart, 0), axis=-1)
    n_slots = (n * TOP_K + MOE_BLOCK - 1) // MOE_BLOCK * MOE_BLOCK + N_EXPERTS * MOE_BLOCK
    n_blocks = n_slots // MOE_BLOCK
    tok = jnp.repeat(jnp.arange(n, dtype=jnp.int32), TOP_K)
    slot_tok = jnp.zeros((n_slots,), jnp.int32).at[dest.reshape(-1)].set(tok)
    blk0 = jnp.arange(n_blocks, dtype=jnp.int32)[:, None] * MOE_BLOCK
    block_e = jnp.minimum(jnp.sum((pend[None, :] <= blk0).astype(jnp.int32), axis=-1), N_EXPERTS - 1)
    n_used = (pend[-1] // MOE_BLOCK).astype(jnp.int32).reshape(1)
    return dest, slot_tok, block_e.astype(jnp.int32), n_used


SC_CORES, SC_SUBCORES = 2, 16
SC_WINDOW = 32


def _gather_rows(table, idx):
    n_rows, d = idx.shape[0], table.shape[1]
    workers = SC_CORES * SC_SUBCORES
    per_worker = n_rows // workers
    assert per_worker * workers == n_rows and per_worker % SC_WINDOW == 0
    mesh = plsc.VectorSubcoreMesh(core_axis_name="c", subcore_axis_name="s")

    @functools.partial(
        pl.kernel, mesh=mesh,
        out_type=jax.ShapeDtypeStruct((n_rows, d), table.dtype),
        scratch_types=[pltpu.VMEM((SC_WINDOW,), jnp.int32), pltpu.VMEM((SC_WINDOW,), jnp.int32),
                       pltpu.VMEM((SC_WINDOW, d), table.dtype), pltpu.VMEM((SC_WINDOW, d), table.dtype),
                       pltpu.SemaphoreType.DMA, pltpu.SemaphoreType.DMA],
    )
    def gather_kernel(table_hbm, idx_hbm, out_hbm, idx0, idx1, rows0, rows1, sem0, sem1):
        base = (lax.axis_index("s") * SC_CORES + lax.axis_index("c")) * per_worker
        n_win = per_worker // SC_WINDOW
        slots = ((idx0, rows0, sem0), (idx1, rows1, sem1))
        window = lambda j: pl.ds(pl.multiple_of(base + j * SC_WINDOW, 8), SC_WINDOW)

        def start(j, slot):
            idx_v, rows_v, sem = slots[slot]
            pltpu.sync_copy(idx_hbm.at[window(j)], idx_v)
            pltpu.async_copy(table_hbm.at[idx_v], rows_v, sem)

        def finish(j, slot):
            idx_v, rows_v, sem = slots[slot]
            pltpu.make_async_copy(table_hbm.at[idx_v], rows_v, sem).wait()
            pltpu.sync_copy(rows_v, out_hbm.at[window(j)])

        start(0, 0)

        @pl.loop(0, n_win, step=2)
        def _(j):
            @pl.when(j + 1 < n_win)
            def _():
                start(j + 1, 1)

            finish(j, 0)

            @pl.when(j + 2 < n_win)
            def _():
                start(j + 2, 0)

            @pl.when(j + 1 < n_win)
            def _():
                finish(j + 1, 1)

    return gather_kernel(table, idx)


def _alongside(gather, idx, side_fn, side_in):
    idx, side_in = lax.optimization_barrier((idx, side_in))
    return lax.optimization_barrier((gather(idx), side_fn(side_in)))


def _moe(h, g, shift, scale, gate, router_w_pad, router_b, layer, w_gate, w_up, w_down, n_ctx,
         final_g=None, side=None):
    bsz, t, d = h.shape
    n = bsz * t
    v, ri, rwt, counts = _ffn_pre(h, g, shift, scale, router_w_pad, router_b, n_ctx)
    dest, slot_tok, block_e, n_used = _slot_layout(n, ri, counts[:, 0].astype(jnp.int32))
    w = jnp.swapaxes(rwt[:, 0:2], 1, 2).reshape(n, TOP_K)
    dispatch = lambda idx: _gather_rows(v.reshape(n, d // 2), idx)
    if side is None:
        xb = dispatch(slot_tok)
    else:
        xb, side_a = _alongside(dispatch, slot_tok, *side[0])
    yb = _experts(xb, block_e, n_used, layer, w_gate, w_up, w_down)
    combine = lambda idx: _gather_rows(yb, idx)
    dest_flat = jnp.swapaxes(dest, 0, 1).reshape(-1)
    if side is None:
        y, side_b = combine(dest_flat), None
    else:
        y, side_b = _alongside(combine, dest_flat, side[1], side_a)
    wpad = jnp.zeros((n, LANES), F32).at[:, :TOP_K].set(w).reshape(bsz, t, LANES)
    out = _ffn_post(h, y.reshape(TOP_K, bsz, t, d), wpad, gate, n_ctx, final_g)
    return out if side is None else (out, side_b)


DFT_STEP = 8


def _dft_tables(n):
    size = 2 * n
    r = int(round(math.sqrt(size)))
    assert r * r == size and r % DFT_STEP == 0
    p1 = np.arange(r // 2)[None, None, :]
    p2 = np.arange(r)[:, None, None]
    k1 = np.arange(r)[None, :, None]
    ang = 2.0 * np.pi * (((r * p1 + p2) * k1) % size) / size
    g_re, g_im = np.cos(ang), -np.sin(ang)
    g_in = np.concatenate([g_re, g_im], axis=1)
    g_out = np.concatenate([np.swapaxes(g_re, 1, 2), np.swapaxes(g_im, 1, 2)], axis=2) / size
    a2 = 2.0 * np.pi * ((np.arange(r)[:, None] * np.arange(r)[None, :]) % r) / r
    f_re, f_im = np.cos(a2), -np.sin(a2)
    f_fwd = np.block([[f_re, -f_im], [f_im, f_re]])
    f_inv = np.block([[f_re, f_im], [-f_im, f_re]])
    cast = lambda a: jnp.asarray(a.astype(np.float32)).astype(BF16)
    return r, cast(g_in), cast(g_out), cast(f_fwd), cast(f_inv)


def _hy_filter_kernel(z_ref, w1_ref, b1_ref, w2_ref, b2_ref, w3_ref, fr_ref, rates_ref, o_ref):
    z = z_ref[...]
    hid = jnp.sin(fr_ref[...] * (_dot_f32(z, w1_ref[...]) + b1_ref[...]))
    hid = jnp.sin(fr_ref[...] * (_dot_f32(hid, w2_ref[...]) + b2_ref[...]))
    filt = _dot_f32(hid, w3_ref[...])
    decay = jnp.exp(-z[:, 0:1] * rates_ref[...])
    for q in range(o_ref.shape[0]):
        o_ref[q] = filt[:, q * HY_CH:(q + 1) * HY_CH] * decay


def _hy_filters(n, w1, b1, w2, b2, w3, freq):
    t = jnp.linspace(0.0, 1.0, n, dtype=F32)[:, None]
    bands = jnp.linspace(1e-4, HY_BANDS - 1, HY_BANDS, dtype=F32)
    ang = (2.0 * math.pi / n) * jnp.arange(n, dtype=F32)[:, None] * bands
    z = jnp.concatenate([t, jnp.cos(ang), -jnp.sin(ang)], axis=-1)
    z = jnp.pad(z, ((0, 0), (0, LANES - z.shape[1])))
    w1p = jnp.pad(w1, ((0, LANES - w1.shape[0]), (0, 0)))
    rates = jnp.abs(jnp.linspace(HY_MIN_DECAY, HY_MAX_DECAY, HY_CH, dtype=F32)).reshape(1, HY_CH)
    tm = ROW_TILE
    nq = HY_ORDER * 2
    full = lambda a: pl.BlockSpec(a.shape, lambda i: (0,) * a.ndim)
    args = (z, w1p, b1.reshape(1, -1), w2, b2.reshape(1, -1), w3, freq.reshape(1, -1), rates)
    return pl.pallas_call(
        _hy_filter_kernel,
        grid=(n // tm,),
        in_specs=[pl.BlockSpec((tm, LANES), lambda i: (i, 0))] + [full(a) for a in args[1:]],
        out_specs=pl.BlockSpec((nq, tm, HY_CH), lambda i: (0, i, 0)),
        out_shape=jax.ShapeDtypeStruct((nq, n, HY_CH), F32),
        compiler_params=_params("parallel"),
        name="hy_filters",
    )(*args)


def _dft_in_kernel(x_ref, g_ref, a_ref):
    for j in range(DFT_STEP):
        a_ref[j] = _dot(g_ref[j], x_ref[:, j, :].astype(BF16))


def _dft_in(x4, col, g_in):
    bx, rh, r, _ = x4.shape
    c = HY_CH
    return pl.pallas_call(
        _dft_in_kernel,
        grid=(bx, r // DFT_STEP),
        in_specs=[pl.BlockSpec((None, rh, DFT_STEP, c), lambda b, i: (b, 0, i, col)),
                  pl.BlockSpec((DFT_STEP, 2 * r, rh), lambda b, i: (i, 0, 0))],
        out_specs=pl.BlockSpec((None, DFT_STEP, 2 * r, c), lambda b, i: (b, i, 0, 0)),
        out_shape=jax.ShapeDtypeStruct((bx, r, 2 * r, c), F32),
        compiler_params=_params("parallel", "parallel"),
        name="dft_in",
    )(x4, g_in)


def _stage2(a_ref, b, j, f_ref):
    a = jnp.concatenate([a_ref[b, :, 0, j, :], a_ref[b, :, 1, j, :]], axis=0)
    return _dot(f_ref[...], a.astype(BF16))


def _dft_filt_kernel(a_ref, f_ref, k_ref):
    r = f_ref.shape[0] // 2
    for j in range(DFT_STEP):
        sf, sb = _stage2(a_ref, 0, j, f_ref), _stage2(a_ref, 1, j, f_ref)
        k_ref[j, 0:r, :] = sf[0:r] + sb[0:r]
        k_ref[j, r:2 * r, :] = sf[r:2 * r] - sb[r:2 * r]


def _dft_filt(a, f_fwd):
    nq, r, _, c = a.shape
    a5 = a.reshape(nq, r, 2, r, c)
    return pl.pallas_call(
        _dft_filt_kernel,
        grid=(nq // 2, r // DFT_STEP),
        in_specs=[pl.BlockSpec((2, r, 2, DFT_STEP, c), lambda o, i: (o, 0, 0, i, 0)),
                  pl.BlockSpec(f_fwd.shape, lambda o, i: (0, 0))],
        out_specs=pl.BlockSpec((None, DFT_STEP, 2 * r, c), lambda o, i: (o, i, 0, 0)),
        out_shape=jax.ShapeDtypeStruct((nq // 2, r, 2 * r, c), F32),
        compiler_params=_params("parallel", "parallel"),
        name="dft_filt",
    )(a5, f_fwd)


def _dft_mid_kernel(a_ref, k_ref, ff_ref, fi_ref, b_ref):
    r = ff_ref.shape[0] // 2
    for b in range(a_ref.shape[0]):
        for j in range(DFT_STEP):
            s = _stage2(a_ref, b, j, ff_ref)
            sr, si = s[0:r], s[r:2 * r]
            kr, ki = k_ref[j, 0:r, :], k_ref[j, r:2 * r, :]
            p = jnp.concatenate([sr * kr - si * ki, sr * ki + si * kr], axis=0).astype(BF16)
            b_ref[b, j] = _dot(fi_ref[...], p)


def _dft_mid(a, kspec, order, f_fwd, f_inv):
    bsz, r, _, c = a.shape
    a5 = a.reshape(bsz, r, 2, r, c)
    return pl.pallas_call(
        _dft_mid_kernel,
        grid=(r // DFT_STEP,),
        in_specs=[pl.BlockSpec((bsz, r, 2, DFT_STEP, c), lambda i: (0, 0, 0, i, 0)),
                  pl.BlockSpec((None, DFT_STEP, 2 * r, c), lambda i: (order, i, 0, 0)),
                  pl.BlockSpec(f_fwd.shape, lambda i: (0, 0)),
                  pl.BlockSpec(f_inv.shape, lambda i: (0, 0))],
        out_specs=pl.BlockSpec((bsz, DFT_STEP, 2 * r, c), lambda i: (0, i, 0, 0)),
        out_shape=jax.ShapeDtypeStruct((bsz, r, 2 * r, c), F32),
        compiler_params=_params("parallel"),
        name="dft_mid",
    )(a5, kspec, f_fwd, f_inv)


def _dft_out_kernel(b_ref, g_ref, u_ref, x_ref, bias_ref, o_ref):
    for j in range(DFT_STEP):
        rhs = jnp.concatenate([b_ref[:, 0, j, :], b_ref[:, 1, j, :]], axis=0).astype(BF16)
        y = _dot(g_ref[j], rhs)
        o_ref[:, j, :] = x_ref[:, j, :] * (y + u_ref[:, j, :] * bias_ref[...])


def _dft_out(bm, g_out, u4, u_col, x4, x_col, bias):
    bsz, r, _, c = bm.shape
    rh = r // 2
    b5 = bm.reshape(bsz, r, 2, r, c)
    seq = lambda col: pl.BlockSpec((None, rh, DFT_STEP, c), lambda b, i: (b, 0, i, col))
    return pl.pallas_call(
        _dft_out_kernel,
        grid=(bsz, r // DFT_STEP),
        in_specs=[pl.BlockSpec((None, r, 2, DFT_STEP, c), lambda b, i: (b, 0, 0, i, 0)),
                  pl.BlockSpec((DFT_STEP, rh, 2 * r), lambda b, i: (i, 0, 0)),
                  seq(u_col), seq(x_col), pl.BlockSpec((1, c), lambda b, i: (0, 0))],
        out_specs=seq(0),
        out_shape=jax.ShapeDtypeStruct((bsz, rh, r, c), F32),
        compiler_params=_params("parallel", "parallel"),
        name="dft_out",
    )(b5, g_out, u4, x4, bias.reshape(1, c))


def _hyena_filter_stage1(n, filter_params):
    r, g_in, _, _, _ = _dft_tables(n)
    filt = _hy_filters(n, *filter_params)
    return _dft_in(filt.reshape(-1, r // 2, r, HY_CH), 0, g_in)


def _hyena_filter_spectra(n, stage1):
    return _dft_filt(stage1, _dft_tables(n)[3])


def _hyena(hy_in, kspec, conv_bias):
    bsz, n, _ = hy_in.shape
    r, g_in, g_out, f_fwd, f_inv = _dft_tables(n)
    seq4 = hy_in.reshape(bsz, r // 2, r, 3 * HY_CH)
    zz = _dft_out(_dft_mid(_dft_in(seq4, 0, g_in), kspec, 0, f_fwd, f_inv), g_out,
                  seq4, 0, seq4, 1, conv_bias[0])
    out = _dft_out(_dft_mid(_dft_in(zz, 0, g_in), kspec, 1, f_fwd, f_inv), g_out,
                   zz, 0, seq4, 2, conv_bias[1])
    return out.reshape(bsz, n, HY_CH)


CD_HY, CD_Z, CD_XBC, CD_DT = 0, 1536, 2048, 3072
CD_PAD_COLS = CD_DT + 2 * LANES


def _reorder_ab(w):
    gq, gk, gv, gg, lr_f, lr_b, hq, hf_f, hf_b, hi, hg = jnp.split(
        w, np.cumsum([256, 256, 512, 512, 16, 16, 512, 512, 512, 512, 512])[:-1].tolist(), axis=-1)
    pad = jnp.zeros((w.shape[0], AB_PAD_COLS - AB_LR - 2 * GLA_LOW_RANK), w.dtype)
    return jnp.concatenate([gq, gk, gv, gg, hq, hf_f, hf_b, hi, hg, lr_f, lr_b, pad], axis=-1)


def _reorder_cd(w):
    hy, z, xbc, dt_f, dt_b = jnp.split(w, np.cumsum([1536, 512, 1024, 8, 8])[:-1].tolist(), axis=-1)
    pad = jnp.zeros((w.shape[0], LANES - MB_HEADS), w.dtype)
    return jnp.concatenate([hy, z, xbc, dt_f, pad, dt_b, pad], axis=-1)


def kernel(x, c, ctx, c_ctx, ada_w, ada_b, norm_mix_g, norm_ffn_g, norm_out_g, ab_w_in, ab_w_out, gla_gate_w, gla_gate_b, gla_norm_g, hg_lb, hg_norm_g, cd_w_in, cd_w_out, hy_short_w, hy_short_b, hy_w1, hy_b1, hy_w2, hy_b2, hy_w3, hy_freq, hy_bias, mb_conv_w, mb_conv_b, mb_dt_bias, mb_a_log, mb_d, mb_norm_g, router_w, router_b, moe_w_gate, moe_w_up, moe_w_down):
    bsz, n_lat, d = x.shape
    n_ctx = ctx.shape[1]
    t = n_ctx + n_lat
    assert ada_w.shape[0] == 2 and ab_w_in.shape[0] == 1 and cd_w_in.shape[0] == 1

    cond = jnp.zeros((8, d), F32).at[:bsz].set(c).at[bsz].set(c_ctx)
    m = _adaln(cond, ada_w, ada_b)

    def mods(layer):
        lat = m[layer, :bsz].reshape(bsz, 6, d)
        cx = jnp.broadcast_to(m[layer, bsz].reshape(1, 6, d), (bsz, 6, d))
        both = jnp.stack([cx, lat], axis=1)
        return [both[:, :, j][:, :, None, :] for j in range(6)]

    lb_all = jnp.cumsum(jax.nn.softmax(hg_lb.astype(F32), axis=1), axis=1)
    router_w_pad = jnp.zeros((d, LANES), F32).at[:, :N_EXPERTS].set(router_w)
    h = jnp.concatenate([ctx, x], axis=1)

    sh_m, sc_m, gt_m, sh_f, sc_f, gt_f = mods(0)
    proj = _norm_proj(h, norm_mix_g[0], sh_m, sc_m, _reorder_ab(ab_w_in[0]).astype(BF16), n_ctx)
    gwp = [jnp.zeros((LANES, GLA_KEY_W), F32).at[GLA_LOW_RANK * dd:GLA_LOW_RANK * (dd + 1)].set(gla_gate_w[0, dd])
           for dd in range(2)]
    o_gla = _gla_scan(proj, gwp, [gla_gate_b[0, dd].reshape(1, -1) for dd in range(2)], n_ctx)
    o_hg = _hgrn_scan(proj, [lb_all[dd, 0].reshape(1, -1) for dd in range(2)], n_ctx)
    h = _mix_out_ab(h, o_gla, o_hg, proj, gla_norm_g[0], hg_norm_g[0], ab_w_out[0].astype(BF16), gt_m, n_ctx)
    filter_params = (hy_w1[0], hy_b1[0], hy_w2[0], hy_b2[0], hy_w3[0], hy_freq[0])
    side = ((functools.partial(_hyena_filter_stage1, n_lat), filter_params),
            functools.partial(_hyena_filter_spectra, n_lat))
    h, kspec = _moe(h, norm_ffn_g[0], sh_f, sc_f, gt_f, router_w_pad, router_b,
                    0, moe_w_gate, moe_w_up, moe_w_down, n_ctx, side=side)

    sh_m, sc_m, gt_m, sh_f, sc_f, gt_f = mods(1)
    proj = _norm_proj(h, norm_mix_g[1], sh_m, sc_m, _reorder_cd(cd_w_in[0]).astype(BF16), n_ctx)
    hy_in = _dwconv(proj, CD_HY, 3 * HY_CH, hy_short_w[0], hy_short_b[0], (n_ctx, t), act=False)
    hy = _hyena(hy_in, kspec, hy_bias[0])
    xbc = _dwconv(proj, CD_XBC, MB_INNER + 2 * MB_BC_W, mb_conv_w[0], mb_conv_b[0], (0, n_ctx, t), act=True)
    y_ssd = [_ssd_scan(xbc, proj, CD_DT // LANES + dd, mb_dt_bias[0, dd], mb_a_log[0, dd], n_ctx, dd == 1)
             for dd in range(2)]
    d_skip_x = jnp.repeat(mb_d[0], MB_HEAD_DIM).reshape(1, MB_INNER)
    h = _mix_out_cd(h, hy, y_ssd, xbc, proj, CD_Z // MB_INNER, d_skip_x, mb_norm_g[0],
                    cd_w_out[0].astype(BF16), gt_m, n_ctx)
    return _moe(h, norm_ffn_g[1], sh_f, sc_f, gt_f, router_w_pad, router_b,
                1, moe_w_gate, moe_w_up, moe_w_down, 0, final_g=norm_out_g)
```

```python
import functools
import math

import numpy as np
import jax
import jax.numpy as jnp
from jax import lax
from jax.experimental import pallas as pl
from jax.experimental.pallas import tpu as pltpu
from jax.experimental.pallas import tpu_sc as plsc

NORM_EPS = 1e-6
GLA_HEADS, GLA_DK, GLA_DV, GLA_LOW_RANK, GLA_TAU = 4, 64, 128, 16, 16.0
GLA_KEY_W, GLA_VAL_W = GLA_HEADS * GLA_DK, GLA_HEADS * GLA_DV
HG_HEADS, HG_EXPAND, HG_DV = 4, 128, 128
HG_KEY_W, HG_VAL_W = HG_HEADS * HG_EXPAND, HG_HEADS * HG_DV
HY_CH, HY_ORDER, HY_SHORT, HY_BANDS, HY_FILT_HID = 512, 2, 3, 16, 64
HY_MIN_DECAY = math.log(1e-2) / 1.5
HY_MAX_DECAY = math.log(1e-2) / 0.3
MB_HEADS, MB_HEAD_DIM, MB_GROUPS, MB_STATE = 8, 64, 2, 128
MB_INNER = MB_HEADS * MB_HEAD_DIM
MB_BC_W = MB_GROUPS * MB_STATE
N_EXPERTS, N_GROUPS, TOP_K, MOE_BLOCK = 16, 4, 2, 256
EXPERTS_PER_GROUP = N_EXPERTS // N_GROUPS

LANES = 128
SCAN_CHUNK = 64
SCAN_BLOCK = 128
SSD_CHUNK = 128
ROW_TILE = 256
VMEM_LIMIT = 56 * 1024 * 1024

BF16 = jnp.bfloat16
F32 = jnp.float32


def _params(*sem):
    return pltpu.CompilerParams(dimension_semantics=sem, vmem_limit_bytes=VMEM_LIMIT)


def _split3(x):
    hi = x.astype(BF16)
    r1 = x - hi.astype(F32)
    mid = r1.astype(BF16)
    lo = (r1 - mid.astype(F32)).astype(BF16)
    return hi, mid, lo


def _dot(a, b):
    return jnp.dot(a, b, preferred_element_type=F32)


def _dot_nt(a, b):
    return lax.dot_general(a, b, (((1,), (1,)), ((), ())), preferred_element_type=F32)


def _dot_tn(a, b):
    return lax.dot_general(a, b, (((0,), (0,)), ((), ())), preferred_element_type=F32)


def _sel_dot(m01, x):
    hi, mid, lo = _split3(x)
    return _dot(m01, hi) + (_dot(m01, mid) + _dot(m01, lo))


def _dot_sel(x, m01):
    hi, mid, lo = _split3(x)
    return _dot(hi, m01) + (_dot(mid, m01) + _dot(lo, m01))


def _dot_f32(a, b):
    ah = a.astype(BF16)
    al = (a - ah.astype(F32)).astype(BF16)
    bh = b.astype(BF16)
    bl = (b - bh.astype(F32)).astype(BF16)
    return _dot(ah, bh) + (_dot(ah, bl) + _dot(al, bh))


def _silu(x):
    return x * (1.0 / (1.0 + jnp.exp(-x)))


def _sigmoid(x):
    return 1.0 / (1.0 + jnp.exp(-x))


def _softplus(x):
    return jnp.maximum(x, 0.0) + jnp.log(1.0 + jnp.exp(-jnp.abs(x)))


def _pack_bf16_pairs(x):
    bits = lax.bitcast_convert_type(x.astype(BF16).astype(F32), jnp.uint32)
    half = x.shape[1] // 2
    return bits[:, :half] | (bits[:, half:] >> 16)


def _unpack_bf16_pairs(p):
    hi = lax.bitcast_convert_type(p & jnp.uint32(0xFFFF0000), F32)
    lo = lax.bitcast_convert_type(p << 16, F32)
    return jnp.concatenate([hi, lo], axis=1)


def _rms(x, g):
    return x * lax.rsqrt(jnp.mean(x * x, axis=-1, keepdims=True) + NORM_EPS) * g


def _adaln_kernel(c_ref, w_ref, b_ref, o_ref):
    o_ref[...] = _dot_f32(_silu(c_ref[...]), w_ref[...]) + b_ref[...]


def _adaln(cond, w, b):
    n_l, d, n6 = w.shape
    tn = 1536
    return pl.pallas_call(
        _adaln_kernel,
        grid=(n_l, n6 // tn),
        in_specs=[pl.BlockSpec((8, d), lambda l, j: (0, 0)),
                  pl.BlockSpec((None, d, tn), lambda l, j: (l, 0, j)),
                  pl.BlockSpec((None, 1, tn), lambda l, j: (l, 0, j))],
        out_specs=pl.BlockSpec((None, 8, tn), lambda l, j: (l, 0, j)),
        out_shape=jax.ShapeDtypeStruct((n_l, 8, n6), F32),
        compiler_params=_params("parallel", "parallel"),
        name="adaln",
    )(cond, w, b.reshape(n_l, 1, n6))


def _norm_proj_kernel(h_ref, g_ref, sh_ref, sc_ref, w_ref, o_ref):
    u = _rms(h_ref[...], g_ref[...]) * (1.0 + sc_ref[...]) + sh_ref[...]
    o_ref[...] = _dot(u.astype(BF16), w_ref[...])


def _norm_proj(h, g, shift, scale, w, n_ctx):
    bsz, t, d = h.shape
    n = w.shape[1]
    tm = ROW_TILE
    seg = lambda b, i: (b, (i * tm >= n_ctx).astype(jnp.int32), 0, 0)
    return pl.pallas_call(
        _norm_proj_kernel,
        grid=(bsz, t // tm),
        in_specs=[pl.BlockSpec((None, tm, d), lambda b, i: (b, i, 0)),
                  pl.BlockSpec((1, d), lambda b, i: (0, 0)),
                  pl.BlockSpec((None, None, 1, d), seg),
                  pl.BlockSpec((None, None, 1, d), seg),
                  pl.BlockSpec((d, n), lambda b, i: (0, 0))],
        out_specs=pl.BlockSpec((None, tm, n), lambda b, i: (b, i, 0)),
        out_shape=jax.ShapeDtypeStruct((bsz, t, n), F32),
        compiler_params=_params("parallel", "parallel"),
        name="norm_proj",
    )(h, g.reshape(1, d), shift, scale, w)


def _scan_constants(c, reverse):
    t = np.arange(c)[:, None]
    u = np.arange(c)[None, :]
    sels = [u <= t, u > t]
    masks = []
    m = c // 2
    while m >= 1:
        blk = t // (2 * m)
        upper_t = (t % (2 * m)) >= m
        r = blk * (2 * m) + m - 1
        s_blk = u // (2 * m)
        upper_s = (u % (2 * m)) >= m
        sels.append((upper_t & (u > r) & (u <= t)) | ((~upper_t) & (u > t) & (u <= r)))
        masks.append((blk == s_blk) & upper_t & (~upper_s))
        m //= 2
    masks.append(t == u)
    sel = np.stack(sels).astype(np.float32)
    msk = np.stack(masks).astype(np.float32)
    if reverse:
        sel = sel[:, ::-1, ::-1]
        msk = msk[:, ::-1, ::-1]
    return np.ascontiguousarray(sel.reshape(-1, c)), np.ascontiguousarray(msk)


def _chunk_order(i, n_ctx_chunks, n_chunks, reverse):
    if not reverse:
        return i
    return jnp.where(i < n_ctx_chunks, n_ctx_chunks - 1 - i, n_chunks - 1 - (i - n_ctx_chunks))


GROUP_KEYS = 256


def _decay_chunk(q, k, v, la, sel_ref, mask_ref, hm_ref, st_ref, heads, dk, dv):
    c = q.shape[0]
    n_lvl = mask_ref.shape[0] - 1
    hpg = GROUP_KEYS // dk
    cs = _dot(sel_ref[...], jnp.concatenate(_split3(la), axis=0))
    e_q = jnp.exp(cs[0:c])
    e_k = jnp.exp(cs[c:2 * c])
    e_tot = jnp.exp(jnp.sum(la, axis=0, keepdims=True))
    vb = v.astype(BF16)
    outs = []
    for g in range(heads // hpg):
        ks = slice(g * GROUP_KEYS, (g + 1) * GROUP_KEYS)
        vs = slice(g * hpg * dv, (g + 1) * hpg * dv)
        qg, kg = q[:, ks], k[:, ks]
        stack = lambda x: jnp.concatenate([x * hm_ref[h] for h in range(hpg)], axis=0).astype(BF16)
        att = mask_ref[n_lvl] * _dot_nt(stack(qg), kg.astype(BF16))
        for l in range(n_lvl):
            e = jnp.exp(cs[(2 + l) * c:(3 + l) * c, ks])
            att = att + mask_ref[l] * _dot_nt(stack(qg * e), (kg * e).astype(BF16))
        att = att.astype(BF16)
        st = st_ref[g]
        inter = _dot_nt(stack(qg * e_q[:, ks]), st.astype(BF16))
        upd = _dot_tn(vb[:, vs], (kg * e_k[:, ks]).astype(BF16))
        new = st * e_tot[:, ks]
        for h in range(hpg):
            rows = slice(h * c, (h + 1) * c)
            hv = slice(g * hpg * dv + h * dv, g * hpg * dv + (h + 1) * dv)
            outs.append(_dot(att[rows], vb[:, hv]) + inter[rows])
            new = new + upd[h * dv:(h + 1) * dv] * hm_ref[h]
        st_ref[g] = new
    return jnp.concatenate(outs, axis=-1)


def _log_sigmoid(x):
    return jnp.minimum(x, 0.0) - jnp.log(1.0 + jnp.exp(-jnp.abs(x)))


def _gla_kernel(*refs):
    ins, hm_ref, (o_refs, st_refs) = (refs[0:8], refs[8:16]), refs[16], (refs[17:19], refs[19:21])

    @pl.when(pl.program_id(1) == 0)
    def _():
        for st_ref in st_refs:
            st_ref[...] = jnp.zeros_like(st_ref)

    for d, ((q_ref, k_ref, v_ref, lr_ref, gw_ref, gb_ref, sel_ref, mask_ref), o_ref, st_ref) in enumerate(
            zip(ins, o_refs, st_refs)):
        z = _dot_f32(lr_ref[...], gw_ref[...]) + gb_ref[...]
        la = _log_sigmoid(z) * (1.0 / GLA_TAU)
        q = q_ref[...] * (GLA_DK ** -0.5)
        k, v = k_ref[...], v_ref[...]
        for rows in _sub_chunks(q.shape[0], d == 1):
            o_ref[rows, :] = _decay_chunk(q[rows], k[rows], v[rows], la[rows], sel_ref, mask_ref, hm_ref,
                                          st_ref, GLA_HEADS, GLA_DK, GLA_DV)


def _hgrn_kernel(*refs):
    ins, hm_ref, (o_refs, st_refs) = (refs[0:6], refs[6:12]), refs[12], (refs[13:15], refs[15:17])

    @pl.when(pl.program_id(1) == 0)
    def _():
        for st_ref in st_refs:
            st_ref[...] = jnp.zeros_like(st_ref)

    for d, ((q_ref, f_ref, v_ref, lb_ref, sel_ref, mask_ref), o_ref, st_ref) in enumerate(
            zip(ins, o_refs, st_refs)):
        lb = lb_ref[...]
        f = lb + (1.0 - lb) * _sigmoid(f_ref[...])
        q, k, v, la = _silu(q_ref[...]), 1.0 - f, v_ref[...], jnp.log(f)
        for rows in _sub_chunks(q.shape[0], d == 1):
            o_ref[rows, :] = _decay_chunk(q[rows], k[rows], v[rows], la[rows], sel_ref, mask_ref, hm_ref,
                                          st_ref, HG_HEADS, HG_EXPAND, HG_DV)


def _sub_chunks(rows, reverse):
    order = range(rows // SCAN_CHUNK)
    return [slice(j * SCAN_CHUNK, (j + 1) * SCAN_CHUNK) for j in (reversed(order) if reverse else order)]


def _scan_specs(blk, n_ctx, t, reverse, chunk=None, stacked_heads=1):
    n_blocks = t // blk
    order = functools.partial(_chunk_order, n_ctx_chunks=n_ctx // blk, n_chunks=n_blocks, reverse=reverse)

    def col(width, idx):
        return pl.BlockSpec((None, blk, width), lambda b, i: (b, order(i), idx))

    sel, msk = _scan_constants(chunk or blk, reverse)
    sel3 = np.concatenate([sel, sel, sel], axis=1)
    msk = np.tile(msk, (1, stacked_heads, 1))
    const = lambda a: pl.BlockSpec(a.shape, lambda b, i: (0,) * a.ndim)
    return n_blocks, col, const, jnp.asarray(sel3, BF16), jnp.asarray(msk, F32)


def _head_masks(dk):
    hpg = GROUP_KEYS // dk
    hm = np.zeros((hpg, 1, GROUP_KEYS), np.float32)
    for h in range(hpg):
        hm[h, 0, h * dk:(h + 1) * dk] = 1.0
    return jnp.asarray(hm)


AB_Q, AB_K, AB_V, AB_G = 0, 256, 512, 1024
AB_HQ, AB_HF, AB_HI, AB_HG, AB_LR = 1536, 2048, 3072, 3584, 4096
AB_PAD_COLS = 4224


def _gla_scan(proj, gate_w_pad, gate_b, n_ctx):
    bsz, t, _ = proj.shape
    hpg = GROUP_KEYS // GLA_DK
    in_specs, args, outs = [], [], []
    for d in range(2):
        n_blocks, col, const, sel, msk = _scan_specs(SCAN_BLOCK, n_ctx, t, d == 1, SCAN_CHUNK, hpg)
        in_specs += [col(GLA_KEY_W, AB_Q // GLA_KEY_W), col(GLA_KEY_W, AB_K // GLA_KEY_W),
                     col(GLA_VAL_W, AB_V // GLA_VAL_W), col(LANES, AB_LR // LANES),
                     const(gate_w_pad[d]), const(gate_b[d]), const(sel), const(msk)]
        args += [proj, proj, proj, proj, gate_w_pad[d], gate_b[d], sel, msk]
        outs.append(col(GLA_VAL_W, 0))
    hm = _head_masks(GLA_DK)
    return pl.pallas_call(
        _gla_kernel,
        grid=(bsz, n_blocks),
        in_specs=in_specs + [const(hm)],
        out_specs=outs,
        out_shape=[jax.ShapeDtypeStruct((bsz, t, GLA_VAL_W), F32)] * 2,
        scratch_shapes=[pltpu.VMEM((GLA_HEADS // hpg, GLA_DV, GROUP_KEYS), F32)] * 2,
        compiler_params=_params("parallel", "arbitrary"),
        name="gla_scan",
    )(*args, hm)


def _hgrn_scan(proj, lb, n_ctx):
    bsz, t, _ = proj.shape
    hpg = GROUP_KEYS // HG_EXPAND
    in_specs, args, outs = [], [], []
    for d in range(2):
        n_blocks, col, const, sel, msk = _scan_specs(SCAN_BLOCK, n_ctx, t, d == 1, SCAN_CHUNK, hpg)
        in_specs += [col(HG_KEY_W, AB_HQ // HG_KEY_W), col(HG_KEY_W, AB_HF // HG_KEY_W + d),
                     col(HG_VAL_W, AB_HI // HG_VAL_W), const(lb[d]), const(sel), const(msk)]
        args += [proj, proj, proj, lb[d], sel, msk]
        outs.append(col(HG_VAL_W, 0))
    hm = _head_masks(HG_EXPAND)
    return pl.pallas_call(
        _hgrn_kernel,
        grid=(bsz, n_blocks),
        in_specs=in_specs + [const(hm)],
        out_specs=outs,
        out_shape=[jax.ShapeDtypeStruct((bsz, t, HG_VAL_W), F32)] * 2,
        scratch_shapes=[pltpu.VMEM((HG_HEADS // hpg, HG_DV, GROUP_KEYS), F32)] * 2,
        compiler_params=_params("parallel", "arbitrary"),
        name="hgrn_scan",
    )(*args, hm)


def _mix_out_ab_kernel(h_ref, gf_ref, gb_ref, hf_ref, hb_ref, gg_ref, hg_ref, gn_ref, hn_ref,
                       w_ref, gt_ref, o_ref):
    feats = []
    for o, gate, g in ((gf_ref[...] + gb_ref[...], gg_ref[...], gn_ref[...]),
                       (hf_ref[...] + hb_ref[...], hg_ref[...], hn_ref[...])):
        for hd in range(o.shape[-1] // LANES):
            s = slice(hd * LANES, (hd + 1) * LANES)
            feats.append(_rms(o[:, s], g) * _silu(gate[:, s]))
    feat = jnp.concatenate(feats, axis=-1).astype(BF16)
    o_ref[...] = h_ref[...] + gt_ref[...] * _dot(feat, w_ref[...])


def _mix_out_ab(h, o_gla, o_hg, proj, gla_norm_g, hg_norm_g, w_out, gate, n_ctx):
    bsz, t, d = h.shape
    tm = ROW_TILE
    seg = lambda b, i: (b, (i * tm >= n_ctx).astype(jnp.int32), 0, 0)
    row = lambda width, idx: pl.BlockSpec((None, tm, width), lambda b, i: (b, i, idx))
    vec = pl.BlockSpec((1, LANES), lambda b, i: (0, 0))
    return pl.pallas_call(
        _mix_out_ab_kernel,
        grid=(bsz, t // tm),
        in_specs=[row(d, 0), row(GLA_VAL_W, 0), row(GLA_VAL_W, 0), row(HG_VAL_W, 0), row(HG_VAL_W, 0),
                  row(GLA_VAL_W, AB_G // GLA_VAL_W), row(HG_VAL_W, AB_HG // HG_VAL_W), vec, vec,
                  pl.BlockSpec(w_out.shape, lambda b, i: (0, 0)),
                  pl.BlockSpec((None, None, 1, d), seg)],
        out_specs=row(d, 0),
        out_shape=jax.ShapeDtypeStruct((bsz, t, d), F32),
        compiler_params=_params("parallel", "parallel"),
        name="mix_out_ab",
    )(h, o_gla[0], o_gla[1], o_hg[0], o_hg[1], proj, proj, gla_norm_g.reshape(1, -1),
      hg_norm_g.reshape(1, -1), w_out, gate)


def _dwconv_kernel(x_ref, prev_ref, next_ref, w_ref, b_ref, o_ref, *, seg_tiles, act):
    i = pl.program_id(1)
    tm = x_ref.shape[0]
    x = x_ref[...]
    first = functools.reduce(jnp.logical_or, [i == s for s in seg_tiles[:-1]])
    last = functools.reduce(jnp.logical_or, [i == s - 1 for s in seg_tiles[1:]])
    prev_row = jnp.where(first, 0.0, prev_ref[7:8, :])
    next_row = jnp.where(last, 0.0, next_ref[0:1, :])
    rows = lax.broadcasted_iota(jnp.int32, x.shape, 0)
    x_prev = jnp.where(rows == 0, prev_row, pltpu.roll(x, 1, axis=0))
    x_next = jnp.where(rows == tm - 1, next_row, pltpu.roll(x, tm - 1, axis=0))
    y = x_prev * w_ref[0:1, :] + x * w_ref[1:2, :] + x_next * w_ref[2:3, :] + b_ref[...]
    o_ref[...] = _silu(y) if act else y


def _dwconv(proj, col0, width, w, b, seg_bounds, act):
    bsz, t, _ = proj.shape
    tm = ROW_TILE
    off = seg_bounds[0] // tm
    n_tiles = t // tm - off
    seg_tiles = tuple(s // tm - off for s in seg_bounds)
    cb = col0 // width
    r8 = tm // 8
    last8 = t // 8 - 1
    kern = functools.partial(_dwconv_kernel, seg_tiles=seg_tiles, act=act)
    return pl.pallas_call(
        kern,
        grid=(bsz, n_tiles),
        in_specs=[pl.BlockSpec((None, tm, width), lambda bb, i: (bb, i + off, cb)),
                  pl.BlockSpec((None, 8, width), lambda bb, i: (bb, jnp.maximum((i + off) * r8 - 1, 0), cb)),
                  pl.BlockSpec((None, 8, width),
                               lambda bb, i: (bb, jnp.minimum((i + off + 1) * r8, last8), cb)),
                  pl.BlockSpec((3, width), lambda bb, i: (0, 0)),
                  pl.BlockSpec((1, width), lambda bb, i: (0, 0))],
        out_specs=pl.BlockSpec((None, tm, width), lambda bb, i: (bb, i, 0)),
        out_shape=jax.ShapeDtypeStruct((bsz, n_tiles * tm, width), F32),
        compiler_params=_params("parallel", "parallel"),
        name="dwconv",
    )(proj, proj, proj, w.T, b.reshape(1, -1))


def _ssd_kernel(xbc_ref, dt_ref, bias_ref, alog_ref, hexp_ref, mq_ref, mk_ref, mask_ref, o_ref, st_ref):
    @pl.when(pl.program_id(1) == 0)
    def _():
        st_ref[...] = jnp.zeros_like(st_ref)

    c = xbc_ref.shape[0]
    hpg = MB_HEADS // MB_GROUPS
    gw = hpg * MB_HEAD_DIM
    dt = _softplus(dt_ref[...] + bias_ref[...])
    la = -dt * jnp.exp(alog_ref[...])
    cq = _sel_dot(mq_ref[...], la)
    ck = _sel_dot(mk_ref[...], la)
    cq_t = _dot_nt_sel(la, mq_ref[...])
    hexp = hexp_ref[...]
    dt_x = _dot_sel(dt, hexp)
    eq_x = jnp.exp(_dot_sel(cq, hexp))
    ek_x = jnp.exp(_dot_sel(ck, hexp))
    etot_x = jnp.exp(_dot_sel(jnp.sum(la, axis=0, keepdims=True), hexp))
    xs = xbc_ref[:, 0:MB_INNER] * dt_x
    mask = mask_ref[...]
    outs = []
    for g in range(MB_GROUPS):
        bm = xbc_ref[:, MB_INNER + g * MB_STATE:MB_INNER + (g + 1) * MB_STATE].astype(BF16)
        cm = xbc_ref[:, MB_INNER + MB_BC_W + g * MB_STATE:MB_INNER + MB_BC_W + (g + 1) * MB_STATE].astype(BF16)
        cb = _dot_nt(cm, bm)
        st = st_ref[g]
        gs = slice(g * gw, (g + 1) * gw)
        y_inter = _dot(cm, st.astype(BF16)) * eq_x[:, gs]
        for r in range(hpg):
            hd = g * hpg + r
            diff = cq[:, hd:hd + 1] - cq_t[hd:hd + 1, :]
            w = cb * jnp.exp(jnp.where(mask > 0.0, diff, -jnp.inf))
            ps = slice(hd * MB_HEAD_DIM, (hd + 1) * MB_HEAD_DIM)
            outs.append(_dot(w.astype(BF16), xs[:, ps].astype(BF16))
                        + y_inter[:, r * MB_HEAD_DIM:(r + 1) * MB_HEAD_DIM])
        st_ref[g] = st * etot_x[:, gs] + _dot_tn(bm, (xs[:, gs] * ek_x[:, gs]).astype(BF16))
    o_ref[...] = jnp.concatenate(outs, axis=-1)


def _dot_nt_sel(x, m01):
    hi, mid, lo = _split3(x)
    f = lambda p: lax.dot_general(p, m01, (((0,), (1,)), ((), ())), preferred_element_type=F32)
    return f(hi) + (f(mid) + f(lo))


def _ssd_scan(xbc, proj, dt_col, dt_bias, a_log, n_ctx, reverse):
    bsz, t, _ = xbc.shape
    c = SSD_CHUNK
    n_chunks, col, const, sel, msk = _scan_specs(c, n_ctx, t, reverse)
    mq, mk = sel[0:c, 0:c], sel[c:2 * c, 0:c]
    tri = np.tril(np.ones((c, c), np.float32))
    mask = jnp.asarray(tri[::-1, ::-1].copy() if reverse else tri)
    pad = lambda v: jnp.zeros((1, LANES), F32).at[0, :MB_HEADS].set(v)
    hexp = np.zeros((LANES, MB_INNER), np.float32)
    for hd in range(MB_HEADS):
        hexp[hd, hd * MB_HEAD_DIM:(hd + 1) * MB_HEAD_DIM] = 1.0
    hexp = jnp.asarray(hexp, BF16)
    bias = pad(dt_bias)
    alog = pad(a_log.astype(F32))
    return pl.pallas_call(
        _ssd_kernel,
        grid=(bsz, n_chunks),
        in_specs=[col(xbc.shape[-1], 0), col(LANES, dt_col), const(bias), const(alog), const(hexp),
                  const(mq), const(mk), const(mask)],
        out_specs=col(MB_INNER, 0),
        out_shape=jax.ShapeDtypeStruct((bsz, t, MB_INNER), F32),
        scratch_shapes=[pltpu.VMEM((MB_GROUPS, MB_STATE, MB_INNER // MB_GROUPS), F32)],
        compiler_params=_params("parallel", "arbitrary"),
        name="ssd_scan_rev" if reverse else "ssd_scan_fwd",
    )(xbc, proj, bias, alog, hexp, mq, mk, mask)


def _mix_out_cd_kernel(h_ref, hy_ref, yf_ref, yb_ref, xs_ref, z_ref, dsk_ref, ng_ref, w_ref, gt_ref, o_ref):
    y = (yf_ref[...] + yb_ref[...] + dsk_ref[...] * xs_ref[...]) * _silu(z_ref[...])
    gw = MB_INNER // MB_GROUPS
    ys = [_rms(y[:, g * gw:(g + 1) * gw], ng_ref[:, g * gw:(g + 1) * gw]) for g in range(MB_GROUPS)]
    feat = jnp.concatenate([hy_ref[...]] + ys, axis=-1).astype(BF16)
    o_ref[...] = h_ref[...] + gt_ref[...] * _dot(feat, w_ref[...])


def _mix_out_cd(h, hy, y_ssd, xbc, proj, z_col, d_skip_x, norm_g, w_out, gate, n_ctx):
    bsz, t, d = h.shape
    tm = ROW_TILE
    n_lat = t - n_ctx
    off = n_ctx // tm
    row = lambda width, idx: pl.BlockSpec((None, tm, width), lambda b, i: (b, i + off, idx))
    vec = pl.BlockSpec((1, MB_INNER), lambda b, i: (0, 0))
    return pl.pallas_call(
        _mix_out_cd_kernel,
        grid=(bsz, n_lat // tm),
        in_specs=[row(d, 0), pl.BlockSpec((None, tm, HY_CH), lambda b, i: (b, i, 0)),
                  row(MB_INNER, 0), row(MB_INNER, 0), row(MB_INNER, 0), row(MB_INNER, z_col), vec, vec,
                  pl.BlockSpec(w_out.shape, lambda b, i: (0, 0)),
                  pl.BlockSpec((None, None, 1, d), lambda b, i: (b, 1, 0, 0))],
        out_specs=pl.BlockSpec((None, tm, d), lambda b, i: (b, i, 0)),
        out_shape=jax.ShapeDtypeStruct((bsz, n_lat, d), F32),
        compiler_params=_params("parallel", "parallel"),
        name="mix_out_cd",
    )(h, hy, y_ssd[0], y_ssd[1], xbc, proj, d_skip_x, norm_g.reshape(1, -1), w_out, gate)


def _top2_of4(a, b, c, d):
    hi1, lo1, hi2, lo2 = jnp.maximum(a, b), jnp.minimum(a, b), jnp.maximum(c, d), jnp.minimum(c, d)
    return jnp.maximum(hi1, hi2) + jnp.maximum(jnp.minimum(hi1, hi2), jnp.maximum(lo1, lo2))


def _first_argmax(vals, skip=None):
    idx = None
    for j, vj in enumerate(vals):
        if idx is None and skip is None:
            idx, best = jnp.zeros(vj.shape, jnp.int32), vj
            continue
        if idx is None:
            idx, best = jnp.full(vj.shape, -1, jnp.int32), jnp.full(vj.shape, -jnp.inf, F32)
        take = vj > best
        if skip is not None:
            take = jnp.logical_and(take, skip != j)
        idx = jnp.where(take, j, idx)
        best = jnp.where(take, vj, best)
    return idx, best


def _ffn_pre_kernel(h_ref, g_ref, sh_ref, sc_ref, rw_ref, rb_ref, tri_ref,
                    v_ref, ri_ref, rwt_ref, cnt_ref, carry_ref):
    @pl.when(jnp.logical_and(pl.program_id(0) == 0, pl.program_id(1) == 0))
    def _():
        carry_ref[...] = jnp.zeros_like(carry_ref)

    v = _rms(h_ref[...], g_ref[...]) * (1.0 + sc_ref[...]) + sh_ref[...]
    v_ref[...] = _pack_bf16_pairs(v)
    st = _sigmoid(_dot_f32(v, rw_ref[...])).T[0:N_EXPERTS]
    sel = st + rb_ref[...]
    row = lambda a, e: a[e:e + 1]
    epg = EXPERTS_PER_GROUP
    gscore = [_top2_of4(*[row(sel, g * epg + j) for j in range(epg)]) for g in range(N_GROUPS)]
    best, _ = _first_argmax(gscore)

    def in_best(a, j):
        out = row(a, j)
        for g in range(1, N_GROUPS):
            out = jnp.where(best == g, row(a, g * epg + j), out)
        return out

    vals = [in_best(sel, j) for j in range(epg)]
    raw = [in_best(st, j) for j in range(epg)]
    i1, _ = _first_argmax(vals)
    i2, _ = _first_argmax(vals, skip=i1)
    pick = lambda i: functools.reduce(lambda acc, j: jnp.where(i == j, raw[j], acc), range(1, epg), raw[0])
    w1, w2 = pick(i1), pick(i2)
    wsum = w1 + w2
    e1, e2 = best * epg + i1, best * epg + i2

    experts = lax.broadcasted_iota(jnp.int32, st.shape, 0)
    oh1 = (experts == e1).astype(F32)
    oh2 = (experts == e2).astype(F32)
    cnt = oh1 + oh2
    before = _dot(cnt.astype(BF16), tri_ref[...]) + carry_ref[:, 0:1]
    ri_ref[0:1, :] = e1
    ri_ref[1:2, :] = e2
    ri_ref[2:3, :] = jnp.sum(oh1 * before, axis=0, keepdims=True).astype(jnp.int32)
    ri_ref[3:4, :] = jnp.sum(oh2 * before, axis=0, keepdims=True).astype(jnp.int32)
    ri_ref[4:8, :] = jnp.zeros((4, st.shape[1]), jnp.int32)
    rwt_ref[0:1, :] = w1 / wsum
    rwt_ref[1:2, :] = w2 / wsum
    rwt_ref[2:8, :] = jnp.zeros((6, st.shape[1]), F32)
    carry_ref[...] = carry_ref[...] + jnp.sum(cnt, axis=1, keepdims=True)
    cnt_ref[...] = carry_ref[...]


def _ffn_pre(h, g, shift, scale, router_w_pad, router_b, n_ctx):
    bsz, t, d = h.shape
    tm = ROW_TILE
    seg = lambda b, i: (b, (i * tm >= n_ctx).astype(jnp.int32), 0, 0)
    tri = jnp.asarray(np.triu(np.ones((tm, tm), np.float32), 1), BF16)
    return pl.pallas_call(
        _ffn_pre_kernel,
        grid=(bsz, t // tm),
        in_specs=[pl.BlockSpec((None, tm, d), lambda b, i: (b, i, 0)),
                  pl.BlockSpec((1, d), lambda b, i: (0, 0)),
                  pl.BlockSpec((None, None, 1, d), seg),
                  pl.BlockSpec((None, None, 1, d), seg),
                  pl.BlockSpec((d, LANES), lambda b, i: (0, 0)),
                  pl.BlockSpec((N_EXPERTS, 1), lambda b, i: (0, 0)),
                  pl.BlockSpec((tm, tm), lambda b, i: (0, 0))],
        out_specs=[pl.BlockSpec((None, tm, d // 2), lambda b, i: (b, i, 0)),
                   pl.BlockSpec((None, 8, tm), lambda b, i: (b, 0, i)),
                   pl.BlockSpec((None, 8, tm), lambda b, i: (b, 0, i)),
                   pl.BlockSpec((N_EXPERTS, LANES), lambda b, i: (0, 0))],
        out_shape=[jax.ShapeDtypeStruct((bsz, t, d // 2), jnp.uint32),
                   jax.ShapeDtypeStruct((bsz, 8, t), jnp.int32),
                   jax.ShapeDtypeStruct((bsz, 8, t), F32),
                   jax.ShapeDtypeStruct((N_EXPERTS, LANES), F32)],
        scratch_shapes=[pltpu.VMEM((N_EXPERTS, LANES), F32)],
        compiler_params=_params("arbitrary", "arbitrary"),
        name="ffn_pre",
    )(h, g.reshape(1, d), shift, scale, router_w_pad, router_b.reshape(N_EXPERTS, 1), tri)


def _experts_kernel(be_ref, nb_ref, x_ref, wg_ref, wu_ref, wd_ref, o_ref, wg_s, wu_s, wd_s):
    i = pl.program_id(0)
    prev = be_ref[jnp.maximum(i - 1, 0)]
    changed = jnp.logical_or(i == 0, be_ref[i] != prev)

    @pl.when(changed)
    def _():
        wg_s[...] = wg_ref[...].astype(BF16)
        wu_s[...] = wu_ref[...].astype(BF16)
        wd_s[...] = wd_ref[...].astype(BF16)

    @pl.when(i < nb_ref[0])
    def _():
        x = _unpack_bf16_pairs(x_ref[...]).astype(BF16)
        hid = _silu(_dot(x, wg_s[...])) * _dot(x, wu_s[...])
        o_ref[...] = _pack_bf16_pairs(_dot(hid.astype(BF16), wd_s[...]))

    @pl.when(i >= nb_ref[0])
    def _():
        o_ref[...] = jnp.zeros_like(o_ref)


def _experts(xb, block_e, n_used, layer, w_gate, w_up, w_down):
    n_slots = xb.shape[0]
    n_blocks = n_slots // MOE_BLOCK
    d, de = w_gate.shape[-2:]
    wspec = lambda shape: pl.BlockSpec((None, None) + shape, lambda i, be, nb: (layer, be[i], 0, 0))
    return pl.pallas_call(
        _experts_kernel,
        grid_spec=pltpu.PrefetchScalarGridSpec(
            num_scalar_prefetch=2,
            grid=(n_blocks,),
            in_specs=[pl.BlockSpec((MOE_BLOCK, d // 2), lambda i, be, nb: (i, 0)),
                      wspec((d, de)), wspec((d, de)), wspec((de, d))],
            out_specs=pl.BlockSpec((MOE_BLOCK, d // 2), lambda i, be, nb: (i, 0)),
            scratch_shapes=[pltpu.VMEM((d, de), BF16), pltpu.VMEM((d, de), BF16), pltpu.VMEM((de, d), BF16)]),
        out_shape=jax.ShapeDtypeStruct((n_slots, d // 2), jnp.uint32),
        compiler_params=_params("arbitrary"),
        name="moe_experts",
    )(block_e, n_used, xb, w_gate, w_up, w_down)


def _ffn_post_kernel(h_ref, y0_ref, y1_ref, w_ref, gt_ref, g_ref, o_ref, *, final):
    w = w_ref[...]
    y = w[:, 0:1] * _unpack_bf16_pairs(y0_ref[...]) + w[:, 1:2] * _unpack_bf16_pairs(y1_ref[...])
    out = h_ref[...] + gt_ref[...] * y
    o_ref[...] = _rms(out, g_ref[...]) if final else out


def _ffn_post(h, y, w, gate, n_ctx, final_g=None):
    bsz, t, d = h.shape
    tm = ROW_TILE
    seg = lambda b, i: (b, (i * tm >= n_ctx).astype(jnp.int32), 0, 0)
    row = lambda width: pl.BlockSpec((None, tm, width), lambda b, i: (b, i, 0))
    choice = lambda kk: pl.BlockSpec((None, None, tm, d // 2), lambda b, i: (kk, b, i, 0))
    final = final_g is not None
    g = final_g if final else jnp.ones((d,), F32)
    return pl.pallas_call(
        functools.partial(_ffn_post_kernel, final=final),
        grid=(bsz, t // tm),
        in_specs=[row(d), choice(0), choice(1), row(LANES), pl.BlockSpec((None, None, 1, d), seg),
                  pl.BlockSpec((1, d), lambda b, i: (0, 0))],
        out_specs=row(d),
        out_shape=jax.ShapeDtypeStruct((bsz, t, d), F32),
        compiler_params=_params("parallel", "parallel"),
        name="ffn_post",
    )(h, y, y, w, gate, g.reshape(1, d))


def _slot_layout(n, ri, counts):
    e = jnp.swapaxes(ri[:, 0:2], 1, 2).reshape(n, TOP_K)
    rank = jnp.swapaxes(ri[:, 2:4], 1, 2).reshape(n, TOP_K)
    padded = (counts + MOE_BLOCK - 1) // MOE_BLOCK * MOE_BLOCK
    pend = jnp.cumsum(padded)
    pstart = pend - padded
    experts = jnp.arange(N_EXPERTS, dtype=jnp.int32)
    dest = rank + jnp.sum(jnp.where(e[..., None] == experts, pstart, 0), axis=-1)
    n_slots = (n * TOP_K + MOE_BLOCK - 1) // MOE_BLOCK * MOE_BLOCK + N_EXPERTS * MOE_BLOCK
    n_blocks = n_slots // MOE_BLOCK
    tok = jnp.repeat(jnp.arange(n, dtype=jnp.int32), TOP_K)
    slot_tok = jnp.zeros((n_slots,), jnp.int32).at[dest.reshape(-1)].set(tok)
    blk0 = jnp.arange(n_blocks, dtype=jnp.int32)[:, None] * MOE_BLOCK
    block_e = jnp.minimum(jnp.sum((pend[None, :] <= blk0).astype(jnp.int32), axis=-1), N_EXPERTS - 1)
    n_used = (pend[-1] // MOE_BLOCK).astype(jnp.int32).reshape(1)
    return dest, slot_tok, block_e.astype(jnp.int32), n_used


SC_CORES, SC_SUBCORES = 2, 16
SC_WINDOW = 32


def _gather_rows(table, idx):
    n_rows, d = idx.shape[0], table.shape[1]
    workers = SC_CORES * SC_SUBCORES
    per_worker = n_rows // workers
    assert per_worker * workers == n_rows and per_worker % SC_WINDOW == 0
    mesh = plsc.VectorSubcoreMesh(core_axis_name="c", subcore_axis_name="s")

    @functools.partial(
        pl.kernel, mesh=mesh,
        out_type=jax.ShapeDtypeStruct((n_rows, d), table.dtype),
        scratch_types=[pltpu.VMEM((SC_WINDOW,), jnp.int32), pltpu.VMEM((SC_WINDOW,), jnp.int32),
                       pltpu.VMEM((SC_WINDOW, d), table.dtype), pltpu.VMEM((SC_WINDOW, d), table.dtype),
                       pltpu.SemaphoreType.DMA, pltpu.SemaphoreType.DMA],
    )
    def gather_kernel(table_hbm, idx_hbm, out_hbm, idx0, idx1, rows0, rows1, sem0, sem1):
        base = (lax.axis_index("s") * SC_CORES + lax.axis_index("c")) * per_worker
        n_win = per_worker // SC_WINDOW
        slots = ((idx0, rows0, sem0), (idx1, rows1, sem1))
        window = lambda j: pl.ds(pl.multiple_of(base + j * SC_WINDOW, 8), SC_WINDOW)

        def start(j, slot):
            idx_v, rows_v, sem = slots[slot]
            pltpu.sync_copy(idx_hbm.at[window(j)], idx_v)
            pltpu.async_copy(table_hbm.at[idx_v], rows_v, sem)

        def finish(j, slot):
            idx_v, rows_v, sem = slots[slot]
            pltpu.make_async_copy(table_hbm.at[idx_v], rows_v, sem).wait()
            pltpu.sync_copy(rows_v, out_hbm.at[window(j)])

        start(0, 0)

        @pl.loop(0, n_win, step=2)
        def _(j):
            @pl.when(j + 1 < n_win)
            def _():
                start(j + 1, 1)

            finish(j, 0)

            @pl.when(j + 2 < n_win)
            def _():
                start(j + 2, 0)

            @pl.when(j + 1 < n_win)
            def _():
                finish(j + 1, 1)

    return gather_kernel(table, idx)


def _alongside(gather, idx, side_fn, side_in):
    idx, side_in = lax.optimization_barrier((idx, side_in))
    return lax.optimization_barrier((gather(idx), side_fn(side_in)))


def _moe(h, g, shift, scale, gate, router_w_pad, router_b, layer, w_gate, w_up, w_down, n_ctx,
         final_g=None, side=None):
    bsz, t, d = h.shape
    n = bsz * t
    v, ri, rwt, counts = _ffn_pre(h, g, shift, scale, router_w_pad, router_b, n_ctx)
    dest, slot_tok, block_e, n_used = _slot_layout(n, ri, counts[:, 0].astype(jnp.int32))
    w = jnp.swapaxes(rwt[:, 0:2], 1, 2).reshape(n, TOP_K)
    dispatch = lambda idx: _gather_rows(v.reshape(n, d // 2), idx)
    if side is None:
        xb = dispatch(slot_tok)
    else:
        xb, side_a = _alongside(dispatch, slot_tok, *side[0])
    yb = _experts(xb, block_e, n_used, layer, w_gate, w_up, w_down)
    combine = lambda idx: _gather_rows(yb, idx)
    dest_flat = jnp.swapaxes(dest, 0, 1).reshape(-1)
    if side is None:
        y, side_b = combine(dest_flat), None
    else:
        y, side_b = _alongside(combine, dest_flat, side[1], side_a)
    wpad = jnp.zeros((n, LANES), F32).at[:, :TOP_K].set(w).reshape(bsz, t, LANES)
    out = _ffn_post(h, y.reshape(TOP_K, bsz, t, d // 2), wpad, gate, n_ctx, final_g)
    return out if side is None else (out, side_b)


DFT_STEP = 8


def _dft_tables(n):
    r, *mats = _dft_tables_np(n)
    return (r,) + tuple(jnp.asarray(a).astype(BF16) for a in mats)


@functools.lru_cache(maxsize=None)
def _dft_tables_np(n):
    size = 2 * n
    r = int(round(math.sqrt(size)))
    assert r * r == size and r % DFT_STEP == 0
    p1 = np.arange(r // 2)[None, None, :]
    p2 = np.arange(r)[:, None, None]
    k1 = np.arange(r)[None, :, None]
    ang = 2.0 * np.pi * (((r * p1 + p2) * k1) % size) / size
    g_re, g_im = np.cos(ang), -np.sin(ang)
    g_in = np.concatenate([g_re, g_im], axis=1)
    g_out = np.concatenate([np.swapaxes(g_re, 1, 2), np.swapaxes(g_im, 1, 2)], axis=2) / size
    a2 = 2.0 * np.pi * ((np.arange(r)[:, None] * np.arange(r)[None, :]) % r) / r
    f_re, f_im = np.cos(a2), -np.sin(a2)
    f_fwd = np.block([[f_re, -f_im], [f_im, f_re]])
    f_inv = np.block([[f_re, f_im], [-f_im, f_re]])
    p1f = np.arange(r)[None, None, :]
    angf = 2.0 * np.pi * (((r * p1f + p2) * k1) % size) / size
    g_full = np.concatenate([np.cos(angf), -np.sin(angf)], axis=1)
    return (r,) + tuple(a.astype(np.float32) for a in (g_in, g_out, f_fwd, f_inv, g_full))


def _hy_filter_kernel(z_ref, w1_ref, b1_ref, w2_ref, b2_ref, w3_ref, fr_ref, rates_ref, o_ref, *, half_tiles):
    i = pl.program_id(0)
    z = z_ref[...]
    hid = jnp.sin(fr_ref[...] * (_dot_f32(z, w1_ref[...]) + b1_ref[...]))
    hid = jnp.sin(fr_ref[...] * (_dot_f32(hid, w2_ref[...]) + b2_ref[...]))
    filt = _dot_f32(hid, w3_ref[...])
    decay = jnp.exp(-z[:, 0:1] * rates_ref[...])
    row0 = lax.broadcasted_iota(jnp.int32, decay.shape, 0) == 0
    for o in range(o_ref.shape[0]):
        fwd = filt[:, (2 * o) * HY_CH:(2 * o + 1) * HY_CH] * decay
        bwd = filt[:, (2 * o + 1) * HY_CH:(2 * o + 2) * HY_CH] * decay
        first = jnp.where(jnp.logical_and(row0, i == 0), fwd + bwd, fwd)
        second = jnp.where(jnp.logical_and(row0, i == half_tiles), 0.0, bwd)
        o_ref[o] = jnp.where(i < half_tiles, first, second)


def _hy_kernels(n, w1, b1, w2, b2, w3, freq):
    pos = np.arange(2 * n)
    pos = np.where(pos < n, pos, 2 * n - pos).astype(np.float32)
    t = jnp.asarray(pos / np.float32(n - 1))[:, None]
    bands = jnp.linspace(1e-4, HY_BANDS - 1, HY_BANDS, dtype=F32)
    ang = (2.0 * math.pi / n) * jnp.asarray(pos)[:, None] * bands
    z = jnp.concatenate([t, jnp.cos(ang), -jnp.sin(ang)], axis=-1)
    z = jnp.pad(z, ((0, 0), (0, LANES - z.shape[1])))
    w1p = jnp.pad(w1, ((0, LANES - w1.shape[0]), (0, 0)))
    rates = jnp.abs(jnp.linspace(HY_MIN_DECAY, HY_MAX_DECAY, HY_CH, dtype=F32)).reshape(1, HY_CH)
    tm = ROW_TILE
    full = lambda a: pl.BlockSpec(a.shape, lambda i: (0,) * a.ndim)
    args = (z, w1p, b1.reshape(1, -1), w2, b2.reshape(1, -1), w3, freq.reshape(1, -1), rates)
    return pl.pallas_call(
        functools.partial(_hy_filter_kernel, half_tiles=n // tm),
        grid=(2 * n // tm,),
        in_specs=[pl.BlockSpec((tm, LANES), lambda i: (i, 0))] + [full(a) for a in args[1:]],
        out_specs=pl.BlockSpec((HY_ORDER, tm, HY_CH), lambda i: (0, i, 0)),
        out_shape=jax.ShapeDtypeStruct((HY_ORDER, 2 * n, HY_CH), F32),
        compiler_params=_params("parallel"),
        name="hy_kernels",
    )(*args)


def _dft_in_kernel(x_ref, g_ref, a_ref):
    for j in range(DFT_STEP):
        a_ref[j] = _dot(g_ref[j], x_ref[:, j, :].astype(BF16))


def _dft_in(x4, col, g_in):
    bx, rh, r, _ = x4.shape
    c = HY_CH
    return pl.pallas_call(
        _dft_in_kernel,
        grid=(bx, r // DFT_STEP),
        in_specs=[pl.BlockSpec((None, rh, DFT_STEP, c), lambda b, i: (b, 0, i, col)),
                  pl.BlockSpec((DFT_STEP, 2 * r, rh), lambda b, i: (i, 0, 0))],
        out_specs=pl.BlockSpec((None, DFT_STEP, 2 * r, c), lambda b, i: (b, i, 0, 0)),
        out_shape=jax.ShapeDtypeStruct((bx, r, 2 * r, c), F32),
        compiler_params=_params("parallel", "parallel"),
        name="dft_in",
    )(x4, g_in)


def _stage2(a_ref, b, j, f_ref):
    a = jnp.concatenate([a_ref[b, :, 0, j, :], a_ref[b, :, 1, j, :]], axis=0)
    return _dot(f_ref[...], a.astype(BF16))


def _dft_filt_kernel(a_ref, f_ref, k_ref):
    for j in range(DFT_STEP):
        k_ref[j] = _stage2(a_ref, 0, j, f_ref)


def _dft_filt(a, f_fwd):
    nq, r, _, c = a.shape
    a5 = a.reshape(nq, r, 2, r, c)
    return pl.pallas_call(
        _dft_filt_kernel,
        grid=(nq, r // DFT_STEP),
        in_specs=[pl.BlockSpec((1, r, 2, DFT_STEP, c), lambda o, i: (o, 0, 0, i, 0)),
                  pl.BlockSpec(f_fwd.shape, lambda o, i: (0, 0))],
        out_specs=pl.BlockSpec((None, DFT_STEP, 2 * r, c), lambda o, i: (o, i, 0, 0)),
        out_shape=jax.ShapeDtypeStruct((nq, r, 2 * r, c), F32),
        compiler_params=_params("parallel", "parallel"),
        name="dft_filt",
    )(a5, f_fwd)


def _dft_mid_kernel(a_ref, k_ref, ff_ref, fi_ref, b_ref):
    r = ff_ref.shape[0] // 2
    for b in range(a_ref.shape[0]):
        for j in range(DFT_STEP):
            s = _stage2(a_ref, b, j, ff_ref)
            sr, si = s[0:r], s[r:2 * r]
            kr, ki = k_ref[j, 0:r, :], k_ref[j, r:2 * r, :]
            p = jnp.concatenate([sr * kr - si * ki, sr * ki + si * kr], axis=0).astype(BF16)
            b_ref[b, j] = _dot(fi_ref[...], p)


def _dft_mid(a, kspec, order, f_fwd, f_inv):
    bsz, r, _, c = a.shape
    a5 = a.reshape(bsz, r, 2, r, c)
    return pl.pallas_call(
        _dft_mid_kernel,
        grid=(r // DFT_STEP,),
        in_specs=[pl.BlockSpec((bsz, r, 2, DFT_STEP, c), lambda i: (0, 0, 0, i, 0)),
                  pl.BlockSpec((None, DFT_STEP, 2 * r, c), lambda i: (order, i, 0, 0)),
                  pl.BlockSpec(f_fwd.shape, lambda i: (0, 0)),
                  pl.BlockSpec(f_inv.shape, lambda i: (0, 0))],
        out_specs=pl.BlockSpec((bsz, DFT_STEP, 2 * r, c), lambda i: (0, i, 0, 0)),
        out_shape=jax.ShapeDtypeStruct((bsz, r, 2 * r, c), F32),
        compiler_params=_params("parallel"),
        name="dft_mid",
    )(a5, kspec, f_fwd, f_inv)


def _dft_out_kernel(b_ref, g_ref, u_ref, x_ref, bias_ref, o_ref):
    for j in range(DFT_STEP):
        rhs = jnp.concatenate([b_ref[:, 0, j, :], b_ref[:, 1, j, :]], axis=0).astype(BF16)
        y = _dot(g_ref[j], rhs)
        o_ref[:, j, :] = x_ref[:, j, :] * (y + u_ref[:, j, :] * bias_ref[...])


def _dft_out(bm, g_out, u4, u_col, x4, x_col, bias):
    bsz, r, _, c = bm.shape
    rh = r // 2
    b5 = bm.reshape(bsz, r, 2, r, c)
    seq = lambda col: pl.BlockSpec((None, rh, DFT_STEP, c), lambda b, i: (b, 0, i, col))
    return pl.pallas_call(
        _dft_out_kernel,
        grid=(bsz, r // DFT_STEP),
        in_specs=[pl.BlockSpec((None, r, 2, DFT_STEP, c), lambda b, i: (b, 0, 0, i, 0)),
                  pl.BlockSpec((DFT_STEP, rh, 2 * r), lambda b, i: (i, 0, 0)),
                  seq(u_col), seq(x_col), pl.BlockSpec((1, c), lambda b, i: (0, 0))],
        out_specs=seq(0),
        out_shape=jax.ShapeDtypeStruct((bsz, rh, r, c), F32),
        compiler_params=_params("parallel", "parallel"),
        name="dft_out",
    )(b5, g_out, u4, x4, bias.reshape(1, c))


def _hyena_filter_stage1(n, filter_params):
    r, g_full = _dft_tables(n)[0], _dft_tables(n)[5]
    kern = _hy_kernels(n, *filter_params)
    return _dft_in(kern.reshape(-1, r, r, HY_CH), 0, g_full)


def _hyena_filter_spectra(n, stage1):
    return _dft_filt(stage1, _dft_tables(n)[3])


def _hyena(hy_in, kspec, conv_bias):
    bsz, n, _ = hy_in.shape
    r, g_in, g_out, f_fwd, f_inv, _ = _dft_tables(n)
    seq4 = hy_in.reshape(bsz, r // 2, r, 3 * HY_CH)
    zz = _dft_out(_dft_mid(_dft_in(seq4, 0, g_in), kspec, 0, f_fwd, f_inv), g_out,
                  seq4, 0, seq4, 1, conv_bias[0])
    out = _dft_out(_dft_mid(_dft_in(zz, 0, g_in), kspec, 1, f_fwd, f_inv), g_out,
                   zz, 0, seq4, 2, conv_bias[1])
    return out.reshape(bsz, n, HY_CH)


CD_HY, CD_Z, CD_XBC, CD_DT = 0, 1536, 2048, 3072
CD_PAD_COLS = CD_DT + 2 * LANES


def _reorder_ab(w):
    gq, gk, gv, gg, lr_f, lr_b, hq, hf_f, hf_b, hi, hg = jnp.split(
        w, np.cumsum([256, 256, 512, 512, 16, 16, 512, 512, 512, 512, 512])[:-1].tolist(), axis=-1)
    pad = jnp.zeros((w.shape[0], AB_PAD_COLS - AB_LR - 2 * GLA_LOW_RANK), w.dtype)
    return jnp.concatenate([gq, gk, gv, gg, hq, hf_f, hf_b, hi, hg, lr_f, lr_b, pad], axis=-1)


def _reorder_cd(w):
    hy, z, xbc, dt_f, dt_b = jnp.split(w, np.cumsum([1536, 512, 1024, 8, 8])[:-1].tolist(), axis=-1)
    pad = jnp.zeros((w.shape[0], LANES - MB_HEADS), w.dtype)
    return jnp.concatenate([hy, z, xbc, dt_f, pad, dt_b, pad], axis=-1)


def kernel(x, c, ctx, c_ctx, ada_w, ada_b, norm_mix_g, norm_ffn_g, norm_out_g, ab_w_in, ab_w_out, gla_gate_w, gla_gate_b, gla_norm_g, hg_lb, hg_norm_g, cd_w_in, cd_w_out, hy_short_w, hy_short_b, hy_w1, hy_b1, hy_w2, hy_b2, hy_w3, hy_freq, hy_bias, mb_conv_w, mb_conv_b, mb_dt_bias, mb_a_log, mb_d, mb_norm_g, router_w, router_b, moe_w_gate, moe_w_up, moe_w_down):
    bsz, n_lat, d = x.shape
    n_ctx = ctx.shape[1]
    t = n_ctx + n_lat
    assert ada_w.shape[0] == 2 and ab_w_in.shape[0] == 1 and cd_w_in.shape[0] == 1

    cond = jnp.zeros((8, d), F32).at[:bsz].set(c).at[bsz].set(c_ctx)
    m = _adaln(cond, ada_w, ada_b)

    def mods(layer):
        lat = m[layer, :bsz].reshape(bsz, 6, d)
        cx = jnp.broadcast_to(m[layer, bsz].reshape(1, 6, d), (bsz, 6, d))
        both = jnp.stack([cx, lat], axis=1)
        return [both[:, :, j][:, :, None, :] for j in range(6)]

    lb_all = jnp.cumsum(jax.nn.softmax(hg_lb.astype(F32), axis=1), axis=1)
    router_w_pad = jnp.zeros((d, LANES), F32).at[:, :N_EXPERTS].set(router_w)
    h = jnp.concatenate([ctx, x], axis=1)

    sh_m, sc_m, gt_m, sh_f, sc_f, gt_f = mods(0)
    proj = _norm_proj(h, norm_mix_g[0], sh_m, sc_m, _reorder_ab(ab_w_in[0]).astype(BF16), n_ctx)
    gwp = [jnp.zeros((LANES, GLA_KEY_W), F32).at[GLA_LOW_RANK * dd:GLA_LOW_RANK * (dd + 1)].set(gla_gate_w[0, dd])
           for dd in range(2)]
    o_gla = _gla_scan(proj, gwp, [gla_gate_b[0, dd].reshape(1, -1) for dd in range(2)], n_ctx)
    o_hg = _hgrn_scan(proj, [lb_all[dd, 0].reshape(1, -1) for dd in range(2)], n_ctx)
    h = _mix_out_ab(h, o_gla, o_hg, proj, gla_norm_g[0], hg_norm_g[0], ab_w_out[0].astype(BF16), gt_m, n_ctx)
    filter_params = (hy_w1[0], hy_b1[0], hy_w2[0], hy_b2[0], hy_w3[0], hy_freq[0])
    side = ((functools.partial(_hyena_filter_stage1, n_lat), filter_params),
            functools.partial(_hyena_filter_spectra, n_lat))
    h, kspec = _moe(h, norm_ffn_g[0], sh_f, sc_f, gt_f, router_w_pad, router_b,
                    0, moe_w_gate, moe_w_up, moe_w_down, n_ctx, side=side)

    sh_m, sc_m, gt_m, sh_f, sc_f, gt_f = mods(1)
    proj = _norm_proj(h, norm_mix_g[1], sh_m, sc_m, _reorder_cd(cd_w_in[0]).astype(BF16), n_ctx)
    hy_in = _dwconv(proj, CD_HY, 3 * HY_CH, hy_short_w[0], hy_short_b[0], (n_ctx, t), act=False)
    hy = _hyena(hy_in, kspec, hy_bias[0])
    xbc = _dwconv(proj, CD_XBC, MB_INNER + 2 * MB_BC_W, mb_conv_w[0], mb_conv_b[0], (0, n_ctx, t), act=True)
    y_ssd = [_ssd_scan(xbc, proj, CD_DT // LANES + dd, mb_dt_bias[0, dd], mb_a_log[0, dd], n_ctx, dd == 1)
             for dd in range(2)]
    d_skip_x = jnp.repeat(mb_d[0], MB_HEAD_DIM).reshape(1, MB_INNER)
    h = _mix_out_cd(h, hy, y_ssd, xbc, proj, CD_Z // MB_INNER, d_skip_x, mb_norm_g[0],
                    cd_w_out[0].astype(BF16), gt_m, n_ctx)
    return _moe(h, norm_ffn_g[1], sh_f, sc_f, gt_f, router_w_pad, router_b,
                1, moe_w_gate, moe_w_up, moe_w_down, 0, final_g=norm_out_g)
```

```python
import functools
import math

import numpy as np
import jax
import jax.numpy as jnp
from jax import lax
from jax.experimental import pallas as pl
from jax.experimental.pallas import tpu as pltpu
from jax.experimental.pallas import tpu_sc as plsc

NORM_EPS = 1e-6
GLA_HEADS, GLA_DK, GLA_DV, GLA_LOW_RANK, GLA_TAU = 4, 64, 128, 16, 16.0
GLA_KEY_W, GLA_VAL_W = GLA_HEADS * GLA_DK, GLA_HEADS * GLA_DV
HG_HEADS, HG_EXPAND, HG_DV = 4, 128, 128
HG_KEY_W, HG_VAL_W = HG_HEADS * HG_EXPAND, HG_HEADS * HG_DV
HY_CH, HY_ORDER, HY_SHORT, HY_BANDS, HY_FILT_HID = 512, 2, 3, 16, 64
HY_MIN_DECAY = math.log(1e-2) / 1.5
HY_MAX_DECAY = math.log(1e-2) / 0.3
MB_HEADS, MB_HEAD_DIM, MB_GROUPS, MB_STATE = 8, 64, 2, 128
MB_INNER = MB_HEADS * MB_HEAD_DIM
MB_BC_W = MB_GROUPS * MB_STATE
N_EXPERTS, N_GROUPS, TOP_K, MOE_BLOCK = 16, 4, 2, 256
EXPERTS_PER_GROUP = N_EXPERTS // N_GROUPS

LANES = 128
SCAN_CHUNK = 64
SCAN_BLOCK = 128
SSD_CHUNK = 128
ROW_TILE = 256
VMEM_LIMIT = 56 * 1024 * 1024

BF16 = jnp.bfloat16
F32 = jnp.float32


def _params(*sem):
    return pltpu.CompilerParams(dimension_semantics=sem, vmem_limit_bytes=VMEM_LIMIT)


def _split3(x):
    hi = x.astype(BF16)
    r1 = x - hi.astype(F32)
    mid = r1.astype(BF16)
    lo = (r1 - mid.astype(F32)).astype(BF16)
    return hi, mid, lo


def _dot(a, b):
    return jnp.dot(a, b, preferred_element_type=F32)


def _dot_nt(a, b):
    return lax.dot_general(a, b, (((1,), (1,)), ((), ())), preferred_element_type=F32)


def _dot_tn(a, b):
    return lax.dot_general(a, b, (((0,), (0,)), ((), ())), preferred_element_type=F32)


def _sel_dot(m01, x):
    hi, mid, lo = _split3(x)
    return _dot(m01, hi) + (_dot(m01, mid) + _dot(m01, lo))


def _dot_sel(x, m01):
    hi, mid, lo = _split3(x)
    return _dot(hi, m01) + (_dot(mid, m01) + _dot(lo, m01))


def _dot_f32(a, b):
    ah = a.astype(BF16)
    al = (a - ah.astype(F32)).astype(BF16)
    bh = b.astype(BF16)
    bl = (b - bh.astype(F32)).astype(BF16)
    return _dot(ah, bh) + (_dot(ah, bl) + _dot(al, bh))


def _silu(x):
    return x * (1.0 / (1.0 + jnp.exp(-x)))


def _sigmoid(x):
    return 1.0 / (1.0 + jnp.exp(-x))


def _softplus(x):
    return jnp.maximum(x, 0.0) + jnp.log(1.0 + jnp.exp(-jnp.abs(x)))


def _pack_bf16_pairs(x):
    bits = lax.bitcast_convert_type(x.astype(BF16).astype(F32), jnp.uint32)
    half = x.shape[1] // 2
    return bits[:, :half] | (bits[:, half:] >> 16)


def _unpack_bf16_pairs(p):
    hi = lax.bitcast_convert_type(p & jnp.uint32(0xFFFF0000), F32)
    lo = lax.bitcast_convert_type(p << 16, F32)
    return jnp.concatenate([hi, lo], axis=1)


def _rms(x, g):
    return x * lax.rsqrt(jnp.mean(x * x, axis=-1, keepdims=True) + NORM_EPS) * g


def _adaln_kernel(c_ref, w_ref, b_ref, o_ref):
    o_ref[...] = _dot_f32(_silu(c_ref[...]), w_ref[...]) + b_ref[...]


def _adaln(cond, w, b):
    n_l, d, n6 = w.shape
    tn = 1536
    return pl.pallas_call(
        _adaln_kernel,
        grid=(n_l, n6 // tn),
        in_specs=[pl.BlockSpec((8, d), lambda l, j: (0, 0)),
                  pl.BlockSpec((None, d, tn), lambda l, j: (l, 0, j)),
                  pl.BlockSpec((None, 1, tn), lambda l, j: (l, 0, j))],
        out_specs=pl.BlockSpec((None, 8, tn), lambda l, j: (l, 0, j)),
        out_shape=jax.ShapeDtypeStruct((n_l, 8, n6), F32),
        compiler_params=_params("parallel", "parallel"),
        name="adaln",
    )(cond, w, b.reshape(n_l, 1, n6))


def _norm_proj_kernel(h_ref, g_ref, sh_ref, sc_ref, w_ref, o_ref):
    u = _rms(h_ref[...], g_ref[...]) * (1.0 + sc_ref[...]) + sh_ref[...]
    o_ref[...] = _dot(u.astype(BF16), w_ref[...])


def _norm_proj(h, g, shift, scale, w, n_ctx):
    bsz, t, d = h.shape
    n = w.shape[1]
    tm = ROW_TILE
    seg = lambda b, i: (b, (i * tm >= n_ctx).astype(jnp.int32), 0, 0)
    return pl.pallas_call(
        _norm_proj_kernel,
        grid=(bsz, t // tm),
        in_specs=[pl.BlockSpec((None, tm, d), lambda b, i: (b, i, 0)),
                  pl.BlockSpec((1, d), lambda b, i: (0, 0)),
                  pl.BlockSpec((None, None, 1, d), seg),
                  pl.BlockSpec((None, None, 1, d), seg),
                  pl.BlockSpec((d, n), lambda b, i: (0, 0))],
        out_specs=pl.BlockSpec((None, tm, n), lambda b, i: (b, i, 0)),
        out_shape=jax.ShapeDtypeStruct((bsz, t, n), F32),
        compiler_params=_params("parallel", "parallel"),
        name="norm_proj",
    )(h, g.reshape(1, d), shift, scale, w)


def _scan_constants(c, reverse):
    t = np.arange(c)[:, None]
    u = np.arange(c)[None, :]
    sels = [u <= t, u > t]
    masks = []
    m = c // 2
    while m >= 1:
        blk = t // (2 * m)
        upper_t = (t % (2 * m)) >= m
        r = blk * (2 * m) + m - 1
        s_blk = u // (2 * m)
        upper_s = (u % (2 * m)) >= m
        sels.append((upper_t & (u > r) & (u <= t)) | ((~upper_t) & (u > t) & (u <= r)))
        masks.append((blk == s_blk) & upper_t & (~upper_s))
        m //= 2
    masks.append(t == u)
    sel = np.stack(sels).astype(np.float32)
    msk = np.stack(masks).astype(np.float32)
    if reverse:
        sel = sel[:, ::-1, ::-1]
        msk = msk[:, ::-1, ::-1]
    return np.ascontiguousarray(sel.reshape(-1, c)), np.ascontiguousarray(msk)


def _chunk_order(i, n_ctx_chunks, n_chunks, reverse):
    if not reverse:
        return i
    return jnp.where(i < n_ctx_chunks, n_ctx_chunks - 1 - i, n_chunks - 1 - (i - n_ctx_chunks))


GROUP_KEYS = 256


def _decay_chunk(q, k, v, la, sel_ref, mask_ref, hm_ref, st_ref, heads, dk, dv):
    c = q.shape[0]
    n_lvl = mask_ref.shape[0] - 1
    hpg = GROUP_KEYS // dk
    cs = _dot(sel_ref[...], jnp.concatenate(_split3(la), axis=0))
    e_q = jnp.exp(cs[0:c])
    e_k = jnp.exp(cs[c:2 * c])
    e_tot = jnp.exp(jnp.sum(la, axis=0, keepdims=True))
    vb = v.astype(BF16)
    outs = []
    for g in range(heads // hpg):
        ks = slice(g * GROUP_KEYS, (g + 1) * GROUP_KEYS)
        vs = slice(g * hpg * dv, (g + 1) * hpg * dv)
        qg, kg = q[:, ks], k[:, ks]
        stack = lambda x: jnp.concatenate([x * hm_ref[h] for h in range(hpg)], axis=0).astype(BF16)
        att = mask_ref[n_lvl] * _dot_nt(stack(qg), kg.astype(BF16))
        for l in range(n_lvl):
            e = jnp.exp(cs[(2 + l) * c:(3 + l) * c, ks])
            att = att + mask_ref[l] * _dot_nt(stack(qg * e), (kg * e).astype(BF16))
        att = att.astype(BF16)
        st = st_ref[g]
        inter = _dot_nt(stack(qg * e_q[:, ks]), st.astype(BF16))
        upd = _dot_tn(vb[:, vs], (kg * e_k[:, ks]).astype(BF16))
        new = st * e_tot[:, ks]
        for h in range(hpg):
            rows = slice(h * c, (h + 1) * c)
            hv = slice(g * hpg * dv + h * dv, g * hpg * dv + (h + 1) * dv)
            outs.append(_dot(att[rows], vb[:, hv]) + inter[rows])
            new = new + upd[h * dv:(h + 1) * dv] * hm_ref[h]
        st_ref[g] = new
    return jnp.concatenate(outs, axis=-1)


def _log_sigmoid(x):
    return jnp.minimum(x, 0.0) - jnp.log(1.0 + jnp.exp(-jnp.abs(x)))


def _gla_kernel(*refs):
    ins, hm_ref, (o_refs, st_refs) = (refs[0:8], refs[8:16]), refs[16], (refs[17:19], refs[19:21])

    @pl.when(pl.program_id(1) == 0)
    def _():
        for st_ref in st_refs:
            st_ref[...] = jnp.zeros_like(st_ref)

    for d, ((q_ref, k_ref, v_ref, lr_ref, gw_ref, gb_ref, sel_ref, mask_ref), o_ref, st_ref) in enumerate(
            zip(ins, o_refs, st_refs)):
        z = _dot_f32(lr_ref[...], gw_ref[...]) + gb_ref[...]
        la = _log_sigmoid(z) * (1.0 / GLA_TAU)
        q = q_ref[...] * (GLA_DK ** -0.5)
        k, v = k_ref[...], v_ref[...]
        for rows in _sub_chunks(q.shape[0], d == 1):
            o_ref[rows, :] = _decay_chunk(q[rows], k[rows], v[rows], la[rows], sel_ref, mask_ref, hm_ref,
                                          st_ref, GLA_HEADS, GLA_DK, GLA_DV)


def _hgrn_kernel(*refs):
    ins, hm_ref, (o_refs, st_refs) = (refs[0:6], refs[6:12]), refs[12], (refs[13:15], refs[15:17])

    @pl.when(pl.program_id(1) == 0)
    def _():
        for st_ref in st_refs:
            st_ref[...] = jnp.zeros_like(st_ref)

    for d, ((q_ref, f_ref, v_ref, lb_ref, sel_ref, mask_ref), o_ref, st_ref) in enumerate(
            zip(ins, o_refs, st_refs)):
        lb = lb_ref[...]
        f = lb + (1.0 - lb) * _sigmoid(f_ref[...])
        q, k, v, la = _silu(q_ref[...]), 1.0 - f, v_ref[...], jnp.log(f)
        for rows in _sub_chunks(q.shape[0], d == 1):
            o_ref[rows, :] = _decay_chunk(q[rows], k[rows], v[rows], la[rows], sel_ref, mask_ref, hm_ref,
                                          st_ref, HG_HEADS, HG_EXPAND, HG_DV)


def _sub_chunks(rows, reverse):
    order = range(rows // SCAN_CHUNK)
    return [slice(j * SCAN_CHUNK, (j + 1) * SCAN_CHUNK) for j in (reversed(order) if reverse else order)]


def _scan_specs(blk, n_ctx, t, reverse, chunk=None, stacked_heads=1):
    n_blocks = t // blk
    order = functools.partial(_chunk_order, n_ctx_chunks=n_ctx // blk, n_chunks=n_blocks, reverse=reverse)

    def col(width, idx):
        return pl.BlockSpec((None, blk, width), lambda b, i: (b, order(i), idx))

    sel, msk = _scan_constants(chunk or blk, reverse)
    sel3 = np.concatenate([sel, sel, sel], axis=1)
    msk = np.tile(msk, (1, stacked_heads, 1))
    const = lambda a: pl.BlockSpec(a.shape, lambda b, i: (0,) * a.ndim)
    return n_blocks, col, const, jnp.asarray(sel3, BF16), jnp.asarray(msk, F32)


def _head_masks(dk):
    hpg = GROUP_KEYS // dk
    hm = np.zeros((hpg, 1, GROUP_KEYS), np.float32)
    for h in range(hpg):
        hm[h, 0, h * dk:(h + 1) * dk] = 1.0
    return jnp.asarray(hm)


AB_Q, AB_K, AB_V, AB_G = 0, 256, 512, 1024
AB_HQ, AB_HF, AB_HI, AB_HG, AB_LR = 1536, 2048, 3072, 3584, 4096
AB_PAD_COLS = 4224


def _gla_scan(proj, gate_w_pad, gate_b, n_ctx):
    bsz, t, _ = proj.shape
    hpg = GROUP_KEYS // GLA_DK
    in_specs, args, outs = [], [], []
    for d in range(2):
        n_blocks, col, const, sel, msk = _scan_specs(SCAN_BLOCK, n_ctx, t, d == 1, SCAN_CHUNK, hpg)
        in_specs += [col(GLA_KEY_W, AB_Q // GLA_KEY_W), col(GLA_KEY_W, AB_K // GLA_KEY_W),
                     col(GLA_VAL_W, AB_V // GLA_VAL_W), col(LANES, AB_LR // LANES),
                     const(gate_w_pad[d]), const(gate_b[d]), const(sel), const(msk)]
        args += [proj, proj, proj, proj, gate_w_pad[d], gate_b[d], sel, msk]
        outs.append(col(GLA_VAL_W, 0))
    hm = _head_masks(GLA_DK)
    return pl.pallas_call(
        _gla_kernel,
        grid=(bsz, n_blocks),
        in_specs=in_specs + [const(hm)],
        out_specs=outs,
        out_shape=[jax.ShapeDtypeStruct((bsz, t, GLA_VAL_W), F32)] * 2,
        scratch_shapes=[pltpu.VMEM((GLA_HEADS // hpg, GLA_DV, GROUP_KEYS), F32)] * 2,
        compiler_params=_params("parallel", "arbitrary"),
        name="gla_scan",
    )(*args, hm)


def _hgrn_scan(proj, lb, n_ctx):
    bsz, t, _ = proj.shape
    hpg = GROUP_KEYS // HG_EXPAND
    in_specs, args, outs = [], [], []
    for d in range(2):
        n_blocks, col, const, sel, msk = _scan_specs(SCAN_BLOCK, n_ctx, t, d == 1, SCAN_CHUNK, hpg)
        in_specs += [col(HG_KEY_W, AB_HQ // HG_KEY_W), col(HG_KEY_W, AB_HF // HG_KEY_W + d),
                     col(HG_VAL_W, AB_HI // HG_VAL_W), const(lb[d]), const(sel), const(msk)]
        args += [proj, proj, proj, lb[d], sel, msk]
        outs.append(col(HG_VAL_W, 0))
    hm = _head_masks(HG_EXPAND)
    return pl.pallas_call(
        _hgrn_kernel,
        grid=(bsz, n_blocks),
        in_specs=in_specs + [const(hm)],
        out_specs=outs,
        out_shape=[jax.ShapeDtypeStruct((bsz, t, HG_VAL_W), F32)] * 2,
        scratch_shapes=[pltpu.VMEM((HG_HEADS // hpg, HG_DV, GROUP_KEYS), F32)] * 2,
        compiler_params=_params("parallel", "arbitrary"),
        name="hgrn_scan",
    )(*args, hm)


def _mix_out_ab_kernel(h_ref, gf_ref, gb_ref, hf_ref, hb_ref, gg_ref, hg_ref, gn_ref, hn_ref,
                       w_ref, gt_ref, o_ref):
    feats = []
    for o, gate, g in ((gf_ref[...] + gb_ref[...], gg_ref[...], gn_ref[...]),
                       (hf_ref[...] + hb_ref[...], hg_ref[...], hn_ref[...])):
        for hd in range(o.shape[-1] // LANES):
            s = slice(hd * LANES, (hd + 1) * LANES)
            feats.append(_rms(o[:, s], g) * _silu(gate[:, s]))
    feat = jnp.concatenate(feats, axis=-1).astype(BF16)
    o_ref[...] = h_ref[...] + gt_ref[...] * _dot(feat, w_ref[...])


def _mix_out_ab(h, o_gla, o_hg, proj, gla_norm_g, hg_norm_g, w_out, gate, n_ctx):
    bsz, t, d = h.shape
    tm = ROW_TILE
    seg = lambda b, i: (b, (i * tm >= n_ctx).astype(jnp.int32), 0, 0)
    row = lambda width, idx: pl.BlockSpec((None, tm, width), lambda b, i: (b, i, idx))
    vec = pl.BlockSpec((1, LANES), lambda b, i: (0, 0))
    return pl.pallas_call(
        _mix_out_ab_kernel,
        grid=(bsz, t // tm),
        in_specs=[row(d, 0), row(GLA_VAL_W, 0), row(GLA_VAL_W, 0), row(HG_VAL_W, 0), row(HG_VAL_W, 0),
                  row(GLA_VAL_W, AB_G // GLA_VAL_W), row(HG_VAL_W, AB_HG // HG_VAL_W), vec, vec,
                  pl.BlockSpec(w_out.shape, lambda b, i: (0, 0)),
                  pl.BlockSpec((None, None, 1, d), seg)],
        out_specs=row(d, 0),
        out_shape=jax.ShapeDtypeStruct((bsz, t, d), F32),
        compiler_params=_params("parallel", "parallel"),
        name="mix_out_ab",
    )(h, o_gla[0], o_gla[1], o_hg[0], o_hg[1], proj, proj, gla_norm_g.reshape(1, -1),
      hg_norm_g.reshape(1, -1), w_out, gate)


def _dwconv_kernel(x_ref, prev_ref, next_ref, w_ref, b_ref, o_ref, *, seg_tiles, act):
    i = pl.program_id(1)
    tm = x_ref.shape[0]
    x = x_ref[...]
    first = functools.reduce(jnp.logical_or, [i == s for s in seg_tiles[:-1]])
    last = functools.reduce(jnp.logical_or, [i == s - 1 for s in seg_tiles[1:]])
    prev_row = jnp.where(first, 0.0, prev_ref[7:8, :])
    next_row = jnp.where(last, 0.0, next_ref[0:1, :])
    rows = lax.broadcasted_iota(jnp.int32, x.shape, 0)
    x_prev = jnp.where(rows == 0, prev_row, pltpu.roll(x, 1, axis=0))
    x_next = jnp.where(rows == tm - 1, next_row, pltpu.roll(x, tm - 1, axis=0))
    y = x_prev * w_ref[0:1, :] + x * w_ref[1:2, :] + x_next * w_ref[2:3, :] + b_ref[...]
    o_ref[...] = _silu(y) if act else y


def _dwconv(proj, col0, width, w, b, seg_bounds, act):
    bsz, t, _ = proj.shape
    tm = ROW_TILE
    off = seg_bounds[0] // tm
    n_tiles = t // tm - off
    seg_tiles = tuple(s // tm - off for s in seg_bounds)
    cb = col0 // width
    r8 = tm // 8
    last8 = t // 8 - 1
    kern = functools.partial(_dwconv_kernel, seg_tiles=seg_tiles, act=act)
    return pl.pallas_call(
        kern,
        grid=(bsz, n_tiles),
        in_specs=[pl.BlockSpec((None, tm, width), lambda bb, i: (bb, i + off, cb)),
                  pl.BlockSpec((None, 8, width), lambda bb, i: (bb, jnp.maximum((i + off) * r8 - 1, 0), cb)),
                  pl.BlockSpec((None, 8, width),
                               lambda bb, i: (bb, jnp.minimum((i + off + 1) * r8, last8), cb)),
                  pl.BlockSpec((3, width), lambda bb, i: (0, 0)),
                  pl.BlockSpec((1, width), lambda bb, i: (0, 0))],
        out_specs=pl.BlockSpec((None, tm, width), lambda bb, i: (bb, i, 0)),
        out_shape=jax.ShapeDtypeStruct((bsz, n_tiles * tm, width), F32),
        compiler_params=_params("parallel", "parallel"),
        name="dwconv",
    )(proj, proj, proj, w.T, b.reshape(1, -1))


def _ssd_kernel(xbc_ref, dt_ref, bias_ref, alog_ref, hexp_ref, mq_ref, mk_ref, mask_ref, o_ref, st_ref):
    @pl.when(pl.program_id(1) == 0)
    def _():
        st_ref[...] = jnp.zeros_like(st_ref)

    c = xbc_ref.shape[0]
    hpg = MB_HEADS // MB_GROUPS
    gw = hpg * MB_HEAD_DIM
    dt = _softplus(dt_ref[...] + bias_ref[...])
    la = -dt * jnp.exp(alog_ref[...])
    cq = _sel_dot(mq_ref[...], la)
    ck = _sel_dot(mk_ref[...], la)
    cq_t = _dot_nt_sel(la, mq_ref[...])
    hexp = hexp_ref[...]
    dt_x = _dot_sel(dt, hexp)
    eq_x = jnp.exp(_dot_sel(cq, hexp))
    ek_x = jnp.exp(_dot_sel(ck, hexp))
    etot_x = jnp.exp(_dot_sel(jnp.sum(la, axis=0, keepdims=True), hexp))
    xs = xbc_ref[:, 0:MB_INNER] * dt_x
    mask = mask_ref[...]
    outs = []
    for g in range(MB_GROUPS):
        bm = xbc_ref[:, MB_INNER + g * MB_STATE:MB_INNER + (g + 1) * MB_STATE].astype(BF16)
        cm = xbc_ref[:, MB_INNER + MB_BC_W + g * MB_STATE:MB_INNER + MB_BC_W + (g + 1) * MB_STATE].astype(BF16)
        cb = _dot_nt(cm, bm)
        st = st_ref[g]
        gs = slice(g * gw, (g + 1) * gw)
        y_inter = _dot(cm, st.astype(BF16)) * eq_x[:, gs]
        for r in range(hpg):
            hd = g * hpg + r
            diff = cq[:, hd:hd + 1] - cq_t[hd:hd + 1, :]
            w = cb * jnp.exp(jnp.where(mask > 0.0, diff, -jnp.inf))
            ps = slice(hd * MB_HEAD_DIM, (hd + 1) * MB_HEAD_DIM)
            outs.append(_dot(w.astype(BF16), xs[:, ps].astype(BF16))
                        + y_inter[:, r * MB_HEAD_DIM:(r + 1) * MB_HEAD_DIM])
        st_ref[g] = st * etot_x[:, gs] + _dot_tn(bm, (xs[:, gs] * ek_x[:, gs]).astype(BF16))
    o_ref[...] = jnp.concatenate(outs, axis=-1)


def _dot_nt_sel(x, m01):
    hi, mid, lo = _split3(x)
    f = lambda p: lax.dot_general(p, m01, (((0,), (1,)), ((), ())), preferred_element_type=F32)
    return f(hi) + (f(mid) + f(lo))


def _ssd_scan(xbc, proj, dt_col, dt_bias, a_log, n_ctx, reverse):
    bsz, t, _ = xbc.shape
    c = SSD_CHUNK
    n_chunks, col, const, sel, msk = _scan_specs(c, n_ctx, t, reverse)
    mq, mk = sel[0:c, 0:c], sel[c:2 * c, 0:c]
    tri = np.tril(np.ones((c, c), np.float32))
    mask = jnp.asarray(tri[::-1, ::-1].copy() if reverse else tri)
    pad = lambda v: jnp.zeros((1, LANES), F32).at[0, :MB_HEADS].set(v)
    hexp = np.zeros((LANES, MB_INNER), np.float32)
    for hd in range(MB_HEADS):
        hexp[hd, hd * MB_HEAD_DIM:(hd + 1) * MB_HEAD_DIM] = 1.0
    hexp = jnp.asarray(hexp, BF16)
    bias = pad(dt_bias)
    alog = pad(a_log.astype(F32))
    return pl.pallas_call(
        _ssd_kernel,
        grid=(bsz, n_chunks),
        in_specs=[col(xbc.shape[-1], 0), col(LANES, dt_col), const(bias), const(alog), const(hexp),
                  const(mq), const(mk), const(mask)],
        out_specs=col(MB_INNER, 0),
        out_shape=jax.ShapeDtypeStruct((bsz, t, MB_INNER), F32),
        scratch_shapes=[pltpu.VMEM((MB_GROUPS, MB_STATE, MB_INNER // MB_GROUPS), F32)],
        compiler_params=_params("parallel", "arbitrary"),
        name="ssd_scan_rev" if reverse else "ssd_scan_fwd",
    )(xbc, proj, bias, alog, hexp, mq, mk, mask)


def _mix_out_cd_kernel(h_ref, hy_ref, yf_ref, yb_ref, xs_ref, z_ref, dsk_ref, ng_ref, w_ref, gt_ref, o_ref):
    y = (yf_ref[...] + yb_ref[...] + dsk_ref[...] * xs_ref[...]) * _silu(z_ref[...])
    gw = MB_INNER // MB_GROUPS
    ys = [_rms(y[:, g * gw:(g + 1) * gw], ng_ref[:, g * gw:(g + 1) * gw]) for g in range(MB_GROUPS)]
    feat = jnp.concatenate([hy_ref[...]] + ys, axis=-1).astype(BF16)
    o_ref[...] = h_ref[...] + gt_ref[...] * _dot(feat, w_ref[...])


def _mix_out_cd(h, hy, y_ssd, xbc, proj, z_col, d_skip_x, norm_g, w_out, gate, n_ctx):
    bsz, t, d = h.shape
    tm = ROW_TILE
    n_lat = t - n_ctx
    off = n_ctx // tm
    row = lambda width, idx: pl.BlockSpec((None, tm, width), lambda b, i: (b, i + off, idx))
    vec = pl.BlockSpec((1, MB_INNER), lambda b, i: (0, 0))
    return pl.pallas_call(
        _mix_out_cd_kernel,
        grid=(bsz, n_lat // tm),
        in_specs=[row(d, 0), pl.BlockSpec((None, tm, HY_CH), lambda b, i: (b, i, 0)),
                  row(MB_INNER, 0), row(MB_INNER, 0), row(MB_INNER, 0), row(MB_INNER, z_col), vec, vec,
                  pl.BlockSpec(w_out.shape, lambda b, i: (0, 0)),
                  pl.BlockSpec((None, None, 1, d), lambda b, i: (b, 1, 0, 0))],
        out_specs=pl.BlockSpec((None, tm, d), lambda b, i: (b, i, 0)),
        out_shape=jax.ShapeDtypeStruct((bsz, n_lat, d), F32),
        compiler_params=_params("parallel", "parallel"),
        name="mix_out_cd",
    )(h, hy, y_ssd[0], y_ssd[1], xbc, proj, d_skip_x, norm_g.reshape(1, -1), w_out, gate)


def _top2_of4(a, b, c, d):
    hi1, lo1, hi2, lo2 = jnp.maximum(a, b), jnp.minimum(a, b), jnp.maximum(c, d), jnp.minimum(c, d)
    return jnp.maximum(hi1, hi2) + jnp.maximum(jnp.minimum(hi1, hi2), jnp.maximum(lo1, lo2))


def _first_argmax(vals, skip=None):
    idx = None
    for j, vj in enumerate(vals):
        if idx is None and skip is None:
            idx, best = jnp.zeros(vj.shape, jnp.int32), vj
            continue
        if idx is None:
            idx, best = jnp.full(vj.shape, -1, jnp.int32), jnp.full(vj.shape, -jnp.inf, F32)
        take = vj > best
        if skip is not None:
            take = jnp.logical_and(take, skip != j)
        idx = jnp.where(take, j, idx)
        best = jnp.where(take, vj, best)
    return idx, best


def _ffn_pre_kernel(h_ref, g_ref, sh_ref, sc_ref, rw_ref, rb_ref, tri_ref,
                    v_ref, ri_ref, rwt_ref, cnt_ref, carry_ref):
    @pl.when(jnp.logical_and(pl.program_id(0) == 0, pl.program_id(1) == 0))
    def _():
        carry_ref[...] = jnp.zeros_like(carry_ref)

    v = _rms(h_ref[...], g_ref[...]) * (1.0 + sc_ref[...]) + sh_ref[...]
    v_ref[...] = _pack_bf16_pairs(v)
    st = _sigmoid(_dot_f32(v, rw_ref[...])).T[0:N_EXPERTS]
    sel = st + rb_ref[...]
    row = lambda a, e: a[e:e + 1]
    epg = EXPERTS_PER_GROUP
    gscore = [_top2_of4(*[row(sel, g * epg + j) for j in range(epg)]) for g in range(N_GROUPS)]
    best, _ = _first_argmax(gscore)

    def in_best(a, j):
        out = row(a, j)
        for g in range(1, N_GROUPS):
            out = jnp.where(best == g, row(a, g * epg + j), out)
        return out

    vals = [in_best(sel, j) for j in range(epg)]
    raw = [in_best(st, j) for j in range(epg)]
    i1, _ = _first_argmax(vals)
    i2, _ = _first_argmax(vals, skip=i1)
    pick = lambda i: functools.reduce(lambda acc, j: jnp.where(i == j, raw[j], acc), range(1, epg), raw[0])
    w1, w2 = pick(i1), pick(i2)
    wsum = w1 + w2
    e1, e2 = best * epg + i1, best * epg + i2

    experts = lax.broadcasted_iota(jnp.int32, st.shape, 0)
    oh1 = (experts == e1).astype(F32)
    oh2 = (experts == e2).astype(F32)
    cnt = oh1 + oh2
    before = _dot(cnt.astype(BF16), tri_ref[...]) + carry_ref[:, 0:1]
    ri_ref[0:1, :] = e1
    ri_ref[1:2, :] = e2
    ri_ref[2:3, :] = jnp.sum(oh1 * before, axis=0, keepdims=True).astype(jnp.int32)
    ri_ref[3:4, :] = jnp.sum(oh2 * before, axis=0, keepdims=True).astype(jnp.int32)
    ri_ref[4:8, :] = jnp.zeros((4, st.shape[1]), jnp.int32)
    rwt_ref[0:1, :] = w1 / wsum
    rwt_ref[1:2, :] = w2 / wsum
    rwt_ref[2:8, :] = jnp.zeros((6, st.shape[1]), F32)
    carry_ref[...] = carry_ref[...] + jnp.sum(cnt, axis=1, keepdims=True)
    cnt_ref[...] = carry_ref[...]


def _ffn_pre(h, g, shift, scale, router_w_pad, router_b, n_ctx):
    bsz, t, d = h.shape
    tm = ROW_TILE
    seg = lambda b, i: (b, (i * tm >= n_ctx).astype(jnp.int32), 0, 0)
    tri = jnp.asarray(np.triu(np.ones((tm, tm), np.float32), 1), BF16)
    return pl.pallas_call(
        _ffn_pre_kernel,
        grid=(bsz, t // tm),
        in_specs=[pl.BlockSpec((None, tm, d), lambda b, i: (b, i, 0)),
                  pl.BlockSpec((1, d), lambda b, i: (0, 0)),
                  pl.BlockSpec((None, None, 1, d), seg),
                  pl.BlockSpec((None, None, 1, d), seg),
                  pl.BlockSpec((d, LANES), lambda b, i: (0, 0)),
                  pl.BlockSpec((N_EXPERTS, 1), lambda b, i: (0, 0)),
                  pl.BlockSpec((tm, tm), lambda b, i: (0, 0))],
        out_specs=[pl.BlockSpec((None, tm, d // 2), lambda b, i: (b, i, 0)),
                   pl.BlockSpec((None, 8, tm), lambda b, i: (b, 0, i)),
                   pl.BlockSpec((None, 8, tm), lambda b, i: (b, 0, i)),
                   pl.BlockSpec((N_EXPERTS, LANES), lambda b, i: (0, 0))],
        out_shape=[jax.ShapeDtypeStruct((bsz, t, d // 2), jnp.uint32),
                   jax.ShapeDtypeStruct((bsz, 8, t), jnp.int32),
                   jax.ShapeDtypeStruct((bsz, 8, t), F32),
                   jax.ShapeDtypeStruct((N_EXPERTS, LANES), F32)],
        scratch_shapes=[pltpu.VMEM((N_EXPERTS, LANES), F32)],
        compiler_params=_params("arbitrary", "arbitrary"),
        name="ffn_pre",
    )(h, g.reshape(1, d), shift, scale, router_w_pad, router_b.reshape(N_EXPERTS, 1), tri)


def _experts_kernel(be_ref, nb_ref, x_ref, wg_ref, wu_ref, wd_ref, o_ref, wg_s, wu_s, wd_s):
    i = pl.program_id(0)
    prev = be_ref[jnp.maximum(i - 1, 0)]
    changed = jnp.logical_or(i == 0, be_ref[i] != prev)

    @pl.when(changed)
    def _():
        wg_s[...] = wg_ref[...].astype(BF16)
        wu_s[...] = wu_ref[...].astype(BF16)
        wd_s[...] = wd_ref[...].astype(BF16)

    @pl.when(i < nb_ref[0])
    def _():
        x = _unpack_bf16_pairs(x_ref[...]).astype(BF16)
        hid = _silu(_dot(x, wg_s[...])) * _dot(x, wu_s[...])
        o_ref[...] = _pack_bf16_pairs(_dot(hid.astype(BF16), wd_s[...]))

    @pl.when(i >= nb_ref[0])
    def _():
        o_ref[...] = jnp.zeros_like(o_ref)


def _experts(xb, block_e, n_used, layer, w_gate, w_up, w_down):
    n_slots = xb.shape[0]
    n_blocks = n_slots // MOE_BLOCK
    d, de = w_gate.shape[-2:]
    wspec = lambda shape: pl.BlockSpec((None, None) + shape, lambda i, be, nb: (layer, be[i], 0, 0))
    return pl.pallas_call(
        _experts_kernel,
        grid_spec=pltpu.PrefetchScalarGridSpec(
            num_scalar_prefetch=2,
            grid=(n_blocks,),
            in_specs=[pl.BlockSpec((MOE_BLOCK, d // 2), lambda i, be, nb: (i, 0)),
                      wspec((d, de)), wspec((d, de)), wspec((de, d))],
            out_specs=pl.BlockSpec((MOE_BLOCK, d // 2), lambda i, be, nb: (i, 0)),
            scratch_shapes=[pltpu.VMEM((d, de), BF16), pltpu.VMEM((d, de), BF16), pltpu.VMEM((de, d), BF16)]),
        out_shape=jax.ShapeDtypeStruct((n_slots, d // 2), jnp.uint32),
        compiler_params=_params("arbitrary"),
        name="moe_experts",
    )(block_e, n_used, xb, w_gate, w_up, w_down)


def _ffn_post_kernel(h_ref, y0_ref, y1_ref, w_ref, gt_ref, g_ref, o_ref, *, final):
    w = w_ref[...]
    y = w[:, 0:1] * _unpack_bf16_pairs(y0_ref[...]) + w[:, 1:2] * _unpack_bf16_pairs(y1_ref[...])
    out = h_ref[...] + gt_ref[...] * y
    o_ref[...] = _rms(out, g_ref[...]) if final else out


def _ffn_post(h, y, w, gate, n_ctx, final_g=None):
    bsz, t, d = h.shape
    tm = ROW_TILE
    seg = lambda b, i: (b, (i * tm >= n_ctx).astype(jnp.int32), 0, 0)
    row = lambda width: pl.BlockSpec((None, tm, width), lambda b, i: (b, i, 0))
    choice = lambda kk: pl.BlockSpec((None, None, tm, d // 2), lambda b, i: (kk, b, i, 0))
    final = final_g is not None
    g = final_g if final else jnp.ones((d,), F32)
    return pl.pallas_call(
        functools.partial(_ffn_post_kernel, final=final),
        grid=(bsz, t // tm),
        in_specs=[row(d), choice(0), choice(1), row(LANES), pl.BlockSpec((None, None, 1, d), seg),
                  pl.BlockSpec((1, d), lambda b, i: (0, 0))],
        out_specs=row(d),
        out_shape=jax.ShapeDtypeStruct((bsz, t, d), F32),
        compiler_params=_params("parallel", "parallel"),
        name="ffn_post",
    )(h, y, y, w, gate, g.reshape(1, d))


def _slot_layout(n, ri, counts):
    e = jnp.swapaxes(ri[:, 0:2], 0, 1).reshape(TOP_K, n)
    rank = jnp.swapaxes(ri[:, 2:4], 0, 1).reshape(TOP_K, n)
    padded = (counts + MOE_BLOCK - 1) // MOE_BLOCK * MOE_BLOCK
    pend = jnp.cumsum(padded)
    pstart = pend - padded
    experts = jnp.arange(N_EXPERTS, dtype=jnp.int32)
    dest = rank + jnp.sum(jnp.where(e[..., None] == experts, pstart, 0), axis=-1)
    n_slots = (n * TOP_K + MOE_BLOCK - 1) // MOE_BLOCK * MOE_BLOCK + N_EXPERTS * MOE_BLOCK
    n_blocks = n_slots // MOE_BLOCK
    blk0 = jnp.arange(n_blocks, dtype=jnp.int32)[:, None] * MOE_BLOCK
    block_e = jnp.minimum(jnp.sum((pend[None, :] <= blk0).astype(jnp.int32), axis=-1), N_EXPERTS - 1)
    n_used = (pend[-1] // MOE_BLOCK).astype(jnp.int32).reshape(1)
    return dest, n_slots, block_e.astype(jnp.int32), n_used


SC_CORES, SC_SUBCORES = 2, 16
SC_WINDOW = 32


def _gather_rows(table, idx):
    n_rows, d = idx.shape[0], table.shape[1]
    workers = SC_CORES * SC_SUBCORES
    per_worker = n_rows // workers
    assert per_worker * workers == n_rows and per_worker % SC_WINDOW == 0
    mesh = plsc.VectorSubcoreMesh(core_axis_name="c", subcore_axis_name="s")

    @functools.partial(
        pl.kernel, mesh=mesh,
        out_type=jax.ShapeDtypeStruct((n_rows, d), table.dtype),
        scratch_types=[pltpu.VMEM((SC_WINDOW,), jnp.int32), pltpu.VMEM((SC_WINDOW,), jnp.int32),
                       pltpu.VMEM((SC_WINDOW, d), table.dtype), pltpu.VMEM((SC_WINDOW, d), table.dtype),
                       pltpu.SemaphoreType.DMA, pltpu.SemaphoreType.DMA],
    )
    def gather_kernel(table_hbm, idx_hbm, out_hbm, idx0, idx1, rows0, rows1, sem0, sem1):
        base = (lax.axis_index("s") * SC_CORES + lax.axis_index("c")) * per_worker
        n_win = per_worker // SC_WINDOW
        slots = ((idx0, rows0, sem0), (idx1, rows1, sem1))
        window = lambda j: pl.ds(pl.multiple_of(base + j * SC_WINDOW, 8), SC_WINDOW)

        def start(j, slot):
            idx_v, rows_v, sem = slots[slot]
            pltpu.sync_copy(idx_hbm.at[window(j)], idx_v)
            pltpu.async_copy(table_hbm.at[idx_v], rows_v, sem)

        def finish(j, slot):
            idx_v, rows_v, sem = slots[slot]
            pltpu.make_async_copy(table_hbm.at[idx_v], rows_v, sem).wait()
            pltpu.sync_copy(rows_v, out_hbm.at[window(j)])

        start(0, 0)

        @pl.loop(0, n_win, step=2)
        def _(j):
            @pl.when(j + 1 < n_win)
            def _():
                start(j + 1, 1)

            finish(j, 0)

            @pl.when(j + 2 < n_win)
            def _():
                start(j + 2, 0)

            @pl.when(j + 1 < n_win)
            def _():
                finish(j + 1, 1)

    return gather_kernel(table, idx)


SC_SCATTER_WINDOW = 16


def _scatter_rows(src, dest, n_slots):
    n, d = src.shape
    workers = SC_CORES * SC_SUBCORES
    per_worker = n // workers
    win = SC_SCATTER_WINDOW
    assert per_worker * workers == n and per_worker % win == 0 and dest.shape == (TOP_K, n)
    mesh = plsc.VectorSubcoreMesh(core_axis_name="c", subcore_axis_name="s")

    @functools.partial(
        pl.kernel, mesh=mesh,
        out_type=jax.ShapeDtypeStruct((n_slots, d), src.dtype),
        scratch_types=[pltpu.VMEM((win,), jnp.int32), pltpu.VMEM((win,), jnp.int32),
                       pltpu.VMEM((win, d), src.dtype), pltpu.SemaphoreType.DMA, pltpu.SemaphoreType.DMA],
    )
    def scatter_kernel(src_hbm, dest_hbm, out_hbm, idx0, idx1, rows_v, sem0, sem1):
        base = (lax.axis_index("s") * SC_CORES + lax.axis_index("c")) * per_worker

        @pl.loop(0, per_worker // win)
        def _(j):
            rows = pl.ds(pl.multiple_of(base + j * win, 8), win)
            pltpu.sync_copy(src_hbm.at[rows], rows_v)
            pltpu.sync_copy(dest_hbm.at[0, rows], idx0)
            pltpu.sync_copy(dest_hbm.at[1, rows], idx1)
            first = pltpu.async_copy(rows_v, out_hbm.at[idx0], sem0)
            second = pltpu.async_copy(rows_v, out_hbm.at[idx1], sem1)
            first.wait()
            second.wait()

    return scatter_kernel(src, dest)


def _alongside(gather, idx, side_fn, side_in):
    idx, side_in = lax.optimization_barrier((idx, side_in))
    return lax.optimization_barrier((gather(idx), side_fn(side_in)))


def _moe(h, g, shift, scale, gate, router_w_pad, router_b, layer, w_gate, w_up, w_down, n_ctx,
         final_g=None, side=None):
    bsz, t, d = h.shape
    n = bsz * t
    v, ri, rwt, counts = _ffn_pre(h, g, shift, scale, router_w_pad, router_b, n_ctx)
    dest, n_slots, block_e, n_used = _slot_layout(n, ri, counts[:, 0].astype(jnp.int32))
    w = jnp.swapaxes(rwt[:, 0:2], 1, 2).reshape(n, TOP_K)
    dispatch = lambda idx: _scatter_rows(v.reshape(n, d // 2), idx, n_slots)
    if side is None:
        xb = dispatch(dest)
    else:
        xb, side_a = _alongside(dispatch, dest, *side[0])
    yb = _experts(xb, block_e, n_used, layer, w_gate, w_up, w_down)
    combine = lambda idx: _gather_rows(yb, idx)
    dest_flat = dest.reshape(-1)
    if side is None:
        y, side_b = combine(dest_flat), None
    else:
        y, side_b = _alongside(combine, dest_flat, side[1], side_a)
    wpad = jnp.zeros((n, LANES), F32).at[:, :TOP_K].set(w).reshape(bsz, t, LANES)
    out = _ffn_post(h, y.reshape(TOP_K, bsz, t, d // 2), wpad, gate, n_ctx, final_g)
    return out if side is None else (out, side_b)


DFT_STEP = 8


def _dft_tables(n):
    r, *mats = _dft_tables_np(n)
    return (r,) + tuple(jnp.asarray(a).astype(BF16) for a in mats)


@functools.lru_cache(maxsize=None)
def _dft_tables_np(n):
    size = 2 * n
    r = int(round(math.sqrt(size)))
    assert r * r == size and r % DFT_STEP == 0
    p1 = np.arange(r // 2)[None, None, :]
    p2 = np.arange(r)[:, None, None]
    k1 = np.arange(r)[None, :, None]
    ang = 2.0 * np.pi * (((r * p1 + p2) * k1) % size) / size
    g_re, g_im = np.cos(ang), -np.sin(ang)
    g_in = np.concatenate([g_re, g_im], axis=1)
    g_out = np.concatenate([np.swapaxes(g_re, 1, 2), np.swapaxes(g_im, 1, 2)], axis=2) / size
    a2 = 2.0 * np.pi * ((np.arange(r)[:, None] * np.arange(r)[None, :]) % r) / r
    f_re, f_im = np.cos(a2), -np.sin(a2)
    f_fwd = np.block([[f_re, -f_im], [f_im, f_re]])
    f_inv = np.block([[f_re, f_im], [-f_im, f_re]])
    p1f = np.arange(r)[None, None, :]
    angf = 2.0 * np.pi * (((r * p1f + p2) * k1) % size) / size
    g_full = np.concatenate([np.cos(angf), -np.sin(angf)], axis=1)
    return (r,) + tuple(a.astype(np.float32) for a in (g_in, g_out, f_fwd, f_inv, g_full))


def _hy_filter_kernel(z_ref, w1_ref, b1_ref, w2_ref, b2_ref, w3_ref, fr_ref, rates_ref, o_ref, *, half_tiles):
    i = pl.program_id(0)
    z = z_ref[...]
    hid = jnp.sin(fr_ref[...] * (_dot_f32(z, w1_ref[...]) + b1_ref[...]))
    hid = jnp.sin(fr_ref[...] * (_dot_f32(hid, w2_ref[...]) + b2_ref[...]))
    filt = _dot_f32(hid, w3_ref[...])
    decay = jnp.exp(-z[:, 0:1] * rates_ref[...])
    row0 = lax.broadcasted_iota(jnp.int32, decay.shape, 0) == 0
    for o in range(o_ref.shape[0]):
        fwd = filt[:, (2 * o) * HY_CH:(2 * o + 1) * HY_CH] * decay
        bwd = filt[:, (2 * o + 1) * HY_CH:(2 * o + 2) * HY_CH] * decay
        first = jnp.where(jnp.logical_and(row0, i == 0), fwd + bwd, fwd)
        second = jnp.where(jnp.logical_and(row0, i == half_tiles), 0.0, bwd)
        o_ref[o] = jnp.where(i < half_tiles, first, second)


def _hy_kernels(n, w1, b1, w2, b2, w3, freq):
    pos = np.arange(2 * n)
    pos = np.where(pos < n, pos, 2 * n - pos).astype(np.float32)
    t = jnp.asarray(pos / np.float32(n - 1))[:, None]
    bands = jnp.linspace(1e-4, HY_BANDS - 1, HY_BANDS, dtype=F32)
    ang = (2.0 * math.pi / n) * jnp.asarray(pos)[:, None] * bands
    z = jnp.concatenate([t, jnp.cos(ang), -jnp.sin(ang)], axis=-1)
    z = jnp.pad(z, ((0, 0), (0, LANES - z.shape[1])))
    w1p = jnp.pad(w1, ((0, LANES - w1.shape[0]), (0, 0)))
    rates = jnp.abs(jnp.linspace(HY_MIN_DECAY, HY_MAX_DECAY, HY_CH, dtype=F32)).reshape(1, HY_CH)
    tm = ROW_TILE
    full = lambda a: pl.BlockSpec(a.shape, lambda i: (0,) * a.ndim)
    args = (z, w1p, b1.reshape(1, -1), w2, b2.reshape(1, -1), w3, freq.reshape(1, -1), rates)
    return pl.pallas_call(
        functools.partial(_hy_filter_kernel, half_tiles=n // tm),
        grid=(2 * n // tm,),
        in_specs=[pl.BlockSpec((tm, LANES), lambda i: (i, 0))] + [full(a) for a in args[1:]],
        out_specs=pl.BlockSpec((HY_ORDER, tm, HY_CH), lambda i: (0, i, 0)),
        out_shape=jax.ShapeDtypeStruct((HY_ORDER, 2 * n, HY_CH), F32),
        compiler_params=_params("parallel"),
        name="hy_kernels",
    )(*args)


def _dft_in_kernel(x_ref, g_ref, a_ref):
    for j in range(DFT_STEP):
        a_ref[j] = _dot(g_ref[j], x_ref[:, j, :].astype(BF16))


def _dft_in(x4, col, g_in):
    bx, rh, r, _ = x4.shape
    c = HY_CH
    return pl.pallas_call(
        _dft_in_kernel,
        grid=(bx, r // DFT_STEP),
        in_specs=[pl.BlockSpec((None, rh, DFT_STEP, c), lambda b, i: (b, 0, i, col)),
                  pl.BlockSpec((DFT_STEP, 2 * r, rh), lambda b, i: (i, 0, 0))],
        out_specs=pl.BlockSpec((None, DFT_STEP, 2 * r, c), lambda b, i: (b, i, 0, 0)),
        out_shape=jax.ShapeDtypeStruct((bx, r, 2 * r, c), F32),
        compiler_params=_params("parallel", "parallel"),
        name="dft_in",
    )(x4, g_in)


def _stage2(a_ref, b, j, f_ref):
    a = jnp.concatenate([a_ref[b, :, 0, j, :], a_ref[b, :, 1, j, :]], axis=0)
    return _dot(f_ref[...], a.astype(BF16))


def _dft_filt_kernel(a_ref, f_ref, k_ref):
    for j in range(DFT_STEP):
        k_ref[j] = _stage2(a_ref, 0, j, f_ref)


def _dft_filt(a, f_fwd):
    nq, r, _, c = a.shape
    a5 = a.reshape(nq, r, 2, r, c)
    return pl.pallas_call(
        _dft_filt_kernel,
        grid=(nq, r // DFT_STEP),
        in_specs=[pl.BlockSpec((1, r, 2, DFT_STEP, c), lambda o, i: (o, 0, 0, i, 0)),
                  pl.BlockSpec(f_fwd.shape, lambda o, i: (0, 0))],
        out_specs=pl.BlockSpec((None, DFT_STEP, 2 * r, c), lambda o, i: (o, i, 0, 0)),
        out_shape=jax.ShapeDtypeStruct((nq, r, 2 * r, c), F32),
        compiler_params=_params("parallel", "parallel"),
        name="dft_filt",
    )(a5, f_fwd)


def _dft_mid_kernel(a_ref, k_ref, ff_ref, fi_ref, b_ref):
    r = ff_ref.shape[0] // 2
    for b in range(a_ref.shape[0]):
        for j in range(DFT_STEP):
            s = _stage2(a_ref, b, j, ff_ref)
            sr, si = s[0:r], s[r:2 * r]
            kr, ki = k_ref[j, 0:r, :], k_ref[j, r:2 * r, :]
            p = jnp.concatenate([sr * kr - si * ki, sr * ki + si * kr], axis=0).astype(BF16)
            b_ref[b, j] = _dot(fi_ref[...], p)


def _dft_mid(a, kspec, order, f_fwd, f_inv):
    bsz, r, _, c = a.shape
    a5 = a.reshape(bsz, r, 2, r, c)
    return pl.pallas_call(
        _dft_mid_kernel,
        grid=(r // DFT_STEP,),
        in_specs=[pl.BlockSpec((bsz, r, 2, DFT_STEP, c), lambda i: (0, 0, 0, i, 0)),
                  pl.BlockSpec((None, DFT_STEP, 2 * r, c), lambda i: (order, i, 0, 0)),
                  pl.BlockSpec(f_fwd.shape, lambda i: (0, 0)),
                  pl.BlockSpec(f_inv.shape, lambda i: (0, 0))],
        out_specs=pl.BlockSpec((bsz, DFT_STEP, 2 * r, c), lambda i: (0, i, 0, 0)),
        out_shape=jax.ShapeDtypeStruct((bsz, r, 2 * r, c), F32),
        compiler_params=_params("parallel"),
        name="dft_mid",
    )(a5, kspec, f_fwd, f_inv)


def _dft_out_kernel(b_ref, g_ref, u_ref, x_ref, bias_ref, o_ref):
    for j in range(DFT_STEP):
        rhs = jnp.concatenate([b_ref[:, 0, j, :], b_ref[:, 1, j, :]], axis=0).astype(BF16)
        y = _dot(g_ref[j], rhs)
        o_ref[:, j, :] = x_ref[:, j, :] * (y + u_ref[:, j, :] * bias_ref[...])


def _dft_out(bm, g_out, u4, u_col, x4, x_col, bias):
    bsz, r, _, c = bm.shape
    rh = r // 2
    b5 = bm.reshape(bsz, r, 2, r, c)
    seq = lambda col: pl.BlockSpec((None, rh, DFT_STEP, c), lambda b, i: (b, 0, i, col))
    return pl.pallas_call(
        _dft_out_kernel,
        grid=(bsz, r // DFT_STEP),
        in_specs=[pl.BlockSpec((None, r, 2, DFT_STEP, c), lambda b, i: (b, 0, 0, i, 0)),
                  pl.BlockSpec((DFT_STEP, rh, 2 * r), lambda b, i: (i, 0, 0)),
                  seq(u_col), seq(x_col), pl.BlockSpec((1, c), lambda b, i: (0, 0))],
        out_specs=seq(0),
        out_shape=jax.ShapeDtypeStruct((bsz, rh, r, c), F32),
        compiler_params=_params("parallel", "parallel"),
        name="dft_out",
    )(b5, g_out, u4, x4, bias.reshape(1, c))


def _hyena_filter_stage1(n, filter_params):
    r, g_full = _dft_tables(n)[0], _dft_tables(n)[5]
    kern = _hy_kernels(n, *filter_params)
    return _dft_in(kern.reshape(-1, r, r, HY_CH), 0, g_full)


def _hyena_filter_spectra(n, stage1):
    return _dft_filt(stage1, _dft_tables(n)[3])


def _hyena(hy_in, kspec, conv_bias):
    bsz, n, _ = hy_in.shape
    r, g_in, g_out, f_fwd, f_inv, _ = _dft_tables(n)
    seq4 = hy_in.reshape(bsz, r // 2, r, 3 * HY_CH)
    zz = _dft_out(_dft_mid(_dft_in(seq4, 0, g_in), kspec, 0, f_fwd, f_inv), g_out,
                  seq4, 0, seq4, 1, conv_bias[0])
    out = _dft_out(_dft_mid(_dft_in(zz, 0, g_in), kspec, 1, f_fwd, f_inv), g_out,
                   zz, 0, seq4, 2, conv_bias[1])
    return out.reshape(bsz, n, HY_CH)


CD_HY, CD_Z, CD_XBC, CD_DT = 0, 1536, 2048, 3072
CD_PAD_COLS = CD_DT + 2 * LANES


def _reorder_ab(w):
    gq, gk, gv, gg, lr_f, lr_b, hq, hf_f, hf_b, hi, hg = jnp.split(
        w, np.cumsum([256, 256, 512, 512, 16, 16, 512, 512, 512, 512, 512])[:-1].tolist(), axis=-1)
    pad = jnp.zeros((w.shape[0], AB_PAD_COLS - AB_LR - 2 * GLA_LOW_RANK), w.dtype)
    return jnp.concatenate([gq, gk, gv, gg, hq, hf_f, hf_b, hi, hg, lr_f, lr_b, pad], axis=-1)


def _reorder_cd(w):
    hy, z, xbc, dt_f, dt_b = jnp.split(w, np.cumsum([1536, 512, 1024, 8, 8])[:-1].tolist(), axis=-1)
    pad = jnp.zeros((w.shape[0], LANES - MB_HEADS), w.dtype)
    return jnp.concatenate([hy, z, xbc, dt_f, pad, dt_b, pad], axis=-1)


def kernel(x, c, ctx, c_ctx, ada_w, ada_b, norm_mix_g, norm_ffn_g, norm_out_g, ab_w_in, ab_w_out, gla_gate_w, gla_gate_b, gla_norm_g, hg_lb, hg_norm_g, cd_w_in, cd_w_out, hy_short_w, hy_short_b, hy_w1, hy_b1, hy_w2, hy_b2, hy_w3, hy_freq, hy_bias, mb_conv_w, mb_conv_b, mb_dt_bias, mb_a_log, mb_d, mb_norm_g, router_w, router_b, moe_w_gate, moe_w_up, moe_w_down):
    bsz, n_lat, d = x.shape
    n_ctx = ctx.shape[1]
    t = n_ctx + n_lat
    assert ada_w.shape[0] == 2 and ab_w_in.shape[0] == 1 and cd_w_in.shape[0] == 1

    cond = jnp.zeros((8, d), F32).at[:bsz].set(c).at[bsz].set(c_ctx)
    m = _adaln(cond, ada_w, ada_b)

    def mods(layer):
        lat = m[layer, :bsz].reshape(bsz, 6, d)
        cx = jnp.broadcast_to(m[layer, bsz].reshape(1, 6, d), (bsz, 6, d))
        both = jnp.stack([cx, lat], axis=1)
        return [both[:, :, j][:, :, None, :] for j in range(6)]

    lb_all = jnp.cumsum(jax.nn.softmax(hg_lb.astype(F32), axis=1), axis=1)
    router_w_pad = jnp.zeros((d, LANES), F32).at[:, :N_EXPERTS].set(router_w)
    h = jnp.concatenate([ctx, x], axis=1)

    sh_m, sc_m, gt_m, sh_f, sc_f, gt_f = mods(0)
    proj = _norm_proj(h, norm_mix_g[0], sh_m, sc_m, _reorder_ab(ab_w_in[0]).astype(BF16), n_ctx)
    gwp = [jnp.zeros((LANES, GLA_KEY_W), F32).at[GLA_LOW_RANK * dd:GLA_LOW_RANK * (dd + 1)].set(gla_gate_w[0, dd])
           for dd in range(2)]
    o_gla = _gla_scan(proj, gwp, [gla_gate_b[0, dd].reshape(1, -1) for dd in range(2)], n_ctx)
    o_hg = _hgrn_scan(proj, [lb_all[dd, 0].reshape(1, -1) for dd in range(2)], n_ctx)
    h = _mix_out_ab(h, o_gla, o_hg, proj, gla_norm_g[0], hg_norm_g[0], ab_w_out[0].astype(BF16), gt_m, n_ctx)
    filter_params = (hy_w1[0], hy_b1[0], hy_w2[0], hy_b2[0], hy_w3[0], hy_freq[0])
    side = ((functools.partial(_hyena_filter_stage1, n_lat), filter_params),
            functools.partial(_hyena_filter_spectra, n_lat))
    h, kspec = _moe(h, norm_ffn_g[0], sh_f, sc_f, gt_f, router_w_pad, router_b,
                    0, moe_w_gate, moe_w_up, moe_w_down, n_ctx, side=side)

    sh_m, sc_m, gt_m, sh_f, sc_f, gt_f = mods(1)
    proj = _norm_proj(h, norm_mix_g[1], sh_m, sc_m, _reorder_cd(cd_w_in[0]).astype(BF16), n_ctx)
    hy_in = _dwconv(proj, CD_HY, 3 * HY_CH, hy_short_w[0], hy_short_b[0], (n_ctx, t), act=False)
    hy = _hyena(hy_in, kspec, hy_bias[0])
    xbc = _dwconv(proj, CD_XBC, MB_INNER + 2 * MB_BC_W, mb_conv_w[0], mb_conv_b[0], (0, n_ctx, t), act=True)
    y_ssd = [_ssd_scan(xbc, proj, CD_DT // LANES + dd, mb_dt_bias[0, dd], mb_a_log[0, dd], n_ctx, dd == 1)
             for dd in range(2)]
    d_skip_x = jnp.repeat(mb_d[0], MB_HEAD_DIM).reshape(1, MB_INNER)
    h = _mix_out_cd(h, hy, y_ssd, xbc, proj, CD_Z // MB_INNER, d_skip_x, mb_norm_g[0],
                    cd_w_out[0].astype(BF16), gt_m, n_ctx)
    return _moe(h, norm_ffn_g[1], sh_f, sc_f, gt_f, router_w_pad, router_b,
                1, moe_w_gate, moe_w_up, moe_w_down, 0, final_g=norm_out_g)
```

```python
import functools
import math

import numpy as np
import jax
import jax.numpy as jnp
from jax import lax
from jax.experimental import pallas as pl
from jax.experimental.pallas import tpu as pltpu
from jax.experimental.pallas import tpu_sc as plsc

NORM_EPS = 1e-6
GLA_HEADS, GLA_DK, GLA_DV, GLA_LOW_RANK, GLA_TAU = 4, 64, 128, 16, 16.0
GLA_KEY_W, GLA_VAL_W = GLA_HEADS * GLA_DK, GLA_HEADS * GLA_DV
HG_HEADS, HG_EXPAND, HG_DV = 4, 128, 128
HG_KEY_W, HG_VAL_W = HG_HEADS * HG_EXPAND, HG_HEADS * HG_DV
HY_CH, HY_ORDER, HY_SHORT, HY_BANDS, HY_FILT_HID = 512, 2, 3, 16, 64
HY_MIN_DECAY = math.log(1e-2) / 1.5
HY_MAX_DECAY = math.log(1e-2) / 0.3
MB_HEADS, MB_HEAD_DIM, MB_GROUPS, MB_STATE = 8, 64, 2, 128
MB_INNER = MB_HEADS * MB_HEAD_DIM
MB_BC_W = MB_GROUPS * MB_STATE
N_EXPERTS, N_GROUPS, TOP_K, MOE_BLOCK = 16, 4, 2, 256
EXPERTS_PER_GROUP = N_EXPERTS // N_GROUPS

LANES = 128
SCAN_CHUNK = 64
SCAN_BLOCK = 128
SSD_CHUNK = 128
ROW_TILE = 256
VMEM_LIMIT = 56 * 1024 * 1024

BF16 = jnp.bfloat16
F32 = jnp.float32


def _params(*sem):
    return pltpu.CompilerParams(dimension_semantics=sem, vmem_limit_bytes=VMEM_LIMIT)


def _split3(x):
    hi = x.astype(BF16)
    r1 = x - hi.astype(F32)
    mid = r1.astype(BF16)
    lo = (r1 - mid.astype(F32)).astype(BF16)
    return hi, mid, lo


def _dot(a, b):
    return jnp.dot(a, b, preferred_element_type=F32)


def _dot_nt(a, b):
    return lax.dot_general(a, b, (((1,), (1,)), ((), ())), preferred_element_type=F32)


def _dot_tn(a, b):
    return lax.dot_general(a, b, (((0,), (0,)), ((), ())), preferred_element_type=F32)


def _sel_dot(m01, x):
    hi, mid, lo = _split3(x)
    return _dot(m01, hi) + (_dot(m01, mid) + _dot(m01, lo))


def _dot_sel(x, m01):
    hi, mid, lo = _split3(x)
    return _dot(hi, m01) + (_dot(mid, m01) + _dot(lo, m01))


def _dot_f32(a, b):
    ah = a.astype(BF16)
    al = (a - ah.astype(F32)).astype(BF16)
    bh = b.astype(BF16)
    bl = (b - bh.astype(F32)).astype(BF16)
    return _dot(ah, bh) + (_dot(ah, bl) + _dot(al, bh))


def _silu(x):
    return x * (1.0 / (1.0 + jnp.exp(-x)))


def _sigmoid(x):
    return 1.0 / (1.0 + jnp.exp(-x))


def _softplus(x):
    return jnp.maximum(x, 0.0) + jnp.log(1.0 + jnp.exp(-jnp.abs(x)))


def _pack_bf16_pairs(x):
    bits = lax.bitcast_convert_type(x.astype(BF16).astype(F32), jnp.uint32)
    half = x.shape[1] // 2
    return bits[:, :half] | (bits[:, half:] >> 16)


def _unpack_bf16_pairs(p):
    hi = lax.bitcast_convert_type(p & jnp.uint32(0xFFFF0000), F32)
    lo = lax.bitcast_convert_type(p << 16, F32)
    return jnp.concatenate([hi, lo], axis=1)


def _rms(x, g):
    return x * lax.rsqrt(jnp.mean(x * x, axis=-1, keepdims=True) + NORM_EPS) * g


def _adaln_kernel(c_ref, w_ref, b_ref, o_ref):
    o_ref[...] = _dot_f32(_silu(c_ref[...]), w_ref[...]) + b_ref[...]


def _adaln(cond, w, b):
    n_l, d, n6 = w.shape
    tn = 1536
    return pl.pallas_call(
        _adaln_kernel,
        grid=(n_l, n6 // tn),
        in_specs=[pl.BlockSpec((8, d), lambda l, j: (0, 0)),
                  pl.BlockSpec((None, d, tn), lambda l, j: (l, 0, j)),
                  pl.BlockSpec((None, 1, tn), lambda l, j: (l, 0, j))],
        out_specs=pl.BlockSpec((None, 8, tn), lambda l, j: (l, 0, j)),
        out_shape=jax.ShapeDtypeStruct((n_l, 8, n6), F32),
        compiler_params=_params("parallel", "parallel"),
        name="adaln",
    )(cond, w, b.reshape(n_l, 1, n6))


def _norm_proj_kernel(h_ref, g_ref, sh_ref, sc_ref, w_ref, o_ref):
    u = _rms(h_ref[...], g_ref[...]) * (1.0 + sc_ref[...]) + sh_ref[...]
    o_ref[...] = _dot(u.astype(BF16), w_ref[...])


def _norm_proj(h, g, shift, scale, w, n_ctx):
    bsz, t, d = h.shape
    n = w.shape[1]
    tm = ROW_TILE
    seg = lambda b, i: (b, (i * tm >= n_ctx).astype(jnp.int32), 0, 0)
    return pl.pallas_call(
        _norm_proj_kernel,
        grid=(bsz, t // tm),
        in_specs=[pl.BlockSpec((None, tm, d), lambda b, i: (b, i, 0)),
                  pl.BlockSpec((1, d), lambda b, i: (0, 0)),
                  pl.BlockSpec((None, None, 1, d), seg),
                  pl.BlockSpec((None, None, 1, d), seg),
                  pl.BlockSpec((d, n), lambda b, i: (0, 0))],
        out_specs=pl.BlockSpec((None, tm, n), lambda b, i: (b, i, 0)),
        out_shape=jax.ShapeDtypeStruct((bsz, t, n), F32),
        compiler_params=_params("parallel", "parallel"),
        name="norm_proj",
    )(h, g.reshape(1, d), shift, scale, w)


def _scan_constants(c, reverse):
    t = np.arange(c)[:, None]
    u = np.arange(c)[None, :]
    sels = [u <= t, u > t]
    masks = []
    m = c // 2
    while m >= 1:
        blk = t // (2 * m)
        upper_t = (t % (2 * m)) >= m
        r = blk * (2 * m) + m - 1
        s_blk = u // (2 * m)
        upper_s = (u % (2 * m)) >= m
        sels.append((upper_t & (u > r) & (u <= t)) | ((~upper_t) & (u > t) & (u <= r)))
        masks.append((blk == s_blk) & upper_t & (~upper_s))
        m //= 2
    masks.append(t == u)
    sel = np.stack(sels).astype(np.float32)
    msk = np.stack(masks).astype(np.float32)
    if reverse:
        sel = sel[:, ::-1, ::-1]
        msk = msk[:, ::-1, ::-1]
    return np.ascontiguousarray(sel.reshape(-1, c)), np.ascontiguousarray(msk)


def _chunk_order(i, n_ctx_chunks, n_chunks, reverse):
    if not reverse:
        return i
    return jnp.where(i < n_ctx_chunks, n_ctx_chunks - 1 - i, n_chunks - 1 - (i - n_ctx_chunks))


GROUP_KEYS = 256


def _decay_chunk(q, k, v, la, consts, st_ref, heads, dk, dv):
    sel_ref, mask_ref, hm_ref, hmb_ref, vm_ref = consts
    c = q.shape[0]
    n_lvl = mask_ref.shape[0] - 1
    hpg = GROUP_KEYS // dk
    cs = _dot(sel_ref[...], jnp.concatenate(_split3(la), axis=0))
    e_q = jnp.exp(cs[0:c])
    e_k = jnp.exp(cs[c:2 * c])
    e_tot = jnp.exp(jnp.sum(la, axis=0, keepdims=True))
    vb = v.astype(BF16)
    outs = []
    for g in range(heads // hpg):
        ks = slice(g * GROUP_KEYS, (g + 1) * GROUP_KEYS)
        vs = slice(g * hpg * dv, (g + 1) * hpg * dv)
        qg, kg = q[:, ks], k[:, ks]
        key_stack = lambda x: jnp.concatenate([x.astype(BF16) * hmb_ref[h] for h in range(hpg)], axis=0)
        att = mask_ref[n_lvl] * _dot_nt(qg.astype(BF16), key_stack(kg))
        for l in range(n_lvl):
            e = jnp.exp(cs[(2 + l) * c:(3 + l) * c, ks])
            att = att + mask_ref[l] * _dot_nt((qg * e).astype(BF16), key_stack(kg * e))
        v_blocks = jnp.concatenate([vb[:, vs] * vm_ref[h] for h in range(hpg)], axis=0)
        intra = _dot(att.astype(BF16), v_blocks)
        st = st_ref[g]
        q_stack = jnp.concatenate([(qg * e_q[:, ks]) * hm_ref[h] for h in range(hpg)], axis=0)
        inter = _dot_nt(q_stack.astype(BF16), st.astype(BF16))
        upd = _dot_tn(vb[:, vs], (kg * e_k[:, ks]).astype(BF16))
        new = st * e_tot[:, ks]
        for h in range(hpg):
            new = new + upd[h * dv:(h + 1) * dv] * hm_ref[h]
        st_ref[g] = new
        outs.append(intra + jnp.concatenate([inter[h * c:(h + 1) * c] for h in range(hpg)], axis=-1))
    return jnp.concatenate(outs, axis=-1)


def _log_sigmoid(x):
    return jnp.minimum(x, 0.0) - jnp.log(1.0 + jnp.exp(-jnp.abs(x)))


def _gla_kernel(*refs):
    ins, head_masks, (o_refs, st_refs) = (refs[0:8], refs[8:16]), refs[16:19], (refs[19:21], refs[21:23])

    @pl.when(pl.program_id(1) == 0)
    def _():
        for st_ref in st_refs:
            st_ref[...] = jnp.zeros_like(st_ref)

    for d, ((q_ref, k_ref, v_ref, lr_ref, gw_ref, gb_ref, sel_ref, mask_ref), o_ref, st_ref) in enumerate(
            zip(ins, o_refs, st_refs)):
        z = _dot_f32(lr_ref[...], gw_ref[...]) + gb_ref[...]
        la = _log_sigmoid(z) * (1.0 / GLA_TAU)
        q = q_ref[...] * (GLA_DK ** -0.5)
        k, v = k_ref[...], v_ref[...]
        for rows in _sub_chunks(q.shape[0], d == 1):
            o_ref[rows, :] = _decay_chunk(q[rows], k[rows], v[rows], la[rows], (sel_ref, mask_ref) + head_masks,
                                          st_ref, GLA_HEADS, GLA_DK, GLA_DV)


def _hgrn_kernel(*refs):
    ins, head_masks, (o_refs, st_refs) = (refs[0:6], refs[6:12]), refs[12:15], (refs[15:17], refs[17:19])

    @pl.when(pl.program_id(1) == 0)
    def _():
        for st_ref in st_refs:
            st_ref[...] = jnp.zeros_like(st_ref)

    for d, ((q_ref, f_ref, v_ref, lb_ref, sel_ref, mask_ref), o_ref, st_ref) in enumerate(
            zip(ins, o_refs, st_refs)):
        lb = lb_ref[...]
        f = lb + (1.0 - lb) * _sigmoid(f_ref[...])
        q, k, v, la = _silu(q_ref[...]), 1.0 - f, v_ref[...], jnp.log(f)
        for rows in _sub_chunks(q.shape[0], d == 1):
            o_ref[rows, :] = _decay_chunk(q[rows], k[rows], v[rows], la[rows], (sel_ref, mask_ref) + head_masks,
                                          st_ref, HG_HEADS, HG_EXPAND, HG_DV)


def _sub_chunks(rows, reverse):
    order = range(rows // SCAN_CHUNK)
    return [slice(j * SCAN_CHUNK, (j + 1) * SCAN_CHUNK) for j in (reversed(order) if reverse else order)]


def _scan_specs(blk, n_ctx, t, reverse, chunk=None, stacked_heads=1):
    n_blocks = t // blk
    order = functools.partial(_chunk_order, n_ctx_chunks=n_ctx // blk, n_chunks=n_blocks, reverse=reverse)

    def col(width, idx):
        return pl.BlockSpec((None, blk, width), lambda b, i: (b, order(i), idx))

    sel, msk = _scan_constants(chunk or blk, reverse)
    sel3 = np.concatenate([sel, sel, sel], axis=1)
    msk = np.tile(msk, (1, 1, stacked_heads))
    const = lambda a: pl.BlockSpec(a.shape, lambda b, i: (0,) * a.ndim)
    return n_blocks, col, const, jnp.asarray(sel3, BF16), jnp.asarray(msk, F32)


def _head_masks(dk, dv):
    hpg = GROUP_KEYS // dk
    hm = np.zeros((hpg, 1, GROUP_KEYS), np.float32)
    vm = np.zeros((hpg, 1, hpg * dv), np.float32)
    for h in range(hpg):
        hm[h, 0, h * dk:(h + 1) * dk] = 1.0
        vm[h, 0, h * dv:(h + 1) * dv] = 1.0
    return jnp.asarray(hm), jnp.asarray(hm, BF16), jnp.asarray(vm, BF16)


AB_Q, AB_K, AB_V, AB_G = 0, 256, 512, 1024
AB_HQ, AB_HF, AB_HI, AB_HG, AB_LR = 1536, 2048, 3072, 3584, 4096
AB_PAD_COLS = 4224


def _gla_scan(proj, gate_w_pad, gate_b, n_ctx):
    bsz, t, _ = proj.shape
    hpg = GROUP_KEYS // GLA_DK
    in_specs, args, outs = [], [], []
    for d in range(2):
        n_blocks, col, const, sel, msk = _scan_specs(SCAN_BLOCK, n_ctx, t, d == 1, SCAN_CHUNK, hpg)
        in_specs += [col(GLA_KEY_W, AB_Q // GLA_KEY_W), col(GLA_KEY_W, AB_K // GLA_KEY_W),
                     col(GLA_VAL_W, AB_V // GLA_VAL_W), col(LANES, AB_LR // LANES),
                     const(gate_w_pad[d]), const(gate_b[d]), const(sel), const(msk)]
        args += [proj, proj, proj, proj, gate_w_pad[d], gate_b[d], sel, msk]
        outs.append(col(GLA_VAL_W, 0))
    hm = _head_masks(GLA_DK, GLA_DV)
    return pl.pallas_call(
        _gla_kernel,
        grid=(bsz, n_blocks),
        in_specs=in_specs + [const(m) for m in hm],
        out_specs=outs,
        out_shape=[jax.ShapeDtypeStruct((bsz, t, GLA_VAL_W), F32)] * 2,
        scratch_shapes=[pltpu.VMEM((GLA_HEADS // hpg, GLA_DV, GROUP_KEYS), F32)] * 2,
        compiler_params=_params("parallel", "arbitrary"),
        name="gla_scan",
    )(*args, *hm)


def _hgrn_scan(proj, lb, n_ctx):
    bsz, t, _ = proj.shape
    hpg = GROUP_KEYS // HG_EXPAND
    in_specs, args, outs = [], [], []
    for d in range(2):
        n_blocks, col, const, sel, msk = _scan_specs(SCAN_BLOCK, n_ctx, t, d == 1, SCAN_CHUNK, hpg)
        in_specs += [col(HG_KEY_W, AB_HQ // HG_KEY_W), col(HG_KEY_W, AB_HF // HG_KEY_W + d),
                     col(HG_VAL_W, AB_HI // HG_VAL_W), const(lb[d]), const(sel), const(msk)]
        args += [proj, proj, proj, lb[d], sel, msk]
        outs.append(col(HG_VAL_W, 0))
    hm = _head_masks(HG_EXPAND, HG_DV)
    return pl.pallas_call(
        _hgrn_kernel,
        grid=(bsz, n_blocks),
        in_specs=in_specs + [const(m) for m in hm],
        out_specs=outs,
        out_shape=[jax.ShapeDtypeStruct((bsz, t, HG_VAL_W), F32)] * 2,
        scratch_shapes=[pltpu.VMEM((HG_HEADS // hpg, HG_DV, GROUP_KEYS), F32)] * 2,
        compiler_params=_params("parallel", "arbitrary"),
        name="hgrn_scan",
    )(*args, *hm)


def _mix_out_ab_kernel(h_ref, gf_ref, gb_ref, hf_ref, hb_ref, gg_ref, hg_ref, gn_ref, hn_ref,
                       w_ref, gt_ref, o_ref):
    feats = []
    for o, gate, g in ((gf_ref[...] + gb_ref[...], gg_ref[...], gn_ref[...]),
                       (hf_ref[...] + hb_ref[...], hg_ref[...], hn_ref[...])):
        for hd in range(o.shape[-1] // LANES):
            s = slice(hd * LANES, (hd + 1) * LANES)
            feats.append(_rms(o[:, s], g) * _silu(gate[:, s]))
    feat = jnp.concatenate(feats, axis=-1).astype(BF16)
    o_ref[...] = h_ref[...] + gt_ref[...] * _dot(feat, w_ref[...])


def _mix_out_ab(h, o_gla, o_hg, proj, gla_norm_g, hg_norm_g, w_out, gate, n_ctx):
    bsz, t, d = h.shape
    tm = ROW_TILE
    seg = lambda b, i: (b, (i * tm >= n_ctx).astype(jnp.int32), 0, 0)
    row = lambda width, idx: pl.BlockSpec((None, tm, width), lambda b, i: (b, i, idx))
    vec = pl.BlockSpec((1, LANES), lambda b, i: (0, 0))
    return pl.pallas_call(
        _mix_out_ab_kernel,
        grid=(bsz, t // tm),
        in_specs=[row(d, 0), row(GLA_VAL_W, 0), row(GLA_VAL_W, 0), row(HG_VAL_W, 0), row(HG_VAL_W, 0),
                  row(GLA_VAL_W, AB_G // GLA_VAL_W), row(HG_VAL_W, AB_HG // HG_VAL_W), vec, vec,
                  pl.BlockSpec(w_out.shape, lambda b, i: (0, 0)),
                  pl.BlockSpec((None, None, 1, d), seg)],
        out_specs=row(d, 0),
        out_shape=jax.ShapeDtypeStruct((bsz, t, d), F32),
        compiler_params=_params("parallel", "parallel"),
        name="mix_out_ab",
    )(h, o_gla[0], o_gla[1], o_hg[0], o_hg[1], proj, proj, gla_norm_g.reshape(1, -1),
      hg_norm_g.reshape(1, -1), w_out, gate)


def _dwconv_kernel(x_ref, prev_ref, next_ref, w_ref, b_ref, o_ref, *, seg_tiles, act):
    i = pl.program_id(1)
    tm = x_ref.shape[0]
    x = x_ref[...]
    first = functools.reduce(jnp.logical_or, [i == s for s in seg_tiles[:-1]])
    last = functools.reduce(jnp.logical_or, [i == s - 1 for s in seg_tiles[1:]])
    prev_row = jnp.where(first, 0.0, prev_ref[7:8, :])
    next_row = jnp.where(last, 0.0, next_ref[0:1, :])
    rows = lax.broadcasted_iota(jnp.int32, x.shape, 0)
    x_prev = jnp.where(rows == 0, prev_row, pltpu.roll(x, 1, axis=0))
    x_next = jnp.where(rows == tm - 1, next_row, pltpu.roll(x, tm - 1, axis=0))
    y = x_prev * w_ref[0:1, :] + x * w_ref[1:2, :] + x_next * w_ref[2:3, :] + b_ref[...]
    o_ref[...] = _silu(y) if act else y


def _dwconv(proj, col0, width, w, b, seg_bounds, act):
    bsz, t, _ = proj.shape
    tm = ROW_TILE
    off = seg_bounds[0] // tm
    n_tiles = t // tm - off
    seg_tiles = tuple(s // tm - off for s in seg_bounds)
    cb = col0 // width
    r8 = tm // 8
    last8 = t // 8 - 1
    kern = functools.partial(_dwconv_kernel, seg_tiles=seg_tiles, act=act)
    return pl.pallas_call(
        kern,
        grid=(bsz, n_tiles),
        in_specs=[pl.BlockSpec((None, tm, width), lambda bb, i: (bb, i + off, cb)),
                  pl.BlockSpec((None, 8, width), lambda bb, i: (bb, jnp.maximum((i + off) * r8 - 1, 0), cb)),
                  pl.BlockSpec((None, 8, width),
                               lambda bb, i: (bb, jnp.minimum((i + off + 1) * r8, last8), cb)),
                  pl.BlockSpec((3, width), lambda bb, i: (0, 0)),
                  pl.BlockSpec((1, width), lambda bb, i: (0, 0))],
        out_specs=pl.BlockSpec((None, tm, width), lambda bb, i: (bb, i, 0)),
        out_shape=jax.ShapeDtypeStruct((bsz, n_tiles * tm, width), F32),
        compiler_params=_params("parallel", "parallel"),
        name="dwconv",
    )(proj, proj, proj, w.T, b.reshape(1, -1))


def _ssd_kernel(*refs):
    ins, hexp_ref, o_refs, st_refs = (refs[0:7], refs[7:14]), refs[14], refs[15:17], refs[17:19]

    @pl.when(pl.program_id(1) == 0)
    def _():
        for st_ref in st_refs:
            st_ref[...] = jnp.zeros_like(st_ref)

    for (xbc_ref, dt_ref, bias_ref, alog_ref, mq_ref, mk_ref, mask_ref), o_ref, st_ref in zip(ins, o_refs, st_refs):
        _ssd_chunk(xbc_ref, dt_ref, bias_ref, alog_ref, hexp_ref, mq_ref, mk_ref, mask_ref, o_ref, st_ref)


def _ssd_chunk(xbc_ref, dt_ref, bias_ref, alog_ref, hexp_ref, mq_ref, mk_ref, mask_ref, o_ref, st_ref):
    c = xbc_ref.shape[0]
    hpg = MB_HEADS // MB_GROUPS
    gw = hpg * MB_HEAD_DIM
    dt = _softplus(dt_ref[...] + bias_ref[...])
    la = -dt * jnp.exp(alog_ref[...])
    cq = _sel_dot(mq_ref[...], la)
    ck = _sel_dot(mk_ref[...], la)
    cq_t = _dot_nt_sel(la, mq_ref[...])
    hexp = hexp_ref[...]
    dt_x = _dot_sel(dt, hexp)
    eq_x = jnp.exp(_dot_sel(cq, hexp))
    ek_x = jnp.exp(_dot_sel(ck, hexp))
    etot_x = jnp.exp(_dot_sel(jnp.sum(la, axis=0, keepdims=True), hexp))
    xs = xbc_ref[:, 0:MB_INNER] * dt_x
    mask = mask_ref[...]
    outs = []
    for g in range(MB_GROUPS):
        bm = xbc_ref[:, MB_INNER + g * MB_STATE:MB_INNER + (g + 1) * MB_STATE].astype(BF16)
        cm = xbc_ref[:, MB_INNER + MB_BC_W + g * MB_STATE:MB_INNER + MB_BC_W + (g + 1) * MB_STATE].astype(BF16)
        cb = _dot_nt(cm, bm)
        st = st_ref[g]
        gs = slice(g * gw, (g + 1) * gw)
        y_inter = _dot(cm, st.astype(BF16)) * eq_x[:, gs]
        for r in range(hpg):
            hd = g * hpg + r
            diff = cq[:, hd:hd + 1] - cq_t[hd:hd + 1, :]
            w = cb * jnp.exp(jnp.where(mask > 0.0, diff, -jnp.inf))
            ps = slice(hd * MB_HEAD_DIM, (hd + 1) * MB_HEAD_DIM)
            outs.append(_dot(w.astype(BF16), xs[:, ps].astype(BF16))
                        + y_inter[:, r * MB_HEAD_DIM:(r + 1) * MB_HEAD_DIM])
        st_ref[g] = st * etot_x[:, gs] + _dot_tn(bm, (xs[:, gs] * ek_x[:, gs]).astype(BF16))
    o_ref[...] = jnp.concatenate(outs, axis=-1)


def _dot_nt_sel(x, m01):
    hi, mid, lo = _split3(x)
    f = lambda p: lax.dot_general(p, m01, (((0,), (1,)), ((), ())), preferred_element_type=F32)
    return f(hi) + (f(mid) + f(lo))


def _ssd_scan(xbc, proj, dt_col, dt_bias, a_log, n_ctx):
    bsz, t, _ = xbc.shape
    c = SSD_CHUNK
    pad = lambda v: jnp.zeros((1, LANES), F32).at[0, :MB_HEADS].set(v)
    hexp = np.zeros((LANES, MB_INNER), np.float32)
    for hd in range(MB_HEADS):
        hexp[hd, hd * MB_HEAD_DIM:(hd + 1) * MB_HEAD_DIM] = 1.0
    hexp = jnp.asarray(hexp, BF16)
    tri = np.tril(np.ones((c, c), np.float32))
    in_specs, args, outs = [], [], []
    for d in range(2):
        n_chunks, col, const, sel, _ = _scan_specs(c, n_ctx, t, d == 1)
        mq, mk = sel[0:c, 0:c], sel[c:2 * c, 0:c]
        mask = jnp.asarray(tri[::-1, ::-1].copy() if d == 1 else tri)
        bias, alog = pad(dt_bias[d]), pad(a_log[d].astype(F32))
        in_specs += [col(xbc.shape[-1], 0), col(LANES, dt_col + d), const(bias), const(alog),
                     const(mq), const(mk), const(mask)]
        args += [xbc, proj, bias, alog, mq, mk, mask]
        outs.append(col(MB_INNER, 0))
    return pl.pallas_call(
        _ssd_kernel,
        grid=(bsz, n_chunks),
        in_specs=in_specs + [const(hexp)],
        out_specs=outs,
        out_shape=[jax.ShapeDtypeStruct((bsz, t, MB_INNER), F32)] * 2,
        scratch_shapes=[pltpu.VMEM((MB_GROUPS, MB_STATE, MB_INNER // MB_GROUPS), F32)] * 2,
        compiler_params=_params("parallel", "arbitrary"),
        name="ssd_scan",
    )(*args, hexp)


def _mix_out_cd_kernel(h_ref, hy_ref, yf_ref, yb_ref, xs_ref, z_ref, dsk_ref, ng_ref, w_ref, gt_ref, o_ref):
    y = (yf_ref[...] + yb_ref[...] + dsk_ref[...] * xs_ref[...]) * _silu(z_ref[...])
    gw = MB_INNER // MB_GROUPS
    ys = [_rms(y[:, g * gw:(g + 1) * gw], ng_ref[:, g * gw:(g + 1) * gw]) for g in range(MB_GROUPS)]
    feat = jnp.concatenate([hy_ref[...]] + ys, axis=-1).astype(BF16)
    o_ref[...] = h_ref[...] + gt_ref[...] * _dot(feat, w_ref[...])


def _mix_out_cd(h, hy, y_ssd, xbc, proj, z_col, d_skip_x, norm_g, w_out, gate, n_ctx):
    bsz, t, d = h.shape
    tm = ROW_TILE
    n_lat = t - n_ctx
    off = n_ctx // tm
    row = lambda width, idx: pl.BlockSpec((None, tm, width), lambda b, i: (b, i + off, idx))
    vec = pl.BlockSpec((1, MB_INNER), lambda b, i: (0, 0))
    return pl.pallas_call(
        _mix_out_cd_kernel,
        grid=(bsz, n_lat // tm),
        in_specs=[row(d, 0), pl.BlockSpec((None, tm, HY_CH), lambda b, i: (b, i, 0)),
                  row(MB_INNER, 0), row(MB_INNER, 0), row(MB_INNER, 0), row(MB_INNER, z_col), vec, vec,
                  pl.BlockSpec(w_out.shape, lambda b, i: (0, 0)),
                  pl.BlockSpec((None, None, 1, d), lambda b, i: (b, 1, 0, 0))],
        out_specs=pl.BlockSpec((None, tm, d), lambda b, i: (b, i, 0)),
        out_shape=jax.ShapeDtypeStruct((bsz, n_lat, d), F32),
        compiler_params=_params("parallel", "parallel"),
        name="mix_out_cd",
    )(h, hy, y_ssd[0], y_ssd[1], xbc, proj, d_skip_x, norm_g.reshape(1, -1), w_out, gate)


def _top2_of4(a, b, c, d):
    hi1, lo1, hi2, lo2 = jnp.maximum(a, b), jnp.minimum(a, b), jnp.maximum(c, d), jnp.minimum(c, d)
    return jnp.maximum(hi1, hi2) + jnp.maximum(jnp.minimum(hi1, hi2), jnp.maximum(lo1, lo2))


def _first_argmax(vals, skip=None):
    idx = None
    for j, vj in enumerate(vals):
        if idx is None and skip is None:
            idx, best = jnp.zeros(vj.shape, jnp.int32), vj
            continue
        if idx is None:
            idx, best = jnp.full(vj.shape, -1, jnp.int32), jnp.full(vj.shape, -jnp.inf, F32)
        take = vj > best
        if skip is not None:
            take = jnp.logical_and(take, skip != j)
        idx = jnp.where(take, j, idx)
        best = jnp.where(take, vj, best)
    return idx, best


def _ffn_pre_kernel(h_ref, g_ref, sh_ref, sc_ref, rw_ref, rb_ref, tri_ref,
                    v_ref, ri_ref, rwt_ref, cnt_ref, carry_ref):
    @pl.when(jnp.logical_and(pl.program_id(0) == 0, pl.program_id(1) == 0))
    def _():
        carry_ref[...] = jnp.zeros_like(carry_ref)

    v = _rms(h_ref[...], g_ref[...]) * (1.0 + sc_ref[...]) + sh_ref[...]
    v_ref[...] = _pack_bf16_pairs(v)
    st = _sigmoid(_dot_f32(v, rw_ref[...])).T[0:N_EXPERTS]
    sel = st + rb_ref[...]
    row = lambda a, e: a[e:e + 1]
    epg = EXPERTS_PER_GROUP
    gscore = [_top2_of4(*[row(sel, g * epg + j) for j in range(epg)]) for g in range(N_GROUPS)]
    best, _ = _first_argmax(gscore)

    def in_best(a, j):
        out = row(a, j)
        for g in range(1, N_GROUPS):
            out = jnp.where(best == g, row(a, g * epg + j), out)
        return out

    vals = [in_best(sel, j) for j in range(epg)]
    raw = [in_best(st, j) for j in range(epg)]
    i1, _ = _first_argmax(vals)
    i2, _ = _first_argmax(vals, skip=i1)
    pick = lambda i: functools.reduce(lambda acc, j: jnp.where(i == j, raw[j], acc), range(1, epg), raw[0])
    w1, w2 = pick(i1), pick(i2)
    wsum = w1 + w2
    e1, e2 = best * epg + i1, best * epg + i2

    experts = lax.broadcasted_iota(jnp.int32, st.shape, 0)
    oh1 = (experts == e1).astype(F32)
    oh2 = (experts == e2).astype(F32)
    cnt = oh1 + oh2
    before = _dot(cnt.astype(BF16), tri_ref[...]) + carry_ref[:, 0:1]
    ri_ref[0:1, :] = e1
    ri_ref[1:2, :] = e2
    ri_ref[2:3, :] = jnp.sum(oh1 * before, axis=0, keepdims=True).astype(jnp.int32)
    ri_ref[3:4, :] = jnp.sum(oh2 * before, axis=0, keepdims=True).astype(jnp.int32)
    ri_ref[4:8, :] = jnp.zeros((4, st.shape[1]), jnp.int32)
    rwt_ref[0:1, :] = w1 / wsum
    rwt_ref[1:2, :] = w2 / wsum
    rwt_ref[2:8, :] = jnp.zeros((6, st.shape[1]), F32)
    carry_ref[...] = carry_ref[...] + jnp.sum(cnt, axis=1, keepdims=True)
    cnt_ref[...] = carry_ref[...]


def _ffn_pre(h, g, shift, scale, router_w_pad, router_b, n_ctx):
    bsz, t, d = h.shape
    tm = ROW_TILE
    seg = lambda b, i: (b, (i * tm >= n_ctx).astype(jnp.int32), 0, 0)
    tri = jnp.asarray(np.triu(np.ones((tm, tm), np.float32), 1), BF16)
    return pl.pallas_call(
        _ffn_pre_kernel,
        grid=(bsz, t // tm),
        in_specs=[pl.BlockSpec((None, tm, d), lambda b, i: (b, i, 0)),
                  pl.BlockSpec((1, d), lambda b, i: (0, 0)),
                  pl.BlockSpec((None, None, 1, d), seg),
                  pl.BlockSpec((None, None, 1, d), seg),
                  pl.BlockSpec((d, LANES), lambda b, i: (0, 0)),
                  pl.BlockSpec((N_EXPERTS, 1), lambda b, i: (0, 0)),
                  pl.BlockSpec((tm, tm), lambda b, i: (0, 0))],
        out_specs=[pl.BlockSpec((None, tm, d // 2), lambda b, i: (b, i, 0)),
                   pl.BlockSpec((None, 8, tm), lambda b, i: (b, 0, i)),
                   pl.BlockSpec((None, 8, tm), lambda b, i: (b, 0, i)),
                   pl.BlockSpec((N_EXPERTS, LANES), lambda b, i: (0, 0))],
        out_shape=[jax.ShapeDtypeStruct((bsz, t, d // 2), jnp.uint32),
                   jax.ShapeDtypeStruct((bsz, 8, t), jnp.int32),
                   jax.ShapeDtypeStruct((bsz, 8, t), F32),
                   jax.ShapeDtypeStruct((N_EXPERTS, LANES), F32)],
        scratch_shapes=[pltpu.VMEM((N_EXPERTS, LANES), F32)],
        compiler_params=_params("arbitrary", "arbitrary"),
        name="ffn_pre",
    )(h, g.reshape(1, d), shift, scale, router_w_pad, router_b.reshape(N_EXPERTS, 1), tri)


def _experts_kernel(be_ref, nb_ref, x_ref, wg_ref, wu_ref, wd_ref, o_ref, wg_s, wu_s, wd_s):
    i = pl.program_id(0)
    prev = be_ref[jnp.maximum(i - 1, 0)]
    changed = jnp.logical_or(i == 0, be_ref[i] != prev)

    @pl.when(changed)
    def _():
        wg_s[...] = wg_ref[...].astype(BF16)
        wu_s[...] = wu_ref[...].astype(BF16)
        wd_s[...] = wd_ref[...].astype(BF16)

    @pl.when(i < nb_ref[0])
    def _():
        x = _unpack_bf16_pairs(x_ref[...]).astype(BF16)
        hid = _silu(_dot(x, wg_s[...])) * _dot(x, wu_s[...])
        o_ref[...] = _pack_bf16_pairs(_dot(hid.astype(BF16), wd_s[...]))

    @pl.when(i >= nb_ref[0])
    def _():
        o_ref[...] = jnp.zeros_like(o_ref)


def _experts(xb, block_e, n_used, layer, w_gate, w_up, w_down):
    n_slots = xb.shape[0]
    n_blocks = n_slots // MOE_BLOCK
    d, de = w_gate.shape[-2:]
    wspec = lambda shape: pl.BlockSpec((None, None) + shape, lambda i, be, nb: (layer, be[i], 0, 0))
    return pl.pallas_call(
        _experts_kernel,
        grid_spec=pltpu.PrefetchScalarGridSpec(
            num_scalar_prefetch=2,
            grid=(n_blocks,),
            in_specs=[pl.BlockSpec((MOE_BLOCK, d // 2), lambda i, be, nb: (i, 0)),
                      wspec((d, de)), wspec((d, de)), wspec((de, d))],
            out_specs=pl.BlockSpec((MOE_BLOCK, d // 2), lambda i, be, nb: (i, 0)),
            scratch_shapes=[pltpu.VMEM((d, de), BF16), pltpu.VMEM((d, de), BF16), pltpu.VMEM((de, d), BF16)]),
        out_shape=jax.ShapeDtypeStruct((n_slots, d // 2), jnp.uint32),
        compiler_params=_params("arbitrary"),
        name="moe_experts",
    )(block_e, n_used, xb, w_gate, w_up, w_down)


def _ffn_post_kernel(h_ref, y0_ref, y1_ref, w_ref, gt_ref, g_ref, o_ref, *, final):
    w = w_ref[...]
    y = w[:, 0:1] * _unpack_bf16_pairs(y0_ref[...]) + w[:, 1:2] * _unpack_bf16_pairs(y1_ref[...])
    out = h_ref[...] + gt_ref[...] * y
    o_ref[...] = _rms(out, g_ref[...]) if final else out


def _ffn_post(h, y, w, gate, n_ctx, final_g=None):
    bsz, t, d = h.shape
    tm = ROW_TILE
    seg = lambda b, i: (b, (i * tm >= n_ctx).astype(jnp.int32), 0, 0)
    row = lambda width: pl.BlockSpec((None, tm, width), lambda b, i: (b, i, 0))
    choice = lambda kk: pl.BlockSpec((None, None, tm, d // 2), lambda b, i: (kk, b, i, 0))
    final = final_g is not None
    g = final_g if final else jnp.ones((d,), F32)
    return pl.pallas_call(
        functools.partial(_ffn_post_kernel, final=final),
        grid=(bsz, t // tm),
        in_specs=[row(d), choice(0), choice(1), row(LANES), pl.BlockSpec((None, None, 1, d), seg),
                  pl.BlockSpec((1, d), lambda b, i: (0, 0))],
        out_specs=row(d),
        out_shape=jax.ShapeDtypeStruct((bsz, t, d), F32),
        compiler_params=_params("parallel", "parallel"),
        name="ffn_post",
    )(h, y, y, w, gate, g.reshape(1, d))


def _slot_layout(n, ri, counts):
    e = jnp.swapaxes(ri[:, 0:2], 0, 1).reshape(TOP_K, n)
    rank = jnp.swapaxes(ri[:, 2:4], 0, 1).reshape(TOP_K, n)
    padded = (counts + MOE_BLOCK - 1) // MOE_BLOCK * MOE_BLOCK
    pend = jnp.cumsum(padded)
    pstart = pend - padded
    experts = jnp.arange(N_EXPERTS, dtype=jnp.int32)
    dest = rank + jnp.sum(jnp.where(e[..., None] == experts, pstart, 0), axis=-1)
    n_slots = (n * TOP_K + MOE_BLOCK - 1) // MOE_BLOCK * MOE_BLOCK + N_EXPERTS * MOE_BLOCK
    n_blocks = n_slots // MOE_BLOCK
    blk0 = jnp.arange(n_blocks, dtype=jnp.int32)[:, None] * MOE_BLOCK
    block_e = jnp.minimum(jnp.sum((pend[None, :] <= blk0).astype(jnp.int32), axis=-1), N_EXPERTS - 1)
    n_used = (pend[-1] // MOE_BLOCK).astype(jnp.int32).reshape(1)
    return dest, n_slots, block_e.astype(jnp.int32), n_used


SC_CORES, SC_SUBCORES = 2, 16
SC_WINDOW = 32


def _gather_rows(table, idx):
    n_rows, d = idx.shape[0], table.shape[1]
    workers = SC_CORES * SC_SUBCORES
    per_worker = n_rows // workers
    assert per_worker * workers == n_rows and per_worker % SC_WINDOW == 0
    mesh = plsc.VectorSubcoreMesh(core_axis_name="c", subcore_axis_name="s")

    @functools.partial(
        pl.kernel, mesh=mesh,
        out_type=jax.ShapeDtypeStruct((n_rows, d), table.dtype),
        scratch_types=[pltpu.VMEM((SC_WINDOW,), jnp.int32), pltpu.VMEM((SC_WINDOW,), jnp.int32),
                       pltpu.VMEM((SC_WINDOW, d), table.dtype), pltpu.VMEM((SC_WINDOW, d), table.dtype),
                       pltpu.SemaphoreType.DMA, pltpu.SemaphoreType.DMA],
    )
    def gather_kernel(table_hbm, idx_hbm, out_hbm, idx0, idx1, rows0, rows1, sem0, sem1):
        base = (lax.axis_index("s") * SC_CORES + lax.axis_index("c")) * per_worker
        n_win = per_worker // SC_WINDOW
        slots = ((idx0, rows0, sem0), (idx1, rows1, sem1))
        window = lambda j: pl.ds(pl.multiple_of(base + j * SC_WINDOW, 8), SC_WINDOW)

        def start(j, slot):
            idx_v, rows_v, sem = slots[slot]
            pltpu.sync_copy(idx_hbm.at[window(j)], idx_v)
            pltpu.async_copy(table_hbm.at[idx_v], rows_v, sem)

        def finish(j, slot):
            idx_v, rows_v, sem = slots[slot]
            pltpu.make_async_copy(table_hbm.at[idx_v], rows_v, sem).wait()
            pltpu.sync_copy(rows_v, out_hbm.at[window(j)])

        start(0, 0)

        @pl.loop(0, n_win, step=2)
        def _(j):
            @pl.when(j + 1 < n_win)
            def _():
                start(j + 1, 1)

            finish(j, 0)

            @pl.when(j + 2 < n_win)
            def _():
                start(j + 2, 0)

            @pl.when(j + 1 < n_win)
            def _():
                finish(j + 1, 1)

    return gather_kernel(table, idx)


SC_SCATTER_WINDOW = 16


def _scatter_rows(src, dest, n_slots):
    n, d = src.shape
    workers = SC_CORES * SC_SUBCORES
    per_worker = n // workers
    win = SC_SCATTER_WINDOW
    assert per_worker * workers == n and per_worker % win == 0 and dest.shape == (TOP_K, n)
    mesh = plsc.VectorSubcoreMesh(core_axis_name="c", subcore_axis_name="s")

    @functools.partial(
        pl.kernel, mesh=mesh,
        out_type=jax.ShapeDtypeStruct((n_slots, d), src.dtype),
        scratch_types=[pltpu.VMEM((win,), jnp.int32), pltpu.VMEM((win,), jnp.int32),
                       pltpu.VMEM((win, d), src.dtype), pltpu.SemaphoreType.DMA, pltpu.SemaphoreType.DMA],
    )
    def scatter_kernel(src_hbm, dest_hbm, out_hbm, idx0, idx1, rows_v, sem0, sem1):
        base = (lax.axis_index("s") * SC_CORES + lax.axis_index("c")) * per_worker

        @pl.loop(0, per_worker // win)
        def _(j):
            rows = pl.ds(pl.multiple_of(base + j * win, 8), win)
            pltpu.sync_copy(src_hbm.at[rows], rows_v)
            pltpu.sync_copy(dest_hbm.at[0, rows], idx0)
            pltpu.sync_copy(dest_hbm.at[1, rows], idx1)
            first = pltpu.async_copy(rows_v, out_hbm.at[idx0], sem0)
            second = pltpu.async_copy(rows_v, out_hbm.at[idx1], sem1)
            first.wait()
            second.wait()

    return scatter_kernel(src, dest)


def _alongside(gather, idx, side_fn, side_in):
    idx, side_in = lax.optimization_barrier((idx, side_in))
    return lax.optimization_barrier((gather(idx), side_fn(side_in)))


def _moe(h, g, shift, scale, gate, router_w_pad, router_b, layer, w_gate, w_up, w_down, n_ctx,
         final_g=None, side=None):
    bsz, t, d = h.shape
    n = bsz * t
    v, ri, rwt, counts = _ffn_pre(h, g, shift, scale, router_w_pad, router_b, n_ctx)
    dest, n_slots, block_e, n_used = _slot_layout(n, ri, counts[:, 0].astype(jnp.int32))
    w = jnp.swapaxes(rwt[:, 0:2], 1, 2).reshape(n, TOP_K)
    dispatch = lambda idx: _scatter_rows(v.reshape(n, d // 2), idx, n_slots)
    if side is None:
        xb = dispatch(dest)
    else:
        xb, side_a = _alongside(dispatch, dest, *side[0])
    yb = _experts(xb, block_e, n_used, layer, w_gate, w_up, w_down)
    combine = lambda idx: _gather_rows(yb, idx)
    dest_flat = dest.reshape(-1)
    if side is None:
        y, side_b = combine(dest_flat), None
    else:
        y, side_b = _alongside(combine, dest_flat, side[1], side_a)
    wpad = jnp.zeros((n, LANES), F32).at[:, :TOP_K].set(w).reshape(bsz, t, LANES)
    out = _ffn_post(h, y.reshape(TOP_K, bsz, t, d // 2), wpad, gate, n_ctx, final_g)
    return out if side is None else (out, side_b)


DFT_STEP = 8


def _dft_tables(n):
    r, *mats = _dft_tables_np(n)
    return (r,) + tuple(jnp.asarray(a).astype(BF16) for a in mats)


@functools.lru_cache(maxsize=None)
def _dft_tables_np(n):
    size = 2 * n
    r = int(round(math.sqrt(size)))
    assert r * r == size and r % DFT_STEP == 0
    p1 = np.arange(r // 2)[None, None, :]
    p2 = np.arange(r)[:, None, None]
    k1 = np.arange(r)[None, :, None]
    ang = 2.0 * np.pi * (((r * p1 + p2) * k1) % size) / size
    g_re, g_im = np.cos(ang), -np.sin(ang)
    g_in = np.concatenate([g_re, g_im], axis=1)
    g_out = np.concatenate([np.swapaxes(g_re, 1, 2), np.swapaxes(g_im, 1, 2)], axis=2) / size
    a2 = 2.0 * np.pi * ((np.arange(r)[:, None] * np.arange(r)[None, :]) % r) / r
    f_re, f_im = np.cos(a2), -np.sin(a2)
    f_fwd = np.block([[f_re, -f_im], [f_im, f_re]])
    f_inv = np.block([[f_re, f_im], [-f_im, f_re]])
    p1f = np.arange(r)[None, None, :]
    angf = 2.0 * np.pi * (((r * p1f + p2) * k1) % size) / size
    g_full = np.concatenate([np.cos(angf), -np.sin(angf)], axis=1)
    return (r,) + tuple(a.astype(np.float32) for a in (g_in, g_out, f_fwd, f_inv, g_full))


def _hy_filter_kernel(z_ref, w1_ref, b1_ref, w2_ref, b2_ref, w3_ref, fr_ref, rates_ref, o_ref, *, half_tiles):
    i = pl.program_id(0)
    z = z_ref[...]
    hid = jnp.sin(fr_ref[...] * (_dot_f32(z, w1_ref[...]) + b1_ref[...]))
    hid = jnp.sin(fr_ref[...] * (_dot_f32(hid, w2_ref[...]) + b2_ref[...]))
    filt = _dot_f32(hid, w3_ref[...])
    decay = jnp.exp(-z[:, 0:1] * rates_ref[...])
    row0 = lax.broadcasted_iota(jnp.int32, decay.shape, 0) == 0
    for o in range(o_ref.shape[0]):
        fwd = filt[:, (2 * o) * HY_CH:(2 * o + 1) * HY_CH] * decay
        bwd = filt[:, (2 * o + 1) * HY_CH:(2 * o + 2) * HY_CH] * decay
        first = jnp.where(jnp.logical_and(row0, i == 0), fwd + bwd, fwd)
        second = jnp.where(jnp.logical_and(row0, i == half_tiles), 0.0, bwd)
        o_ref[o] = jnp.where(i < half_tiles, first, second)


def _hy_kernels(n, w1, b1, w2, b2, w3, freq):
    pos = np.arange(2 * n)
    pos = np.where(pos < n, pos, 2 * n - pos).astype(np.float32)
    t = jnp.asarray(pos / np.float32(n - 1))[:, None]
    bands = jnp.linspace(1e-4, HY_BANDS - 1, HY_BANDS, dtype=F32)
    ang = (2.0 * math.pi / n) * jnp.asarray(pos)[:, None] * bands
    z = jnp.concatenate([t, jnp.cos(ang), -jnp.sin(ang)], axis=-1)
    z = jnp.pad(z, ((0, 0), (0, LANES - z.shape[1])))
    w1p = jnp.pad(w1, ((0, LANES - w1.shape[0]), (0, 0)))
    rates = jnp.abs(jnp.linspace(HY_MIN_DECAY, HY_MAX_DECAY, HY_CH, dtype=F32)).reshape(1, HY_CH)
    tm = ROW_TILE
    full = lambda a: pl.BlockSpec(a.shape, lambda i: (0,) * a.ndim)
    args = (z, w1p, b1.reshape(1, -1), w2, b2.reshape(1, -1), w3, freq.reshape(1, -1), rates)
    return pl.pallas_call(
        functools.partial(_hy_filter_kernel, half_tiles=n // tm),
        grid=(2 * n // tm,),
        in_specs=[pl.BlockSpec((tm, LANES), lambda i: (i, 0))] + [full(a) for a in args[1:]],
        out_specs=pl.BlockSpec((HY_ORDER, tm, HY_CH), lambda i: (0, i, 0)),
        out_shape=jax.ShapeDtypeStruct((HY_ORDER, 2 * n, HY_CH), F32),
        compiler_params=_params("parallel"),
        name="hy_kernels",
    )(*args)


def _dft_in_kernel(x_ref, g_ref, a_ref):
    for j in range(DFT_STEP):
        a_ref[j] = _dot(g_ref[j], x_ref[:, j, :].astype(BF16))


def _dft_in(x4, col, g_in):
    bx, rh, r, _ = x4.shape
    c = HY_CH
    return pl.pallas_call(
        _dft_in_kernel,
        grid=(bx, r // DFT_STEP),
        in_specs=[pl.BlockSpec((None, rh, DFT_STEP, c), lambda b, i: (b, 0, i, col)),
                  pl.BlockSpec((DFT_STEP, 2 * r, rh), lambda b, i: (i, 0, 0))],
        out_specs=pl.BlockSpec((None, DFT_STEP, 2 * r, c), lambda b, i: (b, i, 0, 0)),
        out_shape=jax.ShapeDtypeStruct((bx, r, 2 * r, c), F32),
        compiler_params=_params("parallel", "parallel"),
        name="dft_in",
    )(x4, g_in)


def _stage2(a_ref, b, j, f_ref):
    a = jnp.concatenate([a_ref[b, :, 0, j, :], a_ref[b, :, 1, j, :]], axis=0)
    return _dot(f_ref[...], a.astype(BF16))


def _dft_filt_kernel(a_ref, f_ref, k_ref):
    for j in range(DFT_STEP):
        k_ref[j] = _stage2(a_ref, 0, j, f_ref)


def _dft_filt(a, f_fwd):
    nq, r, _, c = a.shape
    a5 = a.reshape(nq, r, 2, r, c)
    return pl.pallas_call(
        _dft_filt_kernel,
        grid=(nq, r // DFT_STEP),
        in_specs=[pl.BlockSpec((1, r, 2, DFT_STEP, c), lambda o, i: (o, 0, 0, i, 0)),
                  pl.BlockSpec(f_fwd.shape, lambda o, i: (0, 0))],
        out_specs=pl.BlockSpec((None, DFT_STEP, 2 * r, c), lambda o, i: (o, i, 0, 0)),
        out_shape=jax.ShapeDtypeStruct((nq, r, 2 * r, c), F32),
        compiler_params=_params("parallel", "parallel"),
        name="dft_filt",
    )(a5, f_fwd)


def _dft_mid_kernel(a_ref, k_ref, ff_ref, fi_ref, b_ref):
    r = ff_ref.shape[0] // 2
    for b in range(a_ref.shape[0]):
        for j in range(DFT_STEP):
            s = _stage2(a_ref, b, j, ff_ref)
            sr, si = s[0:r], s[r:2 * r]
            kr, ki = k_ref[j, 0:r, :], k_ref[j, r:2 * r, :]
            p = jnp.concatenate([sr * kr - si * ki, sr * ki + si * kr], axis=0).astype(BF16)
            b_ref[b, j] = _dot(fi_ref[...], p)


def _dft_mid(a, kspec, order, f_fwd, f_inv):
    bsz, r, _, c = a.shape
    a5 = a.reshape(bsz, r, 2, r, c)
    return pl.pallas_call(
        _dft_mid_kernel,
        grid=(r // DFT_STEP,),
        in_specs=[pl.BlockSpec((bsz, r, 2, DFT_STEP, c), lambda i: (0, 0, 0, i, 0)),
                  pl.BlockSpec((None, DFT_STEP, 2 * r, c), lambda i: (order, i, 0, 0)),
                  pl.BlockSpec(f_fwd.shape, lambda i: (0, 0)),
                  pl.BlockSpec(f_inv.shape, lambda i: (0, 0))],
        out_specs=pl.BlockSpec((bsz, DFT_STEP, 2 * r, c), lambda i: (0, i, 0, 0)),
        out_shape=jax.ShapeDtypeStruct((bsz, r, 2 * r, c), F32),
        compiler_params=_params("parallel"),
        name="dft_mid",
    )(a5, kspec, f_fwd, f_inv)


def _dft_out_kernel(b_ref, g_ref, u_ref, x_ref, bias_ref, o_ref):
    for j in range(DFT_STEP):
        rhs = jnp.concatenate([b_ref[:, 0, j, :], b_ref[:, 1, j, :]], axis=0).astype(BF16)
        y = _dot(g_ref[j], rhs)
        o_ref[:, j, :] = x_ref[:, j, :] * (y + u_ref[:, j, :] * bias_ref[...])


def _dft_out(bm, g_out, u4, u_col, x4, x_col, bias):
    bsz, r, _, c = bm.shape
    rh = r // 2
    b5 = bm.reshape(bsz, r, 2, r, c)
    seq = lambda col: pl.BlockSpec((None, rh, DFT_STEP, c), lambda b, i: (b, 0, i, col))
    return pl.pallas_call(
        _dft_out_kernel,
        grid=(bsz, r // DFT_STEP),
        in_specs=[pl.BlockSpec((None, r, 2, DFT_STEP, c), lambda b, i: (b, 0, 0, i, 0)),
                  pl.BlockSpec((DFT_STEP, rh, 2 * r), lambda b, i: (i, 0, 0)),
                  seq(u_col), seq(x_col), pl.BlockSpec((1, c), lambda b, i: (0, 0))],
        out_specs=seq(0),
        out_shape=jax.ShapeDtypeStruct((bsz, rh, r, c), F32),
        compiler_params=_params("parallel", "parallel"),
        name="dft_out",
    )(b5, g_out, u4, x4, bias.reshape(1, c))


def _hyena_filter_stage1(n, filter_params):
    r, g_full = _dft_tables(n)[0], _dft_tables(n)[5]
    kern = _hy_kernels(n, *filter_params)
    return _dft_in(kern.reshape(-1, r, r, HY_CH), 0, g_full)


def _hyena_filter_spectra(n, stage1):
    return _dft_filt(stage1, _dft_tables(n)[3])


def _hyena(hy_in, kspec, conv_bias):
    bsz, n, _ = hy_in.shape
    r, g_in, g_out, f_fwd, f_inv, _ = _dft_tables(n)
    seq4 = hy_in.reshape(bsz, r // 2, r, 3 * HY_CH)
    zz = _dft_out(_dft_mid(_dft_in(seq4, 0, g_in), kspec, 0, f_fwd, f_inv), g_out,
                  seq4, 0, seq4, 1, conv_bias[0])
    out = _dft_out(_dft_mid(_dft_in(zz, 0, g_in), kspec, 1, f_fwd, f_inv), g_out,
                   zz, 0, seq4, 2, conv_bias[1])
    return out.reshape(bsz, n, HY_CH)


CD_HY, CD_Z, CD_XBC, CD_DT = 0, 1536, 2048, 3072
CD_PAD_COLS = CD_DT + 2 * LANES


def _reorder_ab(w):
    gq, gk, gv, gg, lr_f, lr_b, hq, hf_f, hf_b, hi, hg = jnp.split(
        w, np.cumsum([256, 256, 512, 512, 16, 16, 512, 512, 512, 512, 512])[:-1].tolist(), axis=-1)
    pad = jnp.zeros((w.shape[0], AB_PAD_COLS - AB_LR - 2 * GLA_LOW_RANK), w.dtype)
    return jnp.concatenate([gq, gk, gv, gg, hq, hf_f, hf_b, hi, hg, lr_f, lr_b, pad], axis=-1)


def _reorder_cd(w):
    hy, z, xbc, dt_f, dt_b = jnp.split(w, np.cumsum([1536, 512, 1024, 8, 8])[:-1].tolist(), axis=-1)
    pad = jnp.zeros((w.shape[0], LANES - MB_HEADS), w.dtype)
    return jnp.concatenate([hy, z, xbc, dt_f, pad, dt_b, pad], axis=-1)


def kernel(x, c, ctx, c_ctx, ada_w, ada_b, norm_mix_g, norm_ffn_g, norm_out_g, ab_w_in, ab_w_out, gla_gate_w, gla_gate_b, gla_norm_g, hg_lb, hg_norm_g, cd_w_in, cd_w_out, hy_short_w, hy_short_b, hy_w1, hy_b1, hy_w2, hy_b2, hy_w3, hy_freq, hy_bias, mb_conv_w, mb_conv_b, mb_dt_bias, mb_a_log, mb_d, mb_norm_g, router_w, router_b, moe_w_gate, moe_w_up, moe_w_down):
    bsz, n_lat, d = x.shape
    n_ctx = ctx.shape[1]
    t = n_ctx + n_lat
    assert ada_w.shape[0] == 2 and ab_w_in.shape[0] == 1 and cd_w_in.shape[0] == 1

    cond = jnp.zeros((8, d), F32).at[:bsz].set(c).at[bsz].set(c_ctx)
    m = _adaln(cond, ada_w, ada_b)

    def mods(layer):
        lat = m[layer, :bsz].reshape(bsz, 6, d)
        cx = jnp.broadcast_to(m[layer, bsz].reshape(1, 6, d), (bsz, 6, d))
        both = jnp.stack([cx, lat], axis=1)
        return [both[:, :, j][:, :, None, :] for j in range(6)]

    lb_all = jnp.cumsum(jax.nn.softmax(hg_lb.astype(F32), axis=1), axis=1)
    router_w_pad = jnp.zeros((d, LANES), F32).at[:, :N_EXPERTS].set(router_w)
    h = jnp.concatenate([ctx, x], axis=1)

    sh_m, sc_m, gt_m, sh_f, sc_f, gt_f = mods(0)
    proj = _norm_proj(h, norm_mix_g[0], sh_m, sc_m, _reorder_ab(ab_w_in[0]).astype(BF16), n_ctx)
    gwp = [jnp.zeros((LANES, GLA_KEY_W), F32).at[GLA_LOW_RANK * dd:GLA_LOW_RANK * (dd + 1)].set(gla_gate_w[0, dd])
           for dd in range(2)]
    o_gla = _gla_scan(proj, gwp, [gla_gate_b[0, dd].reshape(1, -1) for dd in range(2)], n_ctx)
    o_hg = _hgrn_scan(proj, [lb_all[dd, 0].reshape(1, -1) for dd in range(2)], n_ctx)
    h = _mix_out_ab(h, o_gla, o_hg, proj, gla_norm_g[0], hg_norm_g[0], ab_w_out[0].astype(BF16), gt_m, n_ctx)
    filter_params = (hy_w1[0], hy_b1[0], hy_w2[0], hy_b2[0], hy_w3[0], hy_freq[0])
    side = ((functools.partial(_hyena_filter_stage1, n_lat), filter_params),
            functools.partial(_hyena_filter_spectra, n_lat))
    h, kspec = _moe(h, norm_ffn_g[0], sh_f, sc_f, gt_f, router_w_pad, router_b,
                    0, moe_w_gate, moe_w_up, moe_w_down, n_ctx, side=side)

    sh_m, sc_m, gt_m, sh_f, sc_f, gt_f = mods(1)
    proj = _norm_proj(h, norm_mix_g[1], sh_m, sc_m, _reorder_cd(cd_w_in[0]).astype(BF16), n_ctx)
    hy_in = _dwconv(proj, CD_HY, 3 * HY_CH, hy_short_w[0], hy_short_b[0], (n_ctx, t), act=False)
    hy = _hyena(hy_in, kspec, hy_bias[0])
    xbc = _dwconv(proj, CD_XBC, MB_INNER + 2 * MB_BC_W, mb_conv_w[0], mb_conv_b[0], (0, n_ctx, t), act=True)
    y_ssd = _ssd_scan(xbc, proj, CD_DT // LANES, mb_dt_bias[0], mb_a_log[0], n_ctx)
    d_skip_x = jnp.repeat(mb_d[0], MB_HEAD_DIM).reshape(1, MB_INNER)
    h = _mix_out_cd(h, hy, y_ssd, xbc, proj, CD_Z // MB_INNER, d_skip_x, mb_norm_g[0],
                    cd_w_out[0].astype(BF16), gt_m, n_ctx)
    return _moe(h, norm_ffn_g[1], sh_f, sc_f, gt_f, router_w_pad, router_b,
                1, moe_w_gate, moe_w_up, moe_w_down, 0, final_g=norm_out_g)
```

```python
import functools
import math

import numpy as np
import jax
import jax.numpy as jnp
from jax import lax
from jax.experimental import pallas as pl
from jax.experimental.pallas import tpu as pltpu
from jax.experimental.pallas import tpu_sc as plsc

NORM_EPS = 1e-6
GLA_HEADS, GLA_DK, GLA_DV, GLA_LOW_RANK, GLA_TAU = 4, 64, 128, 16, 16.0
GLA_KEY_W, GLA_VAL_W = GLA_HEADS * GLA_DK, GLA_HEADS * GLA_DV
HG_HEADS, HG_EXPAND, HG_DV = 4, 128, 128
HG_KEY_W, HG_VAL_W = HG_HEADS * HG_EXPAND, HG_HEADS * HG_DV
HY_CH, HY_ORDER, HY_SHORT, HY_BANDS, HY_FILT_HID = 512, 2, 3, 16, 64
HY_MIN_DECAY = math.log(1e-2) / 1.5
HY_MAX_DECAY = math.log(1e-2) / 0.3
MB_HEADS, MB_HEAD_DIM, MB_GROUPS, MB_STATE = 8, 64, 2, 128
MB_INNER = MB_HEADS * MB_HEAD_DIM
MB_BC_W = MB_GROUPS * MB_STATE
N_EXPERTS, N_GROUPS, TOP_K, MOE_BLOCK = 16, 4, 2, 256
EXPERTS_PER_GROUP = N_EXPERTS // N_GROUPS

LANES = 128
SCAN_CHUNK = 64
SCAN_BLOCK = 128
SSD_CHUNK = 128
ROW_TILE = 256
VMEM_LIMIT = 56 * 1024 * 1024

BF16 = jnp.bfloat16
F32 = jnp.float32


def _params(*sem):
    return pltpu.CompilerParams(dimension_semantics=sem, vmem_limit_bytes=VMEM_LIMIT)


def _split3(x):
    hi = x.astype(BF16)
    r1 = x - hi.astype(F32)
    mid = r1.astype(BF16)
    lo = (r1 - mid.astype(F32)).astype(BF16)
    return hi, mid, lo


def _dot(a, b):
    return jnp.dot(a, b, preferred_element_type=F32)


def _dot_nt(a, b):
    return lax.dot_general(a, b, (((1,), (1,)), ((), ())), preferred_element_type=F32)


def _dot_tn(a, b):
    return lax.dot_general(a, b, (((0,), (0,)), ((), ())), preferred_element_type=F32)


def _sel_dot(m01, x):
    hi, mid, lo = _split3(x)
    return _dot(m01, hi) + (_dot(m01, mid) + _dot(m01, lo))


def _dot_sel(x, m01):
    hi, mid, lo = _split3(x)
    return _dot(hi, m01) + (_dot(mid, m01) + _dot(lo, m01))


def _dot_f32(a, b):
    ah = a.astype(BF16)
    al = (a - ah.astype(F32)).astype(BF16)
    bh = b.astype(BF16)
    bl = (b - bh.astype(F32)).astype(BF16)
    return _dot(ah, bh) + (_dot(ah, bl) + _dot(al, bh))


def _silu(x):
    return x * (1.0 / (1.0 + jnp.exp(-x)))


def _sigmoid(x):
    return 1.0 / (1.0 + jnp.exp(-x))


def _softplus(x):
    return jnp.maximum(x, 0.0) + jnp.log(1.0 + jnp.exp(-jnp.abs(x)))


def _pack_bf16_pairs(x):
    bits = lax.bitcast_convert_type(x.astype(BF16).astype(F32), jnp.uint32)
    half = x.shape[1] // 2
    return bits[:, :half] | (bits[:, half:] >> 16)


def _unpack_bf16_pairs(p):
    hi = lax.bitcast_convert_type(p & jnp.uint32(0xFFFF0000), F32)
    lo = lax.bitcast_convert_type(p << 16, F32)
    return jnp.concatenate([hi, lo], axis=1)


def _rms(x, g):
    return x * lax.rsqrt(jnp.mean(x * x, axis=-1, keepdims=True) + NORM_EPS) * g


def _adaln_kernel(c_ref, w_ref, b_ref, o_ref):
    o_ref[...] = _dot_f32(_silu(c_ref[...]), w_ref[...]) + b_ref[...]


def _adaln(cond, w, b):
    n_l, d, n6 = w.shape
    tn = 1536
    return pl.pallas_call(
        _adaln_kernel,
        grid=(n_l, n6 // tn),
        in_specs=[pl.BlockSpec((8, d), lambda l, j: (0, 0)),
                  pl.BlockSpec((None, d, tn), lambda l, j: (l, 0, j)),
                  pl.BlockSpec((None, 1, tn), lambda l, j: (l, 0, j))],
        out_specs=pl.BlockSpec((None, 8, tn), lambda l, j: (l, 0, j)),
        out_shape=jax.ShapeDtypeStruct((n_l, 8, n6), F32),
        compiler_params=_params("parallel", "parallel"),
        name="adaln",
    )(cond, w, b.reshape(n_l, 1, n6))


def _norm_proj_kernel(h_ref, g_ref, sh_ref, sc_ref, w_ref, o_ref):
    u = _rms(h_ref[...], g_ref[...]) * (1.0 + sc_ref[...]) + sh_ref[...]
    o_ref[...] = _dot(u.astype(BF16), w_ref[...])


def _norm_proj(h, g, shift, scale, w, n_ctx):
    bsz, t, d = h.shape
    n = w.shape[1]
    tm = ROW_TILE
    seg = lambda b, i: (b, (i * tm >= n_ctx).astype(jnp.int32), 0, 0)
    return pl.pallas_call(
        _norm_proj_kernel,
        grid=(bsz, t // tm),
        in_specs=[pl.BlockSpec((None, tm, d), lambda b, i: (b, i, 0)),
                  pl.BlockSpec((1, d), lambda b, i: (0, 0)),
                  pl.BlockSpec((None, None, 1, d), seg),
                  pl.BlockSpec((None, None, 1, d), seg),
                  pl.BlockSpec((d, n), lambda b, i: (0, 0))],
        out_specs=pl.BlockSpec((None, tm, n), lambda b, i: (b, i, 0)),
        out_shape=jax.ShapeDtypeStruct((bsz, t, n), F32),
        compiler_params=_params("parallel", "parallel"),
        name="norm_proj",
    )(h, g.reshape(1, d), shift, scale, w)


def _scan_constants(c, reverse):
    t = np.arange(c)[:, None]
    u = np.arange(c)[None, :]
    sels = [u <= t, u > t]
    masks = []
    m = c // 2
    while m >= 1:
        blk = t // (2 * m)
        upper_t = (t % (2 * m)) >= m
        r = blk * (2 * m) + m - 1
        s_blk = u // (2 * m)
        upper_s = (u % (2 * m)) >= m
        sels.append((upper_t & (u > r) & (u <= t)) | ((~upper_t) & (u > t) & (u <= r)))
        masks.append((blk == s_blk) & upper_t & (~upper_s))
        m //= 2
    masks.append(t == u)
    sel = np.stack(sels).astype(np.float32)
    msk = np.stack(masks).astype(np.float32)
    if reverse:
        sel = sel[:, ::-1, ::-1]
        msk = msk[:, ::-1, ::-1]
    return np.ascontiguousarray(sel.reshape(-1, c)), np.ascontiguousarray(msk)


def _chunk_order(i, n_ctx_chunks, n_chunks, reverse):
    if not reverse:
        return i
    return jnp.where(i < n_ctx_chunks, n_ctx_chunks - 1 - i, n_chunks - 1 - (i - n_ctx_chunks))


GROUP_KEYS = 256


def _decay_chunk(q, k, v, la, consts, st_ref, heads, dk, dv):
    sel_ref, mask_ref, hm_ref, hmb_ref, vm_ref = consts
    c = q.shape[0]
    n_lvl = mask_ref.shape[0] - 1
    hpg = GROUP_KEYS // dk
    cs = _dot(sel_ref[...], jnp.concatenate(_split3(la), axis=0))
    e_q = jnp.exp(cs[0:c])
    e_k = jnp.exp(cs[c:2 * c])
    e_tot = jnp.exp(jnp.sum(la, axis=0, keepdims=True))
    vb = v.astype(BF16)
    outs = []
    for g in range(heads // hpg):
        ks = slice(g * GROUP_KEYS, (g + 1) * GROUP_KEYS)
        vs = slice(g * hpg * dv, (g + 1) * hpg * dv)
        qg, kg = q[:, ks], k[:, ks]
        key_stack = lambda x: jnp.concatenate([x.astype(BF16) * hmb_ref[h] for h in range(hpg)], axis=0)
        att = mask_ref[n_lvl] * _dot_nt(qg.astype(BF16), key_stack(kg))
        for l in range(n_lvl):
            e = jnp.exp(cs[(2 + l) * c:(3 + l) * c, ks])
            att = att + mask_ref[l] * _dot_nt((qg * e).astype(BF16), key_stack(kg * e))
        v_blocks = jnp.concatenate([vb[:, vs] * vm_ref[h] for h in range(hpg)], axis=0)
        intra = _dot(att.astype(BF16), v_blocks)
        st = st_ref[g]
        q_stack = jnp.concatenate([(qg * e_q[:, ks]) * hm_ref[h] for h in range(hpg)], axis=0)
        inter = _dot_nt(q_stack.astype(BF16), st.astype(BF16))
        upd = _dot_tn(vb[:, vs], (kg * e_k[:, ks]).astype(BF16))
        new = st * e_tot[:, ks]
        for h in range(hpg):
            new = new + upd[h * dv:(h + 1) * dv] * hm_ref[h]
        st_ref[g] = new
        outs.append(intra + jnp.concatenate([inter[h * c:(h + 1) * c] for h in range(hpg)], axis=-1))
    return jnp.concatenate(outs, axis=-1)


def _log_sigmoid(x):
    return jnp.minimum(x, 0.0) - jnp.log(1.0 + jnp.exp(-jnp.abs(x)))


def _gla_kernel(*refs):
    ins, head_masks, (o_refs, st_refs) = (refs[0:8], refs[8:16]), refs[16:19], (refs[19:21], refs[21:23])

    @pl.when(pl.program_id(1) == 0)
    def _():
        for st_ref in st_refs:
            st_ref[...] = jnp.zeros_like(st_ref)

    for d, ((q_ref, k_ref, v_ref, lr_ref, gw_ref, gb_ref, sel_ref, mask_ref), o_ref, st_ref) in enumerate(
            zip(ins, o_refs, st_refs)):
        z = _dot_f32(lr_ref[...], gw_ref[...]) + gb_ref[...]
        la = _log_sigmoid(z) * (1.0 / GLA_TAU)
        q = q_ref[...] * (GLA_DK ** -0.5)
        k, v = k_ref[...], v_ref[...]
        for rows in _sub_chunks(q.shape[0], d == 1):
            o_ref[rows, :] = _decay_chunk(q[rows], k[rows], v[rows], la[rows], (sel_ref, mask_ref) + head_masks,
                                          st_ref, GLA_HEADS, GLA_DK, GLA_DV)


def _hgrn_kernel(*refs):
    ins, head_masks, (o_refs, st_refs) = (refs[0:6], refs[6:12]), refs[12:15], (refs[15:17], refs[17:19])

    @pl.when(pl.program_id(1) == 0)
    def _():
        for st_ref in st_refs:
            st_ref[...] = jnp.zeros_like(st_ref)

    for d, ((q_ref, f_ref, v_ref, lb_ref, sel_ref, mask_ref), o_ref, st_ref) in enumerate(
            zip(ins, o_refs, st_refs)):
        lb = lb_ref[...]
        f = lb + (1.0 - lb) * _sigmoid(f_ref[...])
        q, k, v, la = _silu(q_ref[...]), 1.0 - f, v_ref[...], jnp.log(f)
        for rows in _sub_chunks(q.shape[0], d == 1):
            o_ref[rows, :] = _decay_chunk(q[rows], k[rows], v[rows], la[rows], (sel_ref, mask_ref) + head_masks,
                                          st_ref, HG_HEADS, HG_EXPAND, HG_DV)


def _sub_chunks(rows, reverse):
    order = range(rows // SCAN_CHUNK)
    return [slice(j * SCAN_CHUNK, (j + 1) * SCAN_CHUNK) for j in (reversed(order) if reverse else order)]


def _scan_specs(blk, n_ctx, t, reverse, chunk=None, stacked_heads=1):
    n_blocks = t // blk
    order = functools.partial(_chunk_order, n_ctx_chunks=n_ctx // blk, n_chunks=n_blocks, reverse=reverse)

    def col(width, idx):
        return pl.BlockSpec((None, blk, width), lambda b, i: (b, order(i), idx))

    sel, msk = _scan_constants(chunk or blk, reverse)
    sel3 = np.concatenate([sel, sel, sel], axis=1)
    msk = np.tile(msk, (1, 1, stacked_heads))
    const = lambda a: pl.BlockSpec(a.shape, lambda b, i: (0,) * a.ndim)
    return n_blocks, col, const, jnp.asarray(sel3, BF16), jnp.asarray(msk, F32)


def _head_masks(dk, dv):
    hpg = GROUP_KEYS // dk
    hm = np.zeros((hpg, 1, GROUP_KEYS), np.float32)
    vm = np.zeros((hpg, 1, hpg * dv), np.float32)
    for h in range(hpg):
        hm[h, 0, h * dk:(h + 1) * dk] = 1.0
        vm[h, 0, h * dv:(h + 1) * dv] = 1.0
    return jnp.asarray(hm), jnp.asarray(hm, BF16), jnp.asarray(vm, BF16)


AB_Q, AB_K, AB_V, AB_G = 0, 256, 512, 1024
AB_HQ, AB_HF, AB_HI, AB_HG, AB_LR = 1536, 2048, 3072, 3584, 4096
AB_PAD_COLS = 4224


def _gla_scan(proj, gate_w_pad, gate_b, n_ctx):
    bsz, t, _ = proj.shape
    hpg = GROUP_KEYS // GLA_DK
    in_specs, args, outs = [], [], []
    for d in range(2):
        n_blocks, col, const, sel, msk = _scan_specs(SCAN_BLOCK, n_ctx, t, d == 1, SCAN_CHUNK, hpg)
        in_specs += [col(GLA_KEY_W, AB_Q // GLA_KEY_W), col(GLA_KEY_W, AB_K // GLA_KEY_W),
                     col(GLA_VAL_W, AB_V // GLA_VAL_W), col(LANES, AB_LR // LANES),
                     const(gate_w_pad[d]), const(gate_b[d]), const(sel), const(msk)]
        args += [proj, proj, proj, proj, gate_w_pad[d], gate_b[d], sel, msk]
        outs.append(col(GLA_VAL_W, 0))
    hm = _head_masks(GLA_DK, GLA_DV)
    return pl.pallas_call(
        _gla_kernel,
        grid=(bsz, n_blocks),
        in_specs=in_specs + [const(m) for m in hm],
        out_specs=outs,
        out_shape=[jax.ShapeDtypeStruct((bsz, t, GLA_VAL_W), F32)] * 2,
        scratch_shapes=[pltpu.VMEM((GLA_HEADS // hpg, GLA_DV, GROUP_KEYS), F32)] * 2,
        compiler_params=_params("parallel", "arbitrary"),
        name="gla_scan",
    )(*args, *hm)


def _hgrn_scan(proj, lb, n_ctx):
    bsz, t, _ = proj.shape
    hpg = GROUP_KEYS // HG_EXPAND
    in_specs, args, outs = [], [], []
    for d in range(2):
        n_blocks, col, const, sel, msk = _scan_specs(SCAN_BLOCK, n_ctx, t, d == 1, SCAN_CHUNK, hpg)
        in_specs += [col(HG_KEY_W, AB_HQ // HG_KEY_W), col(HG_KEY_W, AB_HF // HG_KEY_W + d),
                     col(HG_VAL_W, AB_HI // HG_VAL_W), const(lb[d]), const(sel), const(msk)]
        args += [proj, proj, proj, lb[d], sel, msk]
        outs.append(col(HG_VAL_W, 0))
    hm = _head_masks(HG_EXPAND, HG_DV)
    return pl.pallas_call(
        _hgrn_kernel,
        grid=(bsz, n_blocks),
        in_specs=in_specs + [const(m) for m in hm],
        out_specs=outs,
        out_shape=[jax.ShapeDtypeStruct((bsz, t, HG_VAL_W), F32)] * 2,
        scratch_shapes=[pltpu.VMEM((HG_HEADS // hpg, HG_DV, GROUP_KEYS), F32)] * 2,
        compiler_params=_params("parallel", "arbitrary"),
        name="hgrn_scan",
    )(*args, *hm)


def _mix_out_ab_kernel(h_ref, gf_ref, gb_ref, hf_ref, hb_ref, gg_ref, hg_ref, gn_ref, hn_ref,
                       w_ref, gt_ref, o_ref):
    feats = []
    for o, gate, g in ((gf_ref[...] + gb_ref[...], gg_ref[...], gn_ref[...]),
                       (hf_ref[...] + hb_ref[...], hg_ref[...], hn_ref[...])):
        for hd in range(o.shape[-1] // LANES):
            s = slice(hd * LANES, (hd + 1) * LANES)
            feats.append(_rms(o[:, s], g) * _silu(gate[:, s]))
    feat = jnp.concatenate(feats, axis=-1).astype(BF16)
    o_ref[...] = h_ref[...] + gt_ref[...] * _dot(feat, w_ref[...])


def _mix_out_ab(h, o_gla, o_hg, proj, gla_norm_g, hg_norm_g, w_out, gate, n_ctx):
    bsz, t, d = h.shape
    tm = ROW_TILE
    seg = lambda b, i: (b, (i * tm >= n_ctx).astype(jnp.int32), 0, 0)
    row = lambda width, idx: pl.BlockSpec((None, tm, width), lambda b, i: (b, i, idx))
    vec = pl.BlockSpec((1, LANES), lambda b, i: (0, 0))
    return pl.pallas_call(
        _mix_out_ab_kernel,
        grid=(bsz, t // tm),
        in_specs=[row(d, 0), row(GLA_VAL_W, 0), row(GLA_VAL_W, 0), row(HG_VAL_W, 0), row(HG_VAL_W, 0),
                  row(GLA_VAL_W, AB_G // GLA_VAL_W), row(HG_VAL_W, AB_HG // HG_VAL_W), vec, vec,
                  pl.BlockSpec(w_out.shape, lambda b, i: (0, 0)),
                  pl.BlockSpec((None, None, 1, d), seg)],
        out_specs=row(d, 0),
        out_shape=jax.ShapeDtypeStruct((bsz, t, d), F32),
        compiler_params=_params("parallel", "parallel"),
        name="mix_out_ab",
    )(h, o_gla[0], o_gla[1], o_hg[0], o_hg[1], proj, proj, gla_norm_g.reshape(1, -1),
      hg_norm_g.reshape(1, -1), w_out, gate)


def _dwconv_kernel(x_ref, prev_ref, next_ref, w_ref, b_ref, o_ref, *, seg_tiles, act):
    i = pl.program_id(1)
    tm = x_ref.shape[0]
    x = x_ref[...]
    first = functools.reduce(jnp.logical_or, [i == s for s in seg_tiles[:-1]])
    last = functools.reduce(jnp.logical_or, [i == s - 1 for s in seg_tiles[1:]])
    prev_row = jnp.where(first, 0.0, prev_ref[7:8, :])
    next_row = jnp.where(last, 0.0, next_ref[0:1, :])
    rows = lax.broadcasted_iota(jnp.int32, x.shape, 0)
    x_prev = jnp.where(rows == 0, prev_row, pltpu.roll(x, 1, axis=0))
    x_next = jnp.where(rows == tm - 1, next_row, pltpu.roll(x, tm - 1, axis=0))
    y = x_prev * w_ref[0:1, :] + x * w_ref[1:2, :] + x_next * w_ref[2:3, :] + b_ref[...]
    o_ref[...] = _silu(y) if act else y


def _dwconv(proj, col0, width, w, b, seg_bounds, act):
    bsz, t, _ = proj.shape
    tm = ROW_TILE
    off = seg_bounds[0] // tm
    n_tiles = t // tm - off
    seg_tiles = tuple(s // tm - off for s in seg_bounds)
    cb = col0 // width
    r8 = tm // 8
    last8 = t // 8 - 1
    kern = functools.partial(_dwconv_kernel, seg_tiles=seg_tiles, act=act)
    return pl.pallas_call(
        kern,
        grid=(bsz, n_tiles),
        in_specs=[pl.BlockSpec((None, tm, width), lambda bb, i: (bb, i + off, cb)),
                  pl.BlockSpec((None, 8, width), lambda bb, i: (bb, jnp.maximum((i + off) * r8 - 1, 0), cb)),
                  pl.BlockSpec((None, 8, width),
                               lambda bb, i: (bb, jnp.minimum((i + off + 1) * r8, last8), cb)),
                  pl.BlockSpec((3, width), lambda bb, i: (0, 0)),
                  pl.BlockSpec((1, width), lambda bb, i: (0, 0))],
        out_specs=pl.BlockSpec((None, tm, width), lambda bb, i: (bb, i, 0)),
        out_shape=jax.ShapeDtypeStruct((bsz, n_tiles * tm, width), F32),
        compiler_params=_params("parallel", "parallel"),
        name="dwconv",
    )(proj, proj, proj, w.T, b.reshape(1, -1))


def _ssd_kernel(*refs):
    ins, hexp_ref, o_refs, st_refs = (refs[0:7], refs[7:14]), refs[14], refs[15:17], refs[17:19]

    @pl.when(pl.program_id(1) == 0)
    def _():
        for st_ref in st_refs:
            st_ref[...] = jnp.zeros_like(st_ref)

    for (xbc_ref, dt_ref, bias_ref, alog_ref, mq_ref, mk_ref, mask_ref), o_ref, st_ref in zip(ins, o_refs, st_refs):
        _ssd_chunk(xbc_ref, dt_ref, bias_ref, alog_ref, hexp_ref, mq_ref, mk_ref, mask_ref, o_ref, st_ref)


def _ssd_chunk(xbc_ref, dt_ref, bias_ref, alog_ref, hexp_ref, mq_ref, mk_ref, mask_ref, o_ref, st_ref):
    c = xbc_ref.shape[0]
    hpg = MB_HEADS // MB_GROUPS
    gw = hpg * MB_HEAD_DIM
    dt = _softplus(dt_ref[...] + bias_ref[...])
    la = -dt * jnp.exp(alog_ref[...])
    cq = _sel_dot(mq_ref[...], la)
    ck = _sel_dot(mk_ref[...], la)
    cq_t = _dot_nt_sel(la, mq_ref[...])
    hexp = hexp_ref[...]
    dt_x = _dot_sel(dt, hexp)
    eq_x = jnp.exp(_dot_sel(cq, hexp))
    ek_x = jnp.exp(_dot_sel(ck, hexp))
    etot_x = jnp.exp(_dot_sel(jnp.sum(la, axis=0, keepdims=True), hexp))
    xs = xbc_ref[:, 0:MB_INNER] * dt_x
    mask = mask_ref[...]
    outs = []
    for g in range(MB_GROUPS):
        bm = xbc_ref[:, MB_INNER + g * MB_STATE:MB_INNER + (g + 1) * MB_STATE].astype(BF16)
        cm = xbc_ref[:, MB_INNER + MB_BC_W + g * MB_STATE:MB_INNER + MB_BC_W + (g + 1) * MB_STATE].astype(BF16)
        cb = _dot_nt(cm, bm)
        st = st_ref[g]
        gs = slice(g * gw, (g + 1) * gw)
        y_inter = _dot(cm, st.astype(BF16)) * eq_x[:, gs]
        for r in range(hpg):
            hd = g * hpg + r
            diff = cq[:, hd:hd + 1] - cq_t[hd:hd + 1, :]
            w = cb * jnp.exp(jnp.where(mask > 0.0, diff, -jnp.inf))
            ps = slice(hd * MB_HEAD_DIM, (hd + 1) * MB_HEAD_DIM)
            outs.append(_dot(w.astype(BF16), xs[:, ps].astype(BF16))
                        + y_inter[:, r * MB_HEAD_DIM:(r + 1) * MB_HEAD_DIM])
        st_ref[g] = st * etot_x[:, gs] + _dot_tn(bm, (xs[:, gs] * ek_x[:, gs]).astype(BF16))
    o_ref[...] = jnp.concatenate(outs, axis=-1)


def _dot_nt_sel(x, m01):
    hi, mid, lo = _split3(x)
    f = lambda p: lax.dot_general(p, m01, (((0,), (1,)), ((), ())), preferred_element_type=F32)
    return f(hi) + (f(mid) + f(lo))


def _ssd_scan(xbc, proj, dt_col, dt_bias, a_log, n_ctx):
    bsz, t, _ = xbc.shape
    c = SSD_CHUNK
    pad = lambda v: jnp.zeros((1, LANES), F32).at[0, :MB_HEADS].set(v)
    hexp = np.zeros((LANES, MB_INNER), np.float32)
    for hd in range(MB_HEADS):
        hexp[hd, hd * MB_HEAD_DIM:(hd + 1) * MB_HEAD_DIM] = 1.0
    hexp = jnp.asarray(hexp, BF16)
    tri = np.tril(np.ones((c, c), np.float32))
    in_specs, args, outs = [], [], []
    for d in range(2):
        n_chunks, col, const, sel, _ = _scan_specs(c, n_ctx, t, d == 1)
        mq, mk = sel[0:c, 0:c], sel[c:2 * c, 0:c]
        mask = jnp.asarray(tri[::-1, ::-1].copy() if d == 1 else tri)
        bias, alog = pad(dt_bias[d]), pad(a_log[d].astype(F32))
        in_specs += [col(xbc.shape[-1], 0), col(LANES, dt_col + d), const(bias), const(alog),
                     const(mq), const(mk), const(mask)]
        args += [xbc, proj, bias, alog, mq, mk, mask]
        outs.append(col(MB_INNER, 0))
    return pl.pallas_call(
        _ssd_kernel,
        grid=(bsz, n_chunks),
        in_specs=in_specs + [const(hexp)],
        out_specs=outs,
        out_shape=[jax.ShapeDtypeStruct((bsz, t, MB_INNER), F32)] * 2,
        scratch_shapes=[pltpu.VMEM((MB_GROUPS, MB_STATE, MB_INNER // MB_GROUPS), F32)] * 2,
        compiler_params=_params("parallel", "arbitrary"),
        name="ssd_scan",
    )(*args, hexp)


def _mix_out_cd_kernel(h_ref, hy_ref, yf_ref, yb_ref, xs_ref, z_ref, dsk_ref, ng_ref, w_ref, gt_ref, o_ref):
    y = (yf_ref[...] + yb_ref[...] + dsk_ref[...] * xs_ref[...]) * _silu(z_ref[...])
    gw = MB_INNER // MB_GROUPS
    ys = [_rms(y[:, g * gw:(g + 1) * gw], ng_ref[:, g * gw:(g + 1) * gw]) for g in range(MB_GROUPS)]
    feat = jnp.concatenate([hy_ref[...]] + ys, axis=-1).astype(BF16)
    o_ref[...] = h_ref[...] + gt_ref[...] * _dot(feat, w_ref[...])


def _mix_out_cd(h, hy, y_ssd, xbc, proj, z_col, d_skip_x, norm_g, w_out, gate, n_ctx):
    bsz, t, d = h.shape
    tm = ROW_TILE
    n_lat = t - n_ctx
    off = n_ctx // tm
    row = lambda width, idx: pl.BlockSpec((None, tm, width), lambda b, i: (b, i + off, idx))
    vec = pl.BlockSpec((1, MB_INNER), lambda b, i: (0, 0))
    return pl.pallas_call(
        _mix_out_cd_kernel,
        grid=(bsz, n_lat // tm),
        in_specs=[row(d, 0), pl.BlockSpec((None, tm, HY_CH), lambda b, i: (b, i, 0)),
                  row(MB_INNER, 0), row(MB_INNER, 0), row(MB_INNER, 0), row(MB_INNER, z_col), vec, vec,
                  pl.BlockSpec(w_out.shape, lambda b, i: (0, 0)),
                  pl.BlockSpec((None, None, 1, d), lambda b, i: (b, 1, 0, 0))],
        out_specs=pl.BlockSpec((None, tm, d), lambda b, i: (b, i, 0)),
        out_shape=jax.ShapeDtypeStruct((bsz, n_lat, d), F32),
        compiler_params=_params("parallel", "parallel"),
        name="mix_out_cd",
    )(h, hy, y_ssd[0], y_ssd[1], xbc, proj, d_skip_x, norm_g.reshape(1, -1), w_out, gate)


def _top2_of4(a, b, c, d):
    hi1, lo1, hi2, lo2 = jnp.maximum(a, b), jnp.minimum(a, b), jnp.maximum(c, d), jnp.minimum(c, d)
    return jnp.maximum(hi1, hi2) + jnp.maximum(jnp.minimum(hi1, hi2), jnp.maximum(lo1, lo2))


def _first_argmax(vals, skip=None):
    idx = None
    for j, vj in enumerate(vals):
        if idx is None and skip is None:
            idx, best = jnp.zeros(vj.shape, jnp.int32), vj
            continue
        if idx is None:
            idx, best = jnp.full(vj.shape, -1, jnp.int32), jnp.full(vj.shape, -jnp.inf, F32)
        take = vj > best
        if skip is not None:
            take = jnp.logical_and(take, skip != j)
        idx = jnp.where(take, j, idx)
        best = jnp.where(take, vj, best)
    return idx, best


def _ffn_pre_kernel(h_ref, g_ref, sh_ref, sc_ref, rw_ref, rb_ref, tri_ref,
                    v_ref, ri_ref, rwt_ref, cnt_ref, carry_ref):
    @pl.when(jnp.logical_and(pl.program_id(0) == 0, pl.program_id(1) == 0))
    def _():
        carry_ref[...] = jnp.zeros_like(carry_ref)

    v = _rms(h_ref[...], g_ref[...]) * (1.0 + sc_ref[...]) + sh_ref[...]
    v_ref[...] = _pack_bf16_pairs(v)
    st = _sigmoid(_dot_f32(v, rw_ref[...])).T[0:N_EXPERTS]
    sel = st + rb_ref[...]
    row = lambda a, e: a[e:e + 1]
    epg = EXPERTS_PER_GROUP
    gscore = [_top2_of4(*[row(sel, g * epg + j) for j in range(epg)]) for g in range(N_GROUPS)]
    best, _ = _first_argmax(gscore)

    def in_best(a, j):
        out = row(a, j)
        for g in range(1, N_GROUPS):
            out = jnp.where(best == g, row(a, g * epg + j), out)
        return out

    vals = [in_best(sel, j) for j in range(epg)]
    raw = [in_best(st, j) for j in range(epg)]
    i1, _ = _first_argmax(vals)
    i2, _ = _first_argmax(vals, skip=i1)
    pick = lambda i: functools.reduce(lambda acc, j: jnp.where(i == j, raw[j], acc), range(1, epg), raw[0])
    w1, w2 = pick(i1), pick(i2)
    wsum = w1 + w2
    e1, e2 = best * epg + i1, best * epg + i2

    experts = lax.broadcasted_iota(jnp.int32, st.shape, 0)
    oh1 = (experts == e1).astype(F32)
    oh2 = (experts == e2).astype(F32)
    cnt = oh1 + oh2
    before = _dot(cnt.astype(BF16), tri_ref[...]) + carry_ref[:, 0:1]
    ri_ref[0:1, :] = e1
    ri_ref[1:2, :] = e2
    ri_ref[2:3, :] = jnp.sum(oh1 * before, axis=0, keepdims=True).astype(jnp.int32)
    ri_ref[3:4, :] = jnp.sum(oh2 * before, axis=0, keepdims=True).astype(jnp.int32)
    ri_ref[4:8, :] = jnp.zeros((4, st.shape[1]), jnp.int32)
    rwt_ref[0:1, :] = w1 / wsum
    rwt_ref[1:2, :] = w2 / wsum
    rwt_ref[2:8, :] = jnp.zeros((6, st.shape[1]), F32)
    carry_ref[...] = carry_ref[...] + jnp.sum(cnt, axis=1, keepdims=True)
    cnt_ref[...] = carry_ref[...]


def _ffn_pre(h, g, shift, scale, router_w_pad, router_b, n_ctx):
    bsz, t, d = h.shape
    tm = ROW_TILE
    seg = lambda b, i: (b, (i * tm >= n_ctx).astype(jnp.int32), 0, 0)
    tri = jnp.asarray(np.triu(np.ones((tm, tm), np.float32), 1), BF16)
    return pl.pallas_call(
        _ffn_pre_kernel,
        grid=(bsz, t // tm),
        in_specs=[pl.BlockSpec((None, tm, d), lambda b, i: (b, i, 0)),
                  pl.BlockSpec((1, d), lambda b, i: (0, 0)),
                  pl.BlockSpec((None, None, 1, d), seg),
                  pl.BlockSpec((None, None, 1, d), seg),
                  pl.BlockSpec((d, LANES), lambda b, i: (0, 0)),
                  pl.BlockSpec((N_EXPERTS, 1), lambda b, i: (0, 0)),
                  pl.BlockSpec((tm, tm), lambda b, i: (0, 0))],
        out_specs=[pl.BlockSpec((None, tm, d // 2), lambda b, i: (b, i, 0)),
                   pl.BlockSpec((None, 8, tm), lambda b, i: (b, 0, i)),
                   pl.BlockSpec((None, 8, tm), lambda b, i: (b, 0, i)),
                   pl.BlockSpec((N_EXPERTS, LANES), lambda b, i: (0, 0))],
        out_shape=[jax.ShapeDtypeStruct((bsz, t, d // 2), jnp.uint32),
                   jax.ShapeDtypeStruct((bsz, 8, t), jnp.int32),
                   jax.ShapeDtypeStruct((bsz, 8, t), F32),
                   jax.ShapeDtypeStruct((N_EXPERTS, LANES), F32)],
        scratch_shapes=[pltpu.VMEM((N_EXPERTS, LANES), F32)],
        compiler_params=_params("arbitrary", "arbitrary"),
        name="ffn_pre",
    )(h, g.reshape(1, d), shift, scale, router_w_pad, router_b.reshape(N_EXPERTS, 1), tri)


def _experts_kernel(be_ref, nb_ref, x_ref, wg_ref, wu_ref, wd_ref, o_ref, wg_s, wu_s, wd_s):
    i = pl.program_id(0)
    prev = be_ref[jnp.maximum(i - 1, 0)]
    changed = jnp.logical_or(i == 0, be_ref[i] != prev)

    @pl.when(changed)
    def _():
        wg_s[...] = wg_ref[...].astype(BF16)
        wu_s[...] = wu_ref[...].astype(BF16)
        wd_s[...] = wd_ref[...].astype(BF16)

    @pl.when(i < nb_ref[0])
    def _():
        x = _unpack_bf16_pairs(x_ref[...]).astype(BF16)
        hid = _silu(_dot(x, wg_s[...])) * _dot(x, wu_s[...])
        o_ref[...] = _pack_bf16_pairs(_dot(hid.astype(BF16), wd_s[...]))

    @pl.when(i >= nb_ref[0])
    def _():
        o_ref[...] = jnp.zeros_like(o_ref)


def _experts(xb, block_e, n_used, layer, w_gate, w_up, w_down):
    n_slots = xb.shape[0]
    n_blocks = n_slots // MOE_BLOCK
    d, de = w_gate.shape[-2:]
    wspec = lambda shape: pl.BlockSpec((None, None) + shape, lambda i, be, nb: (layer, be[i], 0, 0))
    return pl.pallas_call(
        _experts_kernel,
        grid_spec=pltpu.PrefetchScalarGridSpec(
            num_scalar_prefetch=2,
            grid=(n_blocks,),
            in_specs=[pl.BlockSpec((MOE_BLOCK, d // 2), lambda i, be, nb: (i, 0)),
                      wspec((d, de)), wspec((d, de)), wspec((de, d))],
            out_specs=pl.BlockSpec((MOE_BLOCK, d // 2), lambda i, be, nb: (i, 0)),
            scratch_shapes=[pltpu.VMEM((d, de), BF16), pltpu.VMEM((d, de), BF16), pltpu.VMEM((de, d), BF16)]),
        out_shape=jax.ShapeDtypeStruct((n_slots, d // 2), jnp.uint32),
        compiler_params=_params("arbitrary"),
        name="moe_experts",
    )(block_e, n_used, xb, w_gate, w_up, w_down)


def _ffn_post_kernel(h_ref, y0_ref, y1_ref, w_ref, gt_ref, g_ref, o_ref, *, final):
    w = w_ref[...]
    y = w[:, 0:1] * _unpack_bf16_pairs(y0_ref[...]) + w[:, 1:2] * _unpack_bf16_pairs(y1_ref[...])
    out = h_ref[...] + gt_ref[...] * y
    o_ref[...] = _rms(out, g_ref[...]) if final else out


def _ffn_post(h, y, w, gate, n_ctx, final_g=None):
    bsz, t, d = h.shape
    tm = ROW_TILE
    seg = lambda b, i: (b, (i * tm >= n_ctx).astype(jnp.int32), 0, 0)
    row = lambda width: pl.BlockSpec((None, tm, width), lambda b, i: (b, i, 0))
    choice = lambda kk: pl.BlockSpec((None, None, tm, d // 2), lambda b, i: (kk, b, i, 0))
    final = final_g is not None
    g = final_g if final else jnp.ones((d,), F32)
    return pl.pallas_call(
        functools.partial(_ffn_post_kernel, final=final),
        grid=(bsz, t // tm),
        in_specs=[row(d), choice(0), choice(1), row(LANES), pl.BlockSpec((None, None, 1, d), seg),
                  pl.BlockSpec((1, d), lambda b, i: (0, 0))],
        out_specs=row(d),
        out_shape=jax.ShapeDtypeStruct((bsz, t, d), F32),
        compiler_params=_params("parallel", "parallel"),
        name="ffn_post",
    )(h, y, y, w, gate, g.reshape(1, d))


def _slot_layout(n, ri, counts):
    e = jnp.swapaxes(ri[:, 0:2], 0, 1).reshape(TOP_K, n)
    rank = jnp.swapaxes(ri[:, 2:4], 0, 1).reshape(TOP_K, n)
    padded = (counts + MOE_BLOCK - 1) // MOE_BLOCK * MOE_BLOCK
    pend = jnp.cumsum(padded)
    pstart = pend - padded
    experts = jnp.arange(N_EXPERTS, dtype=jnp.int32)
    dest = rank + jnp.sum(jnp.where(e[..., None] == experts, pstart, 0), axis=-1)
    n_slots = (n * TOP_K + MOE_BLOCK - 1) // MOE_BLOCK * MOE_BLOCK + N_EXPERTS * MOE_BLOCK
    n_blocks = n_slots // MOE_BLOCK
    blk0 = jnp.arange(n_blocks, dtype=jnp.int32)[:, None] * MOE_BLOCK
    block_e = jnp.minimum(jnp.sum((pend[None, :] <= blk0).astype(jnp.int32), axis=-1), N_EXPERTS - 1)
    n_used = (pend[-1] // MOE_BLOCK).astype(jnp.int32).reshape(1)
    return dest, n_slots, block_e.astype(jnp.int32), n_used


SC_CORES, SC_SUBCORES = 2, 16
SC_WINDOW = 32


def _gather_rows(table, idx):
    n_rows, d = idx.shape[0], table.shape[1]
    workers = SC_CORES * SC_SUBCORES
    per_worker = n_rows // workers
    assert per_worker * workers == n_rows and per_worker % SC_WINDOW == 0
    mesh = plsc.VectorSubcoreMesh(core_axis_name="c", subcore_axis_name="s")

    @functools.partial(
        pl.kernel, mesh=mesh,
        out_type=jax.ShapeDtypeStruct((n_rows, d), table.dtype),
        scratch_types=[pltpu.VMEM((SC_WINDOW,), jnp.int32), pltpu.VMEM((SC_WINDOW,), jnp.int32),
                       pltpu.VMEM((SC_WINDOW, d), table.dtype), pltpu.VMEM((SC_WINDOW, d), table.dtype),
                       pltpu.SemaphoreType.DMA, pltpu.SemaphoreType.DMA],
    )
    def gather_kernel(table_hbm, idx_hbm, out_hbm, idx0, idx1, rows0, rows1, sem0, sem1):
        base = (lax.axis_index("s") * SC_CORES + lax.axis_index("c")) * per_worker
        n_win = per_worker // SC_WINDOW
        slots = ((idx0, rows0, sem0), (idx1, rows1, sem1))
        window = lambda j: pl.ds(pl.multiple_of(base + j * SC_WINDOW, 8), SC_WINDOW)

        def start(j, slot):
            idx_v, rows_v, sem = slots[slot]
            pltpu.sync_copy(idx_hbm.at[window(j)], idx_v)
            pltpu.async_copy(table_hbm.at[idx_v], rows_v, sem)

        def finish(j, slot):
            idx_v, rows_v, sem = slots[slot]
            pltpu.make_async_copy(table_hbm.at[idx_v], rows_v, sem).wait()
            pltpu.sync_copy(rows_v, out_hbm.at[window(j)])

        start(0, 0)

        @pl.loop(0, n_win, step=2)
        def _(j):
            @pl.when(j + 1 < n_win)
            def _():
                start(j + 1, 1)

            finish(j, 0)

            @pl.when(j + 2 < n_win)
            def _():
                start(j + 2, 0)

            @pl.when(j + 1 < n_win)
            def _():
                finish(j + 1, 1)

    return gather_kernel(table, idx)


SC_SCATTER_WINDOW = 16


def _scatter_rows(src, dest, n_slots):
    n, d = src.shape
    workers = SC_CORES * SC_SUBCORES
    per_worker = n // workers
    win = SC_SCATTER_WINDOW
    assert per_worker * workers == n and per_worker % win == 0 and dest.shape == (TOP_K, n)
    mesh = plsc.VectorSubcoreMesh(core_axis_name="c", subcore_axis_name="s")

    @functools.partial(
        pl.kernel, mesh=mesh,
        out_type=jax.ShapeDtypeStruct((n_slots, d), src.dtype),
        scratch_types=[pltpu.VMEM((win,), jnp.int32), pltpu.VMEM((win,), jnp.int32),
                       pltpu.VMEM((win, d), src.dtype), pltpu.SemaphoreType.DMA, pltpu.SemaphoreType.DMA],
    )
    def scatter_kernel(src_hbm, dest_hbm, out_hbm, idx0, idx1, rows_v, sem0, sem1):
        base = (lax.axis_index("s") * SC_CORES + lax.axis_index("c")) * per_worker

        @pl.loop(0, per_worker // win)
        def _(j):
            rows = pl.ds(pl.multiple_of(base + j * win, 8), win)
            pltpu.sync_copy(src_hbm.at[rows], rows_v)
            pltpu.sync_copy(dest_hbm.at[0, rows], idx0)
            pltpu.sync_copy(dest_hbm.at[1, rows], idx1)
            first = pltpu.async_copy(rows_v, out_hbm.at[idx0], sem0)
            second = pltpu.async_copy(rows_v, out_hbm.at[idx1], sem1)
            first.wait()
            second.wait()

    return scatter_kernel(src, dest)


def _alongside(gather, idx, side_fn, side_in):
    idx, side_in = lax.optimization_barrier((idx, side_in))
    return lax.optimization_barrier((gather(idx), side_fn(side_in)))


def _moe(h, g, shift, scale, gate, router_w_pad, router_b, layer, w_gate, w_up, w_down, n_ctx,
         final_g=None, side=None):
    bsz, t, d = h.shape
    n = bsz * t
    v, ri, rwt, counts = _ffn_pre(h, g, shift, scale, router_w_pad, router_b, n_ctx)
    dest, n_slots, block_e, n_used = _slot_layout(n, ri, counts[:, 0].astype(jnp.int32))
    w = jnp.swapaxes(rwt[:, 0:2], 1, 2).reshape(n, TOP_K)
    dispatch = lambda idx: _scatter_rows(v.reshape(n, d // 2), idx, n_slots)
    if side is None:
        xb = dispatch(dest)
    else:
        xb, side_a = _alongside(dispatch, dest, *side[0])
    yb = _experts(xb, block_e, n_used, layer, w_gate, w_up, w_down)
    combine = lambda idx: _gather_rows(yb, idx)
    dest_flat = dest.reshape(-1)
    if side is None:
        y, side_b = combine(dest_flat), None
    else:
        y, side_b = _alongside(combine, dest_flat, side[1], side_a)
    wpad = jnp.zeros((n, LANES), F32).at[:, :TOP_K].set(w).reshape(bsz, t, LANES)
    out = _ffn_post(h, y.reshape(TOP_K, bsz, t, d // 2), wpad, gate, n_ctx, final_g)
    return out if side is None else (out, side_b)


DFT_STEP = 8


def _dft_tables(n):
    r, *mats = _dft_tables_np(n)
    return (r,) + tuple(jnp.asarray(a).astype(BF16) for a in mats)


@functools.lru_cache(maxsize=None)
def _dft_tables_np(n):
    size = 2 * n
    r = int(round(math.sqrt(size)))
    assert r * r == size and r % DFT_STEP == 0
    p1 = np.arange(r // 2)[None, None, :]
    p2 = np.arange(r)[:, None, None]
    k1 = np.arange(r)[None, :, None]
    ang = 2.0 * np.pi * (((r * p1 + p2) * k1) % size) / size
    g_re, g_im = np.cos(ang), -np.sin(ang)
    g_in = np.concatenate([g_re, g_im], axis=1)
    g_out = np.concatenate([np.swapaxes(g_re, 1, 2), np.swapaxes(g_im, 1, 2)], axis=2) / size
    a2 = 2.0 * np.pi * ((np.arange(r)[:, None] * np.arange(r)[None, :]) % r) / r
    f_re, f_im = np.cos(a2), -np.sin(a2)
    f_fwd = np.block([[f_re, -f_im], [f_im, f_re]])
    f_inv = np.block([[f_re, f_im], [-f_im, f_re]])
    p1f = np.arange(r)[None, None, :]
    angf = 2.0 * np.pi * (((r * p1f + p2) * k1) % size) / size
    g_full = np.concatenate([np.cos(angf), -np.sin(angf)], axis=1)
    return (r,) + tuple(a.astype(np.float32) for a in (g_in, g_out, f_fwd, f_inv, g_full))


def _dot_f32_tn(a, b):
    ah = a.astype(BF16)
    al = (a - ah.astype(F32)).astype(BF16)
    bh = b.astype(BF16)
    bl = (b - bh.astype(F32)).astype(BF16)
    return _dot_tn(ah, bh) + (_dot_tn(ah, bl) + _dot_tn(al, bh))


def _hy_filter_kernel(z_ref, t_ref, w1_ref, b1_ref, w2_ref, b2_ref, fr_ref, w3_ref, w3b_ref, rates_ref, o_ref,
                      *, half_tiles):
    i = pl.program_id(0)
    hid = jnp.sin(fr_ref[...] * (_dot_f32(w1_ref[...], z_ref[...]) + b1_ref[...]))
    hid = jnp.sin(fr_ref[...] * (_dot_f32(w2_ref[...], hid) + b2_ref[...]))
    filt = _dot_f32_tn(hid, w3_ref[...])
    decay = jnp.exp(-t_ref[...] * rates_ref[...])
    for o in range(o_ref.shape[0]):
        o_ref[o] = filt[:, o * HY_CH:(o + 1) * HY_CH] * decay

    @pl.when(i == 0)
    def _():
        extra = _dot_f32_tn(hid[:, 0:LANES], w3b_ref[...])[0:8]
        first = lax.broadcasted_iota(jnp.int32, (8, HY_CH), 0) == 0
        for o in range(o_ref.shape[0]):
            add = extra[:, o * HY_CH:(o + 1) * HY_CH] * decay[0:8]
            o_ref[o, 0:8, :] = o_ref[o, 0:8, :] + jnp.where(first, add, 0.0)

    @pl.when(i == half_tiles)
    def _():
        for o in range(o_ref.shape[0]):
            o_ref[o, 0:1, :] = jnp.zeros((1, HY_CH), F32)


HY_TILE = 512


def _hy_kernels(n, w1, b1, w2, b2, w3, freq):
    pos = np.arange(2 * n)
    pos = np.where(pos < n, pos, 2 * n - pos).astype(np.float32)
    t = jnp.asarray(pos / np.float32(n - 1))
    bands = jnp.linspace(1e-4, HY_BANDS - 1, HY_BANDS, dtype=F32)
    ang = (2.0 * math.pi / n) * bands[:, None] * jnp.asarray(pos)[None, :]
    z = jnp.concatenate([t[None, :], jnp.cos(ang), -jnp.sin(ang)], axis=0)
    z = jnp.pad(z, ((0, LANES - z.shape[0]), (0, 0)))
    w1t = jnp.pad(w1, ((0, LANES - w1.shape[0]), (0, 0))).T
    hidden = w1.shape[1]
    col = lambda v: v.reshape(hidden, 1)
    w3d = jnp.swapaxes(w3.reshape(hidden, HY_ORDER, 2, HY_CH), 0, 2)
    w3d = jnp.swapaxes(w3d, 1, 2).reshape(2, hidden, HY_ORDER * HY_CH)
    rates = jnp.abs(jnp.linspace(HY_MIN_DECAY, HY_MAX_DECAY, HY_CH, dtype=F32)).reshape(1, HY_CH)
    tm = HY_TILE
    half_tiles = n // tm
    full = lambda a: pl.BlockSpec(a.shape, lambda i: (0,) * a.ndim)
    small = (w1t, col(b1), w2.T, col(b2), col(freq))
    return pl.pallas_call(
        functools.partial(_hy_filter_kernel, half_tiles=half_tiles),
        grid=(2 * n // tm,),
        in_specs=[pl.BlockSpec((LANES, tm), lambda i: (0, i)), pl.BlockSpec((tm, 1), lambda i: (i, 0))]
                 + [full(a) for a in small]
                 + [pl.BlockSpec((None, hidden, HY_ORDER * HY_CH), lambda i: ((i >= half_tiles).astype(jnp.int32), 0, 0)),
                    pl.BlockSpec((None, hidden, HY_ORDER * HY_CH), lambda i: (1, 0, 0)), full(rates)],
        out_specs=pl.BlockSpec((HY_ORDER, tm, HY_CH), lambda i: (0, i, 0)),
        out_shape=jax.ShapeDtypeStruct((HY_ORDER, 2 * n, HY_CH), F32),
        compiler_params=_params("parallel"),
        name="hy_kernels",
    )(z, t[:, None], *small, w3d, w3d, rates)


def _pack_complex(z):
    r = z.shape[0] // 2
    bits = lax.bitcast_convert_type(z.astype(BF16).astype(F32), jnp.uint32)
    return lax.bitcast_convert_type(bits[0:r] | (bits[r:2 * r] >> 16), F32)


def _unpack_complex(words):
    p = lax.bitcast_convert_type(words, jnp.uint32)
    re = lax.bitcast_convert_type(p & jnp.uint32(0xFFFF0000), F32)
    im = lax.bitcast_convert_type(p << 16, F32)
    return jnp.concatenate([re, im], axis=0).astype(BF16)


def _load_every(ref, j, count):
    return ref.reshape(count * DFT_STEP, LANES)[pl.ds(j, count, stride=DFT_STEP), :]


def _store_every(ref, j, count, val):
    ref.reshape(count * DFT_STEP, LANES)[pl.ds(j, count, stride=DFT_STEP), :] = val


def _dft_in_kernel(x_ref, g_ref, a_ref):
    rh = x_ref.shape[0]
    for j in range(DFT_STEP):
        a_ref[j] = _pack_complex(_dot(g_ref[j], _load_every(x_ref, j, rh).astype(BF16)))


def _dft_in(x4, col, g_in):
    bx, rh, r, _ = x4.shape
    c = HY_CH
    cbs = c // LANES
    return pl.pallas_call(
        _dft_in_kernel,
        grid=(bx, r // DFT_STEP, cbs),
        in_specs=[pl.BlockSpec((None, rh, DFT_STEP, LANES), lambda b, i, cb: (b, 0, i, col * cbs + cb)),
                  pl.BlockSpec((DFT_STEP, 2 * r, rh), lambda b, i, cb: (i, 0, 0))],
        out_specs=pl.BlockSpec((None, DFT_STEP, r, LANES), lambda b, i, cb: (b, i, 0, cb)),
        out_shape=jax.ShapeDtypeStruct((bx, r, r, c), F32),
        compiler_params=_params("parallel", "parallel", "parallel"),
        name="dft_in",
    )(x4, g_in)


def _stage2_operand(a_ref, j, r):
    return jnp.concatenate([_unpack_complex(_load_every(a_ref.at[b], j, r)) for b in range(a_ref.shape[0])],
                           axis=1)


def _dft_filt_kernel(a_ref, f_ref, k_ref):
    r = f_ref.shape[0] // 2
    for j in range(DFT_STEP):
        s = _dot(f_ref[...], _stage2_operand(a_ref, j, r))
        for o in range(a_ref.shape[0]):
            k_ref[o, j] = s[:, o * LANES:(o + 1) * LANES]


def _dft_filt(a, f_fwd):
    nq, r, _, c = a.shape
    return pl.pallas_call(
        _dft_filt_kernel,
        grid=(r // DFT_STEP, c // LANES),
        in_specs=[pl.BlockSpec((nq, r, DFT_STEP, LANES), lambda i, cb: (0, 0, i, cb)),
                  pl.BlockSpec(f_fwd.shape, lambda i, cb: (0, 0))],
        out_specs=pl.BlockSpec((nq, DFT_STEP, 2 * r, LANES), lambda i, cb: (0, i, 0, cb)),
        out_shape=jax.ShapeDtypeStruct((nq, r, 2 * r, c), F32),
        compiler_params=_params("parallel", "parallel"),
        name="dft_filt",
    )(a, f_fwd)


def _dft_mid_kernel(a_ref, k_ref, ff_ref, fi_ref, b_ref):
    r = ff_ref.shape[0] // 2
    n_seq = a_ref.shape[0]
    for j in range(DFT_STEP):
        s = _dot(ff_ref[...], _stage2_operand(a_ref, j, r))
        sr, si = s[0:r], s[r:2 * r]
        kr = jnp.concatenate([k_ref[j, 0:r, :]] * n_seq, axis=1)
        ki = jnp.concatenate([k_ref[j, r:2 * r, :]] * n_seq, axis=1)
        p = jnp.concatenate([sr * kr - si * ki, sr * ki + si * kr], axis=0).astype(BF16)
        back = _dot(fi_ref[...], p)
        for b in range(n_seq):
            b_ref[b, j] = _pack_complex(back[:, b * LANES:(b + 1) * LANES])


def _dft_mid(a, kspec, order, f_fwd, f_inv):
    bsz, r, _, c = a.shape
    return pl.pallas_call(
        _dft_mid_kernel,
        grid=(r // DFT_STEP, c // LANES),
        in_specs=[pl.BlockSpec((bsz, r, DFT_STEP, LANES), lambda i, cb: (0, 0, i, cb)),
                  pl.BlockSpec((None, DFT_STEP, 2 * r, LANES), lambda i, cb: (order, i, 0, cb)),
                  pl.BlockSpec(f_fwd.shape, lambda i, cb: (0, 0)),
                  pl.BlockSpec(f_inv.shape, lambda i, cb: (0, 0))],
        out_specs=pl.BlockSpec((bsz, DFT_STEP, r, LANES), lambda i, cb: (0, i, 0, cb)),
        out_shape=jax.ShapeDtypeStruct((bsz, r, r, c), F32),
        compiler_params=_params("parallel", "parallel"),
        name="dft_mid",
    )(a, kspec, f_fwd, f_inv)


def _dft_out_kernel(b_ref, g_ref, u_ref, x_ref, bias_ref, o_ref):
    r, rh = b_ref.shape[0], o_ref.shape[0]
    for j in range(DFT_STEP):
        y = _dot(g_ref[j], _unpack_complex(_load_every(b_ref, j, r)))
        _store_every(o_ref, j, rh, _load_every(x_ref, j, rh) * (y + _load_every(u_ref, j, rh) * bias_ref[...]))


def _dft_out(bm, g_out, u4, u_col, x4, x_col, bias):
    bsz, r, _, c = bm.shape
    rh = r // 2
    cbs = c // LANES
    seq = lambda col: pl.BlockSpec((None, rh, DFT_STEP, LANES), lambda b, i, cb: (b, 0, i, col * cbs + cb))
    return pl.pallas_call(
        _dft_out_kernel,
        grid=(bsz, r // DFT_STEP, cbs),
        in_specs=[pl.BlockSpec((None, r, DFT_STEP, LANES), lambda b, i, cb: (b, 0, i, cb)),
                  pl.BlockSpec((DFT_STEP, rh, 2 * r), lambda b, i, cb: (i, 0, 0)),
                  seq(u_col), seq(x_col), pl.BlockSpec((1, LANES), lambda b, i, cb: (0, cb))],
        out_specs=seq(0),
        out_shape=jax.ShapeDtypeStruct((bsz, rh, r, c), F32),
        compiler_params=_params("parallel", "parallel", "parallel"),
        name="dft_out",
    )(bm, g_out, u4, x4, bias.reshape(1, c))


def _hyena_filter_stage1(n, filter_params):
    r, g_full = _dft_tables(n)[0], _dft_tables(n)[5]
    kern = _hy_kernels(n, *filter_params)
    return _dft_in(kern.reshape(-1, r, r, HY_CH), 0, g_full)


def _hyena_filter_spectra(n, stage1):
    return _dft_filt(stage1, _dft_tables(n)[3])


def _hyena(hy_in, kspec, conv_bias):
    bsz, n, _ = hy_in.shape
    r, g_in, g_out, f_fwd, f_inv, _ = _dft_tables(n)
    seq4 = hy_in.reshape(bsz, r // 2, r, 3 * HY_CH)
    zz = _dft_out(_dft_mid(_dft_in(seq4, 0, g_in), kspec, 0, f_fwd, f_inv), g_out,
                  seq4, 0, seq4, 1, conv_bias[0])
    out = _dft_out(_dft_mid(_dft_in(zz, 0, g_in), kspec, 1, f_fwd, f_inv), g_out,
                   zz, 0, seq4, 2, conv_bias[1])
    return out.reshape(bsz, n, HY_CH)


CD_HY, CD_Z, CD_XBC, CD_DT = 0, 1536, 2048, 3072
CD_PAD_COLS = CD_DT + 2 * LANES


def _reorder_ab(w):
    gq, gk, gv, gg, lr_f, lr_b, hq, hf_f, hf_b, hi, hg = jnp.split(
        w, np.cumsum([256, 256, 512, 512, 16, 16, 512, 512, 512, 512, 512])[:-1].tolist(), axis=-1)
    pad = jnp.zeros((w.shape[0], AB_PAD_COLS - AB_LR - 2 * GLA_LOW_RANK), w.dtype)
    return jnp.concatenate([gq, gk, gv, gg, hq, hf_f, hf_b, hi, hg, lr_f, lr_b, pad], axis=-1)


def _reorder_cd(w):
    hy, z, xbc, dt_f, dt_b = jnp.split(w, np.cumsum([1536, 512, 1024, 8, 8])[:-1].tolist(), axis=-1)
    pad = jnp.zeros((w.shape[0], LANES - MB_HEADS), w.dtype)
    return jnp.concatenate([hy, z, xbc, dt_f, pad, dt_b, pad], axis=-1)


def kernel(x, c, ctx, c_ctx, ada_w, ada_b, norm_mix_g, norm_ffn_g, norm_out_g, ab_w_in, ab_w_out, gla_gate_w, gla_gate_b, gla_norm_g, hg_lb, hg_norm_g, cd_w_in, cd_w_out, hy_short_w, hy_short_b, hy_w1, hy_b1, hy_w2, hy_b2, hy_w3, hy_freq, hy_bias, mb_conv_w, mb_conv_b, mb_dt_bias, mb_a_log, mb_d, mb_norm_g, router_w, router_b, moe_w_gate, moe_w_up, moe_w_down):
    bsz, n_lat, d = x.shape
    n_ctx = ctx.shape[1]
    t = n_ctx + n_lat
    assert ada_w.shape[0] == 2 and ab_w_in.shape[0] == 1 and cd_w_in.shape[0] == 1

    cond = jnp.zeros((8, d), F32).at[:bsz].set(c).at[bsz].set(c_ctx)
    m = _adaln(cond, ada_w, ada_b)

    def mods(layer):
        lat = m[layer, :bsz].reshape(bsz, 6, d)
        cx = jnp.broadcast_to(m[layer, bsz].reshape(1, 6, d), (bsz, 6, d))
        both = jnp.stack([cx, lat], axis=1)
        return [both[:, :, j][:, :, None, :] for j in range(6)]

    lb_all = jnp.cumsum(jax.nn.softmax(hg_lb.astype(F32), axis=1), axis=1)
    router_w_pad = jnp.zeros((d, LANES), F32).at[:, :N_EXPERTS].set(router_w)
    h = jnp.concatenate([ctx, x], axis=1)

    sh_m, sc_m, gt_m, sh_f, sc_f, gt_f = mods(0)
    proj = _norm_proj(h, norm_mix_g[0], sh_m, sc_m, _reorder_ab(ab_w_in[0]).astype(BF16), n_ctx)
    gwp = [jnp.zeros((LANES, GLA_KEY_W), F32).at[GLA_LOW_RANK * dd:GLA_LOW_RANK * (dd + 1)].set(gla_gate_w[0, dd])
           for dd in range(2)]
    o_gla = _gla_scan(proj, gwp, [gla_gate_b[0, dd].reshape(1, -1) for dd in range(2)], n_ctx)
    o_hg = _hgrn_scan(proj, [lb_all[dd, 0].reshape(1, -1) for dd in range(2)], n_ctx)
    h = _mix_out_ab(h, o_gla, o_hg, proj, gla_norm_g[0], hg_norm_g[0], ab_w_out[0].astype(BF16), gt_m, n_ctx)
    filter_params = (hy_w1[0], hy_b1[0], hy_w2[0], hy_b2[0], hy_w3[0], hy_freq[0])
    side = ((functools.partial(_hyena_filter_stage1, n_lat), filter_params),
            functools.partial(_hyena_filter_spectra, n_lat))
    h, kspec = _moe(h, norm_ffn_g[0], sh_f, sc_f, gt_f, router_w_pad, router_b,
                    0, moe_w_gate, moe_w_up, moe_w_down, n_ctx, side=side)

    sh_m, sc_m, gt_m, sh_f, sc_f, gt_f = mods(1)
    proj = _norm_proj(h, norm_mix_g[1], sh_m, sc_m, _reorder_cd(cd_w_in[0]).astype(BF16), n_ctx)
    hy_in = _dwconv(proj, CD_HY, 3 * HY_CH, hy_short_w[0], hy_short_b[0], (n_ctx, t), act=False)
    hy = _hyena(hy_in, kspec, hy_bias[0])
    xbc = _dwconv(proj, CD_XBC, MB_INNER + 2 * MB_BC_W, mb_conv_w[0], mb_conv_b[0], (0, n_ctx, t), act=True)
    y_ssd = _ssd_scan(xbc, proj, CD_DT // LANES, mb_dt_bias[0], mb_a_log[0], n_ctx)
    d_skip_x = jnp.repeat(mb_d[0], MB_HEAD_DIM).reshape(1, MB_INNER)
    h = _mix_out_cd(h, hy, y_ssd, xbc, proj, CD_Z // MB_INNER, d_skip_x, mb_norm_g[0],
                    cd_w_out[0].astype(BF16), gt_m, n_ctx)
    return _moe(h, norm_ffn_g[1], sh_f, sc_f, gt_f, router_w_pad, router_b,
                1, moe_w_gate, moe_w_up, moe_w_down, 0, final_g=norm_out_g)
```

```python
import functools
import math

import numpy as np
import jax
import jax.numpy as jnp
from jax import lax
from jax.experimental import pallas as pl
from jax.experimental.pallas import tpu as pltpu
from jax.experimental.pallas import tpu_sc as plsc

NORM_EPS = 1e-6
GLA_HEADS, GLA_DK, GLA_DV, GLA_LOW_RANK, GLA_TAU = 4, 64, 128, 16, 16.0
GLA_KEY_W, GLA_VAL_W = GLA_HEADS * GLA_DK, GLA_HEADS * GLA_DV
HG_HEADS, HG_EXPAND, HG_DV = 4, 128, 128
HG_KEY_W, HG_VAL_W = HG_HEADS * HG_EXPAND, HG_HEADS * HG_DV
HY_CH, HY_ORDER, HY_SHORT, HY_BANDS, HY_FILT_HID = 512, 2, 3, 16, 64
HY_MIN_DECAY = math.log(1e-2) / 1.5
HY_MAX_DECAY = math.log(1e-2) / 0.3
MB_HEADS, MB_HEAD_DIM, MB_GROUPS, MB_STATE = 8, 64, 2, 128
MB_INNER = MB_HEADS * MB_HEAD_DIM
MB_BC_W = MB_GROUPS * MB_STATE
N_EXPERTS, N_GROUPS, TOP_K, MOE_BLOCK = 16, 4, 2, 256
EXPERTS_PER_GROUP = N_EXPERTS // N_GROUPS

LANES = 128
SCAN_CHUNK = 64
SCAN_BLOCK = 128
SSD_CHUNK = 128
ROW_TILE = 256
VMEM_LIMIT = 56 * 1024 * 1024

BF16 = jnp.bfloat16
F32 = jnp.float32


def _params(*sem):
    return pltpu.CompilerParams(dimension_semantics=sem, vmem_limit_bytes=VMEM_LIMIT)


def _split3(x):
    hi = x.astype(BF16)
    r1 = x - hi.astype(F32)
    mid = r1.astype(BF16)
    lo = (r1 - mid.astype(F32)).astype(BF16)
    return hi, mid, lo


def _dot(a, b):
    return jnp.dot(a, b, preferred_element_type=F32)


def _dot_nt(a, b):
    return lax.dot_general(a, b, (((1,), (1,)), ((), ())), preferred_element_type=F32)


def _dot_tn(a, b):
    return lax.dot_general(a, b, (((0,), (0,)), ((), ())), preferred_element_type=F32)


def _sel_dot(m01, x):
    hi, mid, lo = _split3(x)
    return _dot(m01, hi) + (_dot(m01, mid) + _dot(m01, lo))


def _dot_sel(x, m01):
    hi, mid, lo = _split3(x)
    return _dot(hi, m01) + (_dot(mid, m01) + _dot(lo, m01))


def _dot_f32(a, b):
    ah = a.astype(BF16)
    al = (a - ah.astype(F32)).astype(BF16)
    bh = b.astype(BF16)
    bl = (b - bh.astype(F32)).astype(BF16)
    return _dot(ah, bh) + (_dot(ah, bl) + _dot(al, bh))


def _silu(x):
    return x * (1.0 / (1.0 + jnp.exp(-x)))


def _sigmoid(x):
    return 1.0 / (1.0 + jnp.exp(-x))


def _softplus(x):
    return jnp.maximum(x, 0.0) + jnp.log(1.0 + jnp.exp(-jnp.abs(x)))


def _pack_bf16_pairs(x):
    bits = lax.bitcast_convert_type(x.astype(BF16).astype(F32), jnp.uint32)
    half = x.shape[1] // 2
    return bits[:, :half] | (bits[:, half:] >> 16)


def _unpack_bf16_pairs(p):
    hi = lax.bitcast_convert_type(p & jnp.uint32(0xFFFF0000), F32)
    lo = lax.bitcast_convert_type(p << 16, F32)
    return jnp.concatenate([hi, lo], axis=1)


def _rms(x, g):
    return x * lax.rsqrt(jnp.mean(x * x, axis=-1, keepdims=True) + NORM_EPS) * g


def _adaln_kernel(c_ref, w_ref, b_ref, o_ref):
    o_ref[...] = _dot_f32(_silu(c_ref[...]), w_ref[...]) + b_ref[...]


def _adaln(cond, w, b):
    n_l, d, n6 = w.shape
    tn = 1536
    return pl.pallas_call(
        _adaln_kernel,
        grid=(n_l, n6 // tn),
        in_specs=[pl.BlockSpec((8, d), lambda l, j: (0, 0)),
                  pl.BlockSpec((None, d, tn), lambda l, j: (l, 0, j)),
                  pl.BlockSpec((None, 1, tn), lambda l, j: (l, 0, j))],
        out_specs=pl.BlockSpec((None, 8, tn), lambda l, j: (l, 0, j)),
        out_shape=jax.ShapeDtypeStruct((n_l, 8, n6), F32),
        compiler_params=_params("parallel", "parallel"),
        name="adaln",
    )(cond, w, b.reshape(n_l, 1, n6))


def _norm_proj_kernel(h_ref, g_ref, sh_ref, sc_ref, w_ref, o_ref):
    u = _rms(h_ref[...], g_ref[...]) * (1.0 + sc_ref[...]) + sh_ref[...]
    o_ref[...] = _dot(u.astype(BF16), w_ref[...])


def _norm_proj(h, g, shift, scale, w, n_ctx):
    bsz, t, d = h.shape
    n = w.shape[1]
    tm = ROW_TILE
    seg = lambda b, i: (b, (i * tm >= n_ctx).astype(jnp.int32), 0, 0)
    return pl.pallas_call(
        _norm_proj_kernel,
        grid=(bsz, t // tm),
        in_specs=[pl.BlockSpec((None, tm, d), lambda b, i: (b, i, 0)),
                  pl.BlockSpec((1, d), lambda b, i: (0, 0)),
                  pl.BlockSpec((None, None, 1, d), seg),
                  pl.BlockSpec((None, None, 1, d), seg),
                  pl.BlockSpec((d, n), lambda b, i: (0, 0))],
        out_specs=pl.BlockSpec((None, tm, n), lambda b, i: (b, i, 0)),
        out_shape=jax.ShapeDtypeStruct((bsz, t, n), F32),
        compiler_params=_params("parallel", "parallel"),
        name="norm_proj",
    )(h, g.reshape(1, d), shift, scale, w)


def _scan_constants(c, reverse):
    t = np.arange(c)[:, None]
    u = np.arange(c)[None, :]
    sels = [u <= t, u > t]
    masks = []
    m = c // 2
    while m >= 1:
        blk = t // (2 * m)
        upper_t = (t % (2 * m)) >= m
        r = blk * (2 * m) + m - 1
        s_blk = u // (2 * m)
        upper_s = (u % (2 * m)) >= m
        sels.append((upper_t & (u > r) & (u <= t)) | ((~upper_t) & (u > t) & (u <= r)))
        masks.append((blk == s_blk) & upper_t & (~upper_s))
        m //= 2
    masks.append(t == u)
    sel = np.stack(sels).astype(np.float32)
    msk = np.stack(masks).astype(np.float32)
    if reverse:
        sel = sel[:, ::-1, ::-1]
        msk = msk[:, ::-1, ::-1]
    return np.ascontiguousarray(sel.reshape(-1, c)), np.ascontiguousarray(msk)


def _chunk_order(i, n_ctx_chunks, n_chunks, reverse):
    if not reverse:
        return i
    return jnp.where(i < n_ctx_chunks, n_ctx_chunks - 1 - i, n_chunks - 1 - (i - n_ctx_chunks))


GROUP_KEYS = 256


def _decay_chunk(q, k, v, la, consts, st_ref, heads, dk, dv):
    sel_ref, mask_ref, hm_ref, hmb_ref, vm_ref = consts
    c = q.shape[0]
    n_lvl = mask_ref.shape[0] - 1
    hpg = GROUP_KEYS // dk
    cs = _dot(sel_ref[...], jnp.concatenate(_split3(la), axis=0))
    e_q = jnp.exp(cs[0:c])
    e_k = jnp.exp(cs[c:2 * c])
    e_tot = jnp.exp(jnp.sum(la, axis=0, keepdims=True))
    vb = v.astype(BF16)
    outs = []
    for g in range(heads // hpg):
        ks = slice(g * GROUP_KEYS, (g + 1) * GROUP_KEYS)
        vs = slice(g * hpg * dv, (g + 1) * hpg * dv)
        qg, kg = q[:, ks], k[:, ks]
        key_stack = lambda x: jnp.concatenate([x.astype(BF16) * hmb_ref[h] for h in range(hpg)], axis=0)
        att = mask_ref[n_lvl] * _dot_nt(qg.astype(BF16), key_stack(kg))
        for l in range(n_lvl):
            e = jnp.exp(cs[(2 + l) * c:(3 + l) * c, ks])
            att = att + mask_ref[l] * _dot_nt((qg * e).astype(BF16), key_stack(kg * e))
        v_blocks = jnp.concatenate([vb[:, vs] * vm_ref[h] for h in range(hpg)], axis=0)
        intra = _dot(att.astype(BF16), v_blocks)
        st = st_ref[g]
        q_stack = jnp.concatenate([(qg * e_q[:, ks]) * hm_ref[h] for h in range(hpg)], axis=0)
        inter = _dot_nt(q_stack.astype(BF16), st.astype(BF16))
        upd = _dot_tn(vb[:, vs], (kg * e_k[:, ks]).astype(BF16))
        new = st * e_tot[:, ks]
        for h in range(hpg):
            new = new + upd[h * dv:(h + 1) * dv] * hm_ref[h]
        st_ref[g] = new
        outs.append(intra + jnp.concatenate([inter[h * c:(h + 1) * c] for h in range(hpg)], axis=-1))
    return jnp.concatenate(outs, axis=-1)


def _log_sigmoid(x):
    return jnp.minimum(x, 0.0) - jnp.log(1.0 + jnp.exp(-jnp.abs(x)))


def _gla_kernel(*refs):
    ins, head_masks, (o_refs, st_refs) = (refs[0:8], refs[8:16]), refs[16:19], (refs[19:21], refs[21:23])

    @pl.when(pl.program_id(1) == 0)
    def _():
        for st_ref in st_refs:
            st_ref[...] = jnp.zeros_like(st_ref)

    for d, ((q_ref, k_ref, v_ref, lr_ref, gw_ref, gb_ref, sel_ref, mask_ref), o_ref, st_ref) in enumerate(
            zip(ins, o_refs, st_refs)):
        z = _dot_f32(lr_ref[...], gw_ref[...]) + gb_ref[...]
        la = _log_sigmoid(z) * (1.0 / GLA_TAU)
        q = q_ref[...] * (GLA_DK ** -0.5)
        k, v = k_ref[...], v_ref[...]
        for rows in _sub_chunks(q.shape[0], d == 1):
            o_ref[rows, :] = _decay_chunk(q[rows], k[rows], v[rows], la[rows], (sel_ref, mask_ref) + head_masks,
                                          st_ref, GLA_HEADS, GLA_DK, GLA_DV)


def _hgrn_kernel(*refs):
    ins, head_masks, (o_refs, st_refs) = (refs[0:6], refs[6:12]), refs[12:15], (refs[15:17], refs[17:19])

    @pl.when(pl.program_id(1) == 0)
    def _():
        for st_ref in st_refs:
            st_ref[...] = jnp.zeros_like(st_ref)

    for d, ((q_ref, f_ref, v_ref, lb_ref, sel_ref, mask_ref), o_ref, st_ref) in enumerate(
            zip(ins, o_refs, st_refs)):
        lb = lb_ref[...]
        f = lb + (1.0 - lb) * _sigmoid(f_ref[...])
        q, k, v, la = _silu(q_ref[...]), 1.0 - f, v_ref[...], jnp.log(f)
        for rows in _sub_chunks(q.shape[0], d == 1):
            o_ref[rows, :] = _decay_chunk(q[rows], k[rows], v[rows], la[rows], (sel_ref, mask_ref) + head_masks,
                                          st_ref, HG_HEADS, HG_EXPAND, HG_DV)


def _sub_chunks(rows, reverse):
    order = range(rows // SCAN_CHUNK)
    return [slice(j * SCAN_CHUNK, (j + 1) * SCAN_CHUNK) for j in (reversed(order) if reverse else order)]


def _scan_specs(blk, n_ctx, t, reverse, chunk=None, stacked_heads=1):
    n_blocks = t // blk
    order = functools.partial(_chunk_order, n_ctx_chunks=n_ctx // blk, n_chunks=n_blocks, reverse=reverse)

    def col(width, idx):
        return pl.BlockSpec((None, blk, width), lambda b, i: (b, order(i), idx))

    sel, msk = _scan_constants(chunk or blk, reverse)
    sel3 = np.concatenate([sel, sel, sel], axis=1)
    msk = np.tile(msk, (1, 1, stacked_heads))
    const = lambda a: pl.BlockSpec(a.shape, lambda b, i: (0,) * a.ndim)
    return n_blocks, col, const, jnp.asarray(sel3, BF16), jnp.asarray(msk, F32)


def _head_masks(dk, dv):
    hpg = GROUP_KEYS // dk
    hm = np.zeros((hpg, 1, GROUP_KEYS), np.float32)
    vm = np.zeros((hpg, 1, hpg * dv), np.float32)
    for h in range(hpg):
        hm[h, 0, h * dk:(h + 1) * dk] = 1.0
        vm[h, 0, h * dv:(h + 1) * dv] = 1.0
    return jnp.asarray(hm), jnp.asarray(hm, BF16), jnp.asarray(vm, BF16)


AB_Q, AB_K, AB_V, AB_G = 0, 256, 512, 1024
AB_HQ, AB_HF, AB_HI, AB_HG, AB_LR = 1536, 2048, 3072, 3584, 4096
AB_PAD_COLS = 4224


def _gla_scan(proj, gate_w_pad, gate_b, n_ctx):
    bsz, t, _ = proj.shape
    hpg = GROUP_KEYS // GLA_DK
    in_specs, args, outs = [], [], []
    for d in range(2):
        n_blocks, col, const, sel, msk = _scan_specs(SCAN_BLOCK, n_ctx, t, d == 1, SCAN_CHUNK, hpg)
        in_specs += [col(GLA_KEY_W, AB_Q // GLA_KEY_W), col(GLA_KEY_W, AB_K // GLA_KEY_W),
                     col(GLA_VAL_W, AB_V // GLA_VAL_W), col(LANES, AB_LR // LANES),
                     const(gate_w_pad[d]), const(gate_b[d]), const(sel), const(msk)]
        args += [proj, proj, proj, proj, gate_w_pad[d], gate_b[d], sel, msk]
        outs.append(col(GLA_VAL_W, 0))
    hm = _head_masks(GLA_DK, GLA_DV)
    return pl.pallas_call(
        _gla_kernel,
        grid=(bsz, n_blocks),
        in_specs=in_specs + [const(m) for m in hm],
        out_specs=outs,
        out_shape=[jax.ShapeDtypeStruct((bsz, t, GLA_VAL_W), F32)] * 2,
        scratch_shapes=[pltpu.VMEM((GLA_HEADS // hpg, GLA_DV, GROUP_KEYS), F32)] * 2,
        compiler_params=_params("parallel", "arbitrary"),
        name="gla_scan",
    )(*args, *hm)


def _hgrn_scan(proj, lb, n_ctx):
    bsz, t, _ = proj.shape
    hpg = GROUP_KEYS // HG_EXPAND
    in_specs, args, outs = [], [], []
    for d in range(2):
        n_blocks, col, const, sel, msk = _scan_specs(SCAN_BLOCK, n_ctx, t, d == 1, SCAN_CHUNK, hpg)
        in_specs += [col(HG_KEY_W, AB_HQ // HG_KEY_W), col(HG_KEY_W, AB_HF // HG_KEY_W + d),
                     col(HG_VAL_W, AB_HI // HG_VAL_W), const(lb[d]), const(sel), const(msk)]
        args += [proj, proj, proj, lb[d], sel, msk]
        outs.append(col(HG_VAL_W, 0))
    hm = _head_masks(HG_EXPAND, HG_DV)
    return pl.pallas_call(
        _hgrn_kernel,
        grid=(bsz, n_blocks),
        in_specs=in_specs + [const(m) for m in hm],
        out_specs=outs,
        out_shape=[jax.ShapeDtypeStruct((bsz, t, HG_VAL_W), F32)] * 2,
        scratch_shapes=[pltpu.VMEM((HG_HEADS // hpg, HG_DV, GROUP_KEYS), F32)] * 2,
        compiler_params=_params("parallel", "arbitrary"),
        name="hgrn_scan",
    )(*args, *hm)


def _mix_out_ab_kernel(h_ref, gf_ref, gb_ref, hf_ref, hb_ref, gg_ref, hg_ref, gn_ref, hn_ref,
                       w_ref, gt_ref, o_ref):
    feats = []
    for o, gate, g in ((gf_ref[...] + gb_ref[...], gg_ref[...], gn_ref[...]),
                       (hf_ref[...] + hb_ref[...], hg_ref[...], hn_ref[...])):
        for hd in range(o.shape[-1] // LANES):
            s = slice(hd * LANES, (hd + 1) * LANES)
            feats.append(_rms(o[:, s], g) * _silu(gate[:, s]))
    feat = jnp.concatenate(feats, axis=-1).astype(BF16)
    o_ref[...] = h_ref[...] + gt_ref[...] * _dot(feat, w_ref[...])


def _mix_out_ab(h, o_gla, o_hg, proj, gla_norm_g, hg_norm_g, w_out, gate, n_ctx):
    bsz, t, d = h.shape
    tm = ROW_TILE
    seg = lambda b, i: (b, (i * tm >= n_ctx).astype(jnp.int32), 0, 0)
    row = lambda width, idx: pl.BlockSpec((None, tm, width), lambda b, i: (b, i, idx))
    vec = pl.BlockSpec((1, LANES), lambda b, i: (0, 0))
    return pl.pallas_call(
        _mix_out_ab_kernel,
        grid=(bsz, t // tm),
        in_specs=[row(d, 0), row(GLA_VAL_W, 0), row(GLA_VAL_W, 0), row(HG_VAL_W, 0), row(HG_VAL_W, 0),
                  row(GLA_VAL_W, AB_G // GLA_VAL_W), row(HG_VAL_W, AB_HG // HG_VAL_W), vec, vec,
                  pl.BlockSpec(w_out.shape, lambda b, i: (0, 0)),
                  pl.BlockSpec((None, None, 1, d), seg)],
        out_specs=row(d, 0),
        out_shape=jax.ShapeDtypeStruct((bsz, t, d), F32),
        compiler_params=_params("parallel", "parallel"),
        name="mix_out_ab",
    )(h, o_gla[0], o_gla[1], o_hg[0], o_hg[1], proj, proj, gla_norm_g.reshape(1, -1),
      hg_norm_g.reshape(1, -1), w_out, gate)


def _dwconv_kernel(x_ref, prev_ref, next_ref, w_ref, b_ref, o_ref, *, seg_tiles, act):
    i = pl.program_id(1)
    tm = x_ref.shape[0]
    x = x_ref[...]
    first = functools.reduce(jnp.logical_or, [i == s for s in seg_tiles[:-1]])
    last = functools.reduce(jnp.logical_or, [i == s - 1 for s in seg_tiles[1:]])
    prev_row = jnp.where(first, 0.0, prev_ref[7:8, :])
    next_row = jnp.where(last, 0.0, next_ref[0:1, :])
    rows = lax.broadcasted_iota(jnp.int32, x.shape, 0)
    x_prev = jnp.where(rows == 0, prev_row, pltpu.roll(x, 1, axis=0))
    x_next = jnp.where(rows == tm - 1, next_row, pltpu.roll(x, tm - 1, axis=0))
    y = x_prev * w_ref[0:1, :] + x * w_ref[1:2, :] + x_next * w_ref[2:3, :] + b_ref[...]
    o_ref[...] = _silu(y) if act else y


def _dwconv(proj, col0, width, w, b, seg_bounds, act):
    bsz, t, _ = proj.shape
    tm = ROW_TILE
    off = seg_bounds[0] // tm
    n_tiles = t // tm - off
    seg_tiles = tuple(s // tm - off for s in seg_bounds)
    cb = col0 // width
    r8 = tm // 8
    last8 = t // 8 - 1
    kern = functools.partial(_dwconv_kernel, seg_tiles=seg_tiles, act=act)
    return pl.pallas_call(
        kern,
        grid=(bsz, n_tiles),
        in_specs=[pl.BlockSpec((None, tm, width), lambda bb, i: (bb, i + off, cb)),
                  pl.BlockSpec((None, 8, width), lambda bb, i: (bb, jnp.maximum((i + off) * r8 - 1, 0), cb)),
                  pl.BlockSpec((None, 8, width),
                               lambda bb, i: (bb, jnp.minimum((i + off + 1) * r8, last8), cb)),
                  pl.BlockSpec((3, width), lambda bb, i: (0, 0)),
                  pl.BlockSpec((1, width), lambda bb, i: (0, 0))],
        out_specs=pl.BlockSpec((None, tm, width), lambda bb, i: (bb, i, 0)),
        out_shape=jax.ShapeDtypeStruct((bsz, n_tiles * tm, width), F32),
        compiler_params=_params("parallel", "parallel"),
        name="dwconv",
    )(proj, proj, proj, w.T, b.reshape(1, -1))


def _ssd_kernel(*refs):
    ins, hexp_ref, o_refs, st_refs = (refs[0:7], refs[7:14]), refs[14], refs[15:17], refs[17:19]

    @pl.when(pl.program_id(1) == 0)
    def _():
        for st_ref in st_refs:
            st_ref[...] = jnp.zeros_like(st_ref)

    for (xbc_ref, dt_ref, bias_ref, alog_ref, mq_ref, mk_ref, mask_ref), o_ref, st_ref in zip(ins, o_refs, st_refs):
        _ssd_chunk(xbc_ref, dt_ref, bias_ref, alog_ref, hexp_ref, mq_ref, mk_ref, mask_ref, o_ref, st_ref)


def _ssd_chunk(xbc_ref, dt_ref, bias_ref, alog_ref, hexp_ref, mq_ref, mk_ref, mask_ref, o_ref, st_ref):
    c = xbc_ref.shape[0]
    hpg = MB_HEADS // MB_GROUPS
    gw = hpg * MB_HEAD_DIM
    dt = _softplus(dt_ref[...] + bias_ref[...])
    la = -dt * jnp.exp(alog_ref[...])
    cq = _sel_dot(mq_ref[...], la)
    ck = _sel_dot(mk_ref[...], la)
    cq_t = _dot_nt_sel(la, mq_ref[...])
    hexp = hexp_ref[...]
    dt_x = _dot_sel(dt, hexp)
    eq_x = jnp.exp(_dot_sel(cq, hexp))
    ek_x = jnp.exp(_dot_sel(ck, hexp))
    etot_x = jnp.exp(_dot_sel(jnp.sum(la, axis=0, keepdims=True), hexp))
    xs = xbc_ref[:, 0:MB_INNER] * dt_x
    mask = mask_ref[...]
    outs = []
    for g in range(MB_GROUPS):
        bm = xbc_ref[:, MB_INNER + g * MB_STATE:MB_INNER + (g + 1) * MB_STATE].astype(BF16)
        cm = xbc_ref[:, MB_INNER + MB_BC_W + g * MB_STATE:MB_INNER + MB_BC_W + (g + 1) * MB_STATE].astype(BF16)
        cb = _dot_nt(cm, bm)
        st = st_ref[g]
        gs = slice(g * gw, (g + 1) * gw)
        y_inter = _dot(cm, st.astype(BF16)) * eq_x[:, gs]
        for r in range(hpg):
            hd = g * hpg + r
            diff = cq[:, hd:hd + 1] - cq_t[hd:hd + 1, :]
            w = cb * jnp.exp(jnp.where(mask > 0.0, diff, -jnp.inf))
            ps = slice(hd * MB_HEAD_DIM, (hd + 1) * MB_HEAD_DIM)
            outs.append(_dot(w.astype(BF16), xs[:, ps].astype(BF16))
                        + y_inter[:, r * MB_HEAD_DIM:(r + 1) * MB_HEAD_DIM])
        st_ref[g] = st * etot_x[:, gs] + _dot_tn(bm, (xs[:, gs] * ek_x[:, gs]).astype(BF16))
    o_ref[...] = jnp.concatenate(outs, axis=-1)


def _dot_nt_sel(x, m01):
    hi, mid, lo = _split3(x)
    f = lambda p: lax.dot_general(p, m01, (((0,), (1,)), ((), ())), preferred_element_type=F32)
    return f(hi) + (f(mid) + f(lo))


def _ssd_scan(xbc, proj, dt_col, dt_bias, a_log, n_ctx):
    bsz, t, _ = xbc.shape
    c = SSD_CHUNK
    pad = lambda v: jnp.zeros((1, LANES), F32).at[0, :MB_HEADS].set(v)
    hexp = np.zeros((LANES, MB_INNER), np.float32)
    for hd in range(MB_HEADS):
        hexp[hd, hd * MB_HEAD_DIM:(hd + 1) * MB_HEAD_DIM] = 1.0
    hexp = jnp.asarray(hexp, BF16)
    tri = np.tril(np.ones((c, c), np.float32))
    in_specs, args, outs = [], [], []
    for d in range(2):
        n_chunks, col, const, sel, _ = _scan_specs(c, n_ctx, t, d == 1)
        mq, mk = sel[0:c, 0:c], sel[c:2 * c, 0:c]
        mask = jnp.asarray(tri[::-1, ::-1].copy() if d == 1 else tri)
        bias, alog = pad(dt_bias[d]), pad(a_log[d].astype(F32))
        in_specs += [col(xbc.shape[-1], 0), col(LANES, dt_col + d), const(bias), const(alog),
                     const(mq), const(mk), const(mask)]
        args += [xbc, proj, bias, alog, mq, mk, mask]
        outs.append(col(MB_INNER, 0))
    return pl.pallas_call(
        _ssd_kernel,
        grid=(bsz, n_chunks),
        in_specs=in_specs + [const(hexp)],
        out_specs=outs,
        out_shape=[jax.ShapeDtypeStruct((bsz, t, MB_INNER), F32)] * 2,
        scratch_shapes=[pltpu.VMEM((MB_GROUPS, MB_STATE, MB_INNER // MB_GROUPS), F32)] * 2,
        compiler_params=_params("parallel", "arbitrary"),
        name="ssd_scan",
    )(*args, hexp)


def _mix_out_cd_kernel(h_ref, hy_ref, yf_ref, yb_ref, xs_ref, z_ref, dsk_ref, ng_ref, w_ref, gt_ref, o_ref):
    y = (yf_ref[...] + yb_ref[...] + dsk_ref[...] * xs_ref[...]) * _silu(z_ref[...])
    gw = MB_INNER // MB_GROUPS
    ys = [_rms(y[:, g * gw:(g + 1) * gw], ng_ref[:, g * gw:(g + 1) * gw]) for g in range(MB_GROUPS)]
    feat = jnp.concatenate([hy_ref[...]] + ys, axis=-1).astype(BF16)
    o_ref[...] = h_ref[...] + gt_ref[...] * _dot(feat, w_ref[...])


def _mix_out_cd(h, hy, y_ssd, xbc, proj, z_col, d_skip_x, norm_g, w_out, gate, n_ctx):
    bsz, t, d = h.shape
    tm = ROW_TILE
    n_lat = t - n_ctx
    off = n_ctx // tm
    row = lambda width, idx: pl.BlockSpec((None, tm, width), lambda b, i: (b, i + off, idx))
    vec = pl.BlockSpec((1, MB_INNER), lambda b, i: (0, 0))
    return pl.pallas_call(
        _mix_out_cd_kernel,
        grid=(bsz, n_lat // tm),
        in_specs=[row(d, 0), pl.BlockSpec((None, tm, HY_CH), lambda b, i: (b, i, 0)),
                  row(MB_INNER, 0), row(MB_INNER, 0), row(MB_INNER, 0), row(MB_INNER, z_col), vec, vec,
                  pl.BlockSpec(w_out.shape, lambda b, i: (0, 0)),
                  pl.BlockSpec((None, None, 1, d), lambda b, i: (b, 1, 0, 0))],
        out_specs=pl.BlockSpec((None, tm, d), lambda b, i: (b, i, 0)),
        out_shape=jax.ShapeDtypeStruct((bsz, n_lat, d), F32),
        compiler_params=_params("parallel", "parallel"),
        name="mix_out_cd",
    )(h, hy, y_ssd[0], y_ssd[1], xbc, proj, d_skip_x, norm_g.reshape(1, -1), w_out, gate)


def _top2_of4(a, b, c, d):
    hi1, lo1, hi2, lo2 = jnp.maximum(a, b), jnp.minimum(a, b), jnp.maximum(c, d), jnp.minimum(c, d)
    return jnp.maximum(hi1, hi2) + jnp.maximum(jnp.minimum(hi1, hi2), jnp.maximum(lo1, lo2))


def _first_argmax(vals, skip=None):
    idx = None
    for j, vj in enumerate(vals):
        if idx is None and skip is None:
            idx, best = jnp.zeros(vj.shape, jnp.int32), vj
            continue
        if idx is None:
            idx, best = jnp.full(vj.shape, -1, jnp.int32), jnp.full(vj.shape, -jnp.inf, F32)
        take = vj > best
        if skip is not None:
            take = jnp.logical_and(take, skip != j)
        idx = jnp.where(take, j, idx)
        best = jnp.where(take, vj, best)
    return idx, best


def _ffn_pre_kernel(h_ref, g_ref, sh_ref, sc_ref, rw_ref, rb_ref, tri_ref,
                    v_ref, ri_ref, rwt_ref, cnt_ref, carry_ref):
    @pl.when(jnp.logical_and(pl.program_id(0) == 0, pl.program_id(1) == 0))
    def _():
        carry_ref[...] = jnp.zeros_like(carry_ref)

    v = _rms(h_ref[...], g_ref[...]) * (1.0 + sc_ref[...]) + sh_ref[...]
    v_ref[...] = _pack_bf16_pairs(v)
    st = _sigmoid(_dot_f32(v, rw_ref[...])).T[0:N_EXPERTS]
    sel = st + rb_ref[...]
    row = lambda a, e: a[e:e + 1]
    epg = EXPERTS_PER_GROUP
    gscore = [_top2_of4(*[row(sel, g * epg + j) for j in range(epg)]) for g in range(N_GROUPS)]
    best, _ = _first_argmax(gscore)

    def in_best(a, j):
        out = row(a, j)
        for g in range(1, N_GROUPS):
            out = jnp.where(best == g, row(a, g * epg + j), out)
        return out

    vals = [in_best(sel, j) for j in range(epg)]
    raw = [in_best(st, j) for j in range(epg)]
    i1, _ = _first_argmax(vals)
    i2, _ = _first_argmax(vals, skip=i1)
    pick = lambda i: functools.reduce(lambda acc, j: jnp.where(i == j, raw[j], acc), range(1, epg), raw[0])
    w1, w2 = pick(i1), pick(i2)
    wsum = w1 + w2
    e1, e2 = best * epg + i1, best * epg + i2

    experts = lax.broadcasted_iota(jnp.int32, st.shape, 0)
    oh1 = (experts == e1).astype(F32)
    oh2 = (experts == e2).astype(F32)
    cnt = oh1 + oh2
    before = _dot(cnt.astype(BF16), tri_ref[...]) + carry_ref[:, 0:1]
    ri_ref[0:1, :] = e1
    ri_ref[1:2, :] = e2
    ri_ref[2:3, :] = jnp.sum(oh1 * before, axis=0, keepdims=True).astype(jnp.int32)
    ri_ref[3:4, :] = jnp.sum(oh2 * before, axis=0, keepdims=True).astype(jnp.int32)
    ri_ref[4:8, :] = jnp.zeros((4, st.shape[1]), jnp.int32)
    rwt_ref[0:1, :] = w1 / wsum
    rwt_ref[1:2, :] = w2 / wsum
    rwt_ref[2:8, :] = jnp.zeros((6, st.shape[1]), F32)
    carry_ref[...] = carry_ref[...] + jnp.sum(cnt, axis=1, keepdims=True)
    cnt_ref[...] = carry_ref[...]


def _ffn_pre(h, g, shift, scale, router_w_pad, router_b, n_ctx):
    bsz, t, d = h.shape
    tm = ROW_TILE
    seg = lambda b, i: (b, (i * tm >= n_ctx).astype(jnp.int32), 0, 0)
    tri = jnp.asarray(np.triu(np.ones((tm, tm), np.float32), 1), BF16)
    return pl.pallas_call(
        _ffn_pre_kernel,
        grid=(bsz, t // tm),
        in_specs=[pl.BlockSpec((None, tm, d), lambda b, i: (b, i, 0)),
                  pl.BlockSpec((1, d), lambda b, i: (0, 0)),
                  pl.BlockSpec((None, None, 1, d), seg),
                  pl.BlockSpec((None, None, 1, d), seg),
                  pl.BlockSpec((d, LANES), lambda b, i: (0, 0)),
                  pl.BlockSpec((N_EXPERTS, 1), lambda b, i: (0, 0)),
                  pl.BlockSpec((tm, tm), lambda b, i: (0, 0))],
        out_specs=[pl.BlockSpec((None, tm, d // 2), lambda b, i: (b, i, 0)),
                   pl.BlockSpec((None, 8, tm), lambda b, i: (b, 0, i)),
                   pl.BlockSpec((None, 8, tm), lambda b, i: (b, 0, i)),
                   pl.BlockSpec((N_EXPERTS, LANES), lambda b, i: (0, 0))],
        out_shape=[jax.ShapeDtypeStruct((bsz, t, d // 2), jnp.uint32),
                   jax.ShapeDtypeStruct((bsz, 8, t), jnp.int32),
                   jax.ShapeDtypeStruct((bsz, 8, t), F32),
                   jax.ShapeDtypeStruct((N_EXPERTS, LANES), F32)],
        scratch_shapes=[pltpu.VMEM((N_EXPERTS, LANES), F32)],
        compiler_params=_params("arbitrary", "arbitrary"),
        name="ffn_pre",
    )(h, g.reshape(1, d), shift, scale, router_w_pad, router_b.reshape(N_EXPERTS, 1), tri)


def _experts_kernel(be_ref, nb_ref, x_ref, wg_ref, wu_ref, wd_ref, o_ref, wg_s, wu_s, wd_s):
    i = pl.program_id(0)
    prev = be_ref[jnp.maximum(i - 1, 0)]
    changed = jnp.logical_or(i == 0, be_ref[i] != prev)

    @pl.when(changed)
    def _():
        wg_s[...] = wg_ref[...].astype(BF16)
        wu_s[...] = wu_ref[...].astype(BF16)
        wd_s[...] = wd_ref[...].astype(BF16)

    @pl.when(i < nb_ref[0])
    def _():
        x = _unpack_bf16_pairs(x_ref[...]).astype(BF16)
        hid = _silu(_dot(x, wg_s[...])) * _dot(x, wu_s[...])
        o_ref[...] = _pack_bf16_pairs(_dot(hid.astype(BF16), wd_s[...]))

    @pl.when(i >= nb_ref[0])
    def _():
        o_ref[...] = jnp.zeros_like(o_ref)


def _experts(xb, block_e, n_used, layer, w_gate, w_up, w_down):
    n_slots = xb.shape[0]
    n_blocks = n_slots // MOE_BLOCK
    d, de = w_gate.shape[-2:]
    wspec = lambda shape: pl.BlockSpec((None, None) + shape, lambda i, be, nb: (layer, be[i], 0, 0))
    return pl.pallas_call(
        _experts_kernel,
        grid_spec=pltpu.PrefetchScalarGridSpec(
            num_scalar_prefetch=2,
            grid=(n_blocks,),
            in_specs=[pl.BlockSpec((MOE_BLOCK, d // 2), lambda i, be, nb: (i, 0)),
                      wspec((d, de)), wspec((d, de)), wspec((de, d))],
            out_specs=pl.BlockSpec((MOE_BLOCK, d // 2), lambda i, be, nb: (i, 0)),
            scratch_shapes=[pltpu.VMEM((d, de), BF16), pltpu.VMEM((d, de), BF16), pltpu.VMEM((de, d), BF16)]),
        out_shape=jax.ShapeDtypeStruct((n_slots, d // 2), jnp.uint32),
        compiler_params=_params("arbitrary"),
        name="moe_experts",
    )(block_e, n_used, xb, w_gate, w_up, w_down)


def _ffn_post_kernel(h_ref, y0_ref, y1_ref, w_ref, gt_ref, g_ref, o_ref, *, final):
    w = w_ref[...]
    y = w[:, 0:1] * _unpack_bf16_pairs(y0_ref[...]) + w[:, 1:2] * _unpack_bf16_pairs(y1_ref[...])
    out = h_ref[...] + gt_ref[...] * y
    o_ref[...] = _rms(out, g_ref[...]) if final else out


def _ffn_post(h, y, w, gate, n_ctx, final_g=None):
    bsz, t, d = h.shape
    tm = ROW_TILE
    seg = lambda b, i: (b, (i * tm >= n_ctx).astype(jnp.int32), 0, 0)
    row = lambda width: pl.BlockSpec((None, tm, width), lambda b, i: (b, i, 0))
    choice = lambda kk: pl.BlockSpec((None, None, tm, d // 2), lambda b, i: (kk, b, i, 0))
    final = final_g is not None
    g = final_g if final else jnp.ones((d,), F32)
    return pl.pallas_call(
        functools.partial(_ffn_post_kernel, final=final),
        grid=(bsz, t // tm),
        in_specs=[row(d), choice(0), choice(1), row(LANES), pl.BlockSpec((None, None, 1, d), seg),
                  pl.BlockSpec((1, d), lambda b, i: (0, 0))],
        out_specs=row(d),
        out_shape=jax.ShapeDtypeStruct((bsz, t, d), F32),
        compiler_params=_params("parallel", "parallel"),
        name="ffn_post",
    )(h, y, y, w, gate, g.reshape(1, d))


def _slot_layout(n, ri, counts):
    e = jnp.swapaxes(ri[:, 0:2], 0, 1).reshape(TOP_K, n)
    rank = jnp.swapaxes(ri[:, 2:4], 0, 1).reshape(TOP_K, n)
    padded = (counts + MOE_BLOCK - 1) // MOE_BLOCK * MOE_BLOCK
    pend = jnp.cumsum(padded)
    pstart = pend - padded
    experts = jnp.arange(N_EXPERTS, dtype=jnp.int32)
    dest = rank + jnp.sum(jnp.where(e[..., None] == experts, pstart, 0), axis=-1)
    n_slots = (n * TOP_K + MOE_BLOCK - 1) // MOE_BLOCK * MOE_BLOCK + N_EXPERTS * MOE_BLOCK
    n_blocks = n_slots // MOE_BLOCK
    blk0 = jnp.arange(n_blocks, dtype=jnp.int32)[:, None] * MOE_BLOCK
    block_e = jnp.minimum(jnp.sum((pend[None, :] <= blk0).astype(jnp.int32), axis=-1), N_EXPERTS - 1)
    n_used = (pend[-1] // MOE_BLOCK).astype(jnp.int32).reshape(1)
    return dest, n_slots, block_e.astype(jnp.int32), n_used


SC_CORES, SC_SUBCORES = 2, 16
SC_WINDOW = 32


def _gather_rows(table, idx):
    n_rows, d = idx.shape[0], table.shape[1]
    workers = SC_CORES * SC_SUBCORES
    per_worker = n_rows // workers
    assert per_worker * workers == n_rows and per_worker % SC_WINDOW == 0
    mesh = plsc.VectorSubcoreMesh(core_axis_name="c", subcore_axis_name="s")

    @functools.partial(
        pl.kernel, mesh=mesh,
        out_type=jax.ShapeDtypeStruct((n_rows, d), table.dtype),
        scratch_types=[pltpu.VMEM((SC_WINDOW,), jnp.int32), pltpu.VMEM((SC_WINDOW,), jnp.int32),
                       pltpu.VMEM((SC_WINDOW, d), table.dtype), pltpu.VMEM((SC_WINDOW, d), table.dtype),
                       pltpu.SemaphoreType.DMA, pltpu.SemaphoreType.DMA],
    )
    def gather_kernel(table_hbm, idx_hbm, out_hbm, idx0, idx1, rows0, rows1, sem0, sem1):
        base = (lax.axis_index("s") * SC_CORES + lax.axis_index("c")) * per_worker
        n_win = per_worker // SC_WINDOW
        slots = ((idx0, rows0, sem0), (idx1, rows1, sem1))
        window = lambda j: pl.ds(pl.multiple_of(base + j * SC_WINDOW, 8), SC_WINDOW)

        def start(j, slot):
            idx_v, rows_v, sem = slots[slot]
            pltpu.sync_copy(idx_hbm.at[window(j)], idx_v)
            pltpu.async_copy(table_hbm.at[idx_v], rows_v, sem)

        def finish(j, slot):
            idx_v, rows_v, sem = slots[slot]
            pltpu.make_async_copy(table_hbm.at[idx_v], rows_v, sem).wait()
            pltpu.sync_copy(rows_v, out_hbm.at[window(j)])

        start(0, 0)

        @pl.loop(0, n_win, step=2)
        def _(j):
            @pl.when(j + 1 < n_win)
            def _():
                start(j + 1, 1)

            finish(j, 0)

            @pl.when(j + 2 < n_win)
            def _():
                start(j + 2, 0)

            @pl.when(j + 1 < n_win)
            def _():
                finish(j + 1, 1)

    return gather_kernel(table, idx)


SC_SCATTER_WINDOW = 16


def _scatter_rows(src, dest, n_slots):
    n, d = src.shape
    workers = SC_CORES * SC_SUBCORES
    per_worker = n // workers
    win = SC_SCATTER_WINDOW
    assert per_worker * workers == n and per_worker % win == 0 and dest.shape == (TOP_K, n)
    mesh = plsc.VectorSubcoreMesh(core_axis_name="c", subcore_axis_name="s")

    @functools.partial(
        pl.kernel, mesh=mesh,
        out_type=jax.ShapeDtypeStruct((n_slots, d), src.dtype),
        scratch_types=[pltpu.VMEM((win,), jnp.int32), pltpu.VMEM((win,), jnp.int32),
                       pltpu.VMEM((win, d), src.dtype), pltpu.SemaphoreType.DMA, pltpu.SemaphoreType.DMA],
    )
    def scatter_kernel(src_hbm, dest_hbm, out_hbm, idx0, idx1, rows_v, sem0, sem1):
        base = (lax.axis_index("s") * SC_CORES + lax.axis_index("c")) * per_worker

        @pl.loop(0, per_worker // win)
        def _(j):
            rows = pl.ds(pl.multiple_of(base + j * win, 8), win)
            pltpu.sync_copy(src_hbm.at[rows], rows_v)
            pltpu.sync_copy(dest_hbm.at[0, rows], idx0)
            pltpu.sync_copy(dest_hbm.at[1, rows], idx1)
            first = pltpu.async_copy(rows_v, out_hbm.at[idx0], sem0)
            second = pltpu.async_copy(rows_v, out_hbm.at[idx1], sem1)
            first.wait()
            second.wait()

    return scatter_kernel(src, dest)


def _alongside(gather, idx, side_fn, side_in):
    idx, side_in = lax.optimization_barrier((idx, side_in))
    return lax.optimization_barrier((gather(idx), side_fn(side_in)))


def _moe(h, g, shift, scale, gate, router_w_pad, router_b, layer, w_gate, w_up, w_down, n_ctx,
         final_g=None, side=None):
    bsz, t, d = h.shape
    n = bsz * t
    v, ri, rwt, counts = _ffn_pre(h, g, shift, scale, router_w_pad, router_b, n_ctx)
    dest, n_slots, block_e, n_used = _slot_layout(n, ri, counts[:, 0].astype(jnp.int32))
    w = jnp.swapaxes(rwt[:, 0:2], 1, 2).reshape(n, TOP_K)
    dispatch = lambda idx: _scatter_rows(v.reshape(n, d // 2), idx, n_slots)
    if side is None:
        xb = dispatch(dest)
    else:
        xb, side_a = _alongside(dispatch, dest, *side[0])
    yb = _experts(xb, block_e, n_used, layer, w_gate, w_up, w_down)
    combine = lambda idx: _gather_rows(yb, idx)
    dest_flat = dest.reshape(-1)
    if side is None:
        y, side_b = combine(dest_flat), None
    else:
        y, side_b = _alongside(combine, dest_flat, side[1], side_a)
    wpad = jnp.zeros((n, LANES), F32).at[:, :TOP_K].set(w).reshape(bsz, t, LANES)
    out = _ffn_post(h, y.reshape(TOP_K, bsz, t, d // 2), wpad, gate, n_ctx, final_g)
    return out if side is None else (out, side_b)


DFT_STEP = 16


def _dft_tables(n):
    r, *mats = _dft_tables_np(n)
    return (r,) + tuple(jnp.asarray(a).astype(BF16) for a in mats)


@functools.lru_cache(maxsize=None)
def _dft_tables_np(n):
    size = 2 * n
    r = int(round(math.sqrt(size)))
    assert r * r == size and r % DFT_STEP == 0
    p1 = np.arange(r // 2)[None, None, :]
    p2 = np.arange(r)[:, None, None]
    k1 = np.arange(r)[None, :, None]
    ang = 2.0 * np.pi * (((r * p1 + p2) * k1) % size) / size
    g_re, g_im = np.cos(ang), -np.sin(ang)
    g_in = np.concatenate([g_re, g_im], axis=1)
    g_out = np.concatenate([np.swapaxes(g_re, 1, 2), np.swapaxes(g_im, 1, 2)], axis=2) / size
    a2 = 2.0 * np.pi * ((np.arange(r)[:, None] * np.arange(r)[None, :]) % r) / r
    f_re, f_im = np.cos(a2), -np.sin(a2)
    f_fwd = np.block([[f_re, -f_im], [f_im, f_re]])
    f_inv = np.block([[f_re, f_im], [-f_im, f_re]])
    p1f = np.arange(r)[None, None, :]
    angf = 2.0 * np.pi * (((r * p1f + p2) * k1) % size) / size
    g_full = np.concatenate([np.cos(angf), -np.sin(angf)], axis=1)
    return (r,) + tuple(a.astype(np.float32) for a in (g_in, g_out, f_fwd, f_inv, g_full))


def _dot_f32_tn(a, b):
    ah = a.astype(BF16)
    al = (a - ah.astype(F32)).astype(BF16)
    bh = b.astype(BF16)
    bl = (b - bh.astype(F32)).astype(BF16)
    return _dot_tn(ah, bh) + (_dot_tn(ah, bl) + _dot_tn(al, bh))


def _hy_filter_kernel(z_ref, t_ref, w1_ref, b1_ref, w2_ref, b2_ref, fr_ref, w3_ref, w3b_ref, rates_ref, o_ref,
                      *, half_tiles):
    i = pl.program_id(0)
    hid = jnp.sin(fr_ref[...] * (_dot_f32(w1_ref[...], z_ref[...]) + b1_ref[...]))
    hid = jnp.sin(fr_ref[...] * (_dot_f32(w2_ref[...], hid) + b2_ref[...]))
    filt = _dot_f32_tn(hid, w3_ref[...])
    decay = jnp.exp(-t_ref[...] * rates_ref[...])
    for o in range(o_ref.shape[0]):
        o_ref[o] = filt[:, o * HY_CH:(o + 1) * HY_CH] * decay

    @pl.when(i == 0)
    def _():
        extra = _dot_f32_tn(hid[:, 0:LANES], w3b_ref[...])[0:8]
        first = lax.broadcasted_iota(jnp.int32, (8, HY_CH), 0) == 0
        for o in range(o_ref.shape[0]):
            add = extra[:, o * HY_CH:(o + 1) * HY_CH] * decay[0:8]
            o_ref[o, 0:8, :] = o_ref[o, 0:8, :] + jnp.where(first, add, 0.0)

    @pl.when(i == half_tiles)
    def _():
        for o in range(o_ref.shape[0]):
            o_ref[o, 0:1, :] = jnp.zeros((1, HY_CH), F32)


HY_TILE = 512


def _hy_kernels(n, w1, b1, w2, b2, w3, freq):
    pos = np.arange(2 * n)
    pos = np.where(pos < n, pos, 2 * n - pos).astype(np.float32)
    t = jnp.asarray(pos / np.float32(n - 1))
    bands = jnp.linspace(1e-4, HY_BANDS - 1, HY_BANDS, dtype=F32)
    ang = (2.0 * math.pi / n) * bands[:, None] * jnp.asarray(pos)[None, :]
    z = jnp.concatenate([t[None, :], jnp.cos(ang), -jnp.sin(ang)], axis=0)
    z = jnp.pad(z, ((0, LANES - z.shape[0]), (0, 0)))
    w1t = jnp.pad(w1, ((0, LANES - w1.shape[0]), (0, 0))).T
    hidden = w1.shape[1]
    col = lambda v: v.reshape(hidden, 1)
    w3d = jnp.swapaxes(w3.reshape(hidden, HY_ORDER, 2, HY_CH), 0, 2)
    w3d = jnp.swapaxes(w3d, 1, 2).reshape(2, hidden, HY_ORDER * HY_CH)
    rates = jnp.abs(jnp.linspace(HY_MIN_DECAY, HY_MAX_DECAY, HY_CH, dtype=F32)).reshape(1, HY_CH)
    tm = HY_TILE
    half_tiles = n // tm
    full = lambda a: pl.BlockSpec(a.shape, lambda i: (0,) * a.ndim)
    small = (w1t, col(b1), w2.T, col(b2), col(freq))
    return pl.pallas_call(
        functools.partial(_hy_filter_kernel, half_tiles=half_tiles),
        grid=(2 * n // tm,),
        in_specs=[pl.BlockSpec((LANES, tm), lambda i: (0, i)), pl.BlockSpec((tm, 1), lambda i: (i, 0))]
                 + [full(a) for a in small]
                 + [pl.BlockSpec((None, hidden, HY_ORDER * HY_CH), lambda i: ((i >= half_tiles).astype(jnp.int32), 0, 0)),
                    pl.BlockSpec((None, hidden, HY_ORDER * HY_CH), lambda i: (1, 0, 0)), full(rates)],
        out_specs=pl.BlockSpec((HY_ORDER, tm, HY_CH), lambda i: (0, i, 0)),
        out_shape=jax.ShapeDtypeStruct((HY_ORDER, 2 * n, HY_CH), F32),
        compiler_params=_params("parallel"),
        name="hy_kernels",
    )(z, t[:, None], *small, w3d, w3d, rates)


def _pack_complex(z):
    r = z.shape[0] // 2
    bits = lax.bitcast_convert_type(z.astype(BF16).astype(F32), jnp.uint32)
    return lax.bitcast_convert_type(bits[0:r] | (bits[r:2 * r] >> 16), F32)


def _unpack_complex(words):
    p = lax.bitcast_convert_type(words, jnp.uint32)
    re = lax.bitcast_convert_type(p & jnp.uint32(0xFFFF0000), F32)
    im = lax.bitcast_convert_type(p << 16, F32)
    return jnp.concatenate([re, im], axis=0).astype(BF16)


def _load_every(ref, j, count):
    return ref.reshape(count * DFT_STEP, LANES)[pl.ds(j, count, stride=DFT_STEP), :]


def _store_every(ref, j, count, val):
    ref.reshape(count * DFT_STEP, LANES)[pl.ds(j, count, stride=DFT_STEP), :] = val


def _dft_in_kernel(x_ref, g_ref, a_ref):
    rh = x_ref.shape[0]
    for j in range(DFT_STEP):
        a_ref[j] = _pack_complex(_dot(g_ref[j], _load_every(x_ref, j, rh).astype(BF16)))


def _dft_in(x4, col, g_in):
    bx, rh, r, _ = x4.shape
    c = HY_CH
    cbs = c // LANES
    return pl.pallas_call(
        _dft_in_kernel,
        grid=(bx, r // DFT_STEP, cbs),
        in_specs=[pl.BlockSpec((None, rh, DFT_STEP, LANES), lambda b, i, cb: (b, 0, i, col * cbs + cb)),
                  pl.BlockSpec((DFT_STEP, 2 * r, rh), lambda b, i, cb: (i, 0, 0))],
        out_specs=pl.BlockSpec((None, DFT_STEP, r, LANES), lambda b, i, cb: (b, i, 0, cb)),
        out_shape=jax.ShapeDtypeStruct((bx, r, r, c), F32),
        compiler_params=_params("parallel", "parallel", "parallel"),
        name="dft_in",
    )(x4, g_in)


def _stage2_operand(a_ref, j, r):
    return jnp.concatenate([_unpack_complex(_load_every(a_ref.at[b], j, r)) for b in range(a_ref.shape[0])],
                           axis=1)


def _dft_filt_kernel(a_ref, f_ref, k_ref):
    r = f_ref.shape[0] // 2
    for j in range(DFT_STEP):
        s = _dot(f_ref[...], _stage2_operand(a_ref, j, r))
        for o in range(a_ref.shape[0]):
            k_ref[o, j] = s[:, o * LANES:(o + 1) * LANES]


def _dft_filt(a, f_fwd):
    nq, r, _, c = a.shape
    return pl.pallas_call(
        _dft_filt_kernel,
        grid=(r // DFT_STEP, c // LANES),
        in_specs=[pl.BlockSpec((nq, r, DFT_STEP, LANES), lambda i, cb: (0, 0, i, cb)),
                  pl.BlockSpec(f_fwd.shape, lambda i, cb: (0, 0))],
        out_specs=pl.BlockSpec((nq, DFT_STEP, 2 * r, LANES), lambda i, cb: (0, i, 0, cb)),
        out_shape=jax.ShapeDtypeStruct((nq, r, 2 * r, c), F32),
        compiler_params=_params("parallel", "parallel"),
        name="dft_filt",
    )(a, f_fwd)


def _dft_mid_kernel(a_ref, k_ref, ff_ref, fi_ref, b_ref):
    r = ff_ref.shape[0] // 2
    n_seq = a_ref.shape[0]
    for j in range(DFT_STEP):
        s = _dot(ff_ref[...], _stage2_operand(a_ref, j, r))
        sr, si = s[0:r], s[r:2 * r]
        kr = jnp.concatenate([k_ref[j, 0:r, :]] * n_seq, axis=1)
        ki = jnp.concatenate([k_ref[j, r:2 * r, :]] * n_seq, axis=1)
        p = jnp.concatenate([sr * kr - si * ki, sr * ki + si * kr], axis=0).astype(BF16)
        back = _dot(fi_ref[...], p)
        for b in range(n_seq):
            b_ref[b, j] = _pack_complex(back[:, b * LANES:(b + 1) * LANES])


def _dft_mid(a, kspec, order, f_fwd, f_inv):
    bsz, r, _, c = a.shape
    return pl.pallas_call(
        _dft_mid_kernel,
        grid=(r // DFT_STEP, c // LANES),
        in_specs=[pl.BlockSpec((bsz, r, DFT_STEP, LANES), lambda i, cb: (0, 0, i, cb)),
                  pl.BlockSpec((None, DFT_STEP, 2 * r, LANES), lambda i, cb: (order, i, 0, cb)),
                  pl.BlockSpec(f_fwd.shape, lambda i, cb: (0, 0)),
                  pl.BlockSpec(f_inv.shape, lambda i, cb: (0, 0))],
        out_specs=pl.BlockSpec((bsz, DFT_STEP, r, LANES), lambda i, cb: (0, i, 0, cb)),
        out_shape=jax.ShapeDtypeStruct((bsz, r, r, c), F32),
        compiler_params=_params("parallel", "parallel"),
        name="dft_mid",
    )(a, kspec, f_fwd, f_inv)


def _dft_out_kernel(b_ref, g_ref, u_ref, x_ref, bias_ref, o_ref):
    r, rh = b_ref.shape[0], o_ref.shape[0]
    for j in range(DFT_STEP):
        y = _dot(g_ref[j], _unpack_complex(_load_every(b_ref, j, r)))
        _store_every(o_ref, j, rh, _load_every(x_ref, j, rh) * (y + _load_every(u_ref, j, rh) * bias_ref[...]))


def _dft_out(bm, g_out, u4, u_col, x4, x_col, bias):
    bsz, r, _, c = bm.shape
    rh = r // 2
    cbs = c // LANES
    seq = lambda col: pl.BlockSpec((None, rh, DFT_STEP, LANES), lambda b, i, cb: (b, 0, i, col * cbs + cb))
    return pl.pallas_call(
        _dft_out_kernel,
        grid=(bsz, r // DFT_STEP, cbs),
        in_specs=[pl.BlockSpec((None, r, DFT_STEP, LANES), lambda b, i, cb: (b, 0, i, cb)),
                  pl.BlockSpec((DFT_STEP, rh, 2 * r), lambda b, i, cb: (i, 0, 0)),
                  seq(u_col), seq(x_col), pl.BlockSpec((1, LANES), lambda b, i, cb: (0, cb))],
        out_specs=seq(0),
        out_shape=jax.ShapeDtypeStruct((bsz, rh, r, c), F32),
        compiler_params=_params("parallel", "parallel", "parallel"),
        name="dft_out",
    )(bm, g_out, u4, x4, bias.reshape(1, c))


def _hyena_filter_stage1(n, filter_params):
    r, g_full = _dft_tables(n)[0], _dft_tables(n)[5]
    kern = _hy_kernels(n, *filter_params)
    return _dft_in(kern.reshape(-1, r, r, HY_CH), 0, g_full)


def _hyena_filter_spectra(n, stage1):
    return _dft_filt(stage1, _dft_tables(n)[3])


def _hyena(hy_in, kspec, conv_bias):
    bsz, n, _ = hy_in.shape
    r, g_in, g_out, f_fwd, f_inv, _ = _dft_tables(n)
    seq4 = hy_in.reshape(bsz, r // 2, r, 3 * HY_CH)
    zz = _dft_out(_dft_mid(_dft_in(seq4, 0, g_in), kspec, 0, f_fwd, f_inv), g_out,
                  seq4, 0, seq4, 1, conv_bias[0])
    out = _dft_out(_dft_mid(_dft_in(zz, 0, g_in), kspec, 1, f_fwd, f_inv), g_out,
                   zz, 0, seq4, 2, conv_bias[1])
    return out.reshape(bsz, n, HY_CH)


CD_HY, CD_Z, CD_XBC, CD_DT = 0, 1536, 2048, 3072
CD_PAD_COLS = CD_DT + 2 * LANES


def _reorder_ab(w):
    gq, gk, gv, gg, lr_f, lr_b, hq, hf_f, hf_b, hi, hg = jnp.split(
        w, np.cumsum([256, 256, 512, 512, 16, 16, 512, 512, 512, 512, 512])[:-1].tolist(), axis=-1)
    pad = jnp.zeros((w.shape[0], AB_PAD_COLS - AB_LR - 2 * GLA_LOW_RANK), w.dtype)
    return jnp.concatenate([gq, gk, gv, gg, hq, hf_f, hf_b, hi, hg, lr_f, lr_b, pad], axis=-1)


def _reorder_cd(w):
    hy, z, xbc, dt_f, dt_b = jnp.split(w, np.cumsum([1536, 512, 1024, 8, 8])[:-1].tolist(), axis=-1)
    pad = jnp.zeros((w.shape[0], LANES - MB_HEADS), w.dtype)
    return jnp.concatenate([hy, z, xbc, dt_f, pad, dt_b, pad], axis=-1)


def kernel(x, c, ctx, c_ctx, ada_w, ada_b, norm_mix_g, norm_ffn_g, norm_out_g, ab_w_in, ab_w_out, gla_gate_w, gla_gate_b, gla_norm_g, hg_lb, hg_norm_g, cd_w_in, cd_w_out, hy_short_w, hy_short_b, hy_w1, hy_b1, hy_w2, hy_b2, hy_w3, hy_freq, hy_bias, mb_conv_w, mb_conv_b, mb_dt_bias, mb_a_log, mb_d, mb_norm_g, router_w, router_b, moe_w_gate, moe_w_up, moe_w_down):
    bsz, n_lat, d = x.shape
    n_ctx = ctx.shape[1]
    t = n_ctx + n_lat
    assert ada_w.shape[0] == 2 and ab_w_in.shape[0] == 1 and cd_w_in.shape[0] == 1

    cond = jnp.zeros((8, d), F32).at[:bsz].set(c).at[bsz].set(c_ctx)
    m = _adaln(cond, ada_w, ada_b)

    def mods(layer):
        lat = m[layer, :bsz].reshape(bsz, 6, d)
        cx = jnp.broadcast_to(m[layer, bsz].reshape(1, 6, d), (bsz, 6, d))
        both = jnp.stack([cx, lat], axis=1)
        return [both[:, :, j][:, :, None, :] for j in range(6)]

    lb_all = jnp.cumsum(jax.nn.softmax(hg_lb.astype(F32), axis=1), axis=1)
    router_w_pad = jnp.zeros((d, LANES), F32).at[:, :N_EXPERTS].set(router_w)
    h = jnp.concatenate([ctx, x], axis=1)

    sh_m, sc_m, gt_m, sh_f, sc_f, gt_f = mods(0)
    proj = _norm_proj(h, norm_mix_g[0], sh_m, sc_m, _reorder_ab(ab_w_in[0]).astype(BF16), n_ctx)
    gwp = [jnp.zeros((LANES, GLA_KEY_W), F32).at[GLA_LOW_RANK * dd:GLA_LOW_RANK * (dd + 1)].set(gla_gate_w[0, dd])
           for dd in range(2)]
    o_gla = _gla_scan(proj, gwp, [gla_gate_b[0, dd].reshape(1, -1) for dd in range(2)], n_ctx)
    o_hg = _hgrn_scan(proj, [lb_all[dd, 0].reshape(1, -1) for dd in range(2)], n_ctx)
    h = _mix_out_ab(h, o_gla, o_hg, proj, gla_norm_g[0], hg_norm_g[0], ab_w_out[0].astype(BF16), gt_m, n_ctx)
    filter_params = (hy_w1[0], hy_b1[0], hy_w2[0], hy_b2[0], hy_w3[0], hy_freq[0])
    side = ((functools.partial(_hyena_filter_stage1, n_lat), filter_params),
            functools.partial(_hyena_filter_spectra, n_lat))
    h, kspec = _moe(h, norm_ffn_g[0], sh_f, sc_f, gt_f, router_w_pad, router_b,
                    0, moe_w_gate, moe_w_up, moe_w_down, n_ctx, side=side)

    sh_m, sc_m, gt_m, sh_f, sc_f, gt_f = mods(1)
    proj = _norm_proj(h, norm_mix_g[1], sh_m, sc_m, _reorder_cd(cd_w_in[0]).astype(BF16), n_ctx)
    hy_in = _dwconv(proj, CD_HY, 3 * HY_CH, hy_short_w[0], hy_short_b[0], (n_ctx, t), act=False)
    hy = _hyena(hy_in, kspec, hy_bias[0])
    xbc = _dwconv(proj, CD_XBC, MB_INNER + 2 * MB_BC_W, mb_conv_w[0], mb_conv_b[0], (0, n_ctx, t), act=True)
    y_ssd = _ssd_scan(xbc, proj, CD_DT // LANES, mb_dt_bias[0], mb_a_log[0], n_ctx)
    d_skip_x = jnp.repeat(mb_d[0], MB_HEAD_DIM).reshape(1, MB_INNER)
    h = _mix_out_cd(h, hy, y_ssd, xbc, proj, CD_Z // MB_INNER, d_skip_x, mb_norm_g[0],
                    cd_w_out[0].astype(BF16), gt_m, n_ctx)
    return _moe(h, norm_ffn_g[1], sh_f, sc_f, gt_f, router_w_pad, router_b,
                1, moe_w_gate, moe_w_up, moe_w_down, 0, final_g=norm_out_g)
```

```python
import functools
import math

import numpy as np
import jax
import jax.numpy as jnp
from jax import lax
from jax.experimental import pallas as pl
from jax.experimental.pallas import tpu as pltpu
from jax.experimental.pallas import tpu_sc as plsc

NORM_EPS = 1e-6
GLA_HEADS, GLA_DK, GLA_DV, GLA_LOW_RANK, GLA_TAU = 4, 64, 128, 16, 16.0
GLA_KEY_W, GLA_VAL_W = GLA_HEADS * GLA_DK, GLA_HEADS * GLA_DV
HG_HEADS, HG_EXPAND, HG_DV = 4, 128, 128
HG_KEY_W, HG_VAL_W = HG_HEADS * HG_EXPAND, HG_HEADS * HG_DV
HY_CH, HY_ORDER, HY_SHORT, HY_BANDS, HY_FILT_HID = 512, 2, 3, 16, 64
HY_MIN_DECAY = math.log(1e-2) / 1.5
HY_MAX_DECAY = math.log(1e-2) / 0.3
MB_HEADS, MB_HEAD_DIM, MB_GROUPS, MB_STATE = 8, 64, 2, 128
MB_INNER = MB_HEADS * MB_HEAD_DIM
MB_BC_W = MB_GROUPS * MB_STATE
N_EXPERTS, N_GROUPS, TOP_K, MOE_BLOCK = 16, 4, 2, 256
EXPERTS_PER_GROUP = N_EXPERTS // N_GROUPS

LANES = 128
SCAN_CHUNK = 64
SCAN_BLOCK = 256
SSD_CHUNK = 128
ROW_TILE = 256
VMEM_LIMIT = 56 * 1024 * 1024

BF16 = jnp.bfloat16
F32 = jnp.float32


def _params(*sem):
    return pltpu.CompilerParams(dimension_semantics=sem, vmem_limit_bytes=VMEM_LIMIT)


def _split3(x):
    hi = x.astype(BF16)
    r1 = x - hi.astype(F32)
    mid = r1.astype(BF16)
    lo = (r1 - mid.astype(F32)).astype(BF16)
    return hi, mid, lo


def _dot(a, b):
    return jnp.dot(a, b, preferred_element_type=F32)


def _dot_nt(a, b):
    return lax.dot_general(a, b, (((1,), (1,)), ((), ())), preferred_element_type=F32)


def _dot_tn(a, b):
    return lax.dot_general(a, b, (((0,), (0,)), ((), ())), preferred_element_type=F32)


def _sel_dot(m01, x):
    hi, mid, lo = _split3(x)
    return _dot(m01, hi) + (_dot(m01, mid) + _dot(m01, lo))


def _dot_sel(x, m01):
    hi, mid, lo = _split3(x)
    return _dot(hi, m01) + (_dot(mid, m01) + _dot(lo, m01))


def _dot_f32(a, b):
    ah = a.astype(BF16)
    al = (a - ah.astype(F32)).astype(BF16)
    bh = b.astype(BF16)
    bl = (b - bh.astype(F32)).astype(BF16)
    return _dot(ah, bh) + (_dot(ah, bl) + _dot(al, bh))


def _silu(x):
    return x * (1.0 / (1.0 + jnp.exp(-x)))


def _sigmoid(x):
    return 1.0 / (1.0 + jnp.exp(-x))


def _softplus(x):
    return jnp.maximum(x, 0.0) + jnp.log(1.0 + jnp.exp(-jnp.abs(x)))


def _pack_bf16_pairs(x):
    bits = lax.bitcast_convert_type(x.astype(BF16).astype(F32), jnp.uint32)
    half = x.shape[1] // 2
    return bits[:, :half] | (bits[:, half:] >> 16)


def _unpack_bf16_pairs(p):
    hi = lax.bitcast_convert_type(p & jnp.uint32(0xFFFF0000), F32)
    lo = lax.bitcast_convert_type(p << 16, F32)
    return jnp.concatenate([hi, lo], axis=1)


def _rms(x, g):
    return x * lax.rsqrt(jnp.mean(x * x, axis=-1, keepdims=True) + NORM_EPS) * g


def _adaln_kernel(c_ref, w_ref, b_ref, o_ref):
    o_ref[...] = _dot_f32(_silu(c_ref[...]), w_ref[...]) + b_ref[...]


def _adaln(cond, w, b):
    n_l, d, n6 = w.shape
    tn = 1536
    return pl.pallas_call(
        _adaln_kernel,
        grid=(n_l, n6 // tn),
        in_specs=[pl.BlockSpec((8, d), lambda l, j: (0, 0)),
                  pl.BlockSpec((None, d, tn), lambda l, j: (l, 0, j)),
                  pl.BlockSpec((None, 1, tn), lambda l, j: (l, 0, j))],
        out_specs=pl.BlockSpec((None, 8, tn), lambda l, j: (l, 0, j)),
        out_shape=jax.ShapeDtypeStruct((n_l, 8, n6), F32),
        compiler_params=_params("parallel", "parallel"),
        name="adaln",
    )(cond, w, b.reshape(n_l, 1, n6))


def _norm_proj_kernel(h_ref, g_ref, sh_ref, sc_ref, w_ref, o_ref):
    u = _rms(h_ref[...], g_ref[...]) * (1.0 + sc_ref[...]) + sh_ref[...]
    o_ref[...] = _dot(u.astype(BF16), w_ref[...])


def _norm_proj(h, g, shift, scale, w, n_ctx):
    bsz, t, d = h.shape
    n = w.shape[1]
    tm = ROW_TILE
    seg = lambda b, i: (b, (i * tm >= n_ctx).astype(jnp.int32), 0, 0)
    return pl.pallas_call(
        _norm_proj_kernel,
        grid=(bsz, t // tm),
        in_specs=[pl.BlockSpec((None, tm, d), lambda b, i: (b, i, 0)),
                  pl.BlockSpec((1, d), lambda b, i: (0, 0)),
                  pl.BlockSpec((None, None, 1, d), seg),
                  pl.BlockSpec((None, None, 1, d), seg),
                  pl.BlockSpec((d, n), lambda b, i: (0, 0))],
        out_specs=pl.BlockSpec((None, tm, n), lambda b, i: (b, i, 0)),
        out_shape=jax.ShapeDtypeStruct((bsz, t, n), F32),
        compiler_params=_params("parallel", "parallel"),
        name="norm_proj",
    )(h, g.reshape(1, d), shift, scale, w)


def _scan_constants(c, reverse):
    t = np.arange(c)[:, None]
    u = np.arange(c)[None, :]
    sels = [u <= t, u > t]
    masks = []
    m = c // 2
    while m >= 1:
        blk = t // (2 * m)
        upper_t = (t % (2 * m)) >= m
        r = blk * (2 * m) + m - 1
        s_blk = u // (2 * m)
        upper_s = (u % (2 * m)) >= m
        sels.append((upper_t & (u > r) & (u <= t)) | ((~upper_t) & (u > t) & (u <= r)))
        masks.append((blk == s_blk) & upper_t & (~upper_s))
        m //= 2
    masks.append(t == u)
    sel = np.stack(sels).astype(np.float32)
    msk = np.stack(masks).astype(np.float32)
    if reverse:
        sel = sel[:, ::-1, ::-1]
        msk = msk[:, ::-1, ::-1]
    return np.ascontiguousarray(sel.reshape(-1, c)), np.ascontiguousarray(msk)


def _chunk_order(i, n_ctx_chunks, n_chunks, reverse):
    if not reverse:
        return i
    return jnp.where(i < n_ctx_chunks, n_ctx_chunks - 1 - i, n_chunks - 1 - (i - n_ctx_chunks))


GROUP_KEYS = 256


def _decay_chunk(q, k, v, la, consts, st_ref, heads, dk, dv):
    sel_ref, mask_ref, hm_ref, hmb_ref, vm_ref = consts
    c = q.shape[0]
    n_lvl = mask_ref.shape[0] - 1
    hpg = GROUP_KEYS // dk
    cs = _dot(sel_ref[...], jnp.concatenate(_split3(la), axis=0))
    e_q = jnp.exp(cs[0:c])
    e_k = jnp.exp(cs[c:2 * c])
    e_tot = jnp.exp(jnp.sum(la, axis=0, keepdims=True))
    vb = v.astype(BF16)
    outs = []
    for g in range(heads // hpg):
        ks = slice(g * GROUP_KEYS, (g + 1) * GROUP_KEYS)
        vs = slice(g * hpg * dv, (g + 1) * hpg * dv)
        qg, kg = q[:, ks], k[:, ks]
        key_stack = lambda x: jnp.concatenate([x.astype(BF16) * hmb_ref[h] for h in range(hpg)], axis=0)
        att = mask_ref[n_lvl] * _dot_nt(qg.astype(BF16), key_stack(kg))
        for l in range(n_lvl):
            e = jnp.exp(cs[(2 + l) * c:(3 + l) * c, ks])
            att = att + mask_ref[l] * _dot_nt((qg * e).astype(BF16), key_stack(kg * e))
        v_blocks = jnp.concatenate([vb[:, vs] * vm_ref[h] for h in range(hpg)], axis=0)
        intra = _dot(att.astype(BF16), v_blocks)
        st = st_ref[g]
        q_stack = jnp.concatenate([(qg * e_q[:, ks]) * hm_ref[h] for h in range(hpg)], axis=0)
        inter = _dot_nt(q_stack.astype(BF16), st.astype(BF16))
        upd = _dot_tn(vb[:, vs], (kg * e_k[:, ks]).astype(BF16))
        new = st * e_tot[:, ks]
        for h in range(hpg):
            new = new + upd[h * dv:(h + 1) * dv] * hm_ref[h]
        st_ref[g] = new
        outs.append(intra + jnp.concatenate([inter[h * c:(h + 1) * c] for h in range(hpg)], axis=-1))
    return jnp.concatenate(outs, axis=-1)


def _log_sigmoid(x):
    return jnp.minimum(x, 0.0) - jnp.log(1.0 + jnp.exp(-jnp.abs(x)))


def _gla_kernel(*refs):
    ins, head_masks, (o_refs, st_refs) = (refs[0:8], refs[8:16]), refs[16:19], (refs[19:21], refs[21:23])

    @pl.when(pl.program_id(1) == 0)
    def _():
        for st_ref in st_refs:
            st_ref[...] = jnp.zeros_like(st_ref)

    for d, ((q_ref, k_ref, v_ref, lr_ref, gw_ref, gb_ref, sel_ref, mask_ref), o_ref, st_ref) in enumerate(
            zip(ins, o_refs, st_refs)):
        z = _dot_f32(lr_ref[...], gw_ref[...]) + gb_ref[...]
        la = _log_sigmoid(z) * (1.0 / GLA_TAU)
        q = q_ref[...] * (GLA_DK ** -0.5)
        k, v = k_ref[...], v_ref[...]
        for rows in _sub_chunks(q.shape[0], d == 1):
            o_ref[rows, :] = _decay_chunk(q[rows], k[rows], v[rows], la[rows], (sel_ref, mask_ref) + head_masks,
                                          st_ref, GLA_HEADS, GLA_DK, GLA_DV)


def _hgrn_kernel(*refs):
    ins, head_masks, (o_refs, st_refs) = (refs[0:6], refs[6:12]), refs[12:15], (refs[15:17], refs[17:19])

    @pl.when(pl.program_id(1) == 0)
    def _():
        for st_ref in st_refs:
            st_ref[...] = jnp.zeros_like(st_ref)

    for d, ((q_ref, f_ref, v_ref, lb_ref, sel_ref, mask_ref), o_ref, st_ref) in enumerate(
            zip(ins, o_refs, st_refs)):
        lb = lb_ref[...]
        f = lb + (1.0 - lb) * _sigmoid(f_ref[...])
        q, k, v, la = _silu(q_ref[...]), 1.0 - f, v_ref[...], jnp.log(f)
        for rows in _sub_chunks(q.shape[0], d == 1):
            o_ref[rows, :] = _decay_chunk(q[rows], k[rows], v[rows], la[rows], (sel_ref, mask_ref) + head_masks,
                                          st_ref, HG_HEADS, HG_EXPAND, HG_DV)


def _sub_chunks(rows, reverse):
    order = range(rows // SCAN_CHUNK)
    return [slice(j * SCAN_CHUNK, (j + 1) * SCAN_CHUNK) for j in (reversed(order) if reverse else order)]


def _scan_specs(blk, n_ctx, t, reverse, chunk=None, stacked_heads=1):
    n_blocks = t // blk
    order = functools.partial(_chunk_order, n_ctx_chunks=n_ctx // blk, n_chunks=n_blocks, reverse=reverse)

    def col(width, idx):
        return pl.BlockSpec((None, blk, width), lambda b, i: (b, order(i), idx))

    sel, msk = _scan_constants(chunk or blk, reverse)
    sel3 = np.concatenate([sel, sel, sel], axis=1)
    msk = np.tile(msk, (1, 1, stacked_heads))
    const = lambda a: pl.BlockSpec(a.shape, lambda b, i: (0,) * a.ndim)
    return n_blocks, col, const, jnp.asarray(sel3, BF16), jnp.asarray(msk, F32)


def _head_masks(dk, dv):
    hpg = GROUP_KEYS // dk
    hm = np.zeros((hpg, 1, GROUP_KEYS), np.float32)
    vm = np.zeros((hpg, 1, hpg * dv), np.float32)
    for h in range(hpg):
        hm[h, 0, h * dk:(h + 1) * dk] = 1.0
        vm[h, 0, h * dv:(h + 1) * dv] = 1.0
    return jnp.asarray(hm), jnp.asarray(hm, BF16), jnp.asarray(vm, BF16)


AB_Q, AB_K, AB_V, AB_G = 0, 256, 512, 1024
AB_HQ, AB_HF, AB_HI, AB_HG, AB_LR = 1536, 2048, 3072, 3584, 4096
AB_PAD_COLS = 4224


def _gla_scan(proj, gate_w_pad, gate_b, n_ctx):
    bsz, t, _ = proj.shape
    hpg = GROUP_KEYS // GLA_DK
    in_specs, args, outs = [], [], []
    for d in range(2):
        n_blocks, col, const, sel, msk = _scan_specs(SCAN_BLOCK, n_ctx, t, d == 1, SCAN_CHUNK, hpg)
        in_specs += [col(GLA_KEY_W, AB_Q // GLA_KEY_W), col(GLA_KEY_W, AB_K // GLA_KEY_W),
                     col(GLA_VAL_W, AB_V // GLA_VAL_W), col(LANES, AB_LR // LANES),
                     const(gate_w_pad[d]), const(gate_b[d]), const(sel), const(msk)]
        args += [proj, proj, proj, proj, gate_w_pad[d], gate_b[d], sel, msk]
        outs.append(col(GLA_VAL_W, 0))
    hm = _head_masks(GLA_DK, GLA_DV)
    return pl.pallas_call(
        _gla_kernel,
        grid=(bsz, n_blocks),
        in_specs=in_specs + [const(m) for m in hm],
        out_specs=outs,
        out_shape=[jax.ShapeDtypeStruct((bsz, t, GLA_VAL_W), F32)] * 2,
        scratch_shapes=[pltpu.VMEM((GLA_HEADS // hpg, GLA_DV, GROUP_KEYS), F32)] * 2,
        compiler_params=_params("parallel", "arbitrary"),
        name="gla_scan",
    )(*args, *hm)


def _hgrn_scan(proj, lb, n_ctx):
    bsz, t, _ = proj.shape
    hpg = GROUP_KEYS // HG_EXPAND
    in_specs, args, outs = [], [], []
    for d in range(2):
        n_blocks, col, const, sel, msk = _scan_specs(SCAN_BLOCK, n_ctx, t, d == 1, SCAN_CHUNK, hpg)
        in_specs += [col(HG_KEY_W, AB_HQ // HG_KEY_W), col(HG_KEY_W, AB_HF // HG_KEY_W + d),
                     col(HG_VAL_W, AB_HI // HG_VAL_W), const(lb[d]), const(sel), const(msk)]
        args += [proj, proj, proj, lb[d], sel, msk]
        outs.append(col(HG_VAL_W, 0))
    hm = _head_masks(HG_EXPAND, HG_DV)
    return pl.pallas_call(
        _hgrn_kernel,
        grid=(bsz, n_blocks),
        in_specs=in_specs + [const(m) for m in hm],
        out_specs=outs,
        out_shape=[jax.ShapeDtypeStruct((bsz, t, HG_VAL_W), F32)] * 2,
        scratch_shapes=[pltpu.VMEM((HG_HEADS // hpg, HG_DV, GROUP_KEYS), F32)] * 2,
        compiler_params=_params("parallel", "arbitrary"),
        name="hgrn_scan",
    )(*args, *hm)


def _mix_out_ab_kernel(h_ref, gf_ref, gb_ref, hf_ref, hb_ref, gg_ref, hg_ref, gn_ref, hn_ref,
                       w_ref, gt_ref, o_ref):
    feats = []
    for o, gate, g in ((gf_ref[...] + gb_ref[...], gg_ref[...], gn_ref[...]),
                       (hf_ref[...] + hb_ref[...], hg_ref[...], hn_ref[...])):
        for hd in range(o.shape[-1] // LANES):
            s = slice(hd * LANES, (hd + 1) * LANES)
            feats.append(_rms(o[:, s], g) * _silu(gate[:, s]))
    feat = jnp.concatenate(feats, axis=-1).astype(BF16)
    o_ref[...] = h_ref[...] + gt_ref[...] * _dot(feat, w_ref[...])


def _mix_out_ab(h, o_gla, o_hg, proj, gla_norm_g, hg_norm_g, w_out, gate, n_ctx):
    bsz, t, d = h.shape
    tm = ROW_TILE
    seg = lambda b, i: (b, (i * tm >= n_ctx).astype(jnp.int32), 0, 0)
    row = lambda width, idx: pl.BlockSpec((None, tm, width), lambda b, i: (b, i, idx))
    vec = pl.BlockSpec((1, LANES), lambda b, i: (0, 0))
    return pl.pallas_call(
        _mix_out_ab_kernel,
        grid=(bsz, t // tm),
        in_specs=[row(d, 0), row(GLA_VAL_W, 0), row(GLA_VAL_W, 0), row(HG_VAL_W, 0), row(HG_VAL_W, 0),
                  row(GLA_VAL_W, AB_G // GLA_VAL_W), row(HG_VAL_W, AB_HG // HG_VAL_W), vec, vec,
                  pl.BlockSpec(w_out.shape, lambda b, i: (0, 0)),
                  pl.BlockSpec((None, None, 1, d), seg)],
        out_specs=row(d, 0),
        out_shape=jax.ShapeDtypeStruct((bsz, t, d), F32),
        compiler_params=_params("parallel", "parallel"),
        name="mix_out_ab",
    )(h, o_gla[0], o_gla[1], o_hg[0], o_hg[1], proj, proj, gla_norm_g.reshape(1, -1),
      hg_norm_g.reshape(1, -1), w_out, gate)


HALO = 8


def _norm_proj_conv_kernel(h_ref, hp_ref, hn_ref, g_ref, sh_ref, sc_ref, w_ref, cw_ref, cb_ref,
                           hy_ref, xbc_ref, zdt_ref, *, ctx_tiles, n_tiles):
    i = pl.program_id(1)
    tm = h_ref.shape[0]
    hh = jnp.concatenate([hp_ref[...], h_ref[...], hn_ref[...]], axis=0)
    u = _rms(hh, g_ref[...]) * (1.0 + sc_ref[...]) + sh_ref[...]
    p = _dot(u.astype(BF16), w_ref[...])
    n_conv = cw_ref.shape[1]
    n_hy = hy_ref.shape[1]
    pc = p[:, 0:n_conv]
    rows_all = tm + 2 * HALO
    cur = pc[HALO:HALO + tm]
    prev = pltpu.roll(pc, 1, axis=0)[HALO:HALO + tm]
    nxt = pltpu.roll(pc, rows_all - 1, axis=0)[HALO:HALO + tm]
    first = jnp.logical_or(i == 0, i == ctx_tiles)
    last = jnp.logical_or(i == ctx_tiles - 1, i == n_tiles - 1)
    rows = lax.broadcasted_iota(jnp.int32, cur.shape, 0)
    prev = jnp.where(jnp.logical_and(first, rows == 0), 0.0, prev)
    nxt = jnp.where(jnp.logical_and(last, rows == tm - 1), 0.0, nxt)
    y = prev * cw_ref[0:1, :] + cur * cw_ref[1:2, :] + nxt * cw_ref[2:3, :] + cb_ref[...]
    hy_ref[...] = y[:, 0:n_hy]
    xbc_ref[...] = _silu(y[:, n_hy:n_conv])
    zdt_ref[...] = p[HALO:HALO + tm, n_conv:]


def _norm_proj_conv(h, g, shift, scale, w, conv_w, conv_b, n_hy, n_ctx):
    bsz, t, d = h.shape
    n = w.shape[1]
    n_conv = conv_w.shape[1]
    tm = ROW_TILE
    n_tiles, ctx_tiles = t // tm, n_ctx // tm
    r8 = tm // HALO
    last8 = t // HALO - 1
    seg = lambda b, i: (b, (i >= ctx_tiles).astype(jnp.int32), 0, 0)
    kern = functools.partial(_norm_proj_conv_kernel, ctx_tiles=ctx_tiles, n_tiles=n_tiles)
    return pl.pallas_call(
        kern,
        grid=(bsz, n_tiles),
        in_specs=[pl.BlockSpec((None, tm, d), lambda b, i: (b, i, 0)),
                  pl.BlockSpec((None, HALO, d), lambda b, i: (b, jnp.maximum(i * r8 - 1, 0), 0)),
                  pl.BlockSpec((None, HALO, d), lambda b, i: (b, jnp.minimum((i + 1) * r8, last8), 0)),
                  pl.BlockSpec((1, d), lambda b, i: (0, 0)),
                  pl.BlockSpec((None, None, 1, d), seg),
                  pl.BlockSpec((None, None, 1, d), seg),
                  pl.BlockSpec((d, n), lambda b, i: (0, 0)),
                  pl.BlockSpec((3, n_conv), lambda b, i: (0, 0)),
                  pl.BlockSpec((1, n_conv), lambda b, i: (0, 0))],
        out_specs=[pl.BlockSpec((None, tm, n_hy), lambda b, i: (b, jnp.maximum(i - ctx_tiles, 0), 0)),
                   pl.BlockSpec((None, tm, n_conv - n_hy), lambda b, i: (b, i, 0)),
                   pl.BlockSpec((None, tm, n - n_conv), lambda b, i: (b, i, 0))],
        out_shape=[jax.ShapeDtypeStruct((bsz, t - n_ctx, n_hy), F32),
                   jax.ShapeDtypeStruct((bsz, t, n_conv - n_hy), F32),
                   jax.ShapeDtypeStruct((bsz, t, n - n_conv), F32)],
        compiler_params=_params("parallel", "arbitrary"),
        name="norm_proj_conv",
    )(h, h, h, g.reshape(1, d), shift, scale, w, conv_w, conv_b)


def _ssd_kernel(*refs):
    ins, hexp_ref, o_refs, st_refs = (refs[0:7], refs[7:14]), refs[14], refs[15:17], refs[17:19]

    @pl.when(pl.program_id(1) == 0)
    def _():
        for st_ref in st_refs:
            st_ref[...] = jnp.zeros_like(st_ref)

    for (xbc_ref, dt_ref, bias_ref, alog_ref, mq_ref, mk_ref, mask_ref), o_ref, st_ref in zip(ins, o_refs, st_refs):
        _ssd_chunk(xbc_ref, dt_ref, bias_ref, alog_ref, hexp_ref, mq_ref, mk_ref, mask_ref, o_ref, st_ref)


def _ssd_chunk(xbc_ref, dt_ref, bias_ref, alog_ref, hexp_ref, mq_ref, mk_ref, mask_ref, o_ref, st_ref):
    c = xbc_ref.shape[0]
    hpg = MB_HEADS // MB_GROUPS
    gw = hpg * MB_HEAD_DIM
    dt = _softplus(dt_ref[...] + bias_ref[...])
    la = -dt * jnp.exp(alog_ref[...])
    cq = _sel_dot(mq_ref[...], la)
    ck = _sel_dot(mk_ref[...], la)
    cq_t = _dot_nt_sel(la, mq_ref[...])
    hexp = hexp_ref[...]
    dt_x = _dot_sel(dt, hexp)
    eq_x = jnp.exp(_dot_sel(cq, hexp))
    ek_x = jnp.exp(_dot_sel(ck, hexp))
    etot_x = jnp.exp(_dot_sel(jnp.sum(la, axis=0, keepdims=True), hexp))
    xs = xbc_ref[:, 0:MB_INNER] * dt_x
    mask = mask_ref[...]
    outs = []
    for g in range(MB_GROUPS):
        bm = xbc_ref[:, MB_INNER + g * MB_STATE:MB_INNER + (g + 1) * MB_STATE].astype(BF16)
        cm = xbc_ref[:, MB_INNER + MB_BC_W + g * MB_STATE:MB_INNER + MB_BC_W + (g + 1) * MB_STATE].astype(BF16)
        cb = _dot_nt(cm, bm)
        st = st_ref[g]
        gs = slice(g * gw, (g + 1) * gw)
        y_inter = _dot(cm, st.astype(BF16)) * eq_x[:, gs]
        for r in range(hpg):
            hd = g * hpg + r
            diff = cq[:, hd:hd + 1] - cq_t[hd:hd + 1, :]
            w = cb * jnp.exp(jnp.where(mask > 0.0, diff, -jnp.inf))
            ps = slice(hd * MB_HEAD_DIM, (hd + 1) * MB_HEAD_DIM)
            outs.append(_dot(w.astype(BF16), xs[:, ps].astype(BF16))
                        + y_inter[:, r * MB_HEAD_DIM:(r + 1) * MB_HEAD_DIM])
        st_ref[g] = st * etot_x[:, gs] + _dot_tn(bm, (xs[:, gs] * ek_x[:, gs]).astype(BF16))
    o_ref[...] = jnp.concatenate(outs, axis=-1)


def _dot_nt_sel(x, m01):
    hi, mid, lo = _split3(x)
    f = lambda p: lax.dot_general(p, m01, (((0,), (1,)), ((), ())), preferred_element_type=F32)
    return f(hi) + (f(mid) + f(lo))


def _ssd_scan(xbc, proj, dt_col, dt_bias, a_log, n_ctx):
    bsz, t, _ = xbc.shape
    c = SSD_CHUNK
    pad = lambda v: jnp.zeros((1, LANES), F32).at[0, :MB_HEADS].set(v)
    hexp = np.zeros((LANES, MB_INNER), np.float32)
    for hd in range(MB_HEADS):
        hexp[hd, hd * MB_HEAD_DIM:(hd + 1) * MB_HEAD_DIM] = 1.0
    hexp = jnp.asarray(hexp, BF16)
    tri = np.tril(np.ones((c, c), np.float32))
    in_specs, args, outs = [], [], []
    for d in range(2):
        n_chunks, col, const, sel, _ = _scan_specs(c, n_ctx, t, d == 1)
        mq, mk = sel[0:c, 0:c], sel[c:2 * c, 0:c]
        mask = jnp.asarray(tri[::-1, ::-1].copy() if d == 1 else tri)
        bias, alog = pad(dt_bias[d]), pad(a_log[d].astype(F32))
        in_specs += [col(xbc.shape[-1], 0), col(LANES, dt_col + d), const(bias), const(alog),
                     const(mq), const(mk), const(mask)]
        args += [xbc, proj, bias, alog, mq, mk, mask]
        outs.append(col(MB_INNER, 0))
    return pl.pallas_call(
        _ssd_kernel,
        grid=(bsz, n_chunks),
        in_specs=in_specs + [const(hexp)],
        out_specs=outs,
        out_shape=[jax.ShapeDtypeStruct((bsz, t, MB_INNER), F32)] * 2,
        scratch_shapes=[pltpu.VMEM((MB_GROUPS, MB_STATE, MB_INNER // MB_GROUPS), F32)] * 2,
        compiler_params=_params("parallel", "arbitrary"),
        name="ssd_scan",
    )(*args, hexp)


def _mix_out_cd_kernel(h_ref, hy_ref, yf_ref, yb_ref, xs_ref, z_ref, dsk_ref, ng_ref, w_ref, gt_ref, o_ref):
    y = (yf_ref[...] + yb_ref[...] + dsk_ref[...] * xs_ref[...]) * _silu(z_ref[...])
    gw = MB_INNER // MB_GROUPS
    ys = [_rms(y[:, g * gw:(g + 1) * gw], ng_ref[:, g * gw:(g + 1) * gw]) for g in range(MB_GROUPS)]
    feat = jnp.concatenate([hy_ref[...]] + ys, axis=-1).astype(BF16)
    o_ref[...] = h_ref[...] + gt_ref[...] * _dot(feat, w_ref[...])


def _mix_out_cd(h, hy, y_ssd, xbc, proj, z_col, d_skip_x, norm_g, w_out, gate, n_ctx):
    bsz, t, d = h.shape
    tm = ROW_TILE
    n_lat = t - n_ctx
    off = n_ctx // tm
    row = lambda width, idx: pl.BlockSpec((None, tm, width), lambda b, i: (b, i + off, idx))
    vec = pl.BlockSpec((1, MB_INNER), lambda b, i: (0, 0))
    return pl.pallas_call(
        _mix_out_cd_kernel,
        grid=(bsz, n_lat // tm),
        in_specs=[row(d, 0), pl.BlockSpec((None, tm, HY_CH), lambda b, i: (b, i, 0)),
                  row(MB_INNER, 0), row(MB_INNER, 0), row(MB_INNER, 0), row(MB_INNER, z_col), vec, vec,
                  pl.BlockSpec(w_out.shape, lambda b, i: (0, 0)),
                  pl.BlockSpec((None, None, 1, d), lambda b, i: (b, 1, 0, 0))],
        out_specs=pl.BlockSpec((None, tm, d), lambda b, i: (b, i, 0)),
        out_shape=jax.ShapeDtypeStruct((bsz, n_lat, d), F32),
        compiler_params=_params("parallel", "parallel"),
        name="mix_out_cd",
    )(h, hy, y_ssd[0], y_ssd[1], xbc, proj, d_skip_x, norm_g.reshape(1, -1), w_out, gate)


def _top2_of4(a, b, c, d):
    hi1, lo1, hi2, lo2 = jnp.maximum(a, b), jnp.minimum(a, b), jnp.maximum(c, d), jnp.minimum(c, d)
    return jnp.maximum(hi1, hi2) + jnp.maximum(jnp.minimum(hi1, hi2), jnp.maximum(lo1, lo2))


def _first_argmax(vals, skip=None):
    idx = None
    for j, vj in enumerate(vals):
        if idx is None and skip is None:
            idx, best = jnp.zeros(vj.shape, jnp.int32), vj
            continue
        if idx is None:
            idx, best = jnp.full(vj.shape, -1, jnp.int32), jnp.full(vj.shape, -jnp.inf, F32)
        take = vj > best
        if skip is not None:
            take = jnp.logical_and(take, skip != j)
        idx = jnp.where(take, j, idx)
        best = jnp.where(take, vj, best)
    return idx, best


def _ffn_pre_kernel(h_ref, g_ref, sh_ref, sc_ref, rw_ref, rb_ref, tri_ref,
                    v_ref, ri_ref, rwt_ref, cnt_ref, carry_ref):
    @pl.when(jnp.logical_and(pl.program_id(0) == 0, pl.program_id(1) == 0))
    def _():
        carry_ref[...] = jnp.zeros_like(carry_ref)

    v = _rms(h_ref[...], g_ref[...]) * (1.0 + sc_ref[...]) + sh_ref[...]
    v_ref[...] = _pack_bf16_pairs(v)
    st = _sigmoid(_dot_f32(v, rw_ref[...])).T[0:N_EXPERTS]
    sel = st + rb_ref[...]
    row = lambda a, e: a[e:e + 1]
    epg = EXPERTS_PER_GROUP
    gscore = [_top2_of4(*[row(sel, g * epg + j) for j in range(epg)]) for g in range(N_GROUPS)]
    best, _ = _first_argmax(gscore)

    def in_best(a, j):
        out = row(a, j)
        for g in range(1, N_GROUPS):
            out = jnp.where(best == g, row(a, g * epg + j), out)
        return out

    vals = [in_best(sel, j) for j in range(epg)]
    raw = [in_best(st, j) for j in range(epg)]
    i1, _ = _first_argmax(vals)
    i2, _ = _first_argmax(vals, skip=i1)
    pick = lambda i: functools.reduce(lambda acc, j: jnp.where(i == j, raw[j], acc), range(1, epg), raw[0])
    w1, w2 = pick(i1), pick(i2)
    wsum = w1 + w2
    e1, e2 = best * epg + i1, best * epg + i2

    experts = lax.broadcasted_iota(jnp.int32, st.shape, 0)
    oh1 = (experts == e1).astype(F32)
    oh2 = (experts == e2).astype(F32)
    cnt = oh1 + oh2
    before = _dot(cnt.astype(BF16), tri_ref[...]) + carry_ref[:, 0:1]
    ri_ref[0:1, :] = e1
    ri_ref[1:2, :] = e2
    ri_ref[2:3, :] = jnp.sum(oh1 * before, axis=0, keepdims=True).astype(jnp.int32)
    ri_ref[3:4, :] = jnp.sum(oh2 * before, axis=0, keepdims=True).astype(jnp.int32)
    ri_ref[4:8, :] = jnp.zeros((4, st.shape[1]), jnp.int32)
    rwt_ref[0:1, :] = w1 / wsum
    rwt_ref[1:2, :] = w2 / wsum
    rwt_ref[2:8, :] = jnp.zeros((6, st.shape[1]), F32)
    carry_ref[...] = carry_ref[...] + jnp.sum(cnt, axis=1, keepdims=True)
    cnt_ref[...] = carry_ref[...]


def _ffn_pre(h, g, shift, scale, router_w_pad, router_b, n_ctx):
    bsz, t, d = h.shape
    tm = ROW_TILE
    seg = lambda b, i: (b, (i * tm >= n_ctx).astype(jnp.int32), 0, 0)
    tri = jnp.asarray(np.triu(np.ones((tm, tm), np.float32), 1), BF16)
    return pl.pallas_call(
        _ffn_pre_kernel,
        grid=(bsz, t // tm),
        in_specs=[pl.BlockSpec((None, tm, d), lambda b, i: (b, i, 0)),
                  pl.BlockSpec((1, d), lambda b, i: (0, 0)),
                  pl.BlockSpec((None, None, 1, d), seg),
                  pl.BlockSpec((None, None, 1, d), seg),
                  pl.BlockSpec((d, LANES), lambda b, i: (0, 0)),
                  pl.BlockSpec((N_EXPERTS, 1), lambda b, i: (0, 0)),
                  pl.BlockSpec((tm, tm), lambda b, i: (0, 0))],
        out_specs=[pl.BlockSpec((None, tm, d // 2), lambda b, i: (b, i, 0)),
                   pl.BlockSpec((None, 8, tm), lambda b, i: (b, 0, i)),
                   pl.BlockSpec((None, 8, tm), lambda b, i: (b, 0, i)),
                   pl.BlockSpec((N_EXPERTS, LANES), lambda b, i: (0, 0))],
        out_shape=[jax.ShapeDtypeStruct((bsz, t, d // 2), jnp.uint32),
                   jax.ShapeDtypeStruct((bsz, 8, t), jnp.int32),
                   jax.ShapeDtypeStruct((bsz, 8, t), F32),
                   jax.ShapeDtypeStruct((N_EXPERTS, LANES), F32)],
        scratch_shapes=[pltpu.VMEM((N_EXPERTS, LANES), F32)],
        compiler_params=_params("arbitrary", "arbitrary"),
        name="ffn_pre",
    )(h, g.reshape(1, d), shift, scale, router_w_pad, router_b.reshape(N_EXPERTS, 1), tri)


def _experts_kernel(be_ref, nb_ref, x_ref, wg_ref, wu_ref, wd_ref, o_ref, wg_s, wu_s, wd_s):
    i = pl.program_id(0)
    prev = be_ref[jnp.maximum(i - 1, 0)]
    changed = jnp.logical_or(i == 0, be_ref[i] != prev)

    @pl.when(changed)
    def _():
        wg_s[...] = wg_ref[...].astype(BF16)
        wu_s[...] = wu_ref[...].astype(BF16)
        wd_s[...] = wd_ref[...].astype(BF16)

    @pl.when(i < nb_ref[0])
    def _():
        x = _unpack_bf16_pairs(x_ref[...]).astype(BF16)
        hid = _silu(_dot(x, wg_s[...])) * _dot(x, wu_s[...])
        o_ref[...] = _pack_bf16_pairs(_dot(hid.astype(BF16), wd_s[...]))

    @pl.when(i >= nb_ref[0])
    def _():
        o_ref[...] = jnp.zeros_like(o_ref)


def _experts(xb, block_e, n_used, layer, w_gate, w_up, w_down):
    n_slots = xb.shape[0]
    n_blocks = n_slots // MOE_BLOCK
    d, de = w_gate.shape[-2:]
    wspec = lambda shape: pl.BlockSpec((None, None) + shape, lambda i, be, nb: (layer, be[i], 0, 0))
    return pl.pallas_call(
        _experts_kernel,
        grid_spec=pltpu.PrefetchScalarGridSpec(
            num_scalar_prefetch=2,
            grid=(n_blocks,),
            in_specs=[pl.BlockSpec((MOE_BLOCK, d // 2), lambda i, be, nb: (i, 0)),
                      wspec((d, de)), wspec((d, de)), wspec((de, d))],
            out_specs=pl.BlockSpec((MOE_BLOCK, d // 2), lambda i, be, nb: (i, 0)),
            scratch_shapes=[pltpu.VMEM((d, de), BF16), pltpu.VMEM((d, de), BF16), pltpu.VMEM((de, d), BF16)]),
        out_shape=jax.ShapeDtypeStruct((n_slots, d // 2), jnp.uint32),
        compiler_params=_params("arbitrary"),
        name="moe_experts",
    )(block_e, n_used, xb, w_gate, w_up, w_down)


def _ffn_post_kernel(h_ref, y0_ref, y1_ref, w_ref, gt_ref, g_ref, o_ref, *, final):
    w = w_ref[...]
    y = w[:, 0:1] * _unpack_bf16_pairs(y0_ref[...]) + w[:, 1:2] * _unpack_bf16_pairs(y1_ref[...])
    out = h_ref[...] + gt_ref[...] * y
    o_ref[...] = _rms(out, g_ref[...]) if final else out


def _ffn_post(h, y, w, gate, n_ctx, final_g=None):
    bsz, t, d = h.shape
    tm = ROW_TILE
    seg = lambda b, i: (b, (i * tm >= n_ctx).astype(jnp.int32), 0, 0)
    row = lambda width: pl.BlockSpec((None, tm, width), lambda b, i: (b, i, 0))
    choice = lambda kk: pl.BlockSpec((None, None, tm, d // 2), lambda b, i: (kk, b, i, 0))
    final = final_g is not None
    g = final_g if final else jnp.ones((d,), F32)
    return pl.pallas_call(
        functools.partial(_ffn_post_kernel, final=final),
        grid=(bsz, t // tm),
        in_specs=[row(d), choice(0), choice(1), row(LANES), pl.BlockSpec((None, None, 1, d), seg),
                  pl.BlockSpec((1, d), lambda b, i: (0, 0))],
        out_specs=row(d),
        out_shape=jax.ShapeDtypeStruct((bsz, t, d), F32),
        compiler_params=_params("parallel", "parallel"),
        name="ffn_post",
    )(h, y, y, w, gate, g.reshape(1, d))


def _slot_layout(n, ri, counts):
    e = jnp.swapaxes(ri[:, 0:2], 0, 1).reshape(TOP_K, n)
    rank = jnp.swapaxes(ri[:, 2:4], 0, 1).reshape(TOP_K, n)
    padded = (counts + MOE_BLOCK - 1) // MOE_BLOCK * MOE_BLOCK
    pend = jnp.cumsum(padded)
    pstart = pend - padded
    experts = jnp.arange(N_EXPERTS, dtype=jnp.int32)
    dest = rank + jnp.sum(jnp.where(e[..., None] == experts, pstart, 0), axis=-1)
    n_slots = (n * TOP_K + MOE_BLOCK - 1) // MOE_BLOCK * MOE_BLOCK + N_EXPERTS * MOE_BLOCK
    n_blocks = n_slots // MOE_BLOCK
    blk0 = jnp.arange(n_blocks, dtype=jnp.int32)[:, None] * MOE_BLOCK
    block_e = jnp.minimum(jnp.sum((pend[None, :] <= blk0).astype(jnp.int32), axis=-1), N_EXPERTS - 1)
    n_used = (pend[-1] // MOE_BLOCK).astype(jnp.int32).reshape(1)
    return dest, n_slots, block_e.astype(jnp.int32), n_used


SC_CORES, SC_SUBCORES = 2, 16
SC_WINDOW = 32


def _gather_rows(table, idx):
    n_rows, d = idx.shape[0], table.shape[1]
    workers = SC_CORES * SC_SUBCORES
    per_worker = n_rows // workers
    assert per_worker * workers == n_rows and per_worker % SC_WINDOW == 0
    mesh = plsc.VectorSubcoreMesh(core_axis_name="c", subcore_axis_name="s")

    @functools.partial(
        pl.kernel, mesh=mesh,
        out_type=jax.ShapeDtypeStruct((n_rows, d), table.dtype),
        scratch_types=[pltpu.VMEM((SC_WINDOW,), jnp.int32), pltpu.VMEM((SC_WINDOW,), jnp.int32),
                       pltpu.VMEM((SC_WINDOW, d), table.dtype), pltpu.VMEM((SC_WINDOW, d), table.dtype),
                       pltpu.SemaphoreType.DMA, pltpu.SemaphoreType.DMA],
    )
    def gather_kernel(table_hbm, idx_hbm, out_hbm, idx0, idx1, rows0, rows1, sem0, sem1):
        base = (lax.axis_index("s") * SC_CORES + lax.axis_index("c")) * per_worker
        n_win = per_worker // SC_WINDOW
        slots = ((idx0, rows0, sem0), (idx1, rows1, sem1))
        window = lambda j: pl.ds(pl.multiple_of(base + j * SC_WINDOW, 8), SC_WINDOW)

        def start(j, slot):
            idx_v, rows_v, sem = slots[slot]
            pltpu.sync_copy(idx_hbm.at[window(j)], idx_v)
            pltpu.async_copy(table_hbm.at[idx_v], rows_v, sem)

        def finish(j, slot):
            idx_v, rows_v, sem = slots[slot]
            pltpu.make_async_copy(table_hbm.at[idx_v], rows_v, sem).wait()
            pltpu.sync_copy(rows_v, out_hbm.at[window(j)])

        start(0, 0)

        @pl.loop(0, n_win, step=2)
        def _(j):
            @pl.when(j + 1 < n_win)
            def _():
                start(j + 1, 1)

            finish(j, 0)

            @pl.when(j + 2 < n_win)
            def _():
                start(j + 2, 0)

            @pl.when(j + 1 < n_win)
            def _():
                finish(j + 1, 1)

    return gather_kernel(table, idx)


SC_SCATTER_WINDOW = 16


def _scatter_rows(src, dest, n_slots):
    n, d = src.shape
    workers = SC_CORES * SC_SUBCORES
    per_worker = n // workers
    win = SC_SCATTER_WINDOW
    assert per_worker * workers == n and per_worker % win == 0 and dest.shape == (TOP_K, n)
    mesh = plsc.VectorSubcoreMesh(core_axis_name="c", subcore_axis_name="s")

    @functools.partial(
        pl.kernel, mesh=mesh,
        out_type=jax.ShapeDtypeStruct((n_slots, d), src.dtype),
        scratch_types=[pltpu.VMEM((win,), jnp.int32), pltpu.VMEM((win,), jnp.int32),
                       pltpu.VMEM((win, d), src.dtype), pltpu.SemaphoreType.DMA, pltpu.SemaphoreType.DMA],
    )
    def scatter_kernel(src_hbm, dest_hbm, out_hbm, idx0, idx1, rows_v, sem0, sem1):
        base = (lax.axis_index("s") * SC_CORES + lax.axis_index("c")) * per_worker

        @pl.loop(0, per_worker // win)
        def _(j):
            rows = pl.ds(pl.multiple_of(base + j * win, 8), win)
            pltpu.sync_copy(src_hbm.at[rows], rows_v)
            pltpu.sync_copy(dest_hbm.at[0, rows], idx0)
            pltpu.sync_copy(dest_hbm.at[1, rows], idx1)
            first = pltpu.async_copy(rows_v, out_hbm.at[idx0], sem0)
            second = pltpu.async_copy(rows_v, out_hbm.at[idx1], sem1)
            first.wait()
            second.wait()

    return scatter_kernel(src, dest)


def _alongside(gather, idx, side_fn, side_in):
    idx, side_in = lax.optimization_barrier((idx, side_in))
    return lax.optimization_barrier((gather(idx), side_fn(side_in)))


def _moe(h, g, shift, scale, gate, router_w_pad, router_b, layer, w_gate, w_up, w_down, n_ctx,
         final_g=None, side=None):
    bsz, t, d = h.shape
    n = bsz * t
    v, ri, rwt, counts = _ffn_pre(h, g, shift, scale, router_w_pad, router_b, n_ctx)
    dest, n_slots, block_e, n_used = _slot_layout(n, ri, counts[:, 0].astype(jnp.int32))
    w = jnp.swapaxes(rwt[:, 0:2], 1, 2).reshape(n, TOP_K)
    dispatch = lambda idx: _scatter_rows(v.reshape(n, d // 2), idx, n_slots)
    if side is None:
        xb = dispatch(dest)
    else:
        xb, side_a = _alongside(dispatch, dest, *side[0])
    yb = _experts(xb, block_e, n_used, layer, w_gate, w_up, w_down)
    combine = lambda idx: _gather_rows(yb, idx)
    dest_flat = dest.reshape(-1)
    if side is None:
        y, side_b = combine(dest_flat), None
    else:
        y, side_b = _alongside(combine, dest_flat, side[1], side_a)
    wpad = jnp.zeros((n, LANES), F32).at[:, :TOP_K].set(w).reshape(bsz, t, LANES)
    out = _ffn_post(h, y.reshape(TOP_K, bsz, t, d // 2), wpad, gate, n_ctx, final_g)
    return out if side is None else (out, side_b)


DFT_STEP = 16


def _dft_tables(n):
    r, *mats = _dft_tables_np(n)
    return (r,) + tuple(jnp.asarray(a).astype(BF16) for a in mats)


@functools.lru_cache(maxsize=None)
def _dft_tables_np(n):
    size = 2 * n
    r = int(round(math.sqrt(size)))
    assert r * r == size and r % DFT_STEP == 0
    p1 = np.arange(r // 2)[None, None, :]
    p2 = np.arange(r)[:, None, None]
    k1 = np.arange(r)[None, :, None]
    ang = 2.0 * np.pi * (((r * p1 + p2) * k1) % size) / size
    g_re, g_im = np.cos(ang), -np.sin(ang)
    g_in = np.concatenate([g_re, g_im], axis=1)
    g_out = np.concatenate([np.swapaxes(g_re, 1, 2), np.swapaxes(g_im, 1, 2)], axis=2) / size
    a2 = 2.0 * np.pi * ((np.arange(r)[:, None] * np.arange(r)[None, :]) % r) / r
    f_re, f_im = np.cos(a2), -np.sin(a2)
    f_fwd = np.block([[f_re, -f_im], [f_im, f_re]])
    f_inv = np.block([[f_re, f_im], [-f_im, f_re]])
    p1f = np.arange(r)[None, None, :]
    angf = 2.0 * np.pi * (((r * p1f + p2) * k1) % size) / size
    g_full = np.concatenate([np.cos(angf), -np.sin(angf)], axis=1)
    return (r,) + tuple(a.astype(np.float32) for a in (g_in, g_out, f_fwd, f_inv, g_full))


def _dot_f32_tn(a, b):
    ah = a.astype(BF16)
    al = (a - ah.astype(F32)).astype(BF16)
    bh = b.astype(BF16)
    bl = (b - bh.astype(F32)).astype(BF16)
    return _dot_tn(ah, bh) + (_dot_tn(ah, bl) + _dot_tn(al, bh))


def _hy_filter_kernel(z_ref, t_ref, w1_ref, b1_ref, w2_ref, b2_ref, fr_ref, w3_ref, w3b_ref, rates_ref, o_ref,
                      *, half_tiles):
    i = pl.program_id(0)
    hid = jnp.sin(fr_ref[...] * (_dot_f32(w1_ref[...], z_ref[...]) + b1_ref[...]))
    hid = jnp.sin(fr_ref[...] * (_dot_f32(w2_ref[...], hid) + b2_ref[...]))
    filt = _dot_f32_tn(hid, w3_ref[...])
    decay = jnp.exp(-t_ref[...] * rates_ref[...])
    for o in range(o_ref.shape[0]):
        o_ref[o] = filt[:, o * HY_CH:(o + 1) * HY_CH] * decay

    @pl.when(i == 0)
    def _():
        extra = _dot_f32_tn(hid[:, 0:LANES], w3b_ref[...])[0:8]
        first = lax.broadcasted_iota(jnp.int32, (8, HY_CH), 0) == 0
        for o in range(o_ref.shape[0]):
            add = extra[:, o * HY_CH:(o + 1) * HY_CH] * decay[0:8]
            o_ref[o, 0:8, :] = o_ref[o, 0:8, :] + jnp.where(first, add, 0.0)

    @pl.when(i == half_tiles)
    def _():
        for o in range(o_ref.shape[0]):
            o_ref[o, 0:1, :] = jnp.zeros((1, HY_CH), F32)


HY_TILE = 512


def _hy_kernels(n, w1, b1, w2, b2, w3, freq):
    pos = np.arange(2 * n)
    pos = np.where(pos < n, pos, 2 * n - pos).astype(np.float32)
    t = jnp.asarray(pos / np.float32(n - 1))
    bands = jnp.linspace(1e-4, HY_BANDS - 1, HY_BANDS, dtype=F32)
    ang = (2.0 * math.pi / n) * bands[:, None] * jnp.asarray(pos)[None, :]
    z = jnp.concatenate([t[None, :], jnp.cos(ang), -jnp.sin(ang)], axis=0)
    z = jnp.pad(z, ((0, LANES - z.shape[0]), (0, 0)))
    w1t = jnp.pad(w1, ((0, LANES - w1.shape[0]), (0, 0))).T
    hidden = w1.shape[1]
    col = lambda v: v.reshape(hidden, 1)
    w3d = jnp.swapaxes(w3.reshape(hidden, HY_ORDER, 2, HY_CH), 0, 2)
    w3d = jnp.swapaxes(w3d, 1, 2).reshape(2, hidden, HY_ORDER * HY_CH)
    rates = jnp.abs(jnp.linspace(HY_MIN_DECAY, HY_MAX_DECAY, HY_CH, dtype=F32)).reshape(1, HY_CH)
    tm = HY_TILE
    half_tiles = n // tm
    full = lambda a: pl.BlockSpec(a.shape, lambda i: (0,) * a.ndim)
    small = (w1t, col(b1), w2.T, col(b2), col(freq))
    return pl.pallas_call(
        functools.partial(_hy_filter_kernel, half_tiles=half_tiles),
        grid=(2 * n // tm,),
        in_specs=[pl.BlockSpec((LANES, tm), lambda i: (0, i)), pl.BlockSpec((tm, 1), lambda i: (i, 0))]
                 + [full(a) for a in small]
                 + [pl.BlockSpec((None, hidden, HY_ORDER * HY_CH), lambda i: ((i >= half_tiles).astype(jnp.int32), 0, 0)),
                    pl.BlockSpec((None, hidden, HY_ORDER * HY_CH), lambda i: (1, 0, 0)), full(rates)],
        out_specs=pl.BlockSpec((HY_ORDER, tm, HY_CH), lambda i: (0, i, 0)),
        out_shape=jax.ShapeDtypeStruct((HY_ORDER, 2 * n, HY_CH), F32),
        compiler_params=_params("parallel"),
        name="hy_kernels",
    )(z, t[:, None], *small, w3d, w3d, rates)


def _pack_complex(z):
    r = z.shape[0] // 2
    bits = lax.bitcast_convert_type(z.astype(BF16).astype(F32), jnp.uint32)
    return lax.bitcast_convert_type(bits[0:r] | (bits[r:2 * r] >> 16), F32)


def _unpack_complex(words):
    p = lax.bitcast_convert_type(words, jnp.uint32)
    re = lax.bitcast_convert_type(p & jnp.uint32(0xFFFF0000), F32)
    im = lax.bitcast_convert_type(p << 16, F32)
    return jnp.concatenate([re, im], axis=0).astype(BF16)


def _load_every(ref, j, count):
    return ref.reshape(count * DFT_STEP, LANES)[pl.ds(j, count, stride=DFT_STEP), :]


def _store_every(ref, j, count, val):
    ref.reshape(count * DFT_STEP, LANES)[pl.ds(j, count, stride=DFT_STEP), :] = val


def _dft_in_kernel(x_ref, g_ref, a_ref):
    rh = x_ref.shape[0]
    for j in range(DFT_STEP):
        a_ref[j] = _pack_complex(_dot(g_ref[j], _load_every(x_ref, j, rh).astype(BF16)))


def _dft_in(x4, col, g_in):
    bx, rh, r, _ = x4.shape
    c = HY_CH
    cbs = c // LANES
    return pl.pallas_call(
        _dft_in_kernel,
        grid=(bx, r // DFT_STEP, cbs),
        in_specs=[pl.BlockSpec((None, rh, DFT_STEP, LANES), lambda b, i, cb: (b, 0, i, col * cbs + cb)),
                  pl.BlockSpec((DFT_STEP, 2 * r, rh), lambda b, i, cb: (i, 0, 0))],
        out_specs=pl.BlockSpec((None, DFT_STEP, r, LANES), lambda b, i, cb: (b, i, 0, cb)),
        out_shape=jax.ShapeDtypeStruct((bx, r, r, c), F32),
        compiler_params=_params("parallel", "parallel", "parallel"),
        name="dft_in",
    )(x4, g_in)


def _stage2_operand(a_ref, j, r):
    return jnp.concatenate([_unpack_complex(_load_every(a_ref.at[b], j, r)) for b in range(a_ref.shape[0])],
                           axis=1)


def _dft_filt_kernel(a_ref, f_ref, k_ref):
    r = f_ref.shape[0] // 2
    for j in range(DFT_STEP):
        s = _dot(f_ref[...], _stage2_operand(a_ref, j, r))
        for o in range(a_ref.shape[0]):
            k_ref[o, j] = s[:, o * LANES:(o + 1) * LANES]


def _dft_filt(a, f_fwd):
    nq, r, _, c = a.shape
    return pl.pallas_call(
        _dft_filt_kernel,
        grid=(r // DFT_STEP, c // LANES),
        in_specs=[pl.BlockSpec((nq, r, DFT_STEP, LANES), lambda i, cb: (0, 0, i, cb)),
                  pl.BlockSpec(f_fwd.shape, lambda i, cb: (0, 0))],
        out_specs=pl.BlockSpec((nq, DFT_STEP, 2 * r, LANES), lambda i, cb: (0, i, 0, cb)),
        out_shape=jax.ShapeDtypeStruct((nq, r, 2 * r, c), F32),
        compiler_params=_params("parallel", "parallel"),
        name="dft_filt",
    )(a, f_fwd)


def _dft_mid_kernel(a_ref, k_ref, ff_ref, fi_ref, b_ref):
    r = ff_ref.shape[0] // 2
    n_seq = a_ref.shape[0]
    for j in range(DFT_STEP):
        s = _dot(ff_ref[...], _stage2_operand(a_ref, j, r))
        sr, si = s[0:r], s[r:2 * r]
        kr = jnp.concatenate([k_ref[j, 0:r, :]] * n_seq, axis=1)
        ki = jnp.concatenate([k_ref[j, r:2 * r, :]] * n_seq, axis=1)
        p = jnp.concatenate([sr * kr - si * ki, sr * ki + si * kr], axis=0).astype(BF16)
        back = _dot(fi_ref[...], p)
        for b in range(n_seq):
            b_ref[b, j] = _pack_complex(back[:, b * LANES:(b + 1) * LANES])


def _dft_mid(a, kspec, order, f_fwd, f_inv):
    bsz, r, _, c = a.shape
    return pl.pallas_call(
        _dft_mid_kernel,
        grid=(r // DFT_STEP, c // LANES),
        in_specs=[pl.BlockSpec((bsz, r, DFT_STEP, LANES), lambda i, cb: (0, 0, i, cb)),
                  pl.BlockSpec((None, DFT_STEP, 2 * r, LANES), lambda i, cb: (order, i, 0, cb)),
                  pl.BlockSpec(f_fwd.shape, lambda i, cb: (0, 0)),
                  pl.BlockSpec(f_inv.shape, lambda i, cb: (0, 0))],
        out_specs=pl.BlockSpec((bsz, DFT_STEP, r, LANES), lambda i, cb: (0, i, 0, cb)),
        out_shape=jax.ShapeDtypeStruct((bsz, r, r, c), F32),
        compiler_params=_params("parallel", "parallel"),
        name="dft_mid",
    )(a, kspec, f_fwd, f_inv)


def _dft_out_kernel(b_ref, g_ref, u_ref, x_ref, bias_ref, o_ref):
    r, rh = b_ref.shape[0], o_ref.shape[0]
    for j in range(DFT_STEP):
        y = _dot(g_ref[j], _unpack_complex(_load_every(b_ref, j, r)))
        _store_every(o_ref, j, rh, _load_every(x_ref, j, rh) * (y + _load_every(u_ref, j, rh) * bias_ref[...]))


def _dft_out(bm, g_out, u4, u_col, x4, x_col, bias):
    bsz, r, _, c = bm.shape
    rh = r // 2
    cbs = c // LANES
    seq = lambda col: pl.BlockSpec((None, rh, DFT_STEP, LANES), lambda b, i, cb: (b, 0, i, col * cbs + cb))
    return pl.pallas_call(
        _dft_out_kernel,
        grid=(bsz, r // DFT_STEP, cbs),
        in_specs=[pl.BlockSpec((None, r, DFT_STEP, LANES), lambda b, i, cb: (b, 0, i, cb)),
                  pl.BlockSpec((DFT_STEP, rh, 2 * r), lambda b, i, cb: (i, 0, 0)),
                  seq(u_col), seq(x_col), pl.BlockSpec((1, LANES), lambda b, i, cb: (0, cb))],
        out_specs=seq(0),
        out_shape=jax.ShapeDtypeStruct((bsz, rh, r, c), F32),
        compiler_params=_params("parallel", "parallel", "parallel"),
        name="dft_out",
    )(bm, g_out, u4, x4, bias.reshape(1, c))


def _hyena_filter_stage1(n, filter_params):
    r, g_full = _dft_tables(n)[0], _dft_tables(n)[5]
    kern = _hy_kernels(n, *filter_params)
    return _dft_in(kern.reshape(-1, r, r, HY_CH), 0, g_full)


def _hyena_filter_spectra(n, stage1):
    return _dft_filt(stage1, _dft_tables(n)[3])


def _hyena(hy_in, kspec, conv_bias):
    bsz, n, _ = hy_in.shape
    r, g_in, g_out, f_fwd, f_inv, _ = _dft_tables(n)
    seq4 = hy_in.reshape(bsz, r // 2, r, 3 * HY_CH)
    zz = _dft_out(_dft_mid(_dft_in(seq4, 0, g_in), kspec, 0, f_fwd, f_inv), g_out,
                  seq4, 0, seq4, 1, conv_bias[0])
    out = _dft_out(_dft_mid(_dft_in(zz, 0, g_in), kspec, 1, f_fwd, f_inv), g_out,
                   zz, 0, seq4, 2, conv_bias[1])
    return out.reshape(bsz, n, HY_CH)


def _reorder_ab(w):
    gq, gk, gv, gg, lr_f, lr_b, hq, hf_f, hf_b, hi, hg = jnp.split(
        w, np.cumsum([256, 256, 512, 512, 16, 16, 512, 512, 512, 512, 512])[:-1].tolist(), axis=-1)
    pad = jnp.zeros((w.shape[0], AB_PAD_COLS - AB_LR - 2 * GLA_LOW_RANK), w.dtype)
    return jnp.concatenate([gq, gk, gv, gg, hq, hf_f, hf_b, hi, hg, lr_f, lr_b, pad], axis=-1)


def _reorder_cd(w):
    hy, z, xbc, dt_f, dt_b = jnp.split(w, np.cumsum([1536, 512, 1024, 8, 8])[:-1].tolist(), axis=-1)
    pad = jnp.zeros((w.shape[0], LANES - MB_HEADS), w.dtype)
    return jnp.concatenate([hy, xbc, z, dt_f, pad, dt_b, pad], axis=-1)


def kernel(x, c, ctx, c_ctx, ada_w, ada_b, norm_mix_g, norm_ffn_g, norm_out_g, ab_w_in, ab_w_out, gla_gate_w, gla_gate_b, gla_norm_g, hg_lb, hg_norm_g, cd_w_in, cd_w_out, hy_short_w, hy_short_b, hy_w1, hy_b1, hy_w2, hy_b2, hy_w3, hy_freq, hy_bias, mb_conv_w, mb_conv_b, mb_dt_bias, mb_a_log, mb_d, mb_norm_g, router_w, router_b, moe_w_gate, moe_w_up, moe_w_down):
    bsz, n_lat, d = x.shape
    n_ctx = ctx.shape[1]
    t = n_ctx + n_lat
    assert ada_w.shape[0] == 2 and ab_w_in.shape[0] == 1 and cd_w_in.shape[0] == 1

    cond = jnp.zeros((8, d), F32).at[:bsz].set(c).at[bsz].set(c_ctx)
    m = _adaln(cond, ada_w, ada_b)

    def mods(layer):
        lat = m[layer, :bsz].reshape(bsz, 6, d)
        cx = jnp.broadcast_to(m[layer, bsz].reshape(1, 6, d), (bsz, 6, d))
        both = jnp.stack([cx, lat], axis=1)
        return [both[:, :, j][:, :, None, :] for j in range(6)]

    lb_all = jnp.cumsum(jax.nn.softmax(hg_lb.astype(F32), axis=1), axis=1)
    router_w_pad = jnp.zeros((d, LANES), F32).at[:, :N_EXPERTS].set(router_w)
    h = jnp.concatenate([ctx, x], axis=1)

    sh_m, sc_m, gt_m, sh_f, sc_f, gt_f = mods(0)
    proj = _norm_proj(h, norm_mix_g[0], sh_m, sc_m, _reorder_ab(ab_w_in[0]).astype(BF16), n_ctx)
    gwp = [jnp.zeros((LANES, GLA_KEY_W), F32).at[GLA_LOW_RANK * dd:GLA_LOW_RANK * (dd + 1)].set(gla_gate_w[0, dd])
           for dd in range(2)]
    o_gla = _gla_scan(proj, gwp, [gla_gate_b[0, dd].reshape(1, -1) for dd in range(2)], n_ctx)
    o_hg = _hgrn_scan(proj, [lb_all[dd, 0].reshape(1, -1) for dd in range(2)], n_ctx)
    h = _mix_out_ab(h, o_gla, o_hg, proj, gla_norm_g[0], hg_norm_g[0], ab_w_out[0].astype(BF16), gt_m, n_ctx)
    filter_params = (hy_w1[0], hy_b1[0], hy_w2[0], hy_b2[0], hy_w3[0], hy_freq[0])
    side = ((functools.partial(_hyena_filter_stage1, n_lat), filter_params),
            functools.partial(_hyena_filter_spectra, n_lat))
    h, kspec = _moe(h, norm_ffn_g[0], sh_f, sc_f, gt_f, router_w_pad, router_b,
                    0, moe_w_gate, moe_w_up, moe_w_down, n_ctx, side=side)

    sh_m, sc_m, gt_m, sh_f, sc_f, gt_f = mods(1)
    conv_w = jnp.concatenate([hy_short_w[0], mb_conv_w[0]], axis=0).T
    conv_b = jnp.concatenate([hy_short_b[0], mb_conv_b[0]]).reshape(1, -1)
    hy_in, xbc, zdt = _norm_proj_conv(h, norm_mix_g[1], sh_m, sc_m, _reorder_cd(cd_w_in[0]).astype(BF16),
                                      conv_w, conv_b, 3 * HY_CH, n_ctx)
    hy = _hyena(hy_in, kspec, hy_bias[0])
    y_ssd = _ssd_scan(xbc, zdt, MB_INNER // LANES, mb_dt_bias[0], mb_a_log[0], n_ctx)
    d_skip_x = jnp.repeat(mb_d[0], MB_HEAD_DIM).reshape(1, MB_INNER)
    h = _mix_out_cd(h, hy, y_ssd, xbc, zdt, 0, d_skip_x, mb_norm_g[0],
                    cd_w_out[0].astype(BF16), gt_m, n_ctx)
    return _moe(h, norm_ffn_g[1], sh_f, sc_f, gt_f, router_w_pad, router_b,
                1, moe_w_gate, moe_w_up, moe_w_down, 0, final_g=norm_out_g)
```

```python
import functools
import math

import numpy as np
import jax
import jax.numpy as jnp
from jax import lax
from jax.experimental import pallas as pl
from jax.experimental.pallas import tpu as pltpu
from jax.experimental.pallas import tpu_sc as plsc

NORM_EPS = 1e-6
GLA_HEADS, GLA_DK, GLA_DV, GLA_LOW_RANK, GLA_TAU = 4, 64, 128, 16, 16.0
GLA_KEY_W, GLA_VAL_W = GLA_HEADS * GLA_DK, GLA_HEADS * GLA_DV
HG_HEADS, HG_EXPAND, HG_DV = 4, 128, 128
HG_KEY_W, HG_VAL_W = HG_HEADS * HG_EXPAND, HG_HEADS * HG_DV
HY_CH, HY_ORDER, HY_SHORT, HY_BANDS, HY_FILT_HID = 512, 2, 3, 16, 64
HY_MIN_DECAY = math.log(1e-2) / 1.5
HY_MAX_DECAY = math.log(1e-2) / 0.3
MB_HEADS, MB_HEAD_DIM, MB_GROUPS, MB_STATE = 8, 64, 2, 128
MB_INNER = MB_HEADS * MB_HEAD_DIM
MB_BC_W = MB_GROUPS * MB_STATE
N_EXPERTS, N_GROUPS, TOP_K, MOE_BLOCK = 16, 4, 2, 256
EXPERTS_PER_GROUP = N_EXPERTS // N_GROUPS

LANES = 128
SCAN_CHUNK = 64
SCAN_BLOCK = 256
SSD_CHUNK = 128
ROW_TILE = 256
VMEM_LIMIT = 56 * 1024 * 1024

BF16 = jnp.bfloat16
F32 = jnp.float32


def _params(*sem):
    return pltpu.CompilerParams(dimension_semantics=sem, vmem_limit_bytes=VMEM_LIMIT)


def _split3(x):
    hi = x.astype(BF16)
    r1 = x - hi.astype(F32)
    mid = r1.astype(BF16)
    lo = (r1 - mid.astype(F32)).astype(BF16)
    return hi, mid, lo


def _dot(a, b):
    return jnp.dot(a, b, preferred_element_type=F32)


def _dot_nt(a, b):
    return lax.dot_general(a, b, (((1,), (1,)), ((), ())), preferred_element_type=F32)


def _dot_tn(a, b):
    return lax.dot_general(a, b, (((0,), (0,)), ((), ())), preferred_element_type=F32)


def _sel_dot(m01, x):
    hi, mid, lo = _split3(x)
    return _dot(m01, hi) + (_dot(m01, mid) + _dot(m01, lo))


def _dot_sel(x, m01):
    hi, mid, lo = _split3(x)
    return _dot(hi, m01) + (_dot(mid, m01) + _dot(lo, m01))


def _dot_f32(a, b):
    ah = a.astype(BF16)
    al = (a - ah.astype(F32)).astype(BF16)
    bh = b.astype(BF16)
    bl = (b - bh.astype(F32)).astype(BF16)
    return _dot(ah, bh) + (_dot(ah, bl) + _dot(al, bh))


def _silu(x):
    return x * (1.0 / (1.0 + jnp.exp(-x)))


def _sigmoid(x):
    return 1.0 / (1.0 + jnp.exp(-x))


def _softplus(x):
    return jnp.maximum(x, 0.0) + jnp.log(1.0 + jnp.exp(-jnp.abs(x)))


def _pack_bf16_pairs(x):
    bits = lax.bitcast_convert_type(x.astype(BF16).astype(F32), jnp.uint32)
    half = x.shape[1] // 2
    return bits[:, :half] | (bits[:, half:] >> 16)


def _unpack_bf16_pairs(p):
    hi = lax.bitcast_convert_type(p & jnp.uint32(0xFFFF0000), F32)
    lo = lax.bitcast_convert_type(p << 16, F32)
    return jnp.concatenate([hi, lo], axis=1)


def _rms(x, g):
    return x * lax.rsqrt(jnp.mean(x * x, axis=-1, keepdims=True) + NORM_EPS) * g


def _adaln_kernel(c_ref, w_ref, b_ref, o_ref):
    o_ref[...] = _dot_f32(_silu(c_ref[...]), w_ref[...]) + b_ref[...]


def _adaln(cond, w, b):
    n_l, d, n6 = w.shape
    tn = 1536
    return pl.pallas_call(
        _adaln_kernel,
        grid=(n_l, n6 // tn),
        in_specs=[pl.BlockSpec((8, d), lambda l, j: (0, 0)),
                  pl.BlockSpec((None, d, tn), lambda l, j: (l, 0, j)),
                  pl.BlockSpec((None, 1, tn), lambda l, j: (l, 0, j))],
        out_specs=pl.BlockSpec((None, 8, tn), lambda l, j: (l, 0, j)),
        out_shape=jax.ShapeDtypeStruct((n_l, 8, n6), F32),
        compiler_params=_params("parallel", "parallel"),
        name="adaln",
    )(cond, w, b.reshape(n_l, 1, n6))


def _norm_proj_kernel(h_ref, g_ref, sh_ref, sc_ref, w_ref, o_ref):
    u = _rms(h_ref[...], g_ref[...]) * (1.0 + sc_ref[...]) + sh_ref[...]
    o_ref[...] = _dot(u.astype(BF16), w_ref[...])


def _norm_proj(h, g, shift, scale, w, n_ctx):
    bsz, t, d = h.shape
    n = w.shape[1]
    tm = ROW_TILE
    seg = lambda b, i: (b, (i * tm >= n_ctx).astype(jnp.int32), 0, 0)
    return pl.pallas_call(
        _norm_proj_kernel,
        grid=(bsz, t // tm),
        in_specs=[pl.BlockSpec((None, tm, d), lambda b, i: (b, i, 0)),
                  pl.BlockSpec((1, d), lambda b, i: (0, 0)),
                  pl.BlockSpec((None, None, 1, d), seg),
                  pl.BlockSpec((None, None, 1, d), seg),
                  pl.BlockSpec((d, n), lambda b, i: (0, 0))],
        out_specs=pl.BlockSpec((None, tm, n), lambda b, i: (b, i, 0)),
        out_shape=jax.ShapeDtypeStruct((bsz, t, n), F32),
        compiler_params=_params("parallel", "parallel"),
        name="norm_proj",
    )(h, g.reshape(1, d), shift, scale, w)


def _scan_constants(c, reverse):
    t = np.arange(c)[:, None]
    u = np.arange(c)[None, :]
    sels = [u <= t, u > t]
    masks = []
    m = c // 2
    while m >= 1:
        blk = t // (2 * m)
        upper_t = (t % (2 * m)) >= m
        r = blk * (2 * m) + m - 1
        s_blk = u // (2 * m)
        upper_s = (u % (2 * m)) >= m
        sels.append((upper_t & (u > r) & (u <= t)) | ((~upper_t) & (u > t) & (u <= r)))
        masks.append((blk == s_blk) & upper_t & (~upper_s))
        m //= 2
    masks.append(t == u)
    sel = np.stack(sels).astype(np.float32)
    msk = np.stack(masks).astype(np.float32)
    if reverse:
        sel = sel[:, ::-1, ::-1]
        msk = msk[:, ::-1, ::-1]
    return np.ascontiguousarray(sel.reshape(-1, c)), np.ascontiguousarray(msk)


def _chunk_order(i, n_ctx_chunks, n_chunks, reverse):
    if not reverse:
        return i
    return jnp.where(i < n_ctx_chunks, n_ctx_chunks - 1 - i, n_chunks - 1 - (i - n_ctx_chunks))


GROUP_KEYS = 256


def _decay_chunk(q, k, v, la, consts, st_ref, heads, dk, dv):
    sel_ref, mask_ref, hm_ref, hmb_ref, vm_ref = consts
    c = q.shape[0]
    n_lvl = mask_ref.shape[0] - 1
    hpg = GROUP_KEYS // dk
    cs = _dot(sel_ref[...], jnp.concatenate(_split3(la), axis=0))
    e_q = jnp.exp(cs[0:c])
    e_k = jnp.exp(cs[c:2 * c])
    e_tot = jnp.exp(jnp.sum(la, axis=0, keepdims=True))
    vb = v.astype(BF16)
    outs = []
    for g in range(heads // hpg):
        ks = slice(g * GROUP_KEYS, (g + 1) * GROUP_KEYS)
        vs = slice(g * hpg * dv, (g + 1) * hpg * dv)
        qg, kg = q[:, ks], k[:, ks]
        key_stack = lambda x: jnp.concatenate([x.astype(BF16) * hmb_ref[h] for h in range(hpg)], axis=0)
        att = mask_ref[n_lvl] * _dot_nt(qg.astype(BF16), key_stack(kg))
        for l in range(n_lvl):
            e = jnp.exp(cs[(2 + l) * c:(3 + l) * c, ks])
            att = att + mask_ref[l] * _dot_nt((qg * e).astype(BF16), key_stack(kg * e))
        v_blocks = jnp.concatenate([vb[:, vs] * vm_ref[h] for h in range(hpg)], axis=0)
        intra = _dot(att.astype(BF16), v_blocks)
        st = st_ref[g]
        q_stack = jnp.concatenate([(qg * e_q[:, ks]) * hm_ref[h] for h in range(hpg)], axis=0)
        inter = _dot_nt(q_stack.astype(BF16), st.astype(BF16))
        upd = _dot_tn(vb[:, vs], (kg * e_k[:, ks]).astype(BF16))
        new = st * e_tot[:, ks]
        for h in range(hpg):
            new = new + upd[h * dv:(h + 1) * dv] * hm_ref[h]
        st_ref[g] = new
        outs.append(intra + jnp.concatenate([inter[h * c:(h + 1) * c] for h in range(hpg)], axis=-1))
    return jnp.concatenate(outs, axis=-1)


def _log_sigmoid(x):
    return jnp.minimum(x, 0.0) - jnp.log(1.0 + jnp.exp(-jnp.abs(x)))


def _gla_kernel(*refs):
    ins, head_masks, (o_refs, st_refs) = (refs[0:8], refs[8:16]), refs[16:19], (refs[19:21], refs[21:23])

    @pl.when(pl.program_id(1) == 0)
    def _():
        for st_ref in st_refs:
            st_ref[...] = jnp.zeros_like(st_ref)

    for d, ((q_ref, k_ref, v_ref, lr_ref, gw_ref, gb_ref, sel_ref, mask_ref), o_ref, st_ref) in enumerate(
            zip(ins, o_refs, st_refs)):
        z = _dot_f32(lr_ref[...], gw_ref[...]) + gb_ref[...]
        la = _log_sigmoid(z) * (1.0 / GLA_TAU)
        q = q_ref[...] * (GLA_DK ** -0.5)
        k, v = k_ref[...], v_ref[...]
        for rows in _sub_chunks(q.shape[0], d == 1):
            o_ref[rows, :] = _decay_chunk(q[rows], k[rows], v[rows], la[rows], (sel_ref, mask_ref) + head_masks,
                                          st_ref, GLA_HEADS, GLA_DK, GLA_DV)


def _hgrn_kernel(*refs):
    ins, head_masks, (o_refs, st_refs) = (refs[0:6], refs[6:12]), refs[12:15], (refs[15:17], refs[17:19])

    @pl.when(pl.program_id(1) == 0)
    def _():
        for st_ref in st_refs:
            st_ref[...] = jnp.zeros_like(st_ref)

    for d, ((q_ref, f_ref, v_ref, lb_ref, sel_ref, mask_ref), o_ref, st_ref) in enumerate(
            zip(ins, o_refs, st_refs)):
        lb = lb_ref[...]
        f = lb + (1.0 - lb) * _sigmoid(f_ref[...])
        q, k, v, la = _silu(q_ref[...]), 1.0 - f, v_ref[...], jnp.log(f)
        for rows in _sub_chunks(q.shape[0], d == 1):
            o_ref[rows, :] = _decay_chunk(q[rows], k[rows], v[rows], la[rows], (sel_ref, mask_ref) + head_masks,
                                          st_ref, HG_HEADS, HG_EXPAND, HG_DV)


def _sub_chunks(rows, reverse):
    order = range(rows // SCAN_CHUNK)
    return [slice(j * SCAN_CHUNK, (j + 1) * SCAN_CHUNK) for j in (reversed(order) if reverse else order)]


def _scan_specs(blk, n_ctx, t, reverse, chunk=None, stacked_heads=1):
    n_blocks = t // blk
    order = functools.partial(_chunk_order, n_ctx_chunks=n_ctx // blk, n_chunks=n_blocks, reverse=reverse)

    def col(width, idx):
        return pl.BlockSpec((None, blk, width), lambda b, i: (b, order(i), idx))

    sel, msk = _scan_constants(chunk or blk, reverse)
    sel3 = np.concatenate([sel, sel, sel], axis=1)
    msk = np.tile(msk, (1, 1, stacked_heads))
    const = lambda a: pl.BlockSpec(a.shape, lambda b, i: (0,) * a.ndim)
    return n_blocks, col, const, jnp.asarray(sel3, BF16), jnp.asarray(msk, F32)


def _head_masks(dk, dv):
    hpg = GROUP_KEYS // dk
    hm = np.zeros((hpg, 1, GROUP_KEYS), np.float32)
    vm = np.zeros((hpg, 1, hpg * dv), np.float32)
    for h in range(hpg):
        hm[h, 0, h * dk:(h + 1) * dk] = 1.0
        vm[h, 0, h * dv:(h + 1) * dv] = 1.0
    return jnp.asarray(hm), jnp.asarray(hm, BF16), jnp.asarray(vm, BF16)


AB_Q, AB_K, AB_V, AB_G = 0, 256, 512, 1024
AB_HQ, AB_HF, AB_HI, AB_HG, AB_LR = 1536, 2048, 3072, 3584, 4096
AB_PAD_COLS = 4224


def _gla_scan(proj, gate_w_pad, gate_b, n_ctx):
    bsz, t, _ = proj.shape
    hpg = GROUP_KEYS // GLA_DK
    in_specs, args, outs = [], [], []
    for d in range(2):
        n_blocks, col, const, sel, msk = _scan_specs(SCAN_BLOCK, n_ctx, t, d == 1, SCAN_CHUNK, hpg)
        in_specs += [col(GLA_KEY_W, AB_Q // GLA_KEY_W), col(GLA_KEY_W, AB_K // GLA_KEY_W),
                     col(GLA_VAL_W, AB_V // GLA_VAL_W), col(LANES, AB_LR // LANES),
                     const(gate_w_pad[d]), const(gate_b[d]), const(sel), const(msk)]
        args += [proj, proj, proj, proj, gate_w_pad[d], gate_b[d], sel, msk]
        outs.append(col(GLA_VAL_W, 0))
    hm = _head_masks(GLA_DK, GLA_DV)
    return pl.pallas_call(
        _gla_kernel,
        grid=(bsz, n_blocks),
        in_specs=in_specs + [const(m) for m in hm],
        out_specs=outs,
        out_shape=[jax.ShapeDtypeStruct((bsz, t, GLA_VAL_W), F32)] * 2,
        scratch_shapes=[pltpu.VMEM((GLA_HEADS // hpg, GLA_DV, GROUP_KEYS), F32)] * 2,
        compiler_params=_params("parallel", "arbitrary"),
        name="gla_scan",
    )(*args, *hm)


def _hgrn_scan(proj, lb, n_ctx):
    bsz, t, _ = proj.shape
    hpg = GROUP_KEYS // HG_EXPAND
    in_specs, args, outs = [], [], []
    for d in range(2):
        n_blocks, col, const, sel, msk = _scan_specs(SCAN_BLOCK, n_ctx, t, d == 1, SCAN_CHUNK, hpg)
        in_specs += [col(HG_KEY_W, AB_HQ // HG_KEY_W), col(HG_KEY_W, AB_HF // HG_KEY_W + d),
                     col(HG_VAL_W, AB_HI // HG_VAL_W), const(lb[d]), const(sel), const(msk)]
        args += [proj, proj, proj, lb[d], sel, msk]
        outs.append(col(HG_VAL_W, 0))
    hm = _head_masks(HG_EXPAND, HG_DV)
    return pl.pallas_call(
        _hgrn_kernel,
        grid=(bsz, n_blocks),
        in_specs=in_specs + [const(m) for m in hm],
        out_specs=outs,
        out_shape=[jax.ShapeDtypeStruct((bsz, t, HG_VAL_W), F32)] * 2,
        scratch_shapes=[pltpu.VMEM((HG_HEADS // hpg, HG_DV, GROUP_KEYS), F32)] * 2,
        compiler_params=_params("parallel", "arbitrary"),
        name="hgrn_scan",
    )(*args, *hm)


def _mix_out_ab_kernel(*refs):
    (h_ref, gf_ref, gb_ref, hf_ref, hb_ref, gg_ref, hg_ref, gn_ref, hn_ref, w_ref, gt_ref) = refs[0:11]
    pre_in, o_ref, pre_out = refs[11:11 + N_FFN_PRE_IN], refs[11 + N_FFN_PRE_IN], refs[12 + N_FFN_PRE_IN:]
    feats = []
    for o, gate, g in ((gf_ref[...] + gb_ref[...], gg_ref[...], gn_ref[...]),
                       (hf_ref[...] + hb_ref[...], hg_ref[...], hn_ref[...])):
        for hd in range(o.shape[-1] // LANES):
            s = slice(hd * LANES, (hd + 1) * LANES)
            feats.append(_rms(o[:, s], g) * _silu(gate[:, s]))
    feat = jnp.concatenate(feats, axis=-1).astype(BF16)
    h_new = h_ref[...] + gt_ref[...] * _dot(feat, w_ref[...])
    o_ref[...] = h_new
    _ffn_pre_body(h_new, *pre_in, *pre_out)


def _mix_out_ab(h, o_gla, o_hg, proj, gla_norm_g, hg_norm_g, w_out, gate, n_ctx, pre):
    bsz, t, d = h.shape
    tm = ROW_TILE
    seg = lambda b, i: (b, (i * tm >= n_ctx).astype(jnp.int32), 0, 0)
    row = lambda width, idx: pl.BlockSpec((None, tm, width), lambda b, i: (b, i, idx))
    vec = pl.BlockSpec((1, LANES), lambda b, i: (0, 0))
    p_in, p_args, p_out, p_shape, p_scratch = _ffn_pre_parts(bsz, t, d, *pre, seg)
    outs = pl.pallas_call(
        _mix_out_ab_kernel,
        grid=(bsz, t // tm),
        in_specs=[row(d, 0), row(GLA_VAL_W, 0), row(GLA_VAL_W, 0), row(HG_VAL_W, 0), row(HG_VAL_W, 0),
                  row(GLA_VAL_W, AB_G // GLA_VAL_W), row(HG_VAL_W, AB_HG // HG_VAL_W), vec, vec,
                  pl.BlockSpec(w_out.shape, lambda b, i: (0, 0)),
                  pl.BlockSpec((None, None, 1, d), seg)] + p_in,
        out_specs=[row(d, 0)] + p_out,
        out_shape=[jax.ShapeDtypeStruct((bsz, t, d), F32)] + p_shape,
        scratch_shapes=p_scratch,
        compiler_params=_params("arbitrary", "arbitrary"),
        name="mix_out_ab",
    )(h, o_gla[0], o_gla[1], o_hg[0], o_hg[1], proj, proj, gla_norm_g.reshape(1, -1),
      hg_norm_g.reshape(1, -1), w_out, gate, *p_args)
    return outs[0], outs[1:]


HALO = 8


def _norm_proj_conv_kernel(h_ref, hp_ref, hn_ref, g_ref, sh_ref, sc_ref, w_ref, cw_ref, cb_ref,
                           hy_ref, xbc_ref, zdt_ref, *, ctx_tiles, n_tiles):
    i = pl.program_id(1)
    tm = h_ref.shape[0]
    hh = jnp.concatenate([hp_ref[...], h_ref[...], hn_ref[...]], axis=0)
    u = _rms(hh, g_ref[...]) * (1.0 + sc_ref[...]) + sh_ref[...]
    p = _dot(u.astype(BF16), w_ref[...])
    n_conv = cw_ref.shape[1]
    n_hy = hy_ref.shape[1]
    pc = p[:, 0:n_conv]
    rows_all = tm + 2 * HALO
    cur = pc[HALO:HALO + tm]
    prev = pltpu.roll(pc, 1, axis=0)[HALO:HALO + tm]
    nxt = pltpu.roll(pc, rows_all - 1, axis=0)[HALO:HALO + tm]
    first = jnp.logical_or(i == 0, i == ctx_tiles)
    last = jnp.logical_or(i == ctx_tiles - 1, i == n_tiles - 1)
    rows = lax.broadcasted_iota(jnp.int32, cur.shape, 0)
    prev = jnp.where(jnp.logical_and(first, rows == 0), 0.0, prev)
    nxt = jnp.where(jnp.logical_and(last, rows == tm - 1), 0.0, nxt)
    y = prev * cw_ref[0:1, :] + cur * cw_ref[1:2, :] + nxt * cw_ref[2:3, :] + cb_ref[...]
    hy_ref[...] = y[:, 0:n_hy]
    xbc_ref[...] = _silu(y[:, n_hy:n_conv])
    zdt_ref[...] = p[HALO:HALO + tm, n_conv:]


def _norm_proj_conv(h, g, shift, scale, w, conv_w, conv_b, n_hy, n_ctx):
    bsz, t, d = h.shape
    n = w.shape[1]
    n_conv = conv_w.shape[1]
    tm = ROW_TILE
    n_tiles, ctx_tiles = t // tm, n_ctx // tm
    r8 = tm // HALO
    last8 = t // HALO - 1
    seg = lambda b, i: (b, (i >= ctx_tiles).astype(jnp.int32), 0, 0)
    kern = functools.partial(_norm_proj_conv_kernel, ctx_tiles=ctx_tiles, n_tiles=n_tiles)
    return pl.pallas_call(
        kern,
        grid=(bsz, n_tiles),
        in_specs=[pl.BlockSpec((None, tm, d), lambda b, i: (b, i, 0)),
                  pl.BlockSpec((None, HALO, d), lambda b, i: (b, jnp.maximum(i * r8 - 1, 0), 0)),
                  pl.BlockSpec((None, HALO, d), lambda b, i: (b, jnp.minimum((i + 1) * r8, last8), 0)),
                  pl.BlockSpec((1, d), lambda b, i: (0, 0)),
                  pl.BlockSpec((None, None, 1, d), seg),
                  pl.BlockSpec((None, None, 1, d), seg),
                  pl.BlockSpec((d, n), lambda b, i: (0, 0)),
                  pl.BlockSpec((3, n_conv), lambda b, i: (0, 0)),
                  pl.BlockSpec((1, n_conv), lambda b, i: (0, 0))],
        out_specs=[pl.BlockSpec((None, tm, n_hy), lambda b, i: (b, jnp.maximum(i - ctx_tiles, 0), 0)),
                   pl.BlockSpec((None, tm, n_conv - n_hy), lambda b, i: (b, i, 0)),
                   pl.BlockSpec((None, tm, n - n_conv), lambda b, i: (b, i, 0))],
        out_shape=[jax.ShapeDtypeStruct((bsz, t - n_ctx, n_hy), F32),
                   jax.ShapeDtypeStruct((bsz, t, n_conv - n_hy), F32),
                   jax.ShapeDtypeStruct((bsz, t, n - n_conv), F32)],
        compiler_params=_params("parallel", "arbitrary"),
        name="norm_proj_conv",
    )(h, h, h, g.reshape(1, d), shift, scale, w, conv_w, conv_b)


def _ssd_kernel(*refs):
    ins, hexp_ref, o_refs, st_refs = (refs[0:7], refs[7:14]), refs[14], refs[15:17], refs[17:19]

    @pl.when(pl.program_id(1) == 0)
    def _():
        for st_ref in st_refs:
            st_ref[...] = jnp.zeros_like(st_ref)

    for (xbc_ref, dt_ref, bias_ref, alog_ref, mq_ref, mk_ref, mask_ref), o_ref, st_ref in zip(ins, o_refs, st_refs):
        _ssd_chunk(xbc_ref, dt_ref, bias_ref, alog_ref, hexp_ref, mq_ref, mk_ref, mask_ref, o_ref, st_ref)


def _ssd_chunk(xbc_ref, dt_ref, bias_ref, alog_ref, hexp_ref, mq_ref, mk_ref, mask_ref, o_ref, st_ref):
    c = xbc_ref.shape[0]
    hpg = MB_HEADS // MB_GROUPS
    gw = hpg * MB_HEAD_DIM
    dt = _softplus(dt_ref[...] + bias_ref[...])
    la = -dt * jnp.exp(alog_ref[...])
    cq = _sel_dot(mq_ref[...], la)
    ck = _sel_dot(mk_ref[...], la)
    cq_t = _dot_nt_sel(la, mq_ref[...])
    hexp = hexp_ref[...]
    dt_x = _dot_sel(dt, hexp)
    eq_x = jnp.exp(_dot_sel(cq, hexp))
    ek_x = jnp.exp(_dot_sel(ck, hexp))
    etot_x = jnp.exp(_dot_sel(jnp.sum(la, axis=0, keepdims=True), hexp))
    xs = xbc_ref[:, 0:MB_INNER] * dt_x
    mask = mask_ref[...]
    outs = []
    for g in range(MB_GROUPS):
        bm = xbc_ref[:, MB_INNER + g * MB_STATE:MB_INNER + (g + 1) * MB_STATE].astype(BF16)
        cm = xbc_ref[:, MB_INNER + MB_BC_W + g * MB_STATE:MB_INNER + MB_BC_W + (g + 1) * MB_STATE].astype(BF16)
        cb = _dot_nt(cm, bm)
        st = st_ref[g]
        gs = slice(g * gw, (g + 1) * gw)
        y_inter = _dot(cm, st.astype(BF16)) * eq_x[:, gs]
        for r in range(hpg):
            hd = g * hpg + r
            diff = cq[:, hd:hd + 1] - cq_t[hd:hd + 1, :]
            w = cb * jnp.exp(jnp.where(mask > 0.0, diff, -jnp.inf))
            ps = slice(hd * MB_HEAD_DIM, (hd + 1) * MB_HEAD_DIM)
            outs.append(_dot(w.astype(BF16), xs[:, ps].astype(BF16))
                        + y_inter[:, r * MB_HEAD_DIM:(r + 1) * MB_HEAD_DIM])
        st_ref[g] = st * etot_x[:, gs] + _dot_tn(bm, (xs[:, gs] * ek_x[:, gs]).astype(BF16))
    o_ref[...] = jnp.concatenate(outs, axis=-1)


def _dot_nt_sel(x, m01):
    hi, mid, lo = _split3(x)
    f = lambda p: lax.dot_general(p, m01, (((0,), (1,)), ((), ())), preferred_element_type=F32)
    return f(hi) + (f(mid) + f(lo))


def _ssd_scan(xbc, proj, dt_col, dt_bias, a_log, n_ctx):
    bsz, t, _ = xbc.shape
    c = SSD_CHUNK
    pad = lambda v: jnp.zeros((1, LANES), F32).at[0, :MB_HEADS].set(v)
    hexp = np.zeros((LANES, MB_INNER), np.float32)
    for hd in range(MB_HEADS):
        hexp[hd, hd * MB_HEAD_DIM:(hd + 1) * MB_HEAD_DIM] = 1.0
    hexp = jnp.asarray(hexp, BF16)
    tri = np.tril(np.ones((c, c), np.float32))
    in_specs, args, outs = [], [], []
    for d in range(2):
        n_chunks, col, const, sel, _ = _scan_specs(c, n_ctx, t, d == 1)
        mq, mk = sel[0:c, 0:c], sel[c:2 * c, 0:c]
        mask = jnp.asarray(tri[::-1, ::-1].copy() if d == 1 else tri)
        bias, alog = pad(dt_bias[d]), pad(a_log[d].astype(F32))
        in_specs += [col(xbc.shape[-1], 0), col(LANES, dt_col + d), const(bias), const(alog),
                     const(mq), const(mk), const(mask)]
        args += [xbc, proj, bias, alog, mq, mk, mask]
        outs.append(col(MB_INNER, 0))
    return pl.pallas_call(
        _ssd_kernel,
        grid=(bsz, n_chunks),
        in_specs=in_specs + [const(hexp)],
        out_specs=outs,
        out_shape=[jax.ShapeDtypeStruct((bsz, t, MB_INNER), F32)] * 2,
        scratch_shapes=[pltpu.VMEM((MB_GROUPS, MB_STATE, MB_INNER // MB_GROUPS), F32)] * 2,
        compiler_params=_params("parallel", "arbitrary"),
        name="ssd_scan",
    )(*args, hexp)


def _mix_out_cd_kernel(*refs):
    h_ref, hy_ref, yf_ref, yb_ref, xs_ref, z_ref, dsk_ref, ng_ref, w_ref, gt_ref = refs[0:10]
    pre_in, o_ref, pre_out = refs[10:10 + N_FFN_PRE_IN], refs[10 + N_FFN_PRE_IN], refs[11 + N_FFN_PRE_IN:]
    y = (yf_ref[...] + yb_ref[...] + dsk_ref[...] * xs_ref[...]) * _silu(z_ref[...])
    gw = MB_INNER // MB_GROUPS
    ys = [_rms(y[:, g * gw:(g + 1) * gw], ng_ref[:, g * gw:(g + 1) * gw]) for g in range(MB_GROUPS)]
    feat = jnp.concatenate([hy_ref[...]] + ys, axis=-1).astype(BF16)
    h_new = h_ref[...] + gt_ref[...] * _dot(feat, w_ref[...])
    o_ref[...] = h_new
    _ffn_pre_body(h_new, *pre_in, *pre_out)


def _mix_out_cd(h, hy, y_ssd, xbc, proj, z_col, d_skip_x, norm_g, w_out, gate, n_ctx, pre):
    bsz, t, d = h.shape
    tm = ROW_TILE
    n_lat = t - n_ctx
    off = n_ctx // tm
    row = lambda width, idx: pl.BlockSpec((None, tm, width), lambda b, i: (b, i + off, idx))
    vec = pl.BlockSpec((1, MB_INNER), lambda b, i: (0, 0))
    latent = lambda b, i: (b, 1, 0, 0)
    p_in, p_args, p_out, p_shape, p_scratch = _ffn_pre_parts(bsz, n_lat, d, *pre, latent)
    outs = pl.pallas_call(
        _mix_out_cd_kernel,
        grid=(bsz, n_lat // tm),
        in_specs=[row(d, 0), pl.BlockSpec((None, tm, HY_CH), lambda b, i: (b, i, 0)),
                  row(MB_INNER, 0), row(MB_INNER, 0), row(MB_INNER, 0), row(MB_INNER, z_col), vec, vec,
                  pl.BlockSpec(w_out.shape, lambda b, i: (0, 0)),
                  pl.BlockSpec((None, None, 1, d), latent)] + p_in,
        out_specs=[pl.BlockSpec((None, tm, d), lambda b, i: (b, i, 0))] + p_out,
        out_shape=[jax.ShapeDtypeStruct((bsz, n_lat, d), F32)] + p_shape,
        scratch_shapes=p_scratch,
        compiler_params=_params("arbitrary", "arbitrary"),
        name="mix_out_cd",
    )(h, hy, y_ssd[0], y_ssd[1], xbc, proj, d_skip_x, norm_g.reshape(1, -1), w_out, gate, *p_args)
    return outs[0], outs[1:]


def _top2_of4(a, b, c, d):
    hi1, lo1, hi2, lo2 = jnp.maximum(a, b), jnp.minimum(a, b), jnp.maximum(c, d), jnp.minimum(c, d)
    return jnp.maximum(hi1, hi2) + jnp.maximum(jnp.minimum(hi1, hi2), jnp.maximum(lo1, lo2))


def _first_argmax(vals, skip=None):
    idx = None
    for j, vj in enumerate(vals):
        if idx is None and skip is None:
            idx, best = jnp.zeros(vj.shape, jnp.int32), vj
            continue
        if idx is None:
            idx, best = jnp.full(vj.shape, -1, jnp.int32), jnp.full(vj.shape, -jnp.inf, F32)
        take = vj > best
        if skip is not None:
            take = jnp.logical_and(take, skip != j)
        idx = jnp.where(take, j, idx)
        best = jnp.where(take, vj, best)
    return idx, best


def _ffn_pre_body(h, g_ref, sh_ref, sc_ref, rw_ref, rb_ref, tri_ref, v_ref, ri_ref, rwt_ref, cnt_ref, carry_ref):
    @pl.when(jnp.logical_and(pl.program_id(0) == 0, pl.program_id(1) == 0))
    def _():
        carry_ref[...] = jnp.zeros_like(carry_ref)

    v = _rms(h, g_ref[...]) * (1.0 + sc_ref[...]) + sh_ref[...]
    v_ref[...] = _pack_bf16_pairs(v)
    st = _sigmoid(_dot_f32(v, rw_ref[...])).T[0:N_EXPERTS]
    sel = st + rb_ref[...]
    row = lambda a, e: a[e:e + 1]
    epg = EXPERTS_PER_GROUP
    gscore = [_top2_of4(*[row(sel, g * epg + j) for j in range(epg)]) for g in range(N_GROUPS)]
    best, _ = _first_argmax(gscore)

    def in_best(a, j):
        out = row(a, j)
        for g in range(1, N_GROUPS):
            out = jnp.where(best == g, row(a, g * epg + j), out)
        return out

    vals = [in_best(sel, j) for j in range(epg)]
    raw = [in_best(st, j) for j in range(epg)]
    i1, _ = _first_argmax(vals)
    i2, _ = _first_argmax(vals, skip=i1)
    pick = lambda i: functools.reduce(lambda acc, j: jnp.where(i == j, raw[j], acc), range(1, epg), raw[0])
    w1, w2 = pick(i1), pick(i2)
    wsum = w1 + w2
    e1, e2 = best * epg + i1, best * epg + i2

    experts = lax.broadcasted_iota(jnp.int32, st.shape, 0)
    oh1 = (experts == e1).astype(F32)
    oh2 = (experts == e2).astype(F32)
    cnt = oh1 + oh2
    before = _dot(cnt.astype(BF16), tri_ref[...]) + carry_ref[:, 0:1]
    ri_ref[0:1, :] = e1
    ri_ref[1:2, :] = e2
    ri_ref[2:3, :] = jnp.sum(oh1 * before, axis=0, keepdims=True).astype(jnp.int32)
    ri_ref[3:4, :] = jnp.sum(oh2 * before, axis=0, keepdims=True).astype(jnp.int32)
    ri_ref[4:8, :] = jnp.zeros((4, st.shape[1]), jnp.int32)
    rwt_ref[0:1, :] = w1 / wsum
    rwt_ref[1:2, :] = w2 / wsum
    rwt_ref[2:8, :] = jnp.zeros((6, st.shape[1]), F32)
    carry_ref[...] = carry_ref[...] + jnp.sum(cnt, axis=1, keepdims=True)
    cnt_ref[...] = carry_ref[...]


N_FFN_PRE_IN = 6


def _ffn_pre_parts(bsz, t, d, g, shift, scale, router_w_pad, router_b, seg):
    tm = ROW_TILE
    tri = jnp.asarray(np.triu(np.ones((tm, tm), np.float32), 1), BF16)
    in_specs = [pl.BlockSpec((1, d), lambda b, i: (0, 0)),
                pl.BlockSpec((None, None, 1, d), seg),
                pl.BlockSpec((None, None, 1, d), seg),
                pl.BlockSpec((d, LANES), lambda b, i: (0, 0)),
                pl.BlockSpec((N_EXPERTS, 1), lambda b, i: (0, 0)),
                pl.BlockSpec((tm, tm), lambda b, i: (0, 0))]
    args = (g.reshape(1, d), shift, scale, router_w_pad, router_b.reshape(N_EXPERTS, 1), tri)
    out_specs = [pl.BlockSpec((None, tm, d // 2), lambda b, i: (b, i, 0)),
                 pl.BlockSpec((None, 8, tm), lambda b, i: (b, 0, i)),
                 pl.BlockSpec((None, 8, tm), lambda b, i: (b, 0, i)),
                 pl.BlockSpec((N_EXPERTS, LANES), lambda b, i: (0, 0))]
    out_shape = [jax.ShapeDtypeStruct((bsz, t, d // 2), jnp.uint32),
                 jax.ShapeDtypeStruct((bsz, 8, t), jnp.int32),
                 jax.ShapeDtypeStruct((bsz, 8, t), F32),
                 jax.ShapeDtypeStruct((N_EXPERTS, LANES), F32)]
    return in_specs, args, out_specs, out_shape, [pltpu.VMEM((N_EXPERTS, LANES), F32)]


def _experts_kernel(be_ref, nb_ref, x_ref, wg_ref, wu_ref, wd_ref, o_ref, wg_s, wu_s, wd_s):
    i = pl.program_id(0)
    prev = be_ref[jnp.maximum(i - 1, 0)]
    changed = jnp.logical_or(i == 0, be_ref[i] != prev)

    @pl.when(changed)
    def _():
        wg_s[...] = wg_ref[...].astype(BF16)
        wu_s[...] = wu_ref[...].astype(BF16)
        wd_s[...] = wd_ref[...].astype(BF16)

    @pl.when(i < nb_ref[0])
    def _():
        x = _unpack_bf16_pairs(x_ref[...]).astype(BF16)
        hid = _silu(_dot(x, wg_s[...])) * _dot(x, wu_s[...])
        o_ref[...] = _pack_bf16_pairs(_dot(hid.astype(BF16), wd_s[...]))

    @pl.when(i >= nb_ref[0])
    def _():
        o_ref[...] = jnp.zeros_like(o_ref)


def _experts(xb, block_e, n_used, layer, w_gate, w_up, w_down):
    n_slots = xb.shape[0]
    n_blocks = n_slots // MOE_BLOCK
    d, de = w_gate.shape[-2:]
    wspec = lambda shape: pl.BlockSpec((None, None) + shape, lambda i, be, nb: (layer, be[i], 0, 0))
    return pl.pallas_call(
        _experts_kernel,
        grid_spec=pltpu.PrefetchScalarGridSpec(
            num_scalar_prefetch=2,
            grid=(n_blocks,),
            in_specs=[pl.BlockSpec((MOE_BLOCK, d // 2), lambda i, be, nb: (i, 0)),
                      wspec((d, de)), wspec((d, de)), wspec((de, d))],
            out_specs=pl.BlockSpec((MOE_BLOCK, d // 2), lambda i, be, nb: (i, 0)),
            scratch_shapes=[pltpu.VMEM((d, de), BF16), pltpu.VMEM((d, de), BF16), pltpu.VMEM((de, d), BF16)]),
        out_shape=jax.ShapeDtypeStruct((n_slots, d // 2), jnp.uint32),
        compiler_params=_params("arbitrary"),
        name="moe_experts",
    )(block_e, n_used, xb, w_gate, w_up, w_down)


def _ffn_post_kernel(h_ref, y0_ref, y1_ref, w_ref, gt_ref, g_ref, o_ref, *, final):
    w = w_ref[...]
    y = w[:, 0:1] * _unpack_bf16_pairs(y0_ref[...]) + w[:, 1:2] * _unpack_bf16_pairs(y1_ref[...])
    out = h_ref[...] + gt_ref[...] * y
    o_ref[...] = _rms(out, g_ref[...]) if final else out


def _ffn_post(h, y, w, gate, n_ctx, final_g=None):
    bsz, t, d = h.shape
    tm = ROW_TILE
    seg = lambda b, i: (b, (i * tm >= n_ctx).astype(jnp.int32), 0, 0)
    row = lambda width: pl.BlockSpec((None, tm, width), lambda b, i: (b, i, 0))
    choice = lambda kk: pl.BlockSpec((None, None, tm, d // 2), lambda b, i: (kk, b, i, 0))
    final = final_g is not None
    g = final_g if final else jnp.ones((d,), F32)
    return pl.pallas_call(
        functools.partial(_ffn_post_kernel, final=final),
        grid=(bsz, t // tm),
        in_specs=[row(d), choice(0), choice(1), row(LANES), pl.BlockSpec((None, None, 1, d), seg),
                  pl.BlockSpec((1, d), lambda b, i: (0, 0))],
        out_specs=row(d),
        out_shape=jax.ShapeDtypeStruct((bsz, t, d), F32),
        compiler_params=_params("parallel", "parallel"),
        name="ffn_post",
    )(h, y, y, w, gate, g.reshape(1, d))


def _slot_layout(n, ri, counts):
    e = jnp.swapaxes(ri[:, 0:2], 0, 1).reshape(TOP_K, n)
    rank = jnp.swapaxes(ri[:, 2:4], 0, 1).reshape(TOP_K, n)
    padded = (counts + MOE_BLOCK - 1) // MOE_BLOCK * MOE_BLOCK
    pend = jnp.cumsum(padded)
    pstart = pend - padded
    experts = jnp.arange(N_EXPERTS, dtype=jnp.int32)
    dest = rank + jnp.sum(jnp.where(e[..., None] == experts, pstart, 0), axis=-1)
    n_slots = (n * TOP_K + MOE_BLOCK - 1) // MOE_BLOCK * MOE_BLOCK + N_EXPERTS * MOE_BLOCK
    n_blocks = n_slots // MOE_BLOCK
    blk0 = jnp.arange(n_blocks, dtype=jnp.int32)[:, None] * MOE_BLOCK
    block_e = jnp.minimum(jnp.sum((pend[None, :] <= blk0).astype(jnp.int32), axis=-1), N_EXPERTS - 1)
    n_used = (pend[-1] // MOE_BLOCK).astype(jnp.int32).reshape(1)
    return dest, n_slots, block_e.astype(jnp.int32), n_used


SC_CORES, SC_SUBCORES = 2, 16
SC_WINDOW = 32


def _gather_rows(table, idx):
    n_rows, d = idx.shape[0], table.shape[1]
    workers = SC_CORES * SC_SUBCORES
    per_worker = n_rows // workers
    assert per_worker * workers == n_rows and per_worker % SC_WINDOW == 0
    mesh = plsc.VectorSubcoreMesh(core_axis_name="c", subcore_axis_name="s")

    @functools.partial(
        pl.kernel, mesh=mesh,
        out_type=jax.ShapeDtypeStruct((n_rows, d), table.dtype),
        scratch_types=[pltpu.VMEM((SC_WINDOW,), jnp.int32), pltpu.VMEM((SC_WINDOW,), jnp.int32),
                       pltpu.VMEM((SC_WINDOW, d), table.dtype), pltpu.VMEM((SC_WINDOW, d), table.dtype),
                       pltpu.SemaphoreType.DMA, pltpu.SemaphoreType.DMA],
    )
    def gather_kernel(table_hbm, idx_hbm, out_hbm, idx0, idx1, rows0, rows1, sem0, sem1):
        base = (lax.axis_index("s") * SC_CORES + lax.axis_index("c")) * per_worker
        n_win = per_worker // SC_WINDOW
        slots = ((idx0, rows0, sem0), (idx1, rows1, sem1))
        window = lambda j: pl.ds(pl.multiple_of(base + j * SC_WINDOW, 8), SC_WINDOW)

        def start(j, slot):
            idx_v, rows_v, sem = slots[slot]
            pltpu.sync_copy(idx_hbm.at[window(j)], idx_v)
            pltpu.async_copy(table_hbm.at[idx_v], rows_v, sem)

        def finish(j, slot):
            idx_v, rows_v, sem = slots[slot]
            pltpu.make_async_copy(table_hbm.at[idx_v], rows_v, sem).wait()
            pltpu.sync_copy(rows_v, out_hbm.at[window(j)])

        start(0, 0)

        @pl.loop(0, n_win, step=2)
        def _(j):
            @pl.when(j + 1 < n_win)
            def _():
                start(j + 1, 1)

            finish(j, 0)

            @pl.when(j + 2 < n_win)
            def _():
                start(j + 2, 0)

            @pl.when(j + 1 < n_win)
            def _():
                finish(j + 1, 1)

    return gather_kernel(table, idx)


SC_SCATTER_WINDOW = 16


def _scatter_rows(src, dest, n_slots):
    n, d = src.shape
    workers = SC_CORES * SC_SUBCORES
    per_worker = n // workers
    win = SC_SCATTER_WINDOW
    assert per_worker * workers == n and per_worker % win == 0 and dest.shape == (TOP_K, n)
    mesh = plsc.VectorSubcoreMesh(core_axis_name="c", subcore_axis_name="s")

    @functools.partial(
        pl.kernel, mesh=mesh,
        out_type=jax.ShapeDtypeStruct((n_slots, d), src.dtype),
        scratch_types=[pltpu.VMEM((win,), jnp.int32), pltpu.VMEM((win,), jnp.int32),
                       pltpu.VMEM((win, d), src.dtype), pltpu.SemaphoreType.DMA, pltpu.SemaphoreType.DMA],
    )
    def scatter_kernel(src_hbm, dest_hbm, out_hbm, idx0, idx1, rows_v, sem0, sem1):
        base = (lax.axis_index("s") * SC_CORES + lax.axis_index("c")) * per_worker

        @pl.loop(0, per_worker // win)
        def _(j):
            rows = pl.ds(pl.multiple_of(base + j * win, 8), win)
            pltpu.sync_copy(src_hbm.at[rows], rows_v)
            pltpu.sync_copy(dest_hbm.at[0, rows], idx0)
            pltpu.sync_copy(dest_hbm.at[1, rows], idx1)
            first = pltpu.async_copy(rows_v, out_hbm.at[idx0], sem0)
            second = pltpu.async_copy(rows_v, out_hbm.at[idx1], sem1)
            first.wait()
            second.wait()

    return scatter_kernel(src, dest)


def _alongside(gather, idx, side_fn, side_in):
    idx, side_in = lax.optimization_barrier((idx, side_in))
    return lax.optimization_barrier((gather(idx), side_fn(side_in)))


def _moe(h, pre_out, gate, layer, w_gate, w_up, w_down, n_ctx, final_g=None, side=None):
    bsz, t, d = h.shape
    n = bsz * t
    v, ri, rwt, counts = pre_out
    dest, n_slots, block_e, n_used = _slot_layout(n, ri, counts[:, 0].astype(jnp.int32))
    w = jnp.swapaxes(rwt[:, 0:2], 1, 2).reshape(n, TOP_K)
    dispatch = lambda idx: _scatter_rows(v.reshape(n, d // 2), idx, n_slots)
    if side is None:
        xb = dispatch(dest)
    else:
        xb, side_a = _alongside(dispatch, dest, *side[0])
    yb = _experts(xb, block_e, n_used, layer, w_gate, w_up, w_down)
    combine = lambda idx: _gather_rows(yb, idx)
    dest_flat = dest.reshape(-1)
    if side is None:
        y, side_b = combine(dest_flat), None
    else:
        y, side_b = _alongside(combine, dest_flat, side[1], side_a)
    wpad = jnp.zeros((n, LANES), F32).at[:, :TOP_K].set(w).reshape(bsz, t, LANES)
    out = _ffn_post(h, y.reshape(TOP_K, bsz, t, d // 2), wpad, gate, n_ctx, final_g)
    return out if side is None else (out, side_b)


DFT_STEP = 16


def _dft_tables(n):
    r, *mats = _dft_tables_np(n)
    return (r,) + tuple(jnp.asarray(a).astype(BF16) for a in mats)


@functools.lru_cache(maxsize=None)
def _dft_tables_np(n):
    size = 2 * n
    r = int(round(math.sqrt(size)))
    assert r * r == size and r % DFT_STEP == 0
    p1 = np.arange(r // 2)[None, None, :]
    p2 = np.arange(r)[:, None, None]
    k1 = np.arange(r)[None, :, None]
    ang = 2.0 * np.pi * (((r * p1 + p2) * k1) % size) / size
    g_re, g_im = np.cos(ang), -np.sin(ang)
    g_in = np.concatenate([g_re, g_im], axis=1)
    g_out = np.concatenate([np.swapaxes(g_re, 1, 2), np.swapaxes(g_im, 1, 2)], axis=2) / size
    a2 = 2.0 * np.pi * ((np.arange(r)[:, None] * np.arange(r)[None, :]) % r) / r
    f_re, f_im = np.cos(a2), -np.sin(a2)
    f_fwd = np.block([[f_re, -f_im], [f_im, f_re]])
    f_inv = np.block([[f_re, f_im], [-f_im, f_re]])
    p1f = np.arange(r)[None, None, :]
    angf = 2.0 * np.pi * (((r * p1f + p2) * k1) % size) / size
    g_full = np.concatenate([np.cos(angf), -np.sin(angf)], axis=1)
    return (r,) + tuple(a.astype(np.float32) for a in (g_in, g_out, f_fwd, f_inv, g_full))


def _dot_f32_tn(a, b):
    ah = a.astype(BF16)
    al = (a - ah.astype(F32)).astype(BF16)
    bh = b.astype(BF16)
    bl = (b - bh.astype(F32)).astype(BF16)
    return _dot_tn(ah, bh) + (_dot_tn(ah, bl) + _dot_tn(al, bh))


def _hy_filter_kernel(z_ref, t_ref, w1_ref, b1_ref, w2_ref, b2_ref, fr_ref, w3_ref, w3b_ref, rates_ref, o_ref,
                      *, half_tiles):
    i = pl.program_id(0)
    hid = jnp.sin(fr_ref[...] * (_dot_f32(w1_ref[...], z_ref[...]) + b1_ref[...]))
    hid = jnp.sin(fr_ref[...] * (_dot_f32(w2_ref[...], hid) + b2_ref[...]))
    filt = _dot_f32_tn(hid, w3_ref[...])
    decay = jnp.exp(-t_ref[...] * rates_ref[...])
    for o in range(o_ref.shape[0]):
        o_ref[o] = filt[:, o * HY_CH:(o + 1) * HY_CH] * decay

    @pl.when(i == 0)
    def _():
        extra = _dot_f32_tn(hid[:, 0:LANES], w3b_ref[...])[0:8]
        first = lax.broadcasted_iota(jnp.int32, (8, HY_CH), 0) == 0
        for o in range(o_ref.shape[0]):
            add = extra[:, o * HY_CH:(o + 1) * HY_CH] * decay[0:8]
            o_ref[o, 0:8, :] = o_ref[o, 0:8, :] + jnp.where(first, add, 0.0)

    @pl.when(i == half_tiles)
    def _():
        for o in range(o_ref.shape[0]):
            o_ref[o, 0:1, :] = jnp.zeros((1, HY_CH), F32)


HY_TILE = 512


def _hy_kernels(n, w1, b1, w2, b2, w3, freq):
    pos = np.arange(2 * n)
    pos = np.where(pos < n, pos, 2 * n - pos).astype(np.float32)
    t = jnp.asarray(pos / np.float32(n - 1))
    bands = jnp.linspace(1e-4, HY_BANDS - 1, HY_BANDS, dtype=F32)
    ang = (2.0 * math.pi / n) * bands[:, None] * jnp.asarray(pos)[None, :]
    z = jnp.concatenate([t[None, :], jnp.cos(ang), -jnp.sin(ang)], axis=0)
    z = jnp.pad(z, ((0, LANES - z.shape[0]), (0, 0)))
    w1t = jnp.pad(w1, ((0, LANES - w1.shape[0]), (0, 0))).T
    hidden = w1.shape[1]
    col = lambda v: v.reshape(hidden, 1)
    w3d = jnp.swapaxes(w3.reshape(hidden, HY_ORDER, 2, HY_CH), 0, 2)
    w3d = jnp.swapaxes(w3d, 1, 2).reshape(2, hidden, HY_ORDER * HY_CH)
    rates = jnp.abs(jnp.linspace(HY_MIN_DECAY, HY_MAX_DECAY, HY_CH, dtype=F32)).reshape(1, HY_CH)
    tm = HY_TILE
    half_tiles = n // tm
    full = lambda a: pl.BlockSpec(a.shape, lambda i: (0,) * a.ndim)
    small = (w1t, col(b1), w2.T, col(b2), col(freq))
    return pl.pallas_call(
        functools.partial(_hy_filter_kernel, half_tiles=half_tiles),
        grid=(2 * n // tm,),
        in_specs=[pl.BlockSpec((LANES, tm), lambda i: (0, i)), pl.BlockSpec((tm, 1), lambda i: (i, 0))]
                 + [full(a) for a in small]
                 + [pl.BlockSpec((None, hidden, HY_ORDER * HY_CH), lambda i: ((i >= half_tiles).astype(jnp.int32), 0, 0)),
                    pl.BlockSpec((None, hidden, HY_ORDER * HY_CH), lambda i: (1, 0, 0)), full(rates)],
        out_specs=pl.BlockSpec((HY_ORDER, tm, HY_CH), lambda i: (0, i, 0)),
        out_shape=jax.ShapeDtypeStruct((HY_ORDER, 2 * n, HY_CH), F32),
        compiler_params=_params("parallel"),
        name="hy_kernels",
    )(z, t[:, None], *small, w3d, w3d, rates)


def _pack_complex(z):
    r = z.shape[0] // 2
    bits = lax.bitcast_convert_type(z.astype(BF16).astype(F32), jnp.uint32)
    return lax.bitcast_convert_type(bits[0:r] | (bits[r:2 * r] >> 16), F32)


def _unpack_complex(words):
    p = lax.bitcast_convert_type(words, jnp.uint32)
    re = lax.bitcast_convert_type(p & jnp.uint32(0xFFFF0000), F32)
    im = lax.bitcast_convert_type(p << 16, F32)
    return jnp.concatenate([re, im], axis=0).astype(BF16)


def _load_every(ref, j, count):
    return ref.reshape(count * DFT_STEP, LANES)[pl.ds(j, count, stride=DFT_STEP), :]


def _store_every(ref, j, count, val):
    ref.reshape(count * DFT_STEP, LANES)[pl.ds(j, count, stride=DFT_STEP), :] = val


def _dft_in_kernel(x_ref, g_ref, a_ref):
    rh = x_ref.shape[0]
    for j in range(DFT_STEP):
        a_ref[j] = _pack_complex(_dot(g_ref[j], _load_every(x_ref, j, rh).astype(BF16)))


def _dft_in(x4, col, g_in):
    bx, rh, r, _ = x4.shape
    c = HY_CH
    cbs = c // LANES
    return pl.pallas_call(
        _dft_in_kernel,
        grid=(bx, r // DFT_STEP, cbs),
        in_specs=[pl.BlockSpec((None, rh, DFT_STEP, LANES), lambda b, i, cb: (b, 0, i, col * cbs + cb)),
                  pl.BlockSpec((DFT_STEP, 2 * r, rh), lambda b, i, cb: (i, 0, 0))],
        out_specs=pl.BlockSpec((None, DFT_STEP, r, LANES), lambda b, i, cb: (b, i, 0, cb)),
        out_shape=jax.ShapeDtypeStruct((bx, r, r, c), F32),
        compiler_params=_params("parallel", "parallel", "parallel"),
        name="dft_in",
    )(x4, g_in)


def _stage2_operand(a_ref, j, r):
    return jnp.concatenate([_unpack_complex(_load_every(a_ref.at[b], j, r)) for b in range(a_ref.shape[0])],
                           axis=1)


def _dft_filt_kernel(a_ref, f_ref, k_ref):
    r = f_ref.shape[0] // 2
    for j in range(DFT_STEP):
        s = _dot(f_ref[...], _stage2_operand(a_ref, j, r))
        for o in range(a_ref.shape[0]):
            k_ref[o, j] = s[:, o * LANES:(o + 1) * LANES]


def _dft_filt(a, f_fwd):
    nq, r, _, c = a.shape
    return pl.pallas_call(
        _dft_filt_kernel,
        grid=(r // DFT_STEP, c // LANES),
        in_specs=[pl.BlockSpec((nq, r, DFT_STEP, LANES), lambda i, cb: (0, 0, i, cb)),
                  pl.BlockSpec(f_fwd.shape, lambda i, cb: (0, 0))],
        out_specs=pl.BlockSpec((nq, DFT_STEP, 2 * r, LANES), lambda i, cb: (0, i, 0, cb)),
        out_shape=jax.ShapeDtypeStruct((nq, r, 2 * r, c), F32),
        compiler_params=_params("parallel", "parallel"),
        name="dft_filt",
    )(a, f_fwd)


def _dft_mid_kernel(a_ref, k_ref, ff_ref, fi_ref, b_ref):
    r = ff_ref.shape[0] // 2
    n_seq = a_ref.shape[0]
    for j in range(DFT_STEP):
        s = _dot(ff_ref[...], _stage2_operand(a_ref, j, r))
        sr, si = s[0:r], s[r:2 * r]
        kr = jnp.concatenate([k_ref[j, 0:r, :]] * n_seq, axis=1)
        ki = jnp.concatenate([k_ref[j, r:2 * r, :]] * n_seq, axis=1)
        p = jnp.concatenate([sr * kr - si * ki, sr * ki + si * kr], axis=0).astype(BF16)
        back = _dot(fi_ref[...], p)
        for b in range(n_seq):
            b_ref[b, j] = _pack_complex(back[:, b * LANES:(b + 1) * LANES])


def _dft_mid(a, kspec, order, f_fwd, f_inv):
    bsz, r, _, c = a.shape
    return pl.pallas_call(
        _dft_mid_kernel,
        grid=(r // DFT_STEP, c // LANES),
        in_specs=[pl.BlockSpec((bsz, r, DFT_STEP, LANES), lambda i, cb: (0, 0, i, cb)),
                  pl.BlockSpec((None, DFT_STEP, 2 * r, LANES), lambda i, cb: (order, i, 0, cb)),
                  pl.BlockSpec(f_fwd.shape, lambda i, cb: (0, 0)),
                  pl.BlockSpec(f_inv.shape, lambda i, cb: (0, 0))],
        out_specs=pl.BlockSpec((bsz, DFT_STEP, r, LANES), lambda i, cb: (0, i, 0, cb)),
        out_shape=jax.ShapeDtypeStruct((bsz, r, r, c), F32),
        compiler_params=_params("parallel", "parallel"),
        name="dft_mid",
    )(a, kspec, f_fwd, f_inv)


def _dft_out_kernel(b_ref, g_ref, u_ref, x_ref, bias_ref, o_ref):
    r, rh = b_ref.shape[0], o_ref.shape[0]
    for j in range(DFT_STEP):
        y = _dot(g_ref[j], _unpack_complex(_load_every(b_ref, j, r)))
        _store_every(o_ref, j, rh, _load_every(x_ref, j, rh) * (y + _load_every(u_ref, j, rh) * bias_ref[...]))


def _dft_out(bm, g_out, u4, u_col, x4, x_col, bias):
    bsz, r, _, c = bm.shape
    rh = r // 2
    cbs = c // LANES
    seq = lambda col: pl.BlockSpec((None, rh, DFT_STEP, LANES), lambda b, i, cb: (b, 0, i, col * cbs + cb))
    return pl.pallas_call(
        _dft_out_kernel,
        grid=(bsz, r // DFT_STEP, cbs),
        in_specs=[pl.BlockSpec((None, r, DFT_STEP, LANES), lambda b, i, cb: (b, 0, i, cb)),
                  pl.BlockSpec((DFT_STEP, rh, 2 * r), lambda b, i, cb: (i, 0, 0)),
                  seq(u_col), seq(x_col), pl.BlockSpec((1, LANES), lambda b, i, cb: (0, cb))],
        out_specs=seq(0),
        out_shape=jax.ShapeDtypeStruct((bsz, rh, r, c), F32),
        compiler_params=_params("parallel", "parallel", "parallel"),
        name="dft_out",
    )(bm, g_out, u4, x4, bias.reshape(1, c))


def _hyena_filter_stage1(n, filter_params):
    r, g_full = _dft_tables(n)[0], _dft_tables(n)[5]
    kern = _hy_kernels(n, *filter_params)
    return _dft_in(kern.reshape(-1, r, r, HY_CH), 0, g_full)


def _hyena_filter_spectra(n, stage1):
    return _dft_filt(stage1, _dft_tables(n)[3])


def _hyena(hy_in, kspec, conv_bias):
    bsz, n, _ = hy_in.shape
    r, g_in, g_out, f_fwd, f_inv, _ = _dft_tables(n)
    seq4 = hy_in.reshape(bsz, r // 2, r, 3 * HY_CH)
    zz = _dft_out(_dft_mid(_dft_in(seq4, 0, g_in), kspec, 0, f_fwd, f_inv), g_out,
                  seq4, 0, seq4, 1, conv_bias[0])
    out = _dft_out(_dft_mid(_dft_in(zz, 0, g_in), kspec, 1, f_fwd, f_inv), g_out,
                   zz, 0, seq4, 2, conv_bias[1])
    return out.reshape(bsz, n, HY_CH)


def _reorder_ab(w):
    gq, gk, gv, gg, lr_f, lr_b, hq, hf_f, hf_b, hi, hg = jnp.split(
        w, np.cumsum([256, 256, 512, 512, 16, 16, 512, 512, 512, 512, 512])[:-1].tolist(), axis=-1)
    pad = jnp.zeros((w.shape[0], AB_PAD_COLS - AB_LR - 2 * GLA_LOW_RANK), w.dtype)
    return jnp.concatenate([gq, gk, gv, gg, hq, hf_f, hf_b, hi, hg, lr_f, lr_b, pad], axis=-1)


def _reorder_cd(w):
    hy, z, xbc, dt_f, dt_b = jnp.split(w, np.cumsum([1536, 512, 1024, 8, 8])[:-1].tolist(), axis=-1)
    pad = jnp.zeros((w.shape[0], LANES - MB_HEADS), w.dtype)
    return jnp.concatenate([hy, xbc, z, dt_f, pad, dt_b, pad], axis=-1)


def kernel(x, c, ctx, c_ctx, ada_w, ada_b, norm_mix_g, norm_ffn_g, norm_out_g, ab_w_in, ab_w_out, gla_gate_w, gla_gate_b, gla_norm_g, hg_lb, hg_norm_g, cd_w_in, cd_w_out, hy_short_w, hy_short_b, hy_w1, hy_b1, hy_w2, hy_b2, hy_w3, hy_freq, hy_bias, mb_conv_w, mb_conv_b, mb_dt_bias, mb_a_log, mb_d, mb_norm_g, router_w, router_b, moe_w_gate, moe_w_up, moe_w_down):
    bsz, n_lat, d = x.shape
    n_ctx = ctx.shape[1]
    t = n_ctx + n_lat
    assert ada_w.shape[0] == 2 and ab_w_in.shape[0] == 1 and cd_w_in.shape[0] == 1

    cond = jnp.zeros((8, d), F32).at[:bsz].set(c).at[bsz].set(c_ctx)
    m = _adaln(cond, ada_w, ada_b)

    def mods(layer):
        lat = m[layer, :bsz].reshape(bsz, 6, d)
        cx = jnp.broadcast_to(m[layer, bsz].reshape(1, 6, d), (bsz, 6, d))
        both = jnp.stack([cx, lat], axis=1)
        return [both[:, :, j][:, :, None, :] for j in range(6)]

    lb_all = jnp.cumsum(jax.nn.softmax(hg_lb.astype(F32), axis=1), axis=1)
    router_w_pad = jnp.zeros((d, LANES), F32).at[:, :N_EXPERTS].set(router_w)
    h = jnp.concatenate([ctx, x], axis=1)

    sh_m, sc_m, gt_m, sh_f, sc_f, gt_f = mods(0)
    proj = _norm_proj(h, norm_mix_g[0], sh_m, sc_m, _reorder_ab(ab_w_in[0]).astype(BF16), n_ctx)
    gwp = [jnp.zeros((LANES, GLA_KEY_W), F32).at[GLA_LOW_RANK * dd:GLA_LOW_RANK * (dd + 1)].set(gla_gate_w[0, dd])
           for dd in range(2)]
    o_gla = _gla_scan(proj, gwp, [gla_gate_b[0, dd].reshape(1, -1) for dd in range(2)], n_ctx)
    o_hg = _hgrn_scan(proj, [lb_all[dd, 0].reshape(1, -1) for dd in range(2)], n_ctx)
    h, pre_out = _mix_out_ab(h, o_gla, o_hg, proj, gla_norm_g[0], hg_norm_g[0], ab_w_out[0].astype(BF16), gt_m,
                             n_ctx, (norm_ffn_g[0], sh_f, sc_f, router_w_pad, router_b))
    filter_params = (hy_w1[0], hy_b1[0], hy_w2[0], hy_b2[0], hy_w3[0], hy_freq[0])
    side = ((functools.partial(_hyena_filter_stage1, n_lat), filter_params),
            functools.partial(_hyena_filter_spectra, n_lat))
    h, kspec = _moe(h, pre_out, gt_f, 0, moe_w_gate, moe_w_up, moe_w_down, n_ctx, side=side)

    sh_m, sc_m, gt_m, sh_f, sc_f, gt_f = mods(1)
    conv_w = jnp.concatenate([hy_short_w[0], mb_conv_w[0]], axis=0).T
    conv_b = jnp.concatenate([hy_short_b[0], mb_conv_b[0]]).reshape(1, -1)
    hy_in, xbc, zdt = _norm_proj_conv(h, norm_mix_g[1], sh_m, sc_m, _reorder_cd(cd_w_in[0]).astype(BF16),
                                      conv_w, conv_b, 3 * HY_CH, n_ctx)
    hy = _hyena(hy_in, kspec, hy_bias[0])
    y_ssd = _ssd_scan(xbc, zdt, MB_INNER // LANES, mb_dt_bias[0], mb_a_log[0], n_ctx)
    d_skip_x = jnp.repeat(mb_d[0], MB_HEAD_DIM).reshape(1, MB_INNER)
    h, pre_out = _mix_out_cd(h, hy, y_ssd, xbc, zdt, 0, d_skip_x, mb_norm_g[0], cd_w_out[0].astype(BF16), gt_m,
                             n_ctx, (norm_ffn_g[1], sh_f, sc_f, router_w_pad, router_b))
    return _moe(h, pre_out, gt_f, 1, moe_w_gate, moe_w_up, moe_w_down, 0, final_g=norm_out_g)
```

```python
import functools
import math

import numpy as np
import jax
import jax.numpy as jnp
from jax import lax
from jax.experimental import pallas as pl
from jax.experimental.pallas import tpu as pltpu
from jax.experimental.pallas import tpu_sc as plsc

NORM_EPS = 1e-6
GLA_HEADS, GLA_DK, GLA_DV, GLA_LOW_RANK, GLA_TAU = 4, 64, 128, 16, 16.0
GLA_KEY_W, GLA_VAL_W = GLA_HEADS * GLA_DK, GLA_HEADS * GLA_DV
HG_HEADS, HG_EXPAND, HG_DV = 4, 128, 128
HG_KEY_W, HG_VAL_W = HG_HEADS * HG_EXPAND, HG_HEADS * HG_DV
HY_CH, HY_ORDER, HY_SHORT, HY_BANDS, HY_FILT_HID = 512, 2, 3, 16, 64
HY_MIN_DECAY = math.log(1e-2) / 1.5
HY_MAX_DECAY = math.log(1e-2) / 0.3
MB_HEADS, MB_HEAD_DIM, MB_GROUPS, MB_STATE = 8, 64, 2, 128
MB_INNER = MB_HEADS * MB_HEAD_DIM
MB_BC_W = MB_GROUPS * MB_STATE
N_EXPERTS, N_GROUPS, TOP_K, MOE_BLOCK = 16, 4, 2, 256
EXPERTS_PER_GROUP = N_EXPERTS // N_GROUPS

LANES = 128
SCAN_CHUNK = 64
SCAN_BLOCK = 256
SSD_CHUNK = 128
ROW_TILE = 256
VMEM_LIMIT = 56 * 1024 * 1024

BF16 = jnp.bfloat16
F32 = jnp.float32


def _params(*sem):
    return pltpu.CompilerParams(dimension_semantics=sem, vmem_limit_bytes=VMEM_LIMIT)


def _split3(x):
    hi = x.astype(BF16)
    r1 = x - hi.astype(F32)
    mid = r1.astype(BF16)
    lo = (r1 - mid.astype(F32)).astype(BF16)
    return hi, mid, lo


def _dot(a, b):
    return jnp.dot(a, b, preferred_element_type=F32)


def _dot_nt(a, b):
    return lax.dot_general(a, b, (((1,), (1,)), ((), ())), preferred_element_type=F32)


def _dot_tn(a, b):
    return lax.dot_general(a, b, (((0,), (0,)), ((), ())), preferred_element_type=F32)


def _sel_dot(m01, x):
    hi, mid, lo = _split3(x)
    return _dot(m01, hi) + (_dot(m01, mid) + _dot(m01, lo))


def _dot_sel(x, m01):
    hi, mid, lo = _split3(x)
    return _dot(hi, m01) + (_dot(mid, m01) + _dot(lo, m01))


def _dot_f32(a, b):
    ah = a.astype(BF16)
    al = (a - ah.astype(F32)).astype(BF16)
    bh = b.astype(BF16)
    bl = (b - bh.astype(F32)).astype(BF16)
    return _dot(ah, bh) + (_dot(ah, bl) + _dot(al, bh))


def _silu(x):
    return x * (1.0 / (1.0 + jnp.exp(-x)))


def _sigmoid(x):
    return 1.0 / (1.0 + jnp.exp(-x))


def _softplus(x):
    return jnp.maximum(x, 0.0) + jnp.log(1.0 + jnp.exp(-jnp.abs(x)))


def _pack_bf16_pairs(x):
    bits = lax.bitcast_convert_type(x.astype(BF16).astype(F32), jnp.uint32)
    half = x.shape[1] // 2
    return bits[:, :half] | (bits[:, half:] >> 16)


def _unpack_bf16_pairs(p):
    hi = lax.bitcast_convert_type(p & jnp.uint32(0xFFFF0000), F32)
    lo = lax.bitcast_convert_type(p << 16, F32)
    return jnp.concatenate([hi, lo], axis=1)


def _rms(x, g):
    return x * lax.rsqrt(jnp.mean(x * x, axis=-1, keepdims=True) + NORM_EPS) * g


def _adaln_kernel(c_ref, w_ref, b_ref, o_ref):
    o_ref[...] = _dot_f32(_silu(c_ref[...]), w_ref[...]) + b_ref[...]


def _adaln(cond, w, b):
    n_l, d, n6 = w.shape
    tn = 1536
    return pl.pallas_call(
        _adaln_kernel,
        grid=(n_l, n6 // tn),
        in_specs=[pl.BlockSpec((8, d), lambda l, j: (0, 0)),
                  pl.BlockSpec((None, d, tn), lambda l, j: (l, 0, j)),
                  pl.BlockSpec((None, 1, tn), lambda l, j: (l, 0, j))],
        out_specs=pl.BlockSpec((None, 8, tn), lambda l, j: (l, 0, j)),
        out_shape=jax.ShapeDtypeStruct((n_l, 8, n6), F32),
        compiler_params=_params("parallel", "parallel"),
        name="adaln",
    )(cond, w, b.reshape(n_l, 1, n6))


def _norm_proj_kernel(h_ref, g_ref, sh_ref, sc_ref, w_ref, o_ref):
    u = _rms(h_ref[...], g_ref[...]) * (1.0 + sc_ref[...]) + sh_ref[...]
    o_ref[...] = _dot(u.astype(BF16), w_ref[...])


def _norm_proj(h, g, shift, scale, w, n_ctx):
    bsz, t, d = h.shape
    n = w.shape[1]
    tm = ROW_TILE
    seg = lambda b, i: (b, (i * tm >= n_ctx).astype(jnp.int32), 0, 0)
    return pl.pallas_call(
        _norm_proj_kernel,
        grid=(bsz, t // tm),
        in_specs=[pl.BlockSpec((None, tm, d), lambda b, i: (b, i, 0)),
                  pl.BlockSpec((1, d), lambda b, i: (0, 0)),
                  pl.BlockSpec((None, None, 1, d), seg),
                  pl.BlockSpec((None, None, 1, d), seg),
                  pl.BlockSpec((d, n), lambda b, i: (0, 0))],
        out_specs=pl.BlockSpec((None, tm, n), lambda b, i: (b, i, 0)),
        out_shape=jax.ShapeDtypeStruct((bsz, t, n), F32),
        compiler_params=_params("parallel", "parallel"),
        name="norm_proj",
    )(h, g.reshape(1, d), shift, scale, w)


def _scan_constants(c, reverse):
    t = np.arange(c)[:, None]
    u = np.arange(c)[None, :]
    sels = [u <= t, u > t]
    masks = []
    m = c // 2
    while m >= 1:
        blk = t // (2 * m)
        upper_t = (t % (2 * m)) >= m
        r = blk * (2 * m) + m - 1
        s_blk = u // (2 * m)
        upper_s = (u % (2 * m)) >= m
        sels.append((upper_t & (u > r) & (u <= t)) | ((~upper_t) & (u > t) & (u <= r)))
        masks.append((blk == s_blk) & upper_t & (~upper_s))
        m //= 2
    masks.append(t == u)
    sel = np.stack(sels).astype(np.float32)
    msk = np.stack(masks).astype(np.float32)
    if reverse:
        sel = sel[:, ::-1, ::-1]
        msk = msk[:, ::-1, ::-1]
    return np.ascontiguousarray(sel.reshape(-1, c)), np.ascontiguousarray(msk)


def _chunk_order(i, n_ctx_chunks, n_chunks, reverse):
    if not reverse:
        return i
    return jnp.where(i < n_ctx_chunks, n_ctx_chunks - 1 - i, n_chunks - 1 - (i - n_ctx_chunks))


GROUP_KEYS = 256


def _decay_chunk(q, k, v, la, consts, st_ref, heads, dk, dv):
    sel_ref, mask_ref, hm_ref, hmb_ref, vm_ref = consts
    c = q.shape[0]
    n_lvl = mask_ref.shape[0] - 1
    hpg = GROUP_KEYS // dk
    cs = _dot(sel_ref[...], jnp.concatenate(_split3(la), axis=0))
    e_q = jnp.exp(cs[0:c])
    e_k = jnp.exp(cs[c:2 * c])
    e_tot = jnp.exp(jnp.sum(la, axis=0, keepdims=True))
    vb = v.astype(BF16)
    outs = []
    for g in range(heads // hpg):
        ks = slice(g * GROUP_KEYS, (g + 1) * GROUP_KEYS)
        vs = slice(g * hpg * dv, (g + 1) * hpg * dv)
        qg, kg = q[:, ks], k[:, ks]
        key_stack = lambda x: jnp.concatenate([x.astype(BF16) * hmb_ref[h] for h in range(hpg)], axis=0)
        att = mask_ref[n_lvl] * _dot_nt(qg.astype(BF16), key_stack(kg))
        for l in range(n_lvl):
            e = jnp.exp(cs[(2 + l) * c:(3 + l) * c, ks])
            att = att + mask_ref[l] * _dot_nt((qg * e).astype(BF16), key_stack(kg * e))
        v_blocks = jnp.concatenate([vb[:, vs] * vm_ref[h] for h in range(hpg)], axis=0)
        intra = _dot(att.astype(BF16), v_blocks)
        st = st_ref[g]
        q_stack = jnp.concatenate([(qg * e_q[:, ks]) * hm_ref[h] for h in range(hpg)], axis=0)
        inter = _dot_nt(q_stack.astype(BF16), st.astype(BF16))
        upd = _dot_tn(vb[:, vs], (kg * e_k[:, ks]).astype(BF16))
        new = st * e_tot[:, ks]
        for h in range(hpg):
            new = new + upd[h * dv:(h + 1) * dv] * hm_ref[h]
        st_ref[g] = new
        outs.append(intra + jnp.concatenate([inter[h * c:(h + 1) * c] for h in range(hpg)], axis=-1))
    return jnp.concatenate(outs, axis=-1)


def _log_sigmoid(x):
    return jnp.minimum(x, 0.0) - jnp.log(1.0 + jnp.exp(-jnp.abs(x)))


def _gla_kernel(*refs):
    ins, head_masks, (o_refs, st_refs) = (refs[0:8], refs[8:16]), refs[16:19], (refs[19:21], refs[21:23])

    @pl.when(pl.program_id(1) == 0)
    def _():
        for st_ref in st_refs:
            st_ref[...] = jnp.zeros_like(st_ref)

    for d, ((q_ref, k_ref, v_ref, lr_ref, gw_ref, gb_ref, sel_ref, mask_ref), o_ref, st_ref) in enumerate(
            zip(ins, o_refs, st_refs)):
        z = _dot_f32(lr_ref[...], gw_ref[...]) + gb_ref[...]
        la = _log_sigmoid(z) * (1.0 / GLA_TAU)
        q = q_ref[...] * (GLA_DK ** -0.5)
        k, v = k_ref[...], v_ref[...]
        for rows in _sub_chunks(q.shape[0], d == 1):
            o_ref[rows, :] = _decay_chunk(q[rows], k[rows], v[rows], la[rows], (sel_ref, mask_ref) + head_masks,
                                          st_ref, GLA_HEADS, GLA_DK, GLA_DV)


def _hgrn_kernel(*refs):
    ins, head_masks, (o_refs, st_refs) = (refs[0:6], refs[6:12]), refs[12:15], (refs[15:17], refs[17:19])

    @pl.when(pl.program_id(1) == 0)
    def _():
        for st_ref in st_refs:
            st_ref[...] = jnp.zeros_like(st_ref)

    for d, ((q_ref, f_ref, v_ref, lb_ref, sel_ref, mask_ref), o_ref, st_ref) in enumerate(
            zip(ins, o_refs, st_refs)):
        lb = lb_ref[...]
        f = lb + (1.0 - lb) * _sigmoid(f_ref[...])
        q, k, v, la = _silu(q_ref[...]), 1.0 - f, v_ref[...], jnp.log(f)
        for rows in _sub_chunks(q.shape[0], d == 1):
            o_ref[rows, :] = _decay_chunk(q[rows], k[rows], v[rows], la[rows], (sel_ref, mask_ref) + head_masks,
                                          st_ref, HG_HEADS, HG_EXPAND, HG_DV)


def _sub_chunks(rows, reverse):
    order = range(rows // SCAN_CHUNK)
    return [slice(j * SCAN_CHUNK, (j + 1) * SCAN_CHUNK) for j in (reversed(order) if reverse else order)]


def _scan_specs(blk, n_ctx, t, reverse, chunk=None, stacked_heads=1):
    n_blocks = t // blk
    order = functools.partial(_chunk_order, n_ctx_chunks=n_ctx // blk, n_chunks=n_blocks, reverse=reverse)

    def col(width, idx):
        return pl.BlockSpec((None, blk, width), lambda b, i: (b, order(i), idx))

    sel, msk = _scan_constants(chunk or blk, reverse)
    sel3 = np.concatenate([sel, sel, sel], axis=1)
    msk = np.tile(msk, (1, 1, stacked_heads))
    const = lambda a: pl.BlockSpec(a.shape, lambda b, i: (0,) * a.ndim)
    return n_blocks, col, const, jnp.asarray(sel3, BF16), jnp.asarray(msk, F32)


def _head_masks(dk, dv):
    hpg = GROUP_KEYS // dk
    hm = np.zeros((hpg, 1, GROUP_KEYS), np.float32)
    vm = np.zeros((hpg, 1, hpg * dv), np.float32)
    for h in range(hpg):
        hm[h, 0, h * dk:(h + 1) * dk] = 1.0
        vm[h, 0, h * dv:(h + 1) * dv] = 1.0
    return jnp.asarray(hm), jnp.asarray(hm, BF16), jnp.asarray(vm, BF16)


AB_Q, AB_K, AB_V, AB_G = 0, 256, 512, 1024
AB_HQ, AB_HF, AB_HI, AB_HG, AB_LR = 1536, 2048, 3072, 3584, 4096
AB_PAD_COLS = 4224


def _gla_scan(proj, gate_w_pad, gate_b, n_ctx):
    bsz, t, _ = proj.shape
    hpg = GROUP_KEYS // GLA_DK
    in_specs, args, outs = [], [], []
    for d in range(2):
        n_blocks, col, const, sel, msk = _scan_specs(SCAN_BLOCK, n_ctx, t, d == 1, SCAN_CHUNK, hpg)
        in_specs += [col(GLA_KEY_W, AB_Q // GLA_KEY_W), col(GLA_KEY_W, AB_K // GLA_KEY_W),
                     col(GLA_VAL_W, AB_V // GLA_VAL_W), col(LANES, AB_LR // LANES),
                     const(gate_w_pad[d]), const(gate_b[d]), const(sel), const(msk)]
        args += [proj, proj, proj, proj, gate_w_pad[d], gate_b[d], sel, msk]
        outs.append(col(GLA_VAL_W, 0))
    hm = _head_masks(GLA_DK, GLA_DV)
    return pl.pallas_call(
        _gla_kernel,
        grid=(bsz, n_blocks),
        in_specs=in_specs + [const(m) for m in hm],
        out_specs=outs,
        out_shape=[jax.ShapeDtypeStruct((bsz, t, GLA_VAL_W), F32)] * 2,
        scratch_shapes=[pltpu.VMEM((GLA_HEADS // hpg, GLA_DV, GROUP_KEYS), F32)] * 2,
        compiler_params=_params("parallel", "arbitrary"),
        name="gla_scan",
    )(*args, *hm)


def _hgrn_scan(proj, lb, n_ctx):
    bsz, t, _ = proj.shape
    hpg = GROUP_KEYS // HG_EXPAND
    in_specs, args, outs = [], [], []
    for d in range(2):
        n_blocks, col, const, sel, msk = _scan_specs(SCAN_BLOCK, n_ctx, t, d == 1, SCAN_CHUNK, hpg)
        in_specs += [col(HG_KEY_W, AB_HQ // HG_KEY_W), col(HG_KEY_W, AB_HF // HG_KEY_W + d),
                     col(HG_VAL_W, AB_HI // HG_VAL_W), const(lb[d]), const(sel), const(msk)]
        args += [proj, proj, proj, lb[d], sel, msk]
        outs.append(col(HG_VAL_W, 0))
    hm = _head_masks(HG_EXPAND, HG_DV)
    return pl.pallas_call(
        _hgrn_kernel,
        grid=(bsz, n_blocks),
        in_specs=in_specs + [const(m) for m in hm],
        out_specs=outs,
        out_shape=[jax.ShapeDtypeStruct((bsz, t, HG_VAL_W), F32)] * 2,
        scratch_shapes=[pltpu.VMEM((HG_HEADS // hpg, HG_DV, GROUP_KEYS), F32)] * 2,
        compiler_params=_params("parallel", "arbitrary"),
        name="hgrn_scan",
    )(*args, *hm)


def _mix_out_ab_kernel(*refs):
    (h_ref, gf_ref, gb_ref, hf_ref, hb_ref, gg_ref, hg_ref, gn_ref, hn_ref, w_ref, gt_ref) = refs[0:11]
    pre_in, o_ref, pre_out = refs[11:11 + N_FFN_PRE_IN], refs[11 + N_FFN_PRE_IN], refs[12 + N_FFN_PRE_IN:]
    feats = []
    for o, gate, g in ((gf_ref[...] + gb_ref[...], gg_ref[...], gn_ref[...]),
                       (hf_ref[...] + hb_ref[...], hg_ref[...], hn_ref[...])):
        for hd in range(o.shape[-1] // LANES):
            s = slice(hd * LANES, (hd + 1) * LANES)
            feats.append(_rms(o[:, s], g) * _silu(gate[:, s]))
    feat = jnp.concatenate(feats, axis=-1).astype(BF16)
    h_new = h_ref[...] + gt_ref[...] * _dot(feat, w_ref[...])
    o_ref[...] = h_new
    _ffn_pre_body(h_new, *pre_in, *pre_out)


def _mix_out_ab(h, o_gla, o_hg, proj, gla_norm_g, hg_norm_g, w_out, gate, n_ctx, pre):
    bsz, t, d = h.shape
    tm = ROW_TILE
    seg = lambda b, i: (b, (i * tm >= n_ctx).astype(jnp.int32), 0, 0)
    row = lambda width, idx: pl.BlockSpec((None, tm, width), lambda b, i: (b, i, idx))
    vec = pl.BlockSpec((1, LANES), lambda b, i: (0, 0))
    p_in, p_args, p_out, p_shape, p_scratch = _ffn_pre_parts(bsz, t, d, *pre, seg)
    outs = pl.pallas_call(
        _mix_out_ab_kernel,
        grid=(bsz, t // tm),
        in_specs=[row(d, 0), row(GLA_VAL_W, 0), row(GLA_VAL_W, 0), row(HG_VAL_W, 0), row(HG_VAL_W, 0),
                  row(GLA_VAL_W, AB_G // GLA_VAL_W), row(HG_VAL_W, AB_HG // HG_VAL_W), vec, vec,
                  pl.BlockSpec(w_out.shape, lambda b, i: (0, 0)),
                  pl.BlockSpec((None, None, 1, d), seg)] + p_in,
        out_specs=[row(d, 0)] + p_out,
        out_shape=[jax.ShapeDtypeStruct((bsz, t, d), F32)] + p_shape,
        scratch_shapes=p_scratch,
        compiler_params=_params("arbitrary", "arbitrary"),
        name="mix_out_ab",
    )(h, o_gla[0], o_gla[1], o_hg[0], o_hg[1], proj, proj, gla_norm_g.reshape(1, -1),
      hg_norm_g.reshape(1, -1), w_out, gate, *p_args)
    return outs[0], outs[1:]


HALO = 8


def _norm_proj_conv_kernel(h_ref, hp_ref, hn_ref, g_ref, sh_ref, sc_ref, w_ref, cw_ref, cb_ref,
                           hy_ref, xbc_ref, zdt_ref, *, ctx_tiles, n_tiles):
    i = pl.program_id(1)
    tm = h_ref.shape[0]
    hh = jnp.concatenate([hp_ref[...], h_ref[...], hn_ref[...]], axis=0)
    u = _rms(hh, g_ref[...]) * (1.0 + sc_ref[...]) + sh_ref[...]
    p = _dot(u.astype(BF16), w_ref[...])
    n_conv = cw_ref.shape[1]
    n_hy = hy_ref.shape[1]
    pc = p[:, 0:n_conv]
    rows_all = tm + 2 * HALO
    cur = pc[HALO:HALO + tm]
    prev = pltpu.roll(pc, 1, axis=0)[HALO:HALO + tm]
    nxt = pltpu.roll(pc, rows_all - 1, axis=0)[HALO:HALO + tm]
    first = jnp.logical_or(i == 0, i == ctx_tiles)
    last = jnp.logical_or(i == ctx_tiles - 1, i == n_tiles - 1)
    rows = lax.broadcasted_iota(jnp.int32, cur.shape, 0)
    prev = jnp.where(jnp.logical_and(first, rows == 0), 0.0, prev)
    nxt = jnp.where(jnp.logical_and(last, rows == tm - 1), 0.0, nxt)
    y = prev * cw_ref[0:1, :] + cur * cw_ref[1:2, :] + nxt * cw_ref[2:3, :] + cb_ref[...]
    hy_ref[...] = y[:, 0:n_hy]
    xbc_ref[...] = _silu(y[:, n_hy:n_conv])
    zdt_ref[...] = p[HALO:HALO + tm, n_conv:]


def _norm_proj_conv(h, g, shift, scale, w, conv_w, conv_b, n_hy, n_ctx):
    bsz, t, d = h.shape
    n = w.shape[1]
    n_conv = conv_w.shape[1]
    tm = ROW_TILE
    n_tiles, ctx_tiles = t // tm, n_ctx // tm
    r8 = tm // HALO
    last8 = t // HALO - 1
    seg = lambda b, i: (b, (i >= ctx_tiles).astype(jnp.int32), 0, 0)
    kern = functools.partial(_norm_proj_conv_kernel, ctx_tiles=ctx_tiles, n_tiles=n_tiles)
    return pl.pallas_call(
        kern,
        grid=(bsz, n_tiles),
        in_specs=[pl.BlockSpec((None, tm, d), lambda b, i: (b, i, 0)),
                  pl.BlockSpec((None, HALO, d), lambda b, i: (b, jnp.maximum(i * r8 - 1, 0), 0)),
                  pl.BlockSpec((None, HALO, d), lambda b, i: (b, jnp.minimum((i + 1) * r8, last8), 0)),
                  pl.BlockSpec((1, d), lambda b, i: (0, 0)),
                  pl.BlockSpec((None, None, 1, d), seg),
                  pl.BlockSpec((None, None, 1, d), seg),
                  pl.BlockSpec((d, n), lambda b, i: (0, 0)),
                  pl.BlockSpec((3, n_conv), lambda b, i: (0, 0)),
                  pl.BlockSpec((1, n_conv), lambda b, i: (0, 0))],
        out_specs=[pl.BlockSpec((None, tm, n_hy), lambda b, i: (b, jnp.maximum(i - ctx_tiles, 0), 0)),
                   pl.BlockSpec((None, tm, n_conv - n_hy), lambda b, i: (b, i, 0)),
                   pl.BlockSpec((None, tm, n - n_conv), lambda b, i: (b, i, 0))],
        out_shape=[jax.ShapeDtypeStruct((bsz, t - n_ctx, n_hy), F32),
                   jax.ShapeDtypeStruct((bsz, t, n_conv - n_hy), F32),
                   jax.ShapeDtypeStruct((bsz, t, n - n_conv), F32)],
        compiler_params=_params("parallel", "arbitrary"),
        name="norm_proj_conv",
    )(h, h, h, g.reshape(1, d), shift, scale, w, conv_w, conv_b)


def _ssd_kernel(*refs):
    ins, hexp_ref, o_refs, st_refs = (refs[0:7], refs[7:14]), refs[14], refs[15:17], refs[17:19]

    @pl.when(pl.program_id(1) == 0)
    def _():
        for st_ref in st_refs:
            st_ref[...] = jnp.zeros_like(st_ref)

    for (xbc_ref, dt_ref, bias_ref, alog_ref, mq_ref, mk_ref, mask_ref), o_ref, st_ref in zip(ins, o_refs, st_refs):
        _ssd_chunk(xbc_ref, dt_ref, bias_ref, alog_ref, hexp_ref, mq_ref, mk_ref, mask_ref, o_ref, st_ref)


def _ssd_chunk(xbc_ref, dt_ref, bias_ref, alog_ref, hexp_ref, mq_ref, mk_ref, mask_ref, o_ref, st_ref):
    c = xbc_ref.shape[0]
    hpg = MB_HEADS // MB_GROUPS
    gw = hpg * MB_HEAD_DIM
    dt = _softplus(dt_ref[...] + bias_ref[...])
    la = -dt * jnp.exp(alog_ref[...])
    cq = _sel_dot(mq_ref[...], la)
    ck = _sel_dot(mk_ref[...], la)
    cq_t = _dot_nt_sel(la, mq_ref[...])
    hexp = hexp_ref[...]
    dt_x = _dot_sel(dt, hexp)
    eq_x = jnp.exp(_dot_sel(cq, hexp))
    ek_x = jnp.exp(_dot_sel(ck, hexp))
    etot_x = jnp.exp(_dot_sel(jnp.sum(la, axis=0, keepdims=True), hexp))
    xs = xbc_ref[:, 0:MB_INNER] * dt_x
    mask = mask_ref[...]
    outs = []
    for g in range(MB_GROUPS):
        bm = xbc_ref[:, MB_INNER + g * MB_STATE:MB_INNER + (g + 1) * MB_STATE].astype(BF16)
        cm = xbc_ref[:, MB_INNER + MB_BC_W + g * MB_STATE:MB_INNER + MB_BC_W + (g + 1) * MB_STATE].astype(BF16)
        cb = _dot_nt(cm, bm)
        st = st_ref[g]
        gs = slice(g * gw, (g + 1) * gw)
        y_inter = _dot(cm, st.astype(BF16)) * eq_x[:, gs]
        for r in range(hpg):
            hd = g * hpg + r
            diff = cq[:, hd:hd + 1] - cq_t[hd:hd + 1, :]
            w = cb * jnp.exp(jnp.where(mask > 0.0, diff, -jnp.inf))
            ps = slice(hd * MB_HEAD_DIM, (hd + 1) * MB_HEAD_DIM)
            outs.append(_dot(w.astype(BF16), xs[:, ps].astype(BF16))
                        + y_inter[:, r * MB_HEAD_DIM:(r + 1) * MB_HEAD_DIM])
        st_ref[g] = st * etot_x[:, gs] + _dot_tn(bm, (xs[:, gs] * ek_x[:, gs]).astype(BF16))
    o_ref[...] = jnp.concatenate(outs, axis=-1)


def _dot_nt_sel(x, m01):
    hi, mid, lo = _split3(x)
    f = lambda p: lax.dot_general(p, m01, (((0,), (1,)), ((), ())), preferred_element_type=F32)
    return f(hi) + (f(mid) + f(lo))


def _ssd_scan(xbc, proj, dt_col, dt_bias, a_log, n_ctx):
    bsz, t, _ = xbc.shape
    c = SSD_CHUNK
    pad = lambda v: jnp.zeros((1, LANES), F32).at[0, :MB_HEADS].set(v)
    hexp = np.zeros((LANES, MB_INNER), np.float32)
    for hd in range(MB_HEADS):
        hexp[hd, hd * MB_HEAD_DIM:(hd + 1) * MB_HEAD_DIM] = 1.0
    hexp = jnp.asarray(hexp, BF16)
    tri = np.tril(np.ones((c, c), np.float32))
    in_specs, args, outs = [], [], []
    for d in range(2):
        n_chunks, col, const, sel, _ = _scan_specs(c, n_ctx, t, d == 1)
        mq, mk = sel[0:c, 0:c], sel[c:2 * c, 0:c]
        mask = jnp.asarray(tri[::-1, ::-1].copy() if d == 1 else tri)
        bias, alog = pad(dt_bias[d]), pad(a_log[d].astype(F32))
        in_specs += [col(xbc.shape[-1], 0), col(LANES, dt_col + d), const(bias), const(alog),
                     const(mq), const(mk), const(mask)]
        args += [xbc, proj, bias, alog, mq, mk, mask]
        outs.append(col(MB_INNER, 0))
    return pl.pallas_call(
        _ssd_kernel,
        grid=(bsz, n_chunks),
        in_specs=in_specs + [const(hexp)],
        out_specs=outs,
        out_shape=[jax.ShapeDtypeStruct((bsz, t, MB_INNER), F32)] * 2,
        scratch_shapes=[pltpu.VMEM((MB_GROUPS, MB_STATE, MB_INNER // MB_GROUPS), F32)] * 2,
        compiler_params=_params("parallel", "arbitrary"),
        name="ssd_scan",
    )(*args, hexp)


def _mix_out_cd_kernel(*refs):
    h_ref, hy_ref, yf_ref, yb_ref, xs_ref, z_ref, dsk_ref, ng_ref, w_ref, gt_ref = refs[0:10]
    pre_in, o_ref, pre_out = refs[10:10 + N_FFN_PRE_IN], refs[10 + N_FFN_PRE_IN], refs[11 + N_FFN_PRE_IN:]
    y = (yf_ref[...] + yb_ref[...] + dsk_ref[...] * xs_ref[...]) * _silu(z_ref[...])
    gw = MB_INNER // MB_GROUPS
    ys = [_rms(y[:, g * gw:(g + 1) * gw], ng_ref[:, g * gw:(g + 1) * gw]) for g in range(MB_GROUPS)]
    feat = jnp.concatenate([hy_ref[...]] + ys, axis=-1).astype(BF16)
    h_new = h_ref[...] + gt_ref[...] * _dot(feat, w_ref[...])
    o_ref[...] = h_new
    _ffn_pre_body(h_new, *pre_in, *pre_out)


def _mix_out_cd(h, hy, y_ssd, xbc, proj, z_col, d_skip_x, norm_g, w_out, gate, n_ctx, pre):
    bsz, t, d = h.shape
    tm = ROW_TILE
    n_lat = t - n_ctx
    off = n_ctx // tm
    row = lambda width, idx: pl.BlockSpec((None, tm, width), lambda b, i: (b, i + off, idx))
    vec = pl.BlockSpec((1, MB_INNER), lambda b, i: (0, 0))
    latent = lambda b, i: (b, 1, 0, 0)
    p_in, p_args, p_out, p_shape, p_scratch = _ffn_pre_parts(bsz, n_lat, d, *pre, latent)
    outs = pl.pallas_call(
        _mix_out_cd_kernel,
        grid=(bsz, n_lat // tm),
        in_specs=[row(d, 0), pl.BlockSpec((None, tm, HY_CH), lambda b, i: (b, i, 0)),
                  row(MB_INNER, 0), row(MB_INNER, 0), row(MB_INNER, 0), row(MB_INNER, z_col), vec, vec,
                  pl.BlockSpec(w_out.shape, lambda b, i: (0, 0)),
                  pl.BlockSpec((None, None, 1, d), latent)] + p_in,
        out_specs=[pl.BlockSpec((None, tm, d), lambda b, i: (b, i, 0))] + p_out,
        out_shape=[jax.ShapeDtypeStruct((bsz, n_lat, d), F32)] + p_shape,
        scratch_shapes=p_scratch,
        compiler_params=_params("arbitrary", "arbitrary"),
        name="mix_out_cd",
    )(h, hy, y_ssd[0], y_ssd[1], xbc, proj, d_skip_x, norm_g.reshape(1, -1), w_out, gate, *p_args)
    return outs[0], outs[1:]


def _top2_of4(a, b, c, d):
    hi1, lo1, hi2, lo2 = jnp.maximum(a, b), jnp.minimum(a, b), jnp.maximum(c, d), jnp.minimum(c, d)
    return jnp.maximum(hi1, hi2) + jnp.maximum(jnp.minimum(hi1, hi2), jnp.maximum(lo1, lo2))


def _first_argmax(vals, skip=None):
    idx = None
    for j, vj in enumerate(vals):
        if idx is None and skip is None:
            idx, best = jnp.zeros(vj.shape, jnp.int32), vj
            continue
        if idx is None:
            idx, best = jnp.full(vj.shape, -1, jnp.int32), jnp.full(vj.shape, -jnp.inf, F32)
        take = vj > best
        if skip is not None:
            take = jnp.logical_and(take, skip != j)
        idx = jnp.where(take, j, idx)
        best = jnp.where(take, vj, best)
    return idx, best


def _ffn_pre_body(h, g_ref, sh_ref, sc_ref, rw_ref, rb_ref, tri_ref, v_ref, ri_ref, rwt_ref, cnt_ref, carry_ref):
    @pl.when(jnp.logical_and(pl.program_id(0) == 0, pl.program_id(1) == 0))
    def _():
        carry_ref[...] = jnp.zeros_like(carry_ref)

    v = _rms(h, g_ref[...]) * (1.0 + sc_ref[...]) + sh_ref[...]
    v_ref[...] = _pack_bf16_pairs(v)
    st = _sigmoid(_dot_f32(v, rw_ref[...])).T[0:N_EXPERTS]
    sel = st + rb_ref[...]
    row = lambda a, e: a[e:e + 1]
    epg = EXPERTS_PER_GROUP
    gscore = [_top2_of4(*[row(sel, g * epg + j) for j in range(epg)]) for g in range(N_GROUPS)]
    best, _ = _first_argmax(gscore)

    def in_best(a, j):
        out = row(a, j)
        for g in range(1, N_GROUPS):
            out = jnp.where(best == g, row(a, g * epg + j), out)
        return out

    vals = [in_best(sel, j) for j in range(epg)]
    raw = [in_best(st, j) for j in range(epg)]
    i1, _ = _first_argmax(vals)
    i2, _ = _first_argmax(vals, skip=i1)
    pick = lambda i: functools.reduce(lambda acc, j: jnp.where(i == j, raw[j], acc), range(1, epg), raw[0])
    w1, w2 = pick(i1), pick(i2)
    wsum = w1 + w2
    e1, e2 = best * epg + i1, best * epg + i2

    experts = lax.broadcasted_iota(jnp.int32, st.shape, 0)
    oh1 = (experts == e1).astype(F32)
    oh2 = (experts == e2).astype(F32)
    cnt = oh1 + oh2
    before = _dot(cnt.astype(BF16), tri_ref[...]) + carry_ref[:, 0:1]
    ri_ref[0:1, :] = e1
    ri_ref[1:2, :] = e2
    ri_ref[2:3, :] = jnp.sum(oh1 * before, axis=0, keepdims=True).astype(jnp.int32)
    ri_ref[3:4, :] = jnp.sum(oh2 * before, axis=0, keepdims=True).astype(jnp.int32)
    ri_ref[4:8, :] = jnp.zeros((4, st.shape[1]), jnp.int32)
    lane_row = lax.broadcasted_iota(jnp.int32, (LANES, st.shape[1]), 0)
    rwt_ref[...] = jnp.where(lane_row == 0, w1 / wsum, jnp.where(lane_row == 1, w2 / wsum, 0.0)).T
    carry_ref[...] = carry_ref[...] + jnp.sum(cnt, axis=1, keepdims=True)
    cnt_ref[...] = carry_ref[...]


N_FFN_PRE_IN = 6


def _ffn_pre_parts(bsz, t, d, g, shift, scale, router_w_pad, router_b, seg):
    tm = ROW_TILE
    tri = jnp.asarray(np.triu(np.ones((tm, tm), np.float32), 1), BF16)
    in_specs = [pl.BlockSpec((1, d), lambda b, i: (0, 0)),
                pl.BlockSpec((None, None, 1, d), seg),
                pl.BlockSpec((None, None, 1, d), seg),
                pl.BlockSpec((d, LANES), lambda b, i: (0, 0)),
                pl.BlockSpec((N_EXPERTS, 1), lambda b, i: (0, 0)),
                pl.BlockSpec((tm, tm), lambda b, i: (0, 0))]
    args = (g.reshape(1, d), shift, scale, router_w_pad, router_b.reshape(N_EXPERTS, 1), tri)
    out_specs = [pl.BlockSpec((None, tm, d // 2), lambda b, i: (b, i, 0)),
                 pl.BlockSpec((None, 8, tm), lambda b, i: (b, 0, i)),
                 pl.BlockSpec((None, tm, LANES), lambda b, i: (b, i, 0)),
                 pl.BlockSpec((N_EXPERTS, LANES), lambda b, i: (0, 0))]
    out_shape = [jax.ShapeDtypeStruct((bsz, t, d // 2), jnp.uint32),
                 jax.ShapeDtypeStruct((bsz, 8, t), jnp.int32),
                 jax.ShapeDtypeStruct((bsz, t, LANES), F32),
                 jax.ShapeDtypeStruct((N_EXPERTS, LANES), F32)]
    return in_specs, args, out_specs, out_shape, [pltpu.VMEM((N_EXPERTS, LANES), F32)]


def _experts_kernel(be_ref, nb_ref, x_ref, wg_ref, wu_ref, wd_ref, o_ref, wg_s, wu_s, wd_s):
    i = pl.program_id(0)
    prev = be_ref[jnp.maximum(i - 1, 0)]
    changed = jnp.logical_or(i == 0, be_ref[i] != prev)

    @pl.when(changed)
    def _():
        wg_s[...] = wg_ref[...].astype(BF16)
        wu_s[...] = wu_ref[...].astype(BF16)
        wd_s[...] = wd_ref[...].astype(BF16)

    @pl.when(i < nb_ref[0])
    def _():
        x = _unpack_bf16_pairs(x_ref[...]).astype(BF16)
        hid = _silu(_dot(x, wg_s[...])) * _dot(x, wu_s[...])
        o_ref[...] = _pack_bf16_pairs(_dot(hid.astype(BF16), wd_s[...]))

    @pl.when(i >= nb_ref[0])
    def _():
        o_ref[...] = jnp.zeros_like(o_ref)


def _experts(xb, block_e, n_used, layer, w_gate, w_up, w_down):
    n_slots = xb.shape[0]
    n_blocks = n_slots // MOE_BLOCK
    d, de = w_gate.shape[-2:]
    wspec = lambda shape: pl.BlockSpec((None, None) + shape, lambda i, be, nb: (layer, be[i], 0, 0))
    return pl.pallas_call(
        _experts_kernel,
        grid_spec=pltpu.PrefetchScalarGridSpec(
            num_scalar_prefetch=2,
            grid=(n_blocks,),
            in_specs=[pl.BlockSpec((MOE_BLOCK, d // 2), lambda i, be, nb: (i, 0)),
                      wspec((d, de)), wspec((d, de)), wspec((de, d))],
            out_specs=pl.BlockSpec((MOE_BLOCK, d // 2), lambda i, be, nb: (i, 0)),
            scratch_shapes=[pltpu.VMEM((d, de), BF16), pltpu.VMEM((d, de), BF16), pltpu.VMEM((de, d), BF16)]),
        out_shape=jax.ShapeDtypeStruct((n_slots, d // 2), jnp.uint32),
        compiler_params=_params("arbitrary"),
        name="moe_experts",
    )(block_e, n_used, xb, w_gate, w_up, w_down)


def _ffn_post_kernel(h_ref, y0_ref, y1_ref, w_ref, gt_ref, g_ref, o_ref, *, final):
    w = w_ref[...]
    y = w[:, 0:1] * _unpack_bf16_pairs(y0_ref[...]) + w[:, 1:2] * _unpack_bf16_pairs(y1_ref[...])
    out = h_ref[...] + gt_ref[...] * y
    o_ref[...] = _rms(out, g_ref[...]) if final else out


def _ffn_post(h, y, w, gate, n_ctx, final_g=None):
    bsz, t, d = h.shape
    tm = ROW_TILE
    seg = lambda b, i: (b, (i * tm >= n_ctx).astype(jnp.int32), 0, 0)
    row = lambda width: pl.BlockSpec((None, tm, width), lambda b, i: (b, i, 0))
    choice = lambda kk: pl.BlockSpec((None, None, tm, d // 2), lambda b, i: (kk, b, i, 0))
    final = final_g is not None
    g = final_g if final else jnp.ones((d,), F32)
    return pl.pallas_call(
        functools.partial(_ffn_post_kernel, final=final),
        grid=(bsz, t // tm),
        in_specs=[row(d), choice(0), choice(1), row(LANES), pl.BlockSpec((None, None, 1, d), seg),
                  pl.BlockSpec((1, d), lambda b, i: (0, 0))],
        out_specs=row(d),
        out_shape=jax.ShapeDtypeStruct((bsz, t, d), F32),
        compiler_params=_params("parallel", "parallel"),
        name="ffn_post",
    )(h, y, y, w, gate, g.reshape(1, d))


def _slot_layout(n, ri, counts):
    e = jnp.swapaxes(ri[:, 0:2], 0, 1).reshape(TOP_K, n)
    rank = jnp.swapaxes(ri[:, 2:4], 0, 1).reshape(TOP_K, n)
    padded = (counts + MOE_BLOCK - 1) // MOE_BLOCK * MOE_BLOCK
    pend = jnp.cumsum(padded)
    pstart = pend - padded
    experts = jnp.arange(N_EXPERTS, dtype=jnp.int32)
    dest = rank + jnp.sum(jnp.where(e[..., None] == experts, pstart, 0), axis=-1)
    n_slots = (n * TOP_K + MOE_BLOCK - 1) // MOE_BLOCK * MOE_BLOCK + N_EXPERTS * MOE_BLOCK
    n_blocks = n_slots // MOE_BLOCK
    blk0 = jnp.arange(n_blocks, dtype=jnp.int32)[:, None] * MOE_BLOCK
    block_e = jnp.minimum(jnp.sum((pend[None, :] <= blk0).astype(jnp.int32), axis=-1), N_EXPERTS - 1)
    n_used = (pend[-1] // MOE_BLOCK).astype(jnp.int32).reshape(1)
    return dest, n_slots, block_e.astype(jnp.int32), n_used


SC_CORES, SC_SUBCORES = 2, 16
SC_WINDOW = 32


def _gather_rows(table, idx):
    n_rows, d = idx.shape[0], table.shape[1]
    workers = SC_CORES * SC_SUBCORES
    per_worker = n_rows // workers
    assert per_worker * workers == n_rows and per_worker % SC_WINDOW == 0
    mesh = plsc.VectorSubcoreMesh(core_axis_name="c", subcore_axis_name="s")

    @functools.partial(
        pl.kernel, mesh=mesh,
        out_type=jax.ShapeDtypeStruct((n_rows, d), table.dtype),
        scratch_types=[pltpu.VMEM((SC_WINDOW,), jnp.int32), pltpu.VMEM((SC_WINDOW,), jnp.int32),
                       pltpu.VMEM((SC_WINDOW, d), table.dtype), pltpu.VMEM((SC_WINDOW, d), table.dtype),
                       pltpu.SemaphoreType.DMA, pltpu.SemaphoreType.DMA],
    )
    def gather_kernel(table_hbm, idx_hbm, out_hbm, idx0, idx1, rows0, rows1, sem0, sem1):
        base = (lax.axis_index("s") * SC_CORES + lax.axis_index("c")) * per_worker
        n_win = per_worker // SC_WINDOW
        slots = ((idx0, rows0, sem0), (idx1, rows1, sem1))
        window = lambda j: pl.ds(pl.multiple_of(base + j * SC_WINDOW, 8), SC_WINDOW)

        def start(j, slot):
            idx_v, rows_v, sem = slots[slot]
            pltpu.sync_copy(idx_hbm.at[window(j)], idx_v)
            pltpu.async_copy(table_hbm.at[idx_v], rows_v, sem)

        def finish(j, slot):
            idx_v, rows_v, sem = slots[slot]
            pltpu.make_async_copy(table_hbm.at[idx_v], rows_v, sem).wait()
            pltpu.sync_copy(rows_v, out_hbm.at[window(j)])

        start(0, 0)

        @pl.loop(0, n_win, step=2)
        def _(j):
            @pl.when(j + 1 < n_win)
            def _():
                start(j + 1, 1)

            finish(j, 0)

            @pl.when(j + 2 < n_win)
            def _():
                start(j + 2, 0)

            @pl.when(j + 1 < n_win)
            def _():
                finish(j + 1, 1)

    return gather_kernel(table, idx)


SC_SCATTER_WINDOW = 16


def _scatter_rows(src, dest, n_slots):
    n, d = src.shape
    workers = SC_CORES * SC_SUBCORES
    per_worker = n // workers
    win = SC_SCATTER_WINDOW
    assert per_worker * workers == n and per_worker % win == 0 and dest.shape == (TOP_K, n)
    mesh = plsc.VectorSubcoreMesh(core_axis_name="c", subcore_axis_name="s")

    @functools.partial(
        pl.kernel, mesh=mesh,
        out_type=jax.ShapeDtypeStruct((n_slots, d), src.dtype),
        scratch_types=[pltpu.VMEM((win,), jnp.int32), pltpu.VMEM((win,), jnp.int32),
                       pltpu.VMEM((win, d), src.dtype), pltpu.SemaphoreType.DMA, pltpu.SemaphoreType.DMA],
    )
    def scatter_kernel(src_hbm, dest_hbm, out_hbm, idx0, idx1, rows_v, sem0, sem1):
        base = (lax.axis_index("s") * SC_CORES + lax.axis_index("c")) * per_worker

        @pl.loop(0, per_worker // win)
        def _(j):
            rows = pl.ds(pl.multiple_of(base + j * win, 8), win)
            pltpu.sync_copy(src_hbm.at[rows], rows_v)
            pltpu.sync_copy(dest_hbm.at[0, rows], idx0)
            pltpu.sync_copy(dest_hbm.at[1, rows], idx1)
            first = pltpu.async_copy(rows_v, out_hbm.at[idx0], sem0)
            second = pltpu.async_copy(rows_v, out_hbm.at[idx1], sem1)
            first.wait()
            second.wait()

    return scatter_kernel(src, dest)


def _alongside(gather, idx, side_fn, side_in):
    idx, side_in = lax.optimization_barrier((idx, side_in))
    return lax.optimization_barrier((gather(idx), side_fn(side_in)))


def _moe(h, pre_out, gate, layer, w_gate, w_up, w_down, n_ctx, final_g=None, side=None):
    bsz, t, d = h.shape
    n = bsz * t
    v, ri, rwt, counts = pre_out
    dest, n_slots, block_e, n_used = _slot_layout(n, ri, counts[:, 0].astype(jnp.int32))
    dispatch = lambda idx: _scatter_rows(v.reshape(n, d // 2), idx, n_slots)
    if side is None:
        xb = dispatch(dest)
    else:
        xb, side_a = _alongside(dispatch, dest, *side[0])
    yb = _experts(xb, block_e, n_used, layer, w_gate, w_up, w_down)
    combine = lambda idx: _gather_rows(yb, idx)
    dest_flat = dest.reshape(-1)
    if side is None:
        y, side_b = combine(dest_flat), None
    else:
        y, side_b = _alongside(combine, dest_flat, side[1], side_a)
    out = _ffn_post(h, y.reshape(TOP_K, bsz, t, d // 2), rwt, gate, n_ctx, final_g)
    return out if side is None else (out, side_b)


DFT_STEP = 16


def _dft_tables(n):
    r, *mats = _dft_tables_np(n)
    return (r,) + tuple(jnp.asarray(a).astype(BF16) for a in mats)


@functools.lru_cache(maxsize=None)
def _dft_tables_np(n):
    size = 2 * n
    r = int(round(math.sqrt(size)))
    assert r * r == size and r % DFT_STEP == 0
    p1 = np.arange(r // 2)[None, None, :]
    p2 = np.arange(r)[:, None, None]
    k1 = np.arange(r)[None, :, None]
    ang = 2.0 * np.pi * (((r * p1 + p2) * k1) % size) / size
    g_re, g_im = np.cos(ang), -np.sin(ang)
    g_in = np.concatenate([g_re, g_im], axis=1)
    g_out = np.concatenate([np.swapaxes(g_re, 1, 2), np.swapaxes(g_im, 1, 2)], axis=2) / size
    a2 = 2.0 * np.pi * ((np.arange(r)[:, None] * np.arange(r)[None, :]) % r) / r
    f_re, f_im = np.cos(a2), -np.sin(a2)
    f_fwd = np.block([[f_re, -f_im], [f_im, f_re]])
    f_inv = np.block([[f_re, f_im], [-f_im, f_re]])
    p1f = np.arange(r)[None, None, :]
    angf = 2.0 * np.pi * (((r * p1f + p2) * k1) % size) / size
    g_full = np.concatenate([np.cos(angf), -np.sin(angf)], axis=1)
    return (r,) + tuple(a.astype(np.float32) for a in (g_in, g_out, f_fwd, f_inv, g_full))


def _dot_f32_tn(a, b):
    ah = a.astype(BF16)
    al = (a - ah.astype(F32)).astype(BF16)
    bh = b.astype(BF16)
    bl = (b - bh.astype(F32)).astype(BF16)
    return _dot_tn(ah, bh) + (_dot_tn(ah, bl) + _dot_tn(al, bh))


def _hy_filter_kernel(z_ref, t_ref, w1_ref, b1_ref, w2_ref, b2_ref, fr_ref, w3_ref, w3b_ref, rates_ref, o_ref,
                      *, half_tiles):
    i = pl.program_id(0)
    hid = jnp.sin(fr_ref[...] * (_dot_f32(w1_ref[...], z_ref[...]) + b1_ref[...]))
    hid = jnp.sin(fr_ref[...] * (_dot_f32(w2_ref[...], hid) + b2_ref[...]))
    filt = _dot_f32_tn(hid, w3_ref[...])
    decay = jnp.exp(-t_ref[...] * rates_ref[...])
    for o in range(o_ref.shape[0]):
        o_ref[o] = filt[:, o * HY_CH:(o + 1) * HY_CH] * decay

    @pl.when(i == 0)
    def _():
        extra = _dot_f32_tn(hid[:, 0:LANES], w3b_ref[...])[0:8]
        first = lax.broadcasted_iota(jnp.int32, (8, HY_CH), 0) == 0
        for o in range(o_ref.shape[0]):
            add = extra[:, o * HY_CH:(o + 1) * HY_CH] * decay[0:8]
            o_ref[o, 0:8, :] = o_ref[o, 0:8, :] + jnp.where(first, add, 0.0)

    @pl.when(i == half_tiles)
    def _():
        for o in range(o_ref.shape[0]):
            o_ref[o, 0:1, :] = jnp.zeros((1, HY_CH), F32)


HY_TILE = 512


def _hy_kernels(n, w1, b1, w2, b2, w3, freq):
    pos = np.arange(2 * n)
    pos = np.where(pos < n, pos, 2 * n - pos).astype(np.float32)
    t = jnp.asarray(pos / np.float32(n - 1))
    bands = jnp.linspace(1e-4, HY_BANDS - 1, HY_BANDS, dtype=F32)
    ang = (2.0 * math.pi / n) * bands[:, None] * jnp.asarray(pos)[None, :]
    z = jnp.concatenate([t[None, :], jnp.cos(ang), -jnp.sin(ang)], axis=0)
    z = jnp.pad(z, ((0, LANES - z.shape[0]), (0, 0)))
    w1t = jnp.pad(w1, ((0, LANES - w1.shape[0]), (0, 0))).T
    hidden = w1.shape[1]
    col = lambda v: v.reshape(hidden, 1)
    w3d = jnp.swapaxes(w3.reshape(hidden, HY_ORDER, 2, HY_CH), 0, 2)
    w3d = jnp.swapaxes(w3d, 1, 2).reshape(2, hidden, HY_ORDER * HY_CH)
    rates = jnp.abs(jnp.linspace(HY_MIN_DECAY, HY_MAX_DECAY, HY_CH, dtype=F32)).reshape(1, HY_CH)
    tm = HY_TILE
    half_tiles = n // tm
    full = lambda a: pl.BlockSpec(a.shape, lambda i: (0,) * a.ndim)
    small = (w1t, col(b1), w2.T, col(b2), col(freq))
    return pl.pallas_call(
        functools.partial(_hy_filter_kernel, half_tiles=half_tiles),
        grid=(2 * n // tm,),
        in_specs=[pl.BlockSpec((LANES, tm), lambda i: (0, i)), pl.BlockSpec((tm, 1), lambda i: (i, 0))]
                 + [full(a) for a in small]
                 + [pl.BlockSpec((None, hidden, HY_ORDER * HY_CH), lambda i: ((i >= half_tiles).astype(jnp.int32), 0, 0)),
                    pl.BlockSpec((None, hidden, HY_ORDER * HY_CH), lambda i: (1, 0, 0)), full(rates)],
        out_specs=pl.BlockSpec((HY_ORDER, tm, HY_CH), lambda i: (0, i, 0)),
        out_shape=jax.ShapeDtypeStruct((HY_ORDER, 2 * n, HY_CH), F32),
        compiler_params=_params("parallel"),
        name="hy_kernels",
    )(z, t[:, None], *small, w3d, w3d, rates)


def _pack_complex(z):
    r = z.shape[0] // 2
    bits = lax.bitcast_convert_type(z.astype(BF16).astype(F32), jnp.uint32)
    return lax.bitcast_convert_type(bits[0:r] | (bits[r:2 * r] >> 16), F32)


def _unpack_complex(words):
    p = lax.bitcast_convert_type(words, jnp.uint32)
    re = lax.bitcast_convert_type(p & jnp.uint32(0xFFFF0000), F32)
    im = lax.bitcast_convert_type(p << 16, F32)
    return jnp.concatenate([re, im], axis=0).astype(BF16)


def _load_every(ref, j, count):
    return ref.reshape(count * DFT_STEP, LANES)[pl.ds(j, count, stride=DFT_STEP), :]


def _store_every(ref, j, count, val):
    ref.reshape(count * DFT_STEP, LANES)[pl.ds(j, count, stride=DFT_STEP), :] = val


def _dft_in_kernel(x_ref, g_ref, a_ref):
    n_seq, rh = x_ref.shape[0:2]
    for j in range(DFT_STEP):
        x = jnp.concatenate([_load_every(x_ref.at[b], j, rh) for b in range(n_seq)], axis=1).astype(BF16)
        res = _dot(g_ref[j], x)
        for b in range(n_seq):
            a_ref[b, j] = _pack_complex(res[:, b * LANES:(b + 1) * LANES])


def _dft_in(x4, col, g_in):
    bx, rh, r, _ = x4.shape
    c = HY_CH
    cbs = c // LANES
    return pl.pallas_call(
        _dft_in_kernel,
        grid=(r // DFT_STEP, cbs),
        in_specs=[pl.BlockSpec((bx, rh, DFT_STEP, LANES), lambda i, cb: (0, 0, i, col * cbs + cb)),
                  pl.BlockSpec((DFT_STEP, 2 * r, rh), lambda i, cb: (i, 0, 0))],
        out_specs=pl.BlockSpec((bx, DFT_STEP, r, LANES), lambda i, cb: (0, i, 0, cb)),
        out_shape=jax.ShapeDtypeStruct((bx, r, r, c), F32),
        compiler_params=_params("parallel", "parallel"),
        name="dft_in",
    )(x4, g_in)


def _stage2_operand(a_ref, j, r):
    return jnp.concatenate([_unpack_complex(_load_every(a_ref.at[b], j, r)) for b in range(a_ref.shape[0])],
                           axis=1)


def _dft_filt_kernel(a_ref, f_ref, k_ref):
    r = f_ref.shape[0] // 2
    for j in range(DFT_STEP):
        s = _dot(f_ref[...], _stage2_operand(a_ref, j, r))
        for o in range(a_ref.shape[0]):
            k_ref[o, j] = s[:, o * LANES:(o + 1) * LANES]


def _dft_filt(a, f_fwd):
    nq, r, _, c = a.shape
    return pl.pallas_call(
        _dft_filt_kernel,
        grid=(r // DFT_STEP, c // LANES),
        in_specs=[pl.BlockSpec((nq, r, DFT_STEP, LANES), lambda i, cb: (0, 0, i, cb)),
                  pl.BlockSpec(f_fwd.shape, lambda i, cb: (0, 0))],
        out_specs=pl.BlockSpec((nq, DFT_STEP, 2 * r, LANES), lambda i, cb: (0, i, 0, cb)),
        out_shape=jax.ShapeDtypeStruct((nq, r, 2 * r, c), F32),
        compiler_params=_params("parallel", "parallel"),
        name="dft_filt",
    )(a, f_fwd)


def _dft_mid_kernel(a_ref, k_ref, ff_ref, fi_ref, b_ref):
    r = ff_ref.shape[0] // 2
    n_seq = a_ref.shape[0]
    for j in range(DFT_STEP):
        s = _dot(ff_ref[...], _stage2_operand(a_ref, j, r))
        sr, si = s[0:r], s[r:2 * r]
        kr = jnp.concatenate([k_ref[j, 0:r, :]] * n_seq, axis=1)
        ki = jnp.concatenate([k_ref[j, r:2 * r, :]] * n_seq, axis=1)
        p = jnp.concatenate([sr * kr - si * ki, sr * ki + si * kr], axis=0).astype(BF16)
        back = _dot(fi_ref[...], p)
        for b in range(n_seq):
            b_ref[b, j] = _pack_complex(back[:, b * LANES:(b + 1) * LANES])


def _dft_mid(a, kspec, order, f_fwd, f_inv):
    bsz, r, _, c = a.shape
    return pl.pallas_call(
        _dft_mid_kernel,
        grid=(r // DFT_STEP, c // LANES),
        in_specs=[pl.BlockSpec((bsz, r, DFT_STEP, LANES), lambda i, cb: (0, 0, i, cb)),
                  pl.BlockSpec((None, DFT_STEP, 2 * r, LANES), lambda i, cb: (order, i, 0, cb)),
                  pl.BlockSpec(f_fwd.shape, lambda i, cb: (0, 0)),
                  pl.BlockSpec(f_inv.shape, lambda i, cb: (0, 0))],
        out_specs=pl.BlockSpec((bsz, DFT_STEP, r, LANES), lambda i, cb: (0, i, 0, cb)),
        out_shape=jax.ShapeDtypeStruct((bsz, r, r, c), F32),
        compiler_params=_params("parallel", "parallel"),
        name="dft_mid",
    )(a, kspec, f_fwd, f_inv)


def _dft_out_kernel(b_ref, g_ref, u_ref, x_ref, bias_ref, o_ref):
    n_seq, r = b_ref.shape[0:2]
    rh = o_ref.shape[1]
    for j in range(DFT_STEP):
        rhs = jnp.concatenate([_unpack_complex(_load_every(b_ref.at[b], j, r)) for b in range(n_seq)], axis=1)
        y = _dot(g_ref[j], rhs)
        for b in range(n_seq):
            yb = y[:, b * LANES:(b + 1) * LANES]
            _store_every(o_ref.at[b], j, rh,
                         _load_every(x_ref.at[b], j, rh) * (yb + _load_every(u_ref.at[b], j, rh) * bias_ref[...]))


def _dft_out(bm, g_out, u4, u_col, x4, x_col, bias):
    bsz, r, _, c = bm.shape
    rh = r // 2
    cbs = c // LANES
    seq = lambda col: pl.BlockSpec((bsz, rh, DFT_STEP, LANES), lambda i, cb: (0, 0, i, col * cbs + cb))
    return pl.pallas_call(
        _dft_out_kernel,
        grid=(r // DFT_STEP, cbs),
        in_specs=[pl.BlockSpec((bsz, r, DFT_STEP, LANES), lambda i, cb: (0, 0, i, cb)),
                  pl.BlockSpec((DFT_STEP, rh, 2 * r), lambda i, cb: (i, 0, 0)),
                  seq(u_col), seq(x_col), pl.BlockSpec((1, LANES), lambda i, cb: (0, cb))],
        out_specs=seq(0),
        out_shape=jax.ShapeDtypeStruct((bsz, rh, r, c), F32),
        compiler_params=_params("parallel", "parallel"),
        name="dft_out",
    )(bm, g_out, u4, x4, bias.reshape(1, c))


def _hyena_filter_stage1(n, filter_params):
    r, g_full = _dft_tables(n)[0], _dft_tables(n)[5]
    kern = _hy_kernels(n, *filter_params)
    return _dft_in(kern.reshape(-1, r, r, HY_CH), 0, g_full)


def _hyena_filter_spectra(n, stage1):
    return _dft_filt(stage1, _dft_tables(n)[3])


def _hyena(hy_in, kspec, conv_bias):
    bsz, n, _ = hy_in.shape
    r, g_in, g_out, f_fwd, f_inv, _ = _dft_tables(n)
    seq4 = hy_in.reshape(bsz, r // 2, r, 3 * HY_CH)
    zz = _dft_out(_dft_mid(_dft_in(seq4, 0, g_in), kspec, 0, f_fwd, f_inv), g_out,
                  seq4, 0, seq4, 1, conv_bias[0])
    out = _dft_out(_dft_mid(_dft_in(zz, 0, g_in), kspec, 1, f_fwd, f_inv), g_out,
                   zz, 0, seq4, 2, conv_bias[1])
    return out.reshape(bsz, n, HY_CH)


def _reorder_ab(w):
    gq, gk, gv, gg, lr_f, lr_b, hq, hf_f, hf_b, hi, hg = jnp.split(
        w, np.cumsum([256, 256, 512, 512, 16, 16, 512, 512, 512, 512, 512])[:-1].tolist(), axis=-1)
    pad = jnp.zeros((w.shape[0], AB_PAD_COLS - AB_LR - 2 * GLA_LOW_RANK), w.dtype)
    return jnp.concatenate([gq, gk, gv, gg, hq, hf_f, hf_b, hi, hg, lr_f, lr_b, pad], axis=-1)


def _reorder_cd(w):
    hy, z, xbc, dt_f, dt_b = jnp.split(w, np.cumsum([1536, 512, 1024, 8, 8])[:-1].tolist(), axis=-1)
    pad = jnp.zeros((w.shape[0], LANES - MB_HEADS), w.dtype)
    return jnp.concatenate([hy, xbc, z, dt_f, pad, dt_b, pad], axis=-1)


def kernel(x, c, ctx, c_ctx, ada_w, ada_b, norm_mix_g, norm_ffn_g, norm_out_g, ab_w_in, ab_w_out, gla_gate_w, gla_gate_b, gla_norm_g, hg_lb, hg_norm_g, cd_w_in, cd_w_out, hy_short_w, hy_short_b, hy_w1, hy_b1, hy_w2, hy_b2, hy_w3, hy_freq, hy_bias, mb_conv_w, mb_conv_b, mb_dt_bias, mb_a_log, mb_d, mb_norm_g, router_w, router_b, moe_w_gate, moe_w_up, moe_w_down):
    bsz, n_lat, d = x.shape
    n_ctx = ctx.shape[1]
    t = n_ctx + n_lat
    assert ada_w.shape[0] == 2 and ab_w_in.shape[0] == 1 and cd_w_in.shape[0] == 1

    cond = jnp.zeros((8, d), F32).at[:bsz].set(c).at[bsz].set(c_ctx)
    m = _adaln(cond, ada_w, ada_b)

    def mods(layer):
        lat = m[layer, :bsz].reshape(bsz, 6, d)
        cx = jnp.broadcast_to(m[layer, bsz].reshape(1, 6, d), (bsz, 6, d))
        both = jnp.stack([cx, lat], axis=1)
        return [both[:, :, j][:, :, None, :] for j in range(6)]

    lb_all = jnp.cumsum(jax.nn.softmax(hg_lb.astype(F32), axis=1), axis=1)
    router_w_pad = jnp.zeros((d, LANES), F32).at[:, :N_EXPERTS].set(router_w)
    h = jnp.concatenate([ctx, x], axis=1)

    sh_m, sc_m, gt_m, sh_f, sc_f, gt_f = mods(0)
    proj = _norm_proj(h, norm_mix_g[0], sh_m, sc_m, _reorder_ab(ab_w_in[0]).astype(BF16), n_ctx)
    gwp = [jnp.zeros((LANES, GLA_KEY_W), F32).at[GLA_LOW_RANK * dd:GLA_LOW_RANK * (dd + 1)].set(gla_gate_w[0, dd])
           for dd in range(2)]
    o_gla = _gla_scan(proj, gwp, [gla_gate_b[0, dd].reshape(1, -1) for dd in range(2)], n_ctx)
    o_hg = _hgrn_scan(proj, [lb_all[dd, 0].reshape(1, -1) for dd in range(2)], n_ctx)
    h, pre_out = _mix_out_ab(h, o_gla, o_hg, proj, gla_norm_g[0], hg_norm_g[0], ab_w_out[0].astype(BF16), gt_m,
                             n_ctx, (norm_ffn_g[0], sh_f, sc_f, router_w_pad, router_b))
    filter_params = (hy_w1[0], hy_b1[0], hy_w2[0], hy_b2[0], hy_w3[0], hy_freq[0])
    side = ((functools.partial(_hyena_filter_stage1, n_lat), filter_params),
            functools.partial(_hyena_filter_spectra, n_lat))
    h, kspec = _moe(h, pre_out, gt_f, 0, moe_w_gate, moe_w_up, moe_w_down, n_ctx, side=side)

    sh_m, sc_m, gt_m, sh_f, sc_f, gt_f = mods(1)
    conv_w = jnp.concatenate([hy_short_w[0], mb_conv_w[0]], axis=0).T
    conv_b = jnp.concatenate([hy_short_b[0], mb_conv_b[0]]).reshape(1, -1)
    hy_in, xbc, zdt = _norm_proj_conv(h, norm_mix_g[1], sh_m, sc_m, _reorder_cd(cd_w_in[0]).astype(BF16),
                                      conv_w, conv_b, 3 * HY_CH, n_ctx)
    hy = _hyena(hy_in, kspec, hy_bias[0])
    y_ssd = _ssd_scan(xbc, zdt, MB_INNER // LANES, mb_dt_bias[0], mb_a_log[0], n_ctx)
    d_skip_x = jnp.repeat(mb_d[0], MB_HEAD_DIM).reshape(1, MB_INNER)
    h, pre_out = _mix_out_cd(h, hy, y_ssd, xbc, zdt, 0, d_skip_x, mb_norm_g[0], cd_w_out[0].astype(BF16), gt_m,
                             n_ctx, (norm_ffn_g[1], sh_f, sc_f, router_w_pad, router_b))
    return _moe(h, pre_out, gt_f, 1, moe_w_gate, moe_w_up, moe_w_down, 0, final_g=norm_out_g)
```

```python
import functools
import math

import numpy as np
import jax
import jax.numpy as jnp
from jax import lax
from jax.experimental import pallas as pl
from jax.experimental.pallas import tpu as pltpu
from jax.experimental.pallas import tpu_sc as plsc

NORM_EPS = 1e-6
GLA_HEADS, GLA_DK, GLA_DV, GLA_LOW_RANK, GLA_TAU = 4, 64, 128, 16, 16.0
GLA_KEY_W, GLA_VAL_W = GLA_HEADS * GLA_DK, GLA_HEADS * GLA_DV
HG_HEADS, HG_EXPAND, HG_DV = 4, 128, 128
HG_KEY_W, HG_VAL_W = HG_HEADS * HG_EXPAND, HG_HEADS * HG_DV
HY_CH, HY_ORDER, HY_SHORT, HY_BANDS, HY_FILT_HID = 512, 2, 3, 16, 64
HY_MIN_DECAY = math.log(1e-2) / 1.5
HY_MAX_DECAY = math.log(1e-2) / 0.3
MB_HEADS, MB_HEAD_DIM, MB_GROUPS, MB_STATE = 8, 64, 2, 128
MB_INNER = MB_HEADS * MB_HEAD_DIM
MB_BC_W = MB_GROUPS * MB_STATE
N_EXPERTS, N_GROUPS, TOP_K, MOE_BLOCK = 16, 4, 2, 256
EXPERTS_PER_GROUP = N_EXPERTS // N_GROUPS

LANES = 128
SCAN_CHUNK = 64
SCAN_BLOCK = 256
SSD_CHUNK = 128
ROW_TILE = 256
VMEM_LIMIT = 56 * 1024 * 1024

BF16 = jnp.bfloat16
F32 = jnp.float32


def _params(*sem):
    return pltpu.CompilerParams(dimension_semantics=sem, vmem_limit_bytes=VMEM_LIMIT)


def _split3(x):
    hi = x.astype(BF16)
    r1 = x - hi.astype(F32)
    mid = r1.astype(BF16)
    lo = (r1 - mid.astype(F32)).astype(BF16)
    return hi, mid, lo


def _dot(a, b):
    return jnp.dot(a, b, preferred_element_type=F32)


def _dot_nt(a, b):
    return lax.dot_general(a, b, (((1,), (1,)), ((), ())), preferred_element_type=F32)


def _dot_tn(a, b):
    return lax.dot_general(a, b, (((0,), (0,)), ((), ())), preferred_element_type=F32)


def _sel_dot(m01, x):
    hi, mid, lo = _split3(x)
    return _dot(m01, hi) + (_dot(m01, mid) + _dot(m01, lo))


def _dot_sel(x, m01):
    hi, mid, lo = _split3(x)
    return _dot(hi, m01) + (_dot(mid, m01) + _dot(lo, m01))


def _dot_f32(a, b):
    ah = a.astype(BF16)
    al = (a - ah.astype(F32)).astype(BF16)
    bh = b.astype(BF16)
    bl = (b - bh.astype(F32)).astype(BF16)
    return _dot(ah, bh) + (_dot(ah, bl) + _dot(al, bh))


def _silu(x):
    return x * (1.0 / (1.0 + jnp.exp(-x)))


def _sigmoid(x):
    return 1.0 / (1.0 + jnp.exp(-x))


def _softplus(x):
    return jnp.maximum(x, 0.0) + jnp.log(1.0 + jnp.exp(-jnp.abs(x)))


def _pack_bf16_pairs(x):
    bits = lax.bitcast_convert_type(x.astype(BF16).astype(F32), jnp.uint32)
    half = x.shape[1] // 2
    return bits[:, :half] | (bits[:, half:] >> 16)


def _unpack_bf16_pairs(p):
    hi = lax.bitcast_convert_type(p & jnp.uint32(0xFFFF0000), F32)
    lo = lax.bitcast_convert_type(p << 16, F32)
    return jnp.concatenate([hi, lo], axis=1)


def _rms(x, g):
    return x * lax.rsqrt(jnp.mean(x * x, axis=-1, keepdims=True) + NORM_EPS) * g


def _adaln_kernel(c_ref, w_ref, b_ref, o_ref):
    o_ref[...] = _dot_f32(_silu(c_ref[...]), w_ref[...]) + b_ref[...]


def _adaln(cond, w, b):
    n_l, d, n6 = w.shape
    tn = 1536
    return pl.pallas_call(
        _adaln_kernel,
        grid=(n_l, n6 // tn),
        in_specs=[pl.BlockSpec((8, d), lambda l, j: (0, 0)),
                  pl.BlockSpec((None, d, tn), lambda l, j: (l, 0, j)),
                  pl.BlockSpec((None, 1, tn), lambda l, j: (l, 0, j))],
        out_specs=pl.BlockSpec((None, 8, tn), lambda l, j: (l, 0, j)),
        out_shape=jax.ShapeDtypeStruct((n_l, 8, n6), F32),
        compiler_params=_params("parallel", "parallel"),
        name="adaln",
    )(cond, w, b.reshape(n_l, 1, n6))


def _project(u, w_refs):
    return jnp.concatenate([_dot(u, w_ref[...]) for w_ref in w_refs], axis=-1)


def _norm_proj_kernel(h_ref, g_ref, sh_ref, sc_ref, *refs):
    u = _rms(h_ref[...], g_ref[...]) * (1.0 + sc_ref[...]) + sh_ref[...]
    refs[-1][...] = _project(u.astype(BF16), refs[:-1])


def _norm_proj(h, g, shift, scale, ws, n_ctx):
    bsz, t, d = h.shape
    n = sum(w.shape[1] for w in ws)
    tm = ROW_TILE
    seg = lambda b, i: (b, (i * tm >= n_ctx).astype(jnp.int32), 0, 0)
    return pl.pallas_call(
        _norm_proj_kernel,
        grid=(bsz, t // tm),
        in_specs=[pl.BlockSpec((None, tm, d), lambda b, i: (b, i, 0)),
                  pl.BlockSpec((1, d), lambda b, i: (0, 0)),
                  pl.BlockSpec((None, None, 1, d), seg),
                  pl.BlockSpec((None, None, 1, d), seg)]
                 + [pl.BlockSpec(w.shape, lambda b, i: (0, 0)) for w in ws],
        out_specs=pl.BlockSpec((None, tm, n), lambda b, i: (b, i, 0)),
        out_shape=jax.ShapeDtypeStruct((bsz, t, n), F32),
        compiler_params=_params("parallel", "parallel"),
        name="norm_proj",
    )(h, g.reshape(1, d), shift, scale, *ws)


def _scan_constants(c, reverse):
    t = np.arange(c)[:, None]
    u = np.arange(c)[None, :]
    sels = [u <= t, u > t]
    masks = []
    m = c // 2
    while m >= 1:
        blk = t // (2 * m)
        upper_t = (t % (2 * m)) >= m
        r = blk * (2 * m) + m - 1
        s_blk = u // (2 * m)
        upper_s = (u % (2 * m)) >= m
        sels.append((upper_t & (u > r) & (u <= t)) | ((~upper_t) & (u > t) & (u <= r)))
        masks.append((blk == s_blk) & upper_t & (~upper_s))
        m //= 2
    masks.append(t == u)
    sel = np.stack(sels).astype(np.float32)
    msk = np.stack(masks).astype(np.float32)
    if reverse:
        sel = sel[:, ::-1, ::-1]
        msk = msk[:, ::-1, ::-1]
    return np.ascontiguousarray(sel.reshape(-1, c)), np.ascontiguousarray(msk)


def _chunk_order(i, n_ctx_chunks, n_chunks, reverse):
    if not reverse:
        return i
    return jnp.where(i < n_ctx_chunks, n_ctx_chunks - 1 - i, n_chunks - 1 - (i - n_ctx_chunks))


GROUP_KEYS = 256


def _decay_chunk(q, k, v, la, consts, st_ref, heads, dk, dv):
    sel_ref, mask_ref, hm_ref, hmb_ref, vm_ref = consts
    c = q.shape[0]
    n_lvl = mask_ref.shape[0] - 1
    hpg = GROUP_KEYS // dk
    cs = _dot(sel_ref[...], jnp.concatenate(_split3(la), axis=0))
    e_q = jnp.exp(cs[0:c])
    e_k = jnp.exp(cs[c:2 * c])
    e_tot = jnp.exp(jnp.sum(la, axis=0, keepdims=True))
    vb = v.astype(BF16)
    outs = []
    for g in range(heads // hpg):
        ks = slice(g * GROUP_KEYS, (g + 1) * GROUP_KEYS)
        vs = slice(g * hpg * dv, (g + 1) * hpg * dv)
        qg, kg = q[:, ks], k[:, ks]
        key_stack = lambda x: jnp.concatenate([x.astype(BF16) * hmb_ref[h] for h in range(hpg)], axis=0)
        att = mask_ref[n_lvl] * _dot_nt(qg.astype(BF16), key_stack(kg))
        for l in range(n_lvl):
            e = jnp.exp(cs[(2 + l) * c:(3 + l) * c, ks])
            att = att + mask_ref[l] * _dot_nt((qg * e).astype(BF16), key_stack(kg * e))
        v_blocks = jnp.concatenate([vb[:, vs] * vm_ref[h] for h in range(hpg)], axis=0)
        intra = _dot(att.astype(BF16), v_blocks)
        st = st_ref[g]
        q_stack = jnp.concatenate([(qg * e_q[:, ks]) * hm_ref[h] for h in range(hpg)], axis=0)
        inter = _dot_nt(q_stack.astype(BF16), st.astype(BF16))
        upd = _dot_tn(vb[:, vs], (kg * e_k[:, ks]).astype(BF16))
        new = st * e_tot[:, ks]
        for h in range(hpg):
            new = new + upd[h * dv:(h + 1) * dv] * hm_ref[h]
        st_ref[g] = new
        outs.append(intra + jnp.concatenate([inter[h * c:(h + 1) * c] for h in range(hpg)], axis=-1))
    return jnp.concatenate(outs, axis=-1)


def _log_sigmoid(x):
    return jnp.minimum(x, 0.0) - jnp.log(1.0 + jnp.exp(-jnp.abs(x)))


def _gla_kernel(*refs):
    ins, head_masks, (o_refs, st_refs) = (refs[0:8], refs[8:16]), refs[16:19], (refs[19:21], refs[21:23])

    @pl.when(pl.program_id(1) == 0)
    def _():
        for st_ref in st_refs:
            st_ref[...] = jnp.zeros_like(st_ref)

    for d, ((q_ref, k_ref, v_ref, lr_ref, gw_ref, gb_ref, sel_ref, mask_ref), o_ref, st_ref) in enumerate(
            zip(ins, o_refs, st_refs)):
        z = _dot_f32(lr_ref[...], gw_ref[...]) + gb_ref[...]
        la = _log_sigmoid(z) * (1.0 / GLA_TAU)
        q = q_ref[...] * (GLA_DK ** -0.5)
        k, v = k_ref[...], v_ref[...]
        for rows in _sub_chunks(q.shape[0], d == 1):
            o_ref[rows, :] = _decay_chunk(q[rows], k[rows], v[rows], la[rows], (sel_ref, mask_ref) + head_masks,
                                          st_ref, GLA_HEADS, GLA_DK, GLA_DV).astype(o_ref.dtype)


def _hgrn_kernel(*refs):
    ins, head_masks, (o_refs, st_refs) = (refs[0:6], refs[6:12]), refs[12:15], (refs[15:17], refs[17:19])

    @pl.when(pl.program_id(1) == 0)
    def _():
        for st_ref in st_refs:
            st_ref[...] = jnp.zeros_like(st_ref)

    for d, ((q_ref, f_ref, v_ref, lb_ref, sel_ref, mask_ref), o_ref, st_ref) in enumerate(
            zip(ins, o_refs, st_refs)):
        lb = lb_ref[...]
        f = lb + (1.0 - lb) * _sigmoid(f_ref[...])
        q, k, v, la = _silu(q_ref[...]), 1.0 - f, v_ref[...], jnp.log(f)
        for rows in _sub_chunks(q.shape[0], d == 1):
            o_ref[rows, :] = _decay_chunk(q[rows], k[rows], v[rows], la[rows], (sel_ref, mask_ref) + head_masks,
                                          st_ref, HG_HEADS, HG_EXPAND, HG_DV).astype(o_ref.dtype)


def _sub_chunks(rows, reverse):
    order = range(rows // SCAN_CHUNK)
    return [slice(j * SCAN_CHUNK, (j + 1) * SCAN_CHUNK) for j in (reversed(order) if reverse else order)]


def _scan_specs(blk, n_ctx, t, reverse, chunk=None, stacked_heads=1):
    n_blocks = t // blk
    order = functools.partial(_chunk_order, n_ctx_chunks=n_ctx // blk, n_chunks=n_blocks, reverse=reverse)

    def col(width, idx):
        return pl.BlockSpec((None, blk, width), lambda b, i: (b, order(i), idx))

    sel, msk = _scan_constants(chunk or blk, reverse)
    sel3 = np.concatenate([sel, sel, sel], axis=1)
    msk = np.tile(msk, (1, 1, stacked_heads))
    const = lambda a: pl.BlockSpec(a.shape, lambda b, i: (0,) * a.ndim)
    return n_blocks, col, const, jnp.asarray(sel3, BF16), jnp.asarray(msk, F32)


def _head_masks(dk, dv):
    hpg = GROUP_KEYS // dk
    hm = np.zeros((hpg, 1, GROUP_KEYS), np.float32)
    vm = np.zeros((hpg, 1, hpg * dv), np.float32)
    for h in range(hpg):
        hm[h, 0, h * dk:(h + 1) * dk] = 1.0
        vm[h, 0, h * dv:(h + 1) * dv] = 1.0
    return jnp.asarray(hm), jnp.asarray(hm, BF16), jnp.asarray(vm, BF16)


AB_Q, AB_K, AB_V, AB_G = 0, 256, 512, 1024
AB_HQ, AB_HF, AB_HI, AB_HG, AB_LR = 1536, 2048, 3072, 3584, 4096
AB_PAD_COLS = 4224


def _gla_scan(proj, gate_w_pad, gate_b, n_ctx):
    bsz, t, _ = proj.shape
    hpg = GROUP_KEYS // GLA_DK
    in_specs, args, outs = [], [], []
    for d in range(2):
        n_blocks, col, const, sel, msk = _scan_specs(SCAN_BLOCK, n_ctx, t, d == 1, SCAN_CHUNK, hpg)
        in_specs += [col(GLA_KEY_W, AB_Q // GLA_KEY_W), col(GLA_KEY_W, AB_K // GLA_KEY_W),
                     col(GLA_VAL_W, AB_V // GLA_VAL_W), col(LANES, AB_LR // LANES),
                     const(gate_w_pad[d]), const(gate_b[d]), const(sel), const(msk)]
        args += [proj, proj, proj, proj, gate_w_pad[d], gate_b[d], sel, msk]
        outs.append(col(GLA_VAL_W, 0))
    hm = _head_masks(GLA_DK, GLA_DV)
    return pl.pallas_call(
        _gla_kernel,
        grid=(bsz, n_blocks),
        in_specs=in_specs + [const(m) for m in hm],
        out_specs=outs,
        out_shape=[jax.ShapeDtypeStruct((bsz, t, GLA_VAL_W), BF16)] * 2,
        scratch_shapes=[pltpu.VMEM((GLA_HEADS // hpg, GLA_DV, GROUP_KEYS), F32)] * 2,
        compiler_params=_params("parallel", "arbitrary"),
        name="gla_scan",
    )(*args, *hm)


def _hgrn_scan(proj, lb, n_ctx):
    bsz, t, _ = proj.shape
    hpg = GROUP_KEYS // HG_EXPAND
    in_specs, args, outs = [], [], []
    for d in range(2):
        n_blocks, col, const, sel, msk = _scan_specs(SCAN_BLOCK, n_ctx, t, d == 1, SCAN_CHUNK, hpg)
        in_specs += [col(HG_KEY_W, AB_HQ // HG_KEY_W), col(HG_KEY_W, AB_HF // HG_KEY_W + d),
                     col(HG_VAL_W, AB_HI // HG_VAL_W), const(lb[d]), const(sel), const(msk)]
        args += [proj, proj, proj, lb[d], sel, msk]
        outs.append(col(HG_VAL_W, 0))
    hm = _head_masks(HG_EXPAND, HG_DV)
    return pl.pallas_call(
        _hgrn_kernel,
        grid=(bsz, n_blocks),
        in_specs=in_specs + [const(m) for m in hm],
        out_specs=outs,
        out_shape=[jax.ShapeDtypeStruct((bsz, t, HG_VAL_W), BF16)] * 2,
        scratch_shapes=[pltpu.VMEM((HG_HEADS // hpg, HG_DV, GROUP_KEYS), F32)] * 2,
        compiler_params=_params("parallel", "arbitrary"),
        name="hgrn_scan",
    )(*args, *hm)


def _mix_out_ab_kernel(*refs):
    (h_ref, gf_ref, gb_ref, hf_ref, hb_ref, gg_ref, hg_ref, gn_ref, hn_ref, w_ref, gt_ref) = refs[0:11]
    pre_in, o_ref, pre_out = refs[11:11 + N_FFN_PRE_IN], refs[11 + N_FFN_PRE_IN], refs[12 + N_FFN_PRE_IN:]
    feats = []
    both = lambda fwd_ref, bwd_ref: fwd_ref[...].astype(F32) + bwd_ref[...].astype(F32)
    for o, gate, g in ((both(gf_ref, gb_ref), gg_ref[...], gn_ref[...]),
                       (both(hf_ref, hb_ref), hg_ref[...], hn_ref[...])):
        for hd in range(o.shape[-1] // LANES):
            s = slice(hd * LANES, (hd + 1) * LANES)
            feats.append(_rms(o[:, s], g) * _silu(gate[:, s]))
    feat = jnp.concatenate(feats, axis=-1).astype(BF16)
    h_new = h_ref[...] + gt_ref[...] * _dot(feat, w_ref[...])
    o_ref[...] = h_new
    _ffn_pre_body(h_new, *pre_in, *pre_out)


def _mix_out_ab(h, o_gla, o_hg, proj, gla_norm_g, hg_norm_g, w_out, gate, n_ctx, pre):
    bsz, t, d = h.shape
    tm = ROW_TILE
    seg = lambda b, i: (b, (i * tm >= n_ctx).astype(jnp.int32), 0, 0)
    row = lambda width, idx: pl.BlockSpec((None, tm, width), lambda b, i: (b, i, idx))
    vec = pl.BlockSpec((1, LANES), lambda b, i: (0, 0))
    p_in, p_args, p_out, p_shape, p_scratch = _ffn_pre_parts(bsz, t, d, *pre, seg)
    outs = pl.pallas_call(
        _mix_out_ab_kernel,
        grid=(bsz, t // tm),
        in_specs=[row(d, 0), row(GLA_VAL_W, 0), row(GLA_VAL_W, 0), row(HG_VAL_W, 0), row(HG_VAL_W, 0),
                  row(GLA_VAL_W, AB_G // GLA_VAL_W), row(HG_VAL_W, AB_HG // HG_VAL_W), vec, vec,
                  pl.BlockSpec(w_out.shape, lambda b, i: (0, 0)),
                  pl.BlockSpec((None, None, 1, d), seg)] + p_in,
        out_specs=[row(d, 0)] + p_out,
        out_shape=[jax.ShapeDtypeStruct((bsz, t, d), F32)] + p_shape,
        scratch_shapes=p_scratch,
        compiler_params=_params("arbitrary", "arbitrary"),
        name="mix_out_ab",
    )(h, o_gla[0], o_gla[1], o_hg[0], o_hg[1], proj, proj, gla_norm_g.reshape(1, -1),
      hg_norm_g.reshape(1, -1), w_out, gate, *p_args)
    return outs[0], outs[1:]


HALO = 8


def _norm_proj_conv_kernel(h_ref, hp_ref, hn_ref, g_ref, sh_ref, sc_ref, cw_ref, cb_ref, *refs,
                           ctx_tiles, n_tiles):
    w_refs, (hy_ref, xbc_ref, zdt_ref) = refs[:-3], refs[-3:]
    i = pl.program_id(1)
    tm = h_ref.shape[0]
    hh = jnp.concatenate([hp_ref[...], h_ref[...], hn_ref[...]], axis=0)
    u = _rms(hh, g_ref[...]) * (1.0 + sc_ref[...]) + sh_ref[...]
    p = _project(u.astype(BF16), w_refs)
    n_conv = cw_ref.shape[1]
    n_hy = hy_ref.shape[1]
    pc = p[:, 0:n_conv]
    rows_all = tm + 2 * HALO
    cur = pc[HALO:HALO + tm]
    prev = pltpu.roll(pc, 1, axis=0)[HALO:HALO + tm]
    nxt = pltpu.roll(pc, rows_all - 1, axis=0)[HALO:HALO + tm]
    first = jnp.logical_or(i == 0, i == ctx_tiles)
    last = jnp.logical_or(i == ctx_tiles - 1, i == n_tiles - 1)
    rows = lax.broadcasted_iota(jnp.int32, cur.shape, 0)
    prev = jnp.where(jnp.logical_and(first, rows == 0), 0.0, prev)
    nxt = jnp.where(jnp.logical_and(last, rows == tm - 1), 0.0, nxt)
    y = prev * cw_ref[0:1, :] + cur * cw_ref[1:2, :] + nxt * cw_ref[2:3, :] + cb_ref[...]
    hy_ref[...] = y[:, 0:n_hy]
    xbc_ref[...] = _silu(y[:, n_hy:n_conv])
    zdt_ref[...] = p[HALO:HALO + tm, n_conv:]


def _norm_proj_conv(h, g, shift, scale, ws, conv_w, conv_b, n_hy, n_ctx):
    bsz, t, d = h.shape
    n = sum(w.shape[1] for w in ws)
    n_conv = conv_w.shape[1]
    tm = ROW_TILE
    n_tiles, ctx_tiles = t // tm, n_ctx // tm
    r8 = tm // HALO
    last8 = t // HALO - 1
    seg = lambda b, i: (b, (i >= ctx_tiles).astype(jnp.int32), 0, 0)
    kern = functools.partial(_norm_proj_conv_kernel, ctx_tiles=ctx_tiles, n_tiles=n_tiles)
    return pl.pallas_call(
        kern,
        grid=(bsz, n_tiles),
        in_specs=[pl.BlockSpec((None, tm, d), lambda b, i: (b, i, 0)),
                  pl.BlockSpec((None, HALO, d), lambda b, i: (b, jnp.maximum(i * r8 - 1, 0), 0)),
                  pl.BlockSpec((None, HALO, d), lambda b, i: (b, jnp.minimum((i + 1) * r8, last8), 0)),
                  pl.BlockSpec((1, d), lambda b, i: (0, 0)),
                  pl.BlockSpec((None, None, 1, d), seg),
                  pl.BlockSpec((None, None, 1, d), seg),
                  pl.BlockSpec((3, n_conv), lambda b, i: (0, 0)),
                  pl.BlockSpec((1, n_conv), lambda b, i: (0, 0))]
                 + [pl.BlockSpec(w.shape, lambda b, i: (0, 0)) for w in ws],
        out_specs=[pl.BlockSpec((None, tm, n_hy), lambda b, i: (b, jnp.maximum(i - ctx_tiles, 0), 0)),
                   pl.BlockSpec((None, tm, n_conv - n_hy), lambda b, i: (b, i, 0)),
                   pl.BlockSpec((None, tm, n - n_conv), lambda b, i: (b, i, 0))],
        out_shape=[jax.ShapeDtypeStruct((bsz, t - n_ctx, n_hy), F32),
                   jax.ShapeDtypeStruct((bsz, t, n_conv - n_hy), F32),
                   jax.ShapeDtypeStruct((bsz, t, n - n_conv), F32)],
        compiler_params=_params("parallel", "arbitrary"),
        name="norm_proj_conv",
    )(h, h, h, g.reshape(1, d), shift, scale, conv_w, conv_b, *ws)


def _ssd_kernel(*refs):
    ins, hexp_ref, o_refs, st_refs = (refs[0:7], refs[7:14]), refs[14], refs[15:17], refs[17:19]

    @pl.when(pl.program_id(1) == 0)
    def _():
        for st_ref in st_refs:
            st_ref[...] = jnp.zeros_like(st_ref)

    for (xbc_ref, dt_ref, bias_ref, alog_ref, mq_ref, mk_ref, mask_ref), o_ref, st_ref in zip(ins, o_refs, st_refs):
        _ssd_chunk(xbc_ref, dt_ref, bias_ref, alog_ref, hexp_ref, mq_ref, mk_ref, mask_ref, o_ref, st_ref)


def _ssd_chunk(xbc_ref, dt_ref, bias_ref, alog_ref, hexp_ref, mq_ref, mk_ref, mask_ref, o_ref, st_ref):
    c = xbc_ref.shape[0]
    hpg = MB_HEADS // MB_GROUPS
    gw = hpg * MB_HEAD_DIM
    dt = _softplus(dt_ref[...] + bias_ref[...])
    la = -dt * jnp.exp(alog_ref[...])
    cq = _sel_dot(mq_ref[...], la)
    ck = _sel_dot(mk_ref[...], la)
    cq_t = _dot_nt_sel(la, mq_ref[...])
    hexp = hexp_ref[...]
    dt_x = _dot_sel(dt, hexp)
    eq_x = jnp.exp(_dot_sel(cq, hexp))
    ek_x = jnp.exp(_dot_sel(ck, hexp))
    etot_x = jnp.exp(_dot_sel(jnp.sum(la, axis=0, keepdims=True), hexp))
    xs = xbc_ref[:, 0:MB_INNER] * dt_x
    mask = mask_ref[...]
    outs = []
    for g in range(MB_GROUPS):
        bm = xbc_ref[:, MB_INNER + g * MB_STATE:MB_INNER + (g + 1) * MB_STATE].astype(BF16)
        cm = xbc_ref[:, MB_INNER + MB_BC_W + g * MB_STATE:MB_INNER + MB_BC_W + (g + 1) * MB_STATE].astype(BF16)
        cb = _dot_nt(cm, bm)
        st = st_ref[g]
        gs = slice(g * gw, (g + 1) * gw)
        y_inter = _dot(cm, st.astype(BF16)) * eq_x[:, gs]
        for r in range(hpg):
            hd = g * hpg + r
            diff = cq[:, hd:hd + 1] - cq_t[hd:hd + 1, :]
            w = cb * jnp.exp(jnp.where(mask > 0.0, diff, -jnp.inf))
            ps = slice(hd * MB_HEAD_DIM, (hd + 1) * MB_HEAD_DIM)
            outs.append(_dot(w.astype(BF16), xs[:, ps].astype(BF16))
                        + y_inter[:, r * MB_HEAD_DIM:(r + 1) * MB_HEAD_DIM])
        st_ref[g] = st * etot_x[:, gs] + _dot_tn(bm, (xs[:, gs] * ek_x[:, gs]).astype(BF16))
    o_ref[...] = jnp.concatenate(outs, axis=-1).astype(o_ref.dtype)


def _dot_nt_sel(x, m01):
    hi, mid, lo = _split3(x)
    f = lambda p: lax.dot_general(p, m01, (((0,), (1,)), ((), ())), preferred_element_type=F32)
    return f(hi) + (f(mid) + f(lo))


def _ssd_scan(xbc, proj, dt_col, dt_bias, a_log, n_ctx):
    bsz, t, _ = xbc.shape
    c = SSD_CHUNK
    pad = lambda v: jnp.zeros((1, LANES), F32).at[0, :MB_HEADS].set(v)
    hexp = np.zeros((LANES, MB_INNER), np.float32)
    for hd in range(MB_HEADS):
        hexp[hd, hd * MB_HEAD_DIM:(hd + 1) * MB_HEAD_DIM] = 1.0
    hexp = jnp.asarray(hexp, BF16)
    tri = np.tril(np.ones((c, c), np.float32))
    in_specs, args, outs = [], [], []
    for d in range(2):
        n_chunks, col, const, sel, _ = _scan_specs(c, n_ctx, t, d == 1)
        mq, mk = sel[0:c, 0:c], sel[c:2 * c, 0:c]
        mask = jnp.asarray(tri[::-1, ::-1].copy() if d == 1 else tri)
        bias, alog = pad(dt_bias[d]), pad(a_log[d].astype(F32))
        in_specs += [col(xbc.shape[-1], 0), col(LANES, dt_col + d), const(bias), const(alog),
                     const(mq), const(mk), const(mask)]
        args += [xbc, proj, bias, alog, mq, mk, mask]
        outs.append(col(MB_INNER, 0))
    return pl.pallas_call(
        _ssd_kernel,
        grid=(bsz, n_chunks),
        in_specs=in_specs + [const(hexp)],
        out_specs=outs,
        out_shape=[jax.ShapeDtypeStruct((bsz, t, MB_INNER), BF16)] * 2,
        scratch_shapes=[pltpu.VMEM((MB_GROUPS, MB_STATE, MB_INNER // MB_GROUPS), F32)] * 2,
        compiler_params=_params("parallel", "arbitrary"),
        name="ssd_scan",
    )(*args, hexp)


def _mix_out_cd_kernel(*refs):
    h_ref, hy_ref, yf_ref, yb_ref, xs_ref, z_ref, dsk_ref, ng_ref, w_ref, gt_ref = refs[0:10]
    pre_in, o_ref, pre_out = refs[10:10 + N_FFN_PRE_IN], refs[10 + N_FFN_PRE_IN], refs[11 + N_FFN_PRE_IN:]
    y = (yf_ref[...].astype(F32) + yb_ref[...].astype(F32) + dsk_ref[...] * xs_ref[...]) * _silu(z_ref[...])
    gw = MB_INNER // MB_GROUPS
    ys = [_rms(y[:, g * gw:(g + 1) * gw], ng_ref[:, g * gw:(g + 1) * gw]) for g in range(MB_GROUPS)]
    feat = jnp.concatenate([hy_ref[...]] + ys, axis=-1).astype(BF16)
    h_new = h_ref[...] + gt_ref[...] * _dot(feat, w_ref[...])
    o_ref[...] = h_new
    _ffn_pre_body(h_new, *pre_in, *pre_out)


def _mix_out_cd(h, hy, y_ssd, xbc, proj, z_col, d_skip_x, norm_g, w_out, gate, n_ctx, pre):
    bsz, t, d = h.shape
    tm = ROW_TILE
    n_lat = t - n_ctx
    off = n_ctx // tm
    row = lambda width, idx: pl.BlockSpec((None, tm, width), lambda b, i: (b, i + off, idx))
    vec = pl.BlockSpec((1, MB_INNER), lambda b, i: (0, 0))
    latent = lambda b, i: (b, 1, 0, 0)
    p_in, p_args, p_out, p_shape, p_scratch = _ffn_pre_parts(bsz, n_lat, d, *pre, latent)
    outs = pl.pallas_call(
        _mix_out_cd_kernel,
        grid=(bsz, n_lat // tm),
        in_specs=[row(d, 0), pl.BlockSpec((None, tm, HY_CH), lambda b, i: (b, i, 0)),
                  row(MB_INNER, 0), row(MB_INNER, 0), row(MB_INNER, 0), row(MB_INNER, z_col), vec, vec,
                  pl.BlockSpec(w_out.shape, lambda b, i: (0, 0)),
                  pl.BlockSpec((None, None, 1, d), latent)] + p_in,
        out_specs=[pl.BlockSpec((None, tm, d), lambda b, i: (b, i, 0))] + p_out,
        out_shape=[jax.ShapeDtypeStruct((bsz, n_lat, d), F32)] + p_shape,
        scratch_shapes=p_scratch,
        compiler_params=_params("arbitrary", "arbitrary"),
        name="mix_out_cd",
    )(h, hy, y_ssd[0], y_ssd[1], xbc, proj, d_skip_x, norm_g.reshape(1, -1), w_out, gate, *p_args)
    return outs[0], outs[1:]


def _top2_of4(a, b, c, d):
    hi1, lo1, hi2, lo2 = jnp.maximum(a, b), jnp.minimum(a, b), jnp.maximum(c, d), jnp.minimum(c, d)
    return jnp.maximum(hi1, hi2) + jnp.maximum(jnp.minimum(hi1, hi2), jnp.maximum(lo1, lo2))


def _first_argmax(vals, skip=None):
    idx = None
    for j, vj in enumerate(vals):
        if idx is None and skip is None:
            idx, best = jnp.zeros(vj.shape, jnp.int32), vj
            continue
        if idx is None:
            idx, best = jnp.full(vj.shape, -1, jnp.int32), jnp.full(vj.shape, -jnp.inf, F32)
        take = vj > best
        if skip is not None:
            take = jnp.logical_and(take, skip != j)
        idx = jnp.where(take, j, idx)
        best = jnp.where(take, vj, best)
    return idx, best


def _ffn_pre_body(h, g_ref, sh_ref, sc_ref, rw_ref, rb_ref, tri_ref, v_ref, ri_ref, rwt_ref, cnt_ref, carry_ref):
    @pl.when(jnp.logical_and(pl.program_id(0) == 0, pl.program_id(1) == 0))
    def _():
        carry_ref[...] = jnp.zeros_like(carry_ref)

    v = _rms(h, g_ref[...]) * (1.0 + sc_ref[...]) + sh_ref[...]
    v_ref[...] = _pack_bf16_pairs(v)
    st = _sigmoid(_dot_f32(v, rw_ref[...])).T[0:N_EXPERTS]
    sel = st + rb_ref[...]
    row = lambda a, e: a[e:e + 1]
    epg = EXPERTS_PER_GROUP
    gscore = [_top2_of4(*[row(sel, g * epg + j) for j in range(epg)]) for g in range(N_GROUPS)]
    best, _ = _first_argmax(gscore)

    def in_best(a, j):
        out = row(a, j)
        for g in range(1, N_GROUPS):
            out = jnp.where(best == g, row(a, g * epg + j), out)
        return out

    vals = [in_best(sel, j) for j in range(epg)]
    raw = [in_best(st, j) for j in range(epg)]
    i1, _ = _first_argmax(vals)
    i2, _ = _first_argmax(vals, skip=i1)
    pick = lambda i: functools.reduce(lambda acc, j: jnp.where(i == j, raw[j], acc), range(1, epg), raw[0])
    w1, w2 = pick(i1), pick(i2)
    wsum = w1 + w2
    e1, e2 = best * epg + i1, best * epg + i2

    experts = lax.broadcasted_iota(jnp.int32, st.shape, 0)
    oh1 = (experts == e1).astype(F32)
    oh2 = (experts == e2).astype(F32)
    cnt = oh1 + oh2
    before = _dot(cnt.astype(BF16), tri_ref[...]) + carry_ref[:, 0:1]
    ri_ref[0:1, :] = e1
    ri_ref[1:2, :] = e2
    ri_ref[2:3, :] = jnp.sum(oh1 * before, axis=0, keepdims=True).astype(jnp.int32)
    ri_ref[3:4, :] = jnp.sum(oh2 * before, axis=0, keepdims=True).astype(jnp.int32)
    ri_ref[4:8, :] = jnp.zeros((4, st.shape[1]), jnp.int32)
    lane_row = lax.broadcasted_iota(jnp.int32, (LANES, st.shape[1]), 0)
    rwt_ref[...] = jnp.where(lane_row == 0, w1 / wsum, jnp.where(lane_row == 1, w2 / wsum, 0.0)).T
    carry_ref[...] = carry_ref[...] + jnp.sum(cnt, axis=1, keepdims=True)
    cnt_ref[...] = carry_ref[...]


N_FFN_PRE_IN = 6


def _ffn_pre_parts(bsz, t, d, g, shift, scale, router_w_pad, router_b, seg):
    tm = ROW_TILE
    tri = jnp.asarray(np.triu(np.ones((tm, tm), np.float32), 1), BF16)
    in_specs = [pl.BlockSpec((1, d), lambda b, i: (0, 0)),
                pl.BlockSpec((None, None, 1, d), seg),
                pl.BlockSpec((None, None, 1, d), seg),
                pl.BlockSpec((d, LANES), lambda b, i: (0, 0)),
                pl.BlockSpec((N_EXPERTS, 1), lambda b, i: (0, 0)),
                pl.BlockSpec((tm, tm), lambda b, i: (0, 0))]
    args = (g.reshape(1, d), shift, scale, router_w_pad, router_b.reshape(N_EXPERTS, 1), tri)
    out_specs = [pl.BlockSpec((None, tm, d // 2), lambda b, i: (b, i, 0)),
                 pl.BlockSpec((None, 8, tm), lambda b, i: (b, 0, i)),
                 pl.BlockSpec((None, tm, LANES), lambda b, i: (b, i, 0)),
                 pl.BlockSpec((N_EXPERTS, LANES), lambda b, i: (0, 0))]
    out_shape = [jax.ShapeDtypeStruct((bsz, t, d // 2), jnp.uint32),
                 jax.ShapeDtypeStruct((bsz, 8, t), jnp.int32),
                 jax.ShapeDtypeStruct((bsz, t, LANES), F32),
                 jax.ShapeDtypeStruct((N_EXPERTS, LANES), F32)]
    return in_specs, args, out_specs, out_shape, [pltpu.VMEM((N_EXPERTS, LANES), F32)]


def _experts_kernel(be_ref, nb_ref, x_ref, wg_ref, wu_ref, wd_ref, o_ref, wg_s, wu_s, wd_s):
    i = pl.program_id(0)
    prev = be_ref[jnp.maximum(i - 1, 0)]
    changed = jnp.logical_or(i == 0, be_ref[i] != prev)

    @pl.when(changed)
    def _():
        wg_s[...] = wg_ref[...].astype(BF16)
        wu_s[...] = wu_ref[...].astype(BF16)
        wd_s[...] = wd_ref[...].astype(BF16)

    @pl.when(i < nb_ref[0])
    def _():
        x = _unpack_bf16_pairs(x_ref[...]).astype(BF16)
        hid = _silu(_dot(x, wg_s[...])) * _dot(x, wu_s[...])
        o_ref[...] = _pack_bf16_pairs(_dot(hid.astype(BF16), wd_s[...]))

    @pl.when(i >= nb_ref[0])
    def _():
        o_ref[...] = jnp.zeros_like(o_ref)


def _experts(xb, block_e, n_used, layer, w_gate, w_up, w_down):
    n_slots = xb.shape[0]
    n_blocks = n_slots // MOE_BLOCK
    d, de = w_gate.shape[-2:]
    wspec = lambda shape: pl.BlockSpec((None, None) + shape, lambda i, be, nb: (layer, be[i], 0, 0))
    return pl.pallas_call(
        _experts_kernel,
        grid_spec=pltpu.PrefetchScalarGridSpec(
            num_scalar_prefetch=2,
            grid=(n_blocks,),
            in_specs=[pl.BlockSpec((MOE_BLOCK, d // 2), lambda i, be, nb: (i, 0)),
                      wspec((d, de)), wspec((d, de)), wspec((de, d))],
            out_specs=pl.BlockSpec((MOE_BLOCK, d // 2), lambda i, be, nb: (i, 0)),
            scratch_shapes=[pltpu.VMEM((d, de), BF16), pltpu.VMEM((d, de), BF16), pltpu.VMEM((de, d), BF16)]),
        out_shape=jax.ShapeDtypeStruct((n_slots, d // 2), jnp.uint32),
        compiler_params=_params("arbitrary"),
        name="moe_experts",
    )(block_e, n_used, xb, w_gate, w_up, w_down)


def _ffn_post_kernel(h_ref, y0_ref, y1_ref, w_ref, gt_ref, g_ref, o_ref, *, final):
    w = w_ref[...]
    y = w[:, 0:1] * _unpack_bf16_pairs(y0_ref[...]) + w[:, 1:2] * _unpack_bf16_pairs(y1_ref[...])
    out = h_ref[...] + gt_ref[...] * y
    o_ref[...] = _rms(out, g_ref[...]) if final else out


def _ffn_post(h, y, w, gate, n_ctx, final_g=None):
    bsz, t, d = h.shape
    tm = ROW_TILE
    seg = lambda b, i: (b, (i * tm >= n_ctx).astype(jnp.int32), 0, 0)
    row = lambda width: pl.BlockSpec((None, tm, width), lambda b, i: (b, i, 0))
    choice = lambda kk: pl.BlockSpec((None, None, tm, d // 2), lambda b, i: (kk, b, i, 0))
    final = final_g is not None
    g = final_g if final else jnp.ones((d,), F32)
    return pl.pallas_call(
        functools.partial(_ffn_post_kernel, final=final),
        grid=(bsz, t // tm),
        in_specs=[row(d), choice(0), choice(1), row(LANES), pl.BlockSpec((None, None, 1, d), seg),
                  pl.BlockSpec((1, d), lambda b, i: (0, 0))],
        out_specs=row(d),
        out_shape=jax.ShapeDtypeStruct((bsz, t, d), F32),
        compiler_params=_params("parallel", "parallel"),
        name="ffn_post",
    )(h, y, y, w, gate, g.reshape(1, d))


def _slot_layout(n, ri, counts):
    e = jnp.swapaxes(ri[:, 0:2], 0, 1).reshape(TOP_K, n)
    rank = jnp.swapaxes(ri[:, 2:4], 0, 1).reshape(TOP_K, n)
    padded = (counts + MOE_BLOCK - 1) // MOE_BLOCK * MOE_BLOCK
    pend = jnp.cumsum(padded)
    pstart = pend - padded
    experts = jnp.arange(N_EXPERTS, dtype=jnp.int32)
    dest = rank + jnp.sum(jnp.where(e[..., None] == experts, pstart, 0), axis=-1)
    n_slots = (n * TOP_K + MOE_BLOCK - 1) // MOE_BLOCK * MOE_BLOCK + N_EXPERTS * MOE_BLOCK
    n_blocks = n_slots // MOE_BLOCK
    blk0 = jnp.arange(n_blocks, dtype=jnp.int32)[:, None] * MOE_BLOCK
    block_e = jnp.minimum(jnp.sum((pend[None, :] <= blk0).astype(jnp.int32), axis=-1), N_EXPERTS - 1)
    n_used = (pend[-1] // MOE_BLOCK).astype(jnp.int32).reshape(1)
    return dest, n_slots, block_e.astype(jnp.int32), n_used


SC_CORES, SC_SUBCORES = 2, 16
SC_WINDOW = 32


def _gather_rows(table, idx):
    n_rows, d = idx.shape[0], table.shape[1]
    workers = SC_CORES * SC_SUBCORES
    per_worker = n_rows // workers
    assert per_worker * workers == n_rows and per_worker % SC_WINDOW == 0
    mesh = plsc.VectorSubcoreMesh(core_axis_name="c", subcore_axis_name="s")

    @functools.partial(
        pl.kernel, mesh=mesh,
        out_type=jax.ShapeDtypeStruct((n_rows, d), table.dtype),
        scratch_types=[pltpu.VMEM((SC_WINDOW,), jnp.int32), pltpu.VMEM((SC_WINDOW,), jnp.int32),
                       pltpu.VMEM((SC_WINDOW, d), table.dtype), pltpu.VMEM((SC_WINDOW, d), table.dtype),
                       pltpu.SemaphoreType.DMA, pltpu.SemaphoreType.DMA],
    )
    def gather_kernel(table_hbm, idx_hbm, out_hbm, idx0, idx1, rows0, rows1, sem0, sem1):
        base = (lax.axis_index("s") * SC_CORES + lax.axis_index("c")) * per_worker
        n_win = per_worker // SC_WINDOW
        slots = ((idx0, rows0, sem0), (idx1, rows1, sem1))
        window = lambda j: pl.ds(pl.multiple_of(base + j * SC_WINDOW, 8), SC_WINDOW)

        def start(j, slot):
            idx_v, rows_v, sem = slots[slot]
            pltpu.sync_copy(idx_hbm.at[window(j)], idx_v)
            pltpu.async_copy(table_hbm.at[idx_v], rows_v, sem)

        def finish(j, slot):
            idx_v, rows_v, sem = slots[slot]
            pltpu.make_async_copy(table_hbm.at[idx_v], rows_v, sem).wait()
            pltpu.sync_copy(rows_v, out_hbm.at[window(j)])

        start(0, 0)

        @pl.loop(0, n_win, step=2)
        def _(j):
            @pl.when(j + 1 < n_win)
            def _():
                start(j + 1, 1)

            finish(j, 0)

            @pl.when(j + 2 < n_win)
            def _():
                start(j + 2, 0)

            @pl.when(j + 1 < n_win)
            def _():
                finish(j + 1, 1)

    return gather_kernel(table, idx)


SC_SCATTER_WINDOW = 16


def _scatter_rows(src, dest, n_slots):
    n, d = src.shape
    workers = SC_CORES * SC_SUBCORES
    per_worker = n // workers
    win = SC_SCATTER_WINDOW
    assert per_worker * workers == n and per_worker % win == 0 and dest.shape == (TOP_K, n)
    mesh = plsc.VectorSubcoreMesh(core_axis_name="c", subcore_axis_name="s")

    @functools.partial(
        pl.kernel, mesh=mesh,
        out_type=jax.ShapeDtypeStruct((n_slots, d), src.dtype),
        scratch_types=[pltpu.VMEM((win,), jnp.int32), pltpu.VMEM((win,), jnp.int32),
                       pltpu.VMEM((win, d), src.dtype), pltpu.SemaphoreType.DMA, pltpu.SemaphoreType.DMA],
    )
    def scatter_kernel(src_hbm, dest_hbm, out_hbm, idx0, idx1, rows_v, sem0, sem1):
        base = (lax.axis_index("s") * SC_CORES + lax.axis_index("c")) * per_worker

        @pl.loop(0, per_worker // win)
        def _(j):
            rows = pl.ds(pl.multiple_of(base + j * win, 8), win)
            pltpu.sync_copy(src_hbm.at[rows], rows_v)
            pltpu.sync_copy(dest_hbm.at[0, rows], idx0)
            pltpu.sync_copy(dest_hbm.at[1, rows], idx1)
            first = pltpu.async_copy(rows_v, out_hbm.at[idx0], sem0)
            second = pltpu.async_copy(rows_v, out_hbm.at[idx1], sem1)
            first.wait()
            second.wait()

    return scatter_kernel(src, dest)


def _alongside(gather, idx, side_fn, side_in):
    idx, side_in = lax.optimization_barrier((idx, side_in))
    return lax.optimization_barrier((gather(idx), side_fn(side_in)))


def _moe(h, pre_out, gate, layer, w_gate, w_up, w_down, n_ctx, final_g=None, side=None):
    bsz, t, d = h.shape
    n = bsz * t
    v, ri, rwt, counts = pre_out
    dest, n_slots, block_e, n_used = _slot_layout(n, ri, counts[:, 0].astype(jnp.int32))
    dispatch = lambda idx: _scatter_rows(v.reshape(n, d // 2), idx, n_slots)
    if side is None:
        xb = dispatch(dest)
    else:
        xb, side_a = _alongside(dispatch, dest, *side[0])
    yb = _experts(xb, block_e, n_used, layer, w_gate, w_up, w_down)
    combine = lambda idx: _gather_rows(yb, idx)
    dest_flat = dest.reshape(-1)
    if side is None:
        y, side_b = combine(dest_flat), None
    else:
        y, side_b = _alongside(combine, dest_flat, side[1], side_a)
    out = _ffn_post(h, y.reshape(TOP_K, bsz, t, d // 2), rwt, gate, n_ctx, final_g)
    return out if side is None else (out, side_b)


DFT_STEP = 16


def _dft_tables(n):
    r, *mats = _dft_tables_np(n)
    return (r,) + tuple(jnp.asarray(a).astype(BF16) for a in mats)


@functools.lru_cache(maxsize=None)
def _dft_tables_np(n):
    size = 2 * n
    r = int(round(math.sqrt(size)))
    assert r * r == size and r % DFT_STEP == 0
    p1 = np.arange(r // 2)[None, None, :]
    p2 = np.arange(r)[:, None, None]
    k1 = np.arange(r)[None, :, None]
    ang = 2.0 * np.pi * (((r * p1 + p2) * k1) % size) / size
    g_re, g_im = np.cos(ang), -np.sin(ang)
    g_in = np.concatenate([g_re, g_im], axis=1)
    g_out = np.concatenate([np.swapaxes(g_re, 1, 2), np.swapaxes(g_im, 1, 2)], axis=2) / size
    a2 = 2.0 * np.pi * ((np.arange(r)[:, None] * np.arange(r)[None, :]) % r) / r
    f_re, f_im = np.cos(a2), -np.sin(a2)
    f_fwd = np.block([[f_re, -f_im], [f_im, f_re]])
    f_inv = np.block([[f_re, f_im], [-f_im, f_re]])
    p1f = np.arange(r)[None, None, :]
    angf = 2.0 * np.pi * (((r * p1f + p2) * k1) % size) / size
    g_full = np.concatenate([np.cos(angf), -np.sin(angf)], axis=1)
    return (r,) + tuple(a.astype(np.float32) for a in (g_in, g_out, f_fwd, f_inv, g_full))


def _dot_f32_tn(a, b):
    ah = a.astype(BF16)
    al = (a - ah.astype(F32)).astype(BF16)
    bh = b.astype(BF16)
    bl = (b - bh.astype(F32)).astype(BF16)
    return _dot_tn(ah, bh) + (_dot_tn(ah, bl) + _dot_tn(al, bh))


def _hy_filter_kernel(z_ref, t_ref, w1_ref, b1_ref, w2_ref, b2_ref, fr_ref, w3_ref, w3b_ref, rates_ref, o_ref,
                      *, half_tiles):
    i = pl.program_id(0)
    hid = jnp.sin(fr_ref[...] * (_dot_f32(w1_ref[...], z_ref[...]) + b1_ref[...]))
    hid = jnp.sin(fr_ref[...] * (_dot_f32(w2_ref[...], hid) + b2_ref[...]))
    filt = _dot_f32_tn(hid, w3_ref[...])
    decay = jnp.exp(-t_ref[...] * rates_ref[...])
    for o in range(o_ref.shape[0]):
        o_ref[o] = filt[:, o * HY_CH:(o + 1) * HY_CH] * decay

    @pl.when(i == 0)
    def _():
        extra = _dot_f32_tn(hid[:, 0:LANES], w3b_ref[...])[0:8]
        first = lax.broadcasted_iota(jnp.int32, (8, HY_CH), 0) == 0
        for o in range(o_ref.shape[0]):
            add = extra[:, o * HY_CH:(o + 1) * HY_CH] * decay[0:8]
            o_ref[o, 0:8, :] = o_ref[o, 0:8, :] + jnp.where(first, add, 0.0)

    @pl.when(i == half_tiles)
    def _():
        for o in range(o_ref.shape[0]):
            o_ref[o, 0:1, :] = jnp.zeros((1, HY_CH), F32)


HY_TILE = 512


def _hy_kernels(n, w1, b1, w2, b2, w3, freq):
    pos = np.arange(2 * n)
    pos = np.where(pos < n, pos, 2 * n - pos).astype(np.float32)
    t = jnp.asarray(pos / np.float32(n - 1))
    bands = jnp.linspace(1e-4, HY_BANDS - 1, HY_BANDS, dtype=F32)
    ang = (2.0 * math.pi / n) * bands[:, None] * jnp.asarray(pos)[None, :]
    z = jnp.concatenate([t[None, :], jnp.cos(ang), -jnp.sin(ang)], axis=0)
    z = jnp.pad(z, ((0, LANES - z.shape[0]), (0, 0)))
    w1t = jnp.pad(w1, ((0, LANES - w1.shape[0]), (0, 0))).T
    hidden = w1.shape[1]
    col = lambda v: v.reshape(hidden, 1)
    w3d = jnp.swapaxes(w3.reshape(hidden, HY_ORDER, 2, HY_CH), 0, 2)
    w3d = jnp.swapaxes(w3d, 1, 2).reshape(2, hidden, HY_ORDER * HY_CH)
    rates = jnp.abs(jnp.linspace(HY_MIN_DECAY, HY_MAX_DECAY, HY_CH, dtype=F32)).reshape(1, HY_CH)
    tm = HY_TILE
    half_tiles = n // tm
    full = lambda a: pl.BlockSpec(a.shape, lambda i: (0,) * a.ndim)
    small = (w1t, col(b1), w2.T, col(b2), col(freq))
    return pl.pallas_call(
        functools.partial(_hy_filter_kernel, half_tiles=half_tiles),
        grid=(2 * n // tm,),
        in_specs=[pl.BlockSpec((LANES, tm), lambda i: (0, i)), pl.BlockSpec((tm, 1), lambda i: (i, 0))]
                 + [full(a) for a in small]
                 + [pl.BlockSpec((None, hidden, HY_ORDER * HY_CH), lambda i: ((i >= half_tiles).astype(jnp.int32), 0, 0)),
                    pl.BlockSpec((None, hidden, HY_ORDER * HY_CH), lambda i: (1, 0, 0)), full(rates)],
        out_specs=pl.BlockSpec((HY_ORDER, tm, HY_CH), lambda i: (0, i, 0)),
        out_shape=jax.ShapeDtypeStruct((HY_ORDER, 2 * n, HY_CH), F32),
        compiler_params=_params("parallel"),
        name="hy_kernels",
    )(z, t[:, None], *small, w3d, w3d, rates)


def _pack_complex(z):
    r = z.shape[0] // 2
    bits = lax.bitcast_convert_type(z.astype(BF16).astype(F32), jnp.uint32)
    return lax.bitcast_convert_type(bits[0:r] | (bits[r:2 * r] >> 16), F32)


def _unpack_complex(words):
    p = lax.bitcast_convert_type(words, jnp.uint32)
    re = lax.bitcast_convert_type(p & jnp.uint32(0xFFFF0000), F32)
    im = lax.bitcast_convert_type(p << 16, F32)
    return jnp.concatenate([re, im], axis=0).astype(BF16)


def _load_every(ref, j, count):
    return ref.reshape(count * DFT_STEP, LANES)[pl.ds(j, count, stride=DFT_STEP), :]


def _store_every(ref, j, count, val):
    ref.reshape(count * DFT_STEP, LANES)[pl.ds(j, count, stride=DFT_STEP), :] = val


def _dft_in_kernel(x_ref, g_ref, a_ref):
    n_seq, rh = x_ref.shape[0:2]
    for j in range(DFT_STEP):
        x = jnp.concatenate([_load_every(x_ref.at[b], j, rh) for b in range(n_seq)], axis=1).astype(BF16)
        res = _dot(g_ref[j], x)
        for b in range(n_seq):
            a_ref[b, j] = _pack_complex(res[:, b * LANES:(b + 1) * LANES])


def _dft_in(x4, col, g_in):
    bx, rh, r, _ = x4.shape
    c = HY_CH
    cbs = c // LANES
    return pl.pallas_call(
        _dft_in_kernel,
        grid=(r // DFT_STEP, cbs),
        in_specs=[pl.BlockSpec((bx, rh, DFT_STEP, LANES), lambda i, cb: (0, 0, i, col * cbs + cb)),
                  pl.BlockSpec((DFT_STEP, 2 * r, rh), lambda i, cb: (i, 0, 0))],
        out_specs=pl.BlockSpec((bx, DFT_STEP, r, LANES), lambda i, cb: (0, i, 0, cb)),
        out_shape=jax.ShapeDtypeStruct((bx, r, r, c), F32),
        compiler_params=_params("parallel", "parallel"),
        name="dft_in",
    )(x4, g_in)


def _stage2_operand(a_ref, j, r):
    return jnp.concatenate([_unpack_complex(_load_every(a_ref.at[b], j, r)) for b in range(a_ref.shape[0])],
                           axis=1)


def _dft_filt_kernel(a_ref, f_ref, k_ref):
    r = f_ref.shape[0] // 2
    for j in range(DFT_STEP):
        s = _dot(f_ref[...], _stage2_operand(a_ref, j, r))
        for o in range(a_ref.shape[0]):
            k_ref[o, j] = s[:, o * LANES:(o + 1) * LANES]


def _dft_filt(a, f_fwd):
    nq, r, _, c = a.shape
    return pl.pallas_call(
        _dft_filt_kernel,
        grid=(r // DFT_STEP, c // LANES),
        in_specs=[pl.BlockSpec((nq, r, DFT_STEP, LANES), lambda i, cb: (0, 0, i, cb)),
                  pl.BlockSpec(f_fwd.shape, lambda i, cb: (0, 0))],
        out_specs=pl.BlockSpec((nq, DFT_STEP, 2 * r, LANES), lambda i, cb: (0, i, 0, cb)),
        out_shape=jax.ShapeDtypeStruct((nq, r, 2 * r, c), F32),
        compiler_params=_params("parallel", "parallel"),
        name="dft_filt",
    )(a, f_fwd)


def _dft_mid_kernel(a_ref, k_ref, ff_ref, fi_ref, b_ref):
    r = ff_ref.shape[0] // 2
    n_seq = a_ref.shape[0]
    for j in range(DFT_STEP):
        s = _dot(ff_ref[...], _stage2_operand(a_ref, j, r))
        sr, si = s[0:r], s[r:2 * r]
        kr = jnp.concatenate([k_ref[j, 0:r, :]] * n_seq, axis=1)
        ki = jnp.concatenate([k_ref[j, r:2 * r, :]] * n_seq, axis=1)
        p = jnp.concatenate([sr * kr - si * ki, sr * ki + si * kr], axis=0).astype(BF16)
        back = _dot(fi_ref[...], p)
        for b in range(n_seq):
            b_ref[b, j] = _pack_complex(back[:, b * LANES:(b + 1) * LANES])


def _dft_mid(a, kspec, order, f_fwd, f_inv):
    bsz, r, _, c = a.shape
    return pl.pallas_call(
        _dft_mid_kernel,
        grid=(r // DFT_STEP, c // LANES),
        in_specs=[pl.BlockSpec((bsz, r, DFT_STEP, LANES), lambda i, cb: (0, 0, i, cb)),
                  pl.BlockSpec((None, DFT_STEP, 2 * r, LANES), lambda i, cb: (order, i, 0, cb)),
                  pl.BlockSpec(f_fwd.shape, lambda i, cb: (0, 0)),
                  pl.BlockSpec(f_inv.shape, lambda i, cb: (0, 0))],
        out_specs=pl.BlockSpec((bsz, DFT_STEP, r, LANES), lambda i, cb: (0, i, 0, cb)),
        out_shape=jax.ShapeDtypeStruct((bsz, r, r, c), F32),
        compiler_params=_params("parallel", "parallel"),
        name="dft_mid",
    )(a, kspec, f_fwd, f_inv)


def _dft_out_kernel(b_ref, g_ref, u_ref, x_ref, bias_ref, o_ref):
    n_seq, r = b_ref.shape[0:2]
    rh = o_ref.shape[1]
    for j in range(DFT_STEP):
        rhs = jnp.concatenate([_unpack_complex(_load_every(b_ref.at[b], j, r)) for b in range(n_seq)], axis=1)
        y = _dot(g_ref[j], rhs)
        for b in range(n_seq):
            yb = y[:, b * LANES:(b + 1) * LANES]
            _store_every(o_ref.at[b], j, rh,
                         _load_every(x_ref.at[b], j, rh) * (yb + _load_every(u_ref.at[b], j, rh) * bias_ref[...]))


def _dft_out(bm, g_out, u4, u_col, x4, x_col, bias):
    bsz, r, _, c = bm.shape
    rh = r // 2
    cbs = c // LANES
    seq = lambda col: pl.BlockSpec((bsz, rh, DFT_STEP, LANES), lambda i, cb: (0, 0, i, col * cbs + cb))
    return pl.pallas_call(
        _dft_out_kernel,
        grid=(r // DFT_STEP, cbs),
        in_specs=[pl.BlockSpec((bsz, r, DFT_STEP, LANES), lambda i, cb: (0, 0, i, cb)),
                  pl.BlockSpec((DFT_STEP, rh, 2 * r), lambda i, cb: (i, 0, 0)),
                  seq(u_col), seq(x_col), pl.BlockSpec((1, LANES), lambda i, cb: (0, cb))],
        out_specs=seq(0),
        out_shape=jax.ShapeDtypeStruct((bsz, rh, r, c), F32),
        compiler_params=_params("parallel", "parallel"),
        name="dft_out",
    )(bm, g_out, u4, x4, bias.reshape(1, c))


def _hyena_filter_stage1(n, filter_params):
    r, g_full = _dft_tables(n)[0], _dft_tables(n)[5]
    kern = _hy_kernels(n, *filter_params)
    return _dft_in(kern.reshape(-1, r, r, HY_CH), 0, g_full)


def _hyena_filter_spectra(n, stage1):
    return _dft_filt(stage1, _dft_tables(n)[3])


def _hyena(hy_in, kspec, conv_bias):
    bsz, n, _ = hy_in.shape
    r, g_in, g_out, f_fwd, f_inv, _ = _dft_tables(n)
    seq4 = hy_in.reshape(bsz, r // 2, r, 3 * HY_CH)
    zz = _dft_out(_dft_mid(_dft_in(seq4, 0, g_in), kspec, 0, f_fwd, f_inv), g_out,
                  seq4, 0, seq4, 1, conv_bias[0])
    out = _dft_out(_dft_mid(_dft_in(zz, 0, g_in), kspec, 1, f_fwd, f_inv), g_out,
                   zz, 0, seq4, 2, conv_bias[1])
    return out.reshape(bsz, n, HY_CH)


def _reorder_ab(w):
    gla_end = 2 * GLA_KEY_W + 2 * GLA_VAL_W
    lr_end = gla_end + 2 * GLA_LOW_RANK
    lr = jnp.pad(w[:, gla_end:lr_end], ((0, 0), (0, AB_PAD_COLS - AB_LR - 2 * GLA_LOW_RANK)))
    return [w[:, :gla_end].astype(BF16), w[:, lr_end:].astype(BF16), lr.astype(BF16)]


def _reorder_cd(w):
    hy_end = 3 * HY_CH
    z_end = hy_end + MB_INNER
    xbc_end = z_end + MB_INNER + 2 * MB_BC_W
    pad = lambda a: jnp.pad(a, ((0, 0), (0, LANES - MB_HEADS)))
    dt = jnp.concatenate([pad(w[:, xbc_end:xbc_end + MB_HEADS]), pad(w[:, xbc_end + MB_HEADS:])], axis=-1)
    return [w[:, :hy_end].astype(BF16), w[:, z_end:xbc_end].astype(BF16), w[:, hy_end:z_end].astype(BF16),
            dt.astype(BF16)]


def kernel(x, c, ctx, c_ctx, ada_w, ada_b, norm_mix_g, norm_ffn_g, norm_out_g, ab_w_in, ab_w_out, gla_gate_w, gla_gate_b, gla_norm_g, hg_lb, hg_norm_g, cd_w_in, cd_w_out, hy_short_w, hy_short_b, hy_w1, hy_b1, hy_w2, hy_b2, hy_w3, hy_freq, hy_bias, mb_conv_w, mb_conv_b, mb_dt_bias, mb_a_log, mb_d, mb_norm_g, router_w, router_b, moe_w_gate, moe_w_up, moe_w_down):
    bsz, n_lat, d = x.shape
    n_ctx = ctx.shape[1]
    t = n_ctx + n_lat
    assert ada_w.shape[0] == 2 and ab_w_in.shape[0] == 1 and cd_w_in.shape[0] == 1

    cond = jnp.zeros((8, d), F32).at[:bsz].set(c).at[bsz].set(c_ctx)
    m = _adaln(cond, ada_w, ada_b)

    def mods(layer):
        lat = m[layer, :bsz].reshape(bsz, 6, d)
        cx = jnp.broadcast_to(m[layer, bsz].reshape(1, 6, d), (bsz, 6, d))
        both = jnp.stack([cx, lat], axis=1)
        return [both[:, :, j][:, :, None, :] for j in range(6)]

    lb_all = jnp.cumsum(jax.nn.softmax(hg_lb.astype(F32), axis=1), axis=1)
    router_w_pad = jnp.zeros((d, LANES), F32).at[:, :N_EXPERTS].set(router_w)
    h = jnp.concatenate([ctx, x], axis=1)

    sh_m, sc_m, gt_m, sh_f, sc_f, gt_f = mods(0)
    proj = _norm_proj(h, norm_mix_g[0], sh_m, sc_m, _reorder_ab(ab_w_in[0]), n_ctx)
    gwp = [jnp.zeros((LANES, GLA_KEY_W), F32).at[GLA_LOW_RANK * dd:GLA_LOW_RANK * (dd + 1)].set(gla_gate_w[0, dd])
           for dd in range(2)]
    o_gla = _gla_scan(proj, gwp, [gla_gate_b[0, dd].reshape(1, -1) for dd in range(2)], n_ctx)
    o_hg = _hgrn_scan(proj, [lb_all[dd, 0].reshape(1, -1) for dd in range(2)], n_ctx)
    h, pre_out = _mix_out_ab(h, o_gla, o_hg, proj, gla_norm_g[0], hg_norm_g[0], ab_w_out[0].astype(BF16), gt_m,
                             n_ctx, (norm_ffn_g[0], sh_f, sc_f, router_w_pad, router_b))
    filter_params = (hy_w1[0], hy_b1[0], hy_w2[0], hy_b2[0], hy_w3[0], hy_freq[0])
    side = ((functools.partial(_hyena_filter_stage1, n_lat), filter_params),
            functools.partial(_hyena_filter_spectra, n_lat))
    h, kspec = _moe(h, pre_out, gt_f, 0, moe_w_gate, moe_w_up, moe_w_down, n_ctx, side=side)

    sh_m, sc_m, gt_m, sh_f, sc_f, gt_f = mods(1)
    conv_w = jnp.concatenate([hy_short_w[0], mb_conv_w[0]], axis=0).T
    conv_b = jnp.concatenate([hy_short_b[0], mb_conv_b[0]]).reshape(1, -1)
    hy_in, xbc, zdt = _norm_proj_conv(h, norm_mix_g[1], sh_m, sc_m, _reorder_cd(cd_w_in[0]),
                                      conv_w, conv_b, 3 * HY_CH, n_ctx)
    hy = _hyena(hy_in, kspec, hy_bias[0])
    y_ssd = _ssd_scan(xbc, zdt, MB_INNER // LANES, mb_dt_bias[0], mb_a_log[0], n_ctx)
    d_skip_x = jnp.repeat(mb_d[0], MB_HEAD_DIM).reshape(1, MB_INNER)
    h, pre_out = _mix_out_cd(h, hy, y_ssd, xbc, zdt, 0, d_skip_x, mb_norm_g[0], cd_w_out[0].astype(BF16), gt_m,
                             n_ctx, (norm_ffn_g[1], sh_f, sc_f, router_w_pad, router_b))
    return _moe(h, pre_out, gt_f, 1, moe_w_gate, moe_w_up, moe_w_down, 0, final_g=norm_out_g)
```

```python
import functools
import math

import numpy as np
import jax
import jax.numpy as jnp
from jax import lax
from jax.experimental import pallas as pl
from jax.experimental.pallas import tpu as pltpu
from jax.experimental.pallas import tpu_sc as plsc

NORM_EPS = 1e-6
GLA_HEADS, GLA_DK, GLA_DV, GLA_LOW_RANK, GLA_TAU = 4, 64, 128, 16, 16.0
GLA_KEY_W, GLA_VAL_W = GLA_HEADS * GLA_DK, GLA_HEADS * GLA_DV
HG_HEADS, HG_EXPAND, HG_DV = 4, 128, 128
HG_KEY_W, HG_VAL_W = HG_HEADS * HG_EXPAND, HG_HEADS * HG_DV
HY_CH, HY_ORDER, HY_SHORT, HY_BANDS, HY_FILT_HID = 512, 2, 3, 16, 64
HY_MIN_DECAY = math.log(1e-2) / 1.5
HY_MAX_DECAY = math.log(1e-2) / 0.3
MB_HEADS, MB_HEAD_DIM, MB_GROUPS, MB_STATE = 8, 64, 2, 128
MB_INNER = MB_HEADS * MB_HEAD_DIM
MB_BC_W = MB_GROUPS * MB_STATE
N_EXPERTS, N_GROUPS, TOP_K, MOE_BLOCK = 16, 4, 2, 256
EXPERTS_PER_GROUP = N_EXPERTS // N_GROUPS

LANES = 128
SCAN_CHUNK = 64
SCAN_BLOCK = 256
SSD_CHUNK = 128
ROW_TILE = 256
VMEM_LIMIT = 56 * 1024 * 1024

BF16 = jnp.bfloat16
F32 = jnp.float32


def _params(*sem):
    return pltpu.CompilerParams(dimension_semantics=sem, vmem_limit_bytes=VMEM_LIMIT)


def _split3(x):
    hi = x.astype(BF16)
    r1 = x - hi.astype(F32)
    mid = r1.astype(BF16)
    lo = (r1 - mid.astype(F32)).astype(BF16)
    return hi, mid, lo


def _dot(a, b):
    return jnp.dot(a, b, preferred_element_type=F32)


def _dot_nt(a, b):
    return lax.dot_general(a, b, (((1,), (1,)), ((), ())), preferred_element_type=F32)


def _dot_tn(a, b):
    return lax.dot_general(a, b, (((0,), (0,)), ((), ())), preferred_element_type=F32)


def _sel_dot(m01, x):
    hi, mid, lo = _split3(x)
    return _dot(m01, hi) + (_dot(m01, mid) + _dot(m01, lo))


def _dot_sel(x, m01):
    hi, mid, lo = _split3(x)
    return _dot(hi, m01) + (_dot(mid, m01) + _dot(lo, m01))


def _dot_f32(a, b):
    ah = a.astype(BF16)
    al = (a - ah.astype(F32)).astype(BF16)
    bh = b.astype(BF16)
    bl = (b - bh.astype(F32)).astype(BF16)
    return _dot(ah, bh) + (_dot(ah, bl) + _dot(al, bh))


def _silu(x):
    return x * (1.0 / (1.0 + jnp.exp(-x)))


def _sigmoid(x):
    return 1.0 / (1.0 + jnp.exp(-x))


def _softplus(x):
    return jnp.maximum(x, 0.0) + jnp.log(1.0 + jnp.exp(-jnp.abs(x)))


def _pack_bf16_pairs(x):
    bits = lax.bitcast_convert_type(x.astype(BF16).astype(F32), jnp.uint32)
    half = x.shape[1] // 2
    return bits[:, :half] | (bits[:, half:] >> 16)


def _unpack_bf16_pairs(p):
    hi = lax.bitcast_convert_type(p & jnp.uint32(0xFFFF0000), F32)
    lo = lax.bitcast_convert_type(p << 16, F32)
    return jnp.concatenate([hi, lo], axis=1)


def _rms(x, g):
    return x * lax.rsqrt(jnp.mean(x * x, axis=-1, keepdims=True) + NORM_EPS) * g


def _adaln_kernel(c_ref, w_ref, b_ref, o_ref):
    o_ref[...] = _dot_f32(_silu(c_ref[...]), w_ref[...]) + b_ref[...]


def _adaln(cond, w, b):
    n_l, d, n6 = w.shape
    tn = 1536
    return pl.pallas_call(
        _adaln_kernel,
        grid=(n_l, n6 // tn),
        in_specs=[pl.BlockSpec((8, d), lambda l, j: (0, 0)),
                  pl.BlockSpec((None, d, tn), lambda l, j: (l, 0, j)),
                  pl.BlockSpec((None, 1, tn), lambda l, j: (l, 0, j))],
        out_specs=pl.BlockSpec((None, 8, tn), lambda l, j: (l, 0, j)),
        out_shape=jax.ShapeDtypeStruct((n_l, 8, n6), F32),
        compiler_params=_params("parallel", "parallel"),
        name="adaln",
    )(cond, w, b.reshape(n_l, 1, n6))


def _project(u, w_refs):
    return jnp.concatenate([_dot(u, w_ref[...]) for w_ref in w_refs], axis=-1)


def _joint_rows_specs(ctx, x):
    tm = ROW_TILE
    ctx_tiles = ctx.shape[1] // tm
    d = ctx.shape[2]
    return [pl.BlockSpec((None, tm, d), lambda b, i: (b, jnp.minimum(i, ctx_tiles - 1), 0)),
            pl.BlockSpec((None, tm, d), lambda b, i: (b, jnp.maximum(i - ctx_tiles, 0), 0))], ctx_tiles


def _joint_rows(ctx_ref, x_ref, ctx_tiles):
    return jnp.where(pl.program_id(1) < ctx_tiles, ctx_ref[...], x_ref[...])


def _norm_proj_kernel(c_ref, x_ref, g_ref, sh_ref, sc_ref, *refs, ctx_tiles):
    u = _rms(_joint_rows(c_ref, x_ref, ctx_tiles), g_ref[...]) * (1.0 + sc_ref[...]) + sh_ref[...]
    refs[-1][...] = _project(u.astype(BF16), refs[:-1])


def _norm_proj(ctx, x, g, shift, scale, ws):
    bsz, n_ctx, d = ctx.shape
    t = n_ctx + x.shape[1]
    n = sum(w.shape[1] for w in ws)
    tm = ROW_TILE
    seg = lambda b, i: (b, (i * tm >= n_ctx).astype(jnp.int32), 0, 0)
    row_specs, ctx_tiles = _joint_rows_specs(ctx, x)
    return pl.pallas_call(
        functools.partial(_norm_proj_kernel, ctx_tiles=ctx_tiles),
        grid=(bsz, t // tm),
        in_specs=row_specs + [pl.BlockSpec((1, d), lambda b, i: (0, 0)),
                              pl.BlockSpec((None, None, 1, d), seg),
                              pl.BlockSpec((None, None, 1, d), seg)]
                 + [pl.BlockSpec(w.shape, lambda b, i: (0, 0)) for w in ws],
        out_specs=pl.BlockSpec((None, tm, n), lambda b, i: (b, i, 0)),
        out_shape=jax.ShapeDtypeStruct((bsz, t, n), F32),
        compiler_params=_params("parallel", "parallel"),
        name="norm_proj",
    )(ctx, x, g.reshape(1, d), shift, scale, *ws)


def _scan_constants(c, reverse):
    t = np.arange(c)[:, None]
    u = np.arange(c)[None, :]
    sels = [u <= t, u > t]
    masks = []
    m = c // 2
    while m >= 1:
        blk = t // (2 * m)
        upper_t = (t % (2 * m)) >= m
        r = blk * (2 * m) + m - 1
        s_blk = u // (2 * m)
        upper_s = (u % (2 * m)) >= m
        sels.append((upper_t & (u > r) & (u <= t)) | ((~upper_t) & (u > t) & (u <= r)))
        masks.append((blk == s_blk) & upper_t & (~upper_s))
        m //= 2
    masks.append(t == u)
    sel = np.stack(sels).astype(np.float32)
    msk = np.stack(masks).astype(np.float32)
    if reverse:
        sel = sel[:, ::-1, ::-1]
        msk = msk[:, ::-1, ::-1]
    return np.ascontiguousarray(sel.reshape(-1, c)), np.ascontiguousarray(msk)


def _chunk_order(i, n_ctx_chunks, n_chunks, reverse):
    if not reverse:
        return i
    return jnp.where(i < n_ctx_chunks, n_ctx_chunks - 1 - i, n_chunks - 1 - (i - n_ctx_chunks))


GROUP_KEYS = 256


def _decay_chunk(q, k, v, la, consts, st_ref, heads, dk, dv):
    sel_ref, mask_ref, hm_ref, hmb_ref, vm_ref = consts
    c = q.shape[0]
    n_lvl = mask_ref.shape[0] - 1
    hpg = GROUP_KEYS // dk
    cs = _dot(sel_ref[...], jnp.concatenate(_split3(la), axis=0))
    e_q = jnp.exp(cs[0:c])
    e_k = jnp.exp(cs[c:2 * c])
    e_tot = jnp.exp(jnp.sum(la, axis=0, keepdims=True))
    vb = v.astype(BF16)
    outs = []
    for g in range(heads // hpg):
        ks = slice(g * GROUP_KEYS, (g + 1) * GROUP_KEYS)
        vs = slice(g * hpg * dv, (g + 1) * hpg * dv)
        qg, kg = q[:, ks], k[:, ks]
        key_stack = lambda x: jnp.concatenate([x.astype(BF16) * hmb_ref[h] for h in range(hpg)], axis=0)
        att = mask_ref[n_lvl] * _dot_nt(qg.astype(BF16), key_stack(kg))
        for l in range(n_lvl):
            e = jnp.exp(cs[(2 + l) * c:(3 + l) * c, ks])
            att = att + mask_ref[l] * _dot_nt((qg * e).astype(BF16), key_stack(kg * e))
        v_blocks = jnp.concatenate([vb[:, vs] * vm_ref[h] for h in range(hpg)], axis=0)
        intra = _dot(att.astype(BF16), v_blocks)
        st = st_ref[g]
        q_stack = jnp.concatenate([(qg * e_q[:, ks]) * hm_ref[h] for h in range(hpg)], axis=0)
        inter = _dot_nt(q_stack.astype(BF16), st.astype(BF16))
        upd = _dot_tn(vb[:, vs], (kg * e_k[:, ks]).astype(BF16))
        new = st * e_tot[:, ks]
        for h in range(hpg):
            new = new + upd[h * dv:(h + 1) * dv] * hm_ref[h]
        st_ref[g] = new
        outs.append(intra + jnp.concatenate([inter[h * c:(h + 1) * c] for h in range(hpg)], axis=-1))
    return jnp.concatenate(outs, axis=-1)


def _log_sigmoid(x):
    return jnp.minimum(x, 0.0) - jnp.log(1.0 + jnp.exp(-jnp.abs(x)))


def _gla_kernel(*refs):
    ins, head_masks, (o_refs, st_refs) = (refs[0:8], refs[8:16]), refs[16:19], (refs[19:21], refs[21:23])

    @pl.when(pl.program_id(1) == 0)
    def _():
        for st_ref in st_refs:
            st_ref[...] = jnp.zeros_like(st_ref)

    for d, ((q_ref, k_ref, v_ref, lr_ref, gw_ref, gb_ref, sel_ref, mask_ref), o_ref, st_ref) in enumerate(
            zip(ins, o_refs, st_refs)):
        z = _dot_f32(lr_ref[...], gw_ref[...]) + gb_ref[...]
        la = _log_sigmoid(z) * (1.0 / GLA_TAU)
        q = q_ref[...] * (GLA_DK ** -0.5)
        k, v = k_ref[...], v_ref[...]
        for rows in _sub_chunks(q.shape[0], d == 1):
            o_ref[rows, :] = _decay_chunk(q[rows], k[rows], v[rows], la[rows], (sel_ref, mask_ref) + head_masks,
                                          st_ref, GLA_HEADS, GLA_DK, GLA_DV).astype(o_ref.dtype)


def _hgrn_kernel(*refs):
    ins, head_masks, (o_refs, st_refs) = (refs[0:6], refs[6:12]), refs[12:15], (refs[15:17], refs[17:19])

    @pl.when(pl.program_id(1) == 0)
    def _():
        for st_ref in st_refs:
            st_ref[...] = jnp.zeros_like(st_ref)

    for d, ((q_ref, f_ref, v_ref, lb_ref, sel_ref, mask_ref), o_ref, st_ref) in enumerate(
            zip(ins, o_refs, st_refs)):
        lb = lb_ref[...]
        f = lb + (1.0 - lb) * _sigmoid(f_ref[...])
        q, k, v, la = _silu(q_ref[...]), 1.0 - f, v_ref[...], jnp.log(f)
        for rows in _sub_chunks(q.shape[0], d == 1):
            o_ref[rows, :] = _decay_chunk(q[rows], k[rows], v[rows], la[rows], (sel_ref, mask_ref) + head_masks,
                                          st_ref, HG_HEADS, HG_EXPAND, HG_DV).astype(o_ref.dtype)


def _sub_chunks(rows, reverse):
    order = range(rows // SCAN_CHUNK)
    return [slice(j * SCAN_CHUNK, (j + 1) * SCAN_CHUNK) for j in (reversed(order) if reverse else order)]


def _scan_specs(blk, n_ctx, t, reverse, chunk=None, stacked_heads=1):
    n_blocks = t // blk
    order = functools.partial(_chunk_order, n_ctx_chunks=n_ctx // blk, n_chunks=n_blocks, reverse=reverse)

    def col(width, idx):
        return pl.BlockSpec((None, blk, width), lambda b, i: (b, order(i), idx))

    sel, msk = _scan_constants(chunk or blk, reverse)
    sel3 = np.concatenate([sel, sel, sel], axis=1)
    msk = np.tile(msk, (1, 1, stacked_heads))
    const = lambda a: pl.BlockSpec(a.shape, lambda b, i: (0,) * a.ndim)
    return n_blocks, col, const, jnp.asarray(sel3, BF16), jnp.asarray(msk, F32)


def _head_masks(dk, dv):
    hpg = GROUP_KEYS // dk
    hm = np.zeros((hpg, 1, GROUP_KEYS), np.float32)
    vm = np.zeros((hpg, 1, hpg * dv), np.float32)
    for h in range(hpg):
        hm[h, 0, h * dk:(h + 1) * dk] = 1.0
        vm[h, 0, h * dv:(h + 1) * dv] = 1.0
    return jnp.asarray(hm), jnp.asarray(hm, BF16), jnp.asarray(vm, BF16)


AB_Q, AB_K, AB_V, AB_G = 0, 256, 512, 1024
AB_HQ, AB_HF, AB_HI, AB_HG, AB_LR = 1536, 2048, 3072, 3584, 4096
AB_PAD_COLS = 4224


def _gla_scan(proj, gate_w_pad, gate_b, n_ctx):
    bsz, t, _ = proj.shape
    hpg = GROUP_KEYS // GLA_DK
    in_specs, args, outs = [], [], []
    for d in range(2):
        n_blocks, col, const, sel, msk = _scan_specs(SCAN_BLOCK, n_ctx, t, d == 1, SCAN_CHUNK, hpg)
        in_specs += [col(GLA_KEY_W, AB_Q // GLA_KEY_W), col(GLA_KEY_W, AB_K // GLA_KEY_W),
                     col(GLA_VAL_W, AB_V // GLA_VAL_W), col(LANES, AB_LR // LANES),
                     const(gate_w_pad[d]), const(gate_b[d]), const(sel), const(msk)]
        args += [proj, proj, proj, proj, gate_w_pad[d], gate_b[d], sel, msk]
        outs.append(col(GLA_VAL_W, 0))
    hm = _head_masks(GLA_DK, GLA_DV)
    return pl.pallas_call(
        _gla_kernel,
        grid=(bsz, n_blocks),
        in_specs=in_specs + [const(m) for m in hm],
        out_specs=outs,
        out_shape=[jax.ShapeDtypeStruct((bsz, t, GLA_VAL_W), BF16)] * 2,
        scratch_shapes=[pltpu.VMEM((GLA_HEADS // hpg, GLA_DV, GROUP_KEYS), F32)] * 2,
        compiler_params=_params("parallel", "arbitrary"),
        name="gla_scan",
    )(*args, *hm)


def _hgrn_scan(proj, lb, n_ctx):
    bsz, t, _ = proj.shape
    hpg = GROUP_KEYS // HG_EXPAND
    in_specs, args, outs = [], [], []
    for d in range(2):
        n_blocks, col, const, sel, msk = _scan_specs(SCAN_BLOCK, n_ctx, t, d == 1, SCAN_CHUNK, hpg)
        in_specs += [col(HG_KEY_W, AB_HQ // HG_KEY_W), col(HG_KEY_W, AB_HF // HG_KEY_W + d),
                     col(HG_VAL_W, AB_HI // HG_VAL_W), const(lb[d]), const(sel), const(msk)]
        args += [proj, proj, proj, lb[d], sel, msk]
        outs.append(col(HG_VAL_W, 0))
    hm = _head_masks(HG_EXPAND, HG_DV)
    return pl.pallas_call(
        _hgrn_kernel,
        grid=(bsz, n_blocks),
        in_specs=in_specs + [const(m) for m in hm],
        out_specs=outs,
        out_shape=[jax.ShapeDtypeStruct((bsz, t, HG_VAL_W), BF16)] * 2,
        scratch_shapes=[pltpu.VMEM((HG_HEADS // hpg, HG_DV, GROUP_KEYS), F32)] * 2,
        compiler_params=_params("parallel", "arbitrary"),
        name="hgrn_scan",
    )(*args, *hm)


def _mix_out_ab_kernel(*refs, ctx_tiles):
    (c_ref, x_ref, gf_ref, gb_ref, hf_ref, hb_ref, gg_ref, hg_ref, gn_ref, hn_ref, w_ref, gt_ref) = refs[0:12]
    pre_in, o_ref, pre_out = refs[12:12 + N_FFN_PRE_IN], refs[12 + N_FFN_PRE_IN], refs[13 + N_FFN_PRE_IN:]
    feats = []
    both = lambda fwd_ref, bwd_ref: fwd_ref[...].astype(F32) + bwd_ref[...].astype(F32)
    for o, gate, g in ((both(gf_ref, gb_ref), gg_ref[...], gn_ref[...]),
                       (both(hf_ref, hb_ref), hg_ref[...], hn_ref[...])):
        for hd in range(o.shape[-1] // LANES):
            s = slice(hd * LANES, (hd + 1) * LANES)
            feats.append(_rms(o[:, s], g) * _silu(gate[:, s]))
    feat = jnp.concatenate(feats, axis=-1).astype(BF16)
    h_new = _joint_rows(c_ref, x_ref, ctx_tiles) + gt_ref[...] * _dot(feat, w_ref[...])
    o_ref[...] = h_new
    _ffn_pre_body(h_new, *pre_in, *pre_out)


def _mix_out_ab(ctx, x, o_gla, o_hg, proj, gla_norm_g, hg_norm_g, w_out, gate, pre):
    bsz, n_ctx, d = ctx.shape
    t = n_ctx + x.shape[1]
    tm = ROW_TILE
    seg = lambda b, i: (b, (i * tm >= n_ctx).astype(jnp.int32), 0, 0)
    row = lambda width, idx: pl.BlockSpec((None, tm, width), lambda b, i: (b, i, idx))
    vec = pl.BlockSpec((1, LANES), lambda b, i: (0, 0))
    p_in, p_args, p_out, p_shape, p_scratch = _ffn_pre_parts(bsz, t, d, *pre, seg)
    row_specs, ctx_tiles = _joint_rows_specs(ctx, x)
    outs = pl.pallas_call(
        functools.partial(_mix_out_ab_kernel, ctx_tiles=ctx_tiles),
        grid=(bsz, t // tm),
        in_specs=row_specs + [row(GLA_VAL_W, 0), row(GLA_VAL_W, 0), row(HG_VAL_W, 0), row(HG_VAL_W, 0),
                  row(GLA_VAL_W, AB_G // GLA_VAL_W), row(HG_VAL_W, AB_HG // HG_VAL_W), vec, vec,
                  pl.BlockSpec(w_out.shape, lambda b, i: (0, 0)),
                  pl.BlockSpec((None, None, 1, d), seg)] + p_in,
        out_specs=[row(d, 0)] + p_out,
        out_shape=[jax.ShapeDtypeStruct((bsz, t, d), F32)] + p_shape,
        scratch_shapes=p_scratch,
        compiler_params=_params("arbitrary", "arbitrary"),
        name="mix_out_ab",
    )(ctx, x, o_gla[0], o_gla[1], o_hg[0], o_hg[1], proj, proj, gla_norm_g.reshape(1, -1),
      hg_norm_g.reshape(1, -1), w_out, gate, *p_args)
    return outs[0], outs[1:]


HALO = 8


def _norm_proj_conv_kernel(h_ref, hp_ref, hn_ref, g_ref, sh_ref, sc_ref, cw_ref, cb_ref, *refs,
                           ctx_tiles, n_tiles):
    w_refs, (hy_ref, xbc_ref, zdt_ref) = refs[:-3], refs[-3:]
    i = pl.program_id(1)
    tm = h_ref.shape[0]
    hh = jnp.concatenate([hp_ref[...], h_ref[...], hn_ref[...]], axis=0)
    u = _rms(hh, g_ref[...]) * (1.0 + sc_ref[...]) + sh_ref[...]
    p = _project(u.astype(BF16), w_refs)
    n_conv = cw_ref.shape[1]
    n_hy = hy_ref.shape[1]
    pc = p[:, 0:n_conv]
    rows_all = tm + 2 * HALO
    cur = pc[HALO:HALO + tm]
    prev = pltpu.roll(pc, 1, axis=0)[HALO:HALO + tm]
    nxt = pltpu.roll(pc, rows_all - 1, axis=0)[HALO:HALO + tm]
    first = jnp.logical_or(i == 0, i == ctx_tiles)
    last = jnp.logical_or(i == ctx_tiles - 1, i == n_tiles - 1)
    rows = lax.broadcasted_iota(jnp.int32, cur.shape, 0)
    prev = jnp.where(jnp.logical_and(first, rows == 0), 0.0, prev)
    nxt = jnp.where(jnp.logical_and(last, rows == tm - 1), 0.0, nxt)
    y = prev * cw_ref[0:1, :] + cur * cw_ref[1:2, :] + nxt * cw_ref[2:3, :] + cb_ref[...]
    hy_ref[...] = y[:, 0:n_hy]
    xbc_ref[...] = _silu(y[:, n_hy:n_conv])
    zdt_ref[...] = p[HALO:HALO + tm, n_conv:]


def _norm_proj_conv(h, g, shift, scale, ws, conv_w, conv_b, n_hy, n_ctx):
    bsz, t, d = h.shape
    n = sum(w.shape[1] for w in ws)
    n_conv = conv_w.shape[1]
    tm = ROW_TILE
    n_tiles, ctx_tiles = t // tm, n_ctx // tm
    r8 = tm // HALO
    last8 = t // HALO - 1
    seg = lambda b, i: (b, (i >= ctx_tiles).astype(jnp.int32), 0, 0)
    kern = functools.partial(_norm_proj_conv_kernel, ctx_tiles=ctx_tiles, n_tiles=n_tiles)
    return pl.pallas_call(
        kern,
        grid=(bsz, n_tiles),
        in_specs=[pl.BlockSpec((None, tm, d), lambda b, i: (b, i, 0)),
                  pl.BlockSpec((None, HALO, d), lambda b, i: (b, jnp.maximum(i * r8 - 1, 0), 0)),
                  pl.BlockSpec((None, HALO, d), lambda b, i: (b, jnp.minimum((i + 1) * r8, last8), 0)),
                  pl.BlockSpec((1, d), lambda b, i: (0, 0)),
                  pl.BlockSpec((None, None, 1, d), seg),
                  pl.BlockSpec((None, None, 1, d), seg),
                  pl.BlockSpec((3, n_conv), lambda b, i: (0, 0)),
                  pl.BlockSpec((1, n_conv), lambda b, i: (0, 0))]
                 + [pl.BlockSpec(w.shape, lambda b, i: (0, 0)) for w in ws],
        out_specs=[pl.BlockSpec((None, tm, n_hy), lambda b, i: (b, jnp.maximum(i - ctx_tiles, 0), 0)),
                   pl.BlockSpec((None, tm, n_conv - n_hy), lambda b, i: (b, i, 0)),
                   pl.BlockSpec((None, tm, n - n_conv), lambda b, i: (b, i, 0))],
        out_shape=[jax.ShapeDtypeStruct((bsz, t - n_ctx, n_hy), F32),
                   jax.ShapeDtypeStruct((bsz, t, n_conv - n_hy), F32),
                   jax.ShapeDtypeStruct((bsz, t, n - n_conv), F32)],
        compiler_params=_params("parallel", "arbitrary"),
        name="norm_proj_conv",
    )(h, h, h, g.reshape(1, d), shift, scale, conv_w, conv_b, *ws)


def _ssd_kernel(*refs):
    ins, hexp_ref, o_refs, st_refs = (refs[0:7], refs[7:14]), refs[14], refs[15:17], refs[17:19]

    @pl.when(pl.program_id(1) == 0)
    def _():
        for st_ref in st_refs:
            st_ref[...] = jnp.zeros_like(st_ref)

    for (xbc_ref, dt_ref, bias_ref, alog_ref, mq_ref, mk_ref, mask_ref), o_ref, st_ref in zip(ins, o_refs, st_refs):
        _ssd_chunk(xbc_ref, dt_ref, bias_ref, alog_ref, hexp_ref, mq_ref, mk_ref, mask_ref, o_ref, st_ref)


def _ssd_chunk(xbc_ref, dt_ref, bias_ref, alog_ref, hexp_ref, mq_ref, mk_ref, mask_ref, o_ref, st_ref):
    c = xbc_ref.shape[0]
    hpg = MB_HEADS // MB_GROUPS
    gw = hpg * MB_HEAD_DIM
    dt = _softplus(dt_ref[...] + bias_ref[...])
    la = -dt * jnp.exp(alog_ref[...])
    cq = _sel_dot(mq_ref[...], la)
    ck = _sel_dot(mk_ref[...], la)
    cq_t = _dot_nt_sel(la, mq_ref[...])
    hexp = hexp_ref[...]
    dt_x = _dot_sel(dt, hexp)
    eq_x = jnp.exp(_dot_sel(cq, hexp))
    ek_x = jnp.exp(_dot_sel(ck, hexp))
    etot_x = jnp.exp(_dot_sel(jnp.sum(la, axis=0, keepdims=True), hexp))
    xs = xbc_ref[:, 0:MB_INNER] * dt_x
    mask = mask_ref[...]
    outs = []
    for g in range(MB_GROUPS):
        bm = xbc_ref[:, MB_INNER + g * MB_STATE:MB_INNER + (g + 1) * MB_STATE].astype(BF16)
        cm = xbc_ref[:, MB_INNER + MB_BC_W + g * MB_STATE:MB_INNER + MB_BC_W + (g + 1) * MB_STATE].astype(BF16)
        cb = _dot_nt(cm, bm)
        st = st_ref[g]
        gs = slice(g * gw, (g + 1) * gw)
        y_inter = _dot(cm, st.astype(BF16)) * eq_x[:, gs]
        for r in range(hpg):
            hd = g * hpg + r
            diff = cq[:, hd:hd + 1] - cq_t[hd:hd + 1, :]
            w = cb * jnp.exp(jnp.where(mask > 0.0, diff, -jnp.inf))
            ps = slice(hd * MB_HEAD_DIM, (hd + 1) * MB_HEAD_DIM)
            outs.append(_dot(w.astype(BF16), xs[:, ps].astype(BF16))
                        + y_inter[:, r * MB_HEAD_DIM:(r + 1) * MB_HEAD_DIM])
        st_ref[g] = st * etot_x[:, gs] + _dot_tn(bm, (xs[:, gs] * ek_x[:, gs]).astype(BF16))
    o_ref[...] = jnp.concatenate(outs, axis=-1).astype(o_ref.dtype)


def _dot_nt_sel(x, m01):
    hi, mid, lo = _split3(x)
    f = lambda p: lax.dot_general(p, m01, (((0,), (1,)), ((), ())), preferred_element_type=F32)
    return f(hi) + (f(mid) + f(lo))


def _ssd_scan(xbc, proj, dt_col, dt_bias, a_log, n_ctx):
    bsz, t, _ = xbc.shape
    c = SSD_CHUNK
    pad = lambda v: jnp.zeros((1, LANES), F32).at[0, :MB_HEADS].set(v)
    hexp = np.zeros((LANES, MB_INNER), np.float32)
    for hd in range(MB_HEADS):
        hexp[hd, hd * MB_HEAD_DIM:(hd + 1) * MB_HEAD_DIM] = 1.0
    hexp = jnp.asarray(hexp, BF16)
    tri = np.tril(np.ones((c, c), np.float32))
    in_specs, args, outs = [], [], []
    for d in range(2):
        n_chunks, col, const, sel, _ = _scan_specs(c, n_ctx, t, d == 1)
        mq, mk = sel[0:c, 0:c], sel[c:2 * c, 0:c]
        mask = jnp.asarray(tri[::-1, ::-1].copy() if d == 1 else tri)
        bias, alog = pad(dt_bias[d]), pad(a_log[d].astype(F32))
        in_specs += [col(xbc.shape[-1], 0), col(LANES, dt_col + d), const(bias), const(alog),
                     const(mq), const(mk), const(mask)]
        args += [xbc, proj, bias, alog, mq, mk, mask]
        outs.append(col(MB_INNER, 0))
    return pl.pallas_call(
        _ssd_kernel,
        grid=(bsz, n_chunks),
        in_specs=in_specs + [const(hexp)],
        out_specs=outs,
        out_shape=[jax.ShapeDtypeStruct((bsz, t, MB_INNER), BF16)] * 2,
        scratch_shapes=[pltpu.VMEM((MB_GROUPS, MB_STATE, MB_INNER // MB_GROUPS), F32)] * 2,
        compiler_params=_params("parallel", "arbitrary"),
        name="ssd_scan",
    )(*args, hexp)


def _mix_out_cd_kernel(*refs):
    h_ref, hy_ref, yf_ref, yb_ref, xs_ref, z_ref, dsk_ref, ng_ref, w_ref, gt_ref = refs[0:10]
    pre_in, o_ref, pre_out = refs[10:10 + N_FFN_PRE_IN], refs[10 + N_FFN_PRE_IN], refs[11 + N_FFN_PRE_IN:]
    y = (yf_ref[...].astype(F32) + yb_ref[...].astype(F32) + dsk_ref[...] * xs_ref[...]) * _silu(z_ref[...])
    gw = MB_INNER // MB_GROUPS
    ys = [_rms(y[:, g * gw:(g + 1) * gw], ng_ref[:, g * gw:(g + 1) * gw]) for g in range(MB_GROUPS)]
    feat = jnp.concatenate([hy_ref[...]] + ys, axis=-1).astype(BF16)
    h_new = h_ref[...] + gt_ref[...] * _dot(feat, w_ref[...])
    o_ref[...] = h_new
    _ffn_pre_body(h_new, *pre_in, *pre_out)


def _mix_out_cd(h, hy, y_ssd, xbc, proj, z_col, d_skip_x, norm_g, w_out, gate, n_ctx, pre):
    bsz, t, d = h.shape
    tm = ROW_TILE
    n_lat = t - n_ctx
    off = n_ctx // tm
    row = lambda width, idx: pl.BlockSpec((None, tm, width), lambda b, i: (b, i + off, idx))
    vec = pl.BlockSpec((1, MB_INNER), lambda b, i: (0, 0))
    latent = lambda b, i: (b, 1, 0, 0)
    p_in, p_args, p_out, p_shape, p_scratch = _ffn_pre_parts(bsz, n_lat, d, *pre, latent)
    outs = pl.pallas_call(
        _mix_out_cd_kernel,
        grid=(bsz, n_lat // tm),
        in_specs=[row(d, 0), pl.BlockSpec((None, tm, HY_CH), lambda b, i: (b, i, 0)),
                  row(MB_INNER, 0), row(MB_INNER, 0), row(MB_INNER, 0), row(MB_INNER, z_col), vec, vec,
                  pl.BlockSpec(w_out.shape, lambda b, i: (0, 0)),
                  pl.BlockSpec((None, None, 1, d), latent)] + p_in,
        out_specs=[pl.BlockSpec((None, tm, d), lambda b, i: (b, i, 0))] + p_out,
        out_shape=[jax.ShapeDtypeStruct((bsz, n_lat, d), F32)] + p_shape,
        scratch_shapes=p_scratch,
        compiler_params=_params("arbitrary", "arbitrary"),
        name="mix_out_cd",
    )(h, hy, y_ssd[0], y_ssd[1], xbc, proj, d_skip_x, norm_g.reshape(1, -1), w_out, gate, *p_args)
    return outs[0], outs[1:]


def _top2_of4(a, b, c, d):
    hi1, lo1, hi2, lo2 = jnp.maximum(a, b), jnp.minimum(a, b), jnp.maximum(c, d), jnp.minimum(c, d)
    return jnp.maximum(hi1, hi2) + jnp.maximum(jnp.minimum(hi1, hi2), jnp.maximum(lo1, lo2))


def _first_argmax(vals, skip=None):
    idx = None
    for j, vj in enumerate(vals):
        if idx is None and skip is None:
            idx, best = jnp.zeros(vj.shape, jnp.int32), vj
            continue
        if idx is None:
            idx, best = jnp.full(vj.shape, -1, jnp.int32), jnp.full(vj.shape, -jnp.inf, F32)
        take = vj > best
        if skip is not None:
            take = jnp.logical_and(take, skip != j)
        idx = jnp.where(take, j, idx)
        best = jnp.where(take, vj, best)
    return idx, best


def _ffn_pre_body(h, g_ref, sh_ref, sc_ref, rw_ref, rb_ref, tri_ref, v_ref, ri_ref, rwt_ref, cnt_ref, carry_ref):
    @pl.when(jnp.logical_and(pl.program_id(0) == 0, pl.program_id(1) == 0))
    def _():
        carry_ref[...] = jnp.zeros_like(carry_ref)

    v = _rms(h, g_ref[...]) * (1.0 + sc_ref[...]) + sh_ref[...]
    v_ref[...] = _pack_bf16_pairs(v)
    st = _sigmoid(_dot_f32(v, rw_ref[...])).T[0:N_EXPERTS]
    sel = st + rb_ref[...]
    row = lambda a, e: a[e:e + 1]
    epg = EXPERTS_PER_GROUP
    gscore = [_top2_of4(*[row(sel, g * epg + j) for j in range(epg)]) for g in range(N_GROUPS)]
    best, _ = _first_argmax(gscore)

    def in_best(a, j):
        out = row(a, j)
        for g in range(1, N_GROUPS):
            out = jnp.where(best == g, row(a, g * epg + j), out)
        return out

    vals = [in_best(sel, j) for j in range(epg)]
    raw = [in_best(st, j) for j in range(epg)]
    i1, _ = _first_argmax(vals)
    i2, _ = _first_argmax(vals, skip=i1)
    pick = lambda i: functools.reduce(lambda acc, j: jnp.where(i == j, raw[j], acc), range(1, epg), raw[0])
    w1, w2 = pick(i1), pick(i2)
    wsum = w1 + w2
    e1, e2 = best * epg + i1, best * epg + i2

    experts = lax.broadcasted_iota(jnp.int32, st.shape, 0)
    oh1 = (experts == e1).astype(F32)
    oh2 = (experts == e2).astype(F32)
    cnt = oh1 + oh2
    before = _dot(cnt.astype(BF16), tri_ref[...]) + carry_ref[:, 0:1]
    ri_ref[0:1, :] = e1
    ri_ref[1:2, :] = e2
    ri_ref[2:3, :] = jnp.sum(oh1 * before, axis=0, keepdims=True).astype(jnp.int32)
    ri_ref[3:4, :] = jnp.sum(oh2 * before, axis=0, keepdims=True).astype(jnp.int32)
    ri_ref[4:8, :] = jnp.zeros((4, st.shape[1]), jnp.int32)
    lane_row = lax.broadcasted_iota(jnp.int32, (LANES, st.shape[1]), 0)
    rwt_ref[...] = jnp.where(lane_row == 0, w1 / wsum, jnp.where(lane_row == 1, w2 / wsum, 0.0)).T
    carry_ref[...] = carry_ref[...] + jnp.sum(cnt, axis=1, keepdims=True)
    cnt_ref[...] = carry_ref[...]


N_FFN_PRE_IN = 6


def _ffn_pre_parts(bsz, t, d, g, shift, scale, router_w_pad, router_b, seg):
    tm = ROW_TILE
    tri = jnp.asarray(np.triu(np.ones((tm, tm), np.float32), 1), BF16)
    in_specs = [pl.BlockSpec((1, d), lambda b, i: (0, 0)),
                pl.BlockSpec((None, None, 1, d), seg),
                pl.BlockSpec((None, None, 1, d), seg),
                pl.BlockSpec((d, LANES), lambda b, i: (0, 0)),
                pl.BlockSpec((N_EXPERTS, 1), lambda b, i: (0, 0)),
                pl.BlockSpec((tm, tm), lambda b, i: (0, 0))]
    args = (g.reshape(1, d), shift, scale, router_w_pad, router_b.reshape(N_EXPERTS, 1), tri)
    out_specs = [pl.BlockSpec((None, tm, d // 2), lambda b, i: (b, i, 0)),
                 pl.BlockSpec((None, 8, tm), lambda b, i: (b, 0, i)),
                 pl.BlockSpec((None, tm, LANES), lambda b, i: (b, i, 0)),
                 pl.BlockSpec((N_EXPERTS, LANES), lambda b, i: (0, 0))]
    out_shape = [jax.ShapeDtypeStruct((bsz, t, d // 2), jnp.uint32),
                 jax.ShapeDtypeStruct((bsz, 8, t), jnp.int32),
                 jax.ShapeDtypeStruct((bsz, t, LANES), F32),
                 jax.ShapeDtypeStruct((N_EXPERTS, LANES), F32)]
    return in_specs, args, out_specs, out_shape, [pltpu.VMEM((N_EXPERTS, LANES), F32)]


def _experts_kernel(be_ref, nb_ref, x_ref, wg_ref, wu_ref, wd_ref, o_ref, wg_s, wu_s, wd_s):
    i = pl.program_id(0)
    prev = be_ref[jnp.maximum(i - 1, 0)]
    changed = jnp.logical_or(i == 0, be_ref[i] != prev)

    @pl.when(changed)
    def _():
        wg_s[...] = wg_ref[...].astype(BF16)
        wu_s[...] = wu_ref[...].astype(BF16)
        wd_s[...] = wd_ref[...].astype(BF16)

    @pl.when(i < nb_ref[0])
    def _():
        x = _unpack_bf16_pairs(x_ref[...]).astype(BF16)
        hid = _silu(_dot(x, wg_s[...])) * _dot(x, wu_s[...])
        o_ref[...] = _pack_bf16_pairs(_dot(hid.astype(BF16), wd_s[...]))

    @pl.when(i >= nb_ref[0])
    def _():
        o_ref[...] = jnp.zeros_like(o_ref)


def _experts(xb, block_e, n_used, layer, w_gate, w_up, w_down):
    n_slots = xb.shape[0]
    n_blocks = n_slots // MOE_BLOCK
    d, de = w_gate.shape[-2:]
    wspec = lambda shape: pl.BlockSpec((None, None) + shape, lambda i, be, nb: (layer, be[i], 0, 0))
    return pl.pallas_call(
        _experts_kernel,
        grid_spec=pltpu.PrefetchScalarGridSpec(
            num_scalar_prefetch=2,
            grid=(n_blocks,),
            in_specs=[pl.BlockSpec((MOE_BLOCK, d // 2), lambda i, be, nb: (i, 0)),
                      wspec((d, de)), wspec((d, de)), wspec((de, d))],
            out_specs=pl.BlockSpec((MOE_BLOCK, d // 2), lambda i, be, nb: (i, 0)),
            scratch_shapes=[pltpu.VMEM((d, de), BF16), pltpu.VMEM((d, de), BF16), pltpu.VMEM((de, d), BF16)]),
        out_shape=jax.ShapeDtypeStruct((n_slots, d // 2), jnp.uint32),
        compiler_params=_params("arbitrary"),
        name="moe_experts",
    )(block_e, n_used, xb, w_gate, w_up, w_down)


def _ffn_post_kernel(h_ref, y0_ref, y1_ref, w_ref, gt_ref, g_ref, o_ref, *, final):
    w = w_ref[...]
    y = w[:, 0:1] * _unpack_bf16_pairs(y0_ref[...]) + w[:, 1:2] * _unpack_bf16_pairs(y1_ref[...])
    out = h_ref[...] + gt_ref[...] * y
    o_ref[...] = _rms(out, g_ref[...]) if final else out


def _ffn_post(h, y, w, gate, n_ctx, final_g=None):
    bsz, t, d = h.shape
    tm = ROW_TILE
    seg = lambda b, i: (b, (i * tm >= n_ctx).astype(jnp.int32), 0, 0)
    row = lambda width: pl.BlockSpec((None, tm, width), lambda b, i: (b, i, 0))
    choice = lambda kk: pl.BlockSpec((None, None, tm, d // 2), lambda b, i: (kk, b, i, 0))
    final = final_g is not None
    g = final_g if final else jnp.ones((d,), F32)
    return pl.pallas_call(
        functools.partial(_ffn_post_kernel, final=final),
        grid=(bsz, t // tm),
        in_specs=[row(d), choice(0), choice(1), row(LANES), pl.BlockSpec((None, None, 1, d), seg),
                  pl.BlockSpec((1, d), lambda b, i: (0, 0))],
        out_specs=row(d),
        out_shape=jax.ShapeDtypeStruct((bsz, t, d), F32),
        compiler_params=_params("parallel", "parallel"),
        name="ffn_post",
    )(h, y, y, w, gate, g.reshape(1, d))


def _slot_layout(n, ri, counts):
    e = jnp.swapaxes(ri[:, 0:2], 0, 1).reshape(TOP_K, n)
    rank = jnp.swapaxes(ri[:, 2:4], 0, 1).reshape(TOP_K, n)
    padded = (counts + MOE_BLOCK - 1) // MOE_BLOCK * MOE_BLOCK
    pend = jnp.cumsum(padded)
    pstart = pend - padded
    experts = jnp.arange(N_EXPERTS, dtype=jnp.int32)
    dest = rank + jnp.sum(jnp.where(e[..., None] == experts, pstart, 0), axis=-1)
    n_slots = (n * TOP_K + MOE_BLOCK - 1) // MOE_BLOCK * MOE_BLOCK + N_EXPERTS * MOE_BLOCK
    n_blocks = n_slots // MOE_BLOCK
    blk0 = jnp.arange(n_blocks, dtype=jnp.int32)[:, None] * MOE_BLOCK
    block_e = jnp.minimum(jnp.sum((pend[None, :] <= blk0).astype(jnp.int32), axis=-1), N_EXPERTS - 1)
    n_used = (pend[-1] // MOE_BLOCK).astype(jnp.int32).reshape(1)
    return dest, n_slots, block_e.astype(jnp.int32), n_used


SC_CORES, SC_SUBCORES = 2, 16
SC_WINDOW = 32


def _gather_rows(table, idx):
    n_rows, d = idx.shape[0], table.shape[1]
    workers = SC_CORES * SC_SUBCORES
    per_worker = n_rows // workers
    assert per_worker * workers == n_rows and per_worker % SC_WINDOW == 0
    mesh = plsc.VectorSubcoreMesh(core_axis_name="c", subcore_axis_name="s")

    @functools.partial(
        pl.kernel, mesh=mesh,
        out_type=jax.ShapeDtypeStruct((n_rows, d), table.dtype),
        scratch_types=[pltpu.VMEM((SC_WINDOW,), jnp.int32), pltpu.VMEM((SC_WINDOW,), jnp.int32),
                       pltpu.VMEM((SC_WINDOW, d), table.dtype), pltpu.VMEM((SC_WINDOW, d), table.dtype),
                       pltpu.SemaphoreType.DMA, pltpu.SemaphoreType.DMA],
    )
    def gather_kernel(table_hbm, idx_hbm, out_hbm, idx0, idx1, rows0, rows1, sem0, sem1):
        base = (lax.axis_index("s") * SC_CORES + lax.axis_index("c")) * per_worker
        n_win = per_worker // SC_WINDOW
        slots = ((idx0, rows0, sem0), (idx1, rows1, sem1))
        window = lambda j: pl.ds(pl.multiple_of(base + j * SC_WINDOW, 8), SC_WINDOW)

        def start(j, slot):
            idx_v, rows_v, sem = slots[slot]
            pltpu.sync_copy(idx_hbm.at[window(j)], idx_v)
            pltpu.async_copy(table_hbm.at[idx_v], rows_v, sem)

        def finish(j, slot):
            idx_v, rows_v, sem = slots[slot]
            pltpu.make_async_copy(table_hbm.at[idx_v], rows_v, sem).wait()
            pltpu.sync_copy(rows_v, out_hbm.at[window(j)])

        start(0, 0)

        @pl.loop(0, n_win, step=2)
        def _(j):
            @pl.when(j + 1 < n_win)
            def _():
                start(j + 1, 1)

            finish(j, 0)

            @pl.when(j + 2 < n_win)
            def _():
                start(j + 2, 0)

            @pl.when(j + 1 < n_win)
            def _():
                finish(j + 1, 1)

    return gather_kernel(table, idx)


SC_SCATTER_WINDOW = 16


def _scatter_rows(src, dest, n_slots):
    n, d = src.shape
    workers = SC_CORES * SC_SUBCORES
    per_worker = n // workers
    win = SC_SCATTER_WINDOW
    assert per_worker * workers == n and per_worker % win == 0 and dest.shape == (TOP_K, n)
    mesh = plsc.VectorSubcoreMesh(core_axis_name="c", subcore_axis_name="s")

    @functools.partial(
        pl.kernel, mesh=mesh,
        out_type=jax.ShapeDtypeStruct((n_slots, d), src.dtype),
        scratch_types=[pltpu.VMEM((win,), jnp.int32), pltpu.VMEM((win,), jnp.int32),
                       pltpu.VMEM((win, d), src.dtype), pltpu.SemaphoreType.DMA, pltpu.SemaphoreType.DMA],
    )
    def scatter_kernel(src_hbm, dest_hbm, out_hbm, idx0, idx1, rows_v, sem0, sem1):
        base = (lax.axis_index("s") * SC_CORES + lax.axis_index("c")) * per_worker

        @pl.loop(0, per_worker // win)
        def _(j):
            rows = pl.ds(pl.multiple_of(base + j * win, 8), win)
            pltpu.sync_copy(src_hbm.at[rows], rows_v)
            pltpu.sync_copy(dest_hbm.at[0, rows], idx0)
            pltpu.sync_copy(dest_hbm.at[1, rows], idx1)
            first = pltpu.async_copy(rows_v, out_hbm.at[idx0], sem0)
            second = pltpu.async_copy(rows_v, out_hbm.at[idx1], sem1)
            first.wait()
            second.wait()

    return scatter_kernel(src, dest)


def _alongside(gather, idx, side_fn, side_in):
    idx, side_in = lax.optimization_barrier((idx, side_in))
    return lax.optimization_barrier((gather(idx), side_fn(side_in)))


def _moe(h, pre_out, gate, layer, w_gate, w_up, w_down, n_ctx, final_g=None, side=None):
    bsz, t, d = h.shape
    n = bsz * t
    v, ri, rwt, counts = pre_out
    dest, n_slots, block_e, n_used = _slot_layout(n, ri, counts[:, 0].astype(jnp.int32))
    dispatch = lambda idx: _scatter_rows(v.reshape(n, d // 2), idx, n_slots)
    if side is None:
        xb = dispatch(dest)
    else:
        xb, side_a = _alongside(dispatch, dest, *side[0])
    yb = _experts(xb, block_e, n_used, layer, w_gate, w_up, w_down)
    combine = lambda idx: _gather_rows(yb, idx)
    dest_flat = dest.reshape(-1)
    if side is None:
        y, side_b = combine(dest_flat), None
    else:
        y, side_b = _alongside(combine, dest_flat, side[1], side_a)
    out = _ffn_post(h, y.reshape(TOP_K, bsz, t, d // 2), rwt, gate, n_ctx, final_g)
    return out if side is None else (out, side_b)


DFT_STEP = 16


def _dft_tables(n):
    r, *mats = _dft_tables_np(n)
    return (r,) + tuple(jnp.asarray(a).astype(BF16) for a in mats)


@functools.lru_cache(maxsize=None)
def _dft_tables_np(n):
    size = 2 * n
    r = int(round(math.sqrt(size)))
    assert r * r == size and r % DFT_STEP == 0
    p1 = np.arange(r // 2)[None, None, :]
    p2 = np.arange(r)[:, None, None]
    k1 = np.arange(r)[None, :, None]
    ang = 2.0 * np.pi * (((r * p1 + p2) * k1) % size) / size
    g_re, g_im = np.cos(ang), -np.sin(ang)
    g_in = np.concatenate([g_re, g_im], axis=1)
    g_out = np.concatenate([np.swapaxes(g_re, 1, 2), np.swapaxes(g_im, 1, 2)], axis=2) / size
    a2 = 2.0 * np.pi * ((np.arange(r)[:, None] * np.arange(r)[None, :]) % r) / r
    f_re, f_im = np.cos(a2), -np.sin(a2)
    f_fwd = np.block([[f_re, -f_im], [f_im, f_re]])
    f_inv = np.block([[f_re, f_im], [-f_im, f_re]])
    p1f = np.arange(r)[None, None, :]
    angf = 2.0 * np.pi * (((r * p1f + p2) * k1) % size) / size
    g_full = np.concatenate([np.cos(angf), -np.sin(angf)], axis=1)
    return (r,) + tuple(a.astype(np.float32) for a in (g_in, g_out, f_fwd, f_inv, g_full))


def _dot_f32_tn(a, b):
    ah = a.astype(BF16)
    al = (a - ah.astype(F32)).astype(BF16)
    bh = b.astype(BF16)
    bl = (b - bh.astype(F32)).astype(BF16)
    return _dot_tn(ah, bh) + (_dot_tn(ah, bl) + _dot_tn(al, bh))


def _hy_filter_kernel(z_ref, t_ref, w1_ref, b1_ref, w2_ref, b2_ref, fr_ref, w3_ref, w3b_ref, rates_ref, o_ref,
                      *, half_tiles):
    i = pl.program_id(0)
    hid = jnp.sin(fr_ref[...] * (_dot_f32(w1_ref[...], z_ref[...]) + b1_ref[...]))
    hid = jnp.sin(fr_ref[...] * (_dot_f32(w2_ref[...], hid) + b2_ref[...]))
    filt = _dot_f32_tn(hid, w3_ref[...])
    decay = jnp.exp(-t_ref[...] * rates_ref[...])
    for o in range(o_ref.shape[0]):
        o_ref[o] = filt[:, o * HY_CH:(o + 1) * HY_CH] * decay

    @pl.when(i == 0)
    def _():
        extra = _dot_f32_tn(hid[:, 0:LANES], w3b_ref[...])[0:8]
        first = lax.broadcasted_iota(jnp.int32, (8, HY_CH), 0) == 0
        for o in range(o_ref.shape[0]):
            add = extra[:, o * HY_CH:(o + 1) * HY_CH] * decay[0:8]
            o_ref[o, 0:8, :] = o_ref[o, 0:8, :] + jnp.where(first, add, 0.0)

    @pl.when(i == half_tiles)
    def _():
        for o in range(o_ref.shape[0]):
            o_ref[o, 0:1, :] = jnp.zeros((1, HY_CH), F32)


HY_TILE = 512


def _hy_kernels(n, w1, b1, w2, b2, w3, freq):
    pos = np.arange(2 * n)
    pos = np.where(pos < n, pos, 2 * n - pos).astype(np.float32)
    t = jnp.asarray(pos / np.float32(n - 1))
    bands = jnp.linspace(1e-4, HY_BANDS - 1, HY_BANDS, dtype=F32)
    ang = (2.0 * math.pi / n) * bands[:, None] * jnp.asarray(pos)[None, :]
    z = jnp.concatenate([t[None, :], jnp.cos(ang), -jnp.sin(ang)], axis=0)
    z = jnp.pad(z, ((0, LANES - z.shape[0]), (0, 0)))
    w1t = jnp.pad(w1, ((0, LANES - w1.shape[0]), (0, 0))).T
    hidden = w1.shape[1]
    col = lambda v: v.reshape(hidden, 1)
    w3d = jnp.swapaxes(w3.reshape(hidden, HY_ORDER, 2, HY_CH), 0, 2)
    w3d = jnp.swapaxes(w3d, 1, 2).reshape(2, hidden, HY_ORDER * HY_CH)
    rates = jnp.abs(jnp.linspace(HY_MIN_DECAY, HY_MAX_DECAY, HY_CH, dtype=F32)).reshape(1, HY_CH)
    tm = HY_TILE
    half_tiles = n // tm
    full = lambda a: pl.BlockSpec(a.shape, lambda i: (0,) * a.ndim)
    small = (w1t, col(b1), w2.T, col(b2), col(freq))
    return pl.pallas_call(
        functools.partial(_hy_filter_kernel, half_tiles=half_tiles),
        grid=(2 * n // tm,),
        in_specs=[pl.BlockSpec((LANES, tm), lambda i: (0, i)), pl.BlockSpec((tm, 1), lambda i: (i, 0))]
                 + [full(a) for a in small]
                 + [pl.BlockSpec((None, hidden, HY_ORDER * HY_CH), lambda i: ((i >= half_tiles).astype(jnp.int32), 0, 0)),
                    pl.BlockSpec((None, hidden, HY_ORDER * HY_CH), lambda i: (1, 0, 0)), full(rates)],
        out_specs=pl.BlockSpec((HY_ORDER, tm, HY_CH), lambda i: (0, i, 0)),
        out_shape=jax.ShapeDtypeStruct((HY_ORDER, 2 * n, HY_CH), F32),
        compiler_params=_params("parallel"),
        name="hy_kernels",
    )(z, t[:, None], *small, w3d, w3d, rates)


def _pack_complex(z):
    r = z.shape[0] // 2
    bits = lax.bitcast_convert_type(z.astype(BF16).astype(F32), jnp.uint32)
    return lax.bitcast_convert_type(bits[0:r] | (bits[r:2 * r] >> 16), F32)


def _unpack_complex(words):
    p = lax.bitcast_convert_type(words, jnp.uint32)
    re = lax.bitcast_convert_type(p & jnp.uint32(0xFFFF0000), F32)
    im = lax.bitcast_convert_type(p << 16, F32)
    return jnp.concatenate([re, im], axis=0).astype(BF16)


def _load_every(ref, j, count):
    return ref.reshape(count * DFT_STEP, LANES)[pl.ds(j, count, stride=DFT_STEP), :]


def _store_every(ref, j, count, val):
    ref.reshape(count * DFT_STEP, LANES)[pl.ds(j, count, stride=DFT_STEP), :] = val


def _dft_in_kernel(x_ref, g_ref, a_ref):
    n_seq, rh = x_ref.shape[0:2]
    for j in range(DFT_STEP):
        x = jnp.concatenate([_load_every(x_ref.at[b], j, rh) for b in range(n_seq)], axis=1).astype(BF16)
        res = _dot(g_ref[j], x)
        for b in range(n_seq):
            a_ref[b, j] = _pack_complex(res[:, b * LANES:(b + 1) * LANES])


def _dft_in(x4, col, g_in):
    bx, rh, r, _ = x4.shape
    c = HY_CH
    cbs = c // LANES
    return pl.pallas_call(
        _dft_in_kernel,
        grid=(r // DFT_STEP, cbs),
        in_specs=[pl.BlockSpec((bx, rh, DFT_STEP, LANES), lambda i, cb: (0, 0, i, col * cbs + cb)),
                  pl.BlockSpec((DFT_STEP, 2 * r, rh), lambda i, cb: (i, 0, 0))],
        out_specs=pl.BlockSpec((bx, DFT_STEP, r, LANES), lambda i, cb: (0, i, 0, cb)),
        out_shape=jax.ShapeDtypeStruct((bx, r, r, c), F32),
        compiler_params=_params("parallel", "parallel"),
        name="dft_in",
    )(x4, g_in)


def _stage2_operand(a_ref, j, r):
    return jnp.concatenate([_unpack_complex(_load_every(a_ref.at[b], j, r)) for b in range(a_ref.shape[0])],
                           axis=1)


def _dft_filt_kernel(a_ref, f_ref, k_ref):
    r = f_ref.shape[0] // 2
    for j in range(DFT_STEP):
        s = _dot(f_ref[...], _stage2_operand(a_ref, j, r))
        for o in range(a_ref.shape[0]):
            k_ref[o, j] = s[:, o * LANES:(o + 1) * LANES]


def _dft_filt(a, f_fwd):
    nq, r, _, c = a.shape
    return pl.pallas_call(
        _dft_filt_kernel,
        grid=(r // DFT_STEP, c // LANES),
        in_specs=[pl.BlockSpec((nq, r, DFT_STEP, LANES), lambda i, cb: (0, 0, i, cb)),
                  pl.BlockSpec(f_fwd.shape, lambda i, cb: (0, 0))],
        out_specs=pl.BlockSpec((nq, DFT_STEP, 2 * r, LANES), lambda i, cb: (0, i, 0, cb)),
        out_shape=jax.ShapeDtypeStruct((nq, r, 2 * r, c), F32),
        compiler_params=_params("parallel", "parallel"),
        name="dft_filt",
    )(a, f_fwd)


def _dft_mid_kernel(a_ref, k_ref, ff_ref, fi_ref, b_ref):
    r = ff_ref.shape[0] // 2
    n_seq = a_ref.shape[0]
    for j in range(DFT_STEP):
        s = _dot(ff_ref[...], _stage2_operand(a_ref, j, r))
        sr, si = s[0:r], s[r:2 * r]
        kr = jnp.concatenate([k_ref[j, 0:r, :]] * n_seq, axis=1)
        ki = jnp.concatenate([k_ref[j, r:2 * r, :]] * n_seq, axis=1)
        p = jnp.concatenate([sr * kr - si * ki, sr * ki + si * kr], axis=0).astype(BF16)
        back = _dot(fi_ref[...], p)
        for b in range(n_seq):
            b_ref[b, j] = _pack_complex(back[:, b * LANES:(b + 1) * LANES])


def _dft_mid(a, kspec, order, f_fwd, f_inv):
    bsz, r, _, c = a.shape
    return pl.pallas_call(
        _dft_mid_kernel,
        grid=(r // DFT_STEP, c // LANES),
        in_specs=[pl.BlockSpec((bsz, r, DFT_STEP, LANES), lambda i, cb: (0, 0, i, cb)),
                  pl.BlockSpec((None, DFT_STEP, 2 * r, LANES), lambda i, cb: (order, i, 0, cb)),
                  pl.BlockSpec(f_fwd.shape, lambda i, cb: (0, 0)),
                  pl.BlockSpec(f_inv.shape, lambda i, cb: (0, 0))],
        out_specs=pl.BlockSpec((bsz, DFT_STEP, r, LANES), lambda i, cb: (0, i, 0, cb)),
        out_shape=jax.ShapeDtypeStruct((bsz, r, r, c), F32),
        compiler_params=_params("parallel", "parallel"),
        name="dft_mid",
    )(a, kspec, f_fwd, f_inv)


def _dft_out_kernel(b_ref, g_ref, u_ref, x_ref, bias_ref, o_ref):
    n_seq, r = b_ref.shape[0:2]
    rh = o_ref.shape[1]
    for j in range(DFT_STEP):
        rhs = jnp.concatenate([_unpack_complex(_load_every(b_ref.at[b], j, r)) for b in range(n_seq)], axis=1)
        y = _dot(g_ref[j], rhs)
        for b in range(n_seq):
            yb = y[:, b * LANES:(b + 1) * LANES]
            _store_every(o_ref.at[b], j, rh,
                         _load_every(x_ref.at[b], j, rh) * (yb + _load_every(u_ref.at[b], j, rh) * bias_ref[...]))


def _dft_out(bm, g_out, u4, u_col, x4, x_col, bias):
    bsz, r, _, c = bm.shape
    rh = r // 2
    cbs = c // LANES
    seq = lambda col: pl.BlockSpec((bsz, rh, DFT_STEP, LANES), lambda i, cb: (0, 0, i, col * cbs + cb))
    return pl.pallas_call(
        _dft_out_kernel,
        grid=(r // DFT_STEP, cbs),
        in_specs=[pl.BlockSpec((bsz, r, DFT_STEP, LANES), lambda i, cb: (0, 0, i, cb)),
                  pl.BlockSpec((DFT_STEP, rh, 2 * r), lambda i, cb: (i, 0, 0)),
                  seq(u_col), seq(x_col), pl.BlockSpec((1, LANES), lambda i, cb: (0, cb))],
        out_specs=seq(0),
        out_shape=jax.ShapeDtypeStruct((bsz, rh, r, c), F32),
        compiler_params=_params("parallel", "parallel"),
        name="dft_out",
    )(bm, g_out, u4, x4, bias.reshape(1, c))


def _hyena_filter_stage1(n, filter_params):
    r, g_full = _dft_tables(n)[0], _dft_tables(n)[5]
    kern = _hy_kernels(n, *filter_params)
    return _dft_in(kern.reshape(-1, r, r, HY_CH), 0, g_full)


def _hyena_filter_spectra(n, stage1):
    return _dft_filt(stage1, _dft_tables(n)[3])


def _hyena(hy_in, kspec, conv_bias):
    bsz, n, _ = hy_in.shape
    r, g_in, g_out, f_fwd, f_inv, _ = _dft_tables(n)
    seq4 = hy_in.reshape(bsz, r // 2, r, 3 * HY_CH)
    zz = _dft_out(_dft_mid(_dft_in(seq4, 0, g_in), kspec, 0, f_fwd, f_inv), g_out,
                  seq4, 0, seq4, 1, conv_bias[0])
    out = _dft_out(_dft_mid(_dft_in(zz, 0, g_in), kspec, 1, f_fwd, f_inv), g_out,
                   zz, 0, seq4, 2, conv_bias[1])
    return out.reshape(bsz, n, HY_CH)


def _reorder_ab(w):
    gla_end = 2 * GLA_KEY_W + 2 * GLA_VAL_W
    lr_end = gla_end + 2 * GLA_LOW_RANK
    lr = jnp.pad(w[:, gla_end:lr_end], ((0, 0), (0, AB_PAD_COLS - AB_LR - 2 * GLA_LOW_RANK)))
    return [w[:, :gla_end].astype(BF16), w[:, lr_end:].astype(BF16), lr.astype(BF16)]


def _reorder_cd(w):
    hy_end = 3 * HY_CH
    z_end = hy_end + MB_INNER
    xbc_end = z_end + MB_INNER + 2 * MB_BC_W
    pad = lambda a: jnp.pad(a, ((0, 0), (0, LANES - MB_HEADS)))
    dt = jnp.concatenate([pad(w[:, xbc_end:xbc_end + MB_HEADS]), pad(w[:, xbc_end + MB_HEADS:])], axis=-1)
    return [w[:, :hy_end].astype(BF16), w[:, z_end:xbc_end].astype(BF16), w[:, hy_end:z_end].astype(BF16),
            dt.astype(BF16)]


def kernel(x, c, ctx, c_ctx, ada_w, ada_b, norm_mix_g, norm_ffn_g, norm_out_g, ab_w_in, ab_w_out, gla_gate_w, gla_gate_b, gla_norm_g, hg_lb, hg_norm_g, cd_w_in, cd_w_out, hy_short_w, hy_short_b, hy_w1, hy_b1, hy_w2, hy_b2, hy_w3, hy_freq, hy_bias, mb_conv_w, mb_conv_b, mb_dt_bias, mb_a_log, mb_d, mb_norm_g, router_w, router_b, moe_w_gate, moe_w_up, moe_w_down):
    bsz, n_lat, d = x.shape
    n_ctx = ctx.shape[1]
    t = n_ctx + n_lat
    assert ada_w.shape[0] == 2 and ab_w_in.shape[0] == 1 and cd_w_in.shape[0] == 1

    cond = jnp.zeros((8, d), F32).at[:bsz].set(c).at[bsz].set(c_ctx)
    m = _adaln(cond, ada_w, ada_b)

    def mods(layer):
        lat = m[layer, :bsz].reshape(bsz, 6, d)
        cx = jnp.broadcast_to(m[layer, bsz].reshape(1, 6, d), (bsz, 6, d))
        both = jnp.stack([cx, lat], axis=1)
        return [both[:, :, j][:, :, None, :] for j in range(6)]

    lb_all = jnp.cumsum(jax.nn.softmax(hg_lb.astype(F32), axis=1), axis=1)
    router_w_pad = jnp.zeros((d, LANES), F32).at[:, :N_EXPERTS].set(router_w)

    sh_m, sc_m, gt_m, sh_f, sc_f, gt_f = mods(0)
    proj = _norm_proj(ctx, x, norm_mix_g[0], sh_m, sc_m, _reorder_ab(ab_w_in[0]))
    gwp = [jnp.zeros((LANES, GLA_KEY_W), F32).at[GLA_LOW_RANK * dd:GLA_LOW_RANK * (dd + 1)].set(gla_gate_w[0, dd])
           for dd in range(2)]
    o_gla = _gla_scan(proj, gwp, [gla_gate_b[0, dd].reshape(1, -1) for dd in range(2)], n_ctx)
    o_hg = _hgrn_scan(proj, [lb_all[dd, 0].reshape(1, -1) for dd in range(2)], n_ctx)
    h, pre_out = _mix_out_ab(ctx, x, o_gla, o_hg, proj, gla_norm_g[0], hg_norm_g[0], ab_w_out[0].astype(BF16),
                             gt_m, (norm_ffn_g[0], sh_f, sc_f, router_w_pad, router_b))
    filter_params = (hy_w1[0], hy_b1[0], hy_w2[0], hy_b2[0], hy_w3[0], hy_freq[0])
    side = ((functools.partial(_hyena_filter_stage1, n_lat), filter_params),
            functools.partial(_hyena_filter_spectra, n_lat))
    h, kspec = _moe(h, pre_out, gt_f, 0, moe_w_gate, moe_w_up, moe_w_down, n_ctx, side=side)

    sh_m, sc_m, gt_m, sh_f, sc_f, gt_f = mods(1)
    conv_w = jnp.concatenate([hy_short_w[0], mb_conv_w[0]], axis=0).T
    conv_b = jnp.concatenate([hy_short_b[0], mb_conv_b[0]]).reshape(1, -1)
    hy_in, xbc, zdt = _norm_proj_conv(h, norm_mix_g[1], sh_m, sc_m, _reorder_cd(cd_w_in[0]),
                                      conv_w, conv_b, 3 * HY_CH, n_ctx)
    hy = _hyena(hy_in, kspec, hy_bias[0])
    y_ssd = _ssd_scan(xbc, zdt, MB_INNER // LANES, mb_dt_bias[0], mb_a_log[0], n_ctx)
    d_skip_x = jnp.repeat(mb_d[0], MB_HEAD_DIM).reshape(1, MB_INNER)
    h, pre_out = _mix_out_cd(h, hy, y_ssd, xbc, zdt, 0, d_skip_x, mb_norm_g[0], cd_w_out[0].astype(BF16), gt_m,
                             n_ctx, (norm_ffn_g[1], sh_f, sc_f, router_w_pad, router_b))
    return _moe(h, pre_out, gt_f, 1, moe_w_gate, moe_w_up, moe_w_down, 0, final_g=norm_out_g)
```

```python
import functools
import math

import numpy as np
import jax
import jax.numpy as jnp
from jax import lax
from jax.experimental import pallas as pl
from jax.experimental.pallas import tpu as pltpu
from jax.experimental.pallas import tpu_sc as plsc

NORM_EPS = 1e-6
GLA_HEADS, GLA_DK, GLA_DV, GLA_LOW_RANK, GLA_TAU = 4, 64, 128, 16, 16.0
GLA_KEY_W, GLA_VAL_W = GLA_HEADS * GLA_DK, GLA_HEADS * GLA_DV
HG_HEADS, HG_EXPAND, HG_DV = 4, 128, 128
HG_KEY_W, HG_VAL_W = HG_HEADS * HG_EXPAND, HG_HEADS * HG_DV
HY_CH, HY_ORDER, HY_SHORT, HY_BANDS, HY_FILT_HID = 512, 2, 3, 16, 64
HY_MIN_DECAY = math.log(1e-2) / 1.5
HY_MAX_DECAY = math.log(1e-2) / 0.3
MB_HEADS, MB_HEAD_DIM, MB_GROUPS, MB_STATE = 8, 64, 2, 128
MB_INNER = MB_HEADS * MB_HEAD_DIM
MB_BC_W = MB_GROUPS * MB_STATE
N_EXPERTS, N_GROUPS, TOP_K, MOE_BLOCK = 16, 4, 2, 256
EXPERTS_PER_GROUP = N_EXPERTS // N_GROUPS

LANES = 128
SCAN_CHUNK = 64
SCAN_BLOCK = 256
SSD_CHUNK = 128
SSD_BLOCK = 256
ROW_TILE = 256
VMEM_LIMIT = 56 * 1024 * 1024

BF16 = jnp.bfloat16
F32 = jnp.float32


def _params(*sem):
    return pltpu.CompilerParams(dimension_semantics=sem, vmem_limit_bytes=VMEM_LIMIT)


def _split3(x):
    hi = x.astype(BF16)
    r1 = x - hi.astype(F32)
    mid = r1.astype(BF16)
    lo = (r1 - mid.astype(F32)).astype(BF16)
    return hi, mid, lo


def _dot(a, b):
    return jnp.dot(a, b, preferred_element_type=F32)


def _dot_nt(a, b):
    return lax.dot_general(a, b, (((1,), (1,)), ((), ())), preferred_element_type=F32)


def _dot_tn(a, b):
    return lax.dot_general(a, b, (((0,), (0,)), ((), ())), preferred_element_type=F32)


def _dot_f32(a, b):
    ah = a.astype(BF16)
    al = (a - ah.astype(F32)).astype(BF16)
    bh = b.astype(BF16)
    bl = (b - bh.astype(F32)).astype(BF16)
    return _dot(ah, bh) + (_dot(ah, bl) + _dot(al, bh))


def _silu(x):
    return x * (1.0 / (1.0 + jnp.exp(-x)))


def _sigmoid(x):
    return 1.0 / (1.0 + jnp.exp(-x))


def _softplus(x):
    return jnp.maximum(x, 0.0) + jnp.log(1.0 + jnp.exp(-jnp.abs(x)))


def _pack_bf16_pairs(x):
    bits = lax.bitcast_convert_type(x.astype(BF16).astype(F32), jnp.uint32)
    half = x.shape[1] // 2
    return bits[:, :half] | (bits[:, half:] >> 16)


def _unpack_bf16_pairs(p):
    hi = lax.bitcast_convert_type(p & jnp.uint32(0xFFFF0000), F32)
    lo = lax.bitcast_convert_type(p << 16, F32)
    return jnp.concatenate([hi, lo], axis=1)


def _rms(x, g):
    return x * lax.rsqrt(jnp.mean(x * x, axis=-1, keepdims=True) + NORM_EPS) * g


def _adaln_kernel(c_ref, w_ref, b_ref, o_ref):
    o_ref[...] = _dot_f32(_silu(c_ref[...]), w_ref[...]) + b_ref[...]


ADALN_ROWS = 8
ADALN_TILE = 1536


def _adaln(cond, w, b):
    n_l, d, n6 = w.shape
    tn = ADALN_TILE
    rows = cond.shape[0]
    return pl.pallas_call(
        _adaln_kernel,
        grid=(n_l, n6 // tn),
        in_specs=[pl.BlockSpec((rows, d), lambda l, j: (0, 0)),
                  pl.BlockSpec((None, d, tn), lambda l, j: (l, 0, j)),
                  pl.BlockSpec((None, 1, tn), lambda l, j: (l, 0, j))],
        out_specs=pl.BlockSpec((None, rows, tn), lambda l, j: (l, 0, j)),
        out_shape=jax.ShapeDtypeStruct((n_l, rows, n6), F32),
        compiler_params=_params("parallel", "parallel"),
        name="adaln",
    )(cond, w, b.reshape(n_l, 1, n6))


def _project(u, w_refs):
    return jnp.concatenate([_dot(u, w_ref[...]) for w_ref in w_refs], axis=-1)


def _joint_rows_specs(ctx, x):
    tm = ROW_TILE
    ctx_tiles = ctx.shape[1] // tm
    d = ctx.shape[2]
    return [pl.BlockSpec((None, tm, d), lambda b, i: (b, jnp.minimum(i, ctx_tiles - 1), 0)),
            pl.BlockSpec((None, tm, d), lambda b, i: (b, jnp.maximum(i - ctx_tiles, 0), 0))], ctx_tiles


def _joint_rows(ctx_ref, x_ref, ctx_tiles):
    return jnp.where(pl.program_id(1) < ctx_tiles, ctx_ref[...], x_ref[...])


def _norm_proj_kernel(c_ref, x_ref, g_ref, sh_ref, sc_ref, *refs, ctx_tiles, n_main):
    u = _rms(_joint_rows(c_ref, x_ref, ctx_tiles), g_ref[...]) * (1.0 + sc_ref[...]) + sh_ref[...]
    u = u.astype(BF16)
    refs[-2][...] = _project(u, refs[:n_main])
    refs[-1][...] = _project(u, refs[n_main:-2]).astype(BF16)


def _norm_proj(ctx, x, g, shift, scale, ws, ws_bf16):
    bsz, n_ctx, d = ctx.shape
    t = n_ctx + x.shape[1]
    n, n_bf = sum(w.shape[1] for w in ws), sum(w.shape[1] for w in ws_bf16)
    tm = ROW_TILE
    seg = lambda b, i: (b, (i * tm >= n_ctx).astype(jnp.int32), 0, 0)
    row_specs, ctx_tiles = _joint_rows_specs(ctx, x)
    return pl.pallas_call(
        functools.partial(_norm_proj_kernel, ctx_tiles=ctx_tiles, n_main=len(ws)),
        grid=(bsz, t // tm),
        in_specs=row_specs + [pl.BlockSpec((1, d), lambda b, i: (0, 0)),
                              pl.BlockSpec((None, None, 1, d), seg),
                              pl.BlockSpec((None, None, 1, d), seg)]
                 + [pl.BlockSpec(w.shape, lambda b, i: (0, 0)) for w in list(ws) + list(ws_bf16)],
        out_specs=[pl.BlockSpec((None, tm, n), lambda b, i: (b, i, 0)),
                   pl.BlockSpec((None, tm, n_bf), lambda b, i: (b, i, 0))],
        out_shape=[jax.ShapeDtypeStruct((bsz, t, n), F32), jax.ShapeDtypeStruct((bsz, t, n_bf), BF16)],
        compiler_params=_params("parallel", "parallel"),
        name="norm_proj",
    )(ctx, x, g.reshape(1, d), shift, scale, *ws, *ws_bf16)


def _scan_constants(c, reverse):
    t = np.arange(c)[:, None]
    u = np.arange(c)[None, :]
    sels = [u <= t, u > t]
    masks = []
    m = c // 2
    while m >= 1:
        blk = t // (2 * m)
        upper_t = (t % (2 * m)) >= m
        r = blk * (2 * m) + m - 1
        s_blk = u // (2 * m)
        upper_s = (u % (2 * m)) >= m
        sels.append((upper_t & (u > r) & (u <= t)) | ((~upper_t) & (u > t) & (u <= r)))
        masks.append((blk == s_blk) & upper_t & (~upper_s))
        m //= 2
    masks.append(t == u)
    sel = np.stack(sels).astype(np.float32)
    msk = np.stack(masks).astype(np.float32)
    if reverse:
        sel = sel[:, ::-1, ::-1]
        msk = msk[:, ::-1, ::-1]
    return np.ascontiguousarray(sel.reshape(-1, c)), np.ascontiguousarray(msk)


def _chunk_order(i, n_ctx_chunks, n_chunks, reverse):
    if not reverse:
        return i
    return jnp.where(i < n_ctx_chunks, n_ctx_chunks - 1 - i, n_chunks - 1 - (i - n_ctx_chunks))


GROUP_KEYS = 256


def _decay_chunk(q, k, v, la, consts, st_ref, heads, dk, dv):
    sel_ref, mask_ref, hm_ref, hmb_ref, vm_ref = consts
    c = q.shape[0]
    n_lvl = mask_ref.shape[0] - 1
    hpg = GROUP_KEYS // dk
    cs = _dot(sel_ref[...], jnp.concatenate(_split3(la), axis=0))
    e_q = jnp.exp(cs[0:c])
    e_k = jnp.exp(cs[c:2 * c])
    e_tot = jnp.exp(jnp.sum(la, axis=0, keepdims=True))
    vb = v.astype(BF16)
    outs = []
    for g in range(heads // hpg):
        ks = slice(g * GROUP_KEYS, (g + 1) * GROUP_KEYS)
        vs = slice(g * hpg * dv, (g + 1) * hpg * dv)
        qg, kg = q[:, ks], k[:, ks]
        key_stack = lambda x: jnp.concatenate([x.astype(BF16) * hmb_ref[h] for h in range(hpg)], axis=0)
        att = mask_ref[n_lvl] * _dot_nt(qg.astype(BF16), key_stack(kg))
        for l in range(n_lvl):
            e = jnp.exp(cs[(2 + l) * c:(3 + l) * c, ks])
            att = att + mask_ref[l] * _dot_nt((qg * e).astype(BF16), key_stack(kg * e))
        v_blocks = jnp.concatenate([vb[:, vs] * vm_ref[h] for h in range(hpg)], axis=0)
        intra = _dot(att.astype(BF16), v_blocks)
        st = st_ref[g]
        q_stack = jnp.concatenate([(qg * e_q[:, ks]) * hm_ref[h] for h in range(hpg)], axis=0)
        inter = _dot_nt(q_stack.astype(BF16), st.astype(BF16))
        upd = _dot_tn(vb[:, vs], (kg * e_k[:, ks]).astype(BF16))
        new = st * e_tot[:, ks]
        for h in range(hpg):
            new = new + upd[h * dv:(h + 1) * dv] * hm_ref[h]
        st_ref[g] = new
        outs.append(intra + jnp.concatenate([inter[h * c:(h + 1) * c] for h in range(hpg)], axis=-1))
    return jnp.concatenate(outs, axis=-1)


def _log_sigmoid(x):
    return jnp.minimum(x, 0.0) - jnp.log(1.0 + jnp.exp(-jnp.abs(x)))


def _gla_kernel(*refs):
    ins, head_masks, (o_refs, st_refs) = (refs[0:8], refs[8:16]), refs[16:19], (refs[19:21], refs[21:23])

    @pl.when(pl.program_id(1) == 0)
    def _():
        for st_ref in st_refs:
            st_ref[...] = jnp.zeros_like(st_ref)

    for d, ((q_ref, k_ref, v_ref, lr_ref, gw_ref, gb_ref, sel_ref, mask_ref), o_ref, st_ref) in enumerate(
            zip(ins, o_refs, st_refs)):
        z = _dot_f32(lr_ref[...], gw_ref[...]) + gb_ref[...]
        la = _log_sigmoid(z) * (1.0 / GLA_TAU)
        q = q_ref[...] * (GLA_DK ** -0.5)
        k, v = k_ref[...], v_ref[...]
        for rows in _sub_chunks(q.shape[0], d == 1):
            o_ref[rows, :] = _decay_chunk(q[rows], k[rows], v[rows], la[rows], (sel_ref, mask_ref) + head_masks,
                                          st_ref, GLA_HEADS, GLA_DK, GLA_DV).astype(o_ref.dtype)


def _hgrn_kernel(*refs, layer):
    ins, head_masks, (o_refs, st_refs) = (refs[0:6], refs[6:12]), refs[12:15], (refs[15:17], refs[17:19])

    @pl.when(pl.program_id(1) == 0)
    def _():
        for st_ref in st_refs:
            st_ref[...] = jnp.zeros_like(st_ref)

    for d, ((q_ref, f_ref, v_ref, lb_ref, sel_ref, mask_ref), o_ref, st_ref) in enumerate(
            zip(ins, o_refs, st_refs)):
        e = jnp.exp(lb_ref[...] - jnp.max(lb_ref[...], axis=0, keepdims=True))
        lb = jnp.sum(e[0:layer + 1], axis=0, keepdims=True) / jnp.sum(e, axis=0, keepdims=True)
        f = lb + (1.0 - lb) * _sigmoid(f_ref[...])
        q, k, v, la = _silu(q_ref[...]), 1.0 - f, v_ref[...], jnp.log(f)
        for rows in _sub_chunks(q.shape[0], d == 1):
            o_ref[rows, :] = _decay_chunk(q[rows], k[rows], v[rows], la[rows], (sel_ref, mask_ref) + head_masks,
                                          st_ref, HG_HEADS, HG_EXPAND, HG_DV).astype(o_ref.dtype)


def _sub_chunks(rows, reverse, chunk=SCAN_CHUNK):
    order = range(rows // chunk)
    return [slice(j * chunk, (j + 1) * chunk) for j in (reversed(order) if reverse else order)]


def _scan_specs(blk, n_ctx, t, reverse, chunk=None, stacked_heads=1):
    n_blocks = t // blk
    order = functools.partial(_chunk_order, n_ctx_chunks=n_ctx // blk, n_chunks=n_blocks, reverse=reverse)

    def col(width, idx):
        return pl.BlockSpec((None, blk, width), lambda b, i: (b, order(i), idx))

    sel, msk = _scan_constants(chunk or blk, reverse)
    sel3 = np.concatenate([sel, sel, sel], axis=1)
    msk = np.tile(msk, (1, 1, stacked_heads))
    const = lambda a: pl.BlockSpec(a.shape, lambda b, i: (0,) * a.ndim)
    return n_blocks, col, const, jnp.asarray(sel3, BF16), jnp.asarray(msk, F32)


def _head_masks(dk, dv):
    hpg = GROUP_KEYS // dk
    hm = np.zeros((hpg, 1, GROUP_KEYS), np.float32)
    vm = np.zeros((hpg, 1, hpg * dv), np.float32)
    for h in range(hpg):
        hm[h, 0, h * dk:(h + 1) * dk] = 1.0
        vm[h, 0, h * dv:(h + 1) * dv] = 1.0
    return jnp.asarray(hm), jnp.asarray(hm, BF16), jnp.asarray(vm, BF16)


AB_Q, AB_K, AB_G = 0, 256, 512
AB_HQ, AB_HF, AB_HG, AB_LR = 1024, 1536, 2560, 3072
AB_PAD_COLS = 3200
AB_V, AB_HI = 0, 1


def _gla_scan(proj, vals, gate_w_pad, gate_b, n_ctx):
    bsz, t, _ = proj.shape
    hpg = GROUP_KEYS // GLA_DK
    in_specs, args, outs = [], [], []
    for d in range(2):
        n_blocks, col, const, sel, msk = _scan_specs(SCAN_BLOCK, n_ctx, t, d == 1, SCAN_CHUNK, hpg)
        in_specs += [col(GLA_KEY_W, AB_Q // GLA_KEY_W), col(GLA_KEY_W, AB_K // GLA_KEY_W),
                     col(GLA_VAL_W, AB_V), col(LANES, AB_LR // LANES),
                     const(gate_w_pad[d]), const(gate_b[d]), const(sel), const(msk)]
        args += [proj, proj, vals, proj, gate_w_pad[d], gate_b[d], sel, msk]
        outs.append(col(GLA_VAL_W, 0))
    hm = _head_masks(GLA_DK, GLA_DV)
    return pl.pallas_call(
        _gla_kernel,
        grid=(bsz, n_blocks),
        in_specs=in_specs + [const(m) for m in hm],
        out_specs=outs,
        out_shape=[jax.ShapeDtypeStruct((bsz, t, GLA_VAL_W), BF16)] * 2,
        scratch_shapes=[pltpu.VMEM((GLA_HEADS // hpg, GLA_DV, GROUP_KEYS), F32)] * 2,
        compiler_params=_params("parallel", "arbitrary"),
        name="gla_scan",
    )(*args, *hm)


def _hgrn_scan(proj, vals, lb, layer, n_ctx):
    bsz, t, _ = proj.shape
    hpg = GROUP_KEYS // HG_EXPAND
    in_specs, args, outs = [], [], []
    for d in range(2):
        n_blocks, col, const, sel, msk = _scan_specs(SCAN_BLOCK, n_ctx, t, d == 1, SCAN_CHUNK, hpg)
        in_specs += [col(HG_KEY_W, AB_HQ // HG_KEY_W), col(HG_KEY_W, AB_HF // HG_KEY_W + d),
                     col(HG_VAL_W, AB_HI), const(lb[d]), const(sel), const(msk)]
        args += [proj, proj, vals, lb[d], sel, msk]
        outs.append(col(HG_VAL_W, 0))
    hm = _head_masks(HG_EXPAND, HG_DV)
    return pl.pallas_call(
        functools.partial(_hgrn_kernel, layer=layer),
        grid=(bsz, n_blocks),
        in_specs=in_specs + [const(m) for m in hm],
        out_specs=outs,
        out_shape=[jax.ShapeDtypeStruct((bsz, t, HG_VAL_W), BF16)] * 2,
        scratch_shapes=[pltpu.VMEM((HG_HEADS // hpg, HG_DV, GROUP_KEYS), F32)] * 2,
        compiler_params=_params("parallel", "arbitrary"),
        name="hgrn_scan",
    )(*args, *hm)


def _mix_out_ab_kernel(*refs, ctx_tiles):
    (c_ref, x_ref, gf_ref, gb_ref, hf_ref, hb_ref, gg_ref, hg_ref, gn_ref, hn_ref, w_ref, gt_ref) = refs[0:12]
    pre_in, o_ref, pre_out = refs[12:12 + N_FFN_PRE_IN], refs[12 + N_FFN_PRE_IN], refs[13 + N_FFN_PRE_IN:]
    feats = []
    both = lambda fwd_ref, bwd_ref: fwd_ref[...].astype(F32) + bwd_ref[...].astype(F32)
    for o, gate, g in ((both(gf_ref, gb_ref), gg_ref[...], gn_ref[...]),
                       (both(hf_ref, hb_ref), hg_ref[...], hn_ref[...])):
        for hd in range(o.shape[-1] // LANES):
            s = slice(hd * LANES, (hd + 1) * LANES)
            feats.append(_rms(o[:, s], g) * _silu(gate[:, s]))
    feat = jnp.concatenate(feats, axis=-1).astype(BF16)
    h_new = _joint_rows(c_ref, x_ref, ctx_tiles) + gt_ref[...] * _dot(feat, w_ref[...])
    o_ref[...] = h_new
    _ffn_pre_body(h_new, *pre_in, *pre_out)


def _mix_out_ab(ctx, x, o_gla, o_hg, proj, gla_norm_g, hg_norm_g, w_out, gate, pre):
    bsz, n_ctx, d = ctx.shape
    t = n_ctx + x.shape[1]
    tm = ROW_TILE
    seg = lambda b, i: (b, (i * tm >= n_ctx).astype(jnp.int32), 0, 0)
    row = lambda width, idx: pl.BlockSpec((None, tm, width), lambda b, i: (b, i, idx))
    vec = pl.BlockSpec((1, LANES), lambda b, i: (0, 0))
    p_in, p_args, p_out, p_shape, p_scratch = _ffn_pre_parts(bsz, t, d, *pre, seg)
    row_specs, ctx_tiles = _joint_rows_specs(ctx, x)
    outs = pl.pallas_call(
        functools.partial(_mix_out_ab_kernel, ctx_tiles=ctx_tiles),
        grid=(bsz, t // tm),
        in_specs=row_specs + [row(GLA_VAL_W, 0), row(GLA_VAL_W, 0), row(HG_VAL_W, 0), row(HG_VAL_W, 0),
                  row(GLA_VAL_W, AB_G // GLA_VAL_W), row(HG_VAL_W, AB_HG // HG_VAL_W), vec, vec,
                  pl.BlockSpec(w_out.shape, lambda b, i: (0, 0)),
                  pl.BlockSpec((None, None, 1, d), seg)] + p_in,
        out_specs=[row(d, 0)] + p_out,
        out_shape=[jax.ShapeDtypeStruct((bsz, t, d), F32)] + p_shape,
        scratch_shapes=p_scratch,
        compiler_params=_params("arbitrary", "arbitrary"),
        name="mix_out_ab",
    )(ctx, x, o_gla[0], o_gla[1], o_hg[0], o_hg[1], proj, proj, gla_norm_g.reshape(1, -1),
      hg_norm_g.reshape(1, -1), w_out, gate, *p_args)
    return outs[0], outs[1:]


HALO = 8


def _norm_proj_conv_kernel(h_ref, hp_ref, hn_ref, g_ref, sh_ref, sc_ref, cw_ref, cb_ref, *refs,
                           ctx_tiles, n_tiles):
    w_refs, (hy_ref, xbc_ref, zdt_ref) = refs[:-3], refs[-3:]
    i = pl.program_id(1)
    tm = h_ref.shape[0]
    hh = jnp.concatenate([hp_ref[...], h_ref[...], hn_ref[...]], axis=0)
    u = _rms(hh, g_ref[...]) * (1.0 + sc_ref[...]) + sh_ref[...]
    p = _project(u.astype(BF16), w_refs)
    n_conv = cw_ref.shape[1]
    n_hy = hy_ref.shape[1]
    pc = p[:, 0:n_conv]
    rows_all = tm + 2 * HALO
    cur = pc[HALO:HALO + tm]
    prev = pltpu.roll(pc, 1, axis=0)[HALO:HALO + tm]
    nxt = pltpu.roll(pc, rows_all - 1, axis=0)[HALO:HALO + tm]
    first = jnp.logical_or(i == 0, i == ctx_tiles)
    last = jnp.logical_or(i == ctx_tiles - 1, i == n_tiles - 1)
    rows = lax.broadcasted_iota(jnp.int32, cur.shape, 0)
    prev = jnp.where(jnp.logical_and(first, rows == 0), 0.0, prev)
    nxt = jnp.where(jnp.logical_and(last, rows == tm - 1), 0.0, nxt)
    y = prev * cw_ref[0:1, :] + cur * cw_ref[1:2, :] + nxt * cw_ref[2:3, :] + cb_ref[...]
    hy_ref[...] = y[:, 0:n_hy]
    xbc_ref[...] = _silu(y[:, n_hy:n_conv])
    zdt_ref[...] = p[HALO:HALO + tm, n_conv:]


def _norm_proj_conv(h, g, shift, scale, ws, conv_w, conv_b, n_hy, n_ctx):
    bsz, t, d = h.shape
    n = sum(w.shape[1] for w in ws)
    n_conv = conv_w.shape[1]
    tm = ROW_TILE
    n_tiles, ctx_tiles = t // tm, n_ctx // tm
    r8 = tm // HALO
    last8 = t // HALO - 1
    seg = lambda b, i: (b, (i >= ctx_tiles).astype(jnp.int32), 0, 0)
    kern = functools.partial(_norm_proj_conv_kernel, ctx_tiles=ctx_tiles, n_tiles=n_tiles)
    return pl.pallas_call(
        kern,
        grid=(bsz, n_tiles),
        in_specs=[pl.BlockSpec((None, tm, d), lambda b, i: (b, i, 0)),
                  pl.BlockSpec((None, HALO, d), lambda b, i: (b, jnp.maximum(i * r8 - 1, 0), 0)),
                  pl.BlockSpec((None, HALO, d), lambda b, i: (b, jnp.minimum((i + 1) * r8, last8), 0)),
                  pl.BlockSpec((1, d), lambda b, i: (0, 0)),
                  pl.BlockSpec((None, None, 1, d), seg),
                  pl.BlockSpec((None, None, 1, d), seg),
                  pl.BlockSpec((3, n_conv), lambda b, i: (0, 0)),
                  pl.BlockSpec((1, n_conv), lambda b, i: (0, 0))]
                 + [pl.BlockSpec(w.shape, lambda b, i: (0, 0)) for w in ws],
        out_specs=[pl.BlockSpec((None, tm, n_hy), lambda b, i: (b, jnp.maximum(i - ctx_tiles, 0), 0)),
                   pl.BlockSpec((None, tm, n_conv - n_hy), lambda b, i: (b, i, 0)),
                   pl.BlockSpec((None, tm, n - n_conv), lambda b, i: (b, i, 0))],
        out_shape=[jax.ShapeDtypeStruct((bsz, t - n_ctx, n_hy), F32),
                   jax.ShapeDtypeStruct((bsz, t, n_conv - n_hy), F32),
                   jax.ShapeDtypeStruct((bsz, t, n - n_conv), F32)],
        compiler_params=_params("parallel", "arbitrary"),
        name="norm_proj_conv",
    )(h, h, h, g.reshape(1, d), shift, scale, conv_w, conv_b, *ws)


def _ssd_kernel(*refs):
    ins, hexp_ref, o_refs, st_refs = (refs[0:7], refs[7:14]), refs[14], refs[15:17], refs[17:19]

    @pl.when(pl.program_id(1) == 0)
    def _():
        for st_ref in st_refs:
            st_ref[...] = jnp.zeros_like(st_ref)

    for d, ((xbc_ref, dt_ref, bias_ref, alog_ref, mq_ref, mk_ref, mask_ref), o_ref, st_ref) in enumerate(
            zip(ins, o_refs, st_refs)):
        for rows in _sub_chunks(xbc_ref.shape[0], d == 1, SSD_CHUNK):
            _ssd_chunk(rows, xbc_ref, dt_ref, bias_ref, alog_ref, hexp_ref, mq_ref, mk_ref, mask_ref, o_ref, st_ref)


def _ssd_chunk(rows, xbc_ref, dt_ref, bias_ref, alog_ref, hexp_ref, mq_ref, mk_ref, mask_ref, o_ref, st_ref):
    c = SSD_CHUNK
    hpg = MB_HEADS // MB_GROUPS
    gw = hpg * MB_HEAD_DIM
    dt = _softplus(dt_ref[rows, :] + bias_ref[...])
    la = -dt * jnp.exp(alog_ref[...])
    la3 = jnp.concatenate(_split3(la), axis=0)
    cq = _dot(mq_ref[...], la3)
    ck = _dot(mk_ref[...], la3)
    cq_t = lax.dot_general(la3, mq_ref[...], (((0,), (1,)), ((), ())), preferred_element_type=F32)
    tot = jnp.broadcast_to(jnp.sum(la, axis=0, keepdims=True), (8, la.shape[1]))
    per_head = jnp.concatenate([dt, cq, ck, tot], axis=0)
    per_lane = _dot(jnp.concatenate(_split3(per_head), axis=1), hexp_ref[...])
    dt_x = per_lane[0:c]
    eq_x = jnp.exp(per_lane[c:2 * c])
    ek_x = jnp.exp(per_lane[2 * c:3 * c])
    etot_x = jnp.exp(per_lane[3 * c:3 * c + 1])
    xs = xbc_ref[rows, 0:MB_INNER] * dt_x
    mask = mask_ref[...]
    outs = []
    for g in range(MB_GROUPS):
        bm = xbc_ref[rows, MB_INNER + g * MB_STATE:MB_INNER + (g + 1) * MB_STATE].astype(BF16)
        cm = xbc_ref[rows, MB_INNER + MB_BC_W + g * MB_STATE:MB_INNER + MB_BC_W + (g + 1) * MB_STATE].astype(BF16)
        cb = _dot_nt(cm, bm)
        st = st_ref[g]
        gs = slice(g * gw, (g + 1) * gw)
        y_inter = _dot(cm, st.astype(BF16)) * eq_x[:, gs]
        for r in range(hpg):
            hd = g * hpg + r
            diff = cq[:, hd:hd + 1] - cq_t[hd:hd + 1, :]
            w = cb * jnp.exp(jnp.where(mask > 0.0, diff, -jnp.inf))
            ps = slice(hd * MB_HEAD_DIM, (hd + 1) * MB_HEAD_DIM)
            outs.append(_dot(w.astype(BF16), xs[:, ps].astype(BF16))
                        + y_inter[:, r * MB_HEAD_DIM:(r + 1) * MB_HEAD_DIM])
        st_ref[g] = st * etot_x[:, gs] + _dot_tn(bm, (xs[:, gs] * ek_x[:, gs]).astype(BF16))
    o_ref[rows, :] = jnp.concatenate(outs, axis=-1).astype(o_ref.dtype)


def _ssd_scan(xbc, proj, dt_col, dt_bias, a_log, n_ctx):
    bsz, t, _ = xbc.shape
    c = SSD_CHUNK
    pad = lambda v: jnp.zeros((1, LANES), F32).at[0, :MB_HEADS].set(v)
    hexp = np.zeros((LANES, MB_INNER), np.float32)
    for hd in range(MB_HEADS):
        hexp[hd, hd * MB_HEAD_DIM:(hd + 1) * MB_HEAD_DIM] = 1.0
    hexp = jnp.asarray(np.concatenate([hexp, hexp, hexp], axis=0), BF16)
    tri = np.tril(np.ones((c, c), np.float32))
    in_specs, args, outs = [], [], []
    for d in range(2):
        n_chunks, col, const, sel, _ = _scan_specs(SSD_BLOCK, n_ctx, t, d == 1, c)
        mq, mk = sel[0:c], sel[c:2 * c]
        mask = jnp.asarray(tri[::-1, ::-1].copy() if d == 1 else tri)
        bias, alog = pad(dt_bias[d]), pad(a_log[d].astype(F32))
        in_specs += [col(xbc.shape[-1], 0), col(LANES, dt_col + d), const(bias), const(alog),
                     const(mq), const(mk), const(mask)]
        args += [xbc, proj, bias, alog, mq, mk, mask]
        outs.append(col(MB_INNER, 0))
    return pl.pallas_call(
        _ssd_kernel,
        grid=(bsz, n_chunks),
        in_specs=in_specs + [const(hexp)],
        out_specs=outs,
        out_shape=[jax.ShapeDtypeStruct((bsz, t, MB_INNER), BF16)] * 2,
        scratch_shapes=[pltpu.VMEM((MB_GROUPS, MB_STATE, MB_INNER // MB_GROUPS), F32)] * 2,
        compiler_params=_params("parallel", "arbitrary"),
        name="ssd_scan",
    )(*args, hexp)


def _mix_out_cd_kernel(*refs):
    h_ref, hy_ref, yf_ref, yb_ref, xs_ref, z_ref, dsk_ref, ng_ref, w_ref, gt_ref = refs[0:10]
    pre_in, o_ref, pre_out = refs[10:10 + N_FFN_PRE_IN], refs[10 + N_FFN_PRE_IN], refs[11 + N_FFN_PRE_IN:]
    y = (yf_ref[...].astype(F32) + yb_ref[...].astype(F32) + dsk_ref[...] * xs_ref[...]) * _silu(z_ref[...])
    gw = MB_INNER // MB_GROUPS
    ys = [_rms(y[:, g * gw:(g + 1) * gw], ng_ref[:, g * gw:(g + 1) * gw]) for g in range(MB_GROUPS)]
    feat = jnp.concatenate([hy_ref[...]] + ys, axis=-1).astype(BF16)
    h_new = h_ref[...] + gt_ref[...] * _dot(feat, w_ref[...])
    o_ref[...] = h_new
    _ffn_pre_body(h_new, *pre_in, *pre_out)


def _mix_out_cd(h, hy, y_ssd, xbc, proj, z_col, d_skip_x, norm_g, w_out, gate, n_ctx, pre):
    bsz, t, d = h.shape
    tm = ROW_TILE
    n_lat = t - n_ctx
    off = n_ctx // tm
    row = lambda width, idx: pl.BlockSpec((None, tm, width), lambda b, i: (b, i + off, idx))
    vec = pl.BlockSpec((1, MB_INNER), lambda b, i: (0, 0))
    latent = lambda b, i: (b, 1, 0, 0)
    p_in, p_args, p_out, p_shape, p_scratch = _ffn_pre_parts(bsz, n_lat, d, *pre, latent)
    outs = pl.pallas_call(
        _mix_out_cd_kernel,
        grid=(bsz, n_lat // tm),
        in_specs=[row(d, 0), pl.BlockSpec((None, tm, HY_CH), lambda b, i: (b, i, 0)),
                  row(MB_INNER, 0), row(MB_INNER, 0), row(MB_INNER, 0), row(MB_INNER, z_col), vec, vec,
                  pl.BlockSpec(w_out.shape, lambda b, i: (0, 0)),
                  pl.BlockSpec((None, None, 1, d), latent)] + p_in,
        out_specs=[pl.BlockSpec((None, tm, d), lambda b, i: (b, i, 0))] + p_out,
        out_shape=[jax.ShapeDtypeStruct((bsz, n_lat, d), F32)] + p_shape,
        scratch_shapes=p_scratch,
        compiler_params=_params("arbitrary", "arbitrary"),
        name="mix_out_cd",
    )(h, hy, y_ssd[0], y_ssd[1], xbc, proj, d_skip_x, norm_g.reshape(1, -1), w_out, gate, *p_args)
    return outs[0], outs[1:]


def _top2_of4(a, b, c, d):
    hi1, lo1, hi2, lo2 = jnp.maximum(a, b), jnp.minimum(a, b), jnp.maximum(c, d), jnp.minimum(c, d)
    return jnp.maximum(hi1, hi2) + jnp.maximum(jnp.minimum(hi1, hi2), jnp.maximum(lo1, lo2))


def _first_argmax(vals, skip=None):
    idx = None
    for j, vj in enumerate(vals):
        if idx is None and skip is None:
            idx, best = jnp.zeros(vj.shape, jnp.int32), vj
            continue
        if idx is None:
            idx, best = jnp.full(vj.shape, -1, jnp.int32), jnp.full(vj.shape, -jnp.inf, F32)
        take = vj > best
        if skip is not None:
            take = jnp.logical_and(take, skip != j)
        idx = jnp.where(take, j, idx)
        best = jnp.where(take, vj, best)
    return idx, best


def _ffn_pre_body(h, g_ref, sh_ref, sc_ref, rw_ref, rb_ref, tri_ref, v_ref, ri_ref, rwt_ref, cnt_ref, carry_ref):
    @pl.when(jnp.logical_and(pl.program_id(0) == 0, pl.program_id(1) == 0))
    def _():
        carry_ref[...] = jnp.zeros_like(carry_ref)

    v = _rms(h, g_ref[...]) * (1.0 + sc_ref[...]) + sh_ref[...]
    v_ref[...] = _pack_bf16_pairs(v)
    st = _sigmoid(_dot_f32(v, rw_ref[...])).T[0:N_EXPERTS]
    sel = st + rb_ref[...]
    row = lambda a, e: a[e:e + 1]
    epg = EXPERTS_PER_GROUP
    gscore = [_top2_of4(*[row(sel, g * epg + j) for j in range(epg)]) for g in range(N_GROUPS)]
    best, _ = _first_argmax(gscore)

    def in_best(a, j):
        out = row(a, j)
        for g in range(1, N_GROUPS):
            out = jnp.where(best == g, row(a, g * epg + j), out)
        return out

    vals = [in_best(sel, j) for j in range(epg)]
    raw = [in_best(st, j) for j in range(epg)]
    i1, _ = _first_argmax(vals)
    i2, _ = _first_argmax(vals, skip=i1)
    pick = lambda i: functools.reduce(lambda acc, j: jnp.where(i == j, raw[j], acc), range(1, epg), raw[0])
    w1, w2 = pick(i1), pick(i2)
    wsum = w1 + w2
    e1, e2 = best * epg + i1, best * epg + i2

    experts = lax.broadcasted_iota(jnp.int32, st.shape, 0)
    oh1 = (experts == e1).astype(F32)
    oh2 = (experts == e2).astype(F32)
    cnt = oh1 + oh2
    before = _dot(cnt.astype(BF16), tri_ref[...]) + carry_ref[:, 0:1]
    ri_ref[0:1, :] = e1
    ri_ref[1:2, :] = e2
    ri_ref[2:3, :] = jnp.sum(oh1 * before, axis=0, keepdims=True).astype(jnp.int32)
    ri_ref[3:4, :] = jnp.sum(oh2 * before, axis=0, keepdims=True).astype(jnp.int32)
    ri_ref[4:8, :] = jnp.zeros((4, st.shape[1]), jnp.int32)
    lane_row = lax.broadcasted_iota(jnp.int32, (LANES, st.shape[1]), 0)
    rwt_ref[...] = jnp.where(lane_row == 0, w1 / wsum, jnp.where(lane_row == 1, w2 / wsum, 0.0)).T
    carry_ref[...] = carry_ref[...] + jnp.sum(cnt, axis=1, keepdims=True)
    cnt_ref[...] = carry_ref[...]


N_FFN_PRE_IN = 6


def _ffn_pre_parts(bsz, t, d, g, shift, scale, router_w_pad, router_b, seg):
    tm = ROW_TILE
    tri = jnp.asarray(np.triu(np.ones((tm, tm), np.float32), 1), BF16)
    in_specs = [pl.BlockSpec((1, d), lambda b, i: (0, 0)),
                pl.BlockSpec((None, None, 1, d), seg),
                pl.BlockSpec((None, None, 1, d), seg),
                pl.BlockSpec((d, LANES), lambda b, i: (0, 0)),
                pl.BlockSpec((N_EXPERTS, 1), lambda b, i: (0, 0)),
                pl.BlockSpec((tm, tm), lambda b, i: (0, 0))]
    args = (g.reshape(1, d), shift, scale, router_w_pad, router_b.reshape(N_EXPERTS, 1), tri)
    out_specs = [pl.BlockSpec((None, tm, d // 2), lambda b, i: (b, i, 0)),
                 pl.BlockSpec((None, 8, tm), lambda b, i: (b, 0, i)),
                 pl.BlockSpec((None, tm, LANES), lambda b, i: (b, i, 0)),
                 pl.BlockSpec((N_EXPERTS, LANES), lambda b, i: (0, 0))]
    out_shape = [jax.ShapeDtypeStruct((bsz, t, d // 2), jnp.uint32),
                 jax.ShapeDtypeStruct((bsz, 8, t), jnp.int32),
                 jax.ShapeDtypeStruct((bsz, t, LANES), F32),
                 jax.ShapeDtypeStruct((N_EXPERTS, LANES), F32)]
    return in_specs, args, out_specs, out_shape, [pltpu.VMEM((N_EXPERTS, LANES), F32)]


def _experts_kernel(be_ref, nb_ref, x_ref, wg_ref, wu_ref, wd_ref, o_ref, wg_s, wu_s, wd_s):
    i = pl.program_id(0)
    prev = be_ref[jnp.maximum(i - 1, 0)]
    changed = jnp.logical_or(i == 0, be_ref[i] != prev)

    @pl.when(changed)
    def _():
        wg_s[...] = wg_ref[...].astype(BF16)
        wu_s[...] = wu_ref[...].astype(BF16)
        wd_s[...] = wd_ref[...].astype(BF16)

    @pl.when(i < nb_ref[0])
    def _():
        x = _unpack_bf16_pairs(x_ref[...]).astype(BF16)
        hid = _silu(_dot(x, wg_s[...])) * _dot(x, wu_s[...])
        o_ref[...] = _pack_bf16_pairs(_dot(hid.astype(BF16), wd_s[...]))

    @pl.when(i >= nb_ref[0])
    def _():
        o_ref[...] = jnp.zeros_like(o_ref)


def _experts(xb, block_e, n_used, layer, w_gate, w_up, w_down):
    n_slots = xb.shape[0]
    n_blocks = n_slots // MOE_BLOCK
    d, de = w_gate.shape[-2:]
    wspec = lambda shape: pl.BlockSpec((None, None) + shape, lambda i, be, nb: (layer, be[i], 0, 0))
    return pl.pallas_call(
        _experts_kernel,
        grid_spec=pltpu.PrefetchScalarGridSpec(
            num_scalar_prefetch=2,
            grid=(n_blocks,),
            in_specs=[pl.BlockSpec((MOE_BLOCK, d // 2), lambda i, be, nb: (i, 0)),
                      wspec((d, de)), wspec((d, de)), wspec((de, d))],
            out_specs=pl.BlockSpec((MOE_BLOCK, d // 2), lambda i, be, nb: (i, 0)),
            scratch_shapes=[pltpu.VMEM((d, de), BF16), pltpu.VMEM((d, de), BF16), pltpu.VMEM((de, d), BF16)]),
        out_shape=jax.ShapeDtypeStruct((n_slots, d // 2), jnp.uint32),
        compiler_params=_params("arbitrary"),
        name="moe_experts",
    )(block_e, n_used, xb, w_gate, w_up, w_down)


def _ffn_post_kernel(h_ref, y0_ref, y1_ref, w_ref, gt_ref, g_ref, o_ref, *, final):
    w = w_ref[...]
    y = w[:, 0:1] * _unpack_bf16_pairs(y0_ref[...]) + w[:, 1:2] * _unpack_bf16_pairs(y1_ref[...])
    out = h_ref[...] + gt_ref[...] * y
    o_ref[...] = _rms(out, g_ref[...]) if final else out


def _ffn_post(h, y, w, gate, n_ctx, final_g=None):
    bsz, t, d = h.shape
    tm = ROW_TILE
    seg = lambda b, i: (b, (i * tm >= n_ctx).astype(jnp.int32), 0, 0)
    row = lambda width: pl.BlockSpec((None, tm, width), lambda b, i: (b, i, 0))
    choice = lambda kk: pl.BlockSpec((None, None, tm, d // 2), lambda b, i: (kk, b, i, 0))
    final = final_g is not None
    g = final_g if final else jnp.ones((d,), F32)
    return pl.pallas_call(
        functools.partial(_ffn_post_kernel, final=final),
        grid=(bsz, t // tm),
        in_specs=[row(d), choice(0), choice(1), row(LANES), pl.BlockSpec((None, None, 1, d), seg),
                  pl.BlockSpec((1, d), lambda b, i: (0, 0))],
        out_specs=row(d),
        out_shape=jax.ShapeDtypeStruct((bsz, t, d), F32),
        compiler_params=_params("parallel", "parallel"),
        name="ffn_post",
    )(h, y, y, w, gate, g.reshape(1, d))


def _slot_layout(n, ri, counts):
    e = jnp.swapaxes(ri[:, 0:2], 0, 1).reshape(TOP_K, n)
    rank = jnp.swapaxes(ri[:, 2:4], 0, 1).reshape(TOP_K, n)
    padded = (counts + MOE_BLOCK - 1) // MOE_BLOCK * MOE_BLOCK
    pend = jnp.cumsum(padded)
    pstart = pend - padded
    experts = jnp.arange(N_EXPERTS, dtype=jnp.int32)
    dest = rank + jnp.sum(jnp.where(e[..., None] == experts, pstart, 0), axis=-1)
    n_slots = (n * TOP_K + MOE_BLOCK - 1) // MOE_BLOCK * MOE_BLOCK + N_EXPERTS * MOE_BLOCK
    n_blocks = n_slots // MOE_BLOCK
    blk0 = jnp.arange(n_blocks, dtype=jnp.int32)[:, None] * MOE_BLOCK
    block_e = jnp.minimum(jnp.sum((pend[None, :] <= blk0).astype(jnp.int32), axis=-1), N_EXPERTS - 1)
    n_used = (pend[-1] // MOE_BLOCK).astype(jnp.int32).reshape(1)
    return dest, n_slots, block_e.astype(jnp.int32), n_used


SC_CORES, SC_SUBCORES = 2, 16
SC_WINDOW = 32


def _gather_rows(table, idx):
    n_rows, d = idx.shape[0], table.shape[1]
    workers = SC_CORES * SC_SUBCORES
    per_worker = n_rows // workers
    assert per_worker * workers == n_rows and per_worker % SC_WINDOW == 0
    mesh = plsc.VectorSubcoreMesh(core_axis_name="c", subcore_axis_name="s")

    @functools.partial(
        pl.kernel, mesh=mesh,
        out_type=jax.ShapeDtypeStruct((n_rows, d), table.dtype),
        scratch_types=[pltpu.VMEM((SC_WINDOW,), jnp.int32), pltpu.VMEM((SC_WINDOW,), jnp.int32),
                       pltpu.VMEM((SC_WINDOW, d), table.dtype), pltpu.VMEM((SC_WINDOW, d), table.dtype),
                       pltpu.SemaphoreType.DMA, pltpu.SemaphoreType.DMA],
    )
    def gather_kernel(table_hbm, idx_hbm, out_hbm, idx0, idx1, rows0, rows1, sem0, sem1):
        base = (lax.axis_index("s") * SC_CORES + lax.axis_index("c")) * per_worker
        n_win = per_worker // SC_WINDOW
        slots = ((idx0, rows0, sem0), (idx1, rows1, sem1))
        window = lambda j: pl.ds(pl.multiple_of(base + j * SC_WINDOW, 8), SC_WINDOW)

        def start(j, slot):
            idx_v, rows_v, sem = slots[slot]
            pltpu.sync_copy(idx_hbm.at[window(j)], idx_v)
            pltpu.async_copy(table_hbm.at[idx_v], rows_v, sem)

        def finish(j, slot):
            idx_v, rows_v, sem = slots[slot]
            pltpu.make_async_copy(table_hbm.at[idx_v], rows_v, sem).wait()
            pltpu.sync_copy(rows_v, out_hbm.at[window(j)])

        start(0, 0)

        @pl.loop(0, n_win, step=2)
        def _(j):
            @pl.when(j + 1 < n_win)
            def _():
                start(j + 1, 1)

            finish(j, 0)

            @pl.when(j + 2 < n_win)
            def _():
                start(j + 2, 0)

            @pl.when(j + 1 < n_win)
            def _():
                finish(j + 1, 1)

    return gather_kernel(table, idx)


SC_SCATTER_WINDOW = 16


def _scatter_rows(src, dest, n_slots):
    n, d = src.shape
    workers = SC_CORES * SC_SUBCORES
    per_worker = n // workers
    win = SC_SCATTER_WINDOW
    assert per_worker * workers == n and per_worker % win == 0 and dest.shape == (TOP_K, n)
    mesh = plsc.VectorSubcoreMesh(core_axis_name="c", subcore_axis_name="s")

    @functools.partial(
        pl.kernel, mesh=mesh,
        out_type=jax.ShapeDtypeStruct((n_slots, d), src.dtype),
        scratch_types=[pltpu.VMEM((win,), jnp.int32), pltpu.VMEM((win,), jnp.int32),
                       pltpu.VMEM((win, d), src.dtype), pltpu.SemaphoreType.DMA, pltpu.SemaphoreType.DMA],
    )
    def scatter_kernel(src_hbm, dest_hbm, out_hbm, idx0, idx1, rows_v, sem0, sem1):
        base = (lax.axis_index("s") * SC_CORES + lax.axis_index("c")) * per_worker

        @pl.loop(0, per_worker // win)
        def _(j):
            rows = pl.ds(pl.multiple_of(base + j * win, 8), win)
            pltpu.sync_copy(src_hbm.at[rows], rows_v)
            pltpu.sync_copy(dest_hbm.at[0, rows], idx0)
            pltpu.sync_copy(dest_hbm.at[1, rows], idx1)
            first = pltpu.async_copy(rows_v, out_hbm.at[idx0], sem0)
            second = pltpu.async_copy(rows_v, out_hbm.at[idx1], sem1)
            first.wait()
            second.wait()

    return scatter_kernel(src, dest)


def _alongside(gather, idx, side_fn, side_in):
    idx, side_in = lax.optimization_barrier((idx, side_in))
    return lax.optimization_barrier((gather(idx), side_fn(side_in)))


def _moe(h, pre_out, gate, layer, w_gate, w_up, w_down, n_ctx, final_g=None, side=None):
    bsz, t, d = h.shape
    n = bsz * t
    v, ri, rwt, counts = pre_out
    dest, n_slots, block_e, n_used = _slot_layout(n, ri, counts[:, 0].astype(jnp.int32))
    dispatch = lambda idx: _scatter_rows(v.reshape(n, d // 2), idx, n_slots)
    if side is None:
        xb = dispatch(dest)
    else:
        xb, side_a = _alongside(dispatch, dest, *side[0])
    yb = _experts(xb, block_e, n_used, layer, w_gate, w_up, w_down)
    combine = lambda idx: _gather_rows(yb, idx)
    dest_flat = dest.reshape(-1)
    if side is None:
        y, side_b = combine(dest_flat), None
    else:
        y, side_b = _alongside(combine, dest_flat, side[1], side_a)
    out = _ffn_post(h, y.reshape(TOP_K, bsz, t, d // 2), rwt, gate, n_ctx, final_g)
    return out if side is None else (out, side_b)


DFT_STEP = 16


def _dft_tables(n):
    r, *mats = _dft_tables_np(n)
    return (r,) + tuple(jnp.asarray(a).astype(BF16) for a in mats)


@functools.lru_cache(maxsize=None)
def _dft_tables_np(n):
    size = 2 * n
    r = int(round(math.sqrt(size)))
    assert r * r == size and r % DFT_STEP == 0
    p1 = np.arange(r // 2)[None, None, :]
    p2 = np.arange(r)[:, None, None]
    k1 = np.arange(r)[None, :, None]
    ang = 2.0 * np.pi * (((r * p1 + p2) * k1) % size) / size
    g_re, g_im = np.cos(ang), -np.sin(ang)
    g_in = np.concatenate([g_re, g_im], axis=1)
    g_out = np.concatenate([np.swapaxes(g_re, 1, 2), np.swapaxes(g_im, 1, 2)], axis=2) / size
    a2 = 2.0 * np.pi * ((np.arange(r)[:, None] * np.arange(r)[None, :]) % r) / r
    f_re, f_im = np.cos(a2), -np.sin(a2)
    f_fwd = np.block([[f_re, -f_im], [f_im, f_re]])
    f_inv = np.block([[f_re, f_im], [-f_im, f_re]])
    p1f = np.arange(r)[None, None, :]
    angf = 2.0 * np.pi * (((r * p1f + p2) * k1) % size) / size
    g_full = np.concatenate([np.cos(angf), -np.sin(angf)], axis=1)
    return (r,) + tuple(a.astype(np.float32) for a in (g_in, g_out, f_fwd, f_inv, g_full))


def _dot_f32_tn(a, b):
    ah = a.astype(BF16)
    al = (a - ah.astype(F32)).astype(BF16)
    bh = b.astype(BF16)
    bl = (b - bh.astype(F32)).astype(BF16)
    return _dot_tn(ah, bh) + (_dot_tn(ah, bl) + _dot_tn(al, bh))


def _hy_filter_kernel(z_ref, t_ref, w1_ref, b1_ref, w2_ref, b2_ref, fr_ref, w3_ref, w3b_ref, rates_ref, o_ref,
                      *, half_tiles):
    i = pl.program_id(0)
    hid = jnp.sin(fr_ref[...] * (_dot_f32(w1_ref[...], z_ref[...]) + b1_ref[...]))
    hid = jnp.sin(fr_ref[...] * (_dot_f32(w2_ref[...], hid) + b2_ref[...]))
    filt = _dot_f32_tn(hid, w3_ref[...])
    decay = jnp.exp(-t_ref[...] * rates_ref[...])
    for o in range(o_ref.shape[0]):
        o_ref[o] = filt[:, o * HY_CH:(o + 1) * HY_CH] * decay

    @pl.when(i == 0)
    def _():
        extra = _dot_f32_tn(hid[:, 0:LANES], w3b_ref[...])[0:8]
        first = lax.broadcasted_iota(jnp.int32, (8, HY_CH), 0) == 0
        for o in range(o_ref.shape[0]):
            add = extra[:, o * HY_CH:(o + 1) * HY_CH] * decay[0:8]
            o_ref[o, 0:8, :] = o_ref[o, 0:8, :] + jnp.where(first, add, 0.0)

    @pl.when(i == half_tiles)
    def _():
        for o in range(o_ref.shape[0]):
            o_ref[o, 0:1, :] = jnp.zeros((1, HY_CH), F32)


HY_TILE = 512


def _hy_kernels(n, w1, b1, w2, b2, w3, freq):
    pos = np.arange(2 * n)
    pos = np.where(pos < n, pos, 2 * n - pos).astype(np.float32)
    t = jnp.asarray(pos / np.float32(n - 1))
    bands = jnp.linspace(1e-4, HY_BANDS - 1, HY_BANDS, dtype=F32)
    ang = (2.0 * math.pi / n) * bands[:, None] * jnp.asarray(pos)[None, :]
    z = jnp.concatenate([t[None, :], jnp.cos(ang), -jnp.sin(ang)], axis=0)
    z = jnp.pad(z, ((0, LANES - z.shape[0]), (0, 0)))
    w1t = jnp.pad(w1, ((0, LANES - w1.shape[0]), (0, 0))).T
    hidden = w1.shape[1]
    col = lambda v: v.reshape(hidden, 1)
    w3d = jnp.swapaxes(w3.reshape(hidden, HY_ORDER, 2, HY_CH), 0, 2)
    w3d = jnp.swapaxes(w3d, 1, 2).reshape(2, hidden, HY_ORDER * HY_CH)
    rates = jnp.abs(jnp.linspace(HY_MIN_DECAY, HY_MAX_DECAY, HY_CH, dtype=F32)).reshape(1, HY_CH)
    tm = HY_TILE
    half_tiles = n // tm
    full = lambda a: pl.BlockSpec(a.shape, lambda i: (0,) * a.ndim)
    small = (w1t, col(b1), w2.T, col(b2), col(freq))
    return pl.pallas_call(
        functools.partial(_hy_filter_kernel, half_tiles=half_tiles),
        grid=(2 * n // tm,),
        in_specs=[pl.BlockSpec((LANES, tm), lambda i: (0, i)), pl.BlockSpec((tm, 1), lambda i: (i, 0))]
                 + [full(a) for a in small]
                 + [pl.BlockSpec((None, hidden, HY_ORDER * HY_CH), lambda i: ((i >= half_tiles).astype(jnp.int32), 0, 0)),
                    pl.BlockSpec((None, hidden, HY_ORDER * HY_CH), lambda i: (1, 0, 0)), full(rates)],
        out_specs=pl.BlockSpec((HY_ORDER, tm, HY_CH), lambda i: (0, i, 0)),
        out_shape=jax.ShapeDtypeStruct((HY_ORDER, 2 * n, HY_CH), F32),
        compiler_params=_params("parallel"),
        name="hy_kernels",
    )(z, t[:, None], *small, w3d, w3d, rates)


def _pack_complex(z):
    r = z.shape[0] // 2
    bits = lax.bitcast_convert_type(z.astype(BF16).astype(F32), jnp.uint32)
    return lax.bitcast_convert_type(bits[0:r] | (bits[r:2 * r] >> 16), F32)


def _unpack_complex(words):
    p = lax.bitcast_convert_type(words, jnp.uint32)
    re = lax.bitcast_convert_type(p & jnp.uint32(0xFFFF0000), F32)
    im = lax.bitcast_convert_type(p << 16, F32)
    return jnp.concatenate([re, im], axis=0).astype(BF16)


def _load_every(ref, j, count):
    return ref.reshape(count * DFT_STEP, LANES)[pl.ds(j, count, stride=DFT_STEP), :]


def _store_every(ref, j, count, val):
    ref.reshape(count * DFT_STEP, LANES)[pl.ds(j, count, stride=DFT_STEP), :] = val


def _dft_in_kernel(x_ref, g_ref, a_ref):
    n_seq, rh = x_ref.shape[0:2]
    for j in range(DFT_STEP):
        x = jnp.concatenate([_load_every(x_ref.at[b], j, rh) for b in range(n_seq)], axis=1).astype(BF16)
        res = _dot(g_ref[j], x)
        for b in range(n_seq):
            a_ref[b, j] = _pack_complex(res[:, b * LANES:(b + 1) * LANES])


def _dft_in(x4, col, g_in):
    bx, rh, r, _ = x4.shape
    c = HY_CH
    cbs = c // LANES
    return pl.pallas_call(
        _dft_in_kernel,
        grid=(r // DFT_STEP, cbs),
        in_specs=[pl.BlockSpec((bx, rh, DFT_STEP, LANES), lambda i, cb: (0, 0, i, col * cbs + cb)),
                  pl.BlockSpec((DFT_STEP, 2 * r, rh), lambda i, cb: (i, 0, 0))],
        out_specs=pl.BlockSpec((bx, DFT_STEP, r, LANES), lambda i, cb: (0, i, 0, cb)),
        out_shape=jax.ShapeDtypeStruct((bx, r, r, c), F32),
        compiler_params=_params("parallel", "parallel"),
        name="dft_in",
    )(x4, g_in)


def _stage2_operand(a_ref, j, r):
    return jnp.concatenate([_unpack_complex(_load_every(a_ref.at[b], j, r)) for b in range(a_ref.shape[0])],
                           axis=1)


def _dft_filt_kernel(a_ref, f_ref, k_ref):
    r = f_ref.shape[0] // 2
    for j in range(DFT_STEP):
        s = _dot(f_ref[...], _stage2_operand(a_ref, j, r))
        for o in range(a_ref.shape[0]):
            k_ref[o, j] = s[:, o * LANES:(o + 1) * LANES]


def _dft_filt(a, f_fwd):
    nq, r, _, c = a.shape
    return pl.pallas_call(
        _dft_filt_kernel,
        grid=(r // DFT_STEP, c // LANES),
        in_specs=[pl.BlockSpec((nq, r, DFT_STEP, LANES), lambda i, cb: (0, 0, i, cb)),
                  pl.BlockSpec(f_fwd.shape, lambda i, cb: (0, 0))],
        out_specs=pl.BlockSpec((nq, DFT_STEP, 2 * r, LANES), lambda i, cb: (0, i, 0, cb)),
        out_shape=jax.ShapeDtypeStruct((nq, r, 2 * r, c), F32),
        compiler_params=_params("parallel", "parallel"),
        name="dft_filt",
    )(a, f_fwd)


def _dft_mid_kernel(a_ref, k_ref, ff_ref, fi_ref, b_ref):
    r = ff_ref.shape[0] // 2
    n_seq = a_ref.shape[0]
    for j in range(DFT_STEP):
        s = _dot(ff_ref[...], _stage2_operand(a_ref, j, r))
        sr, si = s[0:r], s[r:2 * r]
        kr = jnp.concatenate([k_ref[j, 0:r, :]] * n_seq, axis=1)
        ki = jnp.concatenate([k_ref[j, r:2 * r, :]] * n_seq, axis=1)
        p = jnp.concatenate([sr * kr - si * ki, sr * ki + si * kr], axis=0).astype(BF16)
        back = _dot(fi_ref[...], p)
        for b in range(n_seq):
            b_ref[b, j] = _pack_complex(back[:, b * LANES:(b + 1) * LANES])


def _dft_mid(a, kspec, order, f_fwd, f_inv):
    bsz, r, _, c = a.shape
    return pl.pallas_call(
        _dft_mid_kernel,
        grid=(r // DFT_STEP, c // LANES),
        in_specs=[pl.BlockSpec((bsz, r, DFT_STEP, LANES), lambda i, cb: (0, 0, i, cb)),
                  pl.BlockSpec((None, DFT_STEP, 2 * r, LANES), lambda i, cb: (order, i, 0, cb)),
                  pl.BlockSpec(f_fwd.shape, lambda i, cb: (0, 0)),
                  pl.BlockSpec(f_inv.shape, lambda i, cb: (0, 0))],
        out_specs=pl.BlockSpec((bsz, DFT_STEP, r, LANES), lambda i, cb: (0, i, 0, cb)),
        out_shape=jax.ShapeDtypeStruct((bsz, r, r, c), F32),
        compiler_params=_params("parallel", "parallel"),
        name="dft_mid",
    )(a, kspec, f_fwd, f_inv)


def _dft_out_kernel(b_ref, g_ref, u_ref, x_ref, bias_ref, o_ref):
    n_seq, r = b_ref.shape[0:2]
    rh = o_ref.shape[1]
    for j in range(DFT_STEP):
        rhs = jnp.concatenate([_unpack_complex(_load_every(b_ref.at[b], j, r)) for b in range(n_seq)], axis=1)
        y = _dot(g_ref[j], rhs)
        for b in range(n_seq):
            yb = y[:, b * LANES:(b + 1) * LANES]
            _store_every(o_ref.at[b], j, rh,
                         _load_every(x_ref.at[b], j, rh) * (yb + _load_every(u_ref.at[b], j, rh) * bias_ref[...]))


def _dft_out(bm, g_out, u4, u_col, x4, x_col, bias):
    bsz, r, _, c = bm.shape
    rh = r // 2
    cbs = c // LANES
    seq = lambda col: pl.BlockSpec((bsz, rh, DFT_STEP, LANES), lambda i, cb: (0, 0, i, col * cbs + cb))
    return pl.pallas_call(
        _dft_out_kernel,
        grid=(r // DFT_STEP, cbs),
        in_specs=[pl.BlockSpec((bsz, r, DFT_STEP, LANES), lambda i, cb: (0, 0, i, cb)),
                  pl.BlockSpec((DFT_STEP, rh, 2 * r), lambda i, cb: (i, 0, 0)),
                  seq(u_col), seq(x_col), pl.BlockSpec((1, LANES), lambda i, cb: (0, cb))],
        out_specs=seq(0),
        out_shape=jax.ShapeDtypeStruct((bsz, rh, r, c), F32),
        compiler_params=_params("parallel", "parallel"),
        name="dft_out",
    )(bm, g_out, u4, x4, bias.reshape(1, c))


def _hyena_filter_stage1(n, filter_params):
    r, g_full = _dft_tables(n)[0], _dft_tables(n)[5]
    kern = _hy_kernels(n, *filter_params)
    return _dft_in(kern.reshape(-1, r, r, HY_CH), 0, g_full)


def _hyena_filter_spectra(n, stage1):
    return _dft_filt(stage1, _dft_tables(n)[3])


def _hyena(hy_in, kspec, conv_bias):
    bsz, n, _ = hy_in.shape
    r, g_in, g_out, f_fwd, f_inv, _ = _dft_tables(n)
    seq4 = hy_in.reshape(bsz, r // 2, r, 3 * HY_CH)
    zz = _dft_out(_dft_mid(_dft_in(seq4, 0, g_in), kspec, 0, f_fwd, f_inv), g_out,
                  seq4, 0, seq4, 1, conv_bias[0])
    out = _dft_out(_dft_mid(_dft_in(zz, 0, g_in), kspec, 1, f_fwd, f_inv), g_out,
                   zz, 0, seq4, 2, conv_bias[1])
    return out.reshape(bsz, n, HY_CH)


def _reorder_ab(w):
    offs = np.cumsum([0, GLA_KEY_W, GLA_KEY_W, GLA_VAL_W, GLA_VAL_W, GLA_LOW_RANK, GLA_LOW_RANK,
                      HG_KEY_W, HG_KEY_W, HG_KEY_W, HG_VAL_W, HG_VAL_W]).tolist()
    gq, gk, gv, gg, lr_f, lr_b, hq, hf_f, hf_b, hi, hg = range(11)
    cols = lambda first, last: w[:, offs[first]:offs[last + 1]].astype(BF16)
    lr = jnp.pad(cols(lr_f, lr_b), ((0, 0), (0, AB_PAD_COLS - AB_LR - 2 * GLA_LOW_RANK)))
    return [cols(gq, gk), cols(gg, gg), cols(hq, hf_b), cols(hg, hg), lr], [cols(gv, gv), cols(hi, hi)]


def _reorder_cd(w):
    hy_end = 3 * HY_CH
    z_end = hy_end + MB_INNER
    xbc_end = z_end + MB_INNER + 2 * MB_BC_W
    pad = lambda a: jnp.pad(a, ((0, 0), (0, LANES - MB_HEADS)))
    dt = jnp.concatenate([pad(w[:, xbc_end:xbc_end + MB_HEADS]), pad(w[:, xbc_end + MB_HEADS:])], axis=-1)
    return [w[:, :hy_end].astype(BF16), w[:, z_end:xbc_end].astype(BF16), w[:, hy_end:z_end].astype(BF16),
            dt.astype(BF16)]


def kernel(x, c, ctx, c_ctx, ada_w, ada_b, norm_mix_g, norm_ffn_g, norm_out_g, ab_w_in, ab_w_out, gla_gate_w, gla_gate_b, gla_norm_g, hg_lb, hg_norm_g, cd_w_in, cd_w_out, hy_short_w, hy_short_b, hy_w1, hy_b1, hy_w2, hy_b2, hy_w3, hy_freq, hy_bias, mb_conv_w, mb_conv_b, mb_dt_bias, mb_a_log, mb_d, mb_norm_g, router_w, router_b, moe_w_gate, moe_w_up, moe_w_down):
    bsz, n_lat, d = x.shape
    n_ctx = ctx.shape[1]
    assert ada_w.shape[0] == 2 and ab_w_in.shape[0] == 1 and cd_w_in.shape[0] == 1

    cond = jnp.zeros((ADALN_ROWS, d), F32).at[:bsz].set(c).at[bsz].set(c_ctx)
    m = _adaln(cond, ada_w, ada_b)

    def mods(layer):
        lat = m[layer, :bsz].reshape(bsz, 6, d)
        cx = jnp.broadcast_to(m[layer, bsz].reshape(1, 6, d), (bsz, 6, d))
        both = jnp.stack([cx, lat], axis=1)
        return [both[:, :, j][:, :, None, :] for j in range(6)]

    router_w_pad = jnp.zeros((d, LANES), F32).at[:, :N_EXPERTS].set(router_w)

    sh_m, sc_m, gt_m, sh_f, sc_f, gt_f = mods(0)
    proj, vals = _norm_proj(ctx, x, norm_mix_g[0], sh_m, sc_m, *_reorder_ab(ab_w_in[0]))
    gwp = [jnp.zeros((LANES, GLA_KEY_W), F32).at[GLA_LOW_RANK * dd:GLA_LOW_RANK * (dd + 1)].set(gla_gate_w[0, dd])
           for dd in range(2)]
    o_gla = _gla_scan(proj, vals, gwp, [gla_gate_b[0, dd].reshape(1, -1) for dd in range(2)], n_ctx)
    o_hg = _hgrn_scan(proj, vals, [hg_lb[dd].astype(F32) for dd in range(2)], 0, n_ctx)
    h, pre_out = _mix_out_ab(ctx, x, o_gla, o_hg, proj, gla_norm_g[0], hg_norm_g[0], ab_w_out[0].astype(BF16),
                             gt_m, (norm_ffn_g[0], sh_f, sc_f, router_w_pad, router_b))
    filter_params = (hy_w1[0], hy_b1[0], hy_w2[0], hy_b2[0], hy_w3[0], hy_freq[0])
    side = ((functools.partial(_hyena_filter_stage1, n_lat), filter_params),
            functools.partial(_hyena_filter_spectra, n_lat))
    h, kspec = _moe(h, pre_out, gt_f, 0, moe_w_gate, moe_w_up, moe_w_down, n_ctx, side=side)

    sh_m, sc_m, gt_m, sh_f, sc_f, gt_f = mods(1)
    conv_w = jnp.concatenate([hy_short_w[0], mb_conv_w[0]], axis=0).T
    conv_b = jnp.concatenate([hy_short_b[0], mb_conv_b[0]]).reshape(1, -1)
    hy_in, xbc, zdt = _norm_proj_conv(h, norm_mix_g[1], sh_m, sc_m, _reorder_cd(cd_w_in[0]),
                                      conv_w, conv_b, 3 * HY_CH, n_ctx)
    hy = _hyena(hy_in, kspec, hy_bias[0])
    y_ssd = _ssd_scan(xbc, zdt, MB_INNER // LANES, mb_dt_bias[0], mb_a_log[0], n_ctx)
    d_skip_x = jnp.repeat(mb_d[0], MB_HEAD_DIM).reshape(1, MB_INNER)
    h, pre_out = _mix_out_cd(h, hy, y_ssd, xbc, zdt, 0, d_skip_x, mb_norm_g[0], cd_w_out[0].astype(BF16), gt_m,
                             n_ctx, (norm_ffn_g[1], sh_f, sc_f, router_w_pad, router_b))
    return _moe(h, pre_out, gt_f, 1, moe_w_gate, moe_w_up, moe_w_down, 0, final_g=norm_out_g)
```

```python
import functools
import math

import numpy as np
import jax
import jax.numpy as jnp
from jax import lax
from jax.experimental import pallas as pl
from jax.experimental.pallas import tpu as pltpu
from jax.experimental.pallas import tpu_sc as plsc

NORM_EPS = 1e-6
GLA_HEADS, GLA_DK, GLA_DV, GLA_LOW_RANK, GLA_TAU = 4, 64, 128, 16, 16.0
GLA_KEY_W, GLA_VAL_W = GLA_HEADS * GLA_DK, GLA_HEADS * GLA_DV
HG_HEADS, HG_EXPAND, HG_DV = 4, 128, 128
HG_KEY_W, HG_VAL_W = HG_HEADS * HG_EXPAND, HG_HEADS * HG_DV
HY_CH, HY_ORDER, HY_BANDS = 512, 2, 16
HY_MIN_DECAY = math.log(1e-2) / 1.5
HY_MAX_DECAY = math.log(1e-2) / 0.3
MB_HEADS, MB_HEAD_DIM, MB_GROUPS, MB_STATE = 8, 64, 2, 128
MB_INNER = MB_HEADS * MB_HEAD_DIM
MB_BC_W = MB_GROUPS * MB_STATE
N_EXPERTS, N_GROUPS, TOP_K, MOE_BLOCK = 16, 4, 2, 256
EXPERTS_PER_GROUP = N_EXPERTS // N_GROUPS

LANES = 128
SCAN_CHUNK = 64
SCAN_BLOCK = 256
SSD_CHUNK = 128
SSD_BLOCK = 256
ROW_TILE = 256
VMEM_LIMIT = 56 * 1024 * 1024

BF16 = jnp.bfloat16
F32 = jnp.float32


def _params(*sem):
    return pltpu.CompilerParams(dimension_semantics=sem, vmem_limit_bytes=VMEM_LIMIT)


def _split3(x):
    hi = x.astype(BF16)
    r1 = x - hi.astype(F32)
    mid = r1.astype(BF16)
    lo = (r1 - mid.astype(F32)).astype(BF16)
    return hi, mid, lo


def _dot(a, b):
    return jnp.dot(a, b, preferred_element_type=F32)


def _dot_nt(a, b):
    return lax.dot_general(a, b, (((1,), (1,)), ((), ())), preferred_element_type=F32)


def _dot_tn(a, b):
    return lax.dot_general(a, b, (((0,), (0,)), ((), ())), preferred_element_type=F32)


def _dot_f32(a, b):
    ah = a.astype(BF16)
    al = (a - ah.astype(F32)).astype(BF16)
    bh = b.astype(BF16)
    bl = (b - bh.astype(F32)).astype(BF16)
    return _dot(ah, bh) + (_dot(ah, bl) + _dot(al, bh))


def _silu(x):
    return x * (1.0 / (1.0 + jnp.exp(-x)))


def _sigmoid(x):
    return 1.0 / (1.0 + jnp.exp(-x))


def _softplus(x):
    return jnp.maximum(x, 0.0) + jnp.log(1.0 + jnp.exp(-jnp.abs(x)))


def _pack_bf16_pairs(x):
    bits = lax.bitcast_convert_type(x.astype(BF16).astype(F32), jnp.uint32)
    half = x.shape[1] // 2
    return bits[:, :half] | (bits[:, half:] >> 16)


def _unpack_bf16_pairs(p):
    hi = lax.bitcast_convert_type(p & jnp.uint32(0xFFFF0000), F32)
    lo = lax.bitcast_convert_type(p << 16, F32)
    return jnp.concatenate([hi, lo], axis=1)


def _rms(x, g):
    return x * lax.rsqrt(jnp.mean(x * x, axis=-1, keepdims=True) + NORM_EPS) * g


def _adaln_kernel(c_ref, w_ref, b_ref, o_ref):
    o_ref[...] = _dot_f32(_silu(c_ref[...]), w_ref[...]) + b_ref[...]


ADALN_ROWS = 8
ADALN_TILE = 1536


def _adaln(cond, w, b):
    n_l, d, n6 = w.shape
    tn = ADALN_TILE
    rows = cond.shape[0]
    return pl.pallas_call(
        _adaln_kernel,
        grid=(n_l, n6 // tn),
        in_specs=[pl.BlockSpec((rows, d), lambda l, j: (0, 0)),
                  pl.BlockSpec((None, d, tn), lambda l, j: (l, 0, j)),
                  pl.BlockSpec((None, 1, tn), lambda l, j: (l, 0, j))],
        out_specs=pl.BlockSpec((None, rows, tn), lambda l, j: (l, 0, j)),
        out_shape=jax.ShapeDtypeStruct((n_l, rows, n6), F32),
        compiler_params=_params("parallel", "parallel"),
        name="adaln",
    )(cond, w, b.reshape(n_l, 1, n6))


def _project(u, w_refs):
    return jnp.concatenate([_dot(u, w_ref[...]) for w_ref in w_refs], axis=-1)


def _joint_rows_specs(ctx, x):
    tm = ROW_TILE
    ctx_tiles = ctx.shape[1] // tm
    d = ctx.shape[2]
    return [pl.BlockSpec((None, tm, d), lambda b, i: (b, jnp.minimum(i, ctx_tiles - 1), 0)),
            pl.BlockSpec((None, tm, d), lambda b, i: (b, jnp.maximum(i - ctx_tiles, 0), 0))], ctx_tiles


def _joint_rows(ctx_ref, x_ref, ctx_tiles):
    return jnp.where(pl.program_id(1) < ctx_tiles, ctx_ref[...], x_ref[...])


def _norm_proj_kernel(c_ref, x_ref, g_ref, sh_ref, sc_ref, *refs, ctx_tiles, n_main):
    u = _rms(_joint_rows(c_ref, x_ref, ctx_tiles), g_ref[...]) * (1.0 + sc_ref[...]) + sh_ref[...]
    u = u.astype(BF16)
    refs[-2][...] = _project(u, refs[:n_main])
    refs[-1][...] = _project(u, refs[n_main:-2]).astype(BF16)


def _norm_proj(ctx, x, g, shift, scale, ws, ws_bf16):
    bsz, n_ctx, d = ctx.shape
    t = n_ctx + x.shape[1]
    n, n_bf = sum(w.shape[1] for w in ws), sum(w.shape[1] for w in ws_bf16)
    tm = ROW_TILE
    seg = lambda b, i: (b, (i * tm >= n_ctx).astype(jnp.int32), 0, 0)
    row_specs, ctx_tiles = _joint_rows_specs(ctx, x)
    return pl.pallas_call(
        functools.partial(_norm_proj_kernel, ctx_tiles=ctx_tiles, n_main=len(ws)),
        grid=(bsz, t // tm),
        in_specs=row_specs + [pl.BlockSpec((1, d), lambda b, i: (0, 0)),
                              pl.BlockSpec((None, None, 1, d), seg),
                              pl.BlockSpec((None, None, 1, d), seg)]
                 + [pl.BlockSpec(w.shape, lambda b, i: (0, 0)) for w in list(ws) + list(ws_bf16)],
        out_specs=[pl.BlockSpec((None, tm, n), lambda b, i: (b, i, 0)),
                   pl.BlockSpec((None, tm, n_bf), lambda b, i: (b, i, 0))],
        out_shape=[jax.ShapeDtypeStruct((bsz, t, n), F32), jax.ShapeDtypeStruct((bsz, t, n_bf), BF16)],
        compiler_params=_params("parallel", "parallel"),
        name="norm_proj",
    )(ctx, x, g.reshape(1, d), shift, scale, *ws, *ws_bf16)


def _scan_constants(c, reverse):
    t = np.arange(c)[:, None]
    u = np.arange(c)[None, :]
    sels = [u <= t, u > t]
    masks = []
    m = c // 2
    while m >= 1:
        blk = t // (2 * m)
        upper_t = (t % (2 * m)) >= m
        r = blk * (2 * m) + m - 1
        s_blk = u // (2 * m)
        upper_s = (u % (2 * m)) >= m
        sels.append((upper_t & (u > r) & (u <= t)) | ((~upper_t) & (u > t) & (u <= r)))
        masks.append((blk == s_blk) & upper_t & (~upper_s))
        m //= 2
    masks.append(t == u)
    sel = np.stack(sels).astype(np.float32)
    msk = np.stack(masks).astype(np.float32)
    if reverse:
        sel = sel[:, ::-1, ::-1]
        msk = msk[:, ::-1, ::-1]
    return np.ascontiguousarray(sel.reshape(-1, c)), np.ascontiguousarray(msk)


def _chunk_order(i, n_ctx_chunks, n_chunks, reverse):
    if not reverse:
        return i
    return jnp.where(i < n_ctx_chunks, n_ctx_chunks - 1 - i, n_chunks - 1 - (i - n_ctx_chunks))


GROUP_KEYS = 256


def _decay_chunk(q, k, v, la, consts, st_ref, heads, dk, dv):
    sel_ref, mask_ref, hm_ref, hmb_ref, vm_ref = consts
    c = q.shape[0]
    n_lvl = mask_ref.shape[0] - 1
    hpg = GROUP_KEYS // dk
    cs = _dot(sel_ref[...], jnp.concatenate(_split3(la), axis=0))
    e_q = jnp.exp(cs[0:c])
    e_k = jnp.exp(cs[c:2 * c])
    e_tot = jnp.exp(jnp.sum(la, axis=0, keepdims=True))
    vb = v.astype(BF16)
    outs = []
    for g in range(heads // hpg):
        ks = slice(g * GROUP_KEYS, (g + 1) * GROUP_KEYS)
        vs = slice(g * hpg * dv, (g + 1) * hpg * dv)
        qg, kg = q[:, ks], k[:, ks]
        key_stack = lambda x: jnp.concatenate([x.astype(BF16) * hmb_ref[h] for h in range(hpg)], axis=0)
        att = mask_ref[n_lvl] * _dot_nt(qg.astype(BF16), key_stack(kg))
        for l in range(n_lvl):
            e = jnp.exp(cs[(2 + l) * c:(3 + l) * c, ks])
            att = att + mask_ref[l] * _dot_nt((qg * e).astype(BF16), key_stack(kg * e))
        v_blocks = jnp.concatenate([vb[:, vs] * vm_ref[h] for h in range(hpg)], axis=0)
        intra = _dot(att.astype(BF16), v_blocks)
        st = st_ref[g]
        q_stack = jnp.concatenate([(qg * e_q[:, ks]) * hm_ref[h] for h in range(hpg)], axis=0)
        inter = _dot_nt(q_stack.astype(BF16), st.astype(BF16))
        upd = _dot_tn(vb[:, vs], (kg * e_k[:, ks]).astype(BF16))
        new = st * e_tot[:, ks]
        for h in range(hpg):
            new = new + upd[h * dv:(h + 1) * dv] * hm_ref[h]
        st_ref[g] = new
        outs.append(intra + jnp.concatenate([inter[h * c:(h + 1) * c] for h in range(hpg)], axis=-1))
    return jnp.concatenate(outs, axis=-1)


def _log_sigmoid(x):
    return jnp.minimum(x, 0.0) - jnp.log(1.0 + jnp.exp(-jnp.abs(x)))


def _gla_kernel(*refs):
    ins, head_masks, (o_refs, st_refs) = (refs[0:8], refs[8:16]), refs[16:19], (refs[19:21], refs[21:23])

    @pl.when(pl.program_id(1) == 0)
    def _():
        for st_ref in st_refs:
            st_ref[...] = jnp.zeros_like(st_ref)

    for d, ((q_ref, k_ref, v_ref, lr_ref, gw_ref, gb_ref, sel_ref, mask_ref), o_ref, st_ref) in enumerate(
            zip(ins, o_refs, st_refs)):
        z = _dot_f32(lr_ref[...], gw_ref[...]) + gb_ref[...]
        la = _log_sigmoid(z) * (1.0 / GLA_TAU)
        q = q_ref[...] * (GLA_DK ** -0.5)
        k, v = k_ref[...], v_ref[...]
        for rows in _sub_chunks(q.shape[0], d == 1):
            o_ref[rows, :] = _decay_chunk(q[rows], k[rows], v[rows], la[rows], (sel_ref, mask_ref) + head_masks,
                                          st_ref, GLA_HEADS, GLA_DK, GLA_DV).astype(o_ref.dtype)


def _hgrn_kernel(*refs, layer):
    ins, head_masks, (o_refs, st_refs) = (refs[0:6], refs[6:12]), refs[12:15], (refs[15:17], refs[17:19])

    @pl.when(pl.program_id(1) == 0)
    def _():
        for st_ref in st_refs:
            st_ref[...] = jnp.zeros_like(st_ref)

    for d, ((q_ref, f_ref, v_ref, lb_ref, sel_ref, mask_ref), o_ref, st_ref) in enumerate(
            zip(ins, o_refs, st_refs)):
        e = jnp.exp(lb_ref[...] - jnp.max(lb_ref[...], axis=0, keepdims=True))
        lb = jnp.sum(e[0:layer + 1], axis=0, keepdims=True) / jnp.sum(e, axis=0, keepdims=True)
        f = lb + (1.0 - lb) * _sigmoid(f_ref[...])
        q, k, v, la = _silu(q_ref[...]), 1.0 - f, v_ref[...], jnp.log(f)
        for rows in _sub_chunks(q.shape[0], d == 1):
            o_ref[rows, :] = _decay_chunk(q[rows], k[rows], v[rows], la[rows], (sel_ref, mask_ref) + head_masks,
                                          st_ref, HG_HEADS, HG_EXPAND, HG_DV).astype(o_ref.dtype)


def _sub_chunks(rows, reverse, chunk=SCAN_CHUNK):
    order = range(rows // chunk)
    return [slice(j * chunk, (j + 1) * chunk) for j in (reversed(order) if reverse else order)]


def _scan_specs(blk, n_ctx, t, reverse, chunk=None, stacked_heads=1):
    n_blocks = t // blk
    order = functools.partial(_chunk_order, n_ctx_chunks=n_ctx // blk, n_chunks=n_blocks, reverse=reverse)

    def col(width, idx):
        return pl.BlockSpec((None, blk, width), lambda b, i: (b, order(i), idx))

    sel, msk = _scan_constants(chunk or blk, reverse)
    sel3 = np.concatenate([sel, sel, sel], axis=1)
    msk = np.tile(msk, (1, 1, stacked_heads))
    const = lambda a: pl.BlockSpec(a.shape, lambda b, i: (0,) * a.ndim)
    return n_blocks, col, const, jnp.asarray(sel3, BF16), jnp.asarray(msk, F32)


def _head_masks(dk, dv):
    hpg = GROUP_KEYS // dk
    hm = np.zeros((hpg, 1, GROUP_KEYS), np.float32)
    vm = np.zeros((hpg, 1, hpg * dv), np.float32)
    for h in range(hpg):
        hm[h, 0, h * dk:(h + 1) * dk] = 1.0
        vm[h, 0, h * dv:(h + 1) * dv] = 1.0
    return jnp.asarray(hm), jnp.asarray(hm, BF16), jnp.asarray(vm, BF16)


AB_Q, AB_K, AB_G = 0, 256, 512
AB_HQ, AB_HF, AB_HG, AB_LR = 1024, 1536, 2560, 3072
AB_PAD_COLS = 3200
AB_V, AB_HI = 0, 1


def _gla_scan(proj, vals, gate_w_pad, gate_b, n_ctx):
    bsz, t, _ = proj.shape
    hpg = GROUP_KEYS // GLA_DK
    in_specs, args, outs = [], [], []
    for d in range(2):
        n_blocks, col, const, sel, msk = _scan_specs(SCAN_BLOCK, n_ctx, t, d == 1, SCAN_CHUNK, hpg)
        in_specs += [col(GLA_KEY_W, AB_Q // GLA_KEY_W), col(GLA_KEY_W, AB_K // GLA_KEY_W),
                     col(GLA_VAL_W, AB_V), col(LANES, AB_LR // LANES),
                     const(gate_w_pad[d]), const(gate_b[d]), const(sel), const(msk)]
        args += [proj, proj, vals, proj, gate_w_pad[d], gate_b[d], sel, msk]
        outs.append(col(GLA_VAL_W, 0))
    hm = _head_masks(GLA_DK, GLA_DV)
    return pl.pallas_call(
        _gla_kernel,
        grid=(bsz, n_blocks),
        in_specs=in_specs + [const(m) for m in hm],
        out_specs=outs,
        out_shape=[jax.ShapeDtypeStruct((bsz, t, GLA_VAL_W), BF16)] * 2,
        scratch_shapes=[pltpu.VMEM((GLA_HEADS // hpg, GLA_DV, GROUP_KEYS), F32)] * 2,
        compiler_params=_params("parallel", "arbitrary"),
        name="gla_scan",
    )(*args, *hm)


def _hgrn_scan(proj, vals, lb, layer, n_ctx):
    bsz, t, _ = proj.shape
    hpg = GROUP_KEYS // HG_EXPAND
    in_specs, args, outs = [], [], []
    for d in range(2):
        n_blocks, col, const, sel, msk = _scan_specs(SCAN_BLOCK, n_ctx, t, d == 1, SCAN_CHUNK, hpg)
        in_specs += [col(HG_KEY_W, AB_HQ // HG_KEY_W), col(HG_KEY_W, AB_HF // HG_KEY_W + d),
                     col(HG_VAL_W, AB_HI), const(lb[d]), const(sel), const(msk)]
        args += [proj, proj, vals, lb[d], sel, msk]
        outs.append(col(HG_VAL_W, 0))
    hm = _head_masks(HG_EXPAND, HG_DV)
    return pl.pallas_call(
        functools.partial(_hgrn_kernel, layer=layer),
        grid=(bsz, n_blocks),
        in_specs=in_specs + [const(m) for m in hm],
        out_specs=outs,
        out_shape=[jax.ShapeDtypeStruct((bsz, t, HG_VAL_W), BF16)] * 2,
        scratch_shapes=[pltpu.VMEM((HG_HEADS // hpg, HG_DV, GROUP_KEYS), F32)] * 2,
        compiler_params=_params("parallel", "arbitrary"),
        name="hgrn_scan",
    )(*args, *hm)


def _mix_out_ab_kernel(*refs, ctx_tiles):
    (c_ref, x_ref, gf_ref, gb_ref, hf_ref, hb_ref, gg_ref, hg_ref, gn_ref, hn_ref, w_ref, gt_ref) = refs[0:12]
    pre_in, o_ref, pre_out = refs[12:12 + N_FFN_PRE_IN], refs[12 + N_FFN_PRE_IN], refs[13 + N_FFN_PRE_IN:]
    feats = []
    both = lambda fwd_ref, bwd_ref: fwd_ref[...].astype(F32) + bwd_ref[...].astype(F32)
    for o, gate, g in ((both(gf_ref, gb_ref), gg_ref[...], gn_ref[...]),
                       (both(hf_ref, hb_ref), hg_ref[...], hn_ref[...])):
        for hd in range(o.shape[-1] // LANES):
            s = slice(hd * LANES, (hd + 1) * LANES)
            feats.append(_rms(o[:, s], g) * _silu(gate[:, s]))
    feat = jnp.concatenate(feats, axis=-1).astype(BF16)
    h_new = _joint_rows(c_ref, x_ref, ctx_tiles) + gt_ref[...] * _dot(feat, w_ref[...])
    o_ref[...] = h_new
    _ffn_pre_body(h_new, *pre_in, *pre_out)


def _mix_out_ab(ctx, x, o_gla, o_hg, proj, gla_norm_g, hg_norm_g, w_out, gate, pre):
    bsz, n_ctx, d = ctx.shape
    t = n_ctx + x.shape[1]
    tm = ROW_TILE
    seg = lambda b, i: (b, (i * tm >= n_ctx).astype(jnp.int32), 0, 0)
    row = lambda width, idx: pl.BlockSpec((None, tm, width), lambda b, i: (b, i, idx))
    vec = pl.BlockSpec((1, LANES), lambda b, i: (0, 0))
    p_in, p_args, p_out, p_shape, p_scratch = _ffn_pre_parts(bsz, t, d, *pre, seg)
    row_specs, ctx_tiles = _joint_rows_specs(ctx, x)
    outs = pl.pallas_call(
        functools.partial(_mix_out_ab_kernel, ctx_tiles=ctx_tiles),
        grid=(bsz, t // tm),
        in_specs=row_specs + [row(GLA_VAL_W, 0), row(GLA_VAL_W, 0), row(HG_VAL_W, 0), row(HG_VAL_W, 0),
                  row(GLA_VAL_W, AB_G // GLA_VAL_W), row(HG_VAL_W, AB_HG // HG_VAL_W), vec, vec,
                  pl.BlockSpec(w_out.shape, lambda b, i: (0, 0)),
                  pl.BlockSpec((None, None, 1, d), seg)] + p_in,
        out_specs=[row(d, 0)] + p_out,
        out_shape=[jax.ShapeDtypeStruct((bsz, t, d), F32)] + p_shape,
        scratch_shapes=p_scratch,
        compiler_params=_params("arbitrary", "arbitrary"),
        name="mix_out_ab",
    )(ctx, x, o_gla[0], o_gla[1], o_hg[0], o_hg[1], proj, proj, gla_norm_g.reshape(1, -1),
      hg_norm_g.reshape(1, -1), w_out, gate, *p_args)
    return outs[0], outs[1:]


HALO = 8


def _norm_proj_conv_kernel(h_ref, hp_ref, hn_ref, g_ref, sh_ref, sc_ref, cw_ref, cb_ref, *refs,
                           ctx_tiles, n_tiles):
    w_refs, (hy_ref, xbc_ref, zdt_ref) = refs[:-3], refs[-3:]
    i = pl.program_id(1)
    tm = h_ref.shape[0]
    hh = jnp.concatenate([hp_ref[...], h_ref[...], hn_ref[...]], axis=0)
    u = _rms(hh, g_ref[...]) * (1.0 + sc_ref[...]) + sh_ref[...]
    p = _project(u.astype(BF16), w_refs)
    n_conv = cw_ref.shape[1]
    n_hy = hy_ref.shape[1]
    pc = p[:, 0:n_conv]
    rows_all = tm + 2 * HALO
    cur = pc[HALO:HALO + tm]
    prev = pltpu.roll(pc, 1, axis=0)[HALO:HALO + tm]
    nxt = pltpu.roll(pc, rows_all - 1, axis=0)[HALO:HALO + tm]
    first = jnp.logical_or(i == 0, i == ctx_tiles)
    last = jnp.logical_or(i == ctx_tiles - 1, i == n_tiles - 1)
    rows = lax.broadcasted_iota(jnp.int32, cur.shape, 0)
    prev = jnp.where(jnp.logical_and(first, rows == 0), 0.0, prev)
    nxt = jnp.where(jnp.logical_and(last, rows == tm - 1), 0.0, nxt)
    y = prev * cw_ref[0:1, :] + cur * cw_ref[1:2, :] + nxt * cw_ref[2:3, :] + cb_ref[...]
    hy_ref[...] = y[:, 0:n_hy]
    xbc_ref[...] = _silu(y[:, n_hy:n_conv])
    zdt_ref[...] = p[HALO:HALO + tm, n_conv:]


def _norm_proj_conv(h, g, shift, scale, ws, conv_w, conv_b, n_hy, n_ctx):
    bsz, t, d = h.shape
    n = sum(w.shape[1] for w in ws)
    n_conv = conv_w.shape[1]
    tm = ROW_TILE
    n_tiles, ctx_tiles = t // tm, n_ctx // tm
    r8 = tm // HALO
    last8 = t // HALO - 1
    seg = lambda b, i: (b, (i >= ctx_tiles).astype(jnp.int32), 0, 0)
    kern = functools.partial(_norm_proj_conv_kernel, ctx_tiles=ctx_tiles, n_tiles=n_tiles)
    return pl.pallas_call(
        kern,
        grid=(bsz, n_tiles),
        in_specs=[pl.BlockSpec((None, tm, d), lambda b, i: (b, i, 0)),
                  pl.BlockSpec((None, HALO, d), lambda b, i: (b, jnp.maximum(i * r8 - 1, 0), 0)),
                  pl.BlockSpec((None, HALO, d), lambda b, i: (b, jnp.minimum((i + 1) * r8, last8), 0)),
                  pl.BlockSpec((1, d), lambda b, i: (0, 0)),
                  pl.BlockSpec((None, None, 1, d), seg),
                  pl.BlockSpec((None, None, 1, d), seg),
                  pl.BlockSpec((3, n_conv), lambda b, i: (0, 0)),
                  pl.BlockSpec((1, n_conv), lambda b, i: (0, 0))]
                 + [pl.BlockSpec(w.shape, lambda b, i: (0, 0)) for w in ws],
        out_specs=[pl.BlockSpec((None, tm, n_hy), lambda b, i: (b, jnp.maximum(i - ctx_tiles, 0), 0)),
                   pl.BlockSpec((None, tm, n_conv - n_hy), lambda b, i: (b, i, 0)),
                   pl.BlockSpec((None, tm, n - n_conv), lambda b, i: (b, i, 0))],
        out_shape=[jax.ShapeDtypeStruct((bsz, t - n_ctx, n_hy), F32),
                   jax.ShapeDtypeStruct((bsz, t, n_conv - n_hy), F32),
                   jax.ShapeDtypeStruct((bsz, t, n - n_conv), F32)],
        compiler_params=_params("parallel", "arbitrary"),
        name="norm_proj_conv",
    )(h, h, h, g.reshape(1, d), shift, scale, conv_w, conv_b, *ws)


def _ssd_kernel(*refs):
    ins, hexp_ref, o_refs, st_refs = (refs[0:7], refs[7:14]), refs[14], refs[15:17], refs[17:19]

    @pl.when(pl.program_id(1) == 0)
    def _():
        for st_ref in st_refs:
            st_ref[...] = jnp.zeros_like(st_ref)

    for d, ((xbc_ref, dt_ref, bias_ref, alog_ref, mq_ref, mk_ref, mask_ref), o_ref, st_ref) in enumerate(
            zip(ins, o_refs, st_refs)):
        for rows in _sub_chunks(xbc_ref.shape[0], d == 1, SSD_CHUNK):
            _ssd_chunk(rows, xbc_ref, dt_ref, bias_ref, alog_ref, hexp_ref, mq_ref, mk_ref, mask_ref, o_ref, st_ref)


def _ssd_chunk(rows, xbc_ref, dt_ref, bias_ref, alog_ref, hexp_ref, mq_ref, mk_ref, mask_ref, o_ref, st_ref):
    c = SSD_CHUNK
    hpg = MB_HEADS // MB_GROUPS
    gw = hpg * MB_HEAD_DIM
    dt = _softplus(dt_ref[rows, :] + bias_ref[...])
    la = -dt * jnp.exp(alog_ref[...])
    la3 = jnp.concatenate(_split3(la), axis=0)
    cq = _dot(mq_ref[...], la3)
    ck = _dot(mk_ref[...], la3)
    cq_t = lax.dot_general(la3, mq_ref[...], (((0,), (1,)), ((), ())), preferred_element_type=F32)
    tot = jnp.broadcast_to(jnp.sum(la, axis=0, keepdims=True), (8, la.shape[1]))
    per_head = jnp.concatenate([dt, cq, ck, tot], axis=0)
    per_lane = _dot(jnp.concatenate(_split3(per_head), axis=1), hexp_ref[...])
    dt_x = per_lane[0:c]
    eq_x = jnp.exp(per_lane[c:2 * c])
    ek_x = jnp.exp(per_lane[2 * c:3 * c])
    etot_x = jnp.exp(per_lane[3 * c:3 * c + 1])
    xs = xbc_ref[rows, 0:MB_INNER] * dt_x
    mask = mask_ref[...]
    outs = []
    for g in range(MB_GROUPS):
        bm = xbc_ref[rows, MB_INNER + g * MB_STATE:MB_INNER + (g + 1) * MB_STATE].astype(BF16)
        cm = xbc_ref[rows, MB_INNER + MB_BC_W + g * MB_STATE:MB_INNER + MB_BC_W + (g + 1) * MB_STATE].astype(BF16)
        cb = _dot_nt(cm, bm)
        st = st_ref[g]
        gs = slice(g * gw, (g + 1) * gw)
        y_inter = _dot(cm, st.astype(BF16)) * eq_x[:, gs]
        for r in range(hpg):
            hd = g * hpg + r
            diff = cq[:, hd:hd + 1] - cq_t[hd:hd + 1, :]
            w = cb * jnp.exp(jnp.where(mask > 0.0, diff, -jnp.inf))
            ps = slice(hd * MB_HEAD_DIM, (hd + 1) * MB_HEAD_DIM)
            outs.append(_dot(w.astype(BF16), xs[:, ps].astype(BF16))
                        + y_inter[:, r * MB_HEAD_DIM:(r + 1) * MB_HEAD_DIM])
        st_ref[g] = st * etot_x[:, gs] + _dot_tn(bm, (xs[:, gs] * ek_x[:, gs]).astype(BF16))
    o_ref[rows, :] = jnp.concatenate(outs, axis=-1).astype(o_ref.dtype)


def _ssd_scan(xbc, proj, dt_col, dt_bias, a_log, n_ctx):
    bsz, t, _ = xbc.shape
    c = SSD_CHUNK
    pad = lambda v: jnp.zeros((1, LANES), F32).at[0, :MB_HEADS].set(v)
    hexp = np.zeros((LANES, MB_INNER), np.float32)
    for hd in range(MB_HEADS):
        hexp[hd, hd * MB_HEAD_DIM:(hd + 1) * MB_HEAD_DIM] = 1.0
    hexp = jnp.asarray(np.concatenate([hexp, hexp, hexp], axis=0), BF16)
    tri = np.tril(np.ones((c, c), np.float32))
    in_specs, args, outs = [], [], []
    for d in range(2):
        n_chunks, col, const, sel, _ = _scan_specs(SSD_BLOCK, n_ctx, t, d == 1, c)
        mq, mk = sel[0:c], sel[c:2 * c]
        mask = jnp.asarray(tri[::-1, ::-1].copy() if d == 1 else tri)
        bias, alog = pad(dt_bias[d]), pad(a_log[d].astype(F32))
        in_specs += [col(xbc.shape[-1], 0), col(LANES, dt_col + d), const(bias), const(alog),
                     const(mq), const(mk), const(mask)]
        args += [xbc, proj, bias, alog, mq, mk, mask]
        outs.append(col(MB_INNER, 0))
    return pl.pallas_call(
        _ssd_kernel,
        grid=(bsz, n_chunks),
        in_specs=in_specs + [const(hexp)],
        out_specs=outs,
        out_shape=[jax.ShapeDtypeStruct((bsz, t, MB_INNER), BF16)] * 2,
        scratch_shapes=[pltpu.VMEM((MB_GROUPS, MB_STATE, MB_INNER // MB_GROUPS), F32)] * 2,
        compiler_params=_params("parallel", "arbitrary"),
        name="ssd_scan",
    )(*args, hexp)


def _mix_out_cd_kernel(*refs):
    h_ref, hy_ref, yf_ref, yb_ref, xs_ref, z_ref, dsk_ref, ng_ref, w_ref, gt_ref = refs[0:10]
    pre_in, o_ref, pre_out = refs[10:10 + N_FFN_PRE_IN], refs[10 + N_FFN_PRE_IN], refs[11 + N_FFN_PRE_IN:]
    y = (yf_ref[...].astype(F32) + yb_ref[...].astype(F32) + dsk_ref[...] * xs_ref[...]) * _silu(z_ref[...])
    gw = MB_INNER // MB_GROUPS
    ys = [_rms(y[:, g * gw:(g + 1) * gw], ng_ref[:, g * gw:(g + 1) * gw]) for g in range(MB_GROUPS)]
    feat = jnp.concatenate([hy_ref[...]] + ys, axis=-1).astype(BF16)
    h_new = h_ref[...] + gt_ref[...] * _dot(feat, w_ref[...])
    o_ref[...] = h_new
    _ffn_pre_body(h_new, *pre_in, *pre_out)


def _mix_out_cd(h, hy, y_ssd, xbc, proj, z_col, d_skip_x, norm_g, w_out, gate, n_ctx, pre):
    bsz, t, d = h.shape
    tm = ROW_TILE
    n_lat = t - n_ctx
    off = n_ctx // tm
    row = lambda width, idx: pl.BlockSpec((None, tm, width), lambda b, i: (b, i + off, idx))
    vec = pl.BlockSpec((1, MB_INNER), lambda b, i: (0, 0))
    latent = lambda b, i: (b, 1, 0, 0)
    p_in, p_args, p_out, p_shape, p_scratch = _ffn_pre_parts(bsz, n_lat, d, *pre, latent)
    outs = pl.pallas_call(
        _mix_out_cd_kernel,
        grid=(bsz, n_lat // tm),
        in_specs=[row(d, 0), pl.BlockSpec((None, tm, HY_CH), lambda b, i: (b, i, 0)),
                  row(MB_INNER, 0), row(MB_INNER, 0), row(MB_INNER, 0), row(MB_INNER, z_col), vec, vec,
                  pl.BlockSpec(w_out.shape, lambda b, i: (0, 0)),
                  pl.BlockSpec((None, None, 1, d), latent)] + p_in,
        out_specs=[pl.BlockSpec((None, tm, d), lambda b, i: (b, i, 0))] + p_out,
        out_shape=[jax.ShapeDtypeStruct((bsz, n_lat, d), F32)] + p_shape,
        scratch_shapes=p_scratch,
        compiler_params=_params("arbitrary", "arbitrary"),
        name="mix_out_cd",
    )(h, hy, y_ssd[0], y_ssd[1], xbc, proj, d_skip_x, norm_g.reshape(1, -1), w_out, gate, *p_args)
    return outs[0], outs[1:]


def _top2_of4(a, b, c, d):
    hi1, lo1, hi2, lo2 = jnp.maximum(a, b), jnp.minimum(a, b), jnp.maximum(c, d), jnp.minimum(c, d)
    return jnp.maximum(hi1, hi2) + jnp.maximum(jnp.minimum(hi1, hi2), jnp.maximum(lo1, lo2))


def _first_argmax(vals, skip=None):
    idx = None
    for j, vj in enumerate(vals):
        if idx is None and skip is None:
            idx, best = jnp.zeros(vj.shape, jnp.int32), vj
            continue
        if idx is None:
            idx, best = jnp.full(vj.shape, -1, jnp.int32), jnp.full(vj.shape, -jnp.inf, F32)
        take = vj > best
        if skip is not None:
            take = jnp.logical_and(take, skip != j)
        idx = jnp.where(take, j, idx)
        best = jnp.where(take, vj, best)
    return idx, best


def _ffn_pre_body(h, g_ref, sh_ref, sc_ref, rw_ref, rb_ref, tri_ref, v_ref, ri_ref, rwt_ref, cnt_ref, carry_ref):
    @pl.when(jnp.logical_and(pl.program_id(0) == 0, pl.program_id(1) == 0))
    def _():
        carry_ref[...] = jnp.zeros_like(carry_ref)

    v = _rms(h, g_ref[...]) * (1.0 + sc_ref[...]) + sh_ref[...]
    v_ref[...] = _pack_bf16_pairs(v)
    st = _sigmoid(_dot_f32(v, rw_ref[...])).T[0:N_EXPERTS]
    sel = st + rb_ref[...]
    row = lambda a, e: a[e:e + 1]
    epg = EXPERTS_PER_GROUP
    gscore = [_top2_of4(*[row(sel, g * epg + j) for j in range(epg)]) for g in range(N_GROUPS)]
    best, _ = _first_argmax(gscore)

    def in_best(a, j):
        out = row(a, j)
        for g in range(1, N_GROUPS):
            out = jnp.where(best == g, row(a, g * epg + j), out)
        return out

    vals = [in_best(sel, j) for j in range(epg)]
    raw = [in_best(st, j) for j in range(epg)]
    i1, _ = _first_argmax(vals)
    i2, _ = _first_argmax(vals, skip=i1)
    pick = lambda i: functools.reduce(lambda acc, j: jnp.where(i == j, raw[j], acc), range(1, epg), raw[0])
    w1, w2 = pick(i1), pick(i2)
    wsum = w1 + w2
    e1, e2 = best * epg + i1, best * epg + i2

    experts = lax.broadcasted_iota(jnp.int32, st.shape, 0)
    oh1 = (experts == e1).astype(F32)
    oh2 = (experts == e2).astype(F32)
    cnt = oh1 + oh2
    before = _dot(cnt.astype(BF16), tri_ref[...]) + carry_ref[:, 0:1]
    ri_ref[0:1, :] = e1
    ri_ref[1:2, :] = e2
    ri_ref[2:3, :] = jnp.sum(oh1 * before, axis=0, keepdims=True).astype(jnp.int32)
    ri_ref[3:4, :] = jnp.sum(oh2 * before, axis=0, keepdims=True).astype(jnp.int32)
    ri_ref[4:8, :] = jnp.zeros((4, st.shape[1]), jnp.int32)
    lane_row = lax.broadcasted_iota(jnp.int32, (LANES, st.shape[1]), 0)
    rwt_ref[...] = jnp.where(lane_row == 0, w1 / wsum, jnp.where(lane_row == 1, w2 / wsum, 0.0)).T
    carry_ref[...] = carry_ref[...] + jnp.sum(cnt, axis=1, keepdims=True)
    cnt_ref[...] = carry_ref[...]


N_FFN_PRE_IN = 6


def _ffn_pre_parts(bsz, t, d, g, shift, scale, router_w_pad, router_b, seg):
    tm = ROW_TILE
    tri = jnp.asarray(np.triu(np.ones((tm, tm), np.float32), 1), BF16)
    in_specs = [pl.BlockSpec((1, d), lambda b, i: (0, 0)),
                pl.BlockSpec((None, None, 1, d), seg),
                pl.BlockSpec((None, None, 1, d), seg),
                pl.BlockSpec((d, LANES), lambda b, i: (0, 0)),
                pl.BlockSpec((N_EXPERTS, 1), lambda b, i: (0, 0)),
                pl.BlockSpec((tm, tm), lambda b, i: (0, 0))]
    args = (g.reshape(1, d), shift, scale, router_w_pad, router_b.reshape(N_EXPERTS, 1), tri)
    out_specs = [pl.BlockSpec((None, tm, d // 2), lambda b, i: (b, i, 0)),
                 pl.BlockSpec((None, 8, tm), lambda b, i: (b, 0, i)),
                 pl.BlockSpec((None, tm, LANES), lambda b, i: (b, i, 0)),
                 pl.BlockSpec((N_EXPERTS, LANES), lambda b, i: (0, 0))]
    out_shape = [jax.ShapeDtypeStruct((bsz, t, d // 2), jnp.uint32),
                 jax.ShapeDtypeStruct((bsz, 8, t), jnp.int32),
                 jax.ShapeDtypeStruct((bsz, t, LANES), F32),
                 jax.ShapeDtypeStruct((N_EXPERTS, LANES), F32)]
    return in_specs, args, out_specs, out_shape, [pltpu.VMEM((N_EXPERTS, LANES), F32)]


def _experts_kernel(be_ref, nb_ref, x_ref, wg_ref, wu_ref, wd_ref, o_ref, wg_s, wu_s, wd_s):
    i = pl.program_id(0)
    prev = be_ref[jnp.maximum(i - 1, 0)]
    changed = jnp.logical_or(i == 0, be_ref[i] != prev)

    @pl.when(changed)
    def _():
        wg_s[...] = wg_ref[...].astype(BF16)
        wu_s[...] = wu_ref[...].astype(BF16)
        wd_s[...] = wd_ref[...].astype(BF16)

    @pl.when(i < nb_ref[0])
    def _():
        x = _unpack_bf16_pairs(x_ref[...]).astype(BF16)
        hid = _silu(_dot(x, wg_s[...])) * _dot(x, wu_s[...])
        o_ref[...] = _pack_bf16_pairs(_dot(hid.astype(BF16), wd_s[...]))

    @pl.when(i >= nb_ref[0])
    def _():
        o_ref[...] = jnp.zeros_like(o_ref)


def _experts(xb, block_e, n_used, layer, w_gate, w_up, w_down):
    n_slots = xb.shape[0]
    n_blocks = n_slots // MOE_BLOCK
    d, de = w_gate.shape[-2:]
    wspec = lambda shape: pl.BlockSpec((None, None) + shape, lambda i, be, nb: (layer, be[i], 0, 0))
    return pl.pallas_call(
        _experts_kernel,
        grid_spec=pltpu.PrefetchScalarGridSpec(
            num_scalar_prefetch=2,
            grid=(n_blocks,),
            in_specs=[pl.BlockSpec((MOE_BLOCK, d // 2), lambda i, be, nb: (i, 0)),
                      wspec((d, de)), wspec((d, de)), wspec((de, d))],
            out_specs=pl.BlockSpec((MOE_BLOCK, d // 2), lambda i, be, nb: (i, 0)),
            scratch_shapes=[pltpu.VMEM((d, de), BF16), pltpu.VMEM((d, de), BF16), pltpu.VMEM((de, d), BF16)]),
        out_shape=jax.ShapeDtypeStruct((n_slots, d // 2), jnp.uint32),
        compiler_params=_params("arbitrary"),
        name="moe_experts",
    )(block_e, n_used, xb, w_gate, w_up, w_down)


def _ffn_post_kernel(h_ref, y0_ref, y1_ref, w_ref, gt_ref, g_ref, o_ref, *, final):
    w = w_ref[...]
    y = w[:, 0:1] * _unpack_bf16_pairs(y0_ref[...]) + w[:, 1:2] * _unpack_bf16_pairs(y1_ref[...])
    out = h_ref[...] + gt_ref[...] * y
    o_ref[...] = _rms(out, g_ref[...]) if final else out


def _ffn_post(h, y, w, gate, n_ctx, final_g=None):
    bsz, t, d = h.shape
    tm = ROW_TILE
    seg = lambda b, i: (b, (i * tm >= n_ctx).astype(jnp.int32), 0, 0)
    row = lambda width: pl.BlockSpec((None, tm, width), lambda b, i: (b, i, 0))
    choice = lambda kk: pl.BlockSpec((None, None, tm, d // 2), lambda b, i: (kk, b, i, 0))
    final = final_g is not None
    g = final_g if final else jnp.ones((d,), F32)
    return pl.pallas_call(
        functools.partial(_ffn_post_kernel, final=final),
        grid=(bsz, t // tm),
        in_specs=[row(d), choice(0), choice(1), row(LANES), pl.BlockSpec((None, None, 1, d), seg),
                  pl.BlockSpec((1, d), lambda b, i: (0, 0))],
        out_specs=row(d),
        out_shape=jax.ShapeDtypeStruct((bsz, t, d), F32),
        compiler_params=_params("parallel", "parallel"),
        name="ffn_post",
    )(h, y, y, w, gate, g.reshape(1, d))


def _slot_layout(n, ri, counts):
    e = jnp.swapaxes(ri[:, 0:2], 0, 1).reshape(TOP_K, n)
    rank = jnp.swapaxes(ri[:, 2:4], 0, 1).reshape(TOP_K, n)
    padded = (counts + MOE_BLOCK - 1) // MOE_BLOCK * MOE_BLOCK
    pend = jnp.cumsum(padded)
    pstart = pend - padded
    experts = jnp.arange(N_EXPERTS, dtype=jnp.int32)
    dest = rank + jnp.sum(jnp.where(e[..., None] == experts, pstart, 0), axis=-1)
    n_slots = (n * TOP_K + MOE_BLOCK - 1) // MOE_BLOCK * MOE_BLOCK + N_EXPERTS * MOE_BLOCK
    n_blocks = n_slots // MOE_BLOCK
    blk0 = jnp.arange(n_blocks, dtype=jnp.int32)[:, None] * MOE_BLOCK
    block_e = jnp.minimum(jnp.sum((pend[None, :] <= blk0).astype(jnp.int32), axis=-1), N_EXPERTS - 1)
    n_used = (pend[-1] // MOE_BLOCK).astype(jnp.int32).reshape(1)
    return dest, n_slots, block_e.astype(jnp.int32), n_used


SC_CORES, SC_SUBCORES = 2, 16
SC_WINDOW = 32


def _gather_rows(table, idx):
    n_rows, d = idx.shape[0], table.shape[1]
    workers = SC_CORES * SC_SUBCORES
    per_worker = n_rows // workers
    assert per_worker * workers == n_rows and per_worker % SC_WINDOW == 0
    mesh = plsc.VectorSubcoreMesh(core_axis_name="c", subcore_axis_name="s")

    @functools.partial(
        pl.kernel, mesh=mesh,
        out_type=jax.ShapeDtypeStruct((n_rows, d), table.dtype),
        scratch_types=[pltpu.VMEM((SC_WINDOW,), jnp.int32), pltpu.VMEM((SC_WINDOW,), jnp.int32),
                       pltpu.VMEM((SC_WINDOW, d), table.dtype), pltpu.VMEM((SC_WINDOW, d), table.dtype),
                       pltpu.SemaphoreType.DMA, pltpu.SemaphoreType.DMA],
    )
    def gather_kernel(table_hbm, idx_hbm, out_hbm, idx0, idx1, rows0, rows1, sem0, sem1):
        base = (lax.axis_index("s") * SC_CORES + lax.axis_index("c")) * per_worker
        n_win = per_worker // SC_WINDOW
        slots = ((idx0, rows0, sem0), (idx1, rows1, sem1))
        window = lambda j: pl.ds(pl.multiple_of(base + j * SC_WINDOW, 8), SC_WINDOW)

        def start(j, slot):
            idx_v, rows_v, sem = slots[slot]
            pltpu.sync_copy(idx_hbm.at[window(j)], idx_v)
            pltpu.async_copy(table_hbm.at[idx_v], rows_v, sem)

        def finish(j, slot):
            idx_v, rows_v, sem = slots[slot]
            pltpu.make_async_copy(table_hbm.at[idx_v], rows_v, sem).wait()
            pltpu.sync_copy(rows_v, out_hbm.at[window(j)])

        start(0, 0)

        @pl.loop(0, n_win, step=2)
        def _(j):
            @pl.when(j + 1 < n_win)
            def _():
                start(j + 1, 1)

            finish(j, 0)

            @pl.when(j + 2 < n_win)
            def _():
                start(j + 2, 0)

            @pl.when(j + 1 < n_win)
            def _():
                finish(j + 1, 1)

    return gather_kernel(table, idx)


SC_SCATTER_WINDOW = 16


def _scatter_rows(src, dest, n_slots):
    n, d = src.shape
    workers = SC_CORES * SC_SUBCORES
    per_worker = n // workers
    win = SC_SCATTER_WINDOW
    assert per_worker * workers == n and per_worker % win == 0 and dest.shape == (TOP_K, n)
    mesh = plsc.VectorSubcoreMesh(core_axis_name="c", subcore_axis_name="s")

    @functools.partial(
        pl.kernel, mesh=mesh,
        out_type=jax.ShapeDtypeStruct((n_slots, d), src.dtype),
        scratch_types=[pltpu.VMEM((win,), jnp.int32)] * 4 + [pltpu.VMEM((win, d), src.dtype)] * 2
                      + [pltpu.SemaphoreType.DMA] * 6,
    )
    def scatter_kernel(src_hbm, dest_hbm, out_hbm, i0a, i1a, i0b, i1b, rows_a, rows_b,
                       load_a, first_a, second_a, load_b, first_b, second_b):
        base = (lax.axis_index("s") * SC_CORES + lax.axis_index("c")) * per_worker
        n_win = per_worker // win
        slots = ((i0a, i1a, rows_a, load_a, first_a, second_a), (i0b, i1b, rows_b, load_b, first_b, second_b))
        window = lambda j: pl.ds(pl.multiple_of(base + j * win, 8), win)

        def load(j, slot):
            idx0, idx1, rows_v, sem, _, _ = slots[slot]
            pltpu.async_copy(src_hbm.at[window(j)], rows_v, sem)
            pltpu.sync_copy(dest_hbm.at[0, window(j)], idx0)
            pltpu.sync_copy(dest_hbm.at[1, window(j)], idx1)

        def scatter(j, slot):
            idx0, idx1, rows_v, sem, sem0, sem1 = slots[slot]
            pltpu.make_async_copy(src_hbm.at[window(j)], rows_v, sem).wait()
            pltpu.async_copy(rows_v, out_hbm.at[idx0], sem0)
            pltpu.async_copy(rows_v, out_hbm.at[idx1], sem1)

        def drain(slot):
            idx0, idx1, rows_v, _, sem0, sem1 = slots[slot]
            pltpu.make_async_copy(rows_v, out_hbm.at[idx0], sem0).wait()
            pltpu.make_async_copy(rows_v, out_hbm.at[idx1], sem1).wait()

        load(0, 0)

        @pl.loop(0, n_win, step=2)
        def _(j):
            scatter(j, 0)

            @pl.when(j + 1 < n_win)
            def _():
                load(j + 1, 1)

            drain(0)

            @pl.when(j + 1 < n_win)
            def _():
                scatter(j + 1, 1)

            @pl.when(j + 2 < n_win)
            def _():
                load(j + 2, 0)

            @pl.when(j + 1 < n_win)
            def _():
                drain(1)

    return scatter_kernel(src, dest)


def _alongside(gather, idx, side_fn, side_in):
    idx, side_in = lax.optimization_barrier((idx, side_in))
    return lax.optimization_barrier((gather(idx), side_fn(side_in)))


def _moe(h, pre_out, gate, layer, w_gate, w_up, w_down, n_ctx, final_g=None, side=None):
    bsz, t, d = h.shape
    n = bsz * t
    v, ri, rwt, counts = pre_out
    dest, n_slots, block_e, n_used = _slot_layout(n, ri, counts[:, 0].astype(jnp.int32))
    dispatch = lambda idx: _scatter_rows(v.reshape(n, d // 2), idx, n_slots)
    if side is None:
        xb = dispatch(dest)
    else:
        xb, side_a = _alongside(dispatch, dest, *side[0])
    yb = _experts(xb, block_e, n_used, layer, w_gate, w_up, w_down)
    combine = lambda idx: _gather_rows(yb, idx)
    dest_flat = dest.reshape(-1)
    if side is None:
        y, side_b = combine(dest_flat), None
    else:
        y, side_b = _alongside(combine, dest_flat, side[1], side_a)
    out = _ffn_post(h, y.reshape(TOP_K, bsz, t, d // 2), rwt, gate, n_ctx, final_g)
    return out if side is None else (out, side_b)


DFT_STEP = 16


def _dft_tables(n):
    r, *mats = _dft_tables_np(n)
    return (r,) + tuple(jnp.asarray(a).astype(BF16) for a in mats)


@functools.lru_cache(maxsize=None)
def _dft_tables_np(n):
    size = 2 * n
    r = int(round(math.sqrt(size)))
    assert r * r == size and r % DFT_STEP == 0
    p1 = np.arange(r // 2)[None, None, :]
    p2 = np.arange(r)[:, None, None]
    k1 = np.arange(r)[None, :, None]
    ang = 2.0 * np.pi * (((r * p1 + p2) * k1) % size) / size
    g_re, g_im = np.cos(ang), -np.sin(ang)
    g_in = np.concatenate([g_re, g_im], axis=1)
    g_out = np.concatenate([np.swapaxes(g_re, 1, 2), np.swapaxes(g_im, 1, 2)], axis=2) / size
    a2 = 2.0 * np.pi * ((np.arange(r)[:, None] * np.arange(r)[None, :]) % r) / r
    f_re, f_im = np.cos(a2), -np.sin(a2)
    f_fwd = np.block([[f_re, -f_im], [f_im, f_re]])
    f_inv = np.block([[f_re, f_im], [-f_im, f_re]])
    p1f = np.arange(r)[None, None, :]
    angf = 2.0 * np.pi * (((r * p1f + p2) * k1) % size) / size
    g_full = np.concatenate([np.cos(angf), -np.sin(angf)], axis=1)
    return (r,) + tuple(a.astype(np.float32) for a in (g_in, g_out, f_fwd, f_inv, g_full))


def _dot_f32_tn(a, b):
    ah = a.astype(BF16)
    al = (a - ah.astype(F32)).astype(BF16)
    bh = b.astype(BF16)
    bl = (b - bh.astype(F32)).astype(BF16)
    return _dot_tn(ah, bh) + (_dot_tn(ah, bl) + _dot_tn(al, bh))


def _hy_filter_kernel(z_ref, t_ref, w1_ref, b1_ref, w2_ref, b2_ref, fr_ref, w3_ref, w3b_ref, rates_ref, o_ref,
                      *, half_tiles):
    i = pl.program_id(0)
    hid = jnp.sin(fr_ref[...] * (_dot_f32(w1_ref[...], z_ref[...]) + b1_ref[...]))
    hid = jnp.sin(fr_ref[...] * (_dot_f32(w2_ref[...], hid) + b2_ref[...]))
    filt = _dot_f32_tn(hid, w3_ref[...])
    decay = jnp.exp(-t_ref[...] * rates_ref[...])
    for o in range(o_ref.shape[0]):
        o_ref[o] = filt[:, o * HY_CH:(o + 1) * HY_CH] * decay

    @pl.when(i == 0)
    def _():
        extra = _dot_f32_tn(hid[:, 0:LANES], w3b_ref[...])[0:8]
        first = lax.broadcasted_iota(jnp.int32, (8, HY_CH), 0) == 0
        for o in range(o_ref.shape[0]):
            add = extra[:, o * HY_CH:(o + 1) * HY_CH] * decay[0:8]
            o_ref[o, 0:8, :] = o_ref[o, 0:8, :] + jnp.where(first, add, 0.0)

    @pl.when(i == half_tiles)
    def _():
        for o in range(o_ref.shape[0]):
            o_ref[o, 0:1, :] = jnp.zeros((1, HY_CH), F32)


HY_TILE = 512


def _hy_kernels(n, w1, b1, w2, b2, w3, freq):
    pos = np.arange(2 * n)
    pos = np.where(pos < n, pos, 2 * n - pos).astype(np.float32)
    t = jnp.asarray(pos / np.float32(n - 1))
    bands = jnp.linspace(1e-4, HY_BANDS - 1, HY_BANDS, dtype=F32)
    ang = (2.0 * math.pi / n) * bands[:, None] * jnp.asarray(pos)[None, :]
    z = jnp.concatenate([t[None, :], jnp.cos(ang), -jnp.sin(ang)], axis=0)
    z = jnp.pad(z, ((0, LANES - z.shape[0]), (0, 0)))
    w1t = jnp.pad(w1, ((0, LANES - w1.shape[0]), (0, 0))).T
    hidden = w1.shape[1]
    col = lambda v: v.reshape(hidden, 1)
    w3d = jnp.swapaxes(w3.reshape(hidden, HY_ORDER, 2, HY_CH), 0, 2)
    w3d = jnp.swapaxes(w3d, 1, 2).reshape(2, hidden, HY_ORDER * HY_CH)
    rates = jnp.abs(jnp.linspace(HY_MIN_DECAY, HY_MAX_DECAY, HY_CH, dtype=F32)).reshape(1, HY_CH)
    tm = HY_TILE
    half_tiles = n // tm
    full = lambda a: pl.BlockSpec(a.shape, lambda i: (0,) * a.ndim)
    small = (w1t, col(b1), w2.T, col(b2), col(freq))
    return pl.pallas_call(
        functools.partial(_hy_filter_kernel, half_tiles=half_tiles),
        grid=(2 * n // tm,),
        in_specs=[pl.BlockSpec((LANES, tm), lambda i: (0, i)), pl.BlockSpec((tm, 1), lambda i: (i, 0))]
                 + [full(a) for a in small]
                 + [pl.BlockSpec((None, hidden, HY_ORDER * HY_CH), lambda i: ((i >= half_tiles).astype(jnp.int32), 0, 0)),
                    pl.BlockSpec((None, hidden, HY_ORDER * HY_CH), lambda i: (1, 0, 0)), full(rates)],
        out_specs=pl.BlockSpec((HY_ORDER, tm, HY_CH), lambda i: (0, i, 0)),
        out_shape=jax.ShapeDtypeStruct((HY_ORDER, 2 * n, HY_CH), F32),
        compiler_params=_params("parallel"),
        name="hy_kernels",
    )(z, t[:, None], *small, w3d, w3d, rates)


def _pack_complex(z):
    r = z.shape[0] // 2
    bits = lax.bitcast_convert_type(z.astype(BF16).astype(F32), jnp.uint32)
    return lax.bitcast_convert_type(bits[0:r] | (bits[r:2 * r] >> 16), F32)


def _unpack_complex(words):
    p = lax.bitcast_convert_type(words, jnp.uint32)
    re = lax.bitcast_convert_type(p & jnp.uint32(0xFFFF0000), F32)
    im = lax.bitcast_convert_type(p << 16, F32)
    return jnp.concatenate([re, im], axis=0).astype(BF16)


def _load_every(ref, j, count):
    return ref.reshape(count * DFT_STEP, LANES)[pl.ds(j, count, stride=DFT_STEP), :]


def _store_every(ref, j, count, val):
    ref.reshape(count * DFT_STEP, LANES)[pl.ds(j, count, stride=DFT_STEP), :] = val


def _dft_in_kernel(x_ref, g_ref, a_ref):
    n_seq, rh = x_ref.shape[0:2]
    for j in range(DFT_STEP):
        x = jnp.concatenate([_load_every(x_ref.at[b], j, rh) for b in range(n_seq)], axis=1).astype(BF16)
        res = _dot(g_ref[j], x)
        for b in range(n_seq):
            a_ref[b, j] = _pack_complex(res[:, b * LANES:(b + 1) * LANES])


def _dft_in(x4, col, g_in):
    bx, rh, r, _ = x4.shape
    c = HY_CH
    cbs = c // LANES
    return pl.pallas_call(
        _dft_in_kernel,
        grid=(r // DFT_STEP, cbs),
        in_specs=[pl.BlockSpec((bx, rh, DFT_STEP, LANES), lambda i, cb: (0, 0, i, col * cbs + cb)),
                  pl.BlockSpec((DFT_STEP, 2 * r, rh), lambda i, cb: (i, 0, 0))],
        out_specs=pl.BlockSpec((bx, DFT_STEP, r, LANES), lambda i, cb: (0, i, 0, cb)),
        out_shape=jax.ShapeDtypeStruct((bx, r, r, c), F32),
        compiler_params=_params("parallel", "parallel"),
        name="dft_in",
    )(x4, g_in)


def _stage2_operand(a_ref, j, r):
    return jnp.concatenate([_unpack_complex(_load_every(a_ref.at[b], j, r)) for b in range(a_ref.shape[0])],
                           axis=1)


def _dft_filt_kernel(a_ref, f_ref, k_ref):
    r = f_ref.shape[0] // 2
    for j in range(DFT_STEP):
        s = _dot(f_ref[...], _stage2_operand(a_ref, j, r))
        for o in range(a_ref.shape[0]):
            k_ref[o, j] = s[:, o * LANES:(o + 1) * LANES]


def _dft_filt(a, f_fwd):
    nq, r, _, c = a.shape
    return pl.pallas_call(
        _dft_filt_kernel,
        grid=(r // DFT_STEP, c // LANES),
        in_specs=[pl.BlockSpec((nq, r, DFT_STEP, LANES), lambda i, cb: (0, 0, i, cb)),
                  pl.BlockSpec(f_fwd.shape, lambda i, cb: (0, 0))],
        out_specs=pl.BlockSpec((nq, DFT_STEP, 2 * r, LANES), lambda i, cb: (0, i, 0, cb)),
        out_shape=jax.ShapeDtypeStruct((nq, r, 2 * r, c), F32),
        compiler_params=_params("parallel", "parallel"),
        name="dft_filt",
    )(a, f_fwd)


def _dft_mid_kernel(a_ref, k_ref, ff_ref, fi_ref, b_ref):
    r = ff_ref.shape[0] // 2
    n_seq = a_ref.shape[0]
    for j in range(DFT_STEP):
        s = _dot(ff_ref[...], _stage2_operand(a_ref, j, r))
        sr, si = s[0:r], s[r:2 * r]
        kr = jnp.concatenate([k_ref[j, 0:r, :]] * n_seq, axis=1)
        ki = jnp.concatenate([k_ref[j, r:2 * r, :]] * n_seq, axis=1)
        p = jnp.concatenate([sr * kr - si * ki, sr * ki + si * kr], axis=0).astype(BF16)
        back = _dot(fi_ref[...], p)
        for b in range(n_seq):
            b_ref[b, j] = _pack_complex(back[:, b * LANES:(b + 1) * LANES])


def _dft_mid(a, kspec, order, f_fwd, f_inv):
    bsz, r, _, c = a.shape
    return pl.pallas_call(
        _dft_mid_kernel,
        grid=(r // DFT_STEP, c // LANES),
        in_specs=[pl.BlockSpec((bsz, r, DFT_STEP, LANES), lambda i, cb: (0, 0, i, cb)),
                  pl.BlockSpec((None, DFT_STEP, 2 * r, LANES), lambda i, cb: (order, i, 0, cb)),
                  pl.BlockSpec(f_fwd.shape, lambda i, cb: (0, 0)),
                  pl.BlockSpec(f_inv.shape, lambda i, cb: (0, 0))],
        out_specs=pl.BlockSpec((bsz, DFT_STEP, r, LANES), lambda i, cb: (0, i, 0, cb)),
        out_shape=jax.ShapeDtypeStruct((bsz, r, r, c), F32),
        compiler_params=_params("parallel", "parallel"),
        name="dft_mid",
    )(a, kspec, f_fwd, f_inv)


def _dft_out_kernel(b_ref, g_ref, u_ref, x_ref, bias_ref, o_ref):
    n_seq, r = b_ref.shape[0:2]
    rh = o_ref.shape[1]
    for j in range(DFT_STEP):
        rhs = jnp.concatenate([_unpack_complex(_load_every(b_ref.at[b], j, r)) for b in range(n_seq)], axis=1)
        y = _dot(g_ref[j], rhs)
        for b in range(n_seq):
            yb = y[:, b * LANES:(b + 1) * LANES]
            _store_every(o_ref.at[b], j, rh,
                         _load_every(x_ref.at[b], j, rh) * (yb + _load_every(u_ref.at[b], j, rh) * bias_ref[...]))


def _dft_out(bm, g_out, u4, u_col, x4, x_col, bias):
    bsz, r, _, c = bm.shape
    rh = r // 2
    cbs = c // LANES
    seq = lambda col: pl.BlockSpec((bsz, rh, DFT_STEP, LANES), lambda i, cb: (0, 0, i, col * cbs + cb))
    return pl.pallas_call(
        _dft_out_kernel,
        grid=(r // DFT_STEP, cbs),
        in_specs=[pl.BlockSpec((bsz, r, DFT_STEP, LANES), lambda i, cb: (0, 0, i, cb)),
                  pl.BlockSpec((DFT_STEP, rh, 2 * r), lambda i, cb: (i, 0, 0)),
                  seq(u_col), seq(x_col), pl.BlockSpec((1, LANES), lambda i, cb: (0, cb))],
        out_specs=seq(0),
        out_shape=jax.ShapeDtypeStruct((bsz, rh, r, c), F32),
        compiler_params=_params("parallel", "parallel"),
        name="dft_out",
    )(bm, g_out, u4, x4, bias.reshape(1, c))


def _hyena_filter_stage1(n, filter_params):
    r, g_full = _dft_tables(n)[0], _dft_tables(n)[5]
    kern = _hy_kernels(n, *filter_params)
    return _dft_in(kern.reshape(-1, r, r, HY_CH), 0, g_full)


def _hyena_filter_spectra(n, stage1):
    return _dft_filt(stage1, _dft_tables(n)[3])


def _hyena(hy_in, kspec, conv_bias):
    bsz, n, _ = hy_in.shape
    r, g_in, g_out, f_fwd, f_inv, _ = _dft_tables(n)
    seq4 = hy_in.reshape(bsz, r // 2, r, 3 * HY_CH)
    zz = _dft_out(_dft_mid(_dft_in(seq4, 0, g_in), kspec, 0, f_fwd, f_inv), g_out,
                  seq4, 0, seq4, 1, conv_bias[0])
    out = _dft_out(_dft_mid(_dft_in(zz, 0, g_in), kspec, 1, f_fwd, f_inv), g_out,
                   zz, 0, seq4, 2, conv_bias[1])
    return out.reshape(bsz, n, HY_CH)


def _reorder_ab(w):
    offs = np.cumsum([0, GLA_KEY_W, GLA_KEY_W, GLA_VAL_W, GLA_VAL_W, GLA_LOW_RANK, GLA_LOW_RANK,
                      HG_KEY_W, HG_KEY_W, HG_KEY_W, HG_VAL_W, HG_VAL_W]).tolist()
    gq, gk, gv, gg, lr_f, lr_b, hq, hf_f, hf_b, hi, hg = range(11)
    cols = lambda first, last: w[:, offs[first]:offs[last + 1]].astype(BF16)
    lr = jnp.pad(cols(lr_f, lr_b), ((0, 0), (0, AB_PAD_COLS - AB_LR - 2 * GLA_LOW_RANK)))
    return [cols(gq, gk), cols(gg, gg), cols(hq, hf_b), cols(hg, hg), lr], [cols(gv, gv), cols(hi, hi)]


def _reorder_cd(w):
    hy_end = 3 * HY_CH
    z_end = hy_end + MB_INNER
    xbc_end = z_end + MB_INNER + 2 * MB_BC_W
    pad = lambda a: jnp.pad(a, ((0, 0), (0, LANES - MB_HEADS)))
    dt = jnp.concatenate([pad(w[:, xbc_end:xbc_end + MB_HEADS]), pad(w[:, xbc_end + MB_HEADS:])], axis=-1)
    return [w[:, :hy_end].astype(BF16), w[:, z_end:xbc_end].astype(BF16), w[:, hy_end:z_end].astype(BF16),
            dt.astype(BF16)]


def kernel(x, c, ctx, c_ctx, ada_w, ada_b, norm_mix_g, norm_ffn_g, norm_out_g, ab_w_in, ab_w_out, gla_gate_w, gla_gate_b, gla_norm_g, hg_lb, hg_norm_g, cd_w_in, cd_w_out, hy_short_w, hy_short_b, hy_w1, hy_b1, hy_w2, hy_b2, hy_w3, hy_freq, hy_bias, mb_conv_w, mb_conv_b, mb_dt_bias, mb_a_log, mb_d, mb_norm_g, router_w, router_b, moe_w_gate, moe_w_up, moe_w_down):
    bsz, n_lat, d = x.shape
    n_ctx = ctx.shape[1]
    assert ada_w.shape[0] == 2 and ab_w_in.shape[0] == 1 and cd_w_in.shape[0] == 1

    cond = jnp.zeros((ADALN_ROWS, d), F32).at[:bsz].set(c).at[bsz].set(c_ctx)
    m = _adaln(cond, ada_w, ada_b)

    def mods(layer):
        lat = m[layer, :bsz].reshape(bsz, 6, d)
        cx = jnp.broadcast_to(m[layer, bsz].reshape(1, 6, d), (bsz, 6, d))
        both = jnp.stack([cx, lat], axis=1)
        return [both[:, :, j][:, :, None, :] for j in range(6)]

    router_w_pad = jnp.zeros((d, LANES), F32).at[:, :N_EXPERTS].set(router_w)

    sh_m, sc_m, gt_m, sh_f, sc_f, gt_f = mods(0)
    proj, vals = _norm_proj(ctx, x, norm_mix_g[0], sh_m, sc_m, *_reorder_ab(ab_w_in[0]))
    gwp = [jnp.zeros((LANES, GLA_KEY_W), F32).at[GLA_LOW_RANK * dd:GLA_LOW_RANK * (dd + 1)].set(gla_gate_w[0, dd])
           for dd in range(2)]
    o_gla = _gla_scan(proj, vals, gwp, [gla_gate_b[0, dd].reshape(1, -1) for dd in range(2)], n_ctx)
    o_hg = _hgrn_scan(proj, vals, [hg_lb[dd].astype(F32) for dd in range(2)], 0, n_ctx)
    h, pre_out = _mix_out_ab(ctx, x, o_gla, o_hg, proj, gla_norm_g[0], hg_norm_g[0], ab_w_out[0].astype(BF16),
                             gt_m, (norm_ffn_g[0], sh_f, sc_f, router_w_pad, router_b))
    filter_params = (hy_w1[0], hy_b1[0], hy_w2[0], hy_b2[0], hy_w3[0], hy_freq[0])
    side = ((functools.partial(_hyena_filter_stage1, n_lat), filter_params),
            functools.partial(_hyena_filter_spectra, n_lat))
    h, kspec = _moe(h, pre_out, gt_f, 0, moe_w_gate, moe_w_up, moe_w_down, n_ctx, side=side)

    sh_m, sc_m, gt_m, sh_f, sc_f, gt_f = mods(1)
    conv_w = jnp.concatenate([hy_short_w[0], mb_conv_w[0]], axis=0).T
    conv_b = jnp.concatenate([hy_short_b[0], mb_conv_b[0]]).reshape(1, -1)
    hy_in, xbc, zdt = _norm_proj_conv(h, norm_mix_g[1], sh_m, sc_m, _reorder_cd(cd_w_in[0]),
                                      conv_w, conv_b, 3 * HY_CH, n_ctx)
    hy = _hyena(hy_in, kspec, hy_bias[0])
    y_ssd = _ssd_scan(xbc, zdt, MB_INNER // LANES, mb_dt_bias[0], mb_a_log[0], n_ctx)
    d_skip_x = jnp.repeat(mb_d[0], MB_HEAD_DIM).reshape(1, MB_INNER)
    h, pre_out = _mix_out_cd(h, hy, y_ssd, xbc, zdt, 0, d_skip_x, mb_norm_g[0], cd_w_out[0].astype(BF16), gt_m,
                             n_ctx, (norm_ffn_g[1], sh_f, sc_f, router_w_pad, router_b))
    return _moe(h, pre_out, gt_f, 1, moe_w_gate, moe_w_up, moe_w_down, 0, final_g=norm_out_g)
```

```python
import functools
import math

import numpy as np
import jax
import jax.numpy as jnp
from jax import lax
from jax.experimental import pallas as pl
from jax.experimental.pallas import tpu as pltpu
from jax.experimental.pallas import tpu_sc as plsc

NORM_EPS = 1e-6
GLA_HEADS, GLA_DK, GLA_DV, GLA_LOW_RANK, GLA_TAU = 4, 64, 128, 16, 16.0
GLA_KEY_W, GLA_VAL_W = GLA_HEADS * GLA_DK, GLA_HEADS * GLA_DV
HG_HEADS, HG_EXPAND, HG_DV = 4, 128, 128
HG_KEY_W, HG_VAL_W = HG_HEADS * HG_EXPAND, HG_HEADS * HG_DV
HY_CH, HY_ORDER, HY_BANDS = 512, 2, 16
HY_MIN_DECAY = math.log(1e-2) / 1.5
HY_MAX_DECAY = math.log(1e-2) / 0.3
MB_HEADS, MB_HEAD_DIM, MB_GROUPS, MB_STATE = 8, 64, 2, 128
MB_INNER = MB_HEADS * MB_HEAD_DIM
MB_BC_W = MB_GROUPS * MB_STATE
N_EXPERTS, N_GROUPS, TOP_K, MOE_BLOCK = 16, 4, 2, 256
EXPERTS_PER_GROUP = N_EXPERTS // N_GROUPS

LANES = 128
SCAN_CHUNK = 64
SCAN_BLOCK = 256
SSD_CHUNK = 128
SSD_BLOCK = 256
ROW_TILE = 256
VMEM_LIMIT = 56 * 1024 * 1024

BF16 = jnp.bfloat16
F32 = jnp.float32


def _params(*sem):
    return pltpu.CompilerParams(dimension_semantics=sem, vmem_limit_bytes=VMEM_LIMIT)


def _split3(x):
    hi = x.astype(BF16)
    r1 = x - hi.astype(F32)
    mid = r1.astype(BF16)
    lo = (r1 - mid.astype(F32)).astype(BF16)
    return hi, mid, lo


def _dot(a, b):
    return jnp.dot(a, b, preferred_element_type=F32)


def _dot_nt(a, b):
    return lax.dot_general(a, b, (((1,), (1,)), ((), ())), preferred_element_type=F32)


def _dot_tn(a, b):
    return lax.dot_general(a, b, (((0,), (0,)), ((), ())), preferred_element_type=F32)


def _dot_f32(a, b):
    ah = a.astype(BF16)
    al = (a - ah.astype(F32)).astype(BF16)
    bh = b.astype(BF16)
    bl = (b - bh.astype(F32)).astype(BF16)
    return _dot(ah, bh) + (_dot(ah, bl) + _dot(al, bh))


def _silu(x):
    return x * (1.0 / (1.0 + jnp.exp(-x)))


def _sigmoid(x):
    return 1.0 / (1.0 + jnp.exp(-x))


def _softplus(x):
    return jnp.maximum(x, 0.0) + jnp.log(1.0 + jnp.exp(-jnp.abs(x)))


def _pack_bf16_pairs(x):
    bits = lax.bitcast_convert_type(x.astype(BF16).astype(F32), jnp.uint32)
    half = x.shape[1] // 2
    return bits[:, :half] | (bits[:, half:] >> 16)


def _unpack_bf16_pairs(p):
    hi = lax.bitcast_convert_type(p & jnp.uint32(0xFFFF0000), F32)
    lo = lax.bitcast_convert_type(p << 16, F32)
    return jnp.concatenate([hi, lo], axis=1)


def _rms(x, g):
    return x * lax.rsqrt(jnp.mean(x * x, axis=-1, keepdims=True) + NORM_EPS) * g


def _adaln_kernel(c_ref, w_ref, b_ref, o_ref):
    o_ref[...] = _dot_f32(_silu(c_ref[...]), w_ref[...]) + b_ref[...]


ADALN_ROWS = 8
ADALN_TILE = 1536


def _adaln(cond, w, b):
    n_l, d, n6 = w.shape
    tn = ADALN_TILE
    rows = cond.shape[0]
    return pl.pallas_call(
        _adaln_kernel,
        grid=(n_l, n6 // tn),
        in_specs=[pl.BlockSpec((rows, d), lambda l, j: (0, 0)),
                  pl.BlockSpec((None, d, tn), lambda l, j: (l, 0, j)),
                  pl.BlockSpec((None, 1, tn), lambda l, j: (l, 0, j))],
        out_specs=pl.BlockSpec((None, rows, tn), lambda l, j: (l, 0, j)),
        out_shape=jax.ShapeDtypeStruct((n_l, rows, n6), F32),
        compiler_params=_params("parallel", "parallel"),
        name="adaln",
    )(cond, w, b.reshape(n_l, 1, n6))


def _project(u, w_refs):
    return jnp.concatenate([_dot(u, w_ref[...]) for w_ref in w_refs], axis=-1)


def _joint_rows_specs(ctx, x):
    tm = ROW_TILE
    ctx_tiles = ctx.shape[1] // tm
    d = ctx.shape[2]
    return [pl.BlockSpec((None, tm, d), lambda b, i: (b, jnp.minimum(i, ctx_tiles - 1), 0)),
            pl.BlockSpec((None, tm, d), lambda b, i: (b, jnp.maximum(i - ctx_tiles, 0), 0))], ctx_tiles


def _joint_rows(ctx_ref, x_ref, ctx_tiles):
    return jnp.where(pl.program_id(1) < ctx_tiles, ctx_ref[...], x_ref[...])


def _norm_proj_kernel(c_ref, x_ref, g_ref, sh_ref, sc_ref, *refs, ctx_tiles, n_main):
    u = _rms(_joint_rows(c_ref, x_ref, ctx_tiles), g_ref[...]) * (1.0 + sc_ref[...]) + sh_ref[...]
    u = u.astype(BF16)
    refs[-2][...] = _project(u, refs[:n_main])
    refs[-1][...] = _project(u, refs[n_main:-2]).astype(BF16)


def _norm_proj(ctx, x, g, shift, scale, ws, ws_bf16):
    bsz, n_ctx, d = ctx.shape
    t = n_ctx + x.shape[1]
    n, n_bf = sum(w.shape[1] for w in ws), sum(w.shape[1] for w in ws_bf16)
    tm = ROW_TILE
    seg = lambda b, i: (b, (i * tm >= n_ctx).astype(jnp.int32), 0, 0)
    row_specs, ctx_tiles = _joint_rows_specs(ctx, x)
    return pl.pallas_call(
        functools.partial(_norm_proj_kernel, ctx_tiles=ctx_tiles, n_main=len(ws)),
        grid=(bsz, t // tm),
        in_specs=row_specs + [pl.BlockSpec((1, d), lambda b, i: (0, 0)),
                              pl.BlockSpec((None, None, 1, d), seg),
                              pl.BlockSpec((None, None, 1, d), seg)]
                 + [pl.BlockSpec(w.shape, lambda b, i: (0, 0)) for w in list(ws) + list(ws_bf16)],
        out_specs=[pl.BlockSpec((None, tm, n), lambda b, i: (b, i, 0)),
                   pl.BlockSpec((None, tm, n_bf), lambda b, i: (b, i, 0))],
        out_shape=[jax.ShapeDtypeStruct((bsz, t, n), F32), jax.ShapeDtypeStruct((bsz, t, n_bf), BF16)],
        compiler_params=_params("parallel", "parallel"),
        name="norm_proj",
    )(ctx, x, g.reshape(1, d), shift, scale, *ws, *ws_bf16)


def _scan_constants(c, reverse):
    t = np.arange(c)[:, None]
    u = np.arange(c)[None, :]
    sels = [u <= t, u > t]
    masks = []
    m = c // 2
    while m >= 1:
        blk = t // (2 * m)
        upper_t = (t % (2 * m)) >= m
        r = blk * (2 * m) + m - 1
        s_blk = u // (2 * m)
        upper_s = (u % (2 * m)) >= m
        sels.append((upper_t & (u > r) & (u <= t)) | ((~upper_t) & (u > t) & (u <= r)))
        masks.append((blk == s_blk) & upper_t & (~upper_s))
        m //= 2
    masks.append(t == u)
    sel = np.stack(sels).astype(np.float32)
    msk = np.stack(masks).astype(np.float32)
    if reverse:
        sel = sel[:, ::-1, ::-1]
        msk = msk[:, ::-1, ::-1]
    return np.ascontiguousarray(sel.reshape(-1, c)), np.ascontiguousarray(msk)


def _chunk_order(i, n_ctx_chunks, n_chunks, reverse):
    if not reverse:
        return i
    return jnp.where(i < n_ctx_chunks, n_ctx_chunks - 1 - i, n_chunks - 1 - (i - n_ctx_chunks))


GROUP_KEYS = 256


def _decay_chunk(q, k, v, la, consts, st_ref, heads, dk, dv):
    sel_ref, mask_ref, hm_ref, hmb_ref, vm_ref = consts
    c = q.shape[0]
    n_lvl = mask_ref.shape[0] - 1
    hpg = GROUP_KEYS // dk
    cs = _dot(sel_ref[...], jnp.concatenate(_split3(la), axis=0))
    e_q = jnp.exp(cs[0:c])
    e_k = jnp.exp(cs[c:2 * c])
    e_tot = jnp.exp(jnp.sum(la, axis=0, keepdims=True))
    vb = v.astype(BF16)
    outs = []
    for g in range(heads // hpg):
        ks = slice(g * GROUP_KEYS, (g + 1) * GROUP_KEYS)
        vs = slice(g * hpg * dv, (g + 1) * hpg * dv)
        qg, kg = q[:, ks], k[:, ks]
        key_stack = lambda x: jnp.concatenate([x.astype(BF16) * hmb_ref[h] for h in range(hpg)], axis=0)
        att = mask_ref[n_lvl] * _dot_nt(qg.astype(BF16), key_stack(kg))
        for l in range(n_lvl):
            e = jnp.exp(cs[(2 + l) * c:(3 + l) * c, ks])
            att = att + mask_ref[l] * _dot_nt((qg * e).astype(BF16), key_stack(kg * e))
        v_blocks = jnp.concatenate([vb[:, vs] * vm_ref[h] for h in range(hpg)], axis=0)
        intra = _dot(att.astype(BF16), v_blocks)
        st = st_ref[g]
        q_stack = jnp.concatenate([(qg * e_q[:, ks]) * hm_ref[h] for h in range(hpg)], axis=0)
        inter = _dot_nt(q_stack.astype(BF16), st.astype(BF16))
        upd = _dot_tn(vb[:, vs], (kg * e_k[:, ks]).astype(BF16))
        new = st * e_tot[:, ks]
        for h in range(hpg):
            new = new + upd[h * dv:(h + 1) * dv] * hm_ref[h]
        st_ref[g] = new
        outs.append(intra + jnp.concatenate([inter[h * c:(h + 1) * c] for h in range(hpg)], axis=-1))
    return jnp.concatenate(outs, axis=-1)


def _log_sigmoid(x):
    return jnp.minimum(x, 0.0) - jnp.log(1.0 + jnp.exp(-jnp.abs(x)))


def _gla_kernel(*refs):
    ins, head_masks, (o_refs, st_refs) = (refs[0:8], refs[8:16]), refs[16:19], (refs[19:21], refs[21:23])

    @pl.when(pl.program_id(1) == 0)
    def _():
        for st_ref in st_refs:
            st_ref[...] = jnp.zeros_like(st_ref)

    for d, ((q_ref, k_ref, v_ref, lr_ref, gw_ref, gb_ref, sel_ref, mask_ref), o_ref, st_ref) in enumerate(
            zip(ins, o_refs, st_refs)):
        z = _dot_f32(lr_ref[...], gw_ref[...]) + gb_ref[...]
        la = _log_sigmoid(z) * (1.0 / GLA_TAU)
        q = q_ref[...] * (GLA_DK ** -0.5)
        k, v = k_ref[...], v_ref[...]
        for rows in _sub_chunks(q.shape[0], d == 1):
            o_ref[rows, :] = _decay_chunk(q[rows], k[rows], v[rows], la[rows], (sel_ref, mask_ref) + head_masks,
                                          st_ref, GLA_HEADS, GLA_DK, GLA_DV).astype(o_ref.dtype)


def _hgrn_kernel(*refs, layer):
    ins, head_masks, (o_refs, st_refs) = (refs[0:6], refs[6:12]), refs[12:15], (refs[15:17], refs[17:19])

    @pl.when(pl.program_id(1) == 0)
    def _():
        for st_ref in st_refs:
            st_ref[...] = jnp.zeros_like(st_ref)

    for d, ((q_ref, f_ref, v_ref, lb_ref, sel_ref, mask_ref), o_ref, st_ref) in enumerate(
            zip(ins, o_refs, st_refs)):
        e = jnp.exp(lb_ref[...] - jnp.max(lb_ref[...], axis=0, keepdims=True))
        lb = jnp.sum(e[0:layer + 1], axis=0, keepdims=True) / jnp.sum(e, axis=0, keepdims=True)
        f = lb + (1.0 - lb) * _sigmoid(f_ref[...])
        q, k, v, la = _silu(q_ref[...]), 1.0 - f, v_ref[...], jnp.log(f)
        for rows in _sub_chunks(q.shape[0], d == 1):
            o_ref[rows, :] = _decay_chunk(q[rows], k[rows], v[rows], la[rows], (sel_ref, mask_ref) + head_masks,
                                          st_ref, HG_HEADS, HG_EXPAND, HG_DV).astype(o_ref.dtype)


def _sub_chunks(rows, reverse, chunk=SCAN_CHUNK):
    order = range(rows // chunk)
    return [slice(j * chunk, (j + 1) * chunk) for j in (reversed(order) if reverse else order)]


def _scan_specs(blk, n_ctx, t, reverse, chunk=None, stacked_heads=1):
    n_blocks = t // blk
    order = functools.partial(_chunk_order, n_ctx_chunks=n_ctx // blk, n_chunks=n_blocks, reverse=reverse)

    def col(width, idx):
        return pl.BlockSpec((None, blk, width), lambda b, i: (b, order(i), idx))

    sel, msk = _scan_constants(chunk or blk, reverse)
    sel3 = np.concatenate([sel, sel, sel], axis=1)
    msk = np.tile(msk, (1, 1, stacked_heads))
    const = lambda a: pl.BlockSpec(a.shape, lambda b, i: (0,) * a.ndim)
    return n_blocks, col, const, jnp.asarray(sel3, BF16), jnp.asarray(msk, F32)


def _head_masks(dk, dv):
    hpg = GROUP_KEYS // dk
    hm = np.zeros((hpg, 1, GROUP_KEYS), np.float32)
    vm = np.zeros((hpg, 1, hpg * dv), np.float32)
    for h in range(hpg):
        hm[h, 0, h * dk:(h + 1) * dk] = 1.0
        vm[h, 0, h * dv:(h + 1) * dv] = 1.0
    return jnp.asarray(hm), jnp.asarray(hm, BF16), jnp.asarray(vm, BF16)


AB_Q, AB_K, AB_G = 0, 256, 512
AB_HQ, AB_HF, AB_HG, AB_LR = 1024, 1536, 2560, 3072
AB_PAD_COLS = 3200
AB_V, AB_HI = 0, 1


def _gla_scan(proj, vals, gate_w_pad, gate_b, n_ctx):
    bsz, t, _ = proj.shape
    hpg = GROUP_KEYS // GLA_DK
    in_specs, args, outs = [], [], []
    for d in range(2):
        n_blocks, col, const, sel, msk = _scan_specs(SCAN_BLOCK, n_ctx, t, d == 1, SCAN_CHUNK, hpg)
        in_specs += [col(GLA_KEY_W, AB_Q // GLA_KEY_W), col(GLA_KEY_W, AB_K // GLA_KEY_W),
                     col(GLA_VAL_W, AB_V), col(LANES, AB_LR // LANES),
                     const(gate_w_pad[d]), const(gate_b[d]), const(sel), const(msk)]
        args += [proj, proj, vals, proj, gate_w_pad[d], gate_b[d], sel, msk]
        outs.append(col(GLA_VAL_W, 0))
    hm = _head_masks(GLA_DK, GLA_DV)
    return pl.pallas_call(
        _gla_kernel,
        grid=(bsz, n_blocks),
        in_specs=in_specs + [const(m) for m in hm],
        out_specs=outs,
        out_shape=[jax.ShapeDtypeStruct((bsz, t, GLA_VAL_W), BF16)] * 2,
        scratch_shapes=[pltpu.VMEM((GLA_HEADS // hpg, GLA_DV, GROUP_KEYS), F32)] * 2,
        compiler_params=_params("parallel", "arbitrary"),
        name="gla_scan",
    )(*args, *hm)


def _hgrn_scan(proj, vals, lb, layer, n_ctx):
    bsz, t, _ = proj.shape
    hpg = GROUP_KEYS // HG_EXPAND
    in_specs, args, outs = [], [], []
    for d in range(2):
        n_blocks, col, const, sel, msk = _scan_specs(SCAN_BLOCK, n_ctx, t, d == 1, SCAN_CHUNK, hpg)
        in_specs += [col(HG_KEY_W, AB_HQ // HG_KEY_W), col(HG_KEY_W, AB_HF // HG_KEY_W + d),
                     col(HG_VAL_W, AB_HI), const(lb[d]), const(sel), const(msk)]
        args += [proj, proj, vals, lb[d], sel, msk]
        outs.append(col(HG_VAL_W, 0))
    hm = _head_masks(HG_EXPAND, HG_DV)
    return pl.pallas_call(
        functools.partial(_hgrn_kernel, layer=layer),
        grid=(bsz, n_blocks),
        in_specs=in_specs + [const(m) for m in hm],
        out_specs=outs,
        out_shape=[jax.ShapeDtypeStruct((bsz, t, HG_VAL_W), BF16)] * 2,
        scratch_shapes=[pltpu.VMEM((HG_HEADS // hpg, HG_DV, GROUP_KEYS), F32)] * 2,
        compiler_params=_params("parallel", "arbitrary"),
        name="hgrn_scan",
    )(*args, *hm)


def _mix_out_ab_kernel(*refs, ctx_tiles):
    (c_ref, x_ref, gf_ref, gb_ref, hf_ref, hb_ref, gg_ref, hg_ref, gn_ref, hn_ref, w_ref, gt_ref) = refs[0:12]
    pre_in, o_ref, pre_out = refs[12:12 + N_FFN_PRE_IN], refs[12 + N_FFN_PRE_IN], refs[13 + N_FFN_PRE_IN:]
    feats = []
    both = lambda fwd_ref, bwd_ref: fwd_ref[...].astype(F32) + bwd_ref[...].astype(F32)
    for o, gate, g in ((both(gf_ref, gb_ref), gg_ref[...], gn_ref[...]),
                       (both(hf_ref, hb_ref), hg_ref[...], hn_ref[...])):
        for hd in range(o.shape[-1] // LANES):
            s = slice(hd * LANES, (hd + 1) * LANES)
            feats.append(_rms(o[:, s], g) * _silu(gate[:, s]))
    feat = jnp.concatenate(feats, axis=-1).astype(BF16)
    h_new = _joint_rows(c_ref, x_ref, ctx_tiles) + gt_ref[...] * _dot(feat, w_ref[...])
    o_ref[...] = h_new
    _ffn_pre_body(h_new, *pre_in, *pre_out)


def _mix_out_ab(ctx, x, o_gla, o_hg, proj, gla_norm_g, hg_norm_g, w_out, gate, pre):
    bsz, n_ctx, d = ctx.shape
    t = n_ctx + x.shape[1]
    tm = ROW_TILE
    seg = lambda b, i: (b, (i * tm >= n_ctx).astype(jnp.int32), 0, 0)
    row = lambda width, idx: pl.BlockSpec((None, tm, width), lambda b, i: (b, i, idx))
    vec = pl.BlockSpec((1, LANES), lambda b, i: (0, 0))
    p_in, p_args, p_out, p_shape, p_scratch = _ffn_pre_parts(bsz, t, d, *pre, seg)
    row_specs, ctx_tiles = _joint_rows_specs(ctx, x)
    outs = pl.pallas_call(
        functools.partial(_mix_out_ab_kernel, ctx_tiles=ctx_tiles),
        grid=(bsz, t // tm),
        in_specs=row_specs + [row(GLA_VAL_W, 0), row(GLA_VAL_W, 0), row(HG_VAL_W, 0), row(HG_VAL_W, 0),
                  row(GLA_VAL_W, AB_G // GLA_VAL_W), row(HG_VAL_W, AB_HG // HG_VAL_W), vec, vec,
                  pl.BlockSpec(w_out.shape, lambda b, i: (0, 0)),
                  pl.BlockSpec((None, None, 1, d), seg)] + p_in,
        out_specs=[row(d, 0)] + p_out,
        out_shape=[jax.ShapeDtypeStruct((bsz, t, d), F32)] + p_shape,
        scratch_shapes=p_scratch,
        compiler_params=_params("arbitrary", "arbitrary"),
        name="mix_out_ab",
    )(ctx, x, o_gla[0], o_gla[1], o_hg[0], o_hg[1], proj, proj, gla_norm_g.reshape(1, -1),
      hg_norm_g.reshape(1, -1), w_out, gate, *p_args)
    return outs[0], outs[1:]


HALO = 8


def _norm_proj_conv_kernel(h_ref, hp_ref, hn_ref, g_ref, sh_ref, sc_ref, cw_ref, cb_ref, *refs,
                           ctx_tiles, n_tiles):
    w_refs, (hy_ref, xbc_ref, zdt_ref) = refs[:-3], refs[-3:]
    i = pl.program_id(1)
    tm = h_ref.shape[0]
    hh = jnp.concatenate([hp_ref[...], h_ref[...], hn_ref[...]], axis=0)
    u = _rms(hh, g_ref[...]) * (1.0 + sc_ref[...]) + sh_ref[...]
    p = _project(u.astype(BF16), w_refs)
    n_conv = cw_ref.shape[1]
    n_hy = hy_ref.shape[1]
    pc = p[:, 0:n_conv]
    rows_all = tm + 2 * HALO
    cur = pc[HALO:HALO + tm]
    prev = pltpu.roll(pc, 1, axis=0)[HALO:HALO + tm]
    nxt = pltpu.roll(pc, rows_all - 1, axis=0)[HALO:HALO + tm]
    first = jnp.logical_or(i == 0, i == ctx_tiles)
    last = jnp.logical_or(i == ctx_tiles - 1, i == n_tiles - 1)
    rows = lax.broadcasted_iota(jnp.int32, cur.shape, 0)
    prev = jnp.where(jnp.logical_and(first, rows == 0), 0.0, prev)
    nxt = jnp.where(jnp.logical_and(last, rows == tm - 1), 0.0, nxt)
    y = prev * cw_ref[0:1, :] + cur * cw_ref[1:2, :] + nxt * cw_ref[2:3, :] + cb_ref[...]
    hy_ref[...] = y[:, 0:n_hy]
    xbc_ref[...] = _silu(y[:, n_hy:n_conv])
    zdt_ref[...] = p[HALO:HALO + tm, n_conv:]


def _norm_proj_conv(h, g, shift, scale, ws, conv_w, conv_b, n_hy, n_ctx):
    bsz, t, d = h.shape
    n = sum(w.shape[1] for w in ws)
    n_conv = conv_w.shape[1]
    tm = ROW_TILE
    n_tiles, ctx_tiles = t // tm, n_ctx // tm
    r8 = tm // HALO
    last8 = t // HALO - 1
    seg = lambda b, i: (b, (i >= ctx_tiles).astype(jnp.int32), 0, 0)
    kern = functools.partial(_norm_proj_conv_kernel, ctx_tiles=ctx_tiles, n_tiles=n_tiles)
    return pl.pallas_call(
        kern,
        grid=(bsz, n_tiles),
        in_specs=[pl.BlockSpec((None, tm, d), lambda b, i: (b, i, 0)),
                  pl.BlockSpec((None, HALO, d), lambda b, i: (b, jnp.maximum(i * r8 - 1, 0), 0)),
                  pl.BlockSpec((None, HALO, d), lambda b, i: (b, jnp.minimum((i + 1) * r8, last8), 0)),
                  pl.BlockSpec((1, d), lambda b, i: (0, 0)),
                  pl.BlockSpec((None, None, 1, d), seg),
                  pl.BlockSpec((None, None, 1, d), seg),
                  pl.BlockSpec((3, n_conv), lambda b, i: (0, 0)),
                  pl.BlockSpec((1, n_conv), lambda b, i: (0, 0))]
                 + [pl.BlockSpec(w.shape, lambda b, i: (0, 0)) for w in ws],
        out_specs=[pl.BlockSpec((None, tm, n_hy), lambda b, i: (b, jnp.maximum(i - ctx_tiles, 0), 0)),
                   pl.BlockSpec((None, tm, n_conv - n_hy), lambda b, i: (b, i, 0)),
                   pl.BlockSpec((None, tm, n - n_conv), lambda b, i: (b, i, 0))],
        out_shape=[jax.ShapeDtypeStruct((bsz, t - n_ctx, n_hy), F32),
                   jax.ShapeDtypeStruct((bsz, t, n_conv - n_hy), F32),
                   jax.ShapeDtypeStruct((bsz, t, n - n_conv), F32)],
        compiler_params=_params("parallel", "arbitrary"),
        name="norm_proj_conv",
    )(h, h, h, g.reshape(1, d), shift, scale, conv_w, conv_b, *ws)


def _ssd_kernel(*refs):
    ins, hexp_ref, o_refs, st_refs = (refs[0:7], refs[7:14]), refs[14], refs[15:17], refs[17:19]

    @pl.when(pl.program_id(1) == 0)
    def _():
        for st_ref in st_refs:
            st_ref[...] = jnp.zeros_like(st_ref)

    for d, ((xbc_ref, dt_ref, bias_ref, alog_ref, mq_ref, mk_ref, mask_ref), o_ref, st_ref) in enumerate(
            zip(ins, o_refs, st_refs)):
        for rows in _sub_chunks(xbc_ref.shape[0], d == 1, SSD_CHUNK):
            _ssd_chunk(rows, xbc_ref, dt_ref, bias_ref, alog_ref, hexp_ref, mq_ref, mk_ref, mask_ref, o_ref, st_ref)


def _ssd_chunk(rows, xbc_ref, dt_ref, bias_ref, alog_ref, hexp_ref, mq_ref, mk_ref, mask_ref, o_ref, st_ref):
    c = SSD_CHUNK
    hpg = MB_HEADS // MB_GROUPS
    gw = hpg * MB_HEAD_DIM
    dt = _softplus(dt_ref[rows, :] + bias_ref[...])
    la = -dt * jnp.exp(alog_ref[...])
    la3 = jnp.concatenate(_split3(la), axis=0)
    cq = _dot(mq_ref[...], la3)
    ck = _dot(mk_ref[...], la3)
    cq_t = lax.dot_general(la3, mq_ref[...], (((0,), (1,)), ((), ())), preferred_element_type=F32)
    tot = jnp.broadcast_to(jnp.sum(la, axis=0, keepdims=True), (8, la.shape[1]))
    per_head = jnp.concatenate([dt, cq, ck, tot], axis=0)
    per_lane = _dot(jnp.concatenate(_split3(per_head), axis=1), hexp_ref[...])
    dt_x = per_lane[0:c]
    eq_x = jnp.exp(per_lane[c:2 * c])
    ek_x = jnp.exp(per_lane[2 * c:3 * c])
    etot_x = jnp.exp(per_lane[3 * c:3 * c + 1])
    xs = xbc_ref[rows, 0:MB_INNER] * dt_x
    mask = mask_ref[...]
    outs = []
    for g in range(MB_GROUPS):
        bm = xbc_ref[rows, MB_INNER + g * MB_STATE:MB_INNER + (g + 1) * MB_STATE].astype(BF16)
        cm = xbc_ref[rows, MB_INNER + MB_BC_W + g * MB_STATE:MB_INNER + MB_BC_W + (g + 1) * MB_STATE].astype(BF16)
        cb = _dot_nt(cm, bm)
        st = st_ref[g]
        gs = slice(g * gw, (g + 1) * gw)
        y_inter = _dot(cm, st.astype(BF16)) * eq_x[:, gs]
        for r in range(hpg):
            hd = g * hpg + r
            diff = cq[:, hd:hd + 1] - cq_t[hd:hd + 1, :]
            w = cb * jnp.exp(jnp.where(mask > 0.0, diff, -jnp.inf))
            ps = slice(hd * MB_HEAD_DIM, (hd + 1) * MB_HEAD_DIM)
            outs.append(_dot(w.astype(BF16), xs[:, ps].astype(BF16))
                        + y_inter[:, r * MB_HEAD_DIM:(r + 1) * MB_HEAD_DIM])
        st_ref[g] = st * etot_x[:, gs] + _dot_tn(bm, (xs[:, gs] * ek_x[:, gs]).astype(BF16))
    o_ref[rows, :] = jnp.concatenate(outs, axis=-1).astype(o_ref.dtype)


def _ssd_scan(xbc, proj, dt_col, dt_bias, a_log, n_ctx):
    bsz, t, _ = xbc.shape
    c = SSD_CHUNK
    pad = lambda v: jnp.zeros((1, LANES), F32).at[0, :MB_HEADS].set(v)
    hexp = np.zeros((LANES, MB_INNER), np.float32)
    for hd in range(MB_HEADS):
        hexp[hd, hd * MB_HEAD_DIM:(hd + 1) * MB_HEAD_DIM] = 1.0
    hexp = jnp.asarray(np.concatenate([hexp, hexp, hexp], axis=0), BF16)
    tri = np.tril(np.ones((c, c), np.float32))
    in_specs, args, outs = [], [], []
    for d in range(2):
        n_chunks, col, const, sel, _ = _scan_specs(SSD_BLOCK, n_ctx, t, d == 1, c)
        mq, mk = sel[0:c], sel[c:2 * c]
        mask = jnp.asarray(tri[::-1, ::-1].copy() if d == 1 else tri)
        bias, alog = pad(dt_bias[d]), pad(a_log[d].astype(F32))
        in_specs += [col(xbc.shape[-1], 0), col(LANES, dt_col + d), const(bias), const(alog),
                     const(mq), const(mk), const(mask)]
        args += [xbc, proj, bias, alog, mq, mk, mask]
        outs.append(col(MB_INNER, 0))
    return pl.pallas_call(
        _ssd_kernel,
        grid=(bsz, n_chunks),
        in_specs=in_specs + [const(hexp)],
        out_specs=outs,
        out_shape=[jax.ShapeDtypeStruct((bsz, t, MB_INNER), BF16)] * 2,
        scratch_shapes=[pltpu.VMEM((MB_GROUPS, MB_STATE, MB_INNER // MB_GROUPS), F32)] * 2,
        compiler_params=_params("parallel", "arbitrary"),
        name="ssd_scan",
    )(*args, hexp)


def _mix_out_cd_kernel(*refs):
    h_ref, hy_ref, yf_ref, yb_ref, xs_ref, z_ref, dsk_ref, ng_ref, w_ref, gt_ref = refs[0:10]
    pre_in, o_ref, pre_out = refs[10:10 + N_FFN_PRE_IN], refs[10 + N_FFN_PRE_IN], refs[11 + N_FFN_PRE_IN:]
    y = (yf_ref[...].astype(F32) + yb_ref[...].astype(F32) + dsk_ref[...] * xs_ref[...]) * _silu(z_ref[...])
    gw = MB_INNER // MB_GROUPS
    ys = [_rms(y[:, g * gw:(g + 1) * gw], ng_ref[:, g * gw:(g + 1) * gw]) for g in range(MB_GROUPS)]
    feat = jnp.concatenate([hy_ref[...]] + ys, axis=-1).astype(BF16)
    h_new = h_ref[...] + gt_ref[...] * _dot(feat, w_ref[...])
    o_ref[...] = h_new
    _ffn_pre_body(h_new, *pre_in, *pre_out)


def _mix_out_cd(h, hy, y_ssd, xbc, proj, z_col, d_skip_x, norm_g, w_out, gate, n_ctx, pre):
    bsz, t, d = h.shape
    tm = ROW_TILE
    n_lat = t - n_ctx
    off = n_ctx // tm
    row = lambda width, idx: pl.BlockSpec((None, tm, width), lambda b, i: (b, i + off, idx))
    vec = pl.BlockSpec((1, MB_INNER), lambda b, i: (0, 0))
    latent = lambda b, i: (b, 1, 0, 0)
    p_in, p_args, p_out, p_shape, p_scratch = _ffn_pre_parts(bsz, n_lat, d, *pre, latent)
    outs = pl.pallas_call(
        _mix_out_cd_kernel,
        grid=(bsz, n_lat // tm),
        in_specs=[row(d, 0), pl.BlockSpec((None, tm, HY_CH), lambda b, i: (b, i, 0)),
                  row(MB_INNER, 0), row(MB_INNER, 0), row(MB_INNER, 0), row(MB_INNER, z_col), vec, vec,
                  pl.BlockSpec(w_out.shape, lambda b, i: (0, 0)),
                  pl.BlockSpec((None, None, 1, d), latent)] + p_in,
        out_specs=[pl.BlockSpec((None, tm, d), lambda b, i: (b, i, 0))] + p_out,
        out_shape=[jax.ShapeDtypeStruct((bsz, n_lat, d), F32)] + p_shape,
        scratch_shapes=p_scratch,
        compiler_params=_params("arbitrary", "arbitrary"),
        name="mix_out_cd",
    )(h, hy, y_ssd[0], y_ssd[1], xbc, proj, d_skip_x, norm_g.reshape(1, -1), w_out, gate, *p_args)
    return outs[0], outs[1:]


def _top2_of4(a, b, c, d):
    hi1, lo1, hi2, lo2 = jnp.maximum(a, b), jnp.minimum(a, b), jnp.maximum(c, d), jnp.minimum(c, d)
    return jnp.maximum(hi1, hi2) + jnp.maximum(jnp.minimum(hi1, hi2), jnp.maximum(lo1, lo2))


def _first_argmax(vals, skip=None):
    idx = None
    for j, vj in enumerate(vals):
        if idx is None and skip is None:
            idx, best = jnp.zeros(vj.shape, jnp.int32), vj
            continue
        if idx is None:
            idx, best = jnp.full(vj.shape, -1, jnp.int32), jnp.full(vj.shape, -jnp.inf, F32)
        take = vj > best
        if skip is not None:
            take = jnp.logical_and(take, skip != j)
        idx = jnp.where(take, j, idx)
        best = jnp.where(take, vj, best)
    return idx, best


def _ffn_pre_body(h, g_ref, sh_ref, sc_ref, rw_ref, rb_ref, tri_ref, v_ref, ri_ref, rwt_ref, cnt_ref, carry_ref):
    @pl.when(jnp.logical_and(pl.program_id(0) == 0, pl.program_id(1) == 0))
    def _():
        carry_ref[...] = jnp.zeros_like(carry_ref)

    v = _rms(h, g_ref[...]) * (1.0 + sc_ref[...]) + sh_ref[...]
    v_ref[...] = _pack_bf16_pairs(v)
    st = _sigmoid(_dot_f32(v, rw_ref[...])).T[0:N_EXPERTS]
    sel = st + rb_ref[...]
    row = lambda a, e: a[e:e + 1]
    epg = EXPERTS_PER_GROUP
    gscore = [_top2_of4(*[row(sel, g * epg + j) for j in range(epg)]) for g in range(N_GROUPS)]
    best, _ = _first_argmax(gscore)

    def in_best(a, j):
        out = row(a, j)
        for g in range(1, N_GROUPS):
            out = jnp.where(best == g, row(a, g * epg + j), out)
        return out

    vals = [in_best(sel, j) for j in range(epg)]
    raw = [in_best(st, j) for j in range(epg)]
    i1, _ = _first_argmax(vals)
    i2, _ = _first_argmax(vals, skip=i1)
    pick = lambda i: functools.reduce(lambda acc, j: jnp.where(i == j, raw[j], acc), range(1, epg), raw[0])
    w1, w2 = pick(i1), pick(i2)
    wsum = w1 + w2
    e1, e2 = best * epg + i1, best * epg + i2

    experts = lax.broadcasted_iota(jnp.int32, st.shape, 0)
    oh1 = (experts == e1).astype(F32)
    oh2 = (experts == e2).astype(F32)
    cnt = oh1 + oh2
    before = _dot(cnt.astype(BF16), tri_ref[...]) + carry_ref[:, 0:1]
    ri_ref[0:1, :] = e1
    ri_ref[1:2, :] = e2
    ri_ref[2:3, :] = jnp.sum(oh1 * before, axis=0, keepdims=True).astype(jnp.int32)
    ri_ref[3:4, :] = jnp.sum(oh2 * before, axis=0, keepdims=True).astype(jnp.int32)
    ri_ref[4:8, :] = jnp.zeros((4, st.shape[1]), jnp.int32)
    lane_row = lax.broadcasted_iota(jnp.int32, (LANES, st.shape[1]), 0)
    rwt_ref[...] = jnp.where(lane_row == 0, w1 / wsum, jnp.where(lane_row == 1, w2 / wsum, 0.0)).T
    carry_ref[...] = carry_ref[...] + jnp.sum(cnt, axis=1, keepdims=True)
    cnt_ref[...] = carry_ref[...]


N_FFN_PRE_IN = 6


def _ffn_pre_parts(bsz, t, d, g, shift, scale, router_w_pad, router_b, seg):
    tm = ROW_TILE
    tri = jnp.asarray(np.triu(np.ones((tm, tm), np.float32), 1), BF16)
    in_specs = [pl.BlockSpec((1, d), lambda b, i: (0, 0)),
                pl.BlockSpec((None, None, 1, d), seg),
                pl.BlockSpec((None, None, 1, d), seg),
                pl.BlockSpec((d, LANES), lambda b, i: (0, 0)),
                pl.BlockSpec((N_EXPERTS, 1), lambda b, i: (0, 0)),
                pl.BlockSpec((tm, tm), lambda b, i: (0, 0))]
    args = (g.reshape(1, d), shift, scale, router_w_pad, router_b.reshape(N_EXPERTS, 1), tri)
    out_specs = [pl.BlockSpec((None, tm, d // 2), lambda b, i: (b, i, 0)),
                 pl.BlockSpec((None, 8, tm), lambda b, i: (b, 0, i)),
                 pl.BlockSpec((None, tm, LANES), lambda b, i: (b, i, 0)),
                 pl.BlockSpec((N_EXPERTS, LANES), lambda b, i: (0, 0))]
    out_shape = [jax.ShapeDtypeStruct((bsz, t, d // 2), jnp.uint32),
                 jax.ShapeDtypeStruct((bsz, 8, t), jnp.int32),
                 jax.ShapeDtypeStruct((bsz, t, LANES), F32),
                 jax.ShapeDtypeStruct((N_EXPERTS, LANES), F32)]
    return in_specs, args, out_specs, out_shape, [pltpu.VMEM((N_EXPERTS, LANES), F32)]


def _experts_kernel(be_ref, nb_ref, x_ref, wg_ref, wu_ref, wd_ref, o_ref, wg_s, wu_s, wd_s):
    i = pl.program_id(0)
    prev = be_ref[jnp.maximum(i - 1, 0)]
    changed = jnp.logical_or(i == 0, be_ref[i] != prev)

    @pl.when(changed)
    def _():
        wg_s[...] = wg_ref[...].astype(BF16)
        wu_s[...] = wu_ref[...].astype(BF16)
        wd_s[...] = wd_ref[...].astype(BF16)

    @pl.when(i < nb_ref[0])
    def _():
        x = _unpack_bf16_pairs(x_ref[...]).astype(BF16)
        hid = _silu(_dot(x, wg_s[...])) * _dot(x, wu_s[...])
        o_ref[...] = _pack_bf16_pairs(_dot(hid.astype(BF16), wd_s[...]))

    @pl.when(i >= nb_ref[0])
    def _():
        o_ref[...] = jnp.zeros_like(o_ref)


def _experts(xb, block_e, n_used, layer, w_gate, w_up, w_down):
    n_slots = xb.shape[0]
    n_blocks = n_slots // MOE_BLOCK
    d, de = w_gate.shape[-2:]
    wspec = lambda shape: pl.BlockSpec((None, None) + shape, lambda i, be, nb: (layer, be[i], 0, 0))
    return pl.pallas_call(
        _experts_kernel,
        grid_spec=pltpu.PrefetchScalarGridSpec(
            num_scalar_prefetch=2,
            grid=(n_blocks,),
            in_specs=[pl.BlockSpec((MOE_BLOCK, d // 2), lambda i, be, nb: (i, 0)),
                      wspec((d, de)), wspec((d, de)), wspec((de, d))],
            out_specs=pl.BlockSpec((MOE_BLOCK, d // 2), lambda i, be, nb: (i, 0)),
            scratch_shapes=[pltpu.VMEM((d, de), BF16), pltpu.VMEM((d, de), BF16), pltpu.VMEM((de, d), BF16)]),
        out_shape=jax.ShapeDtypeStruct((n_slots, d // 2), jnp.uint32),
        compiler_params=_params("arbitrary"),
        name="moe_experts",
    )(block_e, n_used, xb, w_gate, w_up, w_down)


def _ffn_post_kernel(h_ref, y0_ref, y1_ref, w_ref, gt_ref, g_ref, o_ref, *, final):
    w = w_ref[...]
    y = w[:, 0:1] * _unpack_bf16_pairs(y0_ref[...]) + w[:, 1:2] * _unpack_bf16_pairs(y1_ref[...])
    out = h_ref[...] + gt_ref[...] * y
    o_ref[...] = _rms(out, g_ref[...]) if final else out


def _ffn_post(h, y, w, gate, n_ctx, final_g=None):
    bsz, t, d = h.shape
    tm = ROW_TILE
    seg = lambda b, i: (b, (i * tm >= n_ctx).astype(jnp.int32), 0, 0)
    row = lambda width: pl.BlockSpec((None, tm, width), lambda b, i: (b, i, 0))
    choice = lambda kk: pl.BlockSpec((None, None, tm, d // 2), lambda b, i: (kk, b, i, 0))
    final = final_g is not None
    g = final_g if final else jnp.ones((d,), F32)
    return pl.pallas_call(
        functools.partial(_ffn_post_kernel, final=final),
        grid=(bsz, t // tm),
        in_specs=[row(d), choice(0), choice(1), row(LANES), pl.BlockSpec((None, None, 1, d), seg),
                  pl.BlockSpec((1, d), lambda b, i: (0, 0))],
        out_specs=row(d),
        out_shape=jax.ShapeDtypeStruct((bsz, t, d), F32),
        compiler_params=_params("parallel", "parallel"),
        name="ffn_post",
    )(h, y, y, w, gate, g.reshape(1, d))


def _slot_layout(n, ri, counts):
    e = jnp.swapaxes(ri[:, 0:2], 0, 1).reshape(TOP_K, n)
    rank = jnp.swapaxes(ri[:, 2:4], 0, 1).reshape(TOP_K, n)
    padded = (counts + MOE_BLOCK - 1) // MOE_BLOCK * MOE_BLOCK
    pend = jnp.cumsum(padded)
    pstart = pend - padded
    experts = jnp.arange(N_EXPERTS, dtype=jnp.int32)
    dest = rank + jnp.sum(jnp.where(e[..., None] == experts, pstart, 0), axis=-1)
    n_slots = (n * TOP_K + MOE_BLOCK - 1) // MOE_BLOCK * MOE_BLOCK + N_EXPERTS * MOE_BLOCK
    n_blocks = n_slots // MOE_BLOCK
    blk0 = jnp.arange(n_blocks, dtype=jnp.int32)[:, None] * MOE_BLOCK
    block_e = jnp.minimum(jnp.sum((pend[None, :] <= blk0).astype(jnp.int32), axis=-1), N_EXPERTS - 1)
    n_used = (pend[-1] // MOE_BLOCK).astype(jnp.int32).reshape(1)
    return dest, n_slots, block_e.astype(jnp.int32), n_used


SC_CORES, SC_SUBCORES = 2, 16
SC_GATHER_WINDOW = 64
SC_SCATTER_WINDOW = 32


def _sc_window(per_worker, largest):
    win = largest
    while per_worker % win:
        win //= 2
    assert win >= 8
    return win


def _gather_rows(table, idx):
    n_rows, d = idx.shape[0], table.shape[1]
    workers = SC_CORES * SC_SUBCORES
    per_worker = n_rows // workers
    assert per_worker * workers == n_rows
    win = _sc_window(per_worker, SC_GATHER_WINDOW)
    mesh = plsc.VectorSubcoreMesh(core_axis_name="c", subcore_axis_name="s")

    @functools.partial(
        pl.kernel, mesh=mesh,
        out_type=jax.ShapeDtypeStruct((n_rows, d), table.dtype),
        scratch_types=[pltpu.VMEM((win,), jnp.int32), pltpu.VMEM((win,), jnp.int32),
                       pltpu.VMEM((win, d), table.dtype), pltpu.VMEM((win, d), table.dtype),
                       pltpu.SemaphoreType.DMA, pltpu.SemaphoreType.DMA],
    )
    def gather_kernel(table_hbm, idx_hbm, out_hbm, idx0, idx1, rows0, rows1, sem0, sem1):
        base = (lax.axis_index("s") * SC_CORES + lax.axis_index("c")) * per_worker
        n_win = per_worker // win
        slots = ((idx0, rows0, sem0), (idx1, rows1, sem1))
        window = lambda j: pl.ds(pl.multiple_of(base + j * win, 8), win)

        def start(j, slot):
            idx_v, rows_v, sem = slots[slot]
            pltpu.sync_copy(idx_hbm.at[window(j)], idx_v)
            pltpu.async_copy(table_hbm.at[idx_v], rows_v, sem)

        def finish(j, slot):
            idx_v, rows_v, sem = slots[slot]
            pltpu.make_async_copy(table_hbm.at[idx_v], rows_v, sem).wait()
            pltpu.sync_copy(rows_v, out_hbm.at[window(j)])

        start(0, 0)

        @pl.loop(0, n_win, step=2)
        def _(j):
            @pl.when(j + 1 < n_win)
            def _():
                start(j + 1, 1)

            finish(j, 0)

            @pl.when(j + 2 < n_win)
            def _():
                start(j + 2, 0)

            @pl.when(j + 1 < n_win)
            def _():
                finish(j + 1, 1)

    return gather_kernel(table, idx)


def _scatter_rows(src, dest, n_slots):
    n, d = src.shape
    workers = SC_CORES * SC_SUBCORES
    per_worker = n // workers
    assert per_worker * workers == n and dest.shape == (TOP_K, n)
    win = _sc_window(per_worker, SC_SCATTER_WINDOW)
    mesh = plsc.VectorSubcoreMesh(core_axis_name="c", subcore_axis_name="s")

    @functools.partial(
        pl.kernel, mesh=mesh,
        out_type=jax.ShapeDtypeStruct((n_slots, d), src.dtype),
        scratch_types=[pltpu.VMEM((win,), jnp.int32)] * 4 + [pltpu.VMEM((win, d), src.dtype)] * 2
                      + [pltpu.SemaphoreType.DMA] * 6,
    )
    def scatter_kernel(src_hbm, dest_hbm, out_hbm, i0a, i1a, i0b, i1b, rows_a, rows_b,
                       load_a, first_a, second_a, load_b, first_b, second_b):
        base = (lax.axis_index("s") * SC_CORES + lax.axis_index("c")) * per_worker
        n_win = per_worker // win
        slots = ((i0a, i1a, rows_a, load_a, first_a, second_a), (i0b, i1b, rows_b, load_b, first_b, second_b))
        window = lambda j: pl.ds(pl.multiple_of(base + j * win, 8), win)

        def load(j, slot):
            idx0, idx1, rows_v, sem, _, _ = slots[slot]
            pltpu.async_copy(src_hbm.at[window(j)], rows_v, sem)
            pltpu.sync_copy(dest_hbm.at[0, window(j)], idx0)
            pltpu.sync_copy(dest_hbm.at[1, window(j)], idx1)

        def scatter(j, slot):
            idx0, idx1, rows_v, sem, sem0, sem1 = slots[slot]
            pltpu.make_async_copy(src_hbm.at[window(j)], rows_v, sem).wait()
            pltpu.async_copy(rows_v, out_hbm.at[idx0], sem0)
            pltpu.async_copy(rows_v, out_hbm.at[idx1], sem1)

        def drain(slot):
            idx0, idx1, rows_v, _, sem0, sem1 = slots[slot]
            pltpu.make_async_copy(rows_v, out_hbm.at[idx0], sem0).wait()
            pltpu.make_async_copy(rows_v, out_hbm.at[idx1], sem1).wait()

        load(0, 0)

        @pl.loop(0, n_win, step=2)
        def _(j):
            scatter(j, 0)

            @pl.when(j + 1 < n_win)
            def _():
                load(j + 1, 1)

            drain(0)

            @pl.when(j + 1 < n_win)
            def _():
                scatter(j + 1, 1)

            @pl.when(j + 2 < n_win)
            def _():
                load(j + 2, 0)

            @pl.when(j + 1 < n_win)
            def _():
                drain(1)

    return scatter_kernel(src, dest)


def _alongside(gather, idx, side_fn, side_in):
    idx, side_in = lax.optimization_barrier((idx, side_in))
    return lax.optimization_barrier((gather(idx), side_fn(side_in)))


def _moe(h, pre_out, gate, layer, w_gate, w_up, w_down, n_ctx, final_g=None, side=None):
    bsz, t, d = h.shape
    n = bsz * t
    v, ri, rwt, counts = pre_out
    dest, n_slots, block_e, n_used = _slot_layout(n, ri, counts[:, 0].astype(jnp.int32))
    dispatch = lambda idx: _scatter_rows(v.reshape(n, d // 2), idx, n_slots)
    if side is None:
        xb = dispatch(dest)
    else:
        xb, side_a = _alongside(dispatch, dest, *side[0])
    yb = _experts(xb, block_e, n_used, layer, w_gate, w_up, w_down)
    combine = lambda idx: _gather_rows(yb, idx)
    dest_flat = dest.reshape(-1)
    if side is None:
        y, side_b = combine(dest_flat), None
    else:
        y, side_b = _alongside(combine, dest_flat, side[1], side_a)
    out = _ffn_post(h, y.reshape(TOP_K, bsz, t, d // 2), rwt, gate, n_ctx, final_g)
    return out if side is None else (out, side_b)


DFT_STEP = 16


def _dft_tables(n):
    r, *mats = _dft_tables_np(n)
    return (r,) + tuple(jnp.asarray(a).astype(BF16) for a in mats)


@functools.lru_cache(maxsize=None)
def _dft_tables_np(n):
    size = 2 * n
    r = int(round(math.sqrt(size)))
    assert r * r == size and r % DFT_STEP == 0
    p1 = np.arange(r // 2)[None, None, :]
    p2 = np.arange(r)[:, None, None]
    k1 = np.arange(r)[None, :, None]
    ang = 2.0 * np.pi * (((r * p1 + p2) * k1) % size) / size
    g_re, g_im = np.cos(ang), -np.sin(ang)
    g_in = np.concatenate([g_re, g_im], axis=1)
    g_out = np.concatenate([np.swapaxes(g_re, 1, 2), np.swapaxes(g_im, 1, 2)], axis=2) / size
    a2 = 2.0 * np.pi * ((np.arange(r)[:, None] * np.arange(r)[None, :]) % r) / r
    f_re, f_im = np.cos(a2), -np.sin(a2)
    f_fwd = np.block([[f_re, -f_im], [f_im, f_re]])
    f_inv = np.block([[f_re, f_im], [-f_im, f_re]])
    p1f = np.arange(r)[None, None, :]
    angf = 2.0 * np.pi * (((r * p1f + p2) * k1) % size) / size
    g_full = np.concatenate([np.cos(angf), -np.sin(angf)], axis=1)
    return (r,) + tuple(a.astype(np.float32) for a in (g_in, g_out, f_fwd, f_inv, g_full))


def _dot_f32_tn(a, b):
    ah = a.astype(BF16)
    al = (a - ah.astype(F32)).astype(BF16)
    bh = b.astype(BF16)
    bl = (b - bh.astype(F32)).astype(BF16)
    return _dot_tn(ah, bh) + (_dot_tn(ah, bl) + _dot_tn(al, bh))


def _hy_filter_kernel(z_ref, t_ref, w1_ref, b1_ref, w2_ref, b2_ref, fr_ref, w3_ref, w3b_ref, rates_ref, o_ref,
                      *, half_tiles):
    i = pl.program_id(0)
    hid = jnp.sin(fr_ref[...] * (_dot_f32(w1_ref[...], z_ref[...]) + b1_ref[...]))
    hid = jnp.sin(fr_ref[...] * (_dot_f32(w2_ref[...], hid) + b2_ref[...]))
    filt = _dot_f32_tn(hid, w3_ref[...])
    decay = jnp.exp(-t_ref[...] * rates_ref[...])
    for o in range(o_ref.shape[0]):
        o_ref[o] = filt[:, o * HY_CH:(o + 1) * HY_CH] * decay

    @pl.when(i == 0)
    def _():
        extra = _dot_f32_tn(hid[:, 0:LANES], w3b_ref[...])[0:8]
        first = lax.broadcasted_iota(jnp.int32, (8, HY_CH), 0) == 0
        for o in range(o_ref.shape[0]):
            add = extra[:, o * HY_CH:(o + 1) * HY_CH] * decay[0:8]
            o_ref[o, 0:8, :] = o_ref[o, 0:8, :] + jnp.where(first, add, 0.0)

    @pl.when(i == half_tiles)
    def _():
        for o in range(o_ref.shape[0]):
            o_ref[o, 0:1, :] = jnp.zeros((1, HY_CH), F32)


HY_TILE = 512


def _hy_kernels(n, w1, b1, w2, b2, w3, freq):
    pos = np.arange(2 * n)
    pos = np.where(pos < n, pos, 2 * n - pos).astype(np.float32)
    t = jnp.asarray(pos / np.float32(n - 1))
    bands = jnp.linspace(1e-4, HY_BANDS - 1, HY_BANDS, dtype=F32)
    ang = (2.0 * math.pi / n) * bands[:, None] * jnp.asarray(pos)[None, :]
    z = jnp.concatenate([t[None, :], jnp.cos(ang), -jnp.sin(ang)], axis=0)
    z = jnp.pad(z, ((0, LANES - z.shape[0]), (0, 0)))
    w1t = jnp.pad(w1, ((0, LANES - w1.shape[0]), (0, 0))).T
    hidden = w1.shape[1]
    col = lambda v: v.reshape(hidden, 1)
    w3d = jnp.swapaxes(w3.reshape(hidden, HY_ORDER, 2, HY_CH), 0, 2)
    w3d = jnp.swapaxes(w3d, 1, 2).reshape(2, hidden, HY_ORDER * HY_CH)
    rates = jnp.abs(jnp.linspace(HY_MIN_DECAY, HY_MAX_DECAY, HY_CH, dtype=F32)).reshape(1, HY_CH)
    tm = HY_TILE
    half_tiles = n // tm
    full = lambda a: pl.BlockSpec(a.shape, lambda i: (0,) * a.ndim)
    small = (w1t, col(b1), w2.T, col(b2), col(freq))
    return pl.pallas_call(
        functools.partial(_hy_filter_kernel, half_tiles=half_tiles),
        grid=(2 * n // tm,),
        in_specs=[pl.BlockSpec((LANES, tm), lambda i: (0, i)), pl.BlockSpec((tm, 1), lambda i: (i, 0))]
                 + [full(a) for a in small]
                 + [pl.BlockSpec((None, hidden, HY_ORDER * HY_CH), lambda i: ((i >= half_tiles).astype(jnp.int32), 0, 0)),
                    pl.BlockSpec((None, hidden, HY_ORDER * HY_CH), lambda i: (1, 0, 0)), full(rates)],
        out_specs=pl.BlockSpec((HY_ORDER, tm, HY_CH), lambda i: (0, i, 0)),
        out_shape=jax.ShapeDtypeStruct((HY_ORDER, 2 * n, HY_CH), F32),
        compiler_params=_params("parallel"),
        name="hy_kernels",
    )(z, t[:, None], *small, w3d, w3d, rates)


def _pack_complex(z):
    r = z.shape[0] // 2
    bits = lax.bitcast_convert_type(z.astype(BF16).astype(F32), jnp.uint32)
    return lax.bitcast_convert_type(bits[0:r] | (bits[r:2 * r] >> 16), F32)


def _unpack_complex(words):
    p = lax.bitcast_convert_type(words, jnp.uint32)
    re = lax.bitcast_convert_type(p & jnp.uint32(0xFFFF0000), F32)
    im = lax.bitcast_convert_type(p << 16, F32)
    return jnp.concatenate([re, im], axis=0).astype(BF16)


def _load_every(ref, j, count):
    return ref.reshape(count * DFT_STEP, LANES)[pl.ds(j, count, stride=DFT_STEP), :]


def _store_every(ref, j, count, val):
    ref.reshape(count * DFT_STEP, LANES)[pl.ds(j, count, stride=DFT_STEP), :] = val


def _dft_in_kernel(x_ref, g_ref, a_ref):
    n_seq, rh = x_ref.shape[0:2]
    for j in range(DFT_STEP):
        x = jnp.concatenate([_load_every(x_ref.at[b], j, rh) for b in range(n_seq)], axis=1).astype(BF16)
        res = _dot(g_ref[j], x)
        for b in range(n_seq):
            a_ref[b, j] = _pack_complex(res[:, b * LANES:(b + 1) * LANES])


def _dft_in(x4, col, g_in):
    bx, rh, r, _ = x4.shape
    c = HY_CH
    cbs = c // LANES
    return pl.pallas_call(
        _dft_in_kernel,
        grid=(r // DFT_STEP, cbs),
        in_specs=[pl.BlockSpec((bx, rh, DFT_STEP, LANES), lambda i, cb: (0, 0, i, col * cbs + cb)),
                  pl.BlockSpec((DFT_STEP, 2 * r, rh), lambda i, cb: (i, 0, 0))],
        out_specs=pl.BlockSpec((bx, DFT_STEP, r, LANES), lambda i, cb: (0, i, 0, cb)),
        out_shape=jax.ShapeDtypeStruct((bx, r, r, c), F32),
        compiler_params=_params("parallel", "parallel"),
        name="dft_in",
    )(x4, g_in)


def _stage2_operand(a_ref, j, r):
    return jnp.concatenate([_unpack_complex(_load_every(a_ref.at[b], j, r)) for b in range(a_ref.shape[0])],
                           axis=1)


def _dft_filt_kernel(a_ref, f_ref, k_ref):
    r = f_ref.shape[0] // 2
    for j in range(DFT_STEP):
        s = _dot(f_ref[...], _stage2_operand(a_ref, j, r))
        for o in range(a_ref.shape[0]):
            k_ref[o, j] = s[:, o * LANES:(o + 1) * LANES]


def _dft_filt(a, f_fwd):
    nq, r, _, c = a.shape
    return pl.pallas_call(
        _dft_filt_kernel,
        grid=(r // DFT_STEP, c // LANES),
        in_specs=[pl.BlockSpec((nq, r, DFT_STEP, LANES), lambda i, cb: (0, 0, i, cb)),
                  pl.BlockSpec(f_fwd.shape, lambda i, cb: (0, 0))],
        out_specs=pl.BlockSpec((nq, DFT_STEP, 2 * r, LANES), lambda i, cb: (0, i, 0, cb)),
        out_shape=jax.ShapeDtypeStruct((nq, r, 2 * r, c), F32),
        compiler_params=_params("parallel", "parallel"),
        name="dft_filt",
    )(a, f_fwd)


def _dft_mid_kernel(a_ref, k_ref, ff_ref, fi_ref, b_ref):
    r = ff_ref.shape[0] // 2
    n_seq = a_ref.shape[0]
    for j in range(DFT_STEP):
        s = _dot(ff_ref[...], _stage2_operand(a_ref, j, r))
        sr, si = s[0:r], s[r:2 * r]
        kr = jnp.concatenate([k_ref[j, 0:r, :]] * n_seq, axis=1)
        ki = jnp.concatenate([k_ref[j, r:2 * r, :]] * n_seq, axis=1)
        p = jnp.concatenate([sr * kr - si * ki, sr * ki + si * kr], axis=0).astype(BF16)
        back = _dot(fi_ref[...], p)
        for b in range(n_seq):
            b_ref[b, j] = _pack_complex(back[:, b * LANES:(b + 1) * LANES])


def _dft_mid(a, kspec, order, f_fwd, f_inv):
    bsz, r, _, c = a.shape
    return pl.pallas_call(
        _dft_mid_kernel,
        grid=(r // DFT_STEP, c // LANES),
        in_specs=[pl.BlockSpec((bsz, r, DFT_STEP, LANES), lambda i, cb: (0, 0, i, cb)),
                  pl.BlockSpec((None, DFT_STEP, 2 * r, LANES), lambda i, cb: (order, i, 0, cb)),
                  pl.BlockSpec(f_fwd.shape, lambda i, cb: (0, 0)),
                  pl.BlockSpec(f_inv.shape, lambda i, cb: (0, 0))],
        out_specs=pl.BlockSpec((bsz, DFT_STEP, r, LANES), lambda i, cb: (0, i, 0, cb)),
        out_shape=jax.ShapeDtypeStruct((bsz, r, r, c), F32),
        compiler_params=_params("parallel", "parallel"),
        name="dft_mid",
    )(a, kspec, f_fwd, f_inv)


def _dft_out_kernel(b_ref, g_ref, u_ref, x_ref, bias_ref, o_ref):
    n_seq, r = b_ref.shape[0:2]
    rh = o_ref.shape[1]
    for j in range(DFT_STEP):
        rhs = jnp.concatenate([_unpack_complex(_load_every(b_ref.at[b], j, r)) for b in range(n_seq)], axis=1)
        y = _dot(g_ref[j], rhs)
        for b in range(n_seq):
            yb = y[:, b * LANES:(b + 1) * LANES]
            _store_every(o_ref.at[b], j, rh,
                         _load_every(x_ref.at[b], j, rh) * (yb + _load_every(u_ref.at[b], j, rh) * bias_ref[...]))


def _dft_out(bm, g_out, u4, u_col, x4, x_col, bias):
    bsz, r, _, c = bm.shape
    rh = r // 2
    cbs = c // LANES
    seq = lambda col: pl.BlockSpec((bsz, rh, DFT_STEP, LANES), lambda i, cb: (0, 0, i, col * cbs + cb))
    return pl.pallas_call(
        _dft_out_kernel,
        grid=(r // DFT_STEP, cbs),
        in_specs=[pl.BlockSpec((bsz, r, DFT_STEP, LANES), lambda i, cb: (0, 0, i, cb)),
                  pl.BlockSpec((DFT_STEP, rh, 2 * r), lambda i, cb: (i, 0, 0)),
                  seq(u_col), seq(x_col), pl.BlockSpec((1, LANES), lambda i, cb: (0, cb))],
        out_specs=seq(0),
        out_shape=jax.ShapeDtypeStruct((bsz, rh, r, c), F32),
        compiler_params=_params("parallel", "parallel"),
        name="dft_out",
    )(bm, g_out, u4, x4, bias.reshape(1, c))


def _hyena_filter_stage1(n, filter_params):
    r, g_full = _dft_tables(n)[0], _dft_tables(n)[5]
    kern = _hy_kernels(n, *filter_params)
    return _dft_in(kern.reshape(-1, r, r, HY_CH), 0, g_full)


def _hyena_filter_spectra(n, stage1):
    return _dft_filt(stage1, _dft_tables(n)[3])


def _hyena(hy_in, kspec, conv_bias):
    bsz, n, _ = hy_in.shape
    r, g_in, g_out, f_fwd, f_inv, _ = _dft_tables(n)
    seq4 = hy_in.reshape(bsz, r // 2, r, 3 * HY_CH)
    zz = _dft_out(_dft_mid(_dft_in(seq4, 0, g_in), kspec, 0, f_fwd, f_inv), g_out,
                  seq4, 0, seq4, 1, conv_bias[0])
    out = _dft_out(_dft_mid(_dft_in(zz, 0, g_in), kspec, 1, f_fwd, f_inv), g_out,
                   zz, 0, seq4, 2, conv_bias[1])
    return out.reshape(bsz, n, HY_CH)


def _reorder_ab(w):
    offs = np.cumsum([0, GLA_KEY_W, GLA_KEY_W, GLA_VAL_W, GLA_VAL_W, GLA_LOW_RANK, GLA_LOW_RANK,
                      HG_KEY_W, HG_KEY_W, HG_KEY_W, HG_VAL_W, HG_VAL_W]).tolist()
    gq, gk, gv, gg, lr_f, lr_b, hq, hf_f, hf_b, hi, hg = range(11)
    cols = lambda first, last: w[:, offs[first]:offs[last + 1]].astype(BF16)
    lr = jnp.pad(cols(lr_f, lr_b), ((0, 0), (0, AB_PAD_COLS - AB_LR - 2 * GLA_LOW_RANK)))
    return [cols(gq, gk), cols(gg, gg), cols(hq, hf_b), cols(hg, hg), lr], [cols(gv, gv), cols(hi, hi)]


def _reorder_cd(w):
    hy_end = 3 * HY_CH
    z_end = hy_end + MB_INNER
    xbc_end = z_end + MB_INNER + 2 * MB_BC_W
    pad = lambda a: jnp.pad(a, ((0, 0), (0, LANES - MB_HEADS)))
    dt = jnp.concatenate([pad(w[:, xbc_end:xbc_end + MB_HEADS]), pad(w[:, xbc_end + MB_HEADS:])], axis=-1)
    return [w[:, :hy_end].astype(BF16), w[:, z_end:xbc_end].astype(BF16), w[:, hy_end:z_end].astype(BF16),
            dt.astype(BF16)]


def kernel(x, c, ctx, c_ctx, ada_w, ada_b, norm_mix_g, norm_ffn_g, norm_out_g, ab_w_in, ab_w_out, gla_gate_w, gla_gate_b, gla_norm_g, hg_lb, hg_norm_g, cd_w_in, cd_w_out, hy_short_w, hy_short_b, hy_w1, hy_b1, hy_w2, hy_b2, hy_w3, hy_freq, hy_bias, mb_conv_w, mb_conv_b, mb_dt_bias, mb_a_log, mb_d, mb_norm_g, router_w, router_b, moe_w_gate, moe_w_up, moe_w_down):
    bsz, n_lat, d = x.shape
    n_ctx = ctx.shape[1]
    assert ada_w.shape[0] == 2 and ab_w_in.shape[0] == 1 and cd_w_in.shape[0] == 1

    cond = jnp.zeros((ADALN_ROWS, d), F32).at[:bsz].set(c).at[bsz].set(c_ctx)
    m = _adaln(cond, ada_w, ada_b)

    def mods(layer):
        lat = m[layer, :bsz].reshape(bsz, 6, d)
        cx = jnp.broadcast_to(m[layer, bsz].reshape(1, 6, d), (bsz, 6, d))
        both = jnp.stack([cx, lat], axis=1)
        return [both[:, :, j][:, :, None, :] for j in range(6)]

    router_w_pad = jnp.zeros((d, LANES), F32).at[:, :N_EXPERTS].set(router_w)

    sh_m, sc_m, gt_m, sh_f, sc_f, gt_f = mods(0)
    proj, vals = _norm_proj(ctx, x, norm_mix_g[0], sh_m, sc_m, *_reorder_ab(ab_w_in[0]))
    gwp = [jnp.zeros((LANES, GLA_KEY_W), F32).at[GLA_LOW_RANK * dd:GLA_LOW_RANK * (dd + 1)].set(gla_gate_w[0, dd])
           for dd in range(2)]
    o_gla = _gla_scan(proj, vals, gwp, [gla_gate_b[0, dd].reshape(1, -1) for dd in range(2)], n_ctx)
    o_hg = _hgrn_scan(proj, vals, [hg_lb[dd].astype(F32) for dd in range(2)], 0, n_ctx)
    h, pre_out = _mix_out_ab(ctx, x, o_gla, o_hg, proj, gla_norm_g[0], hg_norm_g[0], ab_w_out[0].astype(BF16),
                             gt_m, (norm_ffn_g[0], sh_f, sc_f, router_w_pad, router_b))
    filter_params = (hy_w1[0], hy_b1[0], hy_w2[0], hy_b2[0], hy_w3[0], hy_freq[0])
    side = ((functools.partial(_hyena_filter_stage1, n_lat), filter_params),
            functools.partial(_hyena_filter_spectra, n_lat))
    h, kspec = _moe(h, pre_out, gt_f, 0, moe_w_gate, moe_w_up, moe_w_down, n_ctx, side=side)

    sh_m, sc_m, gt_m, sh_f, sc_f, gt_f = mods(1)
    conv_w = jnp.concatenate([hy_short_w[0], mb_conv_w[0]], axis=0).T
    conv_b = jnp.concatenate([hy_short_b[0], mb_conv_b[0]]).reshape(1, -1)
    hy_in, xbc, zdt = _norm_proj_conv(h, norm_mix_g[1], sh_m, sc_m, _reorder_cd(cd_w_in[0]),
                                      conv_w, conv_b, 3 * HY_CH, n_ctx)
    hy = _hyena(hy_in, kspec, hy_bias[0])
    y_ssd = _ssd_scan(xbc, zdt, MB_INNER // LANES, mb_dt_bias[0], mb_a_log[0], n_ctx)
    d_skip_x = jnp.repeat(mb_d[0], MB_HEAD_DIM).reshape(1, MB_INNER)
    h, pre_out = _mix_out_cd(h, hy, y_ssd, xbc, zdt, 0, d_skip_x, mb_norm_g[0], cd_w_out[0].astype(BF16), gt_m,
                             n_ctx, (norm_ffn_g[1], sh_f, sc_f, router_w_pad, router_b))
    return _moe(h, pre_out, gt_f, 1, moe_w_gate, moe_w_up, moe_w_down, 0, final_g=norm_out_g)
```

```python
import functools
import math

import numpy as np
import jax
import jax.numpy as jnp
from jax import lax
from jax.experimental import pallas as pl
from jax.experimental.pallas import tpu as pltpu
from jax.experimental.pallas import tpu_sc as plsc

NORM_EPS = 1e-6
GLA_HEADS, GLA_DK, GLA_DV, GLA_LOW_RANK, GLA_TAU = 4, 64, 128, 16, 16.0
GLA_KEY_W, GLA_VAL_W = GLA_HEADS * GLA_DK, GLA_HEADS * GLA_DV
HG_HEADS, HG_EXPAND, HG_DV = 4, 128, 128
HG_KEY_W, HG_VAL_W = HG_HEADS * HG_EXPAND, HG_HEADS * HG_DV
HY_CH, HY_ORDER, HY_BANDS = 512, 2, 16
HY_MIN_DECAY = math.log(1e-2) / 1.5
HY_MAX_DECAY = math.log(1e-2) / 0.3
MB_HEADS, MB_HEAD_DIM, MB_GROUPS, MB_STATE = 8, 64, 2, 128
MB_INNER = MB_HEADS * MB_HEAD_DIM
MB_BC_W = MB_GROUPS * MB_STATE
N_EXPERTS, N_GROUPS, TOP_K, MOE_BLOCK = 16, 4, 2, 256
EXPERTS_PER_GROUP = N_EXPERTS // N_GROUPS

LANES = 128
SCAN_CHUNK = 64
SCAN_BLOCK = 256
SSD_CHUNK = 128
SSD_BLOCK = 256
ROW_TILE = 256
VMEM_LIMIT = 56 * 1024 * 1024

BF16 = jnp.bfloat16
F32 = jnp.float32


def _params(*sem):
    return pltpu.CompilerParams(dimension_semantics=sem, vmem_limit_bytes=VMEM_LIMIT)


def _split3(x):
    hi = x.astype(BF16)
    r1 = x - hi.astype(F32)
    mid = r1.astype(BF16)
    lo = (r1 - mid.astype(F32)).astype(BF16)
    return hi, mid, lo


def _dot(a, b):
    return jnp.dot(a, b, preferred_element_type=F32)


def _dot_nt(a, b):
    return lax.dot_general(a, b, (((1,), (1,)), ((), ())), preferred_element_type=F32)


def _dot_tn(a, b):
    return lax.dot_general(a, b, (((0,), (0,)), ((), ())), preferred_element_type=F32)


def _dot_f32(a, b):
    ah = a.astype(BF16)
    al = (a - ah.astype(F32)).astype(BF16)
    bh = b.astype(BF16)
    bl = (b - bh.astype(F32)).astype(BF16)
    return _dot(ah, bh) + (_dot(ah, bl) + _dot(al, bh))


def _silu(x):
    return x * (1.0 / (1.0 + jnp.exp(-x)))


def _sigmoid(x):
    return 1.0 / (1.0 + jnp.exp(-x))


def _softplus(x):
    return jnp.maximum(x, 0.0) + jnp.log(1.0 + jnp.exp(-jnp.abs(x)))


def _pack_bf16_pairs(x):
    bits = lax.bitcast_convert_type(x.astype(BF16).astype(F32), jnp.uint32)
    half = x.shape[1] // 2
    return bits[:, :half] | (bits[:, half:] >> 16)


def _unpack_bf16_pairs(p):
    hi = lax.bitcast_convert_type(p & jnp.uint32(0xFFFF0000), F32)
    lo = lax.bitcast_convert_type(p << 16, F32)
    return jnp.concatenate([hi, lo], axis=1)


def _rms(x, g):
    return x * lax.rsqrt(jnp.mean(x * x, axis=-1, keepdims=True) + NORM_EPS) * g


def _adaln_kernel(c_ref, w_ref, b_ref, o_ref):
    o_ref[...] = _dot_f32(_silu(c_ref[...]), w_ref[...]) + b_ref[...]


ADALN_ROWS = 8
ADALN_TILE = 1536


def _adaln(cond, w, b):
    n_l, d, n6 = w.shape
    tn = ADALN_TILE
    rows = cond.shape[0]
    return pl.pallas_call(
        _adaln_kernel,
        grid=(n_l, n6 // tn),
        in_specs=[pl.BlockSpec((rows, d), lambda l, j: (0, 0)),
                  pl.BlockSpec((None, d, tn), lambda l, j: (l, 0, j)),
                  pl.BlockSpec((None, 1, tn), lambda l, j: (l, 0, j))],
        out_specs=pl.BlockSpec((None, rows, tn), lambda l, j: (l, 0, j)),
        out_shape=jax.ShapeDtypeStruct((n_l, rows, n6), F32),
        compiler_params=_params("parallel", "parallel"),
        name="adaln",
    )(cond, w, b.reshape(n_l, 1, n6))


def _project(u, w_refs):
    return jnp.concatenate([_dot(u, w_ref[...]) for w_ref in w_refs], axis=-1)


def _joint_rows_specs(ctx, x):
    tm = ROW_TILE
    ctx_tiles = ctx.shape[1] // tm
    d = ctx.shape[2]
    return [pl.BlockSpec((None, tm, d), lambda b, i: (b, jnp.minimum(i, ctx_tiles - 1), 0)),
            pl.BlockSpec((None, tm, d), lambda b, i: (b, jnp.maximum(i - ctx_tiles, 0), 0))], ctx_tiles


def _joint_rows(ctx_ref, x_ref, ctx_tiles):
    return jnp.where(pl.program_id(1) < ctx_tiles, ctx_ref[...], x_ref[...])


def _norm_proj_kernel(c_ref, x_ref, g_ref, sh_ref, sc_ref, *refs, ctx_tiles, n_main):
    u = _rms(_joint_rows(c_ref, x_ref, ctx_tiles), g_ref[...]) * (1.0 + sc_ref[...]) + sh_ref[...]
    u = u.astype(BF16)
    refs[-2][...] = _project(u, refs[:n_main])
    refs[-1][...] = _project(u, refs[n_main:-2]).astype(BF16)


def _norm_proj(ctx, x, g, shift, scale, ws, ws_bf16):
    bsz, n_ctx, d = ctx.shape
    t = n_ctx + x.shape[1]
    n, n_bf = sum(w.shape[1] for w in ws), sum(w.shape[1] for w in ws_bf16)
    tm = ROW_TILE
    seg = lambda b, i: (b, (i * tm >= n_ctx).astype(jnp.int32), 0, 0)
    row_specs, ctx_tiles = _joint_rows_specs(ctx, x)
    return pl.pallas_call(
        functools.partial(_norm_proj_kernel, ctx_tiles=ctx_tiles, n_main=len(ws)),
        grid=(bsz, t // tm),
        in_specs=row_specs + [pl.BlockSpec((1, d), lambda b, i: (0, 0)),
                              pl.BlockSpec((None, None, 1, d), seg),
                              pl.BlockSpec((None, None, 1, d), seg)]
                 + [pl.BlockSpec(w.shape, lambda b, i: (0, 0)) for w in list(ws) + list(ws_bf16)],
        out_specs=[pl.BlockSpec((None, tm, n), lambda b, i: (b, i, 0)),
                   pl.BlockSpec((None, tm, n_bf), lambda b, i: (b, i, 0))],
        out_shape=[jax.ShapeDtypeStruct((bsz, t, n), F32), jax.ShapeDtypeStruct((bsz, t, n_bf), BF16)],
        compiler_params=_params("parallel", "parallel"),
        name="norm_proj",
    )(ctx, x, g.reshape(1, d), shift, scale, *ws, *ws_bf16)


def _scan_constants(c, reverse):
    t = np.arange(c)[:, None]
    u = np.arange(c)[None, :]
    sels = [u <= t, u > t]
    masks = []
    m = c // 2
    while m >= 1:
        blk = t // (2 * m)
        upper_t = (t % (2 * m)) >= m
        r = blk * (2 * m) + m - 1
        s_blk = u // (2 * m)
        upper_s = (u % (2 * m)) >= m
        sels.append((upper_t & (u > r) & (u <= t)) | ((~upper_t) & (u > t) & (u <= r)))
        masks.append((blk == s_blk) & upper_t & (~upper_s))
        m //= 2
    masks.append(t == u)
    sel = np.stack(sels).astype(np.float32)
    msk = np.stack(masks).astype(np.float32)
    if reverse:
        sel = sel[:, ::-1, ::-1]
        msk = msk[:, ::-1, ::-1]
    return np.ascontiguousarray(sel.reshape(-1, c)), np.ascontiguousarray(msk)


def _chunk_order(i, n_ctx_chunks, n_chunks, reverse):
    if not reverse:
        return i
    return jnp.where(i < n_ctx_chunks, n_ctx_chunks - 1 - i, n_chunks - 1 - (i - n_ctx_chunks))


GROUP_KEYS = 256


def _decay_chunk(q, k, v, la, consts, st_ref, heads, dk, dv):
    sel_ref, mask_ref, hm_ref, hmb_ref, vm_ref = consts
    c = q.shape[0]
    n_lvl = mask_ref.shape[0] - 1
    hpg = GROUP_KEYS // dk
    cs = _dot(sel_ref[...], jnp.concatenate(_split3(la), axis=0))
    e_q = jnp.exp(cs[0:c])
    e_k = jnp.exp(cs[c:2 * c])
    e_tot = jnp.exp(jnp.sum(la, axis=0, keepdims=True))
    vb = v.astype(BF16)
    outs = []
    for g in range(heads // hpg):
        ks = slice(g * GROUP_KEYS, (g + 1) * GROUP_KEYS)
        vs = slice(g * hpg * dv, (g + 1) * hpg * dv)
        qg, kg = q[:, ks], k[:, ks]
        key_stack = lambda x: jnp.concatenate([x.astype(BF16) * hmb_ref[h] for h in range(hpg)], axis=0)
        att = mask_ref[n_lvl] * _dot_nt(qg.astype(BF16), key_stack(kg))
        for l in range(n_lvl):
            e = jnp.exp(cs[(2 + l) * c:(3 + l) * c, ks])
            att = att + mask_ref[l] * _dot_nt((qg * e).astype(BF16), key_stack(kg * e))
        v_blocks = jnp.concatenate([vb[:, vs] * vm_ref[h] for h in range(hpg)], axis=0)
        intra = _dot(att.astype(BF16), v_blocks)
        st = st_ref[g]
        q_stack = jnp.concatenate([(qg * e_q[:, ks]) * hm_ref[h] for h in range(hpg)], axis=0)
        inter = _dot_nt(q_stack.astype(BF16), st.astype(BF16))
        upd = _dot_tn(vb[:, vs], (kg * e_k[:, ks]).astype(BF16))
        new = st * e_tot[:, ks]
        for h in range(hpg):
            new = new + upd[h * dv:(h + 1) * dv] * hm_ref[h]
        st_ref[g] = new
        outs.append(intra + jnp.concatenate([inter[h * c:(h + 1) * c] for h in range(hpg)], axis=-1))
    return jnp.concatenate(outs, axis=-1)


def _log_sigmoid(x):
    return jnp.minimum(x, 0.0) - jnp.log(1.0 + jnp.exp(-jnp.abs(x)))


def _gla_kernel(*refs):
    ins, head_masks, (o_refs, st_refs) = (refs[0:8], refs[8:16]), refs[16:19], (refs[19:21], refs[21:23])

    @pl.when(pl.program_id(1) == 0)
    def _():
        for st_ref in st_refs:
            st_ref[...] = jnp.zeros_like(st_ref)

    for d, ((q_ref, k_ref, v_ref, lr_ref, gw_ref, gb_ref, sel_ref, mask_ref), o_ref, st_ref) in enumerate(
            zip(ins, o_refs, st_refs)):
        z = _dot_f32(lr_ref[...], gw_ref[...]) + gb_ref[...]
        la = _log_sigmoid(z) * (1.0 / GLA_TAU)
        q = q_ref[...] * (GLA_DK ** -0.5)
        k, v = k_ref[...], v_ref[...]
        for rows in _sub_chunks(q.shape[0], d == 1):
            o_ref[rows, :] = _decay_chunk(q[rows], k[rows], v[rows], la[rows], (sel_ref, mask_ref) + head_masks,
                                          st_ref, GLA_HEADS, GLA_DK, GLA_DV).astype(o_ref.dtype)


def _hgrn_kernel(*refs, layer):
    ins, head_masks, (o_refs, st_refs) = (refs[0:6], refs[6:12]), refs[12:15], (refs[15:17], refs[17:19])

    @pl.when(pl.program_id(1) == 0)
    def _():
        for st_ref in st_refs:
            st_ref[...] = jnp.zeros_like(st_ref)

    for d, ((q_ref, f_ref, v_ref, lb_ref, sel_ref, mask_ref), o_ref, st_ref) in enumerate(
            zip(ins, o_refs, st_refs)):
        e = jnp.exp(lb_ref[...] - jnp.max(lb_ref[...], axis=0, keepdims=True))
        lb = jnp.sum(e[0:layer + 1], axis=0, keepdims=True) / jnp.sum(e, axis=0, keepdims=True)
        f = lb + (1.0 - lb) * _sigmoid(f_ref[...])
        q, k, v, la = _silu(q_ref[...]), 1.0 - f, v_ref[...], jnp.log(f)
        for rows in _sub_chunks(q.shape[0], d == 1):
            o_ref[rows, :] = _decay_chunk(q[rows], k[rows], v[rows], la[rows], (sel_ref, mask_ref) + head_masks,
                                          st_ref, HG_HEADS, HG_EXPAND, HG_DV).astype(o_ref.dtype)


def _sub_chunks(rows, reverse, chunk=SCAN_CHUNK):
    order = range(rows // chunk)
    return [slice(j * chunk, (j + 1) * chunk) for j in (reversed(order) if reverse else order)]


def _scan_specs(blk, n_ctx, t, reverse, chunk=None, stacked_heads=1):
    n_blocks = t // blk
    order = functools.partial(_chunk_order, n_ctx_chunks=n_ctx // blk, n_chunks=n_blocks, reverse=reverse)

    def col(width, idx):
        return pl.BlockSpec((None, blk, width), lambda b, i: (b, order(i), idx))

    sel, msk = _scan_constants(chunk or blk, reverse)
    sel3 = np.concatenate([sel, sel, sel], axis=1)
    msk = np.tile(msk, (1, 1, stacked_heads))
    const = lambda a: pl.BlockSpec(a.shape, lambda b, i: (0,) * a.ndim)
    return n_blocks, col, const, jnp.asarray(sel3, BF16), jnp.asarray(msk, F32)


def _head_masks(dk, dv):
    hpg = GROUP_KEYS // dk
    hm = np.zeros((hpg, 1, GROUP_KEYS), np.float32)
    vm = np.zeros((hpg, 1, hpg * dv), np.float32)
    for h in range(hpg):
        hm[h, 0, h * dk:(h + 1) * dk] = 1.0
        vm[h, 0, h * dv:(h + 1) * dv] = 1.0
    return jnp.asarray(hm), jnp.asarray(hm, BF16), jnp.asarray(vm, BF16)


AB_Q, AB_K, AB_G = 0, 256, 512
AB_HQ, AB_HF, AB_HG, AB_LR = 1024, 1536, 2560, 3072
AB_PAD_COLS = 3200
AB_V, AB_HI = 0, 1


def _gla_scan(proj, vals, gate_w_pad, gate_b, n_ctx):
    bsz, t, _ = proj.shape
    hpg = GROUP_KEYS // GLA_DK
    in_specs, args, outs = [], [], []
    for d in range(2):
        n_blocks, col, const, sel, msk = _scan_specs(SCAN_BLOCK, n_ctx, t, d == 1, SCAN_CHUNK, hpg)
        in_specs += [col(GLA_KEY_W, AB_Q // GLA_KEY_W), col(GLA_KEY_W, AB_K // GLA_KEY_W),
                     col(GLA_VAL_W, AB_V), col(LANES, AB_LR // LANES),
                     const(gate_w_pad[d]), const(gate_b[d]), const(sel), const(msk)]
        args += [proj, proj, vals, proj, gate_w_pad[d], gate_b[d], sel, msk]
        outs.append(col(GLA_VAL_W, 0))
    hm = _head_masks(GLA_DK, GLA_DV)
    return pl.pallas_call(
        _gla_kernel,
        grid=(bsz, n_blocks),
        in_specs=in_specs + [const(m) for m in hm],
        out_specs=outs,
        out_shape=[jax.ShapeDtypeStruct((bsz, t, GLA_VAL_W), BF16)] * 2,
        scratch_shapes=[pltpu.VMEM((GLA_HEADS // hpg, GLA_DV, GROUP_KEYS), F32)] * 2,
        compiler_params=_params("parallel", "arbitrary"),
        name="gla_scan",
    )(*args, *hm)


def _hgrn_scan(proj, vals, lb, layer, n_ctx):
    bsz, t, _ = proj.shape
    hpg = GROUP_KEYS // HG_EXPAND
    in_specs, args, outs = [], [], []
    for d in range(2):
        n_blocks, col, const, sel, msk = _scan_specs(SCAN_BLOCK, n_ctx, t, d == 1, SCAN_CHUNK, hpg)
        in_specs += [col(HG_KEY_W, AB_HQ // HG_KEY_W), col(HG_KEY_W, AB_HF // HG_KEY_W + d),
                     col(HG_VAL_W, AB_HI), const(lb[d]), const(sel), const(msk)]
        args += [proj, proj, vals, lb[d], sel, msk]
        outs.append(col(HG_VAL_W, 0))
    hm = _head_masks(HG_EXPAND, HG_DV)
    return pl.pallas_call(
        functools.partial(_hgrn_kernel, layer=layer),
        grid=(bsz, n_blocks),
        in_specs=in_specs + [const(m) for m in hm],
        out_specs=outs,
        out_shape=[jax.ShapeDtypeStruct((bsz, t, HG_VAL_W), BF16)] * 2,
        scratch_shapes=[pltpu.VMEM((HG_HEADS // hpg, HG_DV, GROUP_KEYS), F32)] * 2,
        compiler_params=_params("parallel", "arbitrary"),
        name="hgrn_scan",
    )(*args, *hm)


def _mix_out_ab_kernel(*refs, ctx_tiles):
    (c_ref, x_ref, gf_ref, gb_ref, hf_ref, hb_ref, gg_ref, hg_ref, gn_ref, hn_ref, w_ref, gt_ref) = refs[0:12]
    pre_in, o_ref, pre_out = refs[12:12 + N_FFN_PRE_IN], refs[12 + N_FFN_PRE_IN], refs[13 + N_FFN_PRE_IN:]
    feats = []
    both = lambda fwd_ref, bwd_ref: fwd_ref[...].astype(F32) + bwd_ref[...].astype(F32)
    for o, gate, g in ((both(gf_ref, gb_ref), gg_ref[...], gn_ref[...]),
                       (both(hf_ref, hb_ref), hg_ref[...], hn_ref[...])):
        for hd in range(o.shape[-1] // LANES):
            s = slice(hd * LANES, (hd + 1) * LANES)
            feats.append(_rms(o[:, s], g) * _silu(gate[:, s]))
    feat = jnp.concatenate(feats, axis=-1).astype(BF16)
    h_new = _joint_rows(c_ref, x_ref, ctx_tiles) + gt_ref[...] * _dot(feat, w_ref[...])
    o_ref[...] = h_new
    _ffn_pre_body(h_new, *pre_in, *pre_out)


def _mix_out_ab(ctx, x, o_gla, o_hg, proj, gla_norm_g, hg_norm_g, w_out, gate, pre):
    bsz, n_ctx, d = ctx.shape
    t = n_ctx + x.shape[1]
    tm = ROW_TILE
    seg = lambda b, i: (b, (i * tm >= n_ctx).astype(jnp.int32), 0, 0)
    row = lambda width, idx: pl.BlockSpec((None, tm, width), lambda b, i: (b, i, idx))
    vec = pl.BlockSpec((1, LANES), lambda b, i: (0, 0))
    p_in, p_args, p_out, p_shape, p_scratch = _ffn_pre_parts(bsz, t, d, *pre, seg)
    row_specs, ctx_tiles = _joint_rows_specs(ctx, x)
    outs = pl.pallas_call(
        functools.partial(_mix_out_ab_kernel, ctx_tiles=ctx_tiles),
        grid=(bsz, t // tm),
        in_specs=row_specs + [row(GLA_VAL_W, 0), row(GLA_VAL_W, 0), row(HG_VAL_W, 0), row(HG_VAL_W, 0),
                  row(GLA_VAL_W, AB_G // GLA_VAL_W), row(HG_VAL_W, AB_HG // HG_VAL_W), vec, vec,
                  pl.BlockSpec(w_out.shape, lambda b, i: (0, 0)),
                  pl.BlockSpec((None, None, 1, d), seg)] + p_in,
        out_specs=[row(d, 0)] + p_out,
        out_shape=[jax.ShapeDtypeStruct((bsz, t, d), F32)] + p_shape,
        scratch_shapes=p_scratch,
        compiler_params=_params("arbitrary", "arbitrary"),
        name="mix_out_ab",
    )(ctx, x, o_gla[0], o_gla[1], o_hg[0], o_hg[1], proj, proj, gla_norm_g.reshape(1, -1),
      hg_norm_g.reshape(1, -1), w_out, gate, *p_args)
    return outs[0], outs[1:]


HALO = 8


def _norm_proj_conv_kernel(h_ref, hp_ref, hn_ref, g_ref, sh_ref, sc_ref, cw_ref, cb_ref, *refs,
                           ctx_tiles, n_tiles):
    w_refs, (hy_ref, xbc_ref, zdt_ref) = refs[:-3], refs[-3:]
    i = pl.program_id(1)
    tm = h_ref.shape[0]
    hh = jnp.concatenate([hp_ref[...], h_ref[...], hn_ref[...]], axis=0)
    u = _rms(hh, g_ref[...]) * (1.0 + sc_ref[...]) + sh_ref[...]
    p = _project(u.astype(BF16), w_refs)
    n_conv = cw_ref.shape[1]
    n_hy = hy_ref.shape[1]
    pc = p[:, 0:n_conv]
    rows_all = tm + 2 * HALO
    cur = pc[HALO:HALO + tm]
    prev = pltpu.roll(pc, 1, axis=0)[HALO:HALO + tm]
    nxt = pltpu.roll(pc, rows_all - 1, axis=0)[HALO:HALO + tm]
    first = jnp.logical_or(i == 0, i == ctx_tiles)
    last = jnp.logical_or(i == ctx_tiles - 1, i == n_tiles - 1)
    rows = lax.broadcasted_iota(jnp.int32, cur.shape, 0)
    prev = jnp.where(jnp.logical_and(first, rows == 0), 0.0, prev)
    nxt = jnp.where(jnp.logical_and(last, rows == tm - 1), 0.0, nxt)
    y = prev * cw_ref[0:1, :] + cur * cw_ref[1:2, :] + nxt * cw_ref[2:3, :] + cb_ref[...]
    hy_ref[...] = y[:, 0:n_hy]
    xbc_ref[...] = _silu(y[:, n_hy:n_conv])
    zdt_ref[...] = p[HALO:HALO + tm, n_conv:]


def _norm_proj_conv(h, g, shift, scale, ws, conv_w, conv_b, n_hy, n_ctx):
    bsz, t, d = h.shape
    n = sum(w.shape[1] for w in ws)
    n_conv = conv_w.shape[1]
    tm = ROW_TILE
    n_tiles, ctx_tiles = t // tm, n_ctx // tm
    r8 = tm // HALO
    last8 = t // HALO - 1
    seg = lambda b, i: (b, (i >= ctx_tiles).astype(jnp.int32), 0, 0)
    kern = functools.partial(_norm_proj_conv_kernel, ctx_tiles=ctx_tiles, n_tiles=n_tiles)
    return pl.pallas_call(
        kern,
        grid=(bsz, n_tiles),
        in_specs=[pl.BlockSpec((None, tm, d), lambda b, i: (b, i, 0)),
                  pl.BlockSpec((None, HALO, d), lambda b, i: (b, jnp.maximum(i * r8 - 1, 0), 0)),
                  pl.BlockSpec((None, HALO, d), lambda b, i: (b, jnp.minimum((i + 1) * r8, last8), 0)),
                  pl.BlockSpec((1, d), lambda b, i: (0, 0)),
                  pl.BlockSpec((None, None, 1, d), seg),
                  pl.BlockSpec((None, None, 1, d), seg),
                  pl.BlockSpec((3, n_conv), lambda b, i: (0, 0)),
                  pl.BlockSpec((1, n_conv), lambda b, i: (0, 0))]
                 + [pl.BlockSpec(w.shape, lambda b, i: (0, 0)) for w in ws],
        out_specs=[pl.BlockSpec((None, tm, n_hy), lambda b, i: (b, jnp.maximum(i - ctx_tiles, 0), 0)),
                   pl.BlockSpec((None, tm, n_conv - n_hy), lambda b, i: (b, i, 0)),
                   pl.BlockSpec((None, tm, n - n_conv), lambda b, i: (b, i, 0))],
        out_shape=[jax.ShapeDtypeStruct((bsz, t - n_ctx, n_hy), F32),
                   jax.ShapeDtypeStruct((bsz, t, n_conv - n_hy), F32),
                   jax.ShapeDtypeStruct((bsz, t, n - n_conv), F32)],
        compiler_params=_params("parallel", "arbitrary"),
        name="norm_proj_conv",
    )(h, h, h, g.reshape(1, d), shift, scale, conv_w, conv_b, *ws)


def _ssd_kernel(*refs):
    ins, hexp_ref, o_refs, st_refs = (refs[0:7], refs[7:14]), refs[14], refs[15:17], refs[17:19]

    @pl.when(pl.program_id(1) == 0)
    def _():
        for st_ref in st_refs:
            st_ref[...] = jnp.zeros_like(st_ref)

    for d, ((xbc_ref, dt_ref, bias_ref, alog_ref, mq_ref, mk_ref, mask_ref), o_ref, st_ref) in enumerate(
            zip(ins, o_refs, st_refs)):
        for rows in _sub_chunks(xbc_ref.shape[0], d == 1, SSD_CHUNK):
            _ssd_chunk(rows, xbc_ref, dt_ref, bias_ref, alog_ref, hexp_ref, mq_ref, mk_ref, mask_ref, o_ref, st_ref)


def _ssd_chunk(rows, xbc_ref, dt_ref, bias_ref, alog_ref, hexp_ref, mq_ref, mk_ref, mask_ref, o_ref, st_ref):
    c = SSD_CHUNK
    hpg = MB_HEADS // MB_GROUPS
    gw = hpg * MB_HEAD_DIM
    dt = _softplus(dt_ref[rows, :] + bias_ref[...])
    la = -dt * jnp.exp(alog_ref[...])
    la3 = jnp.concatenate(_split3(la), axis=0)
    cq = _dot(mq_ref[...], la3)
    ck = _dot(mk_ref[...], la3)
    cq_t = lax.dot_general(la3, mq_ref[...], (((0,), (1,)), ((), ())), preferred_element_type=F32)
    tot = jnp.broadcast_to(jnp.sum(la, axis=0, keepdims=True), (8, la.shape[1]))
    per_head = jnp.concatenate([dt, cq, ck, tot], axis=0)
    per_lane = _dot(jnp.concatenate(_split3(per_head), axis=1), hexp_ref[...])
    dt_x = per_lane[0:c]
    eq_x = jnp.exp(per_lane[c:2 * c])
    ek_x = jnp.exp(per_lane[2 * c:3 * c])
    etot_x = jnp.exp(per_lane[3 * c:3 * c + 1])
    xs = xbc_ref[rows, 0:MB_INNER] * dt_x
    mask = mask_ref[...]
    outs = []
    for g in range(MB_GROUPS):
        bm = xbc_ref[rows, MB_INNER + g * MB_STATE:MB_INNER + (g + 1) * MB_STATE].astype(BF16)
        cm = xbc_ref[rows, MB_INNER + MB_BC_W + g * MB_STATE:MB_INNER + MB_BC_W + (g + 1) * MB_STATE].astype(BF16)
        cb = _dot_nt(cm, bm)
        st = st_ref[g]
        gs = slice(g * gw, (g + 1) * gw)
        y_inter = _dot(cm, st.astype(BF16)) * eq_x[:, gs]
        for r in range(hpg):
            hd = g * hpg + r
            diff = cq[:, hd:hd + 1] - cq_t[hd:hd + 1, :]
            w = cb * jnp.exp(jnp.where(mask > 0.0, diff, -jnp.inf))
            ps = slice(hd * MB_HEAD_DIM, (hd + 1) * MB_HEAD_DIM)
            outs.append(_dot(w.astype(BF16), xs[:, ps].astype(BF16))
                        + y_inter[:, r * MB_HEAD_DIM:(r + 1) * MB_HEAD_DIM])
        st_ref[g] = st * etot_x[:, gs] + _dot_tn(bm, (xs[:, gs] * ek_x[:, gs]).astype(BF16))
    o_ref[rows, :] = jnp.concatenate(outs, axis=-1).astype(o_ref.dtype)


def _ssd_scan(xbc, proj, dt_col, dt_bias, a_log, n_ctx):
    bsz, t, _ = xbc.shape
    c = SSD_CHUNK
    pad = lambda v: jnp.zeros((1, LANES), F32).at[0, :MB_HEADS].set(v)
    hexp = np.zeros((LANES, MB_INNER), np.float32)
    for hd in range(MB_HEADS):
        hexp[hd, hd * MB_HEAD_DIM:(hd + 1) * MB_HEAD_DIM] = 1.0
    hexp = jnp.asarray(np.concatenate([hexp, hexp, hexp], axis=0), BF16)
    tri = np.tril(np.ones((c, c), np.float32))
    in_specs, args, outs = [], [], []
    for d in range(2):
        n_chunks, col, const, sel, _ = _scan_specs(SSD_BLOCK, n_ctx, t, d == 1, c)
        mq, mk = sel[0:c], sel[c:2 * c]
        mask = jnp.asarray(tri[::-1, ::-1].copy() if d == 1 else tri)
        bias, alog = pad(dt_bias[d]), pad(a_log[d].astype(F32))
        in_specs += [col(xbc.shape[-1], 0), col(LANES, dt_col + d), const(bias), const(alog),
                     const(mq), const(mk), const(mask)]
        args += [xbc, proj, bias, alog, mq, mk, mask]
        outs.append(col(MB_INNER, 0))
    return pl.pallas_call(
        _ssd_kernel,
        grid=(bsz, n_chunks),
        in_specs=in_specs + [const(hexp)],
        out_specs=outs,
        out_shape=[jax.ShapeDtypeStruct((bsz, t, MB_INNER), BF16)] * 2,
        scratch_shapes=[pltpu.VMEM((MB_GROUPS, MB_STATE, MB_INNER // MB_GROUPS), F32)] * 2,
        compiler_params=_params("parallel", "arbitrary"),
        name="ssd_scan",
    )(*args, hexp)


def _mix_out_cd_kernel(*refs):
    h_ref, hy_ref, yf_ref, yb_ref, xs_ref, z_ref, dsk_ref, ng_ref, w_ref, gt_ref = refs[0:10]
    pre_in, o_ref, pre_out = refs[10:10 + N_FFN_PRE_IN], refs[10 + N_FFN_PRE_IN], refs[11 + N_FFN_PRE_IN:]
    y = (yf_ref[...].astype(F32) + yb_ref[...].astype(F32) + dsk_ref[...] * xs_ref[...]) * _silu(z_ref[...])
    gw = MB_INNER // MB_GROUPS
    ys = [_rms(y[:, g * gw:(g + 1) * gw], ng_ref[:, g * gw:(g + 1) * gw]) for g in range(MB_GROUPS)]
    feat = jnp.concatenate([hy_ref[...]] + ys, axis=-1).astype(BF16)
    h_new = h_ref[...] + gt_ref[...] * _dot(feat, w_ref[...])
    o_ref[...] = h_new
    _ffn_pre_body(h_new, *pre_in, *pre_out)


def _mix_out_cd(h, hy, y_ssd, xbc, proj, z_col, d_skip_x, norm_g, w_out, gate, n_ctx, pre):
    bsz, t, d = h.shape
    tm = ROW_TILE
    n_lat = t - n_ctx
    off = n_ctx // tm
    row = lambda width, idx: pl.BlockSpec((None, tm, width), lambda b, i: (b, i + off, idx))
    vec = pl.BlockSpec((1, MB_INNER), lambda b, i: (0, 0))
    latent = lambda b, i: (b, 1, 0, 0)
    p_in, p_args, p_out, p_shape, p_scratch = _ffn_pre_parts(bsz, n_lat, d, *pre, latent)
    outs = pl.pallas_call(
        _mix_out_cd_kernel,
        grid=(bsz, n_lat // tm),
        in_specs=[row(d, 0), pl.BlockSpec((None, tm, HY_CH), lambda b, i: (b, i, 0)),
                  row(MB_INNER, 0), row(MB_INNER, 0), row(MB_INNER, 0), row(MB_INNER, z_col), vec, vec,
                  pl.BlockSpec(w_out.shape, lambda b, i: (0, 0)),
                  pl.BlockSpec((None, None, 1, d), latent)] + p_in,
        out_specs=[pl.BlockSpec((None, tm, d), lambda b, i: (b, i, 0))] + p_out,
        out_shape=[jax.ShapeDtypeStruct((bsz, n_lat, d), F32)] + p_shape,
        scratch_shapes=p_scratch,
        compiler_params=_params("arbitrary", "arbitrary"),
        name="mix_out_cd",
    )(h, hy, y_ssd[0], y_ssd[1], xbc, proj, d_skip_x, norm_g.reshape(1, -1), w_out, gate, *p_args)
    return outs[0], outs[1:]


def _top2_of4(a, b, c, d):
    hi1, lo1, hi2, lo2 = jnp.maximum(a, b), jnp.minimum(a, b), jnp.maximum(c, d), jnp.minimum(c, d)
    return jnp.maximum(hi1, hi2) + jnp.maximum(jnp.minimum(hi1, hi2), jnp.maximum(lo1, lo2))


def _first_argmax(vals, skip=None):
    idx = None
    for j, vj in enumerate(vals):
        if idx is None and skip is None:
            idx, best = jnp.zeros(vj.shape, jnp.int32), vj
            continue
        if idx is None:
            idx, best = jnp.full(vj.shape, -1, jnp.int32), jnp.full(vj.shape, -jnp.inf, F32)
        take = vj > best
        if skip is not None:
            take = jnp.logical_and(take, skip != j)
        idx = jnp.where(take, j, idx)
        best = jnp.where(take, vj, best)
    return idx, best


def _ffn_pre_body(h, g_ref, sh_ref, sc_ref, rw_ref, rb_ref, tri_ref, v_ref, ri_ref, rwt_ref, cnt_ref, carry_ref):
    @pl.when(jnp.logical_and(pl.program_id(0) == 0, pl.program_id(1) == 0))
    def _():
        carry_ref[...] = jnp.zeros_like(carry_ref)

    v = _rms(h, g_ref[...]) * (1.0 + sc_ref[...]) + sh_ref[...]
    v_ref[...] = _pack_bf16_pairs(v)
    st = _sigmoid(_dot_f32(v, rw_ref[...])).T[0:N_EXPERTS]
    sel = st + rb_ref[...]
    row = lambda a, e: a[e:e + 1]
    epg = EXPERTS_PER_GROUP
    gscore = [_top2_of4(*[row(sel, g * epg + j) for j in range(epg)]) for g in range(N_GROUPS)]
    best, _ = _first_argmax(gscore)

    def in_best(a, j):
        out = row(a, j)
        for g in range(1, N_GROUPS):
            out = jnp.where(best == g, row(a, g * epg + j), out)
        return out

    vals = [in_best(sel, j) for j in range(epg)]
    raw = [in_best(st, j) for j in range(epg)]
    i1, _ = _first_argmax(vals)
    i2, _ = _first_argmax(vals, skip=i1)
    pick = lambda i: functools.reduce(lambda acc, j: jnp.where(i == j, raw[j], acc), range(1, epg), raw[0])
    w1, w2 = pick(i1), pick(i2)
    wsum = w1 + w2
    e1, e2 = best * epg + i1, best * epg + i2

    experts = lax.broadcasted_iota(jnp.int32, st.shape, 0)
    oh1 = (experts == e1).astype(F32)
    oh2 = (experts == e2).astype(F32)
    cnt = oh1 + oh2
    before = _dot(cnt.astype(BF16), tri_ref[...]) + carry_ref[:, 0:1]
    ri_ref[0:1, :] = e1
    ri_ref[1:2, :] = e2
    ri_ref[2:3, :] = jnp.sum(oh1 * before, axis=0, keepdims=True).astype(jnp.int32)
    ri_ref[3:4, :] = jnp.sum(oh2 * before, axis=0, keepdims=True).astype(jnp.int32)
    ri_ref[4:8, :] = jnp.zeros((4, st.shape[1]), jnp.int32)
    lane_row = lax.broadcasted_iota(jnp.int32, (LANES, st.shape[1]), 0)
    rwt_ref[...] = jnp.where(lane_row == 0, w1 / wsum, jnp.where(lane_row == 1, w2 / wsum, 0.0)).T
    carry_ref[...] = carry_ref[...] + jnp.sum(cnt, axis=1, keepdims=True)
    cnt_ref[...] = carry_ref[...]


N_FFN_PRE_IN = 6


def _ffn_pre_parts(bsz, t, d, g, shift, scale, router_w_pad, router_b, seg):
    tm = ROW_TILE
    tri = jnp.asarray(np.triu(np.ones((tm, tm), np.float32), 1), BF16)
    in_specs = [pl.BlockSpec((1, d), lambda b, i: (0, 0)),
                pl.BlockSpec((None, None, 1, d), seg),
                pl.BlockSpec((None, None, 1, d), seg),
                pl.BlockSpec((d, LANES), lambda b, i: (0, 0)),
                pl.BlockSpec((N_EXPERTS, 1), lambda b, i: (0, 0)),
                pl.BlockSpec((tm, tm), lambda b, i: (0, 0))]
    args = (g.reshape(1, d), shift, scale, router_w_pad, router_b.reshape(N_EXPERTS, 1), tri)
    out_specs = [pl.BlockSpec((None, tm, d // 2), lambda b, i: (b, i, 0)),
                 pl.BlockSpec((None, 8, tm), lambda b, i: (b, 0, i)),
                 pl.BlockSpec((None, tm, LANES), lambda b, i: (b, i, 0)),
                 pl.BlockSpec((N_EXPERTS, LANES), lambda b, i: (0, 0))]
    out_shape = [jax.ShapeDtypeStruct((bsz, t, d // 2), jnp.uint32),
                 jax.ShapeDtypeStruct((bsz, 8, t), jnp.int32),
                 jax.ShapeDtypeStruct((bsz, t, LANES), F32),
                 jax.ShapeDtypeStruct((N_EXPERTS, LANES), F32)]
    return in_specs, args, out_specs, out_shape, [pltpu.VMEM((N_EXPERTS, LANES), F32)]


def _experts_kernel(be_ref, nb_ref, x_ref, wg_ref, wu_ref, wd_ref, o_ref, wg_s, wu_s, wd_s):
    i = pl.program_id(0)
    prev = be_ref[jnp.maximum(i - 1, 0)]
    changed = jnp.logical_or(i == 0, be_ref[i] != prev)

    @pl.when(changed)
    def _():
        wg_s[...] = wg_ref[...].astype(BF16)
        wu_s[...] = wu_ref[...].astype(BF16)
        wd_s[...] = wd_ref[...].astype(BF16)

    @pl.when(i < nb_ref[0])
    def _():
        x = _unpack_bf16_pairs(x_ref[...]).astype(BF16)
        hid = _silu(_dot(x, wg_s[...])) * _dot(x, wu_s[...])
        o_ref[...] = _pack_bf16_pairs(_dot(hid.astype(BF16), wd_s[...]))

    @pl.when(i >= nb_ref[0])
    def _():
        o_ref[...] = jnp.zeros_like(o_ref)


def _experts(xb, block_e, n_used, layer, w_gate, w_up, w_down):
    n_slots = xb.shape[0]
    n_blocks = n_slots // MOE_BLOCK
    d, de = w_gate.shape[-2:]
    wspec = lambda shape: pl.BlockSpec((None, None) + shape, lambda i, be, nb: (layer, be[i], 0, 0))
    return pl.pallas_call(
        _experts_kernel,
        grid_spec=pltpu.PrefetchScalarGridSpec(
            num_scalar_prefetch=2,
            grid=(n_blocks,),
            in_specs=[pl.BlockSpec((MOE_BLOCK, d // 2), lambda i, be, nb: (i, 0)),
                      wspec((d, de)), wspec((d, de)), wspec((de, d))],
            out_specs=pl.BlockSpec((MOE_BLOCK, d // 2), lambda i, be, nb: (i, 0)),
            scratch_shapes=[pltpu.VMEM((d, de), BF16), pltpu.VMEM((d, de), BF16), pltpu.VMEM((de, d), BF16)]),
        out_shape=jax.ShapeDtypeStruct((n_slots, d // 2), jnp.uint32),
        compiler_params=_params("arbitrary"),
        name="moe_experts",
    )(block_e, n_used, xb, w_gate, w_up, w_down)


def _ffn_post_kernel(h_ref, y0_ref, y1_ref, w_ref, gt_ref, g_ref, o_ref, *, final):
    w = w_ref[...]
    y = w[:, 0:1] * _unpack_bf16_pairs(y0_ref[...]) + w[:, 1:2] * _unpack_bf16_pairs(y1_ref[...])
    out = h_ref[...] + gt_ref[...] * y
    o_ref[...] = _rms(out, g_ref[...]) if final else out


def _ffn_post(h, y, w, gate, n_ctx, final_g=None):
    bsz, t, d = h.shape
    tm = ROW_TILE
    seg = lambda b, i: (b, (i * tm >= n_ctx).astype(jnp.int32), 0, 0)
    row = lambda width: pl.BlockSpec((None, tm, width), lambda b, i: (b, i, 0))
    choice = lambda kk: pl.BlockSpec((None, None, tm, d // 2), lambda b, i: (kk, b, i, 0))
    final = final_g is not None
    g = final_g if final else jnp.ones((d,), F32)
    return pl.pallas_call(
        functools.partial(_ffn_post_kernel, final=final),
        grid=(bsz, t // tm),
        in_specs=[row(d), choice(0), choice(1), row(LANES), pl.BlockSpec((None, None, 1, d), seg),
                  pl.BlockSpec((1, d), lambda b, i: (0, 0))],
        out_specs=row(d),
        out_shape=jax.ShapeDtypeStruct((bsz, t, d), F32),
        compiler_params=_params("parallel", "parallel"),
        name="ffn_post",
    )(h, y, y, w, gate, g.reshape(1, d))


def _slot_layout(n, ri, counts):
    e = jnp.swapaxes(ri[:, 0:2], 0, 1).reshape(TOP_K, n)
    rank = jnp.swapaxes(ri[:, 2:4], 0, 1).reshape(TOP_K, n)
    padded = (counts + MOE_BLOCK - 1) // MOE_BLOCK * MOE_BLOCK
    pend = jnp.cumsum(padded)
    pstart = pend - padded
    experts = jnp.arange(N_EXPERTS, dtype=jnp.int32)
    dest = rank + jnp.sum(jnp.where(e[..., None] == experts, pstart, 0), axis=-1)
    n_slots = (n * TOP_K + MOE_BLOCK - 1) // MOE_BLOCK * MOE_BLOCK + N_EXPERTS * MOE_BLOCK
    n_blocks = n_slots // MOE_BLOCK
    blk0 = jnp.arange(n_blocks, dtype=jnp.int32)[:, None] * MOE_BLOCK
    block_e = jnp.minimum(jnp.sum((pend[None, :] <= blk0).astype(jnp.int32), axis=-1), N_EXPERTS - 1)
    n_used = (pend[-1] // MOE_BLOCK).astype(jnp.int32).reshape(1)
    return dest, n_slots, block_e.astype(jnp.int32), n_used


SC_CORES, SC_SUBCORES = 2, 16
SC_GATHER_WINDOW = 64
SC_SCATTER_WINDOW = 32


def _sc_window(per_worker, largest):
    win = largest
    while per_worker % win:
        win //= 2
    assert win >= 8
    return win


def _gather_rows(table, idx):
    n_rows, d = idx.shape[0], table.shape[1]
    workers = SC_CORES * SC_SUBCORES
    per_worker = n_rows // workers
    assert per_worker * workers == n_rows
    win = _sc_window(per_worker, SC_GATHER_WINDOW)
    mesh = plsc.VectorSubcoreMesh(core_axis_name="c", subcore_axis_name="s")

    @functools.partial(
        pl.kernel, mesh=mesh,
        out_type=jax.ShapeDtypeStruct((n_rows, d), table.dtype),
        scratch_types=[pltpu.VMEM((win,), jnp.int32), pltpu.VMEM((win,), jnp.int32),
                       pltpu.VMEM((win, d), table.dtype), pltpu.VMEM((win, d), table.dtype),
                       pltpu.SemaphoreType.DMA, pltpu.SemaphoreType.DMA],
    )
    def gather_kernel(table_hbm, idx_hbm, out_hbm, idx0, idx1, rows0, rows1, sem0, sem1):
        base = (lax.axis_index("s") * SC_CORES + lax.axis_index("c")) * per_worker
        n_win = per_worker // win
        slots = ((idx0, rows0, sem0), (idx1, rows1, sem1))
        window = lambda j: pl.ds(pl.multiple_of(base + j * win, 8), win)

        def start(j, slot):
            idx_v, rows_v, sem = slots[slot]
            pltpu.sync_copy(idx_hbm.at[window(j)], idx_v)
            pltpu.async_copy(table_hbm.at[idx_v], rows_v, sem)

        def finish(j, slot):
            idx_v, rows_v, sem = slots[slot]
            pltpu.make_async_copy(table_hbm.at[idx_v], rows_v, sem).wait()
            pltpu.sync_copy(rows_v, out_hbm.at[window(j)])

        start(0, 0)

        @pl.loop(0, n_win, step=2)
        def _(j):
            @pl.when(j + 1 < n_win)
            def _():
                start(j + 1, 1)

            finish(j, 0)

            @pl.when(j + 2 < n_win)
            def _():
                start(j + 2, 0)

            @pl.when(j + 1 < n_win)
            def _():
                finish(j + 1, 1)

    return gather_kernel(table, idx)


def _scatter_rows(src, dest, n_slots):
    n, d = src.shape
    workers = SC_CORES * SC_SUBCORES
    per_worker = n // workers
    assert per_worker * workers == n and dest.shape == (TOP_K, n)
    win = _sc_window(per_worker, SC_SCATTER_WINDOW)
    mesh = plsc.VectorSubcoreMesh(core_axis_name="c", subcore_axis_name="s")

    @functools.partial(
        pl.kernel, mesh=mesh,
        out_type=jax.ShapeDtypeStruct((n_slots, d), src.dtype),
        scratch_types=[pltpu.VMEM((win,), jnp.int32)] * 4 + [pltpu.VMEM((win, d), src.dtype)] * 2
                      + [pltpu.SemaphoreType.DMA] * 6,
    )
    def scatter_kernel(src_hbm, dest_hbm, out_hbm, i0a, i1a, i0b, i1b, rows_a, rows_b,
                       load_a, first_a, second_a, load_b, first_b, second_b):
        base = (lax.axis_index("s") * SC_CORES + lax.axis_index("c")) * per_worker
        n_win = per_worker // win
        slots = ((i0a, i1a, rows_a, load_a, first_a, second_a), (i0b, i1b, rows_b, load_b, first_b, second_b))
        window = lambda j: pl.ds(pl.multiple_of(base + j * win, 8), win)

        def load(j, slot):
            idx0, idx1, rows_v, sem, _, _ = slots[slot]
            pltpu.async_copy(src_hbm.at[window(j)], rows_v, sem)
            pltpu.sync_copy(dest_hbm.at[0, window(j)], idx0)
            pltpu.sync_copy(dest_hbm.at[1, window(j)], idx1)

        def scatter(j, slot):
            idx0, idx1, rows_v, sem, sem0, sem1 = slots[slot]
            pltpu.make_async_copy(src_hbm.at[window(j)], rows_v, sem).wait()
            pltpu.async_copy(rows_v, out_hbm.at[idx0], sem0)
            pltpu.async_copy(rows_v, out_hbm.at[idx1], sem1)

        def drain(slot):
            idx0, idx1, rows_v, _, sem0, sem1 = slots[slot]
            pltpu.make_async_copy(rows_v, out_hbm.at[idx0], sem0).wait()
            pltpu.make_async_copy(rows_v, out_hbm.at[idx1], sem1).wait()

        load(0, 0)

        @pl.loop(0, n_win, step=2)
        def _(j):
            scatter(j, 0)

            @pl.when(j + 1 < n_win)
            def _():
                load(j + 1, 1)

            drain(0)

            @pl.when(j + 1 < n_win)
            def _():
                scatter(j + 1, 1)

            @pl.when(j + 2 < n_win)
            def _():
                load(j + 2, 0)

            @pl.when(j + 1 < n_win)
            def _():
                drain(1)

    return scatter_kernel(src, dest)


def _alongside(gather, idx, side_fn, side_in):
    idx, side_in = lax.optimization_barrier((idx, side_in))
    return lax.optimization_barrier((gather(idx), side_fn(side_in)))


def _moe(h, pre_out, gate, layer, w_gate, w_up, w_down, n_ctx, final_g=None, side=None):
    bsz, t, d = h.shape
    n = bsz * t
    v, ri, rwt, counts = pre_out
    dest, n_slots, block_e, n_used = _slot_layout(n, ri, counts[:, 0].astype(jnp.int32))
    dispatch = lambda idx: _scatter_rows(v.reshape(n, d // 2), idx, n_slots)
    if side is None:
        xb = dispatch(dest)
    else:
        xb, side_a = _alongside(dispatch, dest, *side[0])
    yb = _experts(xb, block_e, n_used, layer, w_gate, w_up, w_down)
    combine = lambda idx: _gather_rows(yb, idx)
    dest_flat = dest.reshape(-1)
    if side is None:
        y, side_b = combine(dest_flat), None
    else:
        y, side_b = _alongside(combine, dest_flat, side[1], side_a)
    out = _ffn_post(h, y.reshape(TOP_K, bsz, t, d // 2), rwt, gate, n_ctx, final_g)
    return out if side is None else (out, side_b)


DFT_STEP = 16


def _dft_tables(n):
    r, *mats = _dft_tables_np(n)
    return (r,) + tuple(jnp.asarray(a).astype(BF16) for a in mats)


@functools.lru_cache(maxsize=None)
def _dft_tables_np(n):
    size = 2 * n
    r = int(round(math.sqrt(size)))
    assert r * r == size and r % DFT_STEP == 0
    p1 = np.arange(r // 2)[None, None, :]
    p2 = np.arange(r)[:, None, None]
    k1 = np.arange(r)[None, :, None]
    ang = 2.0 * np.pi * (((r * p1 + p2) * k1) % size) / size
    g_re, g_im = np.cos(ang), -np.sin(ang)
    g_in = np.concatenate([g_re, g_im], axis=1)
    g_out = np.concatenate([np.swapaxes(g_re, 1, 2), np.swapaxes(g_im, 1, 2)], axis=2) / size
    a2 = 2.0 * np.pi * ((np.arange(r)[:, None] * np.arange(r)[None, :]) % r) / r
    f_re, f_im = np.cos(a2), -np.sin(a2)
    f_fwd = np.block([[f_re, -f_im], [f_im, f_re]])
    f_inv = np.block([[f_re, f_im], [-f_im, f_re]])
    p1f = np.arange(r)[None, None, :]
    angf = 2.0 * np.pi * (((r * p1f + p2) * k1) % size) / size
    g_full = np.concatenate([np.cos(angf), -np.sin(angf)], axis=1)
    return (r,) + tuple(a.astype(np.float32) for a in (g_in, g_out, f_fwd, f_inv, g_full))


def _dot_f32_tn(a, b):
    ah = a.astype(BF16)
    al = (a - ah.astype(F32)).astype(BF16)
    bh = b.astype(BF16)
    bl = (b - bh.astype(F32)).astype(BF16)
    return _dot_tn(ah, bh) + (_dot_tn(ah, bl) + _dot_tn(al, bh))


def _hy_filter_kernel(z_ref, t_ref, w1_ref, b1_ref, w2_ref, b2_ref, fr_ref, w3_ref, w3b_ref, rates_ref, o_ref,
                      *, half_tiles):
    i = pl.program_id(0)
    hid = jnp.sin(fr_ref[...] * (_dot_f32(w1_ref[...], z_ref[...]) + b1_ref[...]))
    hid = jnp.sin(fr_ref[...] * (_dot_f32(w2_ref[...], hid) + b2_ref[...]))
    filt = _dot_f32_tn(hid, w3_ref[...])
    decay = jnp.exp(-t_ref[...] * rates_ref[...])
    for o in range(o_ref.shape[0]):
        o_ref[o] = filt[:, o * HY_CH:(o + 1) * HY_CH] * decay

    @pl.when(i == 0)
    def _():
        extra = _dot_f32_tn(hid[:, 0:LANES], w3b_ref[...])[0:8]
        first = lax.broadcasted_iota(jnp.int32, (8, HY_CH), 0) == 0
        for o in range(o_ref.shape[0]):
            add = extra[:, o * HY_CH:(o + 1) * HY_CH] * decay[0:8]
            o_ref[o, 0:8, :] = o_ref[o, 0:8, :] + jnp.where(first, add, 0.0)

    @pl.when(i == half_tiles)
    def _():
        for o in range(o_ref.shape[0]):
            o_ref[o, 0:1, :] = jnp.zeros((1, HY_CH), F32)


HY_TILE = 512


def _hy_kernels(n, w1, b1, w2, b2, w3, freq):
    pos = np.arange(2 * n)
    pos = np.where(pos < n, pos, 2 * n - pos).astype(np.float32)
    t = jnp.asarray(pos / np.float32(n - 1))
    bands = jnp.linspace(1e-4, HY_BANDS - 1, HY_BANDS, dtype=F32)
    ang = (2.0 * math.pi / n) * bands[:, None] * jnp.asarray(pos)[None, :]
    z = jnp.concatenate([t[None, :], jnp.cos(ang), -jnp.sin(ang)], axis=0)
    z = jnp.pad(z, ((0, LANES - z.shape[0]), (0, 0)))
    w1t = jnp.pad(w1, ((0, LANES - w1.shape[0]), (0, 0))).T
    hidden = w1.shape[1]
    col = lambda v: v.reshape(hidden, 1)
    w3d = jnp.swapaxes(w3.reshape(hidden, HY_ORDER, 2, HY_CH), 0, 2)
    w3d = jnp.swapaxes(w3d, 1, 2).reshape(2, hidden, HY_ORDER * HY_CH)
    rates = jnp.abs(jnp.linspace(HY_MIN_DECAY, HY_MAX_DECAY, HY_CH, dtype=F32)).reshape(1, HY_CH)
    tm = HY_TILE
    half_tiles = n // tm
    full = lambda a: pl.BlockSpec(a.shape, lambda i: (0,) * a.ndim)
    small = (w1t, col(b1), w2.T, col(b2), col(freq))
    return pl.pallas_call(
        functools.partial(_hy_filter_kernel, half_tiles=half_tiles),
        grid=(2 * n // tm,),
        in_specs=[pl.BlockSpec((LANES, tm), lambda i: (0, i)), pl.BlockSpec((tm, 1), lambda i: (i, 0))]
                 + [full(a) for a in small]
                 + [pl.BlockSpec((None, hidden, HY_ORDER * HY_CH), lambda i: ((i >= half_tiles).astype(jnp.int32), 0, 0)),
                    pl.BlockSpec((None, hidden, HY_ORDER * HY_CH), lambda i: (1, 0, 0)), full(rates)],
        out_specs=pl.BlockSpec((HY_ORDER, tm, HY_CH), lambda i: (0, i, 0)),
        out_shape=jax.ShapeDtypeStruct((HY_ORDER, 2 * n, HY_CH), F32),
        compiler_params=_params("parallel"),
        name="hy_kernels",
    )(z, t[:, None], *small, w3d, w3d, rates)


def _pack_complex(z):
    r = z.shape[0] // 2
    bits = lax.bitcast_convert_type(z.astype(BF16).astype(F32), jnp.uint32)
    return lax.bitcast_convert_type(bits[0:r] | (bits[r:2 * r] >> 16), F32)


def _unpack_complex(words):
    p = lax.bitcast_convert_type(words, jnp.uint32)
    re = lax.bitcast_convert_type(p & jnp.uint32(0xFFFF0000), F32)
    im = lax.bitcast_convert_type(p << 16, F32)
    return jnp.concatenate([re, im], axis=0).astype(BF16)


def _load_every(ref, j, count):
    return ref.reshape(count * DFT_STEP, LANES)[pl.ds(j, count, stride=DFT_STEP), :]


def _store_every(ref, j, count, val):
    ref.reshape(count * DFT_STEP, LANES)[pl.ds(j, count, stride=DFT_STEP), :] = val


def _dft_in_kernel(x_ref, g_ref, a_ref):
    n_seq, rh = x_ref.shape[0:2]
    for j in range(DFT_STEP):
        x = jnp.concatenate([_load_every(x_ref.at[b], j, rh) for b in range(n_seq)], axis=1).astype(BF16)
        res = _dot(g_ref[j], x)
        for b in range(n_seq):
            a_ref[b, j] = _pack_complex(res[:, b * LANES:(b + 1) * LANES])


def _dft_in(x4, col, g_in):
    bx, rh, r, _ = x4.shape
    c = HY_CH
    cbs = c // LANES
    return pl.pallas_call(
        _dft_in_kernel,
        grid=(r // DFT_STEP, cbs),
        in_specs=[pl.BlockSpec((bx, rh, DFT_STEP, LANES), lambda i, cb: (0, 0, i, col * cbs + cb)),
                  pl.BlockSpec((DFT_STEP, 2 * r, rh), lambda i, cb: (i, 0, 0))],
        out_specs=pl.BlockSpec((bx, DFT_STEP, r, LANES), lambda i, cb: (0, i, 0, cb)),
        out_shape=jax.ShapeDtypeStruct((bx, r, r, c), F32),
        compiler_params=_params("parallel", "parallel"),
        name="dft_in",
    )(x4, g_in)


def _stage2_operand(a_ref, j, r):
    return jnp.concatenate([_unpack_complex(_load_every(a_ref.at[b], j, r)) for b in range(a_ref.shape[0])],
                           axis=1)


def _dft_mid_kernel(a_ref, k_ref, ff_ref, fi_ref, b_ref):
    r = ff_ref.shape[0] // 2
    n_seq = a_ref.shape[0]
    for j in range(DFT_STEP):
        both = jnp.concatenate([_stage2_operand(a_ref, j, r), _stage2_operand(k_ref, j, r)], axis=1)
        spec = _dot(ff_ref[...], both)
        s, kern = spec[:, 0:n_seq * LANES], spec[:, n_seq * LANES:]
        sr, si = s[0:r], s[r:2 * r]
        kr = jnp.concatenate([kern[0:r]] * n_seq, axis=1)
        ki = jnp.concatenate([kern[r:2 * r]] * n_seq, axis=1)
        p = jnp.concatenate([sr * kr - si * ki, sr * ki + si * kr], axis=0).astype(BF16)
        back = _dot(fi_ref[...], p)
        for b in range(n_seq):
            b_ref[b, j] = _pack_complex(back[:, b * LANES:(b + 1) * LANES])


def _dft_mid(a, kern1, order, f_fwd, f_inv):
    bsz, r, _, c = a.shape
    return pl.pallas_call(
        _dft_mid_kernel,
        grid=(r // DFT_STEP, c // LANES),
        in_specs=[pl.BlockSpec((bsz, r, DFT_STEP, LANES), lambda i, cb: (0, 0, i, cb)),
                  pl.BlockSpec((1, r, DFT_STEP, LANES), lambda i, cb: (order, 0, i, cb)),
                  pl.BlockSpec(f_fwd.shape, lambda i, cb: (0, 0)),
                  pl.BlockSpec(f_inv.shape, lambda i, cb: (0, 0))],
        out_specs=pl.BlockSpec((bsz, DFT_STEP, r, LANES), lambda i, cb: (0, i, 0, cb)),
        out_shape=jax.ShapeDtypeStruct((bsz, r, r, c), F32),
        compiler_params=_params("parallel", "parallel"),
        name="dft_mid",
    )(a, kern1, f_fwd, f_inv)


def _dft_out_kernel(b_ref, g_ref, u_ref, x_ref, bias_ref, o_ref):
    n_seq, r = b_ref.shape[0:2]
    rh = o_ref.shape[1]
    for j in range(DFT_STEP):
        rhs = jnp.concatenate([_unpack_complex(_load_every(b_ref.at[b], j, r)) for b in range(n_seq)], axis=1)
        y = _dot(g_ref[j], rhs)
        for b in range(n_seq):
            yb = y[:, b * LANES:(b + 1) * LANES]
            _store_every(o_ref.at[b], j, rh,
                         _load_every(x_ref.at[b], j, rh) * (yb + _load_every(u_ref.at[b], j, rh) * bias_ref[...]))


def _dft_out(bm, g_out, u4, u_col, x4, x_col, bias):
    bsz, r, _, c = bm.shape
    rh = r // 2
    cbs = c // LANES
    seq = lambda col: pl.BlockSpec((bsz, rh, DFT_STEP, LANES), lambda i, cb: (0, 0, i, col * cbs + cb))
    return pl.pallas_call(
        _dft_out_kernel,
        grid=(r // DFT_STEP, cbs),
        in_specs=[pl.BlockSpec((bsz, r, DFT_STEP, LANES), lambda i, cb: (0, 0, i, cb)),
                  pl.BlockSpec((DFT_STEP, rh, 2 * r), lambda i, cb: (i, 0, 0)),
                  seq(u_col), seq(x_col), pl.BlockSpec((1, LANES), lambda i, cb: (0, cb))],
        out_specs=seq(0),
        out_shape=jax.ShapeDtypeStruct((bsz, rh, r, c), F32),
        compiler_params=_params("parallel", "parallel"),
        name="dft_out",
    )(bm, g_out, u4, x4, bias.reshape(1, c))


def _hyena_filter_stage1(n, kern):
    r, g_full = _dft_tables(n)[0], _dft_tables(n)[5]
    return _dft_in(kern.reshape(-1, r, r, HY_CH), 0, g_full)


def _hyena(hy_in, kspec, conv_bias):
    bsz, n, _ = hy_in.shape
    r, g_in, g_out, f_fwd, f_inv, _ = _dft_tables(n)
    seq4 = hy_in.reshape(bsz, r // 2, r, 3 * HY_CH)
    zz = _dft_out(_dft_mid(_dft_in(seq4, 0, g_in), kspec, 0, f_fwd, f_inv), g_out,
                  seq4, 0, seq4, 1, conv_bias[0])
    out = _dft_out(_dft_mid(_dft_in(zz, 0, g_in), kspec, 1, f_fwd, f_inv), g_out,
                   zz, 0, seq4, 2, conv_bias[1])
    return out.reshape(bsz, n, HY_CH)


def _reorder_ab(w):
    offs = np.cumsum([0, GLA_KEY_W, GLA_KEY_W, GLA_VAL_W, GLA_VAL_W, GLA_LOW_RANK, GLA_LOW_RANK,
                      HG_KEY_W, HG_KEY_W, HG_KEY_W, HG_VAL_W, HG_VAL_W]).tolist()
    gq, gk, gv, gg, lr_f, lr_b, hq, hf_f, hf_b, hi, hg = range(11)
    cols = lambda first, last: w[:, offs[first]:offs[last + 1]].astype(BF16)
    lr = jnp.pad(cols(lr_f, lr_b), ((0, 0), (0, AB_PAD_COLS - AB_LR - 2 * GLA_LOW_RANK)))
    return [cols(gq, gk), cols(gg, gg), cols(hq, hf_b), cols(hg, hg), lr], [cols(gv, gv), cols(hi, hi)]


def _reorder_cd(w):
    hy_end = 3 * HY_CH
    z_end = hy_end + MB_INNER
    xbc_end = z_end + MB_INNER + 2 * MB_BC_W
    pad = lambda a: jnp.pad(a, ((0, 0), (0, LANES - MB_HEADS)))
    dt = jnp.concatenate([pad(w[:, xbc_end:xbc_end + MB_HEADS]), pad(w[:, xbc_end + MB_HEADS:])], axis=-1)
    return [w[:, :hy_end].astype(BF16), w[:, z_end:xbc_end].astype(BF16), w[:, hy_end:z_end].astype(BF16),
            dt.astype(BF16)]


def kernel(x, c, ctx, c_ctx, ada_w, ada_b, norm_mix_g, norm_ffn_g, norm_out_g, ab_w_in, ab_w_out, gla_gate_w, gla_gate_b, gla_norm_g, hg_lb, hg_norm_g, cd_w_in, cd_w_out, hy_short_w, hy_short_b, hy_w1, hy_b1, hy_w2, hy_b2, hy_w3, hy_freq, hy_bias, mb_conv_w, mb_conv_b, mb_dt_bias, mb_a_log, mb_d, mb_norm_g, router_w, router_b, moe_w_gate, moe_w_up, moe_w_down):
    bsz, n_lat, d = x.shape
    n_ctx = ctx.shape[1]
    assert ada_w.shape[0] == 2 and ab_w_in.shape[0] == 1 and cd_w_in.shape[0] == 1

    cond = jnp.zeros((ADALN_ROWS, d), F32).at[:bsz].set(c).at[bsz].set(c_ctx)
    m = _adaln(cond, ada_w, ada_b)

    def mods(layer):
        lat = m[layer, :bsz].reshape(bsz, 6, d)
        cx = jnp.broadcast_to(m[layer, bsz].reshape(1, 6, d), (bsz, 6, d))
        both = jnp.stack([cx, lat], axis=1)
        return [both[:, :, j][:, :, None, :] for j in range(6)]

    router_w_pad = jnp.zeros((d, LANES), F32).at[:, :N_EXPERTS].set(router_w)

    sh_m, sc_m, gt_m, sh_f, sc_f, gt_f = mods(0)
    proj, vals = _norm_proj(ctx, x, norm_mix_g[0], sh_m, sc_m, *_reorder_ab(ab_w_in[0]))
    gwp = [jnp.zeros((LANES, GLA_KEY_W), F32).at[GLA_LOW_RANK * dd:GLA_LOW_RANK * (dd + 1)].set(gla_gate_w[0, dd])
           for dd in range(2)]
    o_gla = _gla_scan(proj, vals, gwp, [gla_gate_b[0, dd].reshape(1, -1) for dd in range(2)], n_ctx)
    o_hg = _hgrn_scan(proj, vals, [hg_lb[dd].astype(F32) for dd in range(2)], 0, n_ctx)
    h, pre_out = _mix_out_ab(ctx, x, o_gla, o_hg, proj, gla_norm_g[0], hg_norm_g[0], ab_w_out[0].astype(BF16),
                             gt_m, (norm_ffn_g[0], sh_f, sc_f, router_w_pad, router_b))
    filter_params = (hy_w1[0], hy_b1[0], hy_w2[0], hy_b2[0], hy_w3[0], hy_freq[0])
    side = ((lambda params: _hy_kernels(n_lat, *params), filter_params),
            functools.partial(_hyena_filter_stage1, n_lat))
    h, kspec = _moe(h, pre_out, gt_f, 0, moe_w_gate, moe_w_up, moe_w_down, n_ctx, side=side)

    sh_m, sc_m, gt_m, sh_f, sc_f, gt_f = mods(1)
    conv_w = jnp.concatenate([hy_short_w[0], mb_conv_w[0]], axis=0).T
    conv_b = jnp.concatenate([hy_short_b[0], mb_conv_b[0]]).reshape(1, -1)
    hy_in, xbc, zdt = _norm_proj_conv(h, norm_mix_g[1], sh_m, sc_m, _reorder_cd(cd_w_in[0]),
                                      conv_w, conv_b, 3 * HY_CH, n_ctx)
    hy = _hyena(hy_in, kspec, hy_bias[0])
    y_ssd = _ssd_scan(xbc, zdt, MB_INNER // LANES, mb_dt_bias[0], mb_a_log[0], n_ctx)
    d_skip_x = jnp.repeat(mb_d[0], MB_HEAD_DIM).reshape(1, MB_INNER)
    h, pre_out = _mix_out_cd(h, hy, y_ssd, xbc, zdt, 0, d_skip_x, mb_norm_g[0], cd_w_out[0].astype(BF16), gt_m,
                             n_ctx, (norm_ffn_g[1], sh_f, sc_f, router_w_pad, router_b))
    return _moe(h, pre_out, gt_f, 1, moe_w_gate, moe_w_up, moe_w_down, 0, final_g=norm_out_g)
```

```python
import functools
import math

import numpy as np
import jax
import jax.numpy as jnp
from jax import lax
from jax.experimental import pallas as pl
from jax.experimental.pallas import tpu as pltpu
from jax.experimental.pallas import tpu_sc as plsc

NORM_EPS = 1e-6
GLA_HEADS, GLA_DK, GLA_DV, GLA_LOW_RANK, GLA_TAU = 4, 64, 128, 16, 16.0
GLA_KEY_W, GLA_VAL_W = GLA_HEADS * GLA_DK, GLA_HEADS * GLA_DV
HG_HEADS, HG_EXPAND, HG_DV = 4, 128, 128
HG_KEY_W, HG_VAL_W = HG_HEADS * HG_EXPAND, HG_HEADS * HG_DV
HY_CH, HY_ORDER, HY_BANDS = 512, 2, 16
HY_MIN_DECAY = math.log(1e-2) / 1.5
HY_MAX_DECAY = math.log(1e-2) / 0.3
MB_HEADS, MB_HEAD_DIM, MB_GROUPS, MB_STATE = 8, 64, 2, 128
MB_INNER = MB_HEADS * MB_HEAD_DIM
MB_BC_W = MB_GROUPS * MB_STATE
N_EXPERTS, N_GROUPS, TOP_K, MOE_BLOCK = 16, 4, 2, 256
EXPERTS_PER_GROUP = N_EXPERTS // N_GROUPS

LANES = 128
SCAN_CHUNK = 64
SCAN_BLOCK = 256
SSD_CHUNK = 128
SSD_BLOCK = 256
ROW_TILE = 256
VMEM_LIMIT = 56 * 1024 * 1024

BF16 = jnp.bfloat16
F32 = jnp.float32


def _params(*sem):
    return pltpu.CompilerParams(dimension_semantics=sem, vmem_limit_bytes=VMEM_LIMIT)


def _split3(x):
    hi = x.astype(BF16)
    r1 = x - hi.astype(F32)
    mid = r1.astype(BF16)
    lo = (r1 - mid.astype(F32)).astype(BF16)
    return hi, mid, lo


def _dot(a, b):
    return jnp.dot(a, b, preferred_element_type=F32)


def _dot_nt(a, b):
    return lax.dot_general(a, b, (((1,), (1,)), ((), ())), preferred_element_type=F32)


def _dot_tn(a, b):
    return lax.dot_general(a, b, (((0,), (0,)), ((), ())), preferred_element_type=F32)


def _dot_f32(a, b):
    ah = a.astype(BF16)
    al = (a - ah.astype(F32)).astype(BF16)
    bh = b.astype(BF16)
    bl = (b - bh.astype(F32)).astype(BF16)
    return _dot(ah, bh) + (_dot(ah, bl) + _dot(al, bh))


def _sigmoid(x):
    return 0.5 * jnp.tanh(0.5 * x) + 0.5


def _silu(x):
    return x * _sigmoid(x)


def _softplus(x):
    return jnp.maximum(x, 0.0) + jnp.log(1.0 + jnp.exp(-jnp.abs(x)))


def _pack_bf16_pairs(x):
    bits = lax.bitcast_convert_type(x.astype(BF16).astype(F32), jnp.uint32)
    half = x.shape[1] // 2
    return bits[:, :half] | (bits[:, half:] >> 16)


def _unpack_bf16_pairs(p):
    hi = lax.bitcast_convert_type(p & jnp.uint32(0xFFFF0000), F32)
    lo = lax.bitcast_convert_type(p << 16, F32)
    return jnp.concatenate([hi, lo], axis=1)


def _rms(x, g):
    return x * lax.rsqrt(jnp.mean(x * x, axis=-1, keepdims=True) + NORM_EPS) * g


def _adaln_kernel(c_ref, w_ref, b_ref, o_ref):
    o_ref[...] = _dot_f32(_silu(c_ref[...]), w_ref[...]) + b_ref[...]


ADALN_ROWS = 8
ADALN_TILE = 1536


def _adaln(cond, w, b):
    n_l, d, n6 = w.shape
    tn = ADALN_TILE
    rows = cond.shape[0]
    return pl.pallas_call(
        _adaln_kernel,
        grid=(n_l, n6 // tn),
        in_specs=[pl.BlockSpec((rows, d), lambda l, j: (0, 0)),
                  pl.BlockSpec((None, d, tn), lambda l, j: (l, 0, j)),
                  pl.BlockSpec((None, 1, tn), lambda l, j: (l, 0, j))],
        out_specs=pl.BlockSpec((None, rows, tn), lambda l, j: (l, 0, j)),
        out_shape=jax.ShapeDtypeStruct((n_l, rows, n6), F32),
        compiler_params=_params("parallel", "parallel"),
        name="adaln",
    )(cond, w, b.reshape(n_l, 1, n6))


def _project(u, w_refs):
    return jnp.concatenate([_dot(u, w_ref[...]) for w_ref in w_refs], axis=-1)


def _joint_rows_specs(ctx, x):
    tm = ROW_TILE
    ctx_tiles = ctx.shape[1] // tm
    d = ctx.shape[2]
    return [pl.BlockSpec((None, tm, d), lambda b, i: (b, jnp.minimum(i, ctx_tiles - 1), 0)),
            pl.BlockSpec((None, tm, d), lambda b, i: (b, jnp.maximum(i - ctx_tiles, 0), 0))], ctx_tiles


def _joint_rows(ctx_ref, x_ref, ctx_tiles):
    return jnp.where(pl.program_id(1) < ctx_tiles, ctx_ref[...], x_ref[...])


def _norm_proj_kernel(c_ref, x_ref, g_ref, sh_ref, sc_ref, *refs, ctx_tiles, n_main):
    u = _rms(_joint_rows(c_ref, x_ref, ctx_tiles), g_ref[...]) * (1.0 + sc_ref[...]) + sh_ref[...]
    u = u.astype(BF16)
    refs[-2][...] = _project(u, refs[:n_main])
    refs[-1][...] = _project(u, refs[n_main:-2]).astype(BF16)


def _norm_proj(ctx, x, g, shift, scale, ws, ws_bf16):
    bsz, n_ctx, d = ctx.shape
    t = n_ctx + x.shape[1]
    n, n_bf = sum(w.shape[1] for w in ws), sum(w.shape[1] for w in ws_bf16)
    tm = ROW_TILE
    seg = lambda b, i: (b, (i * tm >= n_ctx).astype(jnp.int32), 0, 0)
    row_specs, ctx_tiles = _joint_rows_specs(ctx, x)
    return pl.pallas_call(
        functools.partial(_norm_proj_kernel, ctx_tiles=ctx_tiles, n_main=len(ws)),
        grid=(bsz, t // tm),
        in_specs=row_specs + [pl.BlockSpec((1, d), lambda b, i: (0, 0)),
                              pl.BlockSpec((None, None, 1, d), seg),
                              pl.BlockSpec((None, None, 1, d), seg)]
                 + [pl.BlockSpec(w.shape, lambda b, i: (0, 0)) for w in list(ws) + list(ws_bf16)],
        out_specs=[pl.BlockSpec((None, tm, n), lambda b, i: (b, i, 0)),
                   pl.BlockSpec((None, tm, n_bf), lambda b, i: (b, i, 0))],
        out_shape=[jax.ShapeDtypeStruct((bsz, t, n), F32), jax.ShapeDtypeStruct((bsz, t, n_bf), BF16)],
        compiler_params=_params("parallel", "parallel"),
        name="norm_proj",
    )(ctx, x, g.reshape(1, d), shift, scale, *ws, *ws_bf16)


def _scan_constants(c, reverse):
    t = np.arange(c)[:, None]
    u = np.arange(c)[None, :]
    sels = [u <= t, u > t]
    masks = []
    m = c // 2
    while m >= 1:
        blk = t // (2 * m)
        upper_t = (t % (2 * m)) >= m
        r = blk * (2 * m) + m - 1
        s_blk = u // (2 * m)
        upper_s = (u % (2 * m)) >= m
        sels.append((upper_t & (u > r) & (u <= t)) | ((~upper_t) & (u > t) & (u <= r)))
        masks.append((blk == s_blk) & upper_t & (~upper_s))
        m //= 2
    masks.append(t == u)
    sel = np.stack(sels).astype(np.float32)
    msk = np.stack(masks).astype(np.float32)
    if reverse:
        sel = sel[:, ::-1, ::-1]
        msk = msk[:, ::-1, ::-1]
    return np.ascontiguousarray(sel.reshape(-1, c)), np.ascontiguousarray(msk)


def _chunk_order(i, n_ctx_chunks, n_chunks, reverse):
    if not reverse:
        return i
    return jnp.where(i < n_ctx_chunks, n_ctx_chunks - 1 - i, n_chunks - 1 - (i - n_ctx_chunks))


GROUP_KEYS = 256


def _decay_chunk(q, k, v, la, consts, st_ref, heads, dk, dv):
    sel_ref, mask_ref, hm_ref, hmb_ref, vm_ref = consts
    c = q.shape[0]
    n_lvl = mask_ref.shape[0] - 1
    hpg = GROUP_KEYS // dk
    cs = _dot(sel_ref[...], jnp.concatenate(_split3(la), axis=0))
    e_q = jnp.exp(cs[0:c])
    e_k = jnp.exp(cs[c:2 * c])
    e_tot = jnp.exp(jnp.sum(la, axis=0, keepdims=True))
    vb = v.astype(BF16)
    outs = []
    for g in range(heads // hpg):
        ks = slice(g * GROUP_KEYS, (g + 1) * GROUP_KEYS)
        vs = slice(g * hpg * dv, (g + 1) * hpg * dv)
        qg, kg = q[:, ks], k[:, ks]
        key_stack = lambda x: jnp.concatenate([x.astype(BF16) * hmb_ref[h] for h in range(hpg)], axis=0)
        att = mask_ref[n_lvl] * _dot_nt(qg.astype(BF16), key_stack(kg))
        for l in range(n_lvl):
            e = jnp.exp(cs[(2 + l) * c:(3 + l) * c, ks])
            att = att + mask_ref[l] * _dot_nt((qg * e).astype(BF16), key_stack(kg * e))
        v_blocks = jnp.concatenate([vb[:, vs] * vm_ref[h] for h in range(hpg)], axis=0)
        intra = _dot(att.astype(BF16), v_blocks)
        st = st_ref[g]
        q_stack = jnp.concatenate([(qg * e_q[:, ks]) * hm_ref[h] for h in range(hpg)], axis=0)
        inter = _dot_nt(q_stack.astype(BF16), st.astype(BF16))
        upd = _dot_tn(vb[:, vs], (kg * e_k[:, ks]).astype(BF16))
        new = st * e_tot[:, ks]
        for h in range(hpg):
            new = new + upd[h * dv:(h + 1) * dv] * hm_ref[h]
        st_ref[g] = new
        outs.append(intra + jnp.concatenate([inter[h * c:(h + 1) * c] for h in range(hpg)], axis=-1))
    return jnp.concatenate(outs, axis=-1)


def _log_sigmoid(x):
    return jnp.minimum(x, 0.0) - jnp.log(1.0 + jnp.exp(-jnp.abs(x)))


def _gla_kernel(*refs):
    ins, head_masks, (o_refs, st_refs) = (refs[0:8], refs[8:16]), refs[16:19], (refs[19:21], refs[21:23])

    @pl.when(pl.program_id(1) == 0)
    def _():
        for st_ref in st_refs:
            st_ref[...] = jnp.zeros_like(st_ref)

    for d, ((q_ref, k_ref, v_ref, lr_ref, gw_ref, gb_ref, sel_ref, mask_ref), o_ref, st_ref) in enumerate(
            zip(ins, o_refs, st_refs)):
        z = _dot_f32(lr_ref[...], gw_ref[...]) + gb_ref[...]
        la = _log_sigmoid(z) * (1.0 / GLA_TAU)
        q = q_ref[...] * (GLA_DK ** -0.5)
        k, v = k_ref[...], v_ref[...]
        for rows in _sub_chunks(q.shape[0], d == 1):
            o_ref[rows, :] = _decay_chunk(q[rows], k[rows], v[rows], la[rows], (sel_ref, mask_ref) + head_masks,
                                          st_ref, GLA_HEADS, GLA_DK, GLA_DV).astype(o_ref.dtype)


def _hgrn_kernel(*refs, layer):
    ins, head_masks, (o_refs, st_refs) = (refs[0:6], refs[6:12]), refs[12:15], (refs[15:17], refs[17:19])

    @pl.when(pl.program_id(1) == 0)
    def _():
        for st_ref in st_refs:
            st_ref[...] = jnp.zeros_like(st_ref)

    for d, ((q_ref, f_ref, v_ref, lb_ref, sel_ref, mask_ref), o_ref, st_ref) in enumerate(
            zip(ins, o_refs, st_refs)):
        e = jnp.exp(lb_ref[...] - jnp.max(lb_ref[...], axis=0, keepdims=True))
        lb = jnp.sum(e[0:layer + 1], axis=0, keepdims=True) / jnp.sum(e, axis=0, keepdims=True)
        f = lb + (1.0 - lb) * _sigmoid(f_ref[...])
        q, k, v, la = _silu(q_ref[...]), 1.0 - f, v_ref[...], jnp.log(f)
        for rows in _sub_chunks(q.shape[0], d == 1):
            o_ref[rows, :] = _decay_chunk(q[rows], k[rows], v[rows], la[rows], (sel_ref, mask_ref) + head_masks,
                                          st_ref, HG_HEADS, HG_EXPAND, HG_DV).astype(o_ref.dtype)


def _sub_chunks(rows, reverse, chunk=SCAN_CHUNK):
    order = range(rows // chunk)
    return [slice(j * chunk, (j + 1) * chunk) for j in (reversed(order) if reverse else order)]


def _scan_specs(blk, n_ctx, t, reverse, chunk=None, stacked_heads=1):
    n_blocks = t // blk
    order = functools.partial(_chunk_order, n_ctx_chunks=n_ctx // blk, n_chunks=n_blocks, reverse=reverse)

    def col(width, idx):
        return pl.BlockSpec((None, blk, width), lambda b, i: (b, order(i), idx))

    sel, msk = _scan_constants(chunk or blk, reverse)
    sel3 = np.concatenate([sel, sel, sel], axis=1)
    msk = np.tile(msk, (1, 1, stacked_heads))
    const = lambda a: pl.BlockSpec(a.shape, lambda b, i: (0,) * a.ndim)
    return n_blocks, col, const, jnp.asarray(sel3, BF16), jnp.asarray(msk, F32)


def _head_masks(dk, dv):
    hpg = GROUP_KEYS // dk
    hm = np.zeros((hpg, 1, GROUP_KEYS), np.float32)
    vm = np.zeros((hpg, 1, hpg * dv), np.float32)
    for h in range(hpg):
        hm[h, 0, h * dk:(h + 1) * dk] = 1.0
        vm[h, 0, h * dv:(h + 1) * dv] = 1.0
    return jnp.asarray(hm), jnp.asarray(hm, BF16), jnp.asarray(vm, BF16)


AB_Q, AB_K, AB_G = 0, 256, 512
AB_HQ, AB_HF, AB_HG, AB_LR = 1024, 1536, 2560, 3072
AB_PAD_COLS = 3200
AB_V, AB_HI = 0, 1


def _gla_scan(proj, vals, gate_w_pad, gate_b, n_ctx):
    bsz, t, _ = proj.shape
    hpg = GROUP_KEYS // GLA_DK
    in_specs, args, outs = [], [], []
    for d in range(2):
        n_blocks, col, const, sel, msk = _scan_specs(SCAN_BLOCK, n_ctx, t, d == 1, SCAN_CHUNK, hpg)
        in_specs += [col(GLA_KEY_W, AB_Q // GLA_KEY_W), col(GLA_KEY_W, AB_K // GLA_KEY_W),
                     col(GLA_VAL_W, AB_V), col(LANES, AB_LR // LANES),
                     const(gate_w_pad[d]), const(gate_b[d]), const(sel), const(msk)]
        args += [proj, proj, vals, proj, gate_w_pad[d], gate_b[d], sel, msk]
        outs.append(col(GLA_VAL_W, 0))
    hm = _head_masks(GLA_DK, GLA_DV)
    return pl.pallas_call(
        _gla_kernel,
        grid=(bsz, n_blocks),
        in_specs=in_specs + [const(m) for m in hm],
        out_specs=outs,
        out_shape=[jax.ShapeDtypeStruct((bsz, t, GLA_VAL_W), BF16)] * 2,
        scratch_shapes=[pltpu.VMEM((GLA_HEADS // hpg, GLA_DV, GROUP_KEYS), F32)] * 2,
        compiler_params=_params("parallel", "arbitrary"),
        name="gla_scan",
    )(*args, *hm)


def _hgrn_scan(proj, vals, lb, layer, n_ctx):
    bsz, t, _ = proj.shape
    hpg = GROUP_KEYS // HG_EXPAND
    in_specs, args, outs = [], [], []
    for d in range(2):
        n_blocks, col, const, sel, msk = _scan_specs(SCAN_BLOCK, n_ctx, t, d == 1, SCAN_CHUNK, hpg)
        in_specs += [col(HG_KEY_W, AB_HQ // HG_KEY_W), col(HG_KEY_W, AB_HF // HG_KEY_W + d),
                     col(HG_VAL_W, AB_HI), const(lb[d]), const(sel), const(msk)]
        args += [proj, proj, vals, lb[d], sel, msk]
        outs.append(col(HG_VAL_W, 0))
    hm = _head_masks(HG_EXPAND, HG_DV)
    return pl.pallas_call(
        functools.partial(_hgrn_kernel, layer=layer),
        grid=(bsz, n_blocks),
        in_specs=in_specs + [const(m) for m in hm],
        out_specs=outs,
        out_shape=[jax.ShapeDtypeStruct((bsz, t, HG_VAL_W), BF16)] * 2,
        scratch_shapes=[pltpu.VMEM((HG_HEADS // hpg, HG_DV, GROUP_KEYS), F32)] * 2,
        compiler_params=_params("parallel", "arbitrary"),
        name="hgrn_scan",
    )(*args, *hm)


def _mix_out_ab_kernel(*refs, ctx_tiles):
    (c_ref, x_ref, gf_ref, gb_ref, hf_ref, hb_ref, gg_ref, hg_ref, gn_ref, hn_ref, w_ref, gt_ref) = refs[0:12]
    pre_in, o_ref, pre_out = refs[12:12 + N_FFN_PRE_IN], refs[12 + N_FFN_PRE_IN], refs[13 + N_FFN_PRE_IN:]
    feats = []
    both = lambda fwd_ref, bwd_ref: fwd_ref[...].astype(F32) + bwd_ref[...].astype(F32)
    for o, gate, g in ((both(gf_ref, gb_ref), gg_ref[...], gn_ref[...]),
                       (both(hf_ref, hb_ref), hg_ref[...], hn_ref[...])):
        for hd in range(o.shape[-1] // LANES):
            s = slice(hd * LANES, (hd + 1) * LANES)
            feats.append(_rms(o[:, s], g) * _silu(gate[:, s]))
    feat = jnp.concatenate(feats, axis=-1).astype(BF16)
    h_new = _joint_rows(c_ref, x_ref, ctx_tiles) + gt_ref[...] * _dot(feat, w_ref[...])
    o_ref[...] = h_new
    _ffn_pre_body(h_new, *pre_in, *pre_out)


def _mix_out_ab(ctx, x, o_gla, o_hg, proj, gla_norm_g, hg_norm_g, w_out, gate, pre):
    bsz, n_ctx, d = ctx.shape
    t = n_ctx + x.shape[1]
    tm = ROW_TILE
    seg = lambda b, i: (b, (i * tm >= n_ctx).astype(jnp.int32), 0, 0)
    row = lambda width, idx: pl.BlockSpec((None, tm, width), lambda b, i: (b, i, idx))
    vec = pl.BlockSpec((1, LANES), lambda b, i: (0, 0))
    p_in, p_args, p_out, p_shape, p_scratch = _ffn_pre_parts(bsz, t, d, *pre, seg)
    row_specs, ctx_tiles = _joint_rows_specs(ctx, x)
    outs = pl.pallas_call(
        functools.partial(_mix_out_ab_kernel, ctx_tiles=ctx_tiles),
        grid=(bsz, t // tm),
        in_specs=row_specs + [row(GLA_VAL_W, 0), row(GLA_VAL_W, 0), row(HG_VAL_W, 0), row(HG_VAL_W, 0),
                  row(GLA_VAL_W, AB_G // GLA_VAL_W), row(HG_VAL_W, AB_HG // HG_VAL_W), vec, vec,
                  pl.BlockSpec(w_out.shape, lambda b, i: (0, 0)),
                  pl.BlockSpec((None, None, 1, d), seg)] + p_in,
        out_specs=[row(d, 0)] + p_out,
        out_shape=[jax.ShapeDtypeStruct((bsz, t, d), F32)] + p_shape,
        scratch_shapes=p_scratch,
        compiler_params=_params("arbitrary", "arbitrary"),
        name="mix_out_ab",
    )(ctx, x, o_gla[0], o_gla[1], o_hg[0], o_hg[1], proj, proj, gla_norm_g.reshape(1, -1),
      hg_norm_g.reshape(1, -1), w_out, gate, *p_args)
    return outs[0], outs[1:]


HALO = 8


def _norm_proj_conv_kernel(h_ref, hp_ref, hn_ref, g_ref, sh_ref, sc_ref, cw_ref, cb_ref, *refs,
                           ctx_tiles, n_tiles):
    w_refs, (hy_ref, xbc_ref, zdt_ref) = refs[:-3], refs[-3:]
    i = pl.program_id(1)
    tm = h_ref.shape[0]
    hh = jnp.concatenate([hp_ref[...], h_ref[...], hn_ref[...]], axis=0)
    u = _rms(hh, g_ref[...]) * (1.0 + sc_ref[...]) + sh_ref[...]
    p = _project(u.astype(BF16), w_refs)
    n_conv = cw_ref.shape[1]
    n_hy = hy_ref.shape[1]
    pc = p[:, 0:n_conv]
    rows_all = tm + 2 * HALO
    cur = pc[HALO:HALO + tm]
    prev = pltpu.roll(pc, 1, axis=0)[HALO:HALO + tm]
    nxt = pltpu.roll(pc, rows_all - 1, axis=0)[HALO:HALO + tm]
    first = jnp.logical_or(i == 0, i == ctx_tiles)
    last = jnp.logical_or(i == ctx_tiles - 1, i == n_tiles - 1)
    rows = lax.broadcasted_iota(jnp.int32, cur.shape, 0)
    prev = jnp.where(jnp.logical_and(first, rows == 0), 0.0, prev)
    nxt = jnp.where(jnp.logical_and(last, rows == tm - 1), 0.0, nxt)
    y = prev * cw_ref[0:1, :] + cur * cw_ref[1:2, :] + nxt * cw_ref[2:3, :] + cb_ref[...]
    hy_ref[...] = y[:, 0:n_hy]
    xbc_ref[...] = _silu(y[:, n_hy:n_conv])
    zdt_ref[...] = p[HALO:HALO + tm, n_conv:]


def _norm_proj_conv(h, g, shift, scale, ws, conv_w, conv_b, n_hy, n_ctx):
    bsz, t, d = h.shape
    n = sum(w.shape[1] for w in ws)
    n_conv = conv_w.shape[1]
    tm = ROW_TILE
    n_tiles, ctx_tiles = t // tm, n_ctx // tm
    r8 = tm // HALO
    last8 = t // HALO - 1
    seg = lambda b, i: (b, (i >= ctx_tiles).astype(jnp.int32), 0, 0)
    kern = functools.partial(_norm_proj_conv_kernel, ctx_tiles=ctx_tiles, n_tiles=n_tiles)
    return pl.pallas_call(
        kern,
        grid=(bsz, n_tiles),
        in_specs=[pl.BlockSpec((None, tm, d), lambda b, i: (b, i, 0)),
                  pl.BlockSpec((None, HALO, d), lambda b, i: (b, jnp.maximum(i * r8 - 1, 0), 0)),
                  pl.BlockSpec((None, HALO, d), lambda b, i: (b, jnp.minimum((i + 1) * r8, last8), 0)),
                  pl.BlockSpec((1, d), lambda b, i: (0, 0)),
                  pl.BlockSpec((None, None, 1, d), seg),
                  pl.BlockSpec((None, None, 1, d), seg),
                  pl.BlockSpec((3, n_conv), lambda b, i: (0, 0)),
                  pl.BlockSpec((1, n_conv), lambda b, i: (0, 0))]
                 + [pl.BlockSpec(w.shape, lambda b, i: (0, 0)) for w in ws],
        out_specs=[pl.BlockSpec((None, tm, n_hy), lambda b, i: (b, jnp.maximum(i - ctx_tiles, 0), 0)),
                   pl.BlockSpec((None, tm, n_conv - n_hy), lambda b, i: (b, i, 0)),
                   pl.BlockSpec((None, tm, n - n_conv), lambda b, i: (b, i, 0))],
        out_shape=[jax.ShapeDtypeStruct((bsz, t - n_ctx, n_hy), F32),
                   jax.ShapeDtypeStruct((bsz, t, n_conv - n_hy), F32),
                   jax.ShapeDtypeStruct((bsz, t, n - n_conv), F32)],
        compiler_params=_params("parallel", "arbitrary"),
        name="norm_proj_conv",
    )(h, h, h, g.reshape(1, d), shift, scale, conv_w, conv_b, *ws)


def _ssd_kernel(*refs):
    ins, hexp_ref, o_refs, st_refs = (refs[0:7], refs[7:14]), refs[14], refs[15:17], refs[17:19]

    @pl.when(pl.program_id(1) == 0)
    def _():
        for st_ref in st_refs:
            st_ref[...] = jnp.zeros_like(st_ref)

    for d, ((xbc_ref, dt_ref, bias_ref, alog_ref, mq_ref, mk_ref, mask_ref), o_ref, st_ref) in enumerate(
            zip(ins, o_refs, st_refs)):
        for rows in _sub_chunks(xbc_ref.shape[0], d == 1, SSD_CHUNK):
            _ssd_chunk(rows, xbc_ref, dt_ref, bias_ref, alog_ref, hexp_ref, mq_ref, mk_ref, mask_ref, o_ref, st_ref)


def _ssd_chunk(rows, xbc_ref, dt_ref, bias_ref, alog_ref, hexp_ref, mq_ref, mk_ref, mask_ref, o_ref, st_ref):
    c = SSD_CHUNK
    hpg = MB_HEADS // MB_GROUPS
    gw = hpg * MB_HEAD_DIM
    dt = _softplus(dt_ref[rows, :] + bias_ref[...])
    la = -dt * jnp.exp(alog_ref[...])
    la3 = jnp.concatenate(_split3(la), axis=0)
    cq = _dot(mq_ref[...], la3)
    ck = _dot(mk_ref[...], la3)
    cq_t = lax.dot_general(la3, mq_ref[...], (((0,), (1,)), ((), ())), preferred_element_type=F32)
    tot = jnp.broadcast_to(jnp.sum(la, axis=0, keepdims=True), (8, la.shape[1]))
    per_head = jnp.concatenate([dt, cq, ck, tot], axis=0)
    per_lane = _dot(jnp.concatenate(_split3(per_head), axis=1), hexp_ref[...])
    dt_x = per_lane[0:c]
    eq_x = jnp.exp(per_lane[c:2 * c])
    ek_x = jnp.exp(per_lane[2 * c:3 * c])
    etot_x = jnp.exp(per_lane[3 * c:3 * c + 1])
    xs = xbc_ref[rows, 0:MB_INNER] * dt_x
    mask = mask_ref[...]
    outs = []
    for g in range(MB_GROUPS):
        bm = xbc_ref[rows, MB_INNER + g * MB_STATE:MB_INNER + (g + 1) * MB_STATE].astype(BF16)
        cm = xbc_ref[rows, MB_INNER + MB_BC_W + g * MB_STATE:MB_INNER + MB_BC_W + (g + 1) * MB_STATE].astype(BF16)
        cb = _dot_nt(cm, bm)
        st = st_ref[g]
        gs = slice(g * gw, (g + 1) * gw)
        y_inter = _dot(cm, st.astype(BF16)) * eq_x[:, gs]
        for r in range(hpg):
            hd = g * hpg + r
            diff = cq[:, hd:hd + 1] - cq_t[hd:hd + 1, :]
            w = cb * jnp.exp(jnp.where(mask > 0.0, diff, -jnp.inf))
            ps = slice(hd * MB_HEAD_DIM, (hd + 1) * MB_HEAD_DIM)
            outs.append(_dot(w.astype(BF16), xs[:, ps].astype(BF16))
                        + y_inter[:, r * MB_HEAD_DIM:(r + 1) * MB_HEAD_DIM])
        st_ref[g] = st * etot_x[:, gs] + _dot_tn(bm, (xs[:, gs] * ek_x[:, gs]).astype(BF16))
    o_ref[rows, :] = jnp.concatenate(outs, axis=-1).astype(o_ref.dtype)


def _ssd_scan(xbc, proj, dt_col, dt_bias, a_log, n_ctx):
    bsz, t, _ = xbc.shape
    c = SSD_CHUNK
    pad = lambda v: jnp.zeros((1, LANES), F32).at[0, :MB_HEADS].set(v)
    hexp = np.zeros((LANES, MB_INNER), np.float32)
    for hd in range(MB_HEADS):
        hexp[hd, hd * MB_HEAD_DIM:(hd + 1) * MB_HEAD_DIM] = 1.0
    hexp = jnp.asarray(np.concatenate([hexp, hexp, hexp], axis=0), BF16)
    tri = np.tril(np.ones((c, c), np.float32))
    in_specs, args, outs = [], [], []
    for d in range(2):
        n_chunks, col, const, sel, _ = _scan_specs(SSD_BLOCK, n_ctx, t, d == 1, c)
        mq, mk = sel[0:c], sel[c:2 * c]
        mask = jnp.asarray(tri[::-1, ::-1].copy() if d == 1 else tri)
        bias, alog = pad(dt_bias[d]), pad(a_log[d].astype(F32))
        in_specs += [col(xbc.shape[-1], 0), col(LANES, dt_col + d), const(bias), const(alog),
                     const(mq), const(mk), const(mask)]
        args += [xbc, proj, bias, alog, mq, mk, mask]
        outs.append(col(MB_INNER, 0))
    return pl.pallas_call(
        _ssd_kernel,
        grid=(bsz, n_chunks),
        in_specs=in_specs + [const(hexp)],
        out_specs=outs,
        out_shape=[jax.ShapeDtypeStruct((bsz, t, MB_INNER), BF16)] * 2,
        scratch_shapes=[pltpu.VMEM((MB_GROUPS, MB_STATE, MB_INNER // MB_GROUPS), F32)] * 2,
        compiler_params=_params("parallel", "arbitrary"),
        name="ssd_scan",
    )(*args, hexp)


def _mix_out_cd_kernel(*refs):
    h_ref, hy_ref, yf_ref, yb_ref, xs_ref, z_ref, dsk_ref, ng_ref, w_ref, gt_ref = refs[0:10]
    pre_in, o_ref, pre_out = refs[10:10 + N_FFN_PRE_IN], refs[10 + N_FFN_PRE_IN], refs[11 + N_FFN_PRE_IN:]
    y = (yf_ref[...].astype(F32) + yb_ref[...].astype(F32) + dsk_ref[...] * xs_ref[...]) * _silu(z_ref[...])
    gw = MB_INNER // MB_GROUPS
    ys = [_rms(y[:, g * gw:(g + 1) * gw], ng_ref[:, g * gw:(g + 1) * gw]) for g in range(MB_GROUPS)]
    feat = jnp.concatenate([hy_ref[...]] + ys, axis=-1).astype(BF16)
    h_new = h_ref[...] + gt_ref[...] * _dot(feat, w_ref[...])
    o_ref[...] = h_new
    _ffn_pre_body(h_new, *pre_in, *pre_out)


def _mix_out_cd(h, hy, y_ssd, xbc, proj, z_col, d_skip_x, norm_g, w_out, gate, n_ctx, pre):
    bsz, t, d = h.shape
    tm = ROW_TILE
    n_lat = t - n_ctx
    off = n_ctx // tm
    row = lambda width, idx: pl.BlockSpec((None, tm, width), lambda b, i: (b, i + off, idx))
    vec = pl.BlockSpec((1, MB_INNER), lambda b, i: (0, 0))
    latent = lambda b, i: (b, 1, 0, 0)
    p_in, p_args, p_out, p_shape, p_scratch = _ffn_pre_parts(bsz, n_lat, d, *pre, latent)
    outs = pl.pallas_call(
        _mix_out_cd_kernel,
        grid=(bsz, n_lat // tm),
        in_specs=[row(d, 0), pl.BlockSpec((None, tm, HY_CH), lambda b, i: (b, i, 0)),
                  row(MB_INNER, 0), row(MB_INNER, 0), row(MB_INNER, 0), row(MB_INNER, z_col), vec, vec,
                  pl.BlockSpec(w_out.shape, lambda b, i: (0, 0)),
                  pl.BlockSpec((None, None, 1, d), latent)] + p_in,
        out_specs=[pl.BlockSpec((None, tm, d), lambda b, i: (b, i, 0))] + p_out,
        out_shape=[jax.ShapeDtypeStruct((bsz, n_lat, d), F32)] + p_shape,
        scratch_shapes=p_scratch,
        compiler_params=_params("arbitrary", "arbitrary"),
        name="mix_out_cd",
    )(h, hy, y_ssd[0], y_ssd[1], xbc, proj, d_skip_x, norm_g.reshape(1, -1), w_out, gate, *p_args)
    return outs[0], outs[1:]


def _top2_of4(a, b, c, d):
    hi1, lo1, hi2, lo2 = jnp.maximum(a, b), jnp.minimum(a, b), jnp.maximum(c, d), jnp.minimum(c, d)
    return jnp.maximum(hi1, hi2) + jnp.maximum(jnp.minimum(hi1, hi2), jnp.maximum(lo1, lo2))


def _first_argmax(vals, skip=None):
    idx = None
    for j, vj in enumerate(vals):
        if idx is None and skip is None:
            idx, best = jnp.zeros(vj.shape, jnp.int32), vj
            continue
        if idx is None:
            idx, best = jnp.full(vj.shape, -1, jnp.int32), jnp.full(vj.shape, -jnp.inf, F32)
        take = vj > best
        if skip is not None:
            take = jnp.logical_and(take, skip != j)
        idx = jnp.where(take, j, idx)
        best = jnp.where(take, vj, best)
    return idx, best


def _ffn_pre_body(h, g_ref, sh_ref, sc_ref, rw_ref, rb_ref, tri_ref, v_ref, ri_ref, rwt_ref, cnt_ref, carry_ref):
    @pl.when(jnp.logical_and(pl.program_id(0) == 0, pl.program_id(1) == 0))
    def _():
        carry_ref[...] = jnp.zeros_like(carry_ref)

    v = _rms(h, g_ref[...]) * (1.0 + sc_ref[...]) + sh_ref[...]
    v_ref[...] = _pack_bf16_pairs(v)
    st = _sigmoid(_dot_f32(v, rw_ref[...])).T[0:N_EXPERTS]
    sel = st + rb_ref[...]
    row = lambda a, e: a[e:e + 1]
    epg = EXPERTS_PER_GROUP
    gscore = [_top2_of4(*[row(sel, g * epg + j) for j in range(epg)]) for g in range(N_GROUPS)]
    best, _ = _first_argmax(gscore)

    def in_best(a, j):
        out = row(a, j)
        for g in range(1, N_GROUPS):
            out = jnp.where(best == g, row(a, g * epg + j), out)
        return out

    vals = [in_best(sel, j) for j in range(epg)]
    raw = [in_best(st, j) for j in range(epg)]
    i1, _ = _first_argmax(vals)
    i2, _ = _first_argmax(vals, skip=i1)
    pick = lambda i: functools.reduce(lambda acc, j: jnp.where(i == j, raw[j], acc), range(1, epg), raw[0])
    w1, w2 = pick(i1), pick(i2)
    wsum = w1 + w2
    e1, e2 = best * epg + i1, best * epg + i2

    experts = lax.broadcasted_iota(jnp.int32, st.shape, 0)
    oh1 = (experts == e1).astype(F32)
    oh2 = (experts == e2).astype(F32)
    cnt = oh1 + oh2
    before = _dot(cnt.astype(BF16), tri_ref[...]) + carry_ref[:, 0:1]
    ri_ref[0:1, :] = e1
    ri_ref[1:2, :] = e2
    ri_ref[2:3, :] = jnp.sum(oh1 * before, axis=0, keepdims=True).astype(jnp.int32)
    ri_ref[3:4, :] = jnp.sum(oh2 * before, axis=0, keepdims=True).astype(jnp.int32)
    ri_ref[4:8, :] = jnp.zeros((4, st.shape[1]), jnp.int32)
    lane_row = lax.broadcasted_iota(jnp.int32, (LANES, st.shape[1]), 0)
    rwt_ref[...] = jnp.where(lane_row == 0, w1 / wsum, jnp.where(lane_row == 1, w2 / wsum, 0.0)).T
    carry_ref[...] = carry_ref[...] + jnp.sum(cnt, axis=1, keepdims=True)
    cnt_ref[...] = carry_ref[...]


N_FFN_PRE_IN = 6


def _ffn_pre_parts(bsz, t, d, g, shift, scale, router_w_pad, router_b, seg):
    tm = ROW_TILE
    tri = jnp.asarray(np.triu(np.ones((tm, tm), np.float32), 1), BF16)
    in_specs = [pl.BlockSpec((1, d), lambda b, i: (0, 0)),
                pl.BlockSpec((None, None, 1, d), seg),
                pl.BlockSpec((None, None, 1, d), seg),
                pl.BlockSpec((d, LANES), lambda b, i: (0, 0)),
                pl.BlockSpec((N_EXPERTS, 1), lambda b, i: (0, 0)),
                pl.BlockSpec((tm, tm), lambda b, i: (0, 0))]
    args = (g.reshape(1, d), shift, scale, router_w_pad, router_b.reshape(N_EXPERTS, 1), tri)
    out_specs = [pl.BlockSpec((None, tm, d // 2), lambda b, i: (b, i, 0)),
                 pl.BlockSpec((None, 8, tm), lambda b, i: (b, 0, i)),
                 pl.BlockSpec((None, tm, LANES), lambda b, i: (b, i, 0)),
                 pl.BlockSpec((N_EXPERTS, LANES), lambda b, i: (0, 0))]
    out_shape = [jax.ShapeDtypeStruct((bsz, t, d // 2), jnp.uint32),
                 jax.ShapeDtypeStruct((bsz, 8, t), jnp.int32),
                 jax.ShapeDtypeStruct((bsz, t, LANES), F32),
                 jax.ShapeDtypeStruct((N_EXPERTS, LANES), F32)]
    return in_specs, args, out_specs, out_shape, [pltpu.VMEM((N_EXPERTS, LANES), F32)]


def _experts_kernel(be_ref, nb_ref, x_ref, wg_ref, wu_ref, wd_ref, o_ref, wg_s, wu_s, wd_s):
    i = pl.program_id(0)
    prev = be_ref[jnp.maximum(i - 1, 0)]
    changed = jnp.logical_or(i == 0, be_ref[i] != prev)

    @pl.when(changed)
    def _():
        wg_s[...] = wg_ref[...].astype(BF16)
        wu_s[...] = wu_ref[...].astype(BF16)
        wd_s[...] = wd_ref[...].astype(BF16)

    @pl.when(i < nb_ref[0])
    def _():
        x = _unpack_bf16_pairs(x_ref[...]).astype(BF16)
        hid = _silu(_dot(x, wg_s[...])) * _dot(x, wu_s[...])
        o_ref[...] = _pack_bf16_pairs(_dot(hid.astype(BF16), wd_s[...]))

    @pl.when(i >= nb_ref[0])
    def _():
        o_ref[...] = jnp.zeros_like(o_ref)


def _experts(xb, block_e, n_used, layer, w_gate, w_up, w_down):
    n_slots = xb.shape[0]
    n_blocks = n_slots // MOE_BLOCK
    d, de = w_gate.shape[-2:]
    wspec = lambda shape: pl.BlockSpec((None, None) + shape, lambda i, be, nb: (layer, be[i], 0, 0))
    return pl.pallas_call(
        _experts_kernel,
        grid_spec=pltpu.PrefetchScalarGridSpec(
            num_scalar_prefetch=2,
            grid=(n_blocks,),
            in_specs=[pl.BlockSpec((MOE_BLOCK, d // 2), lambda i, be, nb: (i, 0)),
                      wspec((d, de)), wspec((d, de)), wspec((de, d))],
            out_specs=pl.BlockSpec((MOE_BLOCK, d // 2), lambda i, be, nb: (i, 0)),
            scratch_shapes=[pltpu.VMEM((d, de), BF16), pltpu.VMEM((d, de), BF16), pltpu.VMEM((de, d), BF16)]),
        out_shape=jax.ShapeDtypeStruct((n_slots, d // 2), jnp.uint32),
        compiler_params=_params("arbitrary"),
        name="moe_experts",
    )(block_e, n_used, xb, w_gate, w_up, w_down)


def _ffn_post_kernel(h_ref, y0_ref, y1_ref, w_ref, gt_ref, g_ref, o_ref, *, final):
    w = w_ref[...]
    y = w[:, 0:1] * _unpack_bf16_pairs(y0_ref[...]) + w[:, 1:2] * _unpack_bf16_pairs(y1_ref[...])
    out = h_ref[...] + gt_ref[...] * y
    o_ref[...] = _rms(out, g_ref[...]) if final else out


def _ffn_post(h, y, w, gate, n_ctx, final_g=None):
    bsz, t, d = h.shape
    tm = ROW_TILE
    seg = lambda b, i: (b, (i * tm >= n_ctx).astype(jnp.int32), 0, 0)
    row = lambda width: pl.BlockSpec((None, tm, width), lambda b, i: (b, i, 0))
    choice = lambda kk: pl.BlockSpec((None, None, tm, d // 2), lambda b, i: (kk, b, i, 0))
    final = final_g is not None
    g = final_g if final else jnp.ones((d,), F32)
    return pl.pallas_call(
        functools.partial(_ffn_post_kernel, final=final),
        grid=(bsz, t // tm),
        in_specs=[row(d), choice(0), choice(1), row(LANES), pl.BlockSpec((None, None, 1, d), seg),
                  pl.BlockSpec((1, d), lambda b, i: (0, 0))],
        out_specs=row(d),
        out_shape=jax.ShapeDtypeStruct((bsz, t, d), F32),
        compiler_params=_params("parallel", "parallel"),
        name="ffn_post",
    )(h, y, y, w, gate, g.reshape(1, d))


def _slot_layout(n, ri, counts):
    e = jnp.swapaxes(ri[:, 0:2], 0, 1).reshape(TOP_K, n)
    rank = jnp.swapaxes(ri[:, 2:4], 0, 1).reshape(TOP_K, n)
    padded = (counts + MOE_BLOCK - 1) // MOE_BLOCK * MOE_BLOCK
    pend = jnp.cumsum(padded)
    pstart = pend - padded
    experts = jnp.arange(N_EXPERTS, dtype=jnp.int32)
    dest = rank + jnp.sum(jnp.where(e[..., None] == experts, pstart, 0), axis=-1)
    n_slots = (n * TOP_K + MOE_BLOCK - 1) // MOE_BLOCK * MOE_BLOCK + N_EXPERTS * MOE_BLOCK
    n_blocks = n_slots // MOE_BLOCK
    blk0 = jnp.arange(n_blocks, dtype=jnp.int32)[:, None] * MOE_BLOCK
    block_e = jnp.minimum(jnp.sum((pend[None, :] <= blk0).astype(jnp.int32), axis=-1), N_EXPERTS - 1)
    n_used = (pend[-1] // MOE_BLOCK).astype(jnp.int32).reshape(1)
    return dest, n_slots, block_e.astype(jnp.int32), n_used


SC_CORES, SC_SUBCORES = 2, 16
SC_GATHER_WINDOW = 64
SC_SCATTER_WINDOW = 32


def _sc_window(per_worker, largest):
    win = largest
    while per_worker % win:
        win //= 2
    assert win >= 8
    return win


def _gather_rows(table, idx):
    n_rows, d = idx.shape[0], table.shape[1]
    workers = SC_CORES * SC_SUBCORES
    per_worker = n_rows // workers
    assert per_worker * workers == n_rows
    win = _sc_window(per_worker, SC_GATHER_WINDOW)
    mesh = plsc.VectorSubcoreMesh(core_axis_name="c", subcore_axis_name="s")

    @functools.partial(
        pl.kernel, mesh=mesh,
        out_type=jax.ShapeDtypeStruct((n_rows, d), table.dtype),
        scratch_types=[pltpu.VMEM((win,), jnp.int32), pltpu.VMEM((win,), jnp.int32),
                       pltpu.VMEM((win, d), table.dtype), pltpu.VMEM((win, d), table.dtype),
                       pltpu.SemaphoreType.DMA, pltpu.SemaphoreType.DMA],
    )
    def gather_kernel(table_hbm, idx_hbm, out_hbm, idx0, idx1, rows0, rows1, sem0, sem1):
        base = (lax.axis_index("s") * SC_CORES + lax.axis_index("c")) * per_worker
        n_win = per_worker // win
        slots = ((idx0, rows0, sem0), (idx1, rows1, sem1))
        window = lambda j: pl.ds(pl.multiple_of(base + j * win, 8), win)

        def start(j, slot):
            idx_v, rows_v, sem = slots[slot]
            pltpu.sync_copy(idx_hbm.at[window(j)], idx_v)
            pltpu.async_copy(table_hbm.at[idx_v], rows_v, sem)

        def finish(j, slot):
            idx_v, rows_v, sem = slots[slot]
            pltpu.make_async_copy(table_hbm.at[idx_v], rows_v, sem).wait()
            pltpu.sync_copy(rows_v, out_hbm.at[window(j)])

        start(0, 0)

        @pl.loop(0, n_win, step=2)
        def _(j):
            @pl.when(j + 1 < n_win)
            def _():
                start(j + 1, 1)

            finish(j, 0)

            @pl.when(j + 2 < n_win)
            def _():
                start(j + 2, 0)

            @pl.when(j + 1 < n_win)
            def _():
                finish(j + 1, 1)

    return gather_kernel(table, idx)


def _scatter_rows(src, dest, n_slots):
    n, d = src.shape
    workers = SC_CORES * SC_SUBCORES
    per_worker = n // workers
    assert per_worker * workers == n and dest.shape == (TOP_K, n)
    win = _sc_window(per_worker, SC_SCATTER_WINDOW)
    mesh = plsc.VectorSubcoreMesh(core_axis_name="c", subcore_axis_name="s")

    @functools.partial(
        pl.kernel, mesh=mesh,
        out_type=jax.ShapeDtypeStruct((n_slots, d), src.dtype),
        scratch_types=[pltpu.VMEM((win,), jnp.int32)] * 4 + [pltpu.VMEM((win, d), src.dtype)] * 2
                      + [pltpu.SemaphoreType.DMA] * 6,
    )
    def scatter_kernel(src_hbm, dest_hbm, out_hbm, i0a, i1a, i0b, i1b, rows_a, rows_b,
                       load_a, first_a, second_a, load_b, first_b, second_b):
        base = (lax.axis_index("s") * SC_CORES + lax.axis_index("c")) * per_worker
        n_win = per_worker // win
        slots = ((i0a, i1a, rows_a, load_a, first_a, second_a), (i0b, i1b, rows_b, load_b, first_b, second_b))
        window = lambda j: pl.ds(pl.multiple_of(base + j * win, 8), win)

        def load(j, slot):
            idx0, idx1, rows_v, sem, _, _ = slots[slot]
            pltpu.async_copy(src_hbm.at[window(j)], rows_v, sem)
            pltpu.sync_copy(dest_hbm.at[0, window(j)], idx0)
            pltpu.sync_copy(dest_hbm.at[1, window(j)], idx1)

        def scatter(j, slot):
            idx0, idx1, rows_v, sem, sem0, sem1 = slots[slot]
            pltpu.make_async_copy(src_hbm.at[window(j)], rows_v, sem).wait()
            pltpu.async_copy(rows_v, out_hbm.at[idx0], sem0)
            pltpu.async_copy(rows_v, out_hbm.at[idx1], sem1)

        def drain(slot):
            idx0, idx1, rows_v, _, sem0, sem1 = slots[slot]
            pltpu.make_async_copy(rows_v, out_hbm.at[idx0], sem0).wait()
            pltpu.make_async_copy(rows_v, out_hbm.at[idx1], sem1).wait()

        load(0, 0)

        @pl.loop(0, n_win, step=2)
        def _(j):
            scatter(j, 0)

            @pl.when(j + 1 < n_win)
            def _():
                load(j + 1, 1)

            drain(0)

            @pl.when(j + 1 < n_win)
            def _():
                scatter(j + 1, 1)

            @pl.when(j + 2 < n_win)
            def _():
                load(j + 2, 0)

            @pl.when(j + 1 < n_win)
            def _():
                drain(1)

    return scatter_kernel(src, dest)


def _alongside(gather, idx, side_fn, side_in):
    idx, side_in = lax.optimization_barrier((idx, side_in))
    return lax.optimization_barrier((gather(idx), side_fn(side_in)))


def _moe(h, pre_out, gate, layer, w_gate, w_up, w_down, n_ctx, final_g=None, side=None):
    bsz, t, d = h.shape
    n = bsz * t
    v, ri, rwt, counts = pre_out
    dest, n_slots, block_e, n_used = _slot_layout(n, ri, counts[:, 0].astype(jnp.int32))
    dispatch = lambda idx: _scatter_rows(v.reshape(n, d // 2), idx, n_slots)
    if side is None:
        xb = dispatch(dest)
    else:
        xb, side_a = _alongside(dispatch, dest, *side[0])
    yb = _experts(xb, block_e, n_used, layer, w_gate, w_up, w_down)
    combine = lambda idx: _gather_rows(yb, idx)
    dest_flat = dest.reshape(-1)
    if side is None:
        y, side_b = combine(dest_flat), None
    else:
        y, side_b = _alongside(combine, dest_flat, side[1], side_a)
    out = _ffn_post(h, y.reshape(TOP_K, bsz, t, d // 2), rwt, gate, n_ctx, final_g)
    return out if side is None else (out, side_b)


DFT_STEP = 16


def _dft_tables(n):
    r, *mats = _dft_tables_np(n)
    return (r,) + tuple(jnp.asarray(a).astype(BF16) for a in mats)


@functools.lru_cache(maxsize=None)
def _dft_tables_np(n):
    size = 2 * n
    r = int(round(math.sqrt(size)))
    assert r * r == size and r % DFT_STEP == 0
    p1 = np.arange(r // 2)[None, None, :]
    p2 = np.arange(r)[:, None, None]
    k1 = np.arange(r)[None, :, None]
    ang = 2.0 * np.pi * (((r * p1 + p2) * k1) % size) / size
    g_re, g_im = np.cos(ang), -np.sin(ang)
    g_in = np.concatenate([g_re, g_im], axis=1)
    g_out = np.concatenate([np.swapaxes(g_re, 1, 2), np.swapaxes(g_im, 1, 2)], axis=2) / size
    a2 = 2.0 * np.pi * ((np.arange(r)[:, None] * np.arange(r)[None, :]) % r) / r
    f_re, f_im = np.cos(a2), -np.sin(a2)
    f_fwd = np.block([[f_re, -f_im], [f_im, f_re]])
    f_inv = np.block([[f_re, f_im], [-f_im, f_re]])
    p1f = np.arange(r)[None, None, :]
    angf = 2.0 * np.pi * (((r * p1f + p2) * k1) % size) / size
    g_full = np.concatenate([np.cos(angf), -np.sin(angf)], axis=1)
    return (r,) + tuple(a.astype(np.float32) for a in (g_in, g_out, f_fwd, f_inv, g_full))


def _dot_f32_tn(a, b):
    ah = a.astype(BF16)
    al = (a - ah.astype(F32)).astype(BF16)
    bh = b.astype(BF16)
    bl = (b - bh.astype(F32)).astype(BF16)
    return _dot_tn(ah, bh) + (_dot_tn(ah, bl) + _dot_tn(al, bh))


def _hy_filter_kernel(z_ref, t_ref, w1_ref, b1_ref, w2_ref, b2_ref, fr_ref, w3_ref, w3b_ref, rates_ref, o_ref,
                      *, half_tiles):
    i = pl.program_id(0)
    hid = jnp.sin(fr_ref[...] * (_dot_f32(w1_ref[...], z_ref[...]) + b1_ref[...]))
    hid = jnp.sin(fr_ref[...] * (_dot_f32(w2_ref[...], hid) + b2_ref[...]))
    filt = _dot_f32_tn(hid, w3_ref[...])
    decay = jnp.exp(-t_ref[...] * rates_ref[...])
    for o in range(o_ref.shape[0]):
        o_ref[o] = filt[:, o * HY_CH:(o + 1) * HY_CH] * decay

    @pl.when(i == 0)
    def _():
        extra = _dot_f32_tn(hid[:, 0:LANES], w3b_ref[...])[0:8]
        first = lax.broadcasted_iota(jnp.int32, (8, HY_CH), 0) == 0
        for o in range(o_ref.shape[0]):
            add = extra[:, o * HY_CH:(o + 1) * HY_CH] * decay[0:8]
            o_ref[o, 0:8, :] = o_ref[o, 0:8, :] + jnp.where(first, add, 0.0)

    @pl.when(i == half_tiles)
    def _():
        for o in range(o_ref.shape[0]):
            o_ref[o, 0:1, :] = jnp.zeros((1, HY_CH), F32)


HY_TILE = 512


def _hy_kernels(n, w1, b1, w2, b2, w3, freq):
    pos = np.arange(2 * n)
    pos = np.where(pos < n, pos, 2 * n - pos).astype(np.float32)
    t = jnp.asarray(pos / np.float32(n - 1))
    bands = jnp.linspace(1e-4, HY_BANDS - 1, HY_BANDS, dtype=F32)
    ang = (2.0 * math.pi / n) * bands[:, None] * jnp.asarray(pos)[None, :]
    z = jnp.concatenate([t[None, :], jnp.cos(ang), -jnp.sin(ang)], axis=0)
    z = jnp.pad(z, ((0, LANES - z.shape[0]), (0, 0)))
    w1t = jnp.pad(w1, ((0, LANES - w1.shape[0]), (0, 0))).T
    hidden = w1.shape[1]
    col = lambda v: v.reshape(hidden, 1)
    w3d = jnp.swapaxes(w3.reshape(hidden, HY_ORDER, 2, HY_CH), 0, 2)
    w3d = jnp.swapaxes(w3d, 1, 2).reshape(2, hidden, HY_ORDER * HY_CH)
    rates = jnp.abs(jnp.linspace(HY_MIN_DECAY, HY_MAX_DECAY, HY_CH, dtype=F32)).reshape(1, HY_CH)
    tm = HY_TILE
    half_tiles = n // tm
    full = lambda a: pl.BlockSpec(a.shape, lambda i: (0,) * a.ndim)
    small = (w1t, col(b1), w2.T, col(b2), col(freq))
    return pl.pallas_call(
        functools.partial(_hy_filter_kernel, half_tiles=half_tiles),
        grid=(2 * n // tm,),
        in_specs=[pl.BlockSpec((LANES, tm), lambda i: (0, i)), pl.BlockSpec((tm, 1), lambda i: (i, 0))]
                 + [full(a) for a in small]
                 + [pl.BlockSpec((None, hidden, HY_ORDER * HY_CH), lambda i: ((i >= half_tiles).astype(jnp.int32), 0, 0)),
                    pl.BlockSpec((None, hidden, HY_ORDER * HY_CH), lambda i: (1, 0, 0)), full(rates)],
        out_specs=pl.BlockSpec((HY_ORDER, tm, HY_CH), lambda i: (0, i, 0)),
        out_shape=jax.ShapeDtypeStruct((HY_ORDER, 2 * n, HY_CH), F32),
        compiler_params=_params("parallel"),
        name="hy_kernels",
    )(z, t[:, None], *small, w3d, w3d, rates)


def _pack_complex(z):
    r = z.shape[0] // 2
    bits = lax.bitcast_convert_type(z.astype(BF16).astype(F32), jnp.uint32)
    return lax.bitcast_convert_type(bits[0:r] | (bits[r:2 * r] >> 16), F32)


def _unpack_complex(words):
    p = lax.bitcast_convert_type(words, jnp.uint32)
    re = lax.bitcast_convert_type(p & jnp.uint32(0xFFFF0000), F32)
    im = lax.bitcast_convert_type(p << 16, F32)
    return jnp.concatenate([re, im], axis=0).astype(BF16)


def _load_every(ref, j, count):
    return ref.reshape(count * DFT_STEP, LANES)[pl.ds(j, count, stride=DFT_STEP), :]


def _store_every(ref, j, count, val):
    ref.reshape(count * DFT_STEP, LANES)[pl.ds(j, count, stride=DFT_STEP), :] = val


def _dft_in_kernel(x_ref, g_ref, a_ref):
    n_seq, rh = x_ref.shape[0:2]
    for j in range(DFT_STEP):
        x = jnp.concatenate([_load_every(x_ref.at[b], j, rh) for b in range(n_seq)], axis=1).astype(BF16)
        res = _dot(g_ref[j], x)
        for b in range(n_seq):
            a_ref[b, j] = _pack_complex(res[:, b * LANES:(b + 1) * LANES])


def _dft_in(x4, col, g_in):
    bx, rh, r, _ = x4.shape
    c = HY_CH
    cbs = c // LANES
    return pl.pallas_call(
        _dft_in_kernel,
        grid=(r // DFT_STEP, cbs),
        in_specs=[pl.BlockSpec((bx, rh, DFT_STEP, LANES), lambda i, cb: (0, 0, i, col * cbs + cb)),
                  pl.BlockSpec((DFT_STEP, 2 * r, rh), lambda i, cb: (i, 0, 0))],
        out_specs=pl.BlockSpec((bx, DFT_STEP, r, LANES), lambda i, cb: (0, i, 0, cb)),
        out_shape=jax.ShapeDtypeStruct((bx, r, r, c), F32),
        compiler_params=_params("parallel", "parallel"),
        name="dft_in",
    )(x4, g_in)


def _stage2_operand(a_ref, j, r):
    return jnp.concatenate([_unpack_complex(_load_every(a_ref.at[b], j, r)) for b in range(a_ref.shape[0])],
                           axis=1)


def _dft_filt_kernel(a_ref, f_ref, k_ref):
    r = f_ref.shape[0] // 2
    for j in range(DFT_STEP):
        s = _dot(f_ref[...], _stage2_operand(a_ref, j, r))
        for o in range(a_ref.shape[0]):
            k_ref[o, j] = s[:, o * LANES:(o + 1) * LANES]


def _dft_filt(a, f_fwd):
    nq, r, _, c = a.shape
    return pl.pallas_call(
        _dft_filt_kernel,
        grid=(r // DFT_STEP, c // LANES),
        in_specs=[pl.BlockSpec((nq, r, DFT_STEP, LANES), lambda i, cb: (0, 0, i, cb)),
                  pl.BlockSpec(f_fwd.shape, lambda i, cb: (0, 0))],
        out_specs=pl.BlockSpec((nq, DFT_STEP, 2 * r, LANES), lambda i, cb: (0, i, 0, cb)),
        out_shape=jax.ShapeDtypeStruct((nq, r, 2 * r, c), F32),
        compiler_params=_params("parallel", "parallel"),
        name="dft_filt",
    )(a, f_fwd)


def _dft_mid_kernel(a_ref, k_ref, ff_ref, fi_ref, b_ref):
    r = ff_ref.shape[0] // 2
    n_seq = a_ref.shape[0]
    for j in range(DFT_STEP):
        s = _dot(ff_ref[...], _stage2_operand(a_ref, j, r))
        sr, si = s[0:r], s[r:2 * r]
        kr = jnp.concatenate([k_ref[j, 0:r, :]] * n_seq, axis=1)
        ki = jnp.concatenate([k_ref[j, r:2 * r, :]] * n_seq, axis=1)
        p = jnp.concatenate([sr * kr - si * ki, sr * ki + si * kr], axis=0).astype(BF16)
        back = _dot(fi_ref[...], p)
        for b in range(n_seq):
            b_ref[b, j] = _pack_complex(back[:, b * LANES:(b + 1) * LANES])


def _dft_mid(a, kspec, order, f_fwd, f_inv):
    bsz, r, _, c = a.shape
    return pl.pallas_call(
        _dft_mid_kernel,
        grid=(r // DFT_STEP, c // LANES),
        in_specs=[pl.BlockSpec((bsz, r, DFT_STEP, LANES), lambda i, cb: (0, 0, i, cb)),
                  pl.BlockSpec((None, DFT_STEP, 2 * r, LANES), lambda i, cb: (order, i, 0, cb)),
                  pl.BlockSpec(f_fwd.shape, lambda i, cb: (0, 0)),
                  pl.BlockSpec(f_inv.shape, lambda i, cb: (0, 0))],
        out_specs=pl.BlockSpec((bsz, DFT_STEP, r, LANES), lambda i, cb: (0, i, 0, cb)),
        out_shape=jax.ShapeDtypeStruct((bsz, r, r, c), F32),
        compiler_params=_params("parallel", "parallel"),
        name="dft_mid",
    )(a, kspec, f_fwd, f_inv)


def _dft_out_kernel(b_ref, g_ref, u_ref, x_ref, bias_ref, o_ref):
    n_seq, r = b_ref.shape[0:2]
    rh = o_ref.shape[1]
    for j in range(DFT_STEP):
        rhs = jnp.concatenate([_unpack_complex(_load_every(b_ref.at[b], j, r)) for b in range(n_seq)], axis=1)
        y = _dot(g_ref[j], rhs)
        for b in range(n_seq):
            yb = y[:, b * LANES:(b + 1) * LANES]
            _store_every(o_ref.at[b], j, rh,
                         _load_every(x_ref.at[b], j, rh) * (yb + _load_every(u_ref.at[b], j, rh) * bias_ref[...]))


def _dft_out(bm, g_out, u4, u_col, x4, x_col, bias):
    bsz, r, _, c = bm.shape
    rh = r // 2
    cbs = c // LANES
    seq = lambda col: pl.BlockSpec((bsz, rh, DFT_STEP, LANES), lambda i, cb: (0, 0, i, col * cbs + cb))
    return pl.pallas_call(
        _dft_out_kernel,
        grid=(r // DFT_STEP, cbs),
        in_specs=[pl.BlockSpec((bsz, r, DFT_STEP, LANES), lambda i, cb: (0, 0, i, cb)),
                  pl.BlockSpec((DFT_STEP, rh, 2 * r), lambda i, cb: (i, 0, 0)),
                  seq(u_col), seq(x_col), pl.BlockSpec((1, LANES), lambda i, cb: (0, cb))],
        out_specs=seq(0),
        out_shape=jax.ShapeDtypeStruct((bsz, rh, r, c), F32),
        compiler_params=_params("parallel", "parallel"),
        name="dft_out",
    )(bm, g_out, u4, x4, bias.reshape(1, c))


def _hyena_filter_stage1(n, filter_params):
    r, g_full = _dft_tables(n)[0], _dft_tables(n)[5]
    kern = _hy_kernels(n, *filter_params)
    return _dft_in(kern.reshape(-1, r, r, HY_CH), 0, g_full)


def _hyena_filter_spectra(n, stage1):
    return _dft_filt(stage1, _dft_tables(n)[3])


def _hyena(hy_in, kspec, conv_bias):
    bsz, n, _ = hy_in.shape
    r, g_in, g_out, f_fwd, f_inv, _ = _dft_tables(n)
    seq4 = hy_in.reshape(bsz, r // 2, r, 3 * HY_CH)
    zz = _dft_out(_dft_mid(_dft_in(seq4, 0, g_in), kspec, 0, f_fwd, f_inv), g_out,
                  seq4, 0, seq4, 1, conv_bias[0])
    out = _dft_out(_dft_mid(_dft_in(zz, 0, g_in), kspec, 1, f_fwd, f_inv), g_out,
                   zz, 0, seq4, 2, conv_bias[1])
    return out.reshape(bsz, n, HY_CH)


def _reorder_ab(w):
    offs = np.cumsum([0, GLA_KEY_W, GLA_KEY_W, GLA_VAL_W, GLA_VAL_W, GLA_LOW_RANK, GLA_LOW_RANK,
                      HG_KEY_W, HG_KEY_W, HG_KEY_W, HG_VAL_W, HG_VAL_W]).tolist()
    gq, gk, gv, gg, lr_f, lr_b, hq, hf_f, hf_b, hi, hg = range(11)
    cols = lambda first, last: w[:, offs[first]:offs[last + 1]].astype(BF16)
    lr = jnp.pad(cols(lr_f, lr_b), ((0, 0), (0, AB_PAD_COLS - AB_LR - 2 * GLA_LOW_RANK)))
    return [cols(gq, gk), cols(gg, gg), cols(hq, hf_b), cols(hg, hg), lr], [cols(gv, gv), cols(hi, hi)]


def _reorder_cd(w):
    hy_end = 3 * HY_CH
    z_end = hy_end + MB_INNER
    xbc_end = z_end + MB_INNER + 2 * MB_BC_W
    pad = lambda a: jnp.pad(a, ((0, 0), (0, LANES - MB_HEADS)))
    dt = jnp.concatenate([pad(w[:, xbc_end:xbc_end + MB_HEADS]), pad(w[:, xbc_end + MB_HEADS:])], axis=-1)
    return [w[:, :hy_end].astype(BF16), w[:, z_end:xbc_end].astype(BF16), w[:, hy_end:z_end].astype(BF16),
            dt.astype(BF16)]


def kernel(x, c, ctx, c_ctx, ada_w, ada_b, norm_mix_g, norm_ffn_g, norm_out_g, ab_w_in, ab_w_out, gla_gate_w, gla_gate_b, gla_norm_g, hg_lb, hg_norm_g, cd_w_in, cd_w_out, hy_short_w, hy_short_b, hy_w1, hy_b1, hy_w2, hy_b2, hy_w3, hy_freq, hy_bias, mb_conv_w, mb_conv_b, mb_dt_bias, mb_a_log, mb_d, mb_norm_g, router_w, router_b, moe_w_gate, moe_w_up, moe_w_down):
    bsz, n_lat, d = x.shape
    n_ctx = ctx.shape[1]
    assert ada_w.shape[0] == 2 and ab_w_in.shape[0] == 1 and cd_w_in.shape[0] == 1

    cond = jnp.zeros((ADALN_ROWS, d), F32).at[:bsz].set(c).at[bsz].set(c_ctx)
    m = _adaln(cond, ada_w, ada_b)

    def mods(layer):
        lat = m[layer, :bsz].reshape(bsz, 6, d)
        cx = jnp.broadcast_to(m[layer, bsz].reshape(1, 6, d), (bsz, 6, d))
        both = jnp.stack([cx, lat], axis=1)
        return [both[:, :, j][:, :, None, :] for j in range(6)]

    router_w_pad = jnp.zeros((d, LANES), F32).at[:, :N_EXPERTS].set(router_w)

    sh_m, sc_m, gt_m, sh_f, sc_f, gt_f = mods(0)
    proj, vals = _norm_proj(ctx, x, norm_mix_g[0], sh_m, sc_m, *_reorder_ab(ab_w_in[0]))
    gwp = [jnp.zeros((LANES, GLA_KEY_W), F32).at[GLA_LOW_RANK * dd:GLA_LOW_RANK * (dd + 1)].set(gla_gate_w[0, dd])
           for dd in range(2)]
    o_gla = _gla_scan(proj, vals, gwp, [gla_gate_b[0, dd].reshape(1, -1) for dd in range(2)], n_ctx)
    o_hg = _hgrn_scan(proj, vals, [hg_lb[dd].astype(F32) for dd in range(2)], 0, n_ctx)
    h, pre_out = _mix_out_ab(ctx, x, o_gla, o_hg, proj, gla_norm_g[0], hg_norm_g[0], ab_w_out[0].astype(BF16),
                             gt_m, (norm_ffn_g[0], sh_f, sc_f, router_w_pad, router_b))
    filter_params = (hy_w1[0], hy_b1[0], hy_w2[0], hy_b2[0], hy_w3[0], hy_freq[0])
    side = ((functools.partial(_hyena_filter_stage1, n_lat), filter_params),
            functools.partial(_hyena_filter_spectra, n_lat))
    h, kspec = _moe(h, pre_out, gt_f, 0, moe_w_gate, moe_w_up, moe_w_down, n_ctx, side=side)

    sh_m, sc_m, gt_m, sh_f, sc_f, gt_f = mods(1)
    conv_w = jnp.concatenate([hy_short_w[0], mb_conv_w[0]], axis=0).T
    conv_b = jnp.concatenate([hy_short_b[0], mb_conv_b[0]]).reshape(1, -1)
    hy_in, xbc, zdt = _norm_proj_conv(h, norm_mix_g[1], sh_m, sc_m, _reorder_cd(cd_w_in[0]),
                                      conv_w, conv_b, 3 * HY_CH, n_ctx)
    hy = _hyena(hy_in, kspec, hy_bias[0])
    y_ssd = _ssd_scan(xbc, zdt, MB_INNER // LANES, mb_dt_bias[0], mb_a_log[0], n_ctx)
    d_skip_x = jnp.repeat(mb_d[0], MB_HEAD_DIM).reshape(1, MB_INNER)
    h, pre_out = _mix_out_cd(h, hy, y_ssd, xbc, zdt, 0, d_skip_x, mb_norm_g[0], cd_w_out[0].astype(BF16), gt_m,
                             n_ctx, (norm_ffn_g[1], sh_f, sc_f, router_w_pad, router_b))
    return _moe(h, pre_out, gt_f, 1, moe_w_gate, moe_w_up, moe_w_down, 0, final_g=norm_out_g)
```

```python
import functools
import math

import numpy as np
import jax
import jax.numpy as jnp
from jax import lax
from jax.experimental import pallas as pl
from jax.experimental.pallas import tpu as pltpu
from jax.experimental.pallas import tpu_sc as plsc

NORM_EPS = 1e-6
GLA_HEADS, GLA_DK, GLA_DV, GLA_LOW_RANK, GLA_TAU = 4, 64, 128, 16, 16.0
GLA_KEY_W, GLA_VAL_W = GLA_HEADS * GLA_DK, GLA_HEADS * GLA_DV
HG_HEADS, HG_EXPAND, HG_DV = 4, 128, 128
HG_KEY_W, HG_VAL_W = HG_HEADS * HG_EXPAND, HG_HEADS * HG_DV
HY_CH, HY_ORDER, HY_BANDS = 512, 2, 16
HY_MIN_DECAY = math.log(1e-2) / 1.5
HY_MAX_DECAY = math.log(1e-2) / 0.3
MB_HEADS, MB_HEAD_DIM, MB_GROUPS, MB_STATE = 8, 64, 2, 128
MB_INNER = MB_HEADS * MB_HEAD_DIM
MB_BC_W = MB_GROUPS * MB_STATE
N_EXPERTS, N_GROUPS, TOP_K = 16, 4, 2
MOE_BLOCK = 512
EXPERTS_PER_GROUP = N_EXPERTS // N_GROUPS

LANES = 128
SCAN_CHUNK = 64
SCAN_BLOCK = 256
SSD_CHUNK = 128
SSD_BLOCK = 256
ROW_TILE = 256
VMEM_LIMIT = 56 * 1024 * 1024

BF16 = jnp.bfloat16
F32 = jnp.float32


def _params(*sem):
    return pltpu.CompilerParams(dimension_semantics=sem, vmem_limit_bytes=VMEM_LIMIT)


def _split3(x):
    hi = x.astype(BF16)
    r1 = x - hi.astype(F32)
    mid = r1.astype(BF16)
    lo = (r1 - mid.astype(F32)).astype(BF16)
    return hi, mid, lo


def _dot(a, b):
    return jnp.dot(a, b, preferred_element_type=F32)


def _dot_nt(a, b):
    return lax.dot_general(a, b, (((1,), (1,)), ((), ())), preferred_element_type=F32)


def _dot_tn(a, b):
    return lax.dot_general(a, b, (((0,), (0,)), ((), ())), preferred_element_type=F32)


def _dot_f32(a, b):
    ah = a.astype(BF16)
    al = (a - ah.astype(F32)).astype(BF16)
    bh = b.astype(BF16)
    bl = (b - bh.astype(F32)).astype(BF16)
    return _dot(ah, bh) + (_dot(ah, bl) + _dot(al, bh))


def _silu(x):
    return x * (1.0 / (1.0 + jnp.exp(-x)))


def _sigmoid(x):
    return 1.0 / (1.0 + jnp.exp(-x))


def _softplus(x):
    return jnp.maximum(x, 0.0) + jnp.log(1.0 + jnp.exp(-jnp.abs(x)))


def _pack_bf16_pairs(x):
    bits = lax.bitcast_convert_type(x.astype(BF16).astype(F32), jnp.uint32)
    half = x.shape[1] // 2
    return bits[:, :half] | (bits[:, half:] >> 16)


def _unpack_bf16_pairs(p):
    hi = lax.bitcast_convert_type(p & jnp.uint32(0xFFFF0000), F32)
    lo = lax.bitcast_convert_type(p << 16, F32)
    return jnp.concatenate([hi, lo], axis=1)


def _rms(x, g):
    return x * lax.rsqrt(jnp.mean(x * x, axis=-1, keepdims=True) + NORM_EPS) * g


def _adaln_kernel(c_ref, w_ref, b_ref, o_ref):
    o_ref[...] = _dot_f32(_silu(c_ref[...]), w_ref[...]) + b_ref[...]


ADALN_ROWS = 8
ADALN_TILE = 1536


def _adaln(cond, w, b):
    n_l, d, n6 = w.shape
    tn = ADALN_TILE
    rows = cond.shape[0]
    return pl.pallas_call(
        _adaln_kernel,
        grid=(n_l, n6 // tn),
        in_specs=[pl.BlockSpec((rows, d), lambda l, j: (0, 0)),
                  pl.BlockSpec((None, d, tn), lambda l, j: (l, 0, j)),
                  pl.BlockSpec((None, 1, tn), lambda l, j: (l, 0, j))],
        out_specs=pl.BlockSpec((None, rows, tn), lambda l, j: (l, 0, j)),
        out_shape=jax.ShapeDtypeStruct((n_l, rows, n6), F32),
        compiler_params=_params("parallel", "parallel"),
        name="adaln",
    )(cond, w, b.reshape(n_l, 1, n6))


def _project(u, w_refs):
    return jnp.concatenate([_dot(u, w_ref[...]) for w_ref in w_refs], axis=-1)


def _joint_rows_specs(ctx, x):
    tm = ROW_TILE
    ctx_tiles = ctx.shape[1] // tm
    d = ctx.shape[2]
    return [pl.BlockSpec((None, tm, d), lambda b, i: (b, jnp.minimum(i, ctx_tiles - 1), 0)),
            pl.BlockSpec((None, tm, d), lambda b, i: (b, jnp.maximum(i - ctx_tiles, 0), 0))], ctx_tiles


def _joint_rows(ctx_ref, x_ref, ctx_tiles):
    return jnp.where(pl.program_id(1) < ctx_tiles, ctx_ref[...], x_ref[...])


def _norm_proj_kernel(c_ref, x_ref, g_ref, sh_ref, sc_ref, *refs, ctx_tiles, n_main):
    u = _rms(_joint_rows(c_ref, x_ref, ctx_tiles), g_ref[...]) * (1.0 + sc_ref[...]) + sh_ref[...]
    u = u.astype(BF16)
    refs[-2][...] = _project(u, refs[:n_main])
    refs[-1][...] = _project(u, refs[n_main:-2]).astype(BF16)


def _norm_proj(ctx, x, g, shift, scale, ws, ws_bf16):
    bsz, n_ctx, d = ctx.shape
    t = n_ctx + x.shape[1]
    n, n_bf = sum(w.shape[1] for w in ws), sum(w.shape[1] for w in ws_bf16)
    tm = ROW_TILE
    seg = lambda b, i: (b, (i * tm >= n_ctx).astype(jnp.int32), 0, 0)
    row_specs, ctx_tiles = _joint_rows_specs(ctx, x)
    return pl.pallas_call(
        functools.partial(_norm_proj_kernel, ctx_tiles=ctx_tiles, n_main=len(ws)),
        grid=(bsz, t // tm),
        in_specs=row_specs + [pl.BlockSpec((1, d), lambda b, i: (0, 0)),
                              pl.BlockSpec((None, None, 1, d), seg),
                              pl.BlockSpec((None, None, 1, d), seg)]
                 + [pl.BlockSpec(w.shape, lambda b, i: (0, 0)) for w in list(ws) + list(ws_bf16)],
        out_specs=[pl.BlockSpec((None, tm, n), lambda b, i: (b, i, 0)),
                   pl.BlockSpec((None, tm, n_bf), lambda b, i: (b, i, 0))],
        out_shape=[jax.ShapeDtypeStruct((bsz, t, n), F32), jax.ShapeDtypeStruct((bsz, t, n_bf), BF16)],
        compiler_params=_params("parallel", "parallel"),
        name="norm_proj",
    )(ctx, x, g.reshape(1, d), shift, scale, *ws, *ws_bf16)


def _scan_constants(c, reverse):
    t = np.arange(c)[:, None]
    u = np.arange(c)[None, :]
    sels = [u <= t, u > t]
    masks = []
    m = c // 2
    while m >= 1:
        blk = t // (2 * m)
        upper_t = (t % (2 * m)) >= m
        r = blk * (2 * m) + m - 1
        s_blk = u // (2 * m)
        upper_s = (u % (2 * m)) >= m
        sels.append((upper_t & (u > r) & (u <= t)) | ((~upper_t) & (u > t) & (u <= r)))
        masks.append((blk == s_blk) & upper_t & (~upper_s))
        m //= 2
    masks.append(t == u)
    sel = np.stack(sels).astype(np.float32)
    msk = np.stack(masks).astype(np.float32)
    if reverse:
        sel = sel[:, ::-1, ::-1]
        msk = msk[:, ::-1, ::-1]
    return np.ascontiguousarray(sel.reshape(-1, c)), np.ascontiguousarray(msk)


def _chunk_order(i, n_ctx_chunks, n_chunks, reverse):
    if not reverse:
        return i
    return jnp.where(i < n_ctx_chunks, n_ctx_chunks - 1 - i, n_chunks - 1 - (i - n_ctx_chunks))


GROUP_KEYS = 256


def _decay_chunk(q, k, v, la, consts, st_ref, heads, dk, dv):
    sel_ref, mask_ref, hm_ref, hmb_ref, vm_ref = consts
    c = q.shape[0]
    n_lvl = mask_ref.shape[0] - 1
    hpg = GROUP_KEYS // dk
    cs = _dot(sel_ref[...], jnp.concatenate(_split3(la), axis=0))
    e_q = jnp.exp(cs[0:c])
    e_k = jnp.exp(cs[c:2 * c])
    e_tot = jnp.exp(jnp.sum(la, axis=0, keepdims=True))
    vb = v.astype(BF16)
    outs = []
    for g in range(heads // hpg):
        ks = slice(g * GROUP_KEYS, (g + 1) * GROUP_KEYS)
        vs = slice(g * hpg * dv, (g + 1) * hpg * dv)
        qg, kg = q[:, ks], k[:, ks]
        key_stack = lambda x: jnp.concatenate([x.astype(BF16) * hmb_ref[h] for h in range(hpg)], axis=0)
        att = mask_ref[n_lvl] * _dot_nt(qg.astype(BF16), key_stack(kg))
        for l in range(n_lvl):
            e = jnp.exp(cs[(2 + l) * c:(3 + l) * c, ks])
            att = att + mask_ref[l] * _dot_nt((qg * e).astype(BF16), key_stack(kg * e))
        v_blocks = jnp.concatenate([vb[:, vs] * vm_ref[h] for h in range(hpg)], axis=0)
        intra = _dot(att.astype(BF16), v_blocks)
        st = st_ref[g]
        q_stack = jnp.concatenate([(qg * e_q[:, ks]) * hm_ref[h] for h in range(hpg)], axis=0)
        inter = _dot_nt(q_stack.astype(BF16), st.astype(BF16))
        upd = _dot_tn(vb[:, vs], (kg * e_k[:, ks]).astype(BF16))
        new = st * e_tot[:, ks]
        for h in range(hpg):
            new = new + upd[h * dv:(h + 1) * dv] * hm_ref[h]
        st_ref[g] = new
        outs.append(intra + jnp.concatenate([inter[h * c:(h + 1) * c] for h in range(hpg)], axis=-1))
    return jnp.concatenate(outs, axis=-1)


def _log_sigmoid(x):
    return jnp.minimum(x, 0.0) - jnp.log(1.0 + jnp.exp(-jnp.abs(x)))


def _gla_kernel(*refs):
    ins, head_masks, (o_refs, st_refs) = (refs[0:8], refs[8:16]), refs[16:19], (refs[19:21], refs[21:23])

    @pl.when(pl.program_id(1) == 0)
    def _():
        for st_ref in st_refs:
            st_ref[...] = jnp.zeros_like(st_ref)

    for d, ((q_ref, k_ref, v_ref, lr_ref, gw_ref, gb_ref, sel_ref, mask_ref), o_ref, st_ref) in enumerate(
            zip(ins, o_refs, st_refs)):
        z = _dot_f32(lr_ref[...], gw_ref[...]) + gb_ref[...]
        la = _log_sigmoid(z) * (1.0 / GLA_TAU)
        q = q_ref[...] * (GLA_DK ** -0.5)
        k, v = k_ref[...], v_ref[...]
        for rows in _sub_chunks(q.shape[0], d == 1):
            o_ref[rows, :] = _decay_chunk(q[rows], k[rows], v[rows], la[rows], (sel_ref, mask_ref) + head_masks,
                                          st_ref, GLA_HEADS, GLA_DK, GLA_DV).astype(o_ref.dtype)


def _hgrn_kernel(*refs, layer):
    ins, head_masks, (o_refs, st_refs) = (refs[0:6], refs[6:12]), refs[12:15], (refs[15:17], refs[17:19])

    @pl.when(pl.program_id(1) == 0)
    def _():
        for st_ref in st_refs:
            st_ref[...] = jnp.zeros_like(st_ref)

    for d, ((q_ref, f_ref, v_ref, lb_ref, sel_ref, mask_ref), o_ref, st_ref) in enumerate(
            zip(ins, o_refs, st_refs)):
        e = jnp.exp(lb_ref[...] - jnp.max(lb_ref[...], axis=0, keepdims=True))
        lb = jnp.sum(e[0:layer + 1], axis=0, keepdims=True) / jnp.sum(e, axis=0, keepdims=True)
        f = lb + (1.0 - lb) * _sigmoid(f_ref[...])
        q, k, v, la = _silu(q_ref[...]), 1.0 - f, v_ref[...], jnp.log(f)
        for rows in _sub_chunks(q.shape[0], d == 1):
            o_ref[rows, :] = _decay_chunk(q[rows], k[rows], v[rows], la[rows], (sel_ref, mask_ref) + head_masks,
                                          st_ref, HG_HEADS, HG_EXPAND, HG_DV).astype(o_ref.dtype)


def _sub_chunks(rows, reverse, chunk=SCAN_CHUNK):
    order = range(rows // chunk)
    return [slice(j * chunk, (j + 1) * chunk) for j in (reversed(order) if reverse else order)]


def _scan_specs(blk, n_ctx, t, reverse, chunk=None, stacked_heads=1):
    n_blocks = t // blk
    order = functools.partial(_chunk_order, n_ctx_chunks=n_ctx // blk, n_chunks=n_blocks, reverse=reverse)

    def col(width, idx):
        return pl.BlockSpec((None, blk, width), lambda b, i: (b, order(i), idx))

    sel, msk = _scan_constants(chunk or blk, reverse)
    sel3 = np.concatenate([sel, sel, sel], axis=1)
    msk = np.tile(msk, (1, 1, stacked_heads))
    const = lambda a: pl.BlockSpec(a.shape, lambda b, i: (0,) * a.ndim)
    return n_blocks, col, const, jnp.asarray(sel3, BF16), jnp.asarray(msk, F32)


def _head_masks(dk, dv):
    hpg = GROUP_KEYS // dk
    hm = np.zeros((hpg, 1, GROUP_KEYS), np.float32)
    vm = np.zeros((hpg, 1, hpg * dv), np.float32)
    for h in range(hpg):
        hm[h, 0, h * dk:(h + 1) * dk] = 1.0
        vm[h, 0, h * dv:(h + 1) * dv] = 1.0
    return jnp.asarray(hm), jnp.asarray(hm, BF16), jnp.asarray(vm, BF16)


AB_Q, AB_K, AB_G = 0, 256, 512
AB_HQ, AB_HF, AB_HG, AB_LR = 1024, 1536, 2560, 3072
AB_PAD_COLS = 3200
AB_V, AB_HI = 0, 1


def _gla_scan(proj, vals, gate_w_pad, gate_b, n_ctx):
    bsz, t, _ = proj.shape
    hpg = GROUP_KEYS // GLA_DK
    in_specs, args, outs = [], [], []
    for d in range(2):
        n_blocks, col, const, sel, msk = _scan_specs(SCAN_BLOCK, n_ctx, t, d == 1, SCAN_CHUNK, hpg)
        in_specs += [col(GLA_KEY_W, AB_Q // GLA_KEY_W), col(GLA_KEY_W, AB_K // GLA_KEY_W),
                     col(GLA_VAL_W, AB_V), col(LANES, AB_LR // LANES),
                     const(gate_w_pad[d]), const(gate_b[d]), const(sel), const(msk)]
        args += [proj, proj, vals, proj, gate_w_pad[d], gate_b[d], sel, msk]
        outs.append(col(GLA_VAL_W, 0))
    hm = _head_masks(GLA_DK, GLA_DV)
    return pl.pallas_call(
        _gla_kernel,
        grid=(bsz, n_blocks),
        in_specs=in_specs + [const(m) for m in hm],
        out_specs=outs,
        out_shape=[jax.ShapeDtypeStruct((bsz, t, GLA_VAL_W), BF16)] * 2,
        scratch_shapes=[pltpu.VMEM((GLA_HEADS // hpg, GLA_DV, GROUP_KEYS), F32)] * 2,
        compiler_params=_params("parallel", "arbitrary"),
        name="gla_scan",
    )(*args, *hm)


def _hgrn_scan(proj, vals, lb, layer, n_ctx):
    bsz, t, _ = proj.shape
    hpg = GROUP_KEYS // HG_EXPAND
    in_specs, args, outs = [], [], []
    for d in range(2):
        n_blocks, col, const, sel, msk = _scan_specs(SCAN_BLOCK, n_ctx, t, d == 1, SCAN_CHUNK, hpg)
        in_specs += [col(HG_KEY_W, AB_HQ // HG_KEY_W), col(HG_KEY_W, AB_HF // HG_KEY_W + d),
                     col(HG_VAL_W, AB_HI), const(lb[d]), const(sel), const(msk)]
        args += [proj, proj, vals, lb[d], sel, msk]
        outs.append(col(HG_VAL_W, 0))
    hm = _head_masks(HG_EXPAND, HG_DV)
    return pl.pallas_call(
        functools.partial(_hgrn_kernel, layer=layer),
        grid=(bsz, n_blocks),
        in_specs=in_specs + [const(m) for m in hm],
        out_specs=outs,
        out_shape=[jax.ShapeDtypeStruct((bsz, t, HG_VAL_W), BF16)] * 2,
        scratch_shapes=[pltpu.VMEM((HG_HEADS // hpg, HG_DV, GROUP_KEYS), F32)] * 2,
        compiler_params=_params("parallel", "arbitrary"),
        name="hgrn_scan",
    )(*args, *hm)


def _mix_out_ab_kernel(*refs, ctx_tiles):
    (c_ref, x_ref, gf_ref, gb_ref, hf_ref, hb_ref, gg_ref, hg_ref, gn_ref, hn_ref, w_ref, gt_ref) = refs[0:12]
    pre_in, o_ref, pre_out = refs[12:12 + N_FFN_PRE_IN], refs[12 + N_FFN_PRE_IN], refs[13 + N_FFN_PRE_IN:]
    feats = []
    both = lambda fwd_ref, bwd_ref: fwd_ref[...].astype(F32) + bwd_ref[...].astype(F32)
    for o, gate, g in ((both(gf_ref, gb_ref), gg_ref[...], gn_ref[...]),
                       (both(hf_ref, hb_ref), hg_ref[...], hn_ref[...])):
        for hd in range(o.shape[-1] // LANES):
            s = slice(hd * LANES, (hd + 1) * LANES)
            feats.append(_rms(o[:, s], g) * _silu(gate[:, s]))
    feat = jnp.concatenate(feats, axis=-1).astype(BF16)
    h_new = _joint_rows(c_ref, x_ref, ctx_tiles) + gt_ref[...] * _dot(feat, w_ref[...])
    o_ref[...] = h_new
    _ffn_pre_body(h_new, *pre_in, *pre_out)


def _mix_out_ab(ctx, x, o_gla, o_hg, proj, gla_norm_g, hg_norm_g, w_out, gate, pre):
    bsz, n_ctx, d = ctx.shape
    t = n_ctx + x.shape[1]
    tm = ROW_TILE
    seg = lambda b, i: (b, (i * tm >= n_ctx).astype(jnp.int32), 0, 0)
    row = lambda width, idx: pl.BlockSpec((None, tm, width), lambda b, i: (b, i, idx))
    vec = pl.BlockSpec((1, LANES), lambda b, i: (0, 0))
    p_in, p_args, p_out, p_shape, p_scratch = _ffn_pre_parts(bsz, t, d, *pre, seg)
    row_specs, ctx_tiles = _joint_rows_specs(ctx, x)
    outs = pl.pallas_call(
        functools.partial(_mix_out_ab_kernel, ctx_tiles=ctx_tiles),
        grid=(bsz, t // tm),
        in_specs=row_specs + [row(GLA_VAL_W, 0), row(GLA_VAL_W, 0), row(HG_VAL_W, 0), row(HG_VAL_W, 0),
                  row(GLA_VAL_W, AB_G // GLA_VAL_W), row(HG_VAL_W, AB_HG // HG_VAL_W), vec, vec,
                  pl.BlockSpec(w_out.shape, lambda b, i: (0, 0)),
                  pl.BlockSpec((None, None, 1, d), seg)] + p_in,
        out_specs=[row(d, 0)] + p_out,
        out_shape=[jax.ShapeDtypeStruct((bsz, t, d), F32)] + p_shape,
        scratch_shapes=p_scratch,
        compiler_params=_params("arbitrary", "arbitrary"),
        name="mix_out_ab",
    )(ctx, x, o_gla[0], o_gla[1], o_hg[0], o_hg[1], proj, proj, gla_norm_g.reshape(1, -1),
      hg_norm_g.reshape(1, -1), w_out, gate, *p_args)
    return outs[0], outs[1:]


HALO = 8


def _norm_proj_conv_kernel(h_ref, hp_ref, hn_ref, g_ref, sh_ref, sc_ref, cw_ref, cb_ref, *refs,
                           ctx_tiles, n_tiles):
    w_refs, (hy_ref, xbc_ref, zdt_ref) = refs[:-3], refs[-3:]
    i = pl.program_id(1)
    tm = h_ref.shape[0]
    hh = jnp.concatenate([hp_ref[...], h_ref[...], hn_ref[...]], axis=0)
    u = _rms(hh, g_ref[...]) * (1.0 + sc_ref[...]) + sh_ref[...]
    p = _project(u.astype(BF16), w_refs)
    n_conv = cw_ref.shape[1]
    n_hy = hy_ref.shape[1]
    pc = p[:, 0:n_conv]
    rows_all = tm + 2 * HALO
    cur = pc[HALO:HALO + tm]
    prev = pltpu.roll(pc, 1, axis=0)[HALO:HALO + tm]
    nxt = pltpu.roll(pc, rows_all - 1, axis=0)[HALO:HALO + tm]
    first = jnp.logical_or(i == 0, i == ctx_tiles)
    last = jnp.logical_or(i == ctx_tiles - 1, i == n_tiles - 1)
    rows = lax.broadcasted_iota(jnp.int32, cur.shape, 0)
    prev = jnp.where(jnp.logical_and(first, rows == 0), 0.0, prev)
    nxt = jnp.where(jnp.logical_and(last, rows == tm - 1), 0.0, nxt)
    y = prev * cw_ref[0:1, :] + cur * cw_ref[1:2, :] + nxt * cw_ref[2:3, :] + cb_ref[...]
    hy_ref[...] = y[:, 0:n_hy]
    xbc_ref[...] = _silu(y[:, n_hy:n_conv])
    zdt_ref[...] = p[HALO:HALO + tm, n_conv:]


def _norm_proj_conv(h, g, shift, scale, ws, conv_w, conv_b, n_hy, n_ctx):
    bsz, t, d = h.shape
    n = sum(w.shape[1] for w in ws)
    n_conv = conv_w.shape[1]
    tm = ROW_TILE
    n_tiles, ctx_tiles = t // tm, n_ctx // tm
    r8 = tm // HALO
    last8 = t // HALO - 1
    seg = lambda b, i: (b, (i >= ctx_tiles).astype(jnp.int32), 0, 0)
    kern = functools.partial(_norm_proj_conv_kernel, ctx_tiles=ctx_tiles, n_tiles=n_tiles)
    return pl.pallas_call(
        kern,
        grid=(bsz, n_tiles),
        in_specs=[pl.BlockSpec((None, tm, d), lambda b, i: (b, i, 0)),
                  pl.BlockSpec((None, HALO, d), lambda b, i: (b, jnp.maximum(i * r8 - 1, 0), 0)),
                  pl.BlockSpec((None, HALO, d), lambda b, i: (b, jnp.minimum((i + 1) * r8, last8), 0)),
                  pl.BlockSpec((1, d), lambda b, i: (0, 0)),
                  pl.BlockSpec((None, None, 1, d), seg),
                  pl.BlockSpec((None, None, 1, d), seg),
                  pl.BlockSpec((3, n_conv), lambda b, i: (0, 0)),
                  pl.BlockSpec((1, n_conv), lambda b, i: (0, 0))]
                 + [pl.BlockSpec(w.shape, lambda b, i: (0, 0)) for w in ws],
        out_specs=[pl.BlockSpec((None, tm, n_hy), lambda b, i: (b, jnp.maximum(i - ctx_tiles, 0), 0)),
                   pl.BlockSpec((None, tm, n_conv - n_hy), lambda b, i: (b, i, 0)),
                   pl.BlockSpec((None, tm, n - n_conv), lambda b, i: (b, i, 0))],
        out_shape=[jax.ShapeDtypeStruct((bsz, t - n_ctx, n_hy), F32),
                   jax.ShapeDtypeStruct((bsz, t, n_conv - n_hy), F32),
                   jax.ShapeDtypeStruct((bsz, t, n - n_conv), F32)],
        compiler_params=_params("parallel", "arbitrary"),
        name="norm_proj_conv",
    )(h, h, h, g.reshape(1, d), shift, scale, conv_w, conv_b, *ws)


def _ssd_kernel(*refs):
    ins, hexp_ref, o_refs, st_refs = (refs[0:7], refs[7:14]), refs[14], refs[15:17], refs[17:19]

    @pl.when(pl.program_id(1) == 0)
    def _():
        for st_ref in st_refs:
            st_ref[...] = jnp.zeros_like(st_ref)

    for d, ((xbc_ref, dt_ref, bias_ref, alog_ref, mq_ref, mk_ref, mask_ref), o_ref, st_ref) in enumerate(
            zip(ins, o_refs, st_refs)):
        for rows in _sub_chunks(xbc_ref.shape[0], d == 1, SSD_CHUNK):
            _ssd_chunk(rows, xbc_ref, dt_ref, bias_ref, alog_ref, hexp_ref, mq_ref, mk_ref, mask_ref, o_ref, st_ref)


def _ssd_chunk(rows, xbc_ref, dt_ref, bias_ref, alog_ref, hexp_ref, mq_ref, mk_ref, mask_ref, o_ref, st_ref):
    c = SSD_CHUNK
    hpg = MB_HEADS // MB_GROUPS
    gw = hpg * MB_HEAD_DIM
    dt = _softplus(dt_ref[rows, :] + bias_ref[...])
    la = -dt * jnp.exp(alog_ref[...])
    la3 = jnp.concatenate(_split3(la), axis=0)
    cq = _dot(mq_ref[...], la3)
    ck = _dot(mk_ref[...], la3)
    cq_t = lax.dot_general(la3, mq_ref[...], (((0,), (1,)), ((), ())), preferred_element_type=F32)
    tot = jnp.broadcast_to(jnp.sum(la, axis=0, keepdims=True), (8, la.shape[1]))
    per_head = jnp.concatenate([dt, cq, ck, tot], axis=0)
    per_lane = _dot(jnp.concatenate(_split3(per_head), axis=1), hexp_ref[...])
    dt_x = per_lane[0:c]
    eq_x = jnp.exp(per_lane[c:2 * c])
    ek_x = jnp.exp(per_lane[2 * c:3 * c])
    etot_x = jnp.exp(per_lane[3 * c:3 * c + 1])
    xs = xbc_ref[rows, 0:MB_INNER] * dt_x
    mask = mask_ref[...]
    outs = []
    for g in range(MB_GROUPS):
        bm = xbc_ref[rows, MB_INNER + g * MB_STATE:MB_INNER + (g + 1) * MB_STATE].astype(BF16)
        cm = xbc_ref[rows, MB_INNER + MB_BC_W + g * MB_STATE:MB_INNER + MB_BC_W + (g + 1) * MB_STATE].astype(BF16)
        cb = _dot_nt(cm, bm)
        st = st_ref[g]
        gs = slice(g * gw, (g + 1) * gw)
        y_inter = _dot(cm, st.astype(BF16)) * eq_x[:, gs]
        for r in range(hpg):
            hd = g * hpg + r
            diff = cq[:, hd:hd + 1] - cq_t[hd:hd + 1, :]
            w = cb * jnp.exp(jnp.where(mask > 0.0, diff, -jnp.inf))
            ps = slice(hd * MB_HEAD_DIM, (hd + 1) * MB_HEAD_DIM)
            outs.append(_dot(w.astype(BF16), xs[:, ps].astype(BF16))
                        + y_inter[:, r * MB_HEAD_DIM:(r + 1) * MB_HEAD_DIM])
        st_ref[g] = st * etot_x[:, gs] + _dot_tn(bm, (xs[:, gs] * ek_x[:, gs]).astype(BF16))
    o_ref[rows, :] = jnp.concatenate(outs, axis=-1).astype(o_ref.dtype)


def _ssd_scan(xbc, proj, dt_col, dt_bias, a_log, n_ctx):
    bsz, t, _ = xbc.shape
    c = SSD_CHUNK
    pad = lambda v: jnp.zeros((1, LANES), F32).at[0, :MB_HEADS].set(v)
    hexp = np.zeros((LANES, MB_INNER), np.float32)
    for hd in range(MB_HEADS):
        hexp[hd, hd * MB_HEAD_DIM:(hd + 1) * MB_HEAD_DIM] = 1.0
    hexp = jnp.asarray(np.concatenate([hexp, hexp, hexp], axis=0), BF16)
    tri = np.tril(np.ones((c, c), np.float32))
    in_specs, args, outs = [], [], []
    for d in range(2):
        n_chunks, col, const, sel, _ = _scan_specs(SSD_BLOCK, n_ctx, t, d == 1, c)
        mq, mk = sel[0:c], sel[c:2 * c]
        mask = jnp.asarray(tri[::-1, ::-1].copy() if d == 1 else tri)
        bias, alog = pad(dt_bias[d]), pad(a_log[d].astype(F32))
        in_specs += [col(xbc.shape[-1], 0), col(LANES, dt_col + d), const(bias), const(alog),
                     const(mq), const(mk), const(mask)]
        args += [xbc, proj, bias, alog, mq, mk, mask]
        outs.append(col(MB_INNER, 0))
    return pl.pallas_call(
        _ssd_kernel,
        grid=(bsz, n_chunks),
        in_specs=in_specs + [const(hexp)],
        out_specs=outs,
        out_shape=[jax.ShapeDtypeStruct((bsz, t, MB_INNER), BF16)] * 2,
        scratch_shapes=[pltpu.VMEM((MB_GROUPS, MB_STATE, MB_INNER // MB_GROUPS), F32)] * 2,
        compiler_params=_params("parallel", "arbitrary"),
        name="ssd_scan",
    )(*args, hexp)


def _mix_out_cd_kernel(*refs):
    h_ref, hy_ref, yf_ref, yb_ref, xs_ref, z_ref, dsk_ref, ng_ref, w_ref, gt_ref = refs[0:10]
    pre_in, o_ref, pre_out = refs[10:10 + N_FFN_PRE_IN], refs[10 + N_FFN_PRE_IN], refs[11 + N_FFN_PRE_IN:]
    y = (yf_ref[...].astype(F32) + yb_ref[...].astype(F32) + dsk_ref[...] * xs_ref[...]) * _silu(z_ref[...])
    gw = MB_INNER // MB_GROUPS
    ys = [_rms(y[:, g * gw:(g + 1) * gw], ng_ref[:, g * gw:(g + 1) * gw]) for g in range(MB_GROUPS)]
    feat = jnp.concatenate([hy_ref[...]] + ys, axis=-1).astype(BF16)
    h_new = h_ref[...] + gt_ref[...] * _dot(feat, w_ref[...])
    o_ref[...] = h_new
    _ffn_pre_body(h_new, *pre_in, *pre_out)


def _mix_out_cd(h, hy, y_ssd, xbc, proj, z_col, d_skip_x, norm_g, w_out, gate, n_ctx, pre):
    bsz, t, d = h.shape
    tm = ROW_TILE
    n_lat = t - n_ctx
    off = n_ctx // tm
    row = lambda width, idx: pl.BlockSpec((None, tm, width), lambda b, i: (b, i + off, idx))
    vec = pl.BlockSpec((1, MB_INNER), lambda b, i: (0, 0))
    latent = lambda b, i: (b, 1, 0, 0)
    p_in, p_args, p_out, p_shape, p_scratch = _ffn_pre_parts(bsz, n_lat, d, *pre, latent)
    outs = pl.pallas_call(
        _mix_out_cd_kernel,
        grid=(bsz, n_lat // tm),
        in_specs=[row(d, 0), pl.BlockSpec((None, tm, HY_CH), lambda b, i: (b, i, 0)),
                  row(MB_INNER, 0), row(MB_INNER, 0), row(MB_INNER, 0), row(MB_INNER, z_col), vec, vec,
                  pl.BlockSpec(w_out.shape, lambda b, i: (0, 0)),
                  pl.BlockSpec((None, None, 1, d), latent)] + p_in,
        out_specs=[pl.BlockSpec((None, tm, d), lambda b, i: (b, i, 0))] + p_out,
        out_shape=[jax.ShapeDtypeStruct((bsz, n_lat, d), F32)] + p_shape,
        scratch_shapes=p_scratch,
        compiler_params=_params("arbitrary", "arbitrary"),
        name="mix_out_cd",
    )(h, hy, y_ssd[0], y_ssd[1], xbc, proj, d_skip_x, norm_g.reshape(1, -1), w_out, gate, *p_args)
    return outs[0], outs[1:]


def _top2_of4(a, b, c, d):
    hi1, lo1, hi2, lo2 = jnp.maximum(a, b), jnp.minimum(a, b), jnp.maximum(c, d), jnp.minimum(c, d)
    return jnp.maximum(hi1, hi2) + jnp.maximum(jnp.minimum(hi1, hi2), jnp.maximum(lo1, lo2))


def _first_argmax(vals, skip=None):
    idx = None
    for j, vj in enumerate(vals):
        if idx is None and skip is None:
            idx, best = jnp.zeros(vj.shape, jnp.int32), vj
            continue
        if idx is None:
            idx, best = jnp.full(vj.shape, -1, jnp.int32), jnp.full(vj.shape, -jnp.inf, F32)
        take = vj > best
        if skip is not None:
            take = jnp.logical_and(take, skip != j)
        idx = jnp.where(take, j, idx)
        best = jnp.where(take, vj, best)
    return idx, best


def _ffn_pre_body(h, g_ref, sh_ref, sc_ref, rw_ref, rb_ref, tri_ref, v_ref, ri_ref, rwt_ref, cnt_ref, carry_ref):
    @pl.when(jnp.logical_and(pl.program_id(0) == 0, pl.program_id(1) == 0))
    def _():
        carry_ref[...] = jnp.zeros_like(carry_ref)

    v = _rms(h, g_ref[...]) * (1.0 + sc_ref[...]) + sh_ref[...]
    v_ref[...] = _pack_bf16_pairs(v)
    st = _sigmoid(_dot_f32(v, rw_ref[...])).T[0:N_EXPERTS]
    sel = st + rb_ref[...]
    row = lambda a, e: a[e:e + 1]
    epg = EXPERTS_PER_GROUP
    gscore = [_top2_of4(*[row(sel, g * epg + j) for j in range(epg)]) for g in range(N_GROUPS)]
    best, _ = _first_argmax(gscore)

    def in_best(a, j):
        out = row(a, j)
        for g in range(1, N_GROUPS):
            out = jnp.where(best == g, row(a, g * epg + j), out)
        return out

    vals = [in_best(sel, j) for j in range(epg)]
    raw = [in_best(st, j) for j in range(epg)]
    i1, _ = _first_argmax(vals)
    i2, _ = _first_argmax(vals, skip=i1)
    pick = lambda i: functools.reduce(lambda acc, j: jnp.where(i == j, raw[j], acc), range(1, epg), raw[0])
    w1, w2 = pick(i1), pick(i2)
    wsum = w1 + w2
    e1, e2 = best * epg + i1, best * epg + i2

    experts = lax.broadcasted_iota(jnp.int32, st.shape, 0)
    oh1 = (experts == e1).astype(F32)
    oh2 = (experts == e2).astype(F32)
    cnt = oh1 + oh2
    before = _dot(cnt.astype(BF16), tri_ref[...]) + carry_ref[:, 0:1]
    ri_ref[0:1, :] = e1
    ri_ref[1:2, :] = e2
    ri_ref[2:3, :] = jnp.sum(oh1 * before, axis=0, keepdims=True).astype(jnp.int32)
    ri_ref[3:4, :] = jnp.sum(oh2 * before, axis=0, keepdims=True).astype(jnp.int32)
    ri_ref[4:8, :] = jnp.zeros((4, st.shape[1]), jnp.int32)
    lane_row = lax.broadcasted_iota(jnp.int32, (LANES, st.shape[1]), 0)
    rwt_ref[...] = jnp.where(lane_row == 0, w1 / wsum, jnp.where(lane_row == 1, w2 / wsum, 0.0)).T
    carry_ref[...] = carry_ref[...] + jnp.sum(cnt, axis=1, keepdims=True)
    cnt_ref[...] = carry_ref[...]


N_FFN_PRE_IN = 6


def _ffn_pre_parts(bsz, t, d, g, shift, scale, router_w_pad, router_b, seg):
    tm = ROW_TILE
    tri = jnp.asarray(np.triu(np.ones((tm, tm), np.float32), 1), BF16)
    in_specs = [pl.BlockSpec((1, d), lambda b, i: (0, 0)),
                pl.BlockSpec((None, None, 1, d), seg),
                pl.BlockSpec((None, None, 1, d), seg),
                pl.BlockSpec((d, LANES), lambda b, i: (0, 0)),
                pl.BlockSpec((N_EXPERTS, 1), lambda b, i: (0, 0)),
                pl.BlockSpec((tm, tm), lambda b, i: (0, 0))]
    args = (g.reshape(1, d), shift, scale, router_w_pad, router_b.reshape(N_EXPERTS, 1), tri)
    out_specs = [pl.BlockSpec((None, tm, d // 2), lambda b, i: (b, i, 0)),
                 pl.BlockSpec((None, 8, tm), lambda b, i: (b, 0, i)),
                 pl.BlockSpec((None, tm, LANES), lambda b, i: (b, i, 0)),
                 pl.BlockSpec((N_EXPERTS, LANES), lambda b, i: (0, 0))]
    out_shape = [jax.ShapeDtypeStruct((bsz, t, d // 2), jnp.uint32),
                 jax.ShapeDtypeStruct((bsz, 8, t), jnp.int32),
                 jax.ShapeDtypeStruct((bsz, t, LANES), F32),
                 jax.ShapeDtypeStruct((N_EXPERTS, LANES), F32)]
    return in_specs, args, out_specs, out_shape, [pltpu.VMEM((N_EXPERTS, LANES), F32)]


def _experts_kernel(be_ref, nb_ref, x_ref, wg_ref, wu_ref, wd_ref, o_ref, wg_s, wu_s, wd_s):
    i = pl.program_id(0)
    prev = be_ref[jnp.maximum(i - 1, 0)]
    changed = jnp.logical_or(i == 0, be_ref[i] != prev)

    @pl.when(changed)
    def _():
        wg_s[...] = wg_ref[...].astype(BF16)
        wu_s[...] = wu_ref[...].astype(BF16)
        wd_s[...] = wd_ref[...].astype(BF16)

    @pl.when(i < nb_ref[0])
    def _():
        x = _unpack_bf16_pairs(x_ref[...]).astype(BF16)
        hid = _silu(_dot(x, wg_s[...])) * _dot(x, wu_s[...])
        o_ref[...] = _pack_bf16_pairs(_dot(hid.astype(BF16), wd_s[...]))

    @pl.when(i >= nb_ref[0])
    def _():
        o_ref[...] = jnp.zeros_like(o_ref)


def _experts(xb, block_e, n_used, layer, w_gate, w_up, w_down):
    n_slots = xb.shape[0]
    n_blocks = n_slots // MOE_BLOCK
    d, de = w_gate.shape[-2:]
    wspec = lambda shape: pl.BlockSpec((None, None) + shape, lambda i, be, nb: (layer, be[i], 0, 0))
    return pl.pallas_call(
        _experts_kernel,
        grid_spec=pltpu.PrefetchScalarGridSpec(
            num_scalar_prefetch=2,
            grid=(n_blocks,),
            in_specs=[pl.BlockSpec((MOE_BLOCK, d // 2), lambda i, be, nb: (i, 0)),
                      wspec((d, de)), wspec((d, de)), wspec((de, d))],
            out_specs=pl.BlockSpec((MOE_BLOCK, d // 2), lambda i, be, nb: (i, 0)),
            scratch_shapes=[pltpu.VMEM((d, de), BF16), pltpu.VMEM((d, de), BF16), pltpu.VMEM((de, d), BF16)]),
        out_shape=jax.ShapeDtypeStruct((n_slots, d // 2), jnp.uint32),
        compiler_params=_params("arbitrary"),
        name="moe_experts",
    )(block_e, n_used, xb, w_gate, w_up, w_down)


def _ffn_post_kernel(h_ref, y0_ref, y1_ref, w_ref, gt_ref, g_ref, o_ref, *, final):
    w = w_ref[...]
    y = w[:, 0:1] * _unpack_bf16_pairs(y0_ref[...]) + w[:, 1:2] * _unpack_bf16_pairs(y1_ref[...])
    out = h_ref[...] + gt_ref[...] * y
    o_ref[...] = _rms(out, g_ref[...]) if final else out


def _ffn_post(h, y, w, gate, n_ctx, final_g=None):
    bsz, t, d = h.shape
    tm = ROW_TILE
    seg = lambda b, i: (b, (i * tm >= n_ctx).astype(jnp.int32), 0, 0)
    row = lambda width: pl.BlockSpec((None, tm, width), lambda b, i: (b, i, 0))
    choice = lambda kk: pl.BlockSpec((None, None, tm, d // 2), lambda b, i: (kk, b, i, 0))
    final = final_g is not None
    g = final_g if final else jnp.ones((d,), F32)
    return pl.pallas_call(
        functools.partial(_ffn_post_kernel, final=final),
        grid=(bsz, t // tm),
        in_specs=[row(d), choice(0), choice(1), row(LANES), pl.BlockSpec((None, None, 1, d), seg),
                  pl.BlockSpec((1, d), lambda b, i: (0, 0))],
        out_specs=row(d),
        out_shape=jax.ShapeDtypeStruct((bsz, t, d), F32),
        compiler_params=_params("parallel", "parallel"),
        name="ffn_post",
    )(h, y, y, w, gate, g.reshape(1, d))


def _slot_layout(n, ri, counts):
    e = jnp.swapaxes(ri[:, 0:2], 0, 1).reshape(TOP_K, n)
    rank = jnp.swapaxes(ri[:, 2:4], 0, 1).reshape(TOP_K, n)
    padded = (counts + MOE_BLOCK - 1) // MOE_BLOCK * MOE_BLOCK
    pend = jnp.cumsum(padded)
    pstart = pend - padded
    experts = jnp.arange(N_EXPERTS, dtype=jnp.int32)
    dest = rank + jnp.sum(jnp.where(e[..., None] == experts, pstart, 0), axis=-1)
    n_slots = (n * TOP_K + MOE_BLOCK - 1) // MOE_BLOCK * MOE_BLOCK + N_EXPERTS * MOE_BLOCK
    n_blocks = n_slots // MOE_BLOCK
    blk0 = jnp.arange(n_blocks, dtype=jnp.int32)[:, None] * MOE_BLOCK
    block_e = jnp.minimum(jnp.sum((pend[None, :] <= blk0).astype(jnp.int32), axis=-1), N_EXPERTS - 1)
    n_used = (pend[-1] // MOE_BLOCK).astype(jnp.int32).reshape(1)
    return dest, n_slots, block_e.astype(jnp.int32), n_used


SC_CORES, SC_SUBCORES = 2, 16
SC_GATHER_WINDOW = 64
SC_SCATTER_WINDOW = 32


def _sc_window(per_worker, largest):
    win = largest
    while per_worker % win:
        win //= 2
    assert win >= 8
    return win


def _gather_rows(table, idx):
    n_rows, d = idx.shape[0], table.shape[1]
    workers = SC_CORES * SC_SUBCORES
    per_worker = n_rows // workers
    assert per_worker * workers == n_rows
    win = _sc_window(per_worker, SC_GATHER_WINDOW)
    mesh = plsc.VectorSubcoreMesh(core_axis_name="c", subcore_axis_name="s")

    @functools.partial(
        pl.kernel, mesh=mesh,
        out_type=jax.ShapeDtypeStruct((n_rows, d), table.dtype),
        scratch_types=[pltpu.VMEM((win,), jnp.int32), pltpu.VMEM((win,), jnp.int32),
                       pltpu.VMEM((win, d), table.dtype), pltpu.VMEM((win, d), table.dtype),
                       pltpu.SemaphoreType.DMA, pltpu.SemaphoreType.DMA],
    )
    def gather_kernel(table_hbm, idx_hbm, out_hbm, idx0, idx1, rows0, rows1, sem0, sem1):
        base = (lax.axis_index("s") * SC_CORES + lax.axis_index("c")) * per_worker
        n_win = per_worker // win
        slots = ((idx0, rows0, sem0), (idx1, rows1, sem1))
        window = lambda j: pl.ds(pl.multiple_of(base + j * win, 8), win)

        def start(j, slot):
            idx_v, rows_v, sem = slots[slot]
            pltpu.sync_copy(idx_hbm.at[window(j)], idx_v)
            pltpu.async_copy(table_hbm.at[idx_v], rows_v, sem)

        def finish(j, slot):
            idx_v, rows_v, sem = slots[slot]
            pltpu.make_async_copy(table_hbm.at[idx_v], rows_v, sem).wait()
            pltpu.sync_copy(rows_v, out_hbm.at[window(j)])

        start(0, 0)

        @pl.loop(0, n_win, step=2)
        def _(j):
            @pl.when(j + 1 < n_win)
            def _():
                start(j + 1, 1)

            finish(j, 0)

            @pl.when(j + 2 < n_win)
            def _():
                start(j + 2, 0)

            @pl.when(j + 1 < n_win)
            def _():
                finish(j + 1, 1)

    return gather_kernel(table, idx)


def _scatter_rows(src, dest, n_slots):
    n, d = src.shape
    workers = SC_CORES * SC_SUBCORES
    per_worker = n // workers
    assert per_worker * workers == n and dest.shape == (TOP_K, n)
    win = _sc_window(per_worker, SC_SCATTER_WINDOW)
    mesh = plsc.VectorSubcoreMesh(core_axis_name="c", subcore_axis_name="s")

    @functools.partial(
        pl.kernel, mesh=mesh,
        out_type=jax.ShapeDtypeStruct((n_slots, d), src.dtype),
        scratch_types=[pltpu.VMEM((win,), jnp.int32)] * 4 + [pltpu.VMEM((win, d), src.dtype)] * 2
                      + [pltpu.SemaphoreType.DMA] * 6,
    )
    def scatter_kernel(src_hbm, dest_hbm, out_hbm, i0a, i1a, i0b, i1b, rows_a, rows_b,
                       load_a, first_a, second_a, load_b, first_b, second_b):
        base = (lax.axis_index("s") * SC_CORES + lax.axis_index("c")) * per_worker
        n_win = per_worker // win
        slots = ((i0a, i1a, rows_a, load_a, first_a, second_a), (i0b, i1b, rows_b, load_b, first_b, second_b))
        window = lambda j: pl.ds(pl.multiple_of(base + j * win, 8), win)

        def load(j, slot):
            idx0, idx1, rows_v, sem, _, _ = slots[slot]
            pltpu.async_copy(src_hbm.at[window(j)], rows_v, sem)
            pltpu.sync_copy(dest_hbm.at[0, window(j)], idx0)
            pltpu.sync_copy(dest_hbm.at[1, window(j)], idx1)

        def scatter(j, slot):
            idx0, idx1, rows_v, sem, sem0, sem1 = slots[slot]
            pltpu.make_async_copy(src_hbm.at[window(j)], rows_v, sem).wait()
            pltpu.async_copy(rows_v, out_hbm.at[idx0], sem0)
            pltpu.async_copy(rows_v, out_hbm.at[idx1], sem1)

        def drain(slot):
            idx0, idx1, rows_v, _, sem0, sem1 = slots[slot]
            pltpu.make_async_copy(rows_v, out_hbm.at[idx0], sem0).wait()
            pltpu.make_async_copy(rows_v, out_hbm.at[idx1], sem1).wait()

        load(0, 0)

        @pl.loop(0, n_win, step=2)
        def _(j):
            scatter(j, 0)

            @pl.when(j + 1 < n_win)
            def _():
                load(j + 1, 1)

            drain(0)

            @pl.when(j + 1 < n_win)
            def _():
                scatter(j + 1, 1)

            @pl.when(j + 2 < n_win)
            def _():
                load(j + 2, 0)

            @pl.when(j + 1 < n_win)
            def _():
                drain(1)

    return scatter_kernel(src, dest)


def _alongside(gather, idx, side_fn, side_in):
    idx, side_in = lax.optimization_barrier((idx, side_in))
    return lax.optimization_barrier((gather(idx), side_fn(side_in)))


def _moe(h, pre_out, gate, layer, w_gate, w_up, w_down, n_ctx, final_g=None, side=None):
    bsz, t, d = h.shape
    n = bsz * t
    v, ri, rwt, counts = pre_out
    dest, n_slots, block_e, n_used = _slot_layout(n, ri, counts[:, 0].astype(jnp.int32))
    dispatch = lambda idx: _scatter_rows(v.reshape(n, d // 2), idx, n_slots)
    if side is None:
        xb = dispatch(dest)
    else:
        xb, side_a = _alongside(dispatch, dest, *side[0])
    yb = _experts(xb, block_e, n_used, layer, w_gate, w_up, w_down)
    combine = lambda idx: _gather_rows(yb, idx)
    dest_flat = dest.reshape(-1)
    if side is None:
        y, side_b = combine(dest_flat), None
    else:
        y, side_b = _alongside(combine, dest_flat, side[1], side_a)
    out = _ffn_post(h, y.reshape(TOP_K, bsz, t, d // 2), rwt, gate, n_ctx, final_g)
    return out if side is None else (out, side_b)


DFT_STEP = 16


def _dft_tables(n):
    r, *mats = _dft_tables_np(n)
    return (r,) + tuple(jnp.asarray(a).astype(BF16) for a in mats)


@functools.lru_cache(maxsize=None)
def _dft_tables_np(n):
    size = 2 * n
    r = int(round(math.sqrt(size)))
    assert r * r == size and r % DFT_STEP == 0
    p1 = np.arange(r // 2)[None, None, :]
    p2 = np.arange(r)[:, None, None]
    k1 = np.arange(r)[None, :, None]
    ang = 2.0 * np.pi * (((r * p1 + p2) * k1) % size) / size
    g_re, g_im = np.cos(ang), -np.sin(ang)
    g_in = np.concatenate([g_re, g_im], axis=1)
    g_out = np.concatenate([np.swapaxes(g_re, 1, 2), np.swapaxes(g_im, 1, 2)], axis=2) / size
    a2 = 2.0 * np.pi * ((np.arange(r)[:, None] * np.arange(r)[None, :]) % r) / r
    f_re, f_im = np.cos(a2), -np.sin(a2)
    f_fwd = np.block([[f_re, -f_im], [f_im, f_re]])
    f_inv = np.block([[f_re, f_im], [-f_im, f_re]])
    p1f = np.arange(r)[None, None, :]
    angf = 2.0 * np.pi * (((r * p1f + p2) * k1) % size) / size
    g_full = np.concatenate([np.cos(angf), -np.sin(angf)], axis=1)
    return (r,) + tuple(a.astype(np.float32) for a in (g_in, g_out, f_fwd, f_inv, g_full))


def _dot_f32_tn(a, b):
    ah = a.astype(BF16)
    al = (a - ah.astype(F32)).astype(BF16)
    bh = b.astype(BF16)
    bl = (b - bh.astype(F32)).astype(BF16)
    return _dot_tn(ah, bh) + (_dot_tn(ah, bl) + _dot_tn(al, bh))


def _hy_filter_kernel(z_ref, t_ref, w1_ref, b1_ref, w2_ref, b2_ref, fr_ref, w3_ref, w3b_ref, rates_ref, o_ref,
                      *, half_tiles):
    i = pl.program_id(0)
    hid = jnp.sin(fr_ref[...] * (_dot_f32(w1_ref[...], z_ref[...]) + b1_ref[...]))
    hid = jnp.sin(fr_ref[...] * (_dot_f32(w2_ref[...], hid) + b2_ref[...]))
    filt = _dot_f32_tn(hid, w3_ref[...])
    decay = jnp.exp(-t_ref[...] * rates_ref[...])
    for o in range(o_ref.shape[0]):
        o_ref[o] = filt[:, o * HY_CH:(o + 1) * HY_CH] * decay

    @pl.when(i == 0)
    def _():
        extra = _dot_f32_tn(hid[:, 0:LANES], w3b_ref[...])[0:8]
        first = lax.broadcasted_iota(jnp.int32, (8, HY_CH), 0) == 0
        for o in range(o_ref.shape[0]):
            add = extra[:, o * HY_CH:(o + 1) * HY_CH] * decay[0:8]
            o_ref[o, 0:8, :] = o_ref[o, 0:8, :] + jnp.where(first, add, 0.0)

    @pl.when(i == half_tiles)
    def _():
        for o in range(o_ref.shape[0]):
            o_ref[o, 0:1, :] = jnp.zeros((1, HY_CH), F32)


HY_TILE = 512


def _hy_kernels(n, w1, b1, w2, b2, w3, freq):
    pos = np.arange(2 * n)
    pos = np.where(pos < n, pos, 2 * n - pos).astype(np.float32)
    t = jnp.asarray(pos / np.float32(n - 1))
    bands = jnp.linspace(1e-4, HY_BANDS - 1, HY_BANDS, dtype=F32)
    ang = (2.0 * math.pi / n) * bands[:, None] * jnp.asarray(pos)[None, :]
    z = jnp.concatenate([t[None, :], jnp.cos(ang), -jnp.sin(ang)], axis=0)
    z = jnp.pad(z, ((0, LANES - z.shape[0]), (0, 0)))
    w1t = jnp.pad(w1, ((0, LANES - w1.shape[0]), (0, 0))).T
    hidden = w1.shape[1]
    col = lambda v: v.reshape(hidden, 1)
    w3d = jnp.swapaxes(w3.reshape(hidden, HY_ORDER, 2, HY_CH), 0, 2)
    w3d = jnp.swapaxes(w3d, 1, 2).reshape(2, hidden, HY_ORDER * HY_CH)
    rates = jnp.abs(jnp.linspace(HY_MIN_DECAY, HY_MAX_DECAY, HY_CH, dtype=F32)).reshape(1, HY_CH)
    tm = HY_TILE
    half_tiles = n // tm
    full = lambda a: pl.BlockSpec(a.shape, lambda i: (0,) * a.ndim)
    small = (w1t, col(b1), w2.T, col(b2), col(freq))
    return pl.pallas_call(
        functools.partial(_hy_filter_kernel, half_tiles=half_tiles),
        grid=(2 * n // tm,),
        in_specs=[pl.BlockSpec((LANES, tm), lambda i: (0, i)), pl.BlockSpec((tm, 1), lambda i: (i, 0))]
                 + [full(a) for a in small]
                 + [pl.BlockSpec((None, hidden, HY_ORDER * HY_CH), lambda i: ((i >= half_tiles).astype(jnp.int32), 0, 0)),
                    pl.BlockSpec((None, hidden, HY_ORDER * HY_CH), lambda i: (1, 0, 0)), full(rates)],
        out_specs=pl.BlockSpec((HY_ORDER, tm, HY_CH), lambda i: (0, i, 0)),
        out_shape=jax.ShapeDtypeStruct((HY_ORDER, 2 * n, HY_CH), F32),
        compiler_params=_params("parallel"),
        name="hy_kernels",
    )(z, t[:, None], *small, w3d, w3d, rates)


def _pack_complex(z):
    r = z.shape[0] // 2
    bits = lax.bitcast_convert_type(z.astype(BF16).astype(F32), jnp.uint32)
    return lax.bitcast_convert_type(bits[0:r] | (bits[r:2 * r] >> 16), F32)


def _unpack_complex(words):
    p = lax.bitcast_convert_type(words, jnp.uint32)
    re = lax.bitcast_convert_type(p & jnp.uint32(0xFFFF0000), F32)
    im = lax.bitcast_convert_type(p << 16, F32)
    return jnp.concatenate([re, im], axis=0).astype(BF16)


def _load_every(ref, j, count):
    return ref.reshape(count * DFT_STEP, LANES)[pl.ds(j, count, stride=DFT_STEP), :]


def _store_every(ref, j, count, val):
    ref.reshape(count * DFT_STEP, LANES)[pl.ds(j, count, stride=DFT_STEP), :] = val


def _dft_in_kernel(x_ref, g_ref, a_ref):
    n_seq, rh = x_ref.shape[0:2]
    for j in range(DFT_STEP):
        x = jnp.concatenate([_load_every(x_ref.at[b], j, rh) for b in range(n_seq)], axis=1).astype(BF16)
        res = _dot(g_ref[j], x)
        for b in range(n_seq):
            a_ref[b, j] = _pack_complex(res[:, b * LANES:(b + 1) * LANES])


def _dft_in(x4, col, g_in):
    bx, rh, r, _ = x4.shape
    c = HY_CH
    cbs = c // LANES
    return pl.pallas_call(
        _dft_in_kernel,
        grid=(r // DFT_STEP, cbs),
        in_specs=[pl.BlockSpec((bx, rh, DFT_STEP, LANES), lambda i, cb: (0, 0, i, col * cbs + cb)),
                  pl.BlockSpec((DFT_STEP, 2 * r, rh), lambda i, cb: (i, 0, 0))],
        out_specs=pl.BlockSpec((bx, DFT_STEP, r, LANES), lambda i, cb: (0, i, 0, cb)),
        out_shape=jax.ShapeDtypeStruct((bx, r, r, c), F32),
        compiler_params=_params("parallel", "parallel"),
        name="dft_in",
    )(x4, g_in)


def _stage2_operand(a_ref, j, r):
    return jnp.concatenate([_unpack_complex(_load_every(a_ref.at[b], j, r)) for b in range(a_ref.shape[0])],
                           axis=1)


def _dft_filt_kernel(a_ref, f_ref, k_ref):
    r = f_ref.shape[0] // 2
    for j in range(DFT_STEP):
        s = _dot(f_ref[...], _stage2_operand(a_ref, j, r))
        for o in range(a_ref.shape[0]):
            k_ref[o, j] = s[:, o * LANES:(o + 1) * LANES]


def _dft_filt(a, f_fwd):
    nq, r, _, c = a.shape
    return pl.pallas_call(
        _dft_filt_kernel,
        grid=(r // DFT_STEP, c // LANES),
        in_specs=[pl.BlockSpec((nq, r, DFT_STEP, LANES), lambda i, cb: (0, 0, i, cb)),
                  pl.BlockSpec(f_fwd.shape, lambda i, cb: (0, 0))],
        out_specs=pl.BlockSpec((nq, DFT_STEP, 2 * r, LANES), lambda i, cb: (0, i, 0, cb)),
        out_shape=jax.ShapeDtypeStruct((nq, r, 2 * r, c), F32),
        compiler_params=_params("parallel", "parallel"),
        name="dft_filt",
    )(a, f_fwd)


def _dft_mid_kernel(a_ref, k_ref, ff_ref, fi_ref, b_ref):
    r = ff_ref.shape[0] // 2
    n_seq = a_ref.shape[0]
    for j in range(DFT_STEP):
        s = _dot(ff_ref[...], _stage2_operand(a_ref, j, r))
        sr, si = s[0:r], s[r:2 * r]
        kr = jnp.concatenate([k_ref[j, 0:r, :]] * n_seq, axis=1)
        ki = jnp.concatenate([k_ref[j, r:2 * r, :]] * n_seq, axis=1)
        p = jnp.concatenate([sr * kr - si * ki, sr * ki + si * kr], axis=0).astype(BF16)
        back = _dot(fi_ref[...], p)
        for b in range(n_seq):
            b_ref[b, j] = _pack_complex(back[:, b * LANES:(b + 1) * LANES])


def _dft_mid(a, kspec, order, f_fwd, f_inv):
    bsz, r, _, c = a.shape
    return pl.pallas_call(
        _dft_mid_kernel,
        grid=(r // DFT_STEP, c // LANES),
        in_specs=[pl.BlockSpec((bsz, r, DFT_STEP, LANES), lambda i, cb: (0, 0, i, cb)),
                  pl.BlockSpec((None, DFT_STEP, 2 * r, LANES), lambda i, cb: (order, i, 0, cb)),
                  pl.BlockSpec(f_fwd.shape, lambda i, cb: (0, 0)),
                  pl.BlockSpec(f_inv.shape, lambda i, cb: (0, 0))],
        out_specs=pl.BlockSpec((bsz, DFT_STEP, r, LANES), lambda i, cb: (0, i, 0, cb)),
        out_shape=jax.ShapeDtypeStruct((bsz, r, r, c), F32),
        compiler_params=_params("parallel", "parallel"),
        name="dft_mid",
    )(a, kspec, f_fwd, f_inv)


def _dft_out_kernel(b_ref, g_ref, u_ref, x_ref, bias_ref, o_ref):
    n_seq, r = b_ref.shape[0:2]
    rh = o_ref.shape[1]
    for j in range(DFT_STEP):
        rhs = jnp.concatenate([_unpack_complex(_load_every(b_ref.at[b], j, r)) for b in range(n_seq)], axis=1)
        y = _dot(g_ref[j], rhs)
        for b in range(n_seq):
            yb = y[:, b * LANES:(b + 1) * LANES]
            _store_every(o_ref.at[b], j, rh,
                         _load_every(x_ref.at[b], j, rh) * (yb + _load_every(u_ref.at[b], j, rh) * bias_ref[...]))


def _dft_out(bm, g_out, u4, u_col, x4, x_col, bias):
    bsz, r, _, c = bm.shape
    rh = r // 2
    cbs = c // LANES
    seq = lambda col: pl.BlockSpec((bsz, rh, DFT_STEP, LANES), lambda i, cb: (0, 0, i, col * cbs + cb))
    return pl.pallas_call(
        _dft_out_kernel,
        grid=(r // DFT_STEP, cbs),
        in_specs=[pl.BlockSpec((bsz, r, DFT_STEP, LANES), lambda i, cb: (0, 0, i, cb)),
                  pl.BlockSpec((DFT_STEP, rh, 2 * r), lambda i, cb: (i, 0, 0)),
                  seq(u_col), seq(x_col), pl.BlockSpec((1, LANES), lambda i, cb: (0, cb))],
        out_specs=seq(0),
        out_shape=jax.ShapeDtypeStruct((bsz, rh, r, c), F32),
        compiler_params=_params("parallel", "parallel"),
        name="dft_out",
    )(bm, g_out, u4, x4, bias.reshape(1, c))


def _hyena_filter_stage1(n, filter_params):
    r, g_full = _dft_tables(n)[0], _dft_tables(n)[5]
    kern = _hy_kernels(n, *filter_params)
    return _dft_in(kern.reshape(-1, r, r, HY_CH), 0, g_full)


def _hyena_filter_spectra(n, stage1):
    return _dft_filt(stage1, _dft_tables(n)[3])


def _hyena(hy_in, kspec, conv_bias):
    bsz, n, _ = hy_in.shape
    r, g_in, g_out, f_fwd, f_inv, _ = _dft_tables(n)
    seq4 = hy_in.reshape(bsz, r // 2, r, 3 * HY_CH)
    zz = _dft_out(_dft_mid(_dft_in(seq4, 0, g_in), kspec, 0, f_fwd, f_inv), g_out,
                  seq4, 0, seq4, 1, conv_bias[0])
    out = _dft_out(_dft_mid(_dft_in(zz, 0, g_in), kspec, 1, f_fwd, f_inv), g_out,
                   zz, 0, seq4, 2, conv_bias[1])
    return out.reshape(bsz, n, HY_CH)


def _reorder_ab(w):
    offs = np.cumsum([0, GLA_KEY_W, GLA_KEY_W, GLA_VAL_W, GLA_VAL_W, GLA_LOW_RANK, GLA_LOW_RANK,
                      HG_KEY_W, HG_KEY_W, HG_KEY_W, HG_VAL_W, HG_VAL_W]).tolist()
    gq, gk, gv, gg, lr_f, lr_b, hq, hf_f, hf_b, hi, hg = range(11)
    cols = lambda first, last: w[:, offs[first]:offs[last + 1]].astype(BF16)
    lr = jnp.pad(cols(lr_f, lr_b), ((0, 0), (0, AB_PAD_COLS - AB_LR - 2 * GLA_LOW_RANK)))
    return [cols(gq, gk), cols(gg, gg), cols(hq, hf_b), cols(hg, hg), lr], [cols(gv, gv), cols(hi, hi)]


def _reorder_cd(w):
    hy_end = 3 * HY_CH
    z_end = hy_end + MB_INNER
    xbc_end = z_end + MB_INNER + 2 * MB_BC_W
    pad = lambda a: jnp.pad(a, ((0, 0), (0, LANES - MB_HEADS)))
    dt = jnp.concatenate([pad(w[:, xbc_end:xbc_end + MB_HEADS]), pad(w[:, xbc_end + MB_HEADS:])], axis=-1)
    return [w[:, :hy_end].astype(BF16), w[:, z_end:xbc_end].astype(BF16), w[:, hy_end:z_end].astype(BF16),
            dt.astype(BF16)]


def kernel(x, c, ctx, c_ctx, ada_w, ada_b, norm_mix_g, norm_ffn_g, norm_out_g, ab_w_in, ab_w_out, gla_gate_w, gla_gate_b, gla_norm_g, hg_lb, hg_norm_g, cd_w_in, cd_w_out, hy_short_w, hy_short_b, hy_w1, hy_b1, hy_w2, hy_b2, hy_w3, hy_freq, hy_bias, mb_conv_w, mb_conv_b, mb_dt_bias, mb_a_log, mb_d, mb_norm_g, router_w, router_b, moe_w_gate, moe_w_up, moe_w_down):
    bsz, n_lat, d = x.shape
    n_ctx = ctx.shape[1]
    assert ada_w.shape[0] == 2 and ab_w_in.shape[0] == 1 and cd_w_in.shape[0] == 1

    cond = jnp.zeros((ADALN_ROWS, d), F32).at[:bsz].set(c).at[bsz].set(c_ctx)
    m = _adaln(cond, ada_w, ada_b)

    def mods(layer):
        lat = m[layer, :bsz].reshape(bsz, 6, d)
        cx = jnp.broadcast_to(m[layer, bsz].reshape(1, 6, d), (bsz, 6, d))
        both = jnp.stack([cx, lat], axis=1)
        return [both[:, :, j][:, :, None, :] for j in range(6)]

    router_w_pad = jnp.zeros((d, LANES), F32).at[:, :N_EXPERTS].set(router_w)

    sh_m, sc_m, gt_m, sh_f, sc_f, gt_f = mods(0)
    proj, vals = _norm_proj(ctx, x, norm_mix_g[0], sh_m, sc_m, *_reorder_ab(ab_w_in[0]))
    gwp = [jnp.zeros((LANES, GLA_KEY_W), F32).at[GLA_LOW_RANK * dd:GLA_LOW_RANK * (dd + 1)].set(gla_gate_w[0, dd])
           for dd in range(2)]
    o_gla = _gla_scan(proj, vals, gwp, [gla_gate_b[0, dd].reshape(1, -1) for dd in range(2)], n_ctx)
    o_hg = _hgrn_scan(proj, vals, [hg_lb[dd].astype(F32) for dd in range(2)], 0, n_ctx)
    h, pre_out = _mix_out_ab(ctx, x, o_gla, o_hg, proj, gla_norm_g[0], hg_norm_g[0], ab_w_out[0].astype(BF16),
                             gt_m, (norm_ffn_g[0], sh_f, sc_f, router_w_pad, router_b))
    filter_params = (hy_w1[0], hy_b1[0], hy_w2[0], hy_b2[0], hy_w3[0], hy_freq[0])
    side = ((functools.partial(_hyena_filter_stage1, n_lat), filter_params),
            functools.partial(_hyena_filter_spectra, n_lat))
    h, kspec = _moe(h, pre_out, gt_f, 0, moe_w_gate, moe_w_up, moe_w_down, n_ctx, side=side)

    sh_m, sc_m, gt_m, sh_f, sc_f, gt_f = mods(1)
    conv_w = jnp.concatenate([hy_short_w[0], mb_conv_w[0]], axis=0).T
    conv_b = jnp.concatenate([hy_short_b[0], mb_conv_b[0]]).reshape(1, -1)
    hy_in, xbc, zdt = _norm_proj_conv(h, norm_mix_g[1], sh_m, sc_m, _reorder_cd(cd_w_in[0]),
                                      conv_w, conv_b, 3 * HY_CH, n_ctx)
    hy = _hyena(hy_in, kspec, hy_bias[0])
    y_ssd = _ssd_scan(xbc, zdt, MB_INNER // LANES, mb_dt_bias[0], mb_a_log[0], n_ctx)
    d_skip_x = jnp.repeat(mb_d[0], MB_HEAD_DIM).reshape(1, MB_INNER)
    h, pre_out = _mix_out_cd(h, hy, y_ssd, xbc, zdt, 0, d_skip_x, mb_norm_g[0], cd_w_out[0].astype(BF16), gt_m,
                             n_ctx, (norm_ffn_g[1], sh_f, sc_f, router_w_pad, router_b))
    return _moe(h, pre_out, gt_f, 1, moe_w_gate, moe_w_up, moe_w_down, 0, final_g=norm_out_g)
```

```python
import functools
import math

import numpy as np
import jax
import jax.numpy as jnp
from jax import lax
from jax.experimental import pallas as pl
from jax.experimental.pallas import tpu as pltpu
from jax.experimental.pallas import tpu_sc as plsc

NORM_EPS = 1e-6
GLA_HEADS, GLA_DK, GLA_DV, GLA_LOW_RANK, GLA_TAU = 4, 64, 128, 16, 16.0
GLA_KEY_W, GLA_VAL_W = GLA_HEADS * GLA_DK, GLA_HEADS * GLA_DV
HG_HEADS, HG_EXPAND, HG_DV = 4, 128, 128
HG_KEY_W, HG_VAL_W = HG_HEADS * HG_EXPAND, HG_HEADS * HG_DV
HY_CH, HY_ORDER, HY_BANDS = 512, 2, 16
HY_MIN_DECAY = math.log(1e-2) / 1.5
HY_MAX_DECAY = math.log(1e-2) / 0.3
MB_HEADS, MB_HEAD_DIM, MB_GROUPS, MB_STATE = 8, 64, 2, 128
MB_INNER = MB_HEADS * MB_HEAD_DIM
MB_BC_W = MB_GROUPS * MB_STATE
N_EXPERTS, N_GROUPS, TOP_K = 16, 4, 2
MOE_BLOCK = 512
EXPERTS_PER_GROUP = N_EXPERTS // N_GROUPS

LANES = 128
SCAN_CHUNK = 64
SCAN_BLOCK = 256
SSD_CHUNK = 128
SSD_BLOCK = 256
ROW_TILE = 256
VMEM_LIMIT = 56 * 1024 * 1024

BF16 = jnp.bfloat16
F32 = jnp.float32


def _params(*sem):
    return pltpu.CompilerParams(dimension_semantics=sem, vmem_limit_bytes=VMEM_LIMIT)


def _split3(x):
    hi = x.astype(BF16)
    r1 = x - hi.astype(F32)
    mid = r1.astype(BF16)
    lo = (r1 - mid.astype(F32)).astype(BF16)
    return hi, mid, lo


def _dot(a, b):
    return jnp.dot(a, b, preferred_element_type=F32)


def _dot_nt(a, b):
    return lax.dot_general(a, b, (((1,), (1,)), ((), ())), preferred_element_type=F32)


def _dot_tn(a, b):
    return lax.dot_general(a, b, (((0,), (0,)), ((), ())), preferred_element_type=F32)


def _dot_f32(a, b):
    ah = a.astype(BF16)
    al = (a - ah.astype(F32)).astype(BF16)
    bh = b.astype(BF16)
    bl = (b - bh.astype(F32)).astype(BF16)
    return _dot(ah, bh) + (_dot(ah, bl) + _dot(al, bh))


def _silu(x):
    return x * (1.0 / (1.0 + jnp.exp(-x)))


def _sigmoid(x):
    return 1.0 / (1.0 + jnp.exp(-x))


def _softplus(x):
    return jnp.maximum(x, 0.0) + jnp.log(1.0 + jnp.exp(-jnp.abs(x)))


def _pack_bf16_pairs(x):
    bits = lax.bitcast_convert_type(x.astype(BF16).astype(F32), jnp.uint32)
    half = x.shape[1] // 2
    return bits[:, :half] | (bits[:, half:] >> 16)


def _unpack_bf16_pairs(p):
    hi = lax.bitcast_convert_type(p & jnp.uint32(0xFFFF0000), F32)
    lo = lax.bitcast_convert_type(p << 16, F32)
    return jnp.concatenate([hi, lo], axis=1)


def _rms(x, g):
    return x * lax.rsqrt(jnp.mean(x * x, axis=-1, keepdims=True) + NORM_EPS) * g


def _adaln_kernel(c_ref, w_ref, b_ref, o_ref):
    o_ref[...] = _dot_f32(_silu(c_ref[...]), w_ref[...]) + b_ref[...]


ADALN_ROWS = 8
ADALN_TILE = 1536


def _adaln(cond, w, b):
    n_l, d, n6 = w.shape
    tn = ADALN_TILE
    rows = cond.shape[0]
    return pl.pallas_call(
        _adaln_kernel,
        grid=(n_l, n6 // tn),
        in_specs=[pl.BlockSpec((rows, d), lambda l, j: (0, 0)),
                  pl.BlockSpec((None, d, tn), lambda l, j: (l, 0, j)),
                  pl.BlockSpec((None, 1, tn), lambda l, j: (l, 0, j))],
        out_specs=pl.BlockSpec((None, rows, tn), lambda l, j: (l, 0, j)),
        out_shape=jax.ShapeDtypeStruct((n_l, rows, n6), F32),
        compiler_params=_params("parallel", "parallel"),
        name="adaln",
    )(cond, w, b.reshape(n_l, 1, n6))


def _project(u, w_refs):
    return jnp.concatenate([_dot(u, w_ref[...]) for w_ref in w_refs], axis=-1)


def _joint_rows_specs(ctx, x):
    tm = ROW_TILE
    ctx_tiles = ctx.shape[1] // tm
    d = ctx.shape[2]
    return [pl.BlockSpec((None, tm, d), lambda b, i: (b, jnp.minimum(i, ctx_tiles - 1), 0)),
            pl.BlockSpec((None, tm, d), lambda b, i: (b, jnp.maximum(i - ctx_tiles, 0), 0))], ctx_tiles


def _joint_rows(ctx_ref, x_ref, ctx_tiles):
    return jnp.where(pl.program_id(1) < ctx_tiles, ctx_ref[...], x_ref[...])


def _norm_proj_kernel(c_ref, x_ref, g_ref, sh_ref, sc_ref, *refs, ctx_tiles, n_main):
    u = _rms(_joint_rows(c_ref, x_ref, ctx_tiles), g_ref[...]) * (1.0 + sc_ref[...]) + sh_ref[...]
    u = u.astype(BF16)
    refs[-2][...] = _project(u, refs[:n_main])
    refs[-1][...] = _project(u, refs[n_main:-2]).astype(BF16)


def _norm_proj(ctx, x, g, shift, scale, ws, ws_bf16):
    bsz, n_ctx, d = ctx.shape
    t = n_ctx + x.shape[1]
    n, n_bf = sum(w.shape[1] for w in ws), sum(w.shape[1] for w in ws_bf16)
    tm = ROW_TILE
    seg = lambda b, i: (b, (i * tm >= n_ctx).astype(jnp.int32), 0, 0)
    row_specs, ctx_tiles = _joint_rows_specs(ctx, x)
    return pl.pallas_call(
        functools.partial(_norm_proj_kernel, ctx_tiles=ctx_tiles, n_main=len(ws)),
        grid=(bsz, t // tm),
        in_specs=row_specs + [pl.BlockSpec((1, d), lambda b, i: (0, 0)),
                              pl.BlockSpec((None, None, 1, d), seg),
                              pl.BlockSpec((None, None, 1, d), seg)]
                 + [pl.BlockSpec(w.shape, lambda b, i: (0, 0)) for w in list(ws) + list(ws_bf16)],
        out_specs=[pl.BlockSpec((None, tm, n), lambda b, i: (b, i, 0)),
                   pl.BlockSpec((None, tm, n_bf), lambda b, i: (b, i, 0))],
        out_shape=[jax.ShapeDtypeStruct((bsz, t, n), F32), jax.ShapeDtypeStruct((bsz, t, n_bf), BF16)],
        compiler_params=_params("parallel", "parallel"),
        name="norm_proj",
    )(ctx, x, g.reshape(1, d), shift, scale, *ws, *ws_bf16)


def _scan_constants(c, reverse):
    t = np.arange(c)[:, None]
    u = np.arange(c)[None, :]
    sels = [u <= t, u > t]
    masks = []
    m = c // 2
    while m >= 1:
        blk = t // (2 * m)
        upper_t = (t % (2 * m)) >= m
        r = blk * (2 * m) + m - 1
        s_blk = u // (2 * m)
        upper_s = (u % (2 * m)) >= m
        sels.append((upper_t & (u > r) & (u <= t)) | ((~upper_t) & (u > t) & (u <= r)))
        masks.append((blk == s_blk) & upper_t & (~upper_s))
        m //= 2
    masks.append(t == u)
    sel = np.stack(sels).astype(np.float32)
    msk = np.stack(masks).astype(np.float32)
    if reverse:
        sel = sel[:, ::-1, ::-1]
        msk = msk[:, ::-1, ::-1]
    return np.ascontiguousarray(sel.reshape(-1, c)), np.ascontiguousarray(msk)


def _chunk_order(i, n_ctx_chunks, n_chunks, reverse):
    if not reverse:
        return i
    return jnp.where(i < n_ctx_chunks, n_ctx_chunks - 1 - i, n_chunks - 1 - (i - n_ctx_chunks))


GROUP_KEYS = 256


def _decay_chunk(q, k, v, la, consts, st_ref, heads, dk, dv):
    sel_ref, mask_ref, hm_ref, hmb_ref, vm_ref = consts
    c = q.shape[0]
    n_lvl = mask_ref.shape[0] - 1
    hpg = GROUP_KEYS // dk
    cs = _dot(sel_ref[...], jnp.concatenate(_split3(la), axis=0))
    e_q = jnp.exp(cs[0:c])
    e_k = jnp.exp(cs[c:2 * c])
    e_tot = jnp.exp(jnp.sum(la, axis=0, keepdims=True))
    vb = v.astype(BF16)
    outs = []
    for g in range(heads // hpg):
        ks = slice(g * GROUP_KEYS, (g + 1) * GROUP_KEYS)
        vs = slice(g * hpg * dv, (g + 1) * hpg * dv)
        qg, kg = q[:, ks], k[:, ks]
        key_stack = lambda x: jnp.concatenate([x.astype(BF16) * hmb_ref[h] for h in range(hpg)], axis=0)
        att = mask_ref[n_lvl] * _dot_nt(qg.astype(BF16), key_stack(kg))
        for l in range(n_lvl):
            e = jnp.exp(cs[(2 + l) * c:(3 + l) * c, ks])
            att = att + mask_ref[l] * _dot_nt((qg * e).astype(BF16), key_stack(kg * e))
        v_blocks = jnp.concatenate([vb[:, vs] * vm_ref[h] for h in range(hpg)], axis=0)
        intra = _dot(att.astype(BF16), v_blocks)
        st = st_ref[g]
        q_stack = jnp.concatenate([(qg * e_q[:, ks]) * hm_ref[h] for h in range(hpg)], axis=0)
        inter = _dot_nt(q_stack.astype(BF16), st.astype(BF16))
        upd = _dot_tn(vb[:, vs], (kg * e_k[:, ks]).astype(BF16))
        new = st * e_tot[:, ks]
        for h in range(hpg):
            new = new + upd[h * dv:(h + 1) * dv] * hm_ref[h]
        st_ref[g] = new
        outs.append(intra + jnp.concatenate([inter[h * c:(h + 1) * c] for h in range(hpg)], axis=-1))
    return jnp.concatenate(outs, axis=-1)


def _log_sigmoid(x):
    return jnp.minimum(x, 0.0) - jnp.log(1.0 + jnp.exp(-jnp.abs(x)))


def _gla_kernel(*refs):
    ins, head_masks, (o_refs, st_refs) = (refs[0:8], refs[8:16]), refs[16:19], (refs[19:21], refs[21:23])

    @pl.when(pl.program_id(1) == 0)
    def _():
        for st_ref in st_refs:
            st_ref[...] = jnp.zeros_like(st_ref)

    for d, ((q_ref, k_ref, v_ref, lr_ref, gw_ref, gb_ref, sel_ref, mask_ref), o_ref, st_ref) in enumerate(
            zip(ins, o_refs, st_refs)):
        z = _dot_f32(lr_ref[...], gw_ref[...]) + gb_ref[...]
        la = _log_sigmoid(z) * (1.0 / GLA_TAU)
        q = q_ref[...] * (GLA_DK ** -0.5)
        k, v = k_ref[...], v_ref[...]
        for rows in _sub_chunks(q.shape[0], d == 1):
            o_ref[rows, :] = _decay_chunk(q[rows], k[rows], v[rows], la[rows], (sel_ref, mask_ref) + head_masks,
                                          st_ref, GLA_HEADS, GLA_DK, GLA_DV).astype(o_ref.dtype)


def _hgrn_kernel(*refs, layer):
    ins, head_masks, (o_refs, st_refs) = (refs[0:6], refs[6:12]), refs[12:15], (refs[15:17], refs[17:19])

    @pl.when(pl.program_id(1) == 0)
    def _():
        for st_ref in st_refs:
            st_ref[...] = jnp.zeros_like(st_ref)

    for d, ((q_ref, f_ref, v_ref, lb_ref, sel_ref, mask_ref), o_ref, st_ref) in enumerate(
            zip(ins, o_refs, st_refs)):
        e = jnp.exp(lb_ref[...] - jnp.max(lb_ref[...], axis=0, keepdims=True))
        lb = jnp.sum(e[0:layer + 1], axis=0, keepdims=True) / jnp.sum(e, axis=0, keepdims=True)
        f = lb + (1.0 - lb) * _sigmoid(f_ref[...])
        q, k, v, la = _silu(q_ref[...]), 1.0 - f, v_ref[...], jnp.log(f)
        for rows in _sub_chunks(q.shape[0], d == 1):
            o_ref[rows, :] = _decay_chunk(q[rows], k[rows], v[rows], la[rows], (sel_ref, mask_ref) + head_masks,
                                          st_ref, HG_HEADS, HG_EXPAND, HG_DV).astype(o_ref.dtype)


def _sub_chunks(rows, reverse, chunk=SCAN_CHUNK):
    order = range(rows // chunk)
    return [slice(j * chunk, (j + 1) * chunk) for j in (reversed(order) if reverse else order)]


def _scan_specs(blk, n_ctx, t, reverse, chunk=None, stacked_heads=1):
    n_blocks = t // blk
    order = functools.partial(_chunk_order, n_ctx_chunks=n_ctx // blk, n_chunks=n_blocks, reverse=reverse)

    def col(width, idx):
        return pl.BlockSpec((None, blk, width), lambda b, i: (b, order(i), idx))

    sel, msk = _scan_constants(chunk or blk, reverse)
    sel3 = np.concatenate([sel, sel, sel], axis=1)
    msk = np.tile(msk, (1, 1, stacked_heads))
    const = lambda a: pl.BlockSpec(a.shape, lambda b, i: (0,) * a.ndim)
    return n_blocks, col, const, jnp.asarray(sel3, BF16), jnp.asarray(msk, F32)


def _head_masks(dk, dv):
    hpg = GROUP_KEYS // dk
    hm = np.zeros((hpg, 1, GROUP_KEYS), np.float32)
    vm = np.zeros((hpg, 1, hpg * dv), np.float32)
    for h in range(hpg):
        hm[h, 0, h * dk:(h + 1) * dk] = 1.0
        vm[h, 0, h * dv:(h + 1) * dv] = 1.0
    return jnp.asarray(hm), jnp.asarray(hm, BF16), jnp.asarray(vm, BF16)


AB_Q, AB_K, AB_G = 0, 256, 512
AB_HQ, AB_HF, AB_HG, AB_LR = 1024, 1536, 2560, 3072
AB_PAD_COLS = 3200
AB_V, AB_HI = 0, 1


def _gla_scan(proj, vals, gate_w_pad, gate_b, n_ctx):
    bsz, t, _ = proj.shape
    hpg = GROUP_KEYS // GLA_DK
    in_specs, args, outs = [], [], []
    for d in range(2):
        n_blocks, col, const, sel, msk = _scan_specs(SCAN_BLOCK, n_ctx, t, d == 1, SCAN_CHUNK, hpg)
        in_specs += [col(GLA_KEY_W, AB_Q // GLA_KEY_W), col(GLA_KEY_W, AB_K // GLA_KEY_W),
                     col(GLA_VAL_W, AB_V), col(LANES, AB_LR // LANES),
                     const(gate_w_pad[d]), const(gate_b[d]), const(sel), const(msk)]
        args += [proj, proj, vals, proj, gate_w_pad[d], gate_b[d], sel, msk]
        outs.append(col(GLA_VAL_W, 0))
    hm = _head_masks(GLA_DK, GLA_DV)
    return pl.pallas_call(
        _gla_kernel,
        grid=(bsz, n_blocks),
        in_specs=in_specs + [const(m) for m in hm],
        out_specs=outs,
        out_shape=[jax.ShapeDtypeStruct((bsz, t, GLA_VAL_W), BF16)] * 2,
        scratch_shapes=[pltpu.VMEM((GLA_HEADS // hpg, GLA_DV, GROUP_KEYS), F32)] * 2,
        compiler_params=_params("parallel", "arbitrary"),
        name="gla_scan",
    )(*args, *hm)


def _hgrn_scan(proj, vals, lb, layer, n_ctx):
    bsz, t, _ = proj.shape
    hpg = GROUP_KEYS // HG_EXPAND
    in_specs, args, outs = [], [], []
    for d in range(2):
        n_blocks, col, const, sel, msk = _scan_specs(SCAN_BLOCK, n_ctx, t, d == 1, SCAN_CHUNK, hpg)
        in_specs += [col(HG_KEY_W, AB_HQ // HG_KEY_W), col(HG_KEY_W, AB_HF // HG_KEY_W + d),
                     col(HG_VAL_W, AB_HI), const(lb[d]), const(sel), const(msk)]
        args += [proj, proj, vals, lb[d], sel, msk]
        outs.append(col(HG_VAL_W, 0))
    hm = _head_masks(HG_EXPAND, HG_DV)
    return pl.pallas_call(
        functools.partial(_hgrn_kernel, layer=layer),
        grid=(bsz, n_blocks),
        in_specs=in_specs + [const(m) for m in hm],
        out_specs=outs,
        out_shape=[jax.ShapeDtypeStruct((bsz, t, HG_VAL_W), BF16)] * 2,
        scratch_shapes=[pltpu.VMEM((HG_HEADS // hpg, HG_DV, GROUP_KEYS), F32)] * 2,
        compiler_params=_params("parallel", "arbitrary"),
        name="hgrn_scan",
    )(*args, *hm)


def _mix_out_ab_kernel(*refs, ctx_tiles):
    (c_ref, x_ref, gf_ref, gb_ref, hf_ref, hb_ref, gg_ref, hg_ref, gn_ref, hn_ref, w_ref, gt_ref) = refs[0:12]
    pre_in, o_ref, pre_out = refs[12:12 + N_FFN_PRE_IN], refs[12 + N_FFN_PRE_IN], refs[13 + N_FFN_PRE_IN:]
    feats = []
    both = lambda fwd_ref, bwd_ref: fwd_ref[...].astype(F32) + bwd_ref[...].astype(F32)
    for o, gate, g in ((both(gf_ref, gb_ref), gg_ref[...], gn_ref[...]),
                       (both(hf_ref, hb_ref), hg_ref[...], hn_ref[...])):
        for hd in range(o.shape[-1] // LANES):
            s = slice(hd * LANES, (hd + 1) * LANES)
            feats.append(_rms(o[:, s], g) * _silu(gate[:, s]))
    feat = jnp.concatenate(feats, axis=-1).astype(BF16)
    h_new = _joint_rows(c_ref, x_ref, ctx_tiles) + gt_ref[...] * _dot(feat, w_ref[...])
    o_ref[...] = h_new
    _ffn_pre_body(h_new, *pre_in, *pre_out)


def _mix_out_ab(ctx, x, o_gla, o_hg, proj, gla_norm_g, hg_norm_g, w_out, gate, pre):
    bsz, n_ctx, d = ctx.shape
    t = n_ctx + x.shape[1]
    tm = ROW_TILE
    seg = lambda b, i: (b, (i * tm >= n_ctx).astype(jnp.int32), 0, 0)
    row = lambda width, idx: pl.BlockSpec((None, tm, width), lambda b, i: (b, i, idx))
    vec = pl.BlockSpec((1, LANES), lambda b, i: (0, 0))
    p_in, p_args, p_out, p_shape, p_scratch = _ffn_pre_parts(bsz, t, d, *pre, seg)
    row_specs, ctx_tiles = _joint_rows_specs(ctx, x)
    outs = pl.pallas_call(
        functools.partial(_mix_out_ab_kernel, ctx_tiles=ctx_tiles),
        grid=(bsz, t // tm),
        in_specs=row_specs + [row(GLA_VAL_W, 0), row(GLA_VAL_W, 0), row(HG_VAL_W, 0), row(HG_VAL_W, 0),
                  row(GLA_VAL_W, AB_G // GLA_VAL_W), row(HG_VAL_W, AB_HG // HG_VAL_W), vec, vec,
                  pl.BlockSpec(w_out.shape, lambda b, i: (0, 0)),
                  pl.BlockSpec((None, None, 1, d), seg)] + p_in,
        out_specs=[row(d, 0)] + p_out,
        out_shape=[jax.ShapeDtypeStruct((bsz, t, d), F32)] + p_shape,
        scratch_shapes=p_scratch,
        compiler_params=_params("arbitrary", "arbitrary"),
        name="mix_out_ab",
    )(ctx, x, o_gla[0], o_gla[1], o_hg[0], o_hg[1], proj, proj, gla_norm_g.reshape(1, -1),
      hg_norm_g.reshape(1, -1), w_out, gate, *p_args)
    return outs[0], outs[1:]


HALO = 8


def _norm_proj_conv_kernel(h_ref, hp_ref, hn_ref, g_ref, sh_ref, sc_ref, cw_ref, cb_ref, *refs,
                           ctx_tiles, n_tiles):
    w_refs, (hy_ref, xbc_ref, zdt_ref) = refs[:-3], refs[-3:]
    i = pl.program_id(1)
    tm = h_ref.shape[0]
    hh = jnp.concatenate([hp_ref[...], h_ref[...], hn_ref[...]], axis=0)
    u = _rms(hh, g_ref[...]) * (1.0 + sc_ref[...]) + sh_ref[...]
    p = _project(u.astype(BF16), w_refs)
    n_conv = cw_ref.shape[1]
    n_hy = hy_ref.shape[1]
    pc = p[:, 0:n_conv]
    rows_all = tm + 2 * HALO
    cur = pc[HALO:HALO + tm]
    prev = pltpu.roll(pc, 1, axis=0)[HALO:HALO + tm]
    nxt = pltpu.roll(pc, rows_all - 1, axis=0)[HALO:HALO + tm]
    first = jnp.logical_or(i == 0, i == ctx_tiles)
    last = jnp.logical_or(i == ctx_tiles - 1, i == n_tiles - 1)
    rows = lax.broadcasted_iota(jnp.int32, cur.shape, 0)
    prev = jnp.where(jnp.logical_and(first, rows == 0), 0.0, prev)
    nxt = jnp.where(jnp.logical_and(last, rows == tm - 1), 0.0, nxt)
    y = prev * cw_ref[0:1, :] + cur * cw_ref[1:2, :] + nxt * cw_ref[2:3, :] + cb_ref[...]
    hy_ref[...] = y[:, 0:n_hy]
    xbc_ref[...] = _silu(y[:, n_hy:n_conv])
    zdt_ref[...] = p[HALO:HALO + tm, n_conv:]


def _norm_proj_conv(h, g, shift, scale, ws, conv_w, conv_b, n_hy, n_ctx):
    bsz, t, d = h.shape
    n = sum(w.shape[1] for w in ws)
    n_conv = conv_w.shape[1]
    tm = ROW_TILE
    n_tiles, ctx_tiles = t // tm, n_ctx // tm
    r8 = tm // HALO
    last8 = t // HALO - 1
    seg = lambda b, i: (b, (i >= ctx_tiles).astype(jnp.int32), 0, 0)
    kern = functools.partial(_norm_proj_conv_kernel, ctx_tiles=ctx_tiles, n_tiles=n_tiles)
    return pl.pallas_call(
        kern,
        grid=(bsz, n_tiles),
        in_specs=[pl.BlockSpec((None, tm, d), lambda b, i: (b, i, 0)),
                  pl.BlockSpec((None, HALO, d), lambda b, i: (b, jnp.maximum(i * r8 - 1, 0), 0)),
                  pl.BlockSpec((None, HALO, d), lambda b, i: (b, jnp.minimum((i + 1) * r8, last8), 0)),
                  pl.BlockSpec((1, d), lambda b, i: (0, 0)),
                  pl.BlockSpec((None, None, 1, d), seg),
                  pl.BlockSpec((None, None, 1, d), seg),
                  pl.BlockSpec((3, n_conv), lambda b, i: (0, 0)),
                  pl.BlockSpec((1, n_conv), lambda b, i: (0, 0))]
                 + [pl.BlockSpec(w.shape, lambda b, i: (0, 0)) for w in ws],
        out_specs=[pl.BlockSpec((None, tm, n_hy), lambda b, i: (b, jnp.maximum(i - ctx_tiles, 0), 0)),
                   pl.BlockSpec((None, tm, n_conv - n_hy), lambda b, i: (b, i, 0)),
                   pl.BlockSpec((None, tm, n - n_conv), lambda b, i: (b, i, 0))],
        out_shape=[jax.ShapeDtypeStruct((bsz, t - n_ctx, n_hy), F32),
                   jax.ShapeDtypeStruct((bsz, t, n_conv - n_hy), F32),
                   jax.ShapeDtypeStruct((bsz, t, n - n_conv), F32)],
        compiler_params=_params("parallel", "arbitrary"),
        name="norm_proj_conv",
    )(h, h, h, g.reshape(1, d), shift, scale, conv_w, conv_b, *ws)


def _ssd_kernel(*refs):
    ins, hexp_ref, o_refs, st_refs = (refs[0:7], refs[7:14]), refs[14], refs[15:17], refs[17:19]

    @pl.when(pl.program_id(1) == 0)
    def _():
        for st_ref in st_refs:
            st_ref[...] = jnp.zeros_like(st_ref)

    for d, ((xbc_ref, dt_ref, bias_ref, alog_ref, mq_ref, mk_ref, mask_ref), o_ref, st_ref) in enumerate(
            zip(ins, o_refs, st_refs)):
        for rows in _sub_chunks(xbc_ref.shape[0], d == 1, SSD_CHUNK):
            _ssd_chunk(rows, xbc_ref, dt_ref, bias_ref, alog_ref, hexp_ref, mq_ref, mk_ref, mask_ref, o_ref, st_ref)


def _ssd_chunk(rows, xbc_ref, dt_ref, bias_ref, alog_ref, hexp_ref, mq_ref, mk_ref, mask_ref, o_ref, st_ref):
    c = SSD_CHUNK
    hpg = MB_HEADS // MB_GROUPS
    gw = hpg * MB_HEAD_DIM
    dt = _softplus(dt_ref[rows, :] + bias_ref[...])
    la = -dt * jnp.exp(alog_ref[...])
    la3 = jnp.concatenate(_split3(la), axis=0)
    cq = _dot(mq_ref[...], la3)
    ck = _dot(mk_ref[...], la3)
    cq_t = lax.dot_general(la3, mq_ref[...], (((0,), (1,)), ((), ())), preferred_element_type=F32)
    tot = jnp.broadcast_to(jnp.sum(la, axis=0, keepdims=True), (8, la.shape[1]))
    per_head = jnp.concatenate([dt, cq, ck, tot], axis=0)
    per_lane = _dot(jnp.concatenate(_split3(per_head), axis=1), hexp_ref[...])
    dt_x = per_lane[0:c]
    eq_x = jnp.exp(per_lane[c:2 * c])
    ek_x = jnp.exp(per_lane[2 * c:3 * c])
    etot_x = jnp.exp(per_lane[3 * c:3 * c + 1])
    xs = xbc_ref[rows, 0:MB_INNER] * dt_x
    mask = mask_ref[...]
    outs = []
    for g in range(MB_GROUPS):
        bm = xbc_ref[rows, MB_INNER + g * MB_STATE:MB_INNER + (g + 1) * MB_STATE].astype(BF16)
        cm = xbc_ref[rows, MB_INNER + MB_BC_W + g * MB_STATE:MB_INNER + MB_BC_W + (g + 1) * MB_STATE].astype(BF16)
        cb = _dot_nt(cm, bm)
        st = st_ref[g]
        gs = slice(g * gw, (g + 1) * gw)
        y_inter = _dot(cm, st.astype(BF16)) * eq_x[:, gs]
        for r in range(hpg):
            hd = g * hpg + r
            diff = cq[:, hd:hd + 1] - cq_t[hd:hd + 1, :]
            w = cb * jnp.exp(jnp.where(mask > 0.0, diff, -jnp.inf))
            ps = slice(hd * MB_HEAD_DIM, (hd + 1) * MB_HEAD_DIM)
            outs.append(_dot(w.astype(BF16), xs[:, ps].astype(BF16))
                        + y_inter[:, r * MB_HEAD_DIM:(r + 1) * MB_HEAD_DIM])
        st_ref[g] = st * etot_x[:, gs] + _dot_tn(bm, (xs[:, gs] * ek_x[:, gs]).astype(BF16))
    o_ref[rows, :] = jnp.concatenate(outs, axis=-1).astype(o_ref.dtype)


def _ssd_scan(xbc, proj, dt_col, dt_bias, a_log, n_ctx):
    bsz, t, _ = xbc.shape
    c = SSD_CHUNK
    pad = lambda v: jnp.zeros((1, LANES), F32).at[0, :MB_HEADS].set(v)
    hexp = np.zeros((LANES, MB_INNER), np.float32)
    for hd in range(MB_HEADS):
        hexp[hd, hd * MB_HEAD_DIM:(hd + 1) * MB_HEAD_DIM] = 1.0
    hexp = jnp.asarray(np.concatenate([hexp, hexp, hexp], axis=0), BF16)
    tri = np.tril(np.ones((c, c), np.float32))
    in_specs, args, outs = [], [], []
    for d in range(2):
        n_chunks, col, const, sel, _ = _scan_specs(SSD_BLOCK, n_ctx, t, d == 1, c)
        mq, mk = sel[0:c], sel[c:2 * c]
        mask = jnp.asarray(tri[::-1, ::-1].copy() if d == 1 else tri)
        bias, alog = pad(dt_bias[d]), pad(a_log[d].astype(F32))
        in_specs += [col(xbc.shape[-1], 0), col(LANES, dt_col + d), const(bias), const(alog),
                     const(mq), const(mk), const(mask)]
        args += [xbc, proj, bias, alog, mq, mk, mask]
        outs.append(col(MB_INNER, 0))
    return pl.pallas_call(
        _ssd_kernel,
        grid=(bsz, n_chunks),
        in_specs=in_specs + [const(hexp)],
        out_specs=outs,
        out_shape=[jax.ShapeDtypeStruct((bsz, t, MB_INNER), BF16)] * 2,
        scratch_shapes=[pltpu.VMEM((MB_GROUPS, MB_STATE, MB_INNER // MB_GROUPS), F32)] * 2,
        compiler_params=_params("parallel", "arbitrary"),
        name="ssd_scan",
    )(*args, hexp)


def _mix_out_cd_kernel(*refs):
    h_ref, hy_ref, yf_ref, yb_ref, xs_ref, z_ref, dsk_ref, ng_ref, w_ref, gt_ref = refs[0:10]
    pre_in, o_ref, pre_out = refs[10:10 + N_FFN_PRE_IN], refs[10 + N_FFN_PRE_IN], refs[11 + N_FFN_PRE_IN:]
    y = (yf_ref[...].astype(F32) + yb_ref[...].astype(F32) + dsk_ref[...] * xs_ref[...]) * _silu(z_ref[...])
    gw = MB_INNER // MB_GROUPS
    ys = [_rms(y[:, g * gw:(g + 1) * gw], ng_ref[:, g * gw:(g + 1) * gw]) for g in range(MB_GROUPS)]
    feat = jnp.concatenate([hy_ref[...]] + ys, axis=-1).astype(BF16)
    h_new = h_ref[...] + gt_ref[...] * _dot(feat, w_ref[...])
    o_ref[...] = h_new
    _ffn_pre_body(h_new, *pre_in, *pre_out)


def _mix_out_cd(h, hy, y_ssd, xbc, proj, z_col, d_skip_x, norm_g, w_out, gate, n_ctx, pre):
    bsz, t, d = h.shape
    tm = ROW_TILE
    n_lat = t - n_ctx
    off = n_ctx // tm
    row = lambda width, idx: pl.BlockSpec((None, tm, width), lambda b, i: (b, i + off, idx))
    vec = pl.BlockSpec((1, MB_INNER), lambda b, i: (0, 0))
    latent = lambda b, i: (b, 1, 0, 0)
    p_in, p_args, p_out, p_shape, p_scratch = _ffn_pre_parts(bsz, n_lat, d, *pre, latent)
    outs = pl.pallas_call(
        _mix_out_cd_kernel,
        grid=(bsz, n_lat // tm),
        in_specs=[row(d, 0), pl.BlockSpec((None, tm, HY_CH), lambda b, i: (b, i, 0)),
                  row(MB_INNER, 0), row(MB_INNER, 0), row(MB_INNER, 0), row(MB_INNER, z_col), vec, vec,
                  pl.BlockSpec(w_out.shape, lambda b, i: (0, 0)),
                  pl.BlockSpec((None, None, 1, d), latent)] + p_in,
        out_specs=[pl.BlockSpec((None, tm, d), lambda b, i: (b, i, 0))] + p_out,
        out_shape=[jax.ShapeDtypeStruct((bsz, n_lat, d), F32)] + p_shape,
        scratch_shapes=p_scratch,
        compiler_params=_params("arbitrary", "arbitrary"),
        name="mix_out_cd",
    )(h, hy, y_ssd[0], y_ssd[1], xbc, proj, d_skip_x, norm_g.reshape(1, -1), w_out, gate, *p_args)
    return outs[0], outs[1:]


def _top2_of4(a, b, c, d):
    hi1, lo1, hi2, lo2 = jnp.maximum(a, b), jnp.minimum(a, b), jnp.maximum(c, d), jnp.minimum(c, d)
    return jnp.maximum(hi1, hi2) + jnp.maximum(jnp.minimum(hi1, hi2), jnp.maximum(lo1, lo2))


def _first_argmax(vals, skip=None):
    idx = None
    for j, vj in enumerate(vals):
        if idx is None and skip is None:
            idx, best = jnp.zeros(vj.shape, jnp.int32), vj
            continue
        if idx is None:
            idx, best = jnp.full(vj.shape, -1, jnp.int32), jnp.full(vj.shape, -jnp.inf, F32)
        take = vj > best
        if skip is not None:
            take = jnp.logical_and(take, skip != j)
        idx = jnp.where(take, j, idx)
        best = jnp.where(take, vj, best)
    return idx, best


def _ffn_pre_body(h, g_ref, sh_ref, sc_ref, rw_ref, rb_ref, tri_ref, v_ref, ri_ref, rwt_ref, cnt_ref, carry_ref):
    @pl.when(jnp.logical_and(pl.program_id(0) == 0, pl.program_id(1) == 0))
    def _():
        carry_ref[...] = jnp.zeros_like(carry_ref)

    v = _rms(h, g_ref[...]) * (1.0 + sc_ref[...]) + sh_ref[...]
    v_ref[...] = _pack_bf16_pairs(v)
    st = _sigmoid(_dot_f32(v, rw_ref[...])).T[0:N_EXPERTS]
    sel = st + rb_ref[...]
    row = lambda a, e: a[e:e + 1]
    epg = EXPERTS_PER_GROUP
    gscore = [_top2_of4(*[row(sel, g * epg + j) for j in range(epg)]) for g in range(N_GROUPS)]
    best, _ = _first_argmax(gscore)

    def in_best(a, j):
        out = row(a, j)
        for g in range(1, N_GROUPS):
            out = jnp.where(best == g, row(a, g * epg + j), out)
        return out

    vals = [in_best(sel, j) for j in range(epg)]
    raw = [in_best(st, j) for j in range(epg)]
    i1, _ = _first_argmax(vals)
    i2, _ = _first_argmax(vals, skip=i1)
    pick = lambda i: functools.reduce(lambda acc, j: jnp.where(i == j, raw[j], acc), range(1, epg), raw[0])
    w1, w2 = pick(i1), pick(i2)
    wsum = w1 + w2
    e1, e2 = best * epg + i1, best * epg + i2

    experts = lax.broadcasted_iota(jnp.int32, st.shape, 0)
    oh1 = (experts == e1).astype(F32)
    oh2 = (experts == e2).astype(F32)
    cnt = oh1 + oh2
    before = _dot(cnt.astype(BF16), tri_ref[...]) + carry_ref[:, 0:1]
    ri_ref[0:1, :] = e1
    ri_ref[1:2, :] = e2
    ri_ref[2:3, :] = jnp.sum(oh1 * before, axis=0, keepdims=True).astype(jnp.int32)
    ri_ref[3:4, :] = jnp.sum(oh2 * before, axis=0, keepdims=True).astype(jnp.int32)
    ri_ref[4:8, :] = jnp.zeros((4, st.shape[1]), jnp.int32)
    lane_row = lax.broadcasted_iota(jnp.int32, (LANES, st.shape[1]), 0)
    rwt_ref[...] = jnp.where(lane_row == 0, w1 / wsum, jnp.where(lane_row == 1, w2 / wsum, 0.0)).T
    carry_ref[...] = carry_ref[...] + jnp.sum(cnt, axis=1, keepdims=True)
    cnt_ref[...] = carry_ref[...]


N_FFN_PRE_IN = 6


def _ffn_pre_parts(bsz, t, d, g, shift, scale, router_w_pad, router_b, seg):
    tm = ROW_TILE
    tri = jnp.asarray(np.triu(np.ones((tm, tm), np.float32), 1), BF16)
    in_specs = [pl.BlockSpec((1, d), lambda b, i: (0, 0)),
                pl.BlockSpec((None, None, 1, d), seg),
                pl.BlockSpec((None, None, 1, d), seg),
                pl.BlockSpec((d, LANES), lambda b, i: (0, 0)),
                pl.BlockSpec((N_EXPERTS, 1), lambda b, i: (0, 0)),
                pl.BlockSpec((tm, tm), lambda b, i: (0, 0))]
    args = (g.reshape(1, d), shift, scale, router_w_pad, router_b.reshape(N_EXPERTS, 1), tri)
    out_specs = [pl.BlockSpec((None, tm, d // 2), lambda b, i: (b, i, 0)),
                 pl.BlockSpec((None, 8, tm), lambda b, i: (b, 0, i)),
                 pl.BlockSpec((None, tm, LANES), lambda b, i: (b, i, 0)),
                 pl.BlockSpec((N_EXPERTS, LANES), lambda b, i: (0, 0))]
    out_shape = [jax.ShapeDtypeStruct((bsz, t, d // 2), jnp.uint32),
                 jax.ShapeDtypeStruct((bsz, 8, t), jnp.int32),
                 jax.ShapeDtypeStruct((bsz, t, LANES), F32),
                 jax.ShapeDtypeStruct((N_EXPERTS, LANES), F32)]
    return in_specs, args, out_specs, out_shape, [pltpu.VMEM((N_EXPERTS, LANES), F32)]


def _experts_kernel(be_ref, nb_ref, x_ref, wg_ref, wu_ref, wd_ref, o_ref, wg_s, wu_s, wd_s):
    i = pl.program_id(0)
    prev = be_ref[jnp.maximum(i - 1, 0)]
    changed = jnp.logical_or(i == 0, be_ref[i] != prev)

    @pl.when(changed)
    def _():
        wg_s[...] = wg_ref[...].astype(BF16)
        wu_s[...] = wu_ref[...].astype(BF16)
        wd_s[...] = wd_ref[...].astype(BF16)

    @pl.when(i < nb_ref[0])
    def _():
        x = _unpack_bf16_pairs(x_ref[...]).astype(BF16)
        hid = _silu(_dot(x, wg_s[...])) * _dot(x, wu_s[...])
        o_ref[...] = _pack_bf16_pairs(_dot(hid.astype(BF16), wd_s[...]))

    @pl.when(i >= nb_ref[0])
    def _():
        o_ref[...] = jnp.zeros_like(o_ref)


def _experts(xb, block_e, n_used, layer, w_gate, w_up, w_down):
    n_slots = xb.shape[0]
    n_blocks = n_slots // MOE_BLOCK
    d, de = w_gate.shape[-2:]
    wspec = lambda shape: pl.BlockSpec((None, None) + shape, lambda i, be, nb: (layer, be[i], 0, 0))
    return pl.pallas_call(
        _experts_kernel,
        grid_spec=pltpu.PrefetchScalarGridSpec(
            num_scalar_prefetch=2,
            grid=(n_blocks,),
            in_specs=[pl.BlockSpec((MOE_BLOCK, d // 2), lambda i, be, nb: (i, 0)),
                      wspec((d, de)), wspec((d, de)), wspec((de, d))],
            out_specs=pl.BlockSpec((MOE_BLOCK, d // 2), lambda i, be, nb: (i, 0)),
            scratch_shapes=[pltpu.VMEM((d, de), BF16), pltpu.VMEM((d, de), BF16), pltpu.VMEM((de, d), BF16)]),
        out_shape=jax.ShapeDtypeStruct((n_slots, d // 2), jnp.uint32),
        compiler_params=_params("arbitrary"),
        name="moe_experts",
    )(block_e, n_used, xb, w_gate, w_up, w_down)


def _ffn_post_kernel(h_ref, y0_ref, y1_ref, w_ref, gt_ref, g_ref, o_ref, *, final):
    w = w_ref[...]
    y = w[:, 0:1] * _unpack_bf16_pairs(y0_ref[...]) + w[:, 1:2] * _unpack_bf16_pairs(y1_ref[...])
    out = h_ref[...] + gt_ref[...] * y
    o_ref[...] = _rms(out, g_ref[...]) if final else out


def _ffn_post(h, y, w, gate, n_ctx, final_g=None):
    bsz, t, d = h.shape
    tm = 2 * ROW_TILE if t % (2 * ROW_TILE) == 0 and n_ctx % (2 * ROW_TILE) == 0 else ROW_TILE
    seg = lambda b, i: (b, (i * tm >= n_ctx).astype(jnp.int32), 0, 0)
    row = lambda width: pl.BlockSpec((None, tm, width), lambda b, i: (b, i, 0))
    choice = lambda kk: pl.BlockSpec((None, None, tm, d // 2), lambda b, i: (kk, b, i, 0))
    final = final_g is not None
    g = final_g if final else jnp.ones((d,), F32)
    return pl.pallas_call(
        functools.partial(_ffn_post_kernel, final=final),
        grid=(bsz, t // tm),
        in_specs=[row(d), choice(0), choice(1), row(LANES), pl.BlockSpec((None, None, 1, d), seg),
                  pl.BlockSpec((1, d), lambda b, i: (0, 0))],
        out_specs=row(d),
        out_shape=jax.ShapeDtypeStruct((bsz, t, d), F32),
        compiler_params=_params("parallel", "parallel"),
        name="ffn_post",
    )(h, y, y, w, gate, g.reshape(1, d))


def _slot_layout(n, ri, counts):
    e = jnp.swapaxes(ri[:, 0:2], 0, 1).reshape(TOP_K, n)
    rank = jnp.swapaxes(ri[:, 2:4], 0, 1).reshape(TOP_K, n)
    padded = (counts + MOE_BLOCK - 1) // MOE_BLOCK * MOE_BLOCK
    pend = jnp.cumsum(padded)
    pstart = pend - padded
    experts = jnp.arange(N_EXPERTS, dtype=jnp.int32)
    dest = rank + jnp.sum(jnp.where(e[..., None] == experts, pstart, 0), axis=-1)
    n_slots = (n * TOP_K + MOE_BLOCK - 1) // MOE_BLOCK * MOE_BLOCK + N_EXPERTS * MOE_BLOCK
    n_blocks = n_slots // MOE_BLOCK
    blk0 = jnp.arange(n_blocks, dtype=jnp.int32)[:, None] * MOE_BLOCK
    block_e = jnp.minimum(jnp.sum((pend[None, :] <= blk0).astype(jnp.int32), axis=-1), N_EXPERTS - 1)
    n_used = (pend[-1] // MOE_BLOCK).astype(jnp.int32).reshape(1)
    return dest, n_slots, block_e.astype(jnp.int32), n_used


SC_CORES, SC_SUBCORES = 2, 16
SC_GATHER_WINDOW = 64
SC_SCATTER_WINDOW = 32


def _sc_window(per_worker, largest):
    win = largest
    while per_worker % win:
        win //= 2
    assert win >= 8
    return win


def _gather_rows(table, idx):
    n_rows, d = idx.shape[0], table.shape[1]
    workers = SC_CORES * SC_SUBCORES
    per_worker = n_rows // workers
    assert per_worker * workers == n_rows
    win = _sc_window(per_worker, SC_GATHER_WINDOW)
    mesh = plsc.VectorSubcoreMesh(core_axis_name="c", subcore_axis_name="s")

    @functools.partial(
        pl.kernel, mesh=mesh,
        out_type=jax.ShapeDtypeStruct((n_rows, d), table.dtype),
        scratch_types=[pltpu.VMEM((win,), jnp.int32), pltpu.VMEM((win,), jnp.int32),
                       pltpu.VMEM((win, d), table.dtype), pltpu.VMEM((win, d), table.dtype),
                       pltpu.SemaphoreType.DMA, pltpu.SemaphoreType.DMA],
    )
    def gather_kernel(table_hbm, idx_hbm, out_hbm, idx0, idx1, rows0, rows1, sem0, sem1):
        base = (lax.axis_index("s") * SC_CORES + lax.axis_index("c")) * per_worker
        n_win = per_worker // win
        slots = ((idx0, rows0, sem0), (idx1, rows1, sem1))
        window = lambda j: pl.ds(pl.multiple_of(base + j * win, 8), win)

        def start(j, slot):
            idx_v, rows_v, sem = slots[slot]
            pltpu.sync_copy(idx_hbm.at[window(j)], idx_v)
            pltpu.async_copy(table_hbm.at[idx_v], rows_v, sem)

        def finish(j, slot):
            idx_v, rows_v, sem = slots[slot]
            pltpu.make_async_copy(table_hbm.at[idx_v], rows_v, sem).wait()
            pltpu.sync_copy(rows_v, out_hbm.at[window(j)])

        start(0, 0)

        @pl.loop(0, n_win, step=2)
        def _(j):
            @pl.when(j + 1 < n_win)
            def _():
                start(j + 1, 1)

            finish(j, 0)

            @pl.when(j + 2 < n_win)
            def _():
                start(j + 2, 0)

            @pl.when(j + 1 < n_win)
            def _():
                finish(j + 1, 1)

    return gather_kernel(table, idx)


def _scatter_rows(src, dest, n_slots):
    n, d = src.shape
    workers = SC_CORES * SC_SUBCORES
    per_worker = n // workers
    assert per_worker * workers == n and dest.shape == (TOP_K, n)
    win = _sc_window(per_worker, SC_SCATTER_WINDOW)
    mesh = plsc.VectorSubcoreMesh(core_axis_name="c", subcore_axis_name="s")

    @functools.partial(
        pl.kernel, mesh=mesh,
        out_type=jax.ShapeDtypeStruct((n_slots, d), src.dtype),
        scratch_types=[pltpu.VMEM((win,), jnp.int32)] * 4 + [pltpu.VMEM((win, d), src.dtype)] * 2
                      + [pltpu.SemaphoreType.DMA] * 6,
    )
    def scatter_kernel(src_hbm, dest_hbm, out_hbm, i0a, i1a, i0b, i1b, rows_a, rows_b,
                       load_a, first_a, second_a, load_b, first_b, second_b):
        base = (lax.axis_index("s") * SC_CORES + lax.axis_index("c")) * per_worker
        n_win = per_worker // win
        slots = ((i0a, i1a, rows_a, load_a, first_a, second_a), (i0b, i1b, rows_b, load_b, first_b, second_b))
        window = lambda j: pl.ds(pl.multiple_of(base + j * win, 8), win)

        def load(j, slot):
            idx0, idx1, rows_v, sem, _, _ = slots[slot]
            pltpu.async_copy(src_hbm.at[window(j)], rows_v, sem)
            pltpu.sync_copy(dest_hbm.at[0, window(j)], idx0)
            pltpu.sync_copy(dest_hbm.at[1, window(j)], idx1)

        def scatter(j, slot):
            idx0, idx1, rows_v, sem, sem0, sem1 = slots[slot]
            pltpu.make_async_copy(src_hbm.at[window(j)], rows_v, sem).wait()
            pltpu.async_copy(rows_v, out_hbm.at[idx0], sem0)
            pltpu.async_copy(rows_v, out_hbm.at[idx1], sem1)

        def drain(slot):
            idx0, idx1, rows_v, _, sem0, sem1 = slots[slot]
            pltpu.make_async_copy(rows_v, out_hbm.at[idx0], sem0).wait()
            pltpu.make_async_copy(rows_v, out_hbm.at[idx1], sem1).wait()

        load(0, 0)

        @pl.loop(0, n_win, step=2)
        def _(j):
            scatter(j, 0)

            @pl.when(j + 1 < n_win)
            def _():
                load(j + 1, 1)

            drain(0)

            @pl.when(j + 1 < n_win)
            def _():
                scatter(j + 1, 1)

            @pl.when(j + 2 < n_win)
            def _():
                load(j + 2, 0)

            @pl.when(j + 1 < n_win)
            def _():
                drain(1)

    return scatter_kernel(src, dest)


def _alongside(gather, idx, side_fn, side_in):
    idx, side_in = lax.optimization_barrier((idx, side_in))
    return lax.optimization_barrier((gather(idx), side_fn(side_in)))


def _moe(h, pre_out, gate, layer, w_gate, w_up, w_down, n_ctx, final_g=None, side=None):
    bsz, t, d = h.shape
    n = bsz * t
    v, ri, rwt, counts = pre_out
    dest, n_slots, block_e, n_used = _slot_layout(n, ri, counts[:, 0].astype(jnp.int32))
    dispatch = lambda idx: _scatter_rows(v.reshape(n, d // 2), idx, n_slots)
    if side is None:
        xb = dispatch(dest)
    else:
        xb, side_a = _alongside(dispatch, dest, *side[0])
    yb = _experts(xb, block_e, n_used, layer, w_gate, w_up, w_down)
    combine = lambda idx: _gather_rows(yb, idx)
    dest_flat = dest.reshape(-1)
    if side is None:
        y, side_b = combine(dest_flat), None
    else:
        y, side_b = _alongside(combine, dest_flat, side[1], side_a)
    out = _ffn_post(h, y.reshape(TOP_K, bsz, t, d // 2), rwt, gate, n_ctx, final_g)
    return out if side is None else (out, side_b)


DFT_STEP = 16


def _dft_tables(n):
    r, *mats = _dft_tables_np(n)
    return (r,) + tuple(jnp.asarray(a).astype(BF16) for a in mats)


@functools.lru_cache(maxsize=None)
def _dft_tables_np(n):
    size = 2 * n
    r = int(round(math.sqrt(size)))
    assert r * r == size and r % DFT_STEP == 0
    p1 = np.arange(r // 2)[None, None, :]
    p2 = np.arange(r)[:, None, None]
    k1 = np.arange(r)[None, :, None]
    ang = 2.0 * np.pi * (((r * p1 + p2) * k1) % size) / size
    g_re, g_im = np.cos(ang), -np.sin(ang)
    g_in = np.concatenate([g_re, g_im], axis=1)
    g_out = np.concatenate([np.swapaxes(g_re, 1, 2), np.swapaxes(g_im, 1, 2)], axis=2) / size
    a2 = 2.0 * np.pi * ((np.arange(r)[:, None] * np.arange(r)[None, :]) % r) / r
    f_re, f_im = np.cos(a2), -np.sin(a2)
    f_fwd = np.block([[f_re, -f_im], [f_im, f_re]])
    f_inv = np.block([[f_re, f_im], [-f_im, f_re]])
    p1f = np.arange(r)[None, None, :]
    angf = 2.0 * np.pi * (((r * p1f + p2) * k1) % size) / size
    g_full = np.concatenate([np.cos(angf), -np.sin(angf)], axis=1)
    return (r,) + tuple(a.astype(np.float32) for a in (g_in, g_out, f_fwd, f_inv, g_full))


def _dot_f32_tn(a, b):
    ah = a.astype(BF16)
    al = (a - ah.astype(F32)).astype(BF16)
    bh = b.astype(BF16)
    bl = (b - bh.astype(F32)).astype(BF16)
    return _dot_tn(ah, bh) + (_dot_tn(ah, bl) + _dot_tn(al, bh))


def _hy_filter_kernel(z_ref, t_ref, w1_ref, b1_ref, w2_ref, b2_ref, fr_ref, w3_ref, w3b_ref, rates_ref, o_ref,
                      *, half_tiles):
    i = pl.program_id(0)
    hid = jnp.sin(fr_ref[...] * (_dot_f32(w1_ref[...], z_ref[...]) + b1_ref[...]))
    hid = jnp.sin(fr_ref[...] * (_dot_f32(w2_ref[...], hid) + b2_ref[...]))
    filt = _dot_f32_tn(hid, w3_ref[...])
    decay = jnp.exp(-t_ref[...] * rates_ref[...])
    for o in range(o_ref.shape[0]):
        o_ref[o] = filt[:, o * HY_CH:(o + 1) * HY_CH] * decay

    @pl.when(i == 0)
    def _():
        extra = _dot_f32_tn(hid[:, 0:LANES], w3b_ref[...])[0:8]
        first = lax.broadcasted_iota(jnp.int32, (8, HY_CH), 0) == 0
        for o in range(o_ref.shape[0]):
            add = extra[:, o * HY_CH:(o + 1) * HY_CH] * decay[0:8]
            o_ref[o, 0:8, :] = o_ref[o, 0:8, :] + jnp.where(first, add, 0.0)

    @pl.when(i == half_tiles)
    def _():
        for o in range(o_ref.shape[0]):
            o_ref[o, 0:1, :] = jnp.zeros((1, HY_CH), F32)


HY_TILE = 1024


def _hy_kernels(n, w1, b1, w2, b2, w3, freq):
    pos = np.arange(2 * n)
    pos = np.where(pos < n, pos, 2 * n - pos).astype(np.float32)
    t = jnp.asarray(pos / np.float32(n - 1))
    bands = jnp.linspace(1e-4, HY_BANDS - 1, HY_BANDS, dtype=F32)
    ang = (2.0 * math.pi / n) * bands[:, None] * jnp.asarray(pos)[None, :]
    z = jnp.concatenate([t[None, :], jnp.cos(ang), -jnp.sin(ang)], axis=0)
    z = jnp.pad(z, ((0, LANES - z.shape[0]), (0, 0)))
    w1t = jnp.pad(w1, ((0, LANES - w1.shape[0]), (0, 0))).T
    hidden = w1.shape[1]
    col = lambda v: v.reshape(hidden, 1)
    w3d = jnp.swapaxes(w3.reshape(hidden, HY_ORDER, 2, HY_CH), 0, 2)
    w3d = jnp.swapaxes(w3d, 1, 2).reshape(2, hidden, HY_ORDER * HY_CH)
    rates = jnp.abs(jnp.linspace(HY_MIN_DECAY, HY_MAX_DECAY, HY_CH, dtype=F32)).reshape(1, HY_CH)
    tm = HY_TILE
    half_tiles = n // tm
    full = lambda a: pl.BlockSpec(a.shape, lambda i: (0,) * a.ndim)
    small = (w1t, col(b1), w2.T, col(b2), col(freq))
    return pl.pallas_call(
        functools.partial(_hy_filter_kernel, half_tiles=half_tiles),
        grid=(2 * n // tm,),
        in_specs=[pl.BlockSpec((LANES, tm), lambda i: (0, i)), pl.BlockSpec((tm, 1), lambda i: (i, 0))]
                 + [full(a) for a in small]
                 + [pl.BlockSpec((None, hidden, HY_ORDER * HY_CH), lambda i: ((i >= half_tiles).astype(jnp.int32), 0, 0)),
                    pl.BlockSpec((None, hidden, HY_ORDER * HY_CH), lambda i: (1, 0, 0)), full(rates)],
        out_specs=pl.BlockSpec((HY_ORDER, tm, HY_CH), lambda i: (0, i, 0)),
        out_shape=jax.ShapeDtypeStruct((HY_ORDER, 2 * n, HY_CH), F32),
        compiler_params=_params("parallel"),
        name="hy_kernels",
    )(z, t[:, None], *small, w3d, w3d, rates)


def _pack_complex(z):
    r = z.shape[0] // 2
    bits = lax.bitcast_convert_type(z.astype(BF16).astype(F32), jnp.uint32)
    return lax.bitcast_convert_type(bits[0:r] | (bits[r:2 * r] >> 16), F32)


def _unpack_complex(words):
    p = lax.bitcast_convert_type(words, jnp.uint32)
    re = lax.bitcast_convert_type(p & jnp.uint32(0xFFFF0000), F32)
    im = lax.bitcast_convert_type(p << 16, F32)
    return jnp.concatenate([re, im], axis=0).astype(BF16)


def _load_every(ref, j, count):
    return ref.reshape(count * DFT_STEP, LANES)[pl.ds(j, count, stride=DFT_STEP), :]


def _store_every(ref, j, count, val):
    ref.reshape(count * DFT_STEP, LANES)[pl.ds(j, count, stride=DFT_STEP), :] = val


def _dft_in_kernel(x_ref, g_ref, a_ref):
    n_seq, rh = x_ref.shape[0:2]
    for j in range(DFT_STEP):
        x = jnp.concatenate([_load_every(x_ref.at[b], j, rh) for b in range(n_seq)], axis=1).astype(BF16)
        res = _dot(g_ref[j], x)
        for b in range(n_seq):
            a_ref[b, j] = _pack_complex(res[:, b * LANES:(b + 1) * LANES])


def _dft_in(x4, col, g_in):
    bx, rh, r, _ = x4.shape
    c = HY_CH
    cbs = c // LANES
    return pl.pallas_call(
        _dft_in_kernel,
        grid=(r // DFT_STEP, cbs),
        in_specs=[pl.BlockSpec((bx, rh, DFT_STEP, LANES), lambda i, cb: (0, 0, i, col * cbs + cb)),
                  pl.BlockSpec((DFT_STEP, 2 * r, rh), lambda i, cb: (i, 0, 0))],
        out_specs=pl.BlockSpec((bx, DFT_STEP, r, LANES), lambda i, cb: (0, i, 0, cb)),
        out_shape=jax.ShapeDtypeStruct((bx, r, r, c), F32),
        compiler_params=_params("parallel", "parallel"),
        name="dft_in",
    )(x4, g_in)


def _stage2_operand(a_ref, j, r):
    return jnp.concatenate([_unpack_complex(_load_every(a_ref.at[b], j, r)) for b in range(a_ref.shape[0])],
                           axis=1)


def _dft_filt_kernel(a_ref, f_ref, k_ref):
    r = f_ref.shape[0] // 2
    for j in range(DFT_STEP):
        s = _dot(f_ref[...], _stage2_operand(a_ref, j, r))
        for o in range(a_ref.shape[0]):
            k_ref[o, j] = s[:, o * LANES:(o + 1) * LANES]


def _dft_filt(a, f_fwd):
    nq, r, _, c = a.shape
    return pl.pallas_call(
        _dft_filt_kernel,
        grid=(r // DFT_STEP, c // LANES),
        in_specs=[pl.BlockSpec((nq, r, DFT_STEP, LANES), lambda i, cb: (0, 0, i, cb)),
                  pl.BlockSpec(f_fwd.shape, lambda i, cb: (0, 0))],
        out_specs=pl.BlockSpec((nq, DFT_STEP, 2 * r, LANES), lambda i, cb: (0, i, 0, cb)),
        out_shape=jax.ShapeDtypeStruct((nq, r, 2 * r, c), F32),
        compiler_params=_params("parallel", "parallel"),
        name="dft_filt",
    )(a, f_fwd)


def _dft_mid_kernel(a_ref, k_ref, ff_ref, fi_ref, b_ref):
    r = ff_ref.shape[0] // 2
    n_seq = a_ref.shape[0]
    for j in range(DFT_STEP):
        s = _dot(ff_ref[...], _stage2_operand(a_ref, j, r))
        sr, si = s[0:r], s[r:2 * r]
        kr = jnp.concatenate([k_ref[j, 0:r, :]] * n_seq, axis=1)
        ki = jnp.concatenate([k_ref[j, r:2 * r, :]] * n_seq, axis=1)
        p = jnp.concatenate([sr * kr - si * ki, sr * ki + si * kr], axis=0).astype(BF16)
        back = _dot(fi_ref[...], p)
        for b in range(n_seq):
            b_ref[b, j] = _pack_complex(back[:, b * LANES:(b + 1) * LANES])


def _dft_mid(a, kspec, order, f_fwd, f_inv):
    bsz, r, _, c = a.shape
    return pl.pallas_call(
        _dft_mid_kernel,
        grid=(r // DFT_STEP, c // LANES),
        in_specs=[pl.BlockSpec((bsz, r, DFT_STEP, LANES), lambda i, cb: (0, 0, i, cb)),
                  pl.BlockSpec((None, DFT_STEP, 2 * r, LANES), lambda i, cb: (order, i, 0, cb)),
                  pl.BlockSpec(f_fwd.shape, lambda i, cb: (0, 0)),
                  pl.BlockSpec(f_inv.shape, lambda i, cb: (0, 0))],
        out_specs=pl.BlockSpec((bsz, DFT_STEP, r, LANES), lambda i, cb: (0, i, 0, cb)),
        out_shape=jax.ShapeDtypeStruct((bsz, r, r, c), F32),
        compiler_params=_params("parallel", "parallel"),
        name="dft_mid",
    )(a, kspec, f_fwd, f_inv)


def _dft_out_kernel(b_ref, g_ref, u_ref, x_ref, bias_ref, o_ref):
    n_seq, r = b_ref.shape[0:2]
    rh = o_ref.shape[1]
    for j in range(DFT_STEP):
        rhs = jnp.concatenate([_unpack_complex(_load_every(b_ref.at[b], j, r)) for b in range(n_seq)], axis=1)
        y = _dot(g_ref[j], rhs)
        for b in range(n_seq):
            yb = y[:, b * LANES:(b + 1) * LANES]
            _store_every(o_ref.at[b], j, rh,
                         _load_every(x_ref.at[b], j, rh) * (yb + _load_every(u_ref.at[b], j, rh) * bias_ref[...]))


def _dft_out(bm, g_out, u4, u_col, x4, x_col, bias):
    bsz, r, _, c = bm.shape
    rh = r // 2
    cbs = c // LANES
    seq = lambda col: pl.BlockSpec((bsz, rh, DFT_STEP, LANES), lambda i, cb: (0, 0, i, col * cbs + cb))
    return pl.pallas_call(
        _dft_out_kernel,
        grid=(r // DFT_STEP, cbs),
        in_specs=[pl.BlockSpec((bsz, r, DFT_STEP, LANES), lambda i, cb: (0, 0, i, cb)),
                  pl.BlockSpec((DFT_STEP, rh, 2 * r), lambda i, cb: (i, 0, 0)),
                  seq(u_col), seq(x_col), pl.BlockSpec((1, LANES), lambda i, cb: (0, cb))],
        out_specs=seq(0),
        out_shape=jax.ShapeDtypeStruct((bsz, rh, r, c), F32),
        compiler_params=_params("parallel", "parallel"),
        name="dft_out",
    )(bm, g_out, u4, x4, bias.reshape(1, c))


def _hyena_filter_stage1(n, filter_params):
    r, g_full = _dft_tables(n)[0], _dft_tables(n)[5]
    kern = _hy_kernels(n, *filter_params)
    return _dft_in(kern.reshape(-1, r, r, HY_CH), 0, g_full)


def _hyena_filter_spectra(n, stage1):
    return _dft_filt(stage1, _dft_tables(n)[3])


def _hyena(hy_in, kspec, conv_bias):
    bsz, n, _ = hy_in.shape
    r, g_in, g_out, f_fwd, f_inv, _ = _dft_tables(n)
    seq4 = hy_in.reshape(bsz, r // 2, r, 3 * HY_CH)
    zz = _dft_out(_dft_mid(_dft_in(seq4, 0, g_in), kspec, 0, f_fwd, f_inv), g_out,
                  seq4, 0, seq4, 1, conv_bias[0])
    out = _dft_out(_dft_mid(_dft_in(zz, 0, g_in), kspec, 1, f_fwd, f_inv), g_out,
                   zz, 0, seq4, 2, conv_bias[1])
    return out.reshape(bsz, n, HY_CH)


def _reorder_ab(w):
    offs = np.cumsum([0, GLA_KEY_W, GLA_KEY_W, GLA_VAL_W, GLA_VAL_W, GLA_LOW_RANK, GLA_LOW_RANK,
                      HG_KEY_W, HG_KEY_W, HG_KEY_W, HG_VAL_W, HG_VAL_W]).tolist()
    gq, gk, gv, gg, lr_f, lr_b, hq, hf_f, hf_b, hi, hg = range(11)
    cols = lambda first, last: w[:, offs[first]:offs[last + 1]].astype(BF16)
    lr = jnp.pad(cols(lr_f, lr_b), ((0, 0), (0, AB_PAD_COLS - AB_LR - 2 * GLA_LOW_RANK)))
    return [cols(gq, gk), cols(gg, gg), cols(hq, hf_b), cols(hg, hg), lr], [cols(gv, gv), cols(hi, hi)]


def _reorder_cd(w):
    hy_end = 3 * HY_CH
    z_end = hy_end + MB_INNER
    xbc_end = z_end + MB_INNER + 2 * MB_BC_W
    pad = lambda a: jnp.pad(a, ((0, 0), (0, LANES - MB_HEADS)))
    dt = jnp.concatenate([pad(w[:, xbc_end:xbc_end + MB_HEADS]), pad(w[:, xbc_end + MB_HEADS:])], axis=-1)
    return [w[:, :hy_end].astype(BF16), w[:, z_end:xbc_end].astype(BF16), w[:, hy_end:z_end].astype(BF16),
            dt.astype(BF16)]


def kernel(x, c, ctx, c_ctx, ada_w, ada_b, norm_mix_g, norm_ffn_g, norm_out_g, ab_w_in, ab_w_out, gla_gate_w, gla_gate_b, gla_norm_g, hg_lb, hg_norm_g, cd_w_in, cd_w_out, hy_short_w, hy_short_b, hy_w1, hy_b1, hy_w2, hy_b2, hy_w3, hy_freq, hy_bias, mb_conv_w, mb_conv_b, mb_dt_bias, mb_a_log, mb_d, mb_norm_g, router_w, router_b, moe_w_gate, moe_w_up, moe_w_down):
    bsz, n_lat, d = x.shape
    n_ctx = ctx.shape[1]
    assert ada_w.shape[0] == 2 and ab_w_in.shape[0] == 1 and cd_w_in.shape[0] == 1

    cond = jnp.zeros((ADALN_ROWS, d), F32).at[:bsz].set(c).at[bsz].set(c_ctx)
    m = _adaln(cond, ada_w, ada_b)

    def mods(layer):
        lat = m[layer, :bsz].reshape(bsz, 6, d)
        cx = jnp.broadcast_to(m[layer, bsz].reshape(1, 6, d), (bsz, 6, d))
        both = jnp.stack([cx, lat], axis=1)
        return [both[:, :, j][:, :, None, :] for j in range(6)]

    router_w_pad = jnp.zeros((d, LANES), F32).at[:, :N_EXPERTS].set(router_w)

    sh_m, sc_m, gt_m, sh_f, sc_f, gt_f = mods(0)
    proj, vals = _norm_proj(ctx, x, norm_mix_g[0], sh_m, sc_m, *_reorder_ab(ab_w_in[0]))
    gwp = [jnp.zeros((LANES, GLA_KEY_W), F32).at[GLA_LOW_RANK * dd:GLA_LOW_RANK * (dd + 1)].set(gla_gate_w[0, dd])
           for dd in range(2)]
    o_gla = _gla_scan(proj, vals, gwp, [gla_gate_b[0, dd].reshape(1, -1) for dd in range(2)], n_ctx)
    o_hg = _hgrn_scan(proj, vals, [hg_lb[dd].astype(F32) for dd in range(2)], 0, n_ctx)
    h, pre_out = _mix_out_ab(ctx, x, o_gla, o_hg, proj, gla_norm_g[0], hg_norm_g[0], ab_w_out[0].astype(BF16),
                             gt_m, (norm_ffn_g[0], sh_f, sc_f, router_w_pad, router_b))
    filter_params = (hy_w1[0], hy_b1[0], hy_w2[0], hy_b2[0], hy_w3[0], hy_freq[0])
    side = ((functools.partial(_hyena_filter_stage1, n_lat), filter_params),
            functools.partial(_hyena_filter_spectra, n_lat))
    h, kspec = _moe(h, pre_out, gt_f, 0, moe_w_gate, moe_w_up, moe_w_down, n_ctx, side=side)

    sh_m, sc_m, gt_m, sh_f, sc_f, gt_f = mods(1)
    conv_w = jnp.concatenate([hy_short_w[0], mb_conv_w[0]], axis=0).T
    conv_b = jnp.concatenate([hy_short_b[0], mb_conv_b[0]]).reshape(1, -1)
    hy_in, xbc, zdt = _norm_proj_conv(h, norm_mix_g[1], sh_m, sc_m, _reorder_cd(cd_w_in[0]),
                                      conv_w, conv_b, 3 * HY_CH, n_ctx)
    hy = _hyena(hy_in, kspec, hy_bias[0])
    y_ssd = _ssd_scan(xbc, zdt, MB_INNER // LANES, mb_dt_bias[0], mb_a_log[0], n_ctx)
    d_skip_x = jnp.repeat(mb_d[0], MB_HEAD_DIM).reshape(1, MB_INNER)
    h, pre_out = _mix_out_cd(h, hy, y_ssd, xbc, zdt, 0, d_skip_x, mb_norm_g[0], cd_w_out[0].astype(BF16), gt_m,
                             n_ctx, (norm_ffn_g[1], sh_f, sc_f, router_w_pad, router_b))
    return _moe(h, pre_out, gt_f, 1, moe_w_gate, moe_w_up, moe_w_down, 0, final_g=norm_out_g)
```
